```python
import math
import jax
import jax.numpy as jnp
from jax import lax
import numpy as np

D_MODEL = 1024
BATCH = 8
SEQ = 4096
DEPTH = 4

CHUNK = 64
N_META = 16
N_MIXERS = 4
EPS = 1e-6
NEG_INF = -1e30
ROPE_BASE = 10000.0
Q_BLOCK = 128

MLA_HEADS = 8
MLA_NOPE = 128
MLA_ROPE = 64
MLA_V = 128
MLA_QK = MLA_NOPE + MLA_ROPE
MLA_Q_LORA = 384
MLA_KV_LORA = 256

HGRN_EXPAND = 128
HGRN_HEADS = D_MODEL // HGRN_EXPAND
HGRN_DK = HGRN_EXPAND
HGRN_DV = D_MODEL // HGRN_HEADS
HGRN_CHUNK = 16

S5_GROUP = 16
S5_GROUPS = D_MODEL // S5_GROUP
S5_STATE = 64
S5_DT_MIN = 1e-3
S5_DT_MAX = 1e-1

RET_HEADS = 4
RET_DK = D_MODEL // RET_HEADS
RET_DV = 2 * RET_DK

FFN_HIDDEN = 2816
CONV_WIDTH = 3

N_MLA_L = (DEPTH + 3) // 4
N_HGRN_L = (DEPTH + 2) // 4
N_S5_L = (DEPTH + 1) // 4
N_RET_L = DEPTH // 4

kernel_name = "chunk_causal_hybrid_trunk"

F32 = jnp.float32


def rms_norm(x, g):
    xf = x.astype(F32)
    y = xf * lax.rsqrt(jnp.mean(xf * xf, axis=-1, keepdims=True) + EPS)
    return (y * g.astype(F32)).astype(x.dtype)


def rope_tables(n_pos, dim):
    inv_freq = 1.0 / (ROPE_BASE ** (jnp.arange(0, dim, 2, dtype=F32) / dim))
    ang = jnp.arange(n_pos, dtype=F32)[:, None] * inv_freq[None, :]
    return jnp.cos(ang), jnp.sin(ang)


def apply_rope(x, cos, sin):
    half = x.shape[-1] // 2
    x1, x2 = x[..., :half], x[..., half:]
    c = cos[None, :, None, :].astype(x.dtype)
    s = sin[None, :, None, :].astype(x.dtype)
    return jnp.concatenate([x1 * c - x2 * s, x1 * s + x2 * c], axis=-1)


def pad_front(t, n):
    return jnp.pad(t, [(0, 0), (n, 0)] + [(0, 0)] * (t.ndim - 2))


def chunk_ids(n_pos):
    real = 1 + jnp.arange(n_pos - N_META, dtype=jnp.int32) // CHUNK
    return jnp.concatenate([jnp.zeros((N_META,), jnp.int32), real])


def mla_mixer(a, w_down, g_cq, g_ckv, w_uq, w_ukv, g_qhead, g_khead, w_o, cos, sin, cid):
    B, L, _ = a.shape
    H = MLA_HEADS
    down = a @ w_down
    c_q = rms_norm(down[..., :MLA_Q_LORA], g_cq)
    c_kv = rms_norm(down[..., MLA_Q_LORA:MLA_Q_LORA + MLA_KV_LORA], g_ckv)
    k_pe = down[..., MLA_Q_LORA + MLA_KV_LORA:]
    q = (c_q @ w_uq).reshape(B, L, H, MLA_QK)
    kv = (c_kv @ w_ukv).reshape(B, L, H, MLA_NOPE + MLA_V)
    k = jnp.concatenate(
        [kv[..., :MLA_NOPE], jnp.broadcast_to(k_pe[:, :, None, :], (B, L, H, MLA_ROPE))], axis=-1)
    v = kv[..., MLA_NOPE:]
    q = rms_norm(q, g_qhead)
    k = rms_norm(k, g_khead)
    q = jnp.concatenate([q[..., :MLA_NOPE], apply_rope(q[..., MLA_NOPE:], cos, sin)], axis=-1)
    k = jnp.concatenate([k[..., :MLA_NOPE], apply_rope(k[..., MLA_NOPE:], cos, sin)], axis=-1)
    scale = MLA_QK ** -0.5

    def attend(qb, kb, vb, mask):
        s = jnp.einsum("bqhd,bkhd->bhqk", qb, kb).astype(F32) * scale
        s = jnp.where(mask[None, None], s, NEG_INF)
        p = jax.nn.softmax(s, axis=-1).astype(vb.dtype)
        return jnp.einsum("bhqk,bkhd->bqhd", p, vb)

    o_meta = attend(q[:, :N_META], k[:, :N_META], v[:, :N_META], jnp.ones((N_META, N_META), bool))
    n_real = L - N_META
    n_blk = n_real // Q_BLOCK
    q_blk = q[:, N_META:].reshape(B, n_blk, Q_BLOCK, H, MLA_QK).transpose(1, 0, 2, 3, 4)
    cid_blk = cid[N_META:].reshape(n_blk, Q_BLOCK)
    o_real = lax.map(lambda args: attend(args[0], k, v, cid[None, :] <= args[1][:, None]),
                     (q_blk, cid_blk))
    o_real = o_real.transpose(1, 0, 2, 3, 4).reshape(B, n_real, H, MLA_V)
    o = jnp.concatenate([o_meta, o_real], axis=1).reshape(B, L, H * MLA_V)
    return o @ w_o


def gla_chunkwise(q, k, v, log_f, chunk):
    B, L, H, DK = q.shape
    DV = v.shape[-1]
    n_pad = (-L) % chunk
    NC = (L + n_pad) // chunk

    def to_chunks(t):
        return pad_front(t, n_pad).reshape(B, NC, chunk, H, t.shape[-1]).transpose(1, 0, 3, 2, 4)

    qc, kc, vc, gc = map(to_chunks, (q, k, v, log_f))
    G = jnp.cumsum(gc, axis=3)
    G_last = G[..., -1:, :]
    q_dec = qc * jnp.exp(G)
    k_inv = kc * jnp.exp(-G)
    k_tail = kc * jnp.exp(G_last - G)
    causal = jnp.tril(jnp.ones((chunk, chunk), bool))
    attn = jnp.where(causal, jnp.einsum("nbhid,nbhjd->nbhij", q_dec, k_inv), 0.0)
    o_intra = jnp.einsum("nbhij,nbhjv->nbhiv", attn, vc)

    def step(state, xs):
        qd, kt, vv, gl = xs
        o_inter = jnp.einsum("bhid,bhdv->bhiv", qd, state)
        state = state * jnp.exp(gl)[:, :, 0, :, None] + jnp.einsum("bhjd,bhjv->bhdv", kt, vv)
        return state, o_inter

    _, o_inter = lax.scan(step, jnp.zeros((B, H, DK, DV), F32), (q_dec, k_tail, vc, G_last))
    o = (o_intra + o_inter).transpose(1, 0, 3, 2, 4).reshape(B, NC * chunk, H, DV)
    return o[:, n_pad:]


def hgrn2_mixer(a, w_in, lb, g_o, w_o):
    B, L, _ = a.shape
    H = HGRN_HEADS
    q, f, i_in, g = jnp.split(a @ w_in, 4, axis=-1)
    q = jax.nn.silu(q.astype(F32))
    forget = lb + (1.0 - lb) * jax.nn.sigmoid(f.astype(F32))
    log_f = jnp.log(forget)
    k = 1.0 - forget
    shp = (B, L, H, HGRN_DK)
    o = gla_chunkwise(q.reshape(shp), k.reshape(shp),
                      i_in.astype(F32).reshape(B, L, H, HGRN_DV), log_f.reshape(shp), HGRN_CHUNK)
    o = rms_norm(o, g_o).reshape(B, L, H * HGRN_DV) * jax.nn.silu(g.astype(F32))
    return o.astype(a.dtype) @ w_o


def _complex_affine_combine(e1, e2):
    a1r, a1i, b1r, b1i = e1
    a2r, a2i, b2r, b2i = e2
    return (a2r * a1r - a2i * a1i,
            a2r * a1i + a2i * a1r,
            a2r * b1r - a2i * b1i + b2r,
            a2r * b1i + a2i * b1r + b2i)


def s5_mixer(a, lam_re, lam_im, log_dt, b_re, b_im, c_re, c_im, d_skip, w_glu):
    B, L, D = a.shape
    G, P, K, C = S5_GROUPS, S5_STATE, S5_GROUP, CHUNK
    lam_re, lam_im = lam_re.astype(F32), lam_im.astype(F32)
    b_re, b_im = b_re.astype(F32), b_im.astype(F32)
    c_re, c_im = c_re.astype(F32), c_im.astype(F32)
    dt = jnp.exp(log_dt.astype(F32))[:, None]
    mag = jnp.exp(lam_re * dt)
    abar_re = mag * jnp.cos(lam_im * dt)
    abar_im = mag * jnp.sin(lam_im * dt)
    den = lam_re * lam_re + lam_im * lam_im
    zoh_re = ((abar_re - 1.0) * lam_re + abar_im * lam_im) / den
    zoh_im = (abar_im * lam_re - (abar_re - 1.0) * lam_im) / den
    bbar_re = zoh_re[..., None] * b_re - zoh_im[..., None] * b_im
    bbar_im = zoh_re[..., None] * b_im + zoh_im[..., None] * b_re
    steps = (jnp.arange(C, dtype=F32) + 1.0)[:, None, None] * dt[None]
    pmag = jnp.exp(lam_re[None] * steps)
    pow_re = pmag * jnp.cos(lam_im[None] * steps)
    pow_im = pmag * jnp.sin(lam_im[None] * steps)

    n_pad = (-L) % C
    NC = (L + n_pad) // C
    u = pad_front(a.astype(F32), n_pad).reshape(B, NC, C, G, K).transpose(1, 0, 2, 3, 4)

    def step(carry, uc):
        xr0, xi0 = carry
        bu_re = jnp.einsum("bcgk,gpk->bcgp", uc, bbar_re)
        bu_im = jnp.einsum("bcgk,gpk->bcgp", uc, bbar_im)
        ar = jnp.broadcast_to(abar_re, bu_re.shape)
        ai = jnp.broadcast_to(abar_im, bu_im.shape)
        _, _, xr, xi = lax.associative_scan(_complex_affine_combine, (ar, ai, bu_re, bu_im), axis=1)
        xr = xr + pow_re * xr0[:, None] - pow_im * xi0[:, None]
        xi = xi + pow_re * xi0[:, None] + pow_im * xr0[:, None]
        y = jnp.einsum("bcgp,gkp->bcgk", xr, c_re) - jnp.einsum("bcgp,gkp->bcgk", xi, c_im)
        return (xr[:, -1], xi[:, -1]), y

    zeros = jnp.zeros((B, G, P), F32)
    _, y = lax.scan(step, (zeros, zeros), u)
    y = y.transpose(1, 0, 2, 3, 4).reshape(B, NC * C, D)[:, n_pad:]
    y = jax.nn.gelu(y + d_skip.astype(F32) * a.astype(F32))
    val, gate = jnp.split(y.astype(a.dtype) @ w_glu, 2, axis=-1)
    return val * jax.nn.sigmoid(gate)


def retention_mixer(a, w_in, gn_g, w_o, cos, sin):
    B, L, _ = a.shape
    H, DK, DV, C = RET_HEADS, RET_DK, RET_DV, CHUNK
    qk_w = H * DK
    q, k, v, g = jnp.split(a @ w_in, [qk_w, 2 * qk_w, 2 * qk_w + H * DV], axis=-1)
    q = apply_rope(q.reshape(B, L, H, DK), cos, sin).astype(F32)
    k = apply_rope(k.reshape(B, L, H, DK), cos, sin).astype(F32) * (DK ** -0.5)
    v = v.reshape(B, L, H, DV).astype(F32)
    n_pad = (-L) % C
    NC = (L + n_pad) // C

    def to_chunks(t):
        return pad_front(t, n_pad).reshape(B, NC, C, H, t.shape[-1]).transpose(1, 0, 3, 2, 4)

    qc, kc, vc = map(to_chunks, (q, k, v))
    log_gamma = jnp.log(1.0 - jnp.exp2(-5.0 - jnp.arange(H, dtype=F32)))
    pos = jnp.arange(C, dtype=F32)
    diff = pos[:, None] - pos[None, :]
    decay = jnp.where(diff >= 0, jnp.exp(diff[None] * log_gamma[:, None, None]), 0.0)
    scores = jnp.einsum("nbhid,nbhjd->nbhij", qc, kc) * decay
    o_intra = jnp.einsum("nbhij,nbhjv->nbhiv", scores, vc)
    q_decay = jnp.exp((pos[None, :] + 1.0) * log_gamma[:, None])[..., None]
    k_decay = jnp.exp((C - 1.0 - pos[None, :]) * log_gamma[:, None])[..., None]
    chunk_decay = jnp.exp(C * log_gamma)[:, None, None]

    def step(state, xs):
        qi, ki, vi = xs
        o_inter = jnp.einsum("bhid,bhdv->bhiv", qi * q_decay, state)
        state = state * chunk_decay + jnp.einsum("bhjd,bhjv->bhdv", ki * k_decay, vi)
        return state, o_inter

    _, o_inter = lax.scan(step, jnp.zeros((B, H, DK, DV), F32), (qc, kc, vc))
    o = (o_intra + o_inter).transpose(1, 0, 3, 2, 4).reshape(B, NC * C, H, DV)[:, n_pad:]
    mu = jnp.mean(o, axis=-1, keepdims=True)
    var = jnp.mean(jnp.square(o - mu), axis=-1, keepdims=True)
    o = ((o - mu) * lax.rsqrt(var + EPS)).reshape(B, L, H * DV)
    o = o * gn_g.astype(F32) * jax.nn.silu(g.astype(F32))
    return o.astype(a.dtype) @ w_o


def conv_glu_ffn(a, w_up, conv_w, conv_b, w_down):
    u = a @ w_up
    u = lax.conv_general_dilated(
        u, conv_w[:, None, :].astype(u.dtype), window_strides=(1,),
        padding=[(CONV_WIDTH - 1, 0)], dimension_numbers=("NWC", "WIO", "NWC"),
        feature_group_count=u.shape[-1]) + conv_b.astype(u.dtype)
    gate, val = jnp.split(u, 2, axis=-1)
    return (jax.nn.silu(gate) * val) @ w_down


def _fwd_setup_inputs(seed: int = 0) -> dict:
    key = jax.random.key(seed)
    ks = iter(jax.random.split(key, 40))
    D = D_MODEL

    def nrm(shape, scale):
        return jax.random.normal(next(ks), shape, F32) * scale

    def gain(shape):
        return 1.0 + 0.02 * jax.random.normal(next(ks), shape, F32)

    n_a, n_b, n_c, n_d = N_MLA_L, N_HGRN_L, N_S5_L, N_RET_L
    return {
        "x": nrm((BATCH, SEQ, D), 1.0),
        "meta_tokens": nrm((N_META, D), 1.0),
        "norm_mix_g": gain((DEPTH, D)),
        "norm_ffn_g": gain((DEPTH, D)),
        "mla_w_down": nrm((n_a, D, MLA_Q_LORA + MLA_KV_LORA + MLA_ROPE), D ** -0.5),
        "mla_cq_norm_g": gain((n_a, MLA_Q_LORA)),
        "mla_ckv_norm_g": gain((n_a, MLA_KV_LORA)),
        "mla_w_uq": nrm((n_a, MLA_Q_LORA, MLA_HEADS * MLA_QK), MLA_Q_LORA ** -0.5),
        "mla_w_ukv": nrm((n_a, MLA_KV_LORA, MLA_HEADS * (MLA_NOPE + MLA_V)), MLA_KV_LORA ** -0.5),
        "mla_q_head_g": gain((n_a, MLA_QK)),
        "mla_k_head_g": gain((n_a, MLA_QK)),
        "mla_w_o": nrm((n_a, MLA_HEADS * MLA_V, D), (MLA_HEADS * MLA_V) ** -0.5),
        "hgrn_w_in": nrm((n_b, D, 4 * D), D ** -0.5),
        "hgrn_lb_logits": nrm((DEPTH, HGRN_HEADS * HGRN_DK), 0.1),
        "hgrn_o_norm_g": gain((n_b, HGRN_DV)),
        "hgrn_w_o": nrm((n_b, D, D), D ** -0.5),
        "s5_lam_re": -0.5 + nrm((n_c, S5_GROUPS, S5_STATE), 0.01),
        "s5_lam_im": jnp.pi * jnp.arange(S5_STATE, dtype=F32)[None, None, :] + nrm((n_c, S5_GROUPS, S5_STATE), 0.01),
        "s5_log_dt": jax.random.uniform(next(ks), (n_c, S5_GROUPS), F32,
                                        minval=math.log(S5_DT_MIN), maxval=math.log(S5_DT_MAX)),
        "s5_b_re": nrm((n_c, S5_GROUPS, S5_STATE, S5_GROUP), (2 * S5_GROUP) ** -0.5),
        "s5_b_im": nrm((n_c, S5_GROUPS, S5_STATE, S5_GROUP), (2 * S5_GROUP) ** -0.5),
        "s5_c_re": nrm((n_c, S5_GROUPS, S5_GROUP, S5_STATE), (2 * S5_STATE) ** -0.5),
        "s5_c_im": nrm((n_c, S5_GROUPS, S5_GROUP, S5_STATE), (2 * S5_STATE) ** -0.5),
        "s5_d": nrm((n_c, D), 1.0),
        "s5_w_glu": nrm((n_c, D, 2 * D), D ** -0.5),
        "ret_w_in": nrm((n_d, D, 2 * RET_HEADS * RET_DK + 2 * RET_HEADS * RET_DV), D ** -0.5),
        "ret_gn_g": gain((n_d, RET_HEADS * RET_DV)),
        "ret_w_o": nrm((n_d, RET_HEADS * RET_DV, D), (RET_HEADS * RET_DV) ** -0.5),
        "ffn_w_up": nrm((DEPTH, D, 2 * FFN_HIDDEN), D ** -0.5),
        "ffn_conv_w": nrm((DEPTH, CONV_WIDTH, 2 * FFN_HIDDEN), CONV_WIDTH ** -0.5),
        "ffn_conv_b": nrm((DEPTH, 2 * FFN_HIDDEN), 0.01),
        "ffn_w_down": nrm((DEPTH, FFN_HIDDEN, D), FFN_HIDDEN ** -0.5),
    }


def _fwd_reference(x, meta_tokens, norm_mix_g, norm_ffn_g,
              mla_w_down, mla_cq_norm_g, mla_ckv_norm_g, mla_w_uq, mla_w_ukv, mla_q_head_g, mla_k_head_g, mla_w_o,
              hgrn_w_in, hgrn_lb_logits, hgrn_o_norm_g, hgrn_w_o,
              s5_lam_re, s5_lam_im, s5_log_dt, s5_b_re, s5_b_im, s5_c_re, s5_c_im, s5_d, s5_w_glu,
              ret_w_in, ret_gn_g, ret_w_o,
              ffn_w_up, ffn_conv_w, ffn_conv_b, ffn_w_down):
    B = x.shape[0]
    L = x.shape[1] + N_META
    h = jnp.concatenate(
        [jnp.broadcast_to(meta_tokens[None].astype(x.dtype), (B, N_META, D_MODEL)), x], axis=1)
    cid = chunk_ids(L)
    cos_a, sin_a = rope_tables(L, MLA_ROPE)
    cos_d, sin_d = rope_tables(L, RET_DK)
    lb_cum = jnp.cumsum(jax.nn.softmax(hgrn_lb_logits.astype(F32), axis=0), axis=0)
    lb_all = lb_cum - lb_cum[0:1]

    for i in range(DEPTH):
        m, j = i % N_MIXERS, i // N_MIXERS
        a = rms_norm(h, norm_mix_g[i])
        if m == 0:
            y = mla_mixer(a, mla_w_down[j], mla_cq_norm_g[j], mla_ckv_norm_g[j], mla_w_uq[j], mla_w_ukv[j],
                          mla_q_head_g[j], mla_k_head_g[j], mla_w_o[j], cos_a, sin_a, cid)
        elif m == 1:
            y = hgrn2_mixer(a, hgrn_w_in[j], lb_all[i], hgrn_o_norm_g[j], hgrn_w_o[j])
        elif m == 2:
            y = s5_mixer(a, s5_lam_re[j], s5_lam_im[j], s5_log_dt[j], s5_b_re[j], s5_b_im[j],
                         s5_c_re[j], s5_c_im[j], s5_d[j], s5_w_glu[j])
        else:
            y = retention_mixer(a, ret_w_in[j], ret_gn_g[j], ret_w_o[j], cos_d, sin_d)
        h = h + y.astype(h.dtype)
        f = conv_glu_ffn(rms_norm(h, norm_ffn_g[i]), ffn_w_up[i], ffn_conv_w[i], ffn_conv_b[i], ffn_w_down[i])
        h = h + f.astype(h.dtype)
    return h[:, N_META:]


import jax as _jax
import jax.numpy as _jnp

TWIN_FORMAT = 'train_step'
FWD_PARAMS = ['x', 'meta_tokens', 'norm_mix_g', 'norm_ffn_g', 'mla_w_down', 'mla_cq_norm_g', 'mla_ckv_norm_g', 'mla_w_uq', 'mla_w_ukv', 'mla_q_head_g', 'mla_k_head_g', 'mla_w_o', 'hgrn_w_in', 'hgrn_lb_logits', 'hgrn_o_norm_g', 'hgrn_w_o', 's5_lam_re', 's5_lam_im', 's5_log_dt', 's5_b_re', 's5_b_im', 's5_c_re', 's5_c_im', 's5_d', 's5_w_glu', 'ret_w_in', 'ret_gn_g', 'ret_w_o', 'ffn_w_up', 'ffn_conv_w', 'ffn_conv_b', 'ffn_w_down']
TWIN_WEIGHTS = ['meta_tokens', 'norm_mix_g', 'norm_ffn_g', 'mla_w_down', 'mla_cq_norm_g', 'mla_ckv_norm_g', 'mla_w_uq', 'mla_w_ukv', 'mla_q_head_g', 'mla_k_head_g', 'mla_w_o', 'hgrn_w_in', 'hgrn_lb_logits', 'hgrn_o_norm_g', 'hgrn_w_o', 's5_lam_re', 's5_lam_im', 's5_log_dt', 's5_b_re', 's5_b_im', 's5_c_re', 's5_c_im', 's5_d', 's5_w_glu', 'ret_w_in', 'ret_gn_g', 'ret_w_o', 'ffn_w_up', 'ffn_conv_w', 'ffn_conv_b', 'ffn_w_down']
TWIN_DIFF_INPUT = 'x'
TWIN_INPUTS = ['x', 'meta_tokens', 'norm_mix_g', 'norm_ffn_g', 'mla_w_down', 'mla_cq_norm_g', 'mla_ckv_norm_g', 'mla_w_uq', 'mla_w_ukv', 'mla_q_head_g', 'mla_k_head_g', 'mla_w_o', 'hgrn_w_in', 'hgrn_lb_logits', 'hgrn_o_norm_g', 'hgrn_w_o', 's5_lam_re', 's5_lam_im', 's5_log_dt', 's5_b_re', 's5_b_im', 's5_c_re', 's5_c_im', 's5_d', 's5_w_glu', 'ret_w_in', 'ret_gn_g', 'ret_w_o', 'ffn_w_up', 'ffn_conv_w', 'ffn_conv_b', 'ffn_w_down', 'loss_target', 'm_meta_tokens', 'm_norm_mix_g', 'm_norm_ffn_g', 'm_mla_w_down', 'm_mla_cq_norm_g', 'm_mla_ckv_norm_g', 'm_mla_w_uq', 'm_mla_w_ukv', 'm_mla_q_head_g', 'm_mla_k_head_g', 'm_mla_w_o', 'm_hgrn_w_in', 'm_hgrn_lb_logits', 'm_hgrn_o_norm_g', 'm_hgrn_w_o', 'm_s5_lam_re', 'm_s5_lam_im', 'm_s5_log_dt', 'm_s5_b_re', 'm_s5_b_im', 'm_s5_c_re', 'm_s5_c_im', 'm_s5_d', 'm_s5_w_glu', 'm_ret_w_in', 'm_ret_gn_g', 'm_ret_w_o', 'm_ffn_w_up', 'm_ffn_conv_w', 'm_ffn_conv_b', 'm_ffn_w_down', 'v_meta_tokens', 'v_norm_mix_g', 'v_norm_ffn_g', 'v_mla_w_down', 'v_mla_cq_norm_g', 'v_mla_ckv_norm_g', 'v_mla_w_uq', 'v_mla_w_ukv', 'v_mla_q_head_g', 'v_mla_k_head_g', 'v_mla_w_o', 'v_hgrn_w_in', 'v_hgrn_lb_logits', 'v_hgrn_o_norm_g', 'v_hgrn_w_o', 'v_s5_lam_re', 'v_s5_lam_im', 'v_s5_log_dt', 'v_s5_b_re', 'v_s5_b_im', 'v_s5_c_re', 'v_s5_c_im', 'v_s5_d', 'v_s5_w_glu', 'v_ret_w_in', 'v_ret_gn_g', 'v_ret_w_o', 'v_ffn_w_up', 'v_ffn_conv_w', 'v_ffn_conv_b', 'v_ffn_w_down']
TWIN_OUTPUTS = ['loss', 'grad_x', 'grad_meta_tokens', 'grad_norm_mix_g', 'grad_norm_ffn_g', 'grad_mla_w_down', 'grad_mla_cq_norm_g', 'grad_mla_ckv_norm_g', 'grad_mla_w_uq', 'grad_mla_w_ukv', 'grad_mla_q_head_g', 'grad_mla_k_head_g', 'grad_mla_w_o', 'grad_hgrn_w_in', 'grad_hgrn_lb_logits', 'grad_hgrn_o_norm_g', 'grad_hgrn_w_o', 'grad_s5_lam_re', 'grad_s5_lam_im', 'grad_s5_log_dt', 'grad_s5_b_re', 'grad_s5_b_im', 'grad_s5_c_re', 'grad_s5_c_im', 'grad_s5_d', 'grad_s5_w_glu', 'grad_ret_w_in', 'grad_ret_gn_g', 'grad_ret_w_o', 'grad_ffn_w_up', 'grad_ffn_conv_w', 'grad_ffn_conv_b', 'grad_ffn_w_down', 'delta_meta_tokens', 'delta_norm_mix_g', 'delta_norm_ffn_g', 'delta_mla_w_down', 'delta_mla_cq_norm_g', 'delta_mla_ckv_norm_g', 'delta_mla_w_uq', 'delta_mla_w_ukv', 'delta_mla_q_head_g', 'delta_mla_k_head_g', 'delta_mla_w_o', 'delta_hgrn_w_in', 'delta_hgrn_lb_logits', 'delta_hgrn_o_norm_g', 'delta_hgrn_w_o', 'delta_s5_lam_re', 'delta_s5_lam_im', 'delta_s5_log_dt', 'delta_s5_b_re', 'delta_s5_b_im', 'delta_s5_c_re', 'delta_s5_c_im', 'delta_s5_d', 'delta_s5_w_glu', 'delta_ret_w_in', 'delta_ret_gn_g', 'delta_ret_w_o', 'delta_ffn_w_up', 'delta_ffn_conv_w', 'delta_ffn_conv_b', 'delta_ffn_w_down', 'new_m_meta_tokens', 'new_m_norm_mix_g', 'new_m_norm_ffn_g', 'new_m_mla_w_down', 'new_m_mla_cq_norm_g', 'new_m_mla_ckv_norm_g', 'new_m_mla_w_uq', 'new_m_mla_w_ukv', 'new_m_mla_q_head_g', 'new_m_mla_k_head_g', 'new_m_mla_w_o', 'new_m_hgrn_w_in', 'new_m_hgrn_lb_logits', 'new_m_hgrn_o_norm_g', 'new_m_hgrn_w_o', 'new_m_s5_lam_re', 'new_m_s5_lam_im', 'new_m_s5_log_dt', 'new_m_s5_b_re', 'new_m_s5_b_im', 'new_m_s5_c_re', 'new_m_s5_c_im', 'new_m_s5_d', 'new_m_s5_w_glu', 'new_m_ret_w_in', 'new_m_ret_gn_g', 'new_m_ret_w_o', 'new_m_ffn_w_up', 'new_m_ffn_conv_w', 'new_m_ffn_conv_b', 'new_m_ffn_w_down', 'new_v_meta_tokens', 'new_v_norm_mix_g', 'new_v_norm_ffn_g', 'new_v_mla_w_down', 'new_v_mla_cq_norm_g', 'new_v_mla_ckv_norm_g', 'new_v_mla_w_uq', 'new_v_mla_w_ukv', 'new_v_mla_q_head_g', 'new_v_mla_k_head_g', 'new_v_mla_w_o', 'new_v_hgrn_w_in', 'new_v_hgrn_lb_logits', 'new_v_hgrn_o_norm_g', 'new_v_hgrn_w_o', 'new_v_s5_lam_re', 'new_v_s5_lam_im', 'new_v_s5_log_dt', 'new_v_s5_b_re', 'new_v_s5_b_im', 'new_v_s5_c_re', 'new_v_s5_c_im', 'new_v_s5_d', 'new_v_s5_w_glu', 'new_v_ret_w_in', 'new_v_ret_gn_g', 'new_v_ret_w_o', 'new_v_ffn_w_up', 'new_v_ffn_conv_w', 'new_v_ffn_conv_b', 'new_v_ffn_w_down']
TWIN_LEAF_KINDS = {'loss': 'loss', 'grad_x': 'grad_x', 'grad_meta_tokens': 'grad_w', 'grad_norm_mix_g': 'grad_w', 'grad_norm_ffn_g': 'grad_w', 'grad_mla_w_down': 'grad_w', 'grad_mla_cq_norm_g': 'grad_w', 'grad_mla_ckv_norm_g': 'grad_w', 'grad_mla_w_uq': 'grad_w', 'grad_mla_w_ukv': 'grad_w', 'grad_mla_q_head_g': 'grad_w', 'grad_mla_k_head_g': 'grad_w', 'grad_mla_w_o': 'grad_w', 'grad_hgrn_w_in': 'grad_w', 'grad_hgrn_lb_logits': 'grad_w', 'grad_hgrn_o_norm_g': 'grad_w', 'grad_hgrn_w_o': 'grad_w', 'grad_s5_lam_re': 'grad_w', 'grad_s5_lam_im': 'grad_w', 'grad_s5_log_dt': 'grad_w', 'grad_s5_b_re': 'grad_w', 'grad_s5_b_im': 'grad_w', 'grad_s5_c_re': 'grad_w', 'grad_s5_c_im': 'grad_w', 'grad_s5_d': 'grad_w', 'grad_s5_w_glu': 'grad_w', 'grad_ret_w_in': 'grad_w', 'grad_ret_gn_g': 'grad_w', 'grad_ret_w_o': 'grad_w', 'grad_ffn_w_up': 'grad_w', 'grad_ffn_conv_w': 'grad_w', 'grad_ffn_conv_b': 'grad_w', 'grad_ffn_w_down': 'grad_w', 'delta_meta_tokens': 'delta_w', 'delta_norm_mix_g': 'delta_w', 'delta_norm_ffn_g': 'delta_w', 'delta_mla_w_down': 'delta_w', 'delta_mla_cq_norm_g': 'delta_w', 'delta_mla_ckv_norm_g': 'delta_w', 'delta_mla_w_uq': 'delta_w', 'delta_mla_w_ukv': 'delta_w', 'delta_mla_q_head_g': 'delta_w', 'delta_mla_k_head_g': 'delta_w', 'delta_mla_w_o': 'delta_w', 'delta_hgrn_w_in': 'delta_w', 'delta_hgrn_lb_logits': 'delta_w', 'delta_hgrn_o_norm_g': 'delta_w', 'delta_hgrn_w_o': 'delta_w', 'delta_s5_lam_re': 'delta_w', 'delta_s5_lam_im': 'delta_w', 'delta_s5_log_dt': 'delta_w', 'delta_s5_b_re': 'delta_w', 'delta_s5_b_im': 'delta_w', 'delta_s5_c_re': 'delta_w', 'delta_s5_c_im': 'delta_w', 'delta_s5_d': 'delta_w', 'delta_s5_w_glu': 'delta_w', 'delta_ret_w_in': 'delta_w', 'delta_ret_gn_g': 'delta_w', 'delta_ret_w_o': 'delta_w', 'delta_ffn_w_up': 'delta_w', 'delta_ffn_conv_w': 'delta_w', 'delta_ffn_conv_b': 'delta_w', 'delta_ffn_w_down': 'delta_w', 'new_m_meta_tokens': 'new_m', 'new_m_norm_mix_g': 'new_m', 'new_m_norm_ffn_g': 'new_m', 'new_m_mla_w_down': 'new_m', 'new_m_mla_cq_norm_g': 'new_m', 'new_m_mla_ckv_norm_g': 'new_m', 'new_m_mla_w_uq': 'new_m', 'new_m_mla_w_ukv': 'new_m', 'new_m_mla_q_head_g': 'new_m', 'new_m_mla_k_head_g': 'new_m', 'new_m_mla_w_o': 'new_m', 'new_m_hgrn_w_in': 'new_m', 'new_m_hgrn_lb_logits': 'new_m', 'new_m_hgrn_o_norm_g': 'new_m', 'new_m_hgrn_w_o': 'new_m', 'new_m_s5_lam_re': 'new_m', 'new_m_s5_lam_im': 'new_m', 'new_m_s5_log_dt': 'new_m', 'new_m_s5_b_re': 'new_m', 'new_m_s5_b_im': 'new_m', 'new_m_s5_c_re': 'new_m', 'new_m_s5_c_im': 'new_m', 'new_m_s5_d': 'new_m', 'new_m_s5_w_glu': 'new_m', 'new_m_ret_w_in': 'new_m', 'new_m_ret_gn_g': 'new_m', 'new_m_ret_w_o': 'new_m', 'new_m_ffn_w_up': 'new_m', 'new_m_ffn_conv_w': 'new_m', 'new_m_ffn_conv_b': 'new_m', 'new_m_ffn_w_down': 'new_m', 'new_v_meta_tokens': 'new_v', 'new_v_norm_mix_g': 'new_v', 'new_v_norm_ffn_g': 'new_v', 'new_v_mla_w_down': 'new_v', 'new_v_mla_cq_norm_g': 'new_v', 'new_v_mla_ckv_norm_g': 'new_v', 'new_v_mla_w_uq': 'new_v', 'new_v_mla_w_ukv': 'new_v', 'new_v_mla_q_head_g': 'new_v', 'new_v_mla_k_head_g': 'new_v', 'new_v_mla_w_o': 'new_v', 'new_v_hgrn_w_in': 'new_v', 'new_v_hgrn_lb_logits': 'new_v', 'new_v_hgrn_o_norm_g': 'new_v', 'new_v_hgrn_w_o': 'new_v', 'new_v_s5_lam_re': 'new_v', 'new_v_s5_lam_im': 'new_v', 'new_v_s5_log_dt': 'new_v', 'new_v_s5_b_re': 'new_v', 'new_v_s5_b_im': 'new_v', 'new_v_s5_c_re': 'new_v', 'new_v_s5_c_im': 'new_v', 'new_v_s5_d': 'new_v', 'new_v_s5_w_glu': 'new_v', 'new_v_ret_w_in': 'new_v', 'new_v_ret_gn_g': 'new_v', 'new_v_ret_w_o': 'new_v', 'new_v_ffn_w_up': 'new_v', 'new_v_ffn_conv_w': 'new_v', 'new_v_ffn_conv_b': 'new_v', 'new_v_ffn_w_down': 'new_v'}


def _forward(args):
    return _fwd_reference(*[args[k] for k in FWD_PARAMS])


def _output_shape():
    def fwd():
        inp = _fwd_setup_inputs(0)
        return _fwd_reference(*[inp[k] for k in FWD_PARAMS])
    out = _jax.eval_shape(fwd)
    return out.shape, out.dtype

N_MICROBATCH = 1
ADAM_LR = 0.001
ADAM_B1 = 0.9
ADAM_B2 = 0.999
ADAM_EPS = 1e-08
ADAM_WD = 0.01
ADAM_STEP = 10
PER_EXAMPLE_BATCH_AXIS = {'x': 0, 'loss_target': 0}
SHARED_INPUTS = []
_WEIGHT_DTYPES = {'meta_tokens': _jnp.float32, 'norm_mix_g': _jnp.float32, 'norm_ffn_g': _jnp.float32, 'mla_w_down': _jnp.float32, 'mla_cq_norm_g': _jnp.float32, 'mla_ckv_norm_g': _jnp.float32, 'mla_w_uq': _jnp.float32, 'mla_w_ukv': _jnp.float32, 'mla_q_head_g': _jnp.float32, 'mla_k_head_g': _jnp.float32, 'mla_w_o': _jnp.float32, 'hgrn_w_in': _jnp.float32, 'hgrn_lb_logits': _jnp.float32, 'hgrn_o_norm_g': _jnp.float32, 'hgrn_w_o': _jnp.float32, 's5_lam_re': _jnp.float32, 's5_lam_im': _jnp.float32, 's5_log_dt': _jnp.float32, 's5_b_re': _jnp.float32, 's5_b_im': _jnp.float32, 's5_c_re': _jnp.float32, 's5_c_im': _jnp.float32, 's5_d': _jnp.float32, 's5_w_glu': _jnp.float32, 'ret_w_in': _jnp.float32, 'ret_gn_g': _jnp.float32, 'ret_w_o': _jnp.float32, 'ffn_w_up': _jnp.float32, 'ffn_conv_w': _jnp.float32, 'ffn_conv_b': _jnp.float32, 'ffn_w_down': _jnp.float32}
MOMENT_SCALE = {'meta_tokens': 7.859117e-02, 'norm_mix_g': 1.021242e+01, 'norm_ffn_g': 2.619780e+01, 'mla_w_down': 5.203163e-01, 'mla_cq_norm_g': 3.692749e-01, 'mla_ckv_norm_g': 1.026905e+00, 'mla_w_uq': 1.810073e-01, 'mla_w_ukv': 2.615691e-01, 'mla_q_head_g': 1.043058e+00, 'mla_k_head_g': 1.050462e+00, 'mla_w_o': 3.154162e-01, 'hgrn_w_in': 4.501803e-01, 'hgrn_lb_logits': 1.651876e-02, 'hgrn_o_norm_g': 9.162901e+01, 'hgrn_w_o': 6.433908e-01, 's5_lam_re': 1.769623e-02, 's5_lam_im': 1.596378e-02, 's5_log_dt': 8.882622e+00, 's5_b_re': 1.202038e-02, 's5_b_im': 1.215866e-02, 's5_c_re': 2.392626e-02, 's5_c_im': 2.543833e-02, 's5_d': 4.915603e+00, 's5_w_glu': 1.379608e+00, 'ret_w_in': 3.273791e-01, 'ret_gn_g': 5.760505e+00, 'ret_w_o': 3.401847e-01, 'ffn_w_up': 4.464529e-01, 'ffn_conv_w': 3.594693e+00, 'ffn_conv_b': 3.380577e+00, 'ffn_w_down': 5.925293e-01}


def _to_microbatches(a, axis):
    t = _jnp.moveaxis(a, axis, 0)
    t = t.reshape((N_MICROBATCH, t.shape[0] // N_MICROBATCH) + t.shape[1:])
    return _jnp.moveaxis(t, 1, axis + 1)


def setup_inputs(seed: int = 0) -> dict:
    inp = _fwd_setup_inputs(seed)
    key = _jax.random.fold_in(_jax.random.key(seed), 7919)
    shape, _ = _output_shape()
    out = dict(inp)
    out["loss_target"] = _jax.random.normal(_jax.random.fold_in(key, 0), shape, _jnp.float32)
    for i, name in enumerate(TWIN_WEIGHTS):
        w = inp[name].astype(_jnp.float32)
        if MOMENT_SCALE is None:
            s = _jnp.sqrt(_jnp.mean(_jnp.square(w)) + 1e-30)
        else:
            s = MOMENT_SCALE[name]
        km, kv = _jax.random.split(_jax.random.fold_in(key, i + 1))
        out[name] = w
        out["m_" + name] = s * _jax.random.normal(km, w.shape, _jnp.float32)
        out["v_" + name] = (s * s) * _jax.random.uniform(kv, w.shape, _jnp.float32, 0.5, 1.5)
    if N_MICROBATCH > 1:
        for name, axis in PER_EXAMPLE_BATCH_AXIS.items():
            out[name] = _to_microbatches(out[name], axis)
    return {'x': out['x'], 'meta_tokens': out['meta_tokens'], 'norm_mix_g': out['norm_mix_g'], 'norm_ffn_g': out['norm_ffn_g'], 'mla_w_down': out['mla_w_down'], 'mla_cq_norm_g': out['mla_cq_norm_g'], 'mla_ckv_norm_g': out['mla_ckv_norm_g'], 'mla_w_uq': out['mla_w_uq'], 'mla_w_ukv': out['mla_w_ukv'], 'mla_q_head_g': out['mla_q_head_g'], 'mla_k_head_g': out['mla_k_head_g'], 'mla_w_o': out['mla_w_o'], 'hgrn_w_in': out['hgrn_w_in'], 'hgrn_lb_logits': out['hgrn_lb_logits'], 'hgrn_o_norm_g': out['hgrn_o_norm_g'], 'hgrn_w_o': out['hgrn_w_o'], 's5_lam_re': out['s5_lam_re'], 's5_lam_im': out['s5_lam_im'], 's5_log_dt': out['s5_log_dt'], 's5_b_re': out['s5_b_re'], 's5_b_im': out['s5_b_im'], 's5_c_re': out['s5_c_re'], 's5_c_im': out['s5_c_im'], 's5_d': out['s5_d'], 's5_w_glu': out['s5_w_glu'], 'ret_w_in': out['ret_w_in'], 'ret_gn_g': out['ret_gn_g'], 'ret_w_o': out['ret_w_o'], 'ffn_w_up': out['ffn_w_up'], 'ffn_conv_w': out['ffn_conv_w'], 'ffn_conv_b': out['ffn_conv_b'], 'ffn_w_down': out['ffn_w_down'], 'loss_target': out['loss_target'], 'm_meta_tokens': out['m_meta_tokens'], 'm_norm_mix_g': out['m_norm_mix_g'], 'm_norm_ffn_g': out['m_norm_ffn_g'], 'm_mla_w_down': out['m_mla_w_down'], 'm_mla_cq_norm_g': out['m_mla_cq_norm_g'], 'm_mla_ckv_norm_g': out['m_mla_ckv_norm_g'], 'm_mla_w_uq': out['m_mla_w_uq'], 'm_mla_w_ukv': out['m_mla_w_ukv'], 'm_mla_q_head_g': out['m_mla_q_head_g'], 'm_mla_k_head_g': out['m_mla_k_head_g'], 'm_mla_w_o': out['m_mla_w_o'], 'm_hgrn_w_in': out['m_hgrn_w_in'], 'm_hgrn_lb_logits': out['m_hgrn_lb_logits'], 'm_hgrn_o_norm_g': out['m_hgrn_o_norm_g'], 'm_hgrn_w_o': out['m_hgrn_w_o'], 'm_s5_lam_re': out['m_s5_lam_re'], 'm_s5_lam_im': out['m_s5_lam_im'], 'm_s5_log_dt': out['m_s5_log_dt'], 'm_s5_b_re': out['m_s5_b_re'], 'm_s5_b_im': out['m_s5_b_im'], 'm_s5_c_re': out['m_s5_c_re'], 'm_s5_c_im': out['m_s5_c_im'], 'm_s5_d': out['m_s5_d'], 'm_s5_w_glu': out['m_s5_w_glu'], 'm_ret_w_in': out['m_ret_w_in'], 'm_ret_gn_g': out['m_ret_gn_g'], 'm_ret_w_o': out['m_ret_w_o'], 'm_ffn_w_up': out['m_ffn_w_up'], 'm_ffn_conv_w': out['m_ffn_conv_w'], 'm_ffn_conv_b': out['m_ffn_conv_b'], 'm_ffn_w_down': out['m_ffn_w_down'], 'v_meta_tokens': out['v_meta_tokens'], 'v_norm_mix_g': out['v_norm_mix_g'], 'v_norm_ffn_g': out['v_norm_ffn_g'], 'v_mla_w_down': out['v_mla_w_down'], 'v_mla_cq_norm_g': out['v_mla_cq_norm_g'], 'v_mla_ckv_norm_g': out['v_mla_ckv_norm_g'], 'v_mla_w_uq': out['v_mla_w_uq'], 'v_mla_w_ukv': out['v_mla_w_ukv'], 'v_mla_q_head_g': out['v_mla_q_head_g'], 'v_mla_k_head_g': out['v_mla_k_head_g'], 'v_mla_w_o': out['v_mla_w_o'], 'v_hgrn_w_in': out['v_hgrn_w_in'], 'v_hgrn_lb_logits': out['v_hgrn_lb_logits'], 'v_hgrn_o_norm_g': out['v_hgrn_o_norm_g'], 'v_hgrn_w_o': out['v_hgrn_w_o'], 'v_s5_lam_re': out['v_s5_lam_re'], 'v_s5_lam_im': out['v_s5_lam_im'], 'v_s5_log_dt': out['v_s5_log_dt'], 'v_s5_b_re': out['v_s5_b_re'], 'v_s5_b_im': out['v_s5_b_im'], 'v_s5_c_re': out['v_s5_c_re'], 'v_s5_c_im': out['v_s5_c_im'], 'v_s5_d': out['v_s5_d'], 'v_s5_w_glu': out['v_s5_w_glu'], 'v_ret_w_in': out['v_ret_w_in'], 'v_ret_gn_g': out['v_ret_gn_g'], 'v_ret_w_o': out['v_ret_w_o'], 'v_ffn_w_up': out['v_ffn_w_up'], 'v_ffn_conv_w': out['v_ffn_conv_w'], 'v_ffn_conv_b': out['v_ffn_conv_b'], 'v_ffn_w_down': out['v_ffn_w_down']}


def _loss(weights, diff, rest, loss_target):
    with _jax.named_scope("forward"):
        args = {**rest, TWIN_DIFF_INPUT: diff, **{k: w.astype(_WEIGHT_DTYPES[k]) for k, w in weights.items()}}
        y = _forward(args)
    with _jax.named_scope("loss_head"):
        err = _jnp.square(y.astype(_jnp.float32) - loss_target)
        return 0.5 * _jnp.sum(_jnp.mean(err, axis=-1)) if err.ndim else 0.5 * err


def _adamw(w, g, m, v):
    m = ADAM_B1 * m + (1.0 - ADAM_B1) * g
    v = ADAM_B2 * v + (1.0 - ADAM_B2) * _jnp.square(g)
    m_hat = m / (1.0 - ADAM_B1 ** ADAM_STEP)
    v_hat = v / (1.0 - ADAM_B2 ** ADAM_STEP)
    delta = -ADAM_LR * (m_hat / (_jnp.sqrt(v_hat) + ADAM_EPS) + ADAM_WD * w)
    return delta, m, v


def reference(x, meta_tokens, norm_mix_g, norm_ffn_g, mla_w_down, mla_cq_norm_g, mla_ckv_norm_g, mla_w_uq, mla_w_ukv, mla_q_head_g, mla_k_head_g, mla_w_o, hgrn_w_in, hgrn_lb_logits, hgrn_o_norm_g, hgrn_w_o, s5_lam_re, s5_lam_im, s5_log_dt, s5_b_re, s5_b_im, s5_c_re, s5_c_im, s5_d, s5_w_glu, ret_w_in, ret_gn_g, ret_w_o, ffn_w_up, ffn_conv_w, ffn_conv_b, ffn_w_down, loss_target, m_meta_tokens, m_norm_mix_g, m_norm_ffn_g, m_mla_w_down, m_mla_cq_norm_g, m_mla_ckv_norm_g, m_mla_w_uq, m_mla_w_ukv, m_mla_q_head_g, m_mla_k_head_g, m_mla_w_o, m_hgrn_w_in, m_hgrn_lb_logits, m_hgrn_o_norm_g, m_hgrn_w_o, m_s5_lam_re, m_s5_lam_im, m_s5_log_dt, m_s5_b_re, m_s5_b_im, m_s5_c_re, m_s5_c_im, m_s5_d, m_s5_w_glu, m_ret_w_in, m_ret_gn_g, m_ret_w_o, m_ffn_w_up, m_ffn_conv_w, m_ffn_conv_b, m_ffn_w_down, v_meta_tokens, v_norm_mix_g, v_norm_ffn_g, v_mla_w_down, v_mla_cq_norm_g, v_mla_ckv_norm_g, v_mla_w_uq, v_mla_w_ukv, v_mla_q_head_g, v_mla_k_head_g, v_mla_w_o, v_hgrn_w_in, v_hgrn_lb_logits, v_hgrn_o_norm_g, v_hgrn_w_o, v_s5_lam_re, v_s5_lam_im, v_s5_log_dt, v_s5_b_re, v_s5_b_im, v_s5_c_re, v_s5_c_im, v_s5_d, v_s5_w_glu, v_ret_w_in, v_ret_gn_g, v_ret_w_o, v_ffn_w_up, v_ffn_conv_w, v_ffn_conv_b, v_ffn_w_down):
    given = dict(x=x, meta_tokens=meta_tokens, norm_mix_g=norm_mix_g, norm_ffn_g=norm_ffn_g, mla_w_down=mla_w_down, mla_cq_norm_g=mla_cq_norm_g, mla_ckv_norm_g=mla_ckv_norm_g, mla_w_uq=mla_w_uq, mla_w_ukv=mla_w_ukv, mla_q_head_g=mla_q_head_g, mla_k_head_g=mla_k_head_g, mla_w_o=mla_w_o, hgrn_w_in=hgrn_w_in, hgrn_lb_logits=hgrn_lb_logits, hgrn_o_norm_g=hgrn_o_norm_g, hgrn_w_o=hgrn_w_o, s5_lam_re=s5_lam_re, s5_lam_im=s5_lam_im, s5_log_dt=s5_log_dt, s5_b_re=s5_b_re, s5_b_im=s5_b_im, s5_c_re=s5_c_re, s5_c_im=s5_c_im, s5_d=s5_d, s5_w_glu=s5_w_glu, ret_w_in=ret_w_in, ret_gn_g=ret_gn_g, ret_w_o=ret_w_o, ffn_w_up=ffn_w_up, ffn_conv_w=ffn_conv_w, ffn_conv_b=ffn_conv_b, ffn_w_down=ffn_w_down, loss_target=loss_target, m_meta_tokens=m_meta_tokens, m_norm_mix_g=m_norm_mix_g, m_norm_ffn_g=m_norm_ffn_g, m_mla_w_down=m_mla_w_down, m_mla_cq_norm_g=m_mla_cq_norm_g, m_mla_ckv_norm_g=m_mla_ckv_norm_g, m_mla_w_uq=m_mla_w_uq, m_mla_w_ukv=m_mla_w_ukv, m_mla_q_head_g=m_mla_q_head_g, m_mla_k_head_g=m_mla_k_head_g, m_mla_w_o=m_mla_w_o, m_hgrn_w_in=m_hgrn_w_in, m_hgrn_lb_logits=m_hgrn_lb_logits, m_hgrn_o_norm_g=m_hgrn_o_norm_g, m_hgrn_w_o=m_hgrn_w_o, m_s5_lam_re=m_s5_lam_re, m_s5_lam_im=m_s5_lam_im, m_s5_log_dt=m_s5_log_dt, m_s5_b_re=m_s5_b_re, m_s5_b_im=m_s5_b_im, m_s5_c_re=m_s5_c_re, m_s5_c_im=m_s5_c_im, m_s5_d=m_s5_d, m_s5_w_glu=m_s5_w_glu, m_ret_w_in=m_ret_w_in, m_ret_gn_g=m_ret_gn_g, m_ret_w_o=m_ret_w_o, m_ffn_w_up=m_ffn_w_up, m_ffn_conv_w=m_ffn_conv_w, m_ffn_conv_b=m_ffn_conv_b, m_ffn_w_down=m_ffn_w_down, v_meta_tokens=v_meta_tokens, v_norm_mix_g=v_norm_mix_g, v_norm_ffn_g=v_norm_ffn_g, v_mla_w_down=v_mla_w_down, v_mla_cq_norm_g=v_mla_cq_norm_g, v_mla_ckv_norm_g=v_mla_ckv_norm_g, v_mla_w_uq=v_mla_w_uq, v_mla_w_ukv=v_mla_w_ukv, v_mla_q_head_g=v_mla_q_head_g, v_mla_k_head_g=v_mla_k_head_g, v_mla_w_o=v_mla_w_o, v_hgrn_w_in=v_hgrn_w_in, v_hgrn_lb_logits=v_hgrn_lb_logits, v_hgrn_o_norm_g=v_hgrn_o_norm_g, v_hgrn_w_o=v_hgrn_w_o, v_s5_lam_re=v_s5_lam_re, v_s5_lam_im=v_s5_lam_im, v_s5_log_dt=v_s5_log_dt, v_s5_b_re=v_s5_b_re, v_s5_b_im=v_s5_b_im, v_s5_c_re=v_s5_c_re, v_s5_c_im=v_s5_c_im, v_s5_d=v_s5_d, v_s5_w_glu=v_s5_w_glu, v_ret_w_in=v_ret_w_in, v_ret_gn_g=v_ret_gn_g, v_ret_w_o=v_ret_w_o, v_ffn_w_up=v_ffn_w_up, v_ffn_conv_w=v_ffn_conv_w, v_ffn_conv_b=v_ffn_conv_b, v_ffn_w_down=v_ffn_w_down)
    weights = {n: given[n] for n in TWIN_WEIGHTS}
    shared = {n: given[n] for n in SHARED_INPUTS}
    per_example = {n: given[n] for n in ['x']}
    grad_fn = _jax.value_and_grad(_loss, argnums=(0, 1))

    def one_microbatch(ex, loss_target):
        ex = dict(ex)
        diff = ex.pop(TWIN_DIFF_INPUT)
        return grad_fn(weights, diff, {**shared, **ex}, loss_target)

    if N_MICROBATCH == 1:
        loss, (grad_w, grad_x) = one_microbatch(per_example, given["loss_target"])
    else:
        def body(carry, xs):
            loss_sum, grad_sum = carry
            l_k, (gw_k, gx_k) = one_microbatch(xs[0], xs[1])
            with _jax.named_scope("update"):
                return (loss_sum + l_k, _jax.tree.map(_jnp.add, grad_sum, gw_k)), gx_k

        init = (_jnp.zeros((), _jnp.float32), _jax.tree.map(_jnp.zeros_like, weights))
        (loss, grad_w), grad_x = _jax.lax.scan(body, init, (per_example, given["loss_target"]))
    with _jax.named_scope("update"):
        delta_w, new_m, new_v = {}, {}, {}
        for n in TWIN_WEIGHTS:
            delta_w[n], new_m[n], new_v[n] = _adamw(weights[n], grad_w[n], given["m_" + n], given["v_" + n])
    return (loss, grad_x, *[grad_w[n] for n in TWIN_WEIGHTS], *[delta_w[n] for n in TWIN_WEIGHTS],
            *[new_m[n] for n in TWIN_WEIGHTS], *[new_v[n] for n in TWIN_WEIGHTS])
```

```python
import functools
import math

import jax
import jax.numpy as jnp
from jax import lax
from jax.experimental import pallas as pl
from jax.experimental.pallas import tpu as pltpu

F32, BF16 = jnp.float32, jnp.bfloat16
HIGHEST = lax.Precision.HIGHEST
MESH_ID = pl.DeviceIdType.MESH

D = 1024
N_META = 16
PAD = 112
LEAD = PAD + N_META
EPS = 1e-6
NEG_INF = -1e30
CHUNK = 64
VMEM_LIMIT_V7X = 56 * 1024 * 1024
MM_VMEM_BUDGET = 36 * 1024 * 1024

MLA_H, MLA_NOPE, MLA_ROPE, MLA_V = 8, 128, 64, 128
MLA_QK = MLA_NOPE + MLA_ROPE
MLA_QL, MLA_KVL = 384, 256
HG_H, HG_D, HG_C = 8, 128, 16
S5_G, S5_P, S5_K = 64, 64, 16
RET_H, RET_DK, RET_DV = 4, 256, 512
FFN_F = 2816

ADAM_LR, ADAM_B1, ADAM_B2, ADAM_EPS, ADAM_WD, ADAM_STEP = 0.001, 0.9, 0.999, 1e-08, 0.01, 10

WEIGHTS = ['meta_tokens', 'norm_mix_g', 'norm_ffn_g', 'mla_w_down', 'mla_cq_norm_g', 'mla_ckv_norm_g', 'mla_w_uq',
           'mla_w_ukv', 'mla_q_head_g', 'mla_k_head_g', 'mla_w_o', 'hgrn_w_in', 'hgrn_lb_logits', 'hgrn_o_norm_g',
           'hgrn_w_o', 's5_lam_re', 's5_lam_im', 's5_log_dt', 's5_b_re', 's5_b_im', 's5_c_re', 's5_c_im', 's5_d',
           's5_w_glu', 'ret_w_in', 'ret_gn_g', 'ret_w_o', 'ffn_w_up', 'ffn_conv_w', 'ffn_conv_b', 'ffn_w_down']
SHARD_AXIS = {'meta_tokens': 1, 'mla_w_down': 1, 'mla_w_uq': 2, 'mla_w_ukv': 2, 'mla_w_o': 1, 'hgrn_w_in': 2,
              'hgrn_w_o': 1, 's5_d': 1, 's5_w_glu': 2, 'ret_w_in': 2, 'ret_gn_g': 1, 'ret_w_o': 1, 'ffn_w_up': 2,
              'ffn_conv_w': 2, 'ffn_w_down': 1}
BIG = ['mla_w_down', 'mla_w_uq', 'mla_w_ukv', 'mla_w_o', 'hgrn_w_in', 'hgrn_w_o', 's5_w_glu', 'ret_w_in', 'ret_w_o',
       'ffn_w_up', 'ffn_w_down']
SMALL_SHARDED = ['meta_tokens', 's5_d', 'ret_gn_g', 'ffn_conv_w']
REPLICATED = [n for n in WEIGHTS if n not in SHARD_AXIS]


def _cparams():
    return pltpu.CompilerParams(vmem_limit_bytes=VMEM_LIMIT_V7X)


def _dg(a, b, ca, cb):
    return lax.dot_general(a.astype(BF16), b.astype(BF16), (((ca,), (cb,)), ((), ())),
                           preferred_element_type=F32)


@jax.custom_vjp
def mm_nn(a, b):
    return _dg(a, b, 1, 0)


@jax.custom_vjp
def mm_nt(a, b):
    return _dg(a, b, 1, 1)


@jax.custom_vjp
def mm_tn(a, b):
    return _dg(a, b, 0, 0)


mm_nn.defvjp(lambda a, b: (mm_nn(a, b), (a, b)),
             lambda r, g: (mm_nt(g, r[1]).astype(r[0].dtype), mm_tn(r[0], g).astype(r[1].dtype)))
mm_nt.defvjp(lambda a, b: (mm_nt(a, b), (a, b)),
             lambda r, g: (mm_nn(g, r[1]).astype(r[0].dtype), mm_tn(g, r[0]).astype(r[1].dtype)))
mm_tn.defvjp(lambda a, b: (mm_tn(a, b), (a, b)),
             lambda r, g: (mm_nt(r[1], g).astype(r[0].dtype), mm_nn(r[0], g).astype(r[1].dtype)))


def _dot_f32(a, b):
    return jnp.dot(a, b, precision=HIGHEST, preferred_element_type=F32)


def _shift_rows(x, s, up):
    n = x.shape[0]
    r = lax.broadcasted_iota(jnp.int32, x.shape, 0)
    if up:
        return jnp.where(r < n - s, pltpu.roll(x, n - s, 0), 0.0)
    return jnp.where(r >= s, pltpu.roll(x, s, 0), 0.0)


@functools.partial(jax.custom_vjp, nondiff_argnums=(1,))
def shift_down(x, s):
    return _shift_rows(x, s, False)


shift_down.defvjp(lambda x, s: (_shift_rows(x, s, False), None), lambda s, _, g: (_shift_rows(g, s, True),))


def _swap32_impl(x):
    ax = x.ndim - 1
    lane = lax.broadcasted_iota(jnp.int32, x.shape, ax)
    return jnp.where(lane < 32, pltpu.roll(x, 96, ax), jnp.where(lane < 64, pltpu.roll(x, 32, ax), 0.0))


@jax.custom_vjp
def swap32(x):
    return _swap32_impl(x)


swap32.defvjp(lambda x: (_swap32_impl(x), None), lambda _, g: (_swap32_impl(g),))


def _rms(x, g):
    return x * lax.rsqrt(jnp.mean(x * x, axis=-1, keepdims=True) + EPS) * g


def _silu(x):
    return x * jax.nn.sigmoid(x)


def _row_ids(pid, n, shape, axis=0):
    return pid * n + lax.broadcasted_iota(jnp.int32, shape, axis)


class Arg:
    def __init__(self, arr, block, imap, diff=True):
        self.arr, self.block, self.imap, self.diff = arr, block, imap, diff


class Out:
    def __init__(self, shape, dtype, block, imap):
        self.shape, self.dtype, self.block, self.imap = shape, dtype, block, imap


def _free_axes(imap, grid):
    ng = len(grid)
    base = tuple(imap(*([0] * ng)))
    free = []
    for ax in range(ng):
        p = [0] * ng
        p[ax] = 1
        if grid[ax] > 1 and tuple(imap(*p)) == base:
            free.append(ax)
    assert free == list(range(ng - len(free), ng)), "revisited blocks must be revisited on the innermost axes"
    return free


def stage_fwd(name, fn, grid, args, outs, state_shape=None):
    n_in, n_out, ng = len(args), len(outs), len(grid)

    def body(*refs):
        pids = tuple(pl.program_id(a) for a in range(ng))
        vals = [r[...] for r in refs[:n_in]]
        o_refs = refs[n_in:n_in + n_out]
        if state_shape is None:
            res = fn(pids, *vals)
        else:
            sv_ref, st_ref = refs[n_in + n_out], refs[n_in + n_out + 1]

            @pl.when(pids[-1] == 0)
            def _():
                st_ref[...] = jnp.zeros(state_shape, F32)

            s = st_ref[...]
            sv_ref[...] = s
            res = fn(pids, *vals, s)
            st_ref[...] = res[-1]
            res = res[:-1]
        for r, v in zip(o_refs, res):
            r[...] = v.astype(r.dtype)

    in_specs = [pl.BlockSpec(a.block, a.imap) for a in args]
    out_specs = [pl.BlockSpec(o.block, o.imap) for o in outs]
    out_shape = [jax.ShapeDtypeStruct(o.shape, o.dtype) for o in outs]
    scratch = []
    if state_shape is not None:
        nz = len(state_shape)
        out_specs.append(pl.BlockSpec((None, None) + tuple(state_shape), lambda i, j: (i, j) + (0,) * nz))
        out_shape.append(jax.ShapeDtypeStruct(tuple(grid) + tuple(state_shape), F32))
        scratch = [pltpu.VMEM(state_shape, F32)]
    return pl.pallas_call(body, grid=grid, in_specs=in_specs, out_specs=out_specs, out_shape=out_shape,
                          scratch_shapes=scratch, name=name, compiler_params=_cparams())(*[a.arr for a in args])


def stage_bwd(name, fn, grid, args, outs, cots, state_shape=None, states=None):
    n_in, n_out, ng = len(args), len(outs), len(grid)
    nb = grid[-1]
    rev = state_shape is not None
    didx = [k for k, a in enumerate(args) if a.diff]
    frees = [_free_axes(args[k].imap, grid) for k in didx]

    def eff(p):
        return tuple(p[:-1]) + (nb - 1 - p[-1],) if rev else tuple(p)

    def wrap(imap):
        return lambda *p: imap(*eff(p))

    def body(*refs):
        pids = tuple(pl.program_id(a) for a in range(ng))
        e = eff(pids)
        vals = [r[...] for r in refs[:n_in]]
        cts = tuple(r[...].astype(F32) for r in refs[n_in:n_in + n_out])
        pos = n_in + n_out
        if rev:
            st_in_ref = refs[pos]
            pos += 1
        g_refs = refs[pos:pos + len(didx)]
        pos += len(didx)
        dvals = [vals[k].astype(F32) for k in didx]

        def f(*dv):
            full = list(vals)
            for k, v in zip(didx, dv[:len(didx)]):
                full[k] = v
            return tuple(fn(e, *full, *dv[len(didx):]))

        if rev:
            ds_ref = refs[pos]

            @pl.when(pids[-1] == 0)
            def _():
                ds_ref[...] = jnp.zeros(state_shape, F32)

            _, vjp = jax.vjp(f, *dvals, st_in_ref[...])
            grads = vjp(cts + (ds_ref[...],))
            ds_ref[...] = grads[-1]
            grads = grads[:-1]
        else:
            _, vjp = jax.vjp(f, *dvals)
            grads = vjp(cts)
        for gref, g, free in zip(g_refs, grads, frees):
            g = g.astype(F32)
            if not free:
                gref[...] = g
            else:
                first = functools.reduce(jnp.logical_and, [pids[ax] == 0 for ax in free])

                @pl.when(first)
                def _():
                    gref[...] = g

                @pl.when(jnp.logical_not(first))
                def _():
                    gref[...] += g

    in_specs = [pl.BlockSpec(a.block, wrap(a.imap)) for a in args]
    in_specs += [pl.BlockSpec(o.block, wrap(o.imap)) for o in outs]
    operands = [a.arr for a in args] + list(cots)
    scratch = []
    if rev:
        nz = len(state_shape)
        in_specs.append(pl.BlockSpec((None, None) + tuple(state_shape), lambda i, j: (i, nb - 1 - j) + (0,) * nz))
        operands.append(states)
        scratch = [pltpu.VMEM(state_shape, F32)]
    out_specs = [pl.BlockSpec(args[k].block, wrap(args[k].imap)) for k in didx]
    out_shape = [jax.ShapeDtypeStruct(args[k].arr.shape, F32) for k in didx]
    return pl.pallas_call(body, grid=grid, in_specs=in_specs, out_specs=out_specs, out_shape=out_shape,
                          scratch_shapes=scratch, name=name, compiler_params=_cparams())(*operands)


def _divisors(n, cands):
    return [c for c in cands if n % c == 0] or [n]


def _nbytes(dt):
    return jnp.dtype(dt).itemsize


def matmul(name, a, b, mode, out_dtype=F32, res=None, mask=False):
    sa, sb, so = _nbytes(a.dtype), _nbytes(b.dtype), _nbytes(out_dtype)
    if mode in ('nn', 'nt'):
        M, K = a.shape
        N = b.shape[1] if mode == 'nn' else b.shape[0]
        best = None
        for tm in _divisors(M, (1408, 1056, 768, 384, 128)):
            for tn in _divisors(N, (1408, 1024, 768, 512, 384, 256, 128)):
                est = 2 * (tm * K * sa + tn * K * sb + tm * tn * (so + (4 if res is not None else 0)))
                if est <= MM_VMEM_BUDGET and (best is None or tm * tn > best[0] * best[1]):
                    best = (tm, tn)
        tm, tn = best
        grid = (M // tm, N // tn)

        def body(*refs):
            a_ref, b_ref = refs[0], refs[1]
            o_ref = refs[-1]
            x = a_ref[...]
            rows = _row_ids(pl.program_id(0), tm, (tm, 1))
            if mask:
                x = jnp.where(rows >= PAD, x, jnp.zeros_like(x))
            acc = _dg(x, b_ref[...], 1, 0 if mode == 'nn' else 1)
            if res is not None:
                acc = refs[2][...] + jnp.where(rows >= PAD, acc, 0.0)
            o_ref[...] = acc.astype(o_ref.dtype)

        in_specs = [pl.BlockSpec((tm, K), lambda i, j: (i, 0)),
                    pl.BlockSpec((K, tn), lambda i, j: (0, j)) if mode == 'nn' else
                    pl.BlockSpec((tn, K), lambda i, j: (j, 0))]
        ops = [a, b]
        if res is not None:
            in_specs.append(pl.BlockSpec((tm, tn), lambda i, j: (i, j)))
            ops.append(res)
        return pl.pallas_call(body, grid=grid, in_specs=in_specs,
                              out_specs=pl.BlockSpec((tm, tn), lambda i, j: (i, j)),
                              out_shape=jax.ShapeDtypeStruct((M, N), out_dtype), name=name,
                              compiler_params=_cparams())(*ops)
    assert mode == 'tn' and res is None
    M, K = a.shape
    N = b.shape[1]
    best = None
    for tk in _divisors(K, (1408, 1024, 768, 512, 384, 256, 128)):
        for tn in _divisors(N, (1408, 1024, 768, 512, 384, 256, 128)):
            est = 2 * (M * tk * sa + M * tn * sb + tk * tn * so)
            if est <= MM_VMEM_BUDGET and (best is None or tk * tn > best[0] * best[1]):
                best = (tk, tn)
    tk, tn = best

    def body_t(a_ref, b_ref, o_ref):
        y = b_ref[...]
        if mask:
            rows = lax.broadcasted_iota(jnp.int32, (M, 1), 0)
            y = jnp.where(rows >= PAD, y, jnp.zeros_like(y))
        o_ref[...] = _dg(a_ref[...], y, 0, 0).astype(o_ref.dtype)

    return pl.pallas_call(body_t, grid=(K // tk, N // tn),
                          in_specs=[pl.BlockSpec((M, tk), lambda i, j: (0, i)),
                                    pl.BlockSpec((M, tn), lambda i, j: (0, j))],
                          out_specs=pl.BlockSpec((tk, tn), lambda i, j: (i, j)),
                          out_shape=jax.ShapeDtypeStruct((K, N), out_dtype), name=name,
                          compiler_params=_cparams())(a, b)


def linear_bwd(name, act, w, dy, mask=False):
    return (matmul(name + "_da", dy, w, 'nt', mask=mask), matmul(name + "_dw", act, dy, 'tn', mask=mask))


def _row_tile(T):
    return _divisors(T, (384, 128))[0]


def _rows(arr, tm):
    return Arg(arr, (tm, arr.shape[1]), lambda i: (i, 0))


def _const(arr, diff=True):
    return Arg(arr, arr.shape, lambda *p: (0,) * arr.ndim, diff)


def _norm_fn(pids, h, g):
    return (_rms(h, g),)


def _norm_bwd_fn(pids, h, g):
    return (_rms(h, g), h)


def norm_fwd(name, h, g, dtype):
    T = h.shape[0]
    tm = _row_tile(T)
    return stage_fwd(name, _norm_fn, (T // tm,), [_rows(h, tm), _const(g)],
                     [Out((T, D), dtype, (tm, D), lambda i: (i, 0))])[0]


def norm_bwd(name, h, g, da, dh):
    T = h.shape[0]
    tm = _row_tile(T)
    o = Out((T, D), F32, (tm, D), lambda i: (i, 0))
    return stage_bwd(name, _norm_bwd_fn, (T // tm,), [_rows(h, tm), _const(g)], [o, o], [da, dh])


def _ffn_act_fn(pids, u, cw, cb):
    c = cw[2:3] * u + cw[1:2] * shift_down(u, 1) + cw[0:1] * shift_down(u, 2) + cb
    return (_silu(c[:, :128]) * c[:, 128:],)


def _ffn_act_args(u, cw, cb):
    T = u.shape[0]
    nt = FFN_F // 128
    args = [Arg(u, (T, 256), lambda j: (0, j)), Arg(cw, (3, 256), lambda j: (0, j)),
            Arg(cb, (1, 256), lambda j: (0, j))]
    outs = [Out((T, FFN_F), BF16, (T, 128), lambda j: (0, j))]
    return (nt,), args, outs


def _interleave_cols(w, n_parts, tile=128):
    lead = w.shape[:-1]
    n = w.shape[-1] // (n_parts * tile)
    k = len(lead)
    return w.reshape(lead + (n_parts, n, tile)).transpose(tuple(range(k)) + (k + 1, k, k + 2)).reshape(w.shape)


def _deinterleave_cols(w, n_parts, tile=128):
    lead = w.shape[:-1]
    n = w.shape[-1] // (n_parts * tile)
    k = len(lead)
    return w.reshape(lead + (n, n_parts, tile)).transpose(tuple(range(k)) + (k + 1, k, k + 2)).reshape(w.shape)


def ffn_layer(i, h, g, w_up, cw, cb, w_down):
    b = norm_fwd(f"ffn{i}_norm", h, g, BF16)
    u = matmul(f"ffn{i}_up", b, w_up, 'nn')
    grid, args, outs = _ffn_act_args(u, cw, cb)
    p = stage_fwd(f"ffn{i}_act", _ffn_act_fn, grid, args, outs)[0]
    h_new = matmul(f"ffn{i}_down", p, w_down, 'nn', res=h)

    def bwd(dh):
        dp, dwd = linear_bwd(f"ffn{i}_down_b", p, w_down, dh, mask=True)
        du, dcw, dcb = stage_bwd(f"ffn{i}_act_b", _ffn_act_fn, grid, args, outs, [dp])
        db, dwu = linear_bwd(f"ffn{i}_up_b", b, w_up, du)
        dh2, dg = norm_bwd(f"ffn{i}_norm_b", h, g, db, dh)
        return dh2, dict(g=dg, w_up=dwu, cw=dcw, cb=dcb, w_down=dwd)

    return h_new, bwd


def _mla_latent_fn(pids, down, gcq, gckv):
    cq = _rms(down[:, :MLA_QL], gcq)
    ckv = _rms(down[:, MLA_QL:MLA_QL + MLA_KVL], gckv)
    return cq, ckv, down[:, MLA_QL + MLA_KVL:]


def _rope64(x, cos, sin_signed):
    return x * cos + swap32(x) * sin_signed


def _mla_heads_fn(pids, qraw, kv, kpe, gqn, gqr, gkn, gkr, cos, sin_signed):
    qn, qr = qraw[:, :128], qraw[:, 128:]
    rq = lax.rsqrt((jnp.sum(qn * qn, -1, keepdims=True) + jnp.sum(qr * qr, -1, keepdims=True)) / MLA_QK + EPS)
    q = jnp.concatenate([qn * rq * gqn, _rope64(qr * rq * gqr, cos, sin_signed)], axis=1)
    kn, v = kv[:, :128], kv[:, 128:]
    rk = lax.rsqrt((jnp.sum(kn * kn, -1, keepdims=True) + jnp.sum(kpe * kpe, -1, keepdims=True)) / MLA_QK + EPS)
    k = jnp.concatenate([kn * rk * gkn, _rope64(kpe * rk * gkr, cos, sin_signed)], axis=1)
    return q, k, v


def _chunk_id(r):
    return jnp.where(r < LEAD, 0, 1 + lax.shift_right_arithmetic(r - LEAD, 6))


def _make_attn_fn(tq, T):
    def attn_fn(pids, q, k, v):
        s = mm_nt(q, k) * (MLA_QK ** -0.5)
        qrow = _row_ids(pids[1], tq, (tq, 1))
        krow = lax.broadcasted_iota(jnp.int32, (1, T), 1)
        ok = jnp.logical_and(_chunk_id(krow) <= _chunk_id(qrow), krow >= PAD)
        s = jnp.where(ok, s, NEG_INF)
        m = lax.stop_gradient(jnp.max(s, axis=-1, keepdims=True))
        p = jnp.exp(s - m)
        p = p / jnp.sum(p, axis=-1, keepdims=True)
        return (mm_nn(p, v),)
    return attn_fn


def mla_mixer(h, g, w, tabs):
    T = h.shape[0]
    tm = _row_tile(T)
    nt = T // tm
    a = norm_fwd("mla_norm", h, g, BF16)
    down = matmul("mla_down", a, w['w_down'], 'nn')
    lat_args = [_rows(down, tm), _const(w['gcq']), _const(w['gckv'])]
    lat_outs = [Out((T, MLA_QL), BF16, (tm, MLA_QL), lambda i: (i, 0)),
                Out((T, MLA_KVL), BF16, (tm, MLA_KVL), lambda i: (i, 0)),
                Out((T, 128), F32, (tm, 128), lambda i: (i, 0))]
    cq, ckv, kpe = stage_fwd("mla_latent", _mla_latent_fn, (nt,), lat_args, lat_outs)
    qraw = matmul("mla_uq", cq, w['w_uq'], 'nn')
    kv = matmul("mla_ukv", ckv, w['w_ukv'], 'nn')
    hd_args = [Arg(qraw, (tm, 256), lambda i, hh: (i, hh)), Arg(kv, (tm, 256), lambda i, hh: (i, hh)),
               Arg(kpe, (tm, 128), lambda i, hh: (i, 0)),
               _const(w['gqn']), _const(w['gqr']), _const(w['gkn']), _const(w['gkr']),
               Arg(tabs['cos_a'], (tm, 128), lambda i, hh: (i, 0), False),
               Arg(tabs['sin_a'], (tm, 128), lambda i, hh: (i, 0), False)]
    hd_outs = [Out((MLA_H, T, 256), BF16, (None, tm, 256), lambda i, hh: (hh, i, 0)),
               Out((MLA_H, T, 256), BF16, (None, tm, 256), lambda i, hh: (hh, i, 0)),
               Out((MLA_H, T, 128), BF16, (None, tm, 128), lambda i, hh: (hh, i, 0))]
    q, k, v = stage_fwd("mla_heads", _mla_heads_fn, (nt, MLA_H), hd_args, hd_outs)
    tq = 128
    attn_fn = _make_attn_fn(tq, T)
    at_args = [Arg(q, (None, tq, 256), lambda hh, j: (hh, j, 0)), Arg(k, (None, T, 256), lambda hh, j: (hh, 0, 0)),
               Arg(v, (None, T, 128), lambda hh, j: (hh, 0, 0))]
    at_outs = [Out((T, D), BF16, (tq, 128), lambda hh, j: (j, hh))]
    o = stage_fwd("mla_attn", attn_fn, (MLA_H, T // tq), at_args, at_outs)[0]
    h_new = matmul("mla_o", o, w['w_o'], 'nn', res=h)

    def bwd(dh):
        do, dwo = linear_bwd("mla_o_b", o, w['w_o'], dh, mask=True)
        dq, dk, dv = stage_bwd("mla_attn_b", attn_fn, (MLA_H, T // tq), at_args, at_outs, [do])
        dqraw, dkv, dkpe, dgqn, dgqr, dgkn, dgkr = stage_bwd("mla_heads_b", _mla_heads_fn, (nt, MLA_H), hd_args,
                                                             hd_outs, [dq, dk, dv])
        dcq, dwuq = linear_bwd("mla_uq_b", cq, w['w_uq'], dqraw)
        dckv, dwukv = linear_bwd("mla_ukv_b", ckv, w['w_ukv'], dkv)
        ddown, dgcq, dgckv = stage_bwd("mla_latent_b", _mla_latent_fn, (nt,), lat_args, lat_outs, [dcq, dckv, dkpe])
        da, dwdown = linear_bwd("mla_down_b", a, w['w_down'], ddown)
        dh2, dg = norm_bwd("mla_norm_b", h, g, da, dh)
        return dh2, dict(g=dg, w_down=dwdown, gcq=dgcq, gckv=dgckv, w_uq=dwuq, w_ukv=dwukv, gqn=dgqn, gqr=dgqr,
                         gkn=dgkn, gkr=dgkr, w_o=dwo)

    return h_new, bwd


HG_R = 128


def _hgrn_fn(pids, z, lb, go, st):
    R = z.shape[0]
    zq, zf, zi, zg = z[:, :128], z[:, 128:256], z[:, 256:384], z[:, 384:]
    q = _silu(zq)
    fg = lb + (1.0 - lb) * jax.nn.sigmoid(zf)
    logf = jnp.log(fg)
    k = 1.0 - fg
    r = lax.broadcasted_iota(jnp.int32, (R, R), 0)
    c = lax.broadcasted_iota(jnp.int32, (R, R), 1)
    same = lax.shift_right_arithmetic(r, 4) == lax.shift_right_arithmetic(c, 4)
    causal = jnp.logical_and(same, c <= r)
    gcum = _dot_f32(jnp.where(causal, 1.0, 0.0), logf)
    glast = _dot_f32(jnp.where(same, 1.0, 0.0), logf)
    qd = q * jnp.exp(gcum)
    ki = k * jnp.exp(-gcum)
    kt = k * jnp.exp(glast - gcum)
    o = mm_nn(jnp.where(causal, mm_nt(qd, ki), 0.0), zi)
    inter = []
    for cc in range(R // HG_C):
        lo = cc * HG_C
        inter.append(mm_nt(qd[lo:lo + HG_C], st))
        st = st * jnp.exp(glast[lo:lo + 1, :]) + mm_tn(zi[lo:lo + HG_C], kt[lo:lo + HG_C])
    o = o + jnp.concatenate(inter, axis=0)
    return _rms(o, go) * _silu(zg), st


def hgrn_mixer(h, g, w):
    T = h.shape[0]
    a = norm_fwd("hgrn_norm", h, g, BF16)
    z = matmul("hgrn_in", a, w['w_in'], 'nn')
    grid = (HG_H, T // HG_R)
    args = [Arg(z, (HG_R, 512), lambda hh, j: (j, hh)), Arg(w['lb'], (1, 128), lambda hh, j: (0, hh)),
            _const(w['go'])]
    outs = [Out((T, D), BF16, (HG_R, 128), lambda hh, j: (j, hh))]
    o, states = stage_fwd("hgrn_gla", _hgrn_fn, grid, args, outs, state_shape=(HG_D, HG_D))
    h_new = matmul("hgrn_o", o, w['w_o'], 'nn', res=h)

    def bwd(dh):
        do, dwo = linear_bwd("hgrn_o_b", o, w['w_o'], dh, mask=True)
        dz, dlb, dgo = stage_bwd("hgrn_gla_b", _hgrn_fn, grid, args, outs, [do], state_shape=(HG_D, HG_D),
                                 states=states)
        da, dwin = linear_bwd("hgrn_in_b", a, w['w_in'], dz)
        dh2, dg = norm_bwd("hgrn_norm_b", h, g, da, dh)
        return dh2, dict(g=dg, w_in=dwin, lb=dlb, go=dgo, w_o=dwo)

    return h_new, bwd


S5_R = 128
S5_W = 512
S5_SLABS = D // 128


def _cmul(ar, ai, br, bi):
    return ar * br - ai * bi, ar * bi + ai * br


def _s5_scan(br, bi, tab, cr, ci, reverse):
    R, W = br.shape
    G = R // 8
    xr, xi = br.reshape(G, 8, W), bi.reshape(G, 8, W)
    for n, d in enumerate((1, 2, 4)):
        sh = (8 - d) if reverse else d
        mr, mi = _cmul(tab[2 * n][None], tab[2 * n + 1][None], pltpu.roll(xr, sh, 1), pltpu.roll(xi, sh, 1))
        xr, xi = xr + mr, xi + mi
    pr, pi = tab[6], tab[7]
    edge = 0 if reverse else 7
    out_r, out_i = [None] * G, [None] * G
    for g in (range(G - 1, -1, -1) if reverse else range(G)):
        ar, ai = _cmul(pr, pi, cr, ci)
        gr, gi = xr[g] + ar, xi[g] + ai
        cr, ci = gr[edge:edge + 1], gi[edge:edge + 1]
        out_r[g], out_i[g] = gr, gi
    return jnp.concatenate(out_r, axis=0), jnp.concatenate(out_i, axis=0), cr, ci


def s5_scan_fwd(a, bb, cb, tab):
    T = a.shape[0]
    nb = T // S5_R

    def body(a_ref, bb_ref, cb_ref, tab_ref, y_ref, xs_ref, c_ref):
        @pl.when(pl.program_id(1) == 0)
        def _():
            c_ref[...] = jnp.zeros(c_ref.shape, F32)

        bu = _dg(a_ref[...], bb_ref[...], 1, 0)
        t = tab_ref[...]
        xr, xi, cr, ci = _s5_scan(bu[:, :S5_W], bu[:, S5_W:], t, c_ref[0:1, :S5_W], c_ref[0:1, S5_W:], False)
        x = jnp.concatenate([xr, xi], axis=1)
        xs_ref[...] = x
        y_ref[...] = _dg(x, cb_ref[...], 1, 0)
        c_ref[0:1, :] = jnp.concatenate([cr, ci], axis=1)

    return pl.pallas_call(
        body, grid=(S5_SLABS, nb),
        in_specs=[pl.BlockSpec((S5_R, 128), lambda j, i: (i, j)),
                  pl.BlockSpec((None, 128, 2 * S5_W), lambda j, i: (j, 0, 0)),
                  pl.BlockSpec((None, 2 * S5_W, 128), lambda j, i: (j, 0, 0)),
                  pl.BlockSpec((None, 10, 8, S5_W), lambda j, i: (j, 0, 0, 0))],
        out_specs=[pl.BlockSpec((S5_R, 128), lambda j, i: (i, j)),
                   pl.BlockSpec((None, S5_R, 2 * S5_W), lambda j, i: (j, i, 0))],
        out_shape=[jax.ShapeDtypeStruct((T, D), F32), jax.ShapeDtypeStruct((S5_SLABS, T, 2 * S5_W), F32)],
        scratch_shapes=[pltpu.VMEM((8, 2 * S5_W), F32)], name="s5_scan", compiler_params=_cparams())(a, bb, cb, tab)


def s5_scan_bwd(a, bb, cb, tab_rev, xs, dy):
    T = a.shape[0]
    nb = T // S5_R
    rg = S5_R // 8

    def body(a_ref, dy_ref, xs_ref, xp_ref, bb_ref, cb_ref, tab_ref, da_ref, dbb_ref, dcb_ref, dab_ref, c_ref):
        i = pl.program_id(1)

        @pl.when(i == 0)
        def _():
            c_ref[...] = jnp.zeros(c_ref.shape, F32)

        dy_v = dy_ref[...]
        x = xs_ref[...]
        dxo = _dg(dy_v, cb_ref[...], 1, 1)
        gr, gi, cr, ci = _s5_scan(dxo[:, :S5_W], dxo[:, S5_W:], tab_ref[...], c_ref[0:1, :S5_W], c_ref[0:1, S5_W:], True)
        c_ref[0:1, :] = jnp.concatenate([cr, ci], axis=1)
        g = jnp.concatenate([gr, gi], axis=1)
        da_ref[...] = _dg(g, bb_ref[...], 1, 1)
        dbb = _dg(a_ref[...], g, 0, 0)
        dcb = _dg(x, dy_v, 0, 0)
        first_tile = i == nb - 1
        prev_last = jnp.where(first_tile, 0.0, xp_ref[7:8, :])
        rows = lax.broadcasted_iota(jnp.int32, x.shape, 0)
        xp = jnp.where(rows == 0, prev_last, pltpu.roll(x, 1, 0))
        xpr, xpi = xp[:, :S5_W], xp[:, S5_W:]
        dar = (gr * xpr + gi * xpi).reshape(rg, 8, S5_W).sum(axis=0)
        dai = (gi * xpr - gr * xpi).reshape(rg, 8, S5_W).sum(axis=0)
        dab = jnp.concatenate([dar, dai], axis=1)

        @pl.when(i == 0)
        def _():
            dbb_ref[...] = dbb
            dcb_ref[...] = dcb
            dab_ref[...] = dab

        @pl.when(i != 0)
        def _():
            dbb_ref[...] += dbb
            dcb_ref[...] += dcb
            dab_ref[...] += dab

    def prev_rows(j, i):
        return (j, jnp.maximum((nb - 1 - i) * rg - 1, 0), 0)

    return pl.pallas_call(
        body, grid=(S5_SLABS, nb),
        in_specs=[pl.BlockSpec((S5_R, 128), lambda j, i: (nb - 1 - i, j)),
                  pl.BlockSpec((S5_R, 128), lambda j, i: (nb - 1 - i, j)),
                  pl.BlockSpec((None, S5_R, 2 * S5_W), lambda j, i: (j, nb - 1 - i, 0)),
                  pl.BlockSpec((None, 8, 2 * S5_W), prev_rows),
                  pl.BlockSpec((None, 128, 2 * S5_W), lambda j, i: (j, 0, 0)),
                  pl.BlockSpec((None, 2 * S5_W, 128), lambda j, i: (j, 0, 0)),
                  pl.BlockSpec((None, 10, 8, S5_W), lambda j, i: (j, 0, 0, 0))],
        out_specs=[pl.BlockSpec((S5_R, 128), lambda j, i: (nb - 1 - i, j)),
                   pl.BlockSpec((None, 128, 2 * S5_W), lambda j, i: (j, 0, 0)),
                   pl.BlockSpec((None, 2 * S5_W, 128), lambda j, i: (j, 0, 0)),
                   pl.BlockSpec((None, 8, 2 * S5_W), lambda j, i: (j, 0, 0))],
        out_shape=[jax.ShapeDtypeStruct((T, D), F32), jax.ShapeDtypeStruct((S5_SLABS, 128, 2 * S5_W), F32),
                   jax.ShapeDtypeStruct((S5_SLABS, 2 * S5_W, 128), F32),
                   jax.ShapeDtypeStruct((S5_SLABS, 8, 2 * S5_W), F32)],
        scratch_shapes=[pltpu.VMEM((8, 2 * S5_W), F32)], name="s5_scan_b",
        compiler_params=_cparams())(a, dy, xs, xs, bb, cb, tab_rev)


def _s5_discretise(lam_re, lam_im, log_dt, b_re, b_im, c_re, c_im):
    dt = jnp.exp(log_dt)[:, None]
    mag = jnp.exp(lam_re * dt)
    abar_re = mag * jnp.cos(lam_im * dt)
    abar_im = mag * jnp.sin(lam_im * dt)
    den = lam_re * lam_re + lam_im * lam_im
    zoh_re = ((abar_re - 1.0) * lam_re + abar_im * lam_im) / den
    zoh_im = (abar_im * lam_re - (abar_re - 1.0) * lam_im) / den
    bbar_re = zoh_re[..., None] * b_re - zoh_im[..., None] * b_im
    bbar_im = zoh_re[..., None] * b_im + zoh_im[..., None] * b_re
    eye = jnp.eye(8, dtype=F32)

    def in_map(bbar):
        t = bbar.reshape(8, 8, S5_P, S5_K).transpose(0, 1, 3, 2)
        return (t[:, :, :, None, :] * eye[None, :, None, :, None]).reshape(8, 8 * S5_K, 8 * S5_P)

    def out_map(c):
        t = c.reshape(8, 8, S5_K, S5_P).transpose(0, 1, 3, 2)
        return (t[:, :, :, None, :] * eye[None, :, None, :, None]).reshape(8, 8 * S5_P, 8 * S5_K)

    bb = jnp.concatenate([in_map(bbar_re), in_map(bbar_im)], axis=2)
    cb = jnp.concatenate([out_map(c_re), -out_map(c_im)], axis=1)
    return bb, cb, abar_re.reshape(8, S5_W), abar_im.reshape(8, S5_W)


def _s5_tables(ar, ai, reverse):
    if reverse:
        ai = -ai
    pw = [(jnp.ones_like(ar), jnp.zeros_like(ar))]
    for _ in range(8):
        pw.append(_cmul(pw[-1][0], pw[-1][1], ar, ai))
    r = jnp.arange(8)[None, :, None]
    rows = []
    for d in (1, 2, 4):
        keep = (r <= 7 - d) if reverse else (r >= d)
        rows += [jnp.where(keep, pw[d][0][:, None, :], 0.0), jnp.where(keep, pw[d][1][:, None, :], 0.0)]
    order = [8 - k for k in range(8)] if reverse else [k + 1 for k in range(8)]
    rows += [jnp.stack([pw[n][0] for n in order], axis=1), jnp.stack([pw[n][1] for n in order], axis=1)]
    rows += [jnp.broadcast_to(pw[8][0][:, None, :], (8, 8, S5_W)), jnp.broadcast_to(pw[8][1][:, None, :], (8, 8, S5_W))]
    return jnp.stack(rows, axis=1)


def _s5_act_fn(pids, yc, a, dskip):
    return (jax.nn.gelu(yc + dskip * a),)


def _make_glu_res_fn(tm):
    def glu_res_fn(pids, zz, h):
        rows = _row_ids(pids[0], tm, (tm, 1))
        return (h + jnp.where(rows >= PAD, zz[:, :D] * jax.nn.sigmoid(zz[:, D:]), 0.0),)
    return glu_res_fn


def s5_mixer(h, g, w):
    T = h.shape[0]
    tm = _row_tile(T)
    nt = T // tm
    a = norm_fwd("s5_norm", h, g, F32)
    ssm = [w[n] for n in ('lam_re', 'lam_im', 'log_dt', 'b_re', 'b_im', 'c_re', 'c_im')]
    (bb, cb, ar, ai), disc_vjp = jax.vjp(_s5_discretise, *ssm)
    yc, xs = s5_scan_fwd(a, bb, cb, _s5_tables(ar, ai, False))
    row = lambda arr: _rows(arr, tm)
    act_args = [row(yc), row(a), _const(w['dskip'])]
    act_outs = [Out((T, D), BF16, (tm, D), lambda i: (i, 0))]
    y = stage_fwd("s5_act", _s5_act_fn, (nt,), act_args, act_outs)[0]
    zz = matmul("s5_glu", y, w['w_glu'], 'nn')
    glu_fn = _make_glu_res_fn(tm)
    glu_args = [row(zz), row(h)]
    glu_outs = [Out((T, D), F32, (tm, D), lambda i: (i, 0))]
    h_new = stage_fwd("s5_gate", glu_fn, (nt,), glu_args, glu_outs)[0]

    def bwd(dh):
        dzz, dh_res = stage_bwd("s5_gate_b", glu_fn, (nt,), glu_args, glu_outs, [dh])
        dy, dwglu = linear_bwd("s5_glu_b", y, w['w_glu'], dzz)
        dyc, da1, ddskip = stage_bwd("s5_act_b", _s5_act_fn, (nt,), act_args, act_outs, [dy])
        da2, dbb, dcb, dab = s5_scan_bwd(a, bb, cb, _s5_tables(ar, ai, True), xs, dyc)
        dab = dab.sum(axis=1)
        dssm = disc_vjp((dbb, dcb, dab[:, :S5_W], dab[:, S5_W:]))
        dh2, dg = _s5_norm_bwd(h, g, da1, da2, dh_res, tm)
        grads = dict(zip(('lam_re', 'lam_im', 'log_dt', 'b_re', 'b_im', 'c_re', 'c_im'), dssm))
        grads.update(g=dg, dskip=ddskip, w_glu=dwglu)
        return dh2, grads

    return h_new, bwd


def _norm3_bwd_fn(pids, h, g):
    a = _rms(h, g)
    return a, a, h


def _s5_norm_bwd(h, g, da1, da2, dh, tm):
    T = h.shape[0]
    o = Out((T, D), F32, (tm, D), lambda i: (i, 0))
    return stage_bwd("s5_norm_b", _norm3_bwd_fn, (T // tm,), [_rows(h, tm), _const(g)], [o, o, o], [da1, da2, dh])


RET_R = 128


def _rope256(x, cos, sin):
    x1, x2 = x[:, :128], x[:, 128:]
    return jnp.concatenate([x1 * cos - x2 * sin, x1 * sin + x2 * cos], axis=1)


def _ret_fn(pids, z, gn, cos, sin, dmat, qdec, kdec, cdec, st):
    R = z.shape[0]
    q = _rope256(z[:, :256], cos, sin)
    k = _rope256(z[:, 256:512], cos, sin) * (RET_DK ** -0.5)
    v, gate = z[:, 512:1024], z[:, 1024:]
    outs = []
    for cc in range(R // CHUNK):
        lo = cc * CHUNK
        qc, kc, vc = q[lo:lo + CHUNK], k[lo:lo + CHUNK], v[lo:lo + CHUNK]
        outs.append(mm_nn(mm_nt(qc, kc) * dmat, vc) + mm_nn(qc * qdec, st))
        st = st * cdec + mm_tn(kc * kdec, vc)
    o = jnp.concatenate(outs, axis=0)
    mu = jnp.mean(o, axis=-1, keepdims=True)
    var = jnp.mean(jnp.square(o - mu), axis=-1, keepdims=True)
    o = (o - mu) * lax.rsqrt(var + EPS)
    return o * gn * _silu(gate), st


def ret_mixer(h, g, w, tabs):
    T = h.shape[0]
    a = norm_fwd("ret_norm", h, g, BF16)
    z = matmul("ret_in", a, w['w_in'], 'nn')
    grid = (RET_H, T // RET_R)
    hw = RET_DK * 2 + RET_DV * 2
    args = [Arg(z, (RET_R, hw), lambda hh, j: (j, hh)), Arg(w['gn'], (1, RET_DV), lambda hh, j: (0, hh)),
            Arg(tabs['cos_d'], (RET_R, 128), lambda hh, j: (j, 0), False),
            Arg(tabs['sin_d'], (RET_R, 128), lambda hh, j: (j, 0), False),
            Arg(tabs['ret_dmat'], (None, CHUNK, CHUNK), lambda hh, j: (hh, 0, 0), False),
            Arg(tabs['ret_qdec'], (None, CHUNK, 1), lambda hh, j: (hh, 0, 0), False),
            Arg(tabs['ret_kdec'], (None, CHUNK, 1), lambda hh, j: (hh, 0, 0), False),
            Arg(tabs['ret_cdec'], (None, 1, 1), lambda hh, j: (hh, 0, 0), False)]
    outs = [Out((T, RET_H * RET_DV), BF16, (RET_R, RET_DV), lambda hh, j: (j, hh))]
    o, states = stage_fwd("ret_chunks", _ret_fn, grid, args, outs, state_shape=(RET_DK, RET_DV))
    h_new = matmul("ret_o", o, w['w_o'], 'nn', res=h)

    def bwd(dh):
        do, dwo = linear_bwd("ret_o_b", o, w['w_o'], dh, mask=True)
        dz, dgn = stage_bwd("ret_chunks_b", _ret_fn, grid, args, outs, [do], state_shape=(RET_DK, RET_DV),
                            states=states)
        da, dwin = linear_bwd("ret_in_b", a, w['w_in'], dz)
        dh2, dg = norm_bwd("ret_norm_b", h, g, da, dh)
        return dh2, dict(g=dg, w_in=dwin, gn=dgn, w_o=dwo)

    return h_new, bwd


def loss_head(h, tgt):
    T = h.shape[0]
    tm = _row_tile(T)

    def body(h_ref, t_ref, loss_ref, dh_ref):
        i = pl.program_id(0)
        rows = _row_ids(i, tm, (tm, 1))
        err = jnp.where(rows >= LEAD, h_ref[...] - t_ref[...], 0.0)
        dh_ref[...] = err * (1.0 / D)
        part = jnp.full((8, 128), 0.5 * jnp.sum(jnp.sum(err * err, axis=1, keepdims=True) * (1.0 / D)), F32)

        @pl.when(i == 0)
        def _():
            loss_ref[...] = part

        @pl.when(i != 0)
        def _():
            loss_ref[...] += part

    loss, dh = pl.pallas_call(
        body, grid=(T // tm,),
        in_specs=[pl.BlockSpec((tm, D), lambda i: (i, 0)), pl.BlockSpec((tm, D), lambda i: (i, 0))],
        out_specs=[pl.BlockSpec((8, 128), lambda i: (0, 0)), pl.BlockSpec((tm, D), lambda i: (i, 0))],
        out_shape=[jax.ShapeDtypeStruct((8, 128), F32), jax.ShapeDtypeStruct((T, D), F32)], name="loss_head",
        compiler_params=_cparams())(h, tgt)
    return loss[0, 0], dh


def _tables(T):
    pos = jnp.maximum(jnp.arange(T, dtype=jnp.int32) - PAD, 0).astype(F32)

    def cs(dim):
        inv_freq = 1.0 / (10000.0 ** (jnp.arange(0, dim, 2, dtype=F32) / dim))
        ang = pos[:, None] * inv_freq[None, :]
        return jnp.cos(ang), jnp.sin(ang)

    ca, sa = cs(MLA_ROPE)
    zeros = jnp.zeros((T, 64), F32)
    cd, sd = cs(RET_DK)
    log_gamma = jnp.log(1.0 - jnp.exp2(-5.0 - jnp.arange(RET_H, dtype=F32)))
    p = jnp.arange(CHUNK, dtype=F32)
    diff = p[:, None] - p[None, :]
    dmat = jnp.where(diff >= 0, jnp.exp(diff[None] * log_gamma[:, None, None]), 0.0)
    return dict(cos_a=jnp.concatenate([ca, ca, zeros], axis=1), sin_a=jnp.concatenate([-sa, sa, zeros], axis=1),
                cos_d=cd, sin_d=sd, ret_dmat=dmat,
                ret_qdec=jnp.exp((p[None, :] + 1.0) * log_gamma[:, None])[..., None],
                ret_kdec=jnp.exp((CHUNK - 1.0 - p[None, :]) * log_gamma[:, None])[..., None],
                ret_cdec=jnp.exp(CHUNK * log_gamma)[:, None, None])


def _hgrn_lower_bound(logits):
    lb_cum = jnp.cumsum(jax.nn.softmax(logits, axis=0), axis=0)
    return (lb_cum - lb_cum[0:1])[1:2]


def _uq_to_heads(w):
    t = w.reshape(w.shape[0], MLA_H, MLA_QK)
    return jnp.pad(t, ((0, 0), (0, 0), (0, 256 - MLA_QK))).reshape(w.shape[0], MLA_H * 256)


def _uq_from_heads(g):
    return g.reshape(g.shape[0], MLA_H, 256)[:, :, :MLA_QK].reshape(g.shape[0], MLA_H * MLA_QK)


def _head_interleave(w, widths, heads):
    parts, lo = [], 0
    for wd in widths:
        parts.append(w[:, lo:lo + heads * wd].reshape(w.shape[0], heads, wd))
        lo += heads * wd
    return jnp.concatenate(parts, axis=2).reshape(w.shape[0], -1)


def _head_deinterleave(g, widths, heads):
    t = g.reshape(g.shape[0], heads, sum(widths))
    parts, lo = [], 0
    for wd in widths:
        parts.append(t[:, :, lo:lo + wd].reshape(g.shape[0], heads * wd))
        lo += wd
    return jnp.concatenate(parts, axis=1)


HG_WIDTHS = (128, 128, 128, 128)
RET_WIDTHS = (RET_DK, RET_DK, RET_DV, RET_DV)


def _split_head_gain(g):
    return g[:, :128], jnp.pad(g[:, 128:], ((0, 0), (0, 64)))


def _join_head_gain(dn, dr):
    return jnp.concatenate([dn, dr[:, :64]], axis=1)


def local_step(x, target, W):
    S = x.shape[0]
    T = S + LEAD
    tabs = _tables(T)
    h = jnp.concatenate([jnp.zeros((PAD, D), F32), W['meta_tokens'], x], axis=0)
    tgt = jnp.concatenate([jnp.zeros((LEAD, D), F32), target], axis=0)
    bwds = []

    gqn, gqr = _split_head_gain(W['mla_q_head_g'])
    gkn, gkr = _split_head_gain(W['mla_k_head_g'])
    lb, lb_vjp = jax.vjp(_hgrn_lower_bound, W['hgrn_lb_logits'])
    mix_w = [
        dict(w_down=jnp.pad(W['mla_w_down'][0], ((0, 0), (0, 64))), gcq=W['mla_cq_norm_g'], gckv=W['mla_ckv_norm_g'],
             w_uq=_uq_to_heads(W['mla_w_uq'][0]), w_ukv=W['mla_w_ukv'][0], gqn=gqn, gqr=gqr, gkn=gkn, gkr=gkr,
             w_o=W['mla_w_o'][0]),
        dict(w_in=_head_interleave(W['hgrn_w_in'][0], HG_WIDTHS, HG_H), lb=lb, go=W['hgrn_o_norm_g'],
             w_o=W['hgrn_w_o'][0]),
        dict(lam_re=W['s5_lam_re'][0], lam_im=W['s5_lam_im'][0], log_dt=W['s5_log_dt'][0], b_re=W['s5_b_re'][0],
             b_im=W['s5_b_im'][0], c_re=W['s5_c_re'][0], c_im=W['s5_c_im'][0], dskip=W['s5_d'], w_glu=W['s5_w_glu'][0]),
        dict(w_in=_head_interleave(W['ret_w_in'][0], RET_WIDTHS, RET_H), gn=W['ret_gn_g'], w_o=W['ret_w_o'][0]),
    ]
    for i in range(4):
        g_mix = W['norm_mix_g'][i:i + 1]
        if i == 0:
            h, b = mla_mixer(h, g_mix, mix_w[0], tabs)
        elif i == 1:
            h, b = hgrn_mixer(h, g_mix, mix_w[1])
        elif i == 2:
            h, b = s5_mixer(h, g_mix, mix_w[2])
        else:
            h, b = ret_mixer(h, g_mix, mix_w[3], tabs)
        bwds.append(b)
        h, b = ffn_layer(i, h, W['norm_ffn_g'][i:i + 1], _interleave_cols(W['ffn_w_up'][i], 2),
                         _interleave_cols(W['ffn_conv_w'][i], 2), _interleave_cols(W['ffn_conv_b'][i:i + 1], 2),
                         W['ffn_w_down'][i])
        bwds.append(b)

    loss, dh = loss_head(h, tgt)
    gm, gf = [None] * 4, [None] * 4
    for i in range(3, -1, -1):
        dh, gf[i] = bwds[2 * i + 1](dh)
        dh, gm[i] = bwds[2 * i](dh)

    G = {}
    G['meta_tokens'] = dh[PAD:LEAD]
    G['norm_mix_g'] = jnp.concatenate([gm[i]['g'] for i in range(4)], axis=0)
    G['norm_ffn_g'] = jnp.concatenate([gf[i]['g'] for i in range(4)], axis=0)
    a = gm[0]
    G['mla_w_down'] = a['w_down'][None, :, :MLA_QL + MLA_KVL + MLA_ROPE]
    G['mla_cq_norm_g'], G['mla_ckv_norm_g'] = a['gcq'], a['gckv']
    G['mla_w_uq'] = _uq_from_heads(a['w_uq'])[None]
    G['mla_w_ukv'] = a['w_ukv'][None]
    G['mla_q_head_g'] = _join_head_gain(a['gqn'], a['gqr'])
    G['mla_k_head_g'] = _join_head_gain(a['gkn'], a['gkr'])
    G['mla_w_o'] = a['w_o'][None]
    b = gm[1]
    G['hgrn_w_in'] = _head_deinterleave(b['w_in'], HG_WIDTHS, HG_H)[None]
    G['hgrn_lb_logits'] = lb_vjp(b['lb'])[0]
    G['hgrn_o_norm_g'] = b['go']
    G['hgrn_w_o'] = b['w_o'][None]
    c = gm[2]
    for n in ('lam_re', 'lam_im', 'log_dt', 'b_re', 'b_im', 'c_re', 'c_im'):
        G['s5_' + n] = c[n][None]
    G['s5_d'] = c['dskip']
    G['s5_w_glu'] = c['w_glu'][None]
    d = gm[3]
    G['ret_w_in'] = _head_deinterleave(d['w_in'], RET_WIDTHS, RET_H)[None]
    G['ret_gn_g'] = d['gn']
    G['ret_w_o'] = d['w_o'][None]
    G['ffn_w_up'] = jnp.stack([_deinterleave_cols(gf[i]['w_up'], 2) for i in range(4)])
    G['ffn_conv_w'] = jnp.stack([_deinterleave_cols(gf[i]['cw'], 2) for i in range(4)])
    G['ffn_conv_b'] = jnp.concatenate([_deinterleave_cols(gf[i]['cb'], 2) for i in range(4)], axis=0)
    G['ffn_w_down'] = jnp.stack([gf[i]['w_down'] for i in range(4)])
    return loss, dh[LEAD:], G


PACK_W = 1024
ANY = pl.BlockSpec(memory_space=pl.ANY)


def _pack(arrs, dtype, row_mult):
    flat = jnp.concatenate([a.reshape(-1).astype(dtype) for a in arrs])
    n = flat.shape[0]
    rows = -(-n // (PACK_W * row_mult)) * row_mult
    return jnp.pad(flat, (0, rows * PACK_W - n)).reshape(rows, PACK_W)


def _unpack(buf, shapes):
    flat = buf.reshape(-1)
    out, off = [], 0
    for s in shapes:
        n = math.prod(s)
        out.append(flat[off:off + n].reshape(s))
        off += n
    return out


def _my_pos():
    return lax.axis_index("x"), lax.axis_index("y"), lax.axis_index("c")


def _other_chips(x, y):
    return [(1 - x, y), (x, 1 - y), (1 - x, 1 - y)]


def gather_chips(name, src):
    def body(src_ref, out_ref, send_sems, recv_sems, local_sem):
        x, y, c = _my_pos()
        q = 2 * x + y
        mine = pltpu.make_async_copy(src_ref, out_ref.at[q], local_sem)
        mine.start()
        peers = _other_chips(x, y)

        def copy(k, slot, peer):
            return pltpu.make_async_remote_copy(src_ref=src_ref, dst_ref=out_ref.at[slot], send_sem=send_sems.at[k],
                                                recv_sem=recv_sems.at[k], device_id=(peer[0], peer[1], c),
                                                device_id_type=MESH_ID)
        sends = [copy(k, q, p) for k, p in enumerate(peers)]
        for cp in sends:
            cp.start()
        for k, p in enumerate(peers):
            copy(k, 2 * p[0] + p[1], p).wait_recv()
        for cp in sends:
            cp.wait_send()
        mine.wait()

    return pl.pallas_call(body, out_shape=jax.ShapeDtypeStruct((4,) + src.shape, src.dtype), in_specs=[ANY],
                          out_specs=ANY, name=name,
                          scratch_shapes=[pltpu.SemaphoreType.DMA((3,)), pltpu.SemaphoreType.DMA((3,)),
                                          pltpu.SemaphoreType.DMA(())])(src)


def scatter_chips(name, src):
    def body(src_ref, out_ref, send_sems, recv_sems, local_sem):
        x, y, c = _my_pos()
        q = 2 * x + y
        mine = pltpu.make_async_copy(src_ref.at[q], out_ref.at[q], local_sem)
        mine.start()
        peers = _other_chips(x, y)

        def copy(k, peer):
            slot = 2 * peer[0] + peer[1]
            return pltpu.make_async_remote_copy(src_ref=src_ref.at[slot], dst_ref=out_ref.at[q], send_sem=send_sems.at[k],
                                                recv_sem=recv_sems.at[k], device_id=(peer[0], peer[1], c),
                                                device_id_type=MESH_ID)

        def landing(k, peer):
            slot = 2 * peer[0] + peer[1]
            return pltpu.make_async_remote_copy(src_ref=src_ref.at[slot], dst_ref=out_ref.at[slot],
                                                send_sem=send_sems.at[k], recv_sem=recv_sems.at[k],
                                                device_id=(peer[0], peer[1], c), device_id_type=MESH_ID)
        sends = [copy(k, p) for k, p in enumerate(peers)]
        for cp in sends:
            cp.start()
        for k, p in enumerate(peers):
            landing(k, p).wait_recv()
        for cp in sends:
            cp.wait_send()
        mine.wait()

    return pl.pallas_call(body, out_shape=jax.ShapeDtypeStruct(src.shape, src.dtype), in_specs=[ANY], out_specs=ANY,
                          name=name, scratch_shapes=[pltpu.SemaphoreType.DMA((3,)), pltpu.SemaphoreType.DMA((3,)),
                                                     pltpu.SemaphoreType.DMA(())])(src)


def swap_sibling(name, src):
    def body(src_ref, out_ref, send_sem, recv_sem):
        x, y, c = _my_pos()
        cp = pltpu.make_async_remote_copy(src_ref=src_ref, dst_ref=out_ref, send_sem=send_sem, recv_sem=recv_sem,
                                          device_id=(x, y, 1 - c), device_id_type=MESH_ID)
        cp.start()
        cp.wait()

    return pl.pallas_call(body, out_shape=jax.ShapeDtypeStruct(src.shape, src.dtype), in_specs=[ANY], out_specs=ANY,
                          name=name, scratch_shapes=[pltpu.SemaphoreType.DMA(()), pltpu.SemaphoreType.DMA(())])(src)


def gather_all(name, src):
    def body(src_ref, out_ref, send_sems, recv_sems, local_sem):
        x, y, c = _my_pos()
        me = 4 * x + 2 * y + c
        mine = pltpu.make_async_copy(src_ref, out_ref.at[me], local_sem)
        mine.start()
        peers = [((1 - x) if m & 4 else x, (1 - y) if m & 2 else y, (1 - c) if m & 1 else c) for m in range(1, 8)]

        def copy(k, slot, peer):
            return pltpu.make_async_remote_copy(src_ref=src_ref, dst_ref=out_ref.at[slot], send_sem=send_sems.at[k],
                                                recv_sem=recv_sems.at[k], device_id=peer, device_id_type=MESH_ID)
        sends = [copy(k, me, p) for k, p in enumerate(peers)]
        for cp in sends:
            cp.start()
        for k, p in enumerate(peers):
            copy(k, 4 * p[0] + 2 * p[1] + p[2], p).wait_recv()
        for cp in sends:
            cp.wait_send()
        mine.wait()

    return pl.pallas_call(body, out_shape=jax.ShapeDtypeStruct((8,) + src.shape, src.dtype), in_specs=[ANY],
                          out_specs=ANY, name=name,
                          scratch_shapes=[pltpu.SemaphoreType.DMA((7,)), pltpu.SemaphoreType.DMA((7,)),
                                          pltpu.SemaphoreType.DMA(())])(src)


def _pack_tile(rows):
    return _divisors(rows, (256, 128, 64, 32, 16, 8))[0] if rows > 512 else rows


def sum_slots(name, slots):
    n, rows, w = slots.shape
    tr = _pack_tile(rows)

    def body(s_ref, o_ref):
        acc = s_ref[0].astype(F32)
        for k in range(1, n):
            acc = acc + s_ref[k].astype(F32)
        o_ref[...] = acc

    return pl.pallas_call(body, grid=(rows // tr,), in_specs=[pl.BlockSpec((n, tr, w), lambda i: (0, i, 0))],
                          out_specs=pl.BlockSpec((tr, w), lambda i: (i, 0)),
                          out_shape=jax.ShapeDtypeStruct((rows, w), F32), name=name, compiler_params=_cparams())(slots)


def adamw(name, grads, w, m, v):
    rows, wd = w.shape
    tr = _pack_tile(rows)
    ng = len(grads)

    def body(*refs):
        g = refs[0][...]
        for r in refs[1:ng]:
            g = g + r[...]
        w_ref, m_ref, v_ref = refs[ng:ng + 3]
        g_out, d_out, m_out, v_out = refs[ng + 3:]
        m_new = ADAM_B1 * m_ref[...] + (1.0 - ADAM_B1) * g
        v_new = ADAM_B2 * v_ref[...] + (1.0 - ADAM_B2) * jnp.square(g)
        m_hat = m_new / (1.0 - ADAM_B1 ** ADAM_STEP)
        v_hat = v_new / (1.0 - ADAM_B2 ** ADAM_STEP)
        g_out[...] = g
        d_out[...] = -ADAM_LR * (m_hat / (jnp.sqrt(v_hat) + ADAM_EPS) + ADAM_WD * w_ref[...])
        m_out[...] = m_new
        v_out[...] = v_new

    spec = pl.BlockSpec((tr, wd), lambda i: (i, 0))
    shape = jax.ShapeDtypeStruct((rows, wd), F32)
    return pl.pallas_call(body, grid=(rows // tr,), in_specs=[spec] * (ng + 3), out_specs=[spec] * 4,
                          out_shape=[shape] * 4, name=name, compiler_params=_cparams())(*grads, w, m, v)


BIG_ROW_MULT = 256


def _shard(a, name, p):
    ax = SHARD_AXIS[name]
    n = a.shape[ax] // 4
    return lax.slice_in_dim(a, p * n, (p + 1) * n, axis=ax)


def kernel(x, meta_tokens, norm_mix_g, norm_ffn_g, mla_w_down, mla_cq_norm_g, mla_ckv_norm_g, mla_w_uq, mla_w_ukv, mla_q_head_g, mla_k_head_g, mla_w_o, hgrn_w_in, hgrn_lb_logits, hgrn_o_norm_g, hgrn_w_o, s5_lam_re, s5_lam_im, s5_log_dt, s5_b_re, s5_b_im, s5_c_re, s5_c_im, s5_d, s5_w_glu, ret_w_in, ret_gn_g, ret_w_o, ffn_w_up, ffn_conv_w, ffn_conv_b, ffn_w_down, loss_target, m_meta_tokens, m_norm_mix_g, m_norm_ffn_g, m_mla_w_down, m_mla_cq_norm_g, m_mla_ckv_norm_g, m_mla_w_uq, m_mla_w_ukv, m_mla_q_head_g, m_mla_k_head_g, m_mla_w_o, m_hgrn_w_in, m_hgrn_lb_logits, m_hgrn_o_norm_g, m_hgrn_w_o, m_s5_lam_re, m_s5_lam_im, m_s5_log_dt, m_s5_b_re, m_s5_b_im, m_s5_c_re, m_s5_c_im, m_s5_d, m_s5_w_glu, m_ret_w_in, m_ret_gn_g, m_ret_w_o, m_ffn_w_up, m_ffn_conv_w, m_ffn_conv_b, m_ffn_w_down, v_meta_tokens, v_norm_mix_g, v_norm_ffn_g, v_mla_w_down, v_mla_cq_norm_g, v_mla_ckv_norm_g, v_mla_w_uq, v_mla_w_ukv, v_mla_q_head_g, v_mla_k_head_g, v_mla_w_o, v_hgrn_w_in, v_hgrn_lb_logits, v_hgrn_o_norm_g, v_hgrn_w_o, v_s5_lam_re, v_s5_lam_im, v_s5_log_dt, v_s5_b_re, v_s5_b_im, v_s5_c_re, v_s5_c_im, v_s5_d, v_s5_w_glu, v_ret_w_in, v_ret_gn_g, v_ret_w_o, v_ffn_w_up, v_ffn_conv_w, v_ffn_conv_b, v_ffn_w_down):
    vals = (x, meta_tokens, norm_mix_g, norm_ffn_g, mla_w_down, mla_cq_norm_g, mla_ckv_norm_g, mla_w_uq, mla_w_ukv, mla_q_head_g, mla_k_head_g, mla_w_o, hgrn_w_in, hgrn_lb_logits, hgrn_o_norm_g, hgrn_w_o, s5_lam_re, s5_lam_im, s5_log_dt, s5_b_re, s5_b_im, s5_c_re, s5_c_im, s5_d, s5_w_glu, ret_w_in, ret_gn_g, ret_w_o, ffn_w_up, ffn_conv_w, ffn_conv_b, ffn_w_down, loss_target, m_meta_tokens, m_norm_mix_g, m_norm_ffn_g, m_mla_w_down, m_mla_cq_norm_g, m_mla_ckv_norm_g, m_mla_w_uq, m_mla_w_ukv, m_mla_q_head_g, m_mla_k_head_g, m_mla_w_o, m_hgrn_w_in, m_hgrn_lb_logits, m_hgrn_o_norm_g, m_hgrn_w_o, m_s5_lam_re, m_s5_lam_im, m_s5_log_dt, m_s5_b_re, m_s5_b_im, m_s5_c_re, m_s5_c_im, m_s5_d, m_s5_w_glu, m_ret_w_in, m_ret_gn_g, m_ret_w_o, m_ffn_w_up, m_ffn_conv_w, m_ffn_conv_b, m_ffn_w_down, v_meta_tokens, v_norm_mix_g, v_norm_ffn_g, v_mla_w_down, v_mla_cq_norm_g, v_mla_ckv_norm_g, v_mla_w_uq, v_mla_w_ukv, v_mla_q_head_g, v_mla_k_head_g, v_mla_w_o, v_hgrn_w_in, v_hgrn_lb_logits, v_hgrn_o_norm_g, v_hgrn_w_o, v_s5_lam_re, v_s5_lam_im, v_s5_log_dt, v_s5_b_re, v_s5_b_im, v_s5_c_re, v_s5_c_im, v_s5_d, v_s5_w_glu, v_ret_w_in, v_ret_gn_g, v_ret_w_o, v_ffn_w_up, v_ffn_conv_w, v_ffn_conv_b, v_ffn_w_down)
    names = ['x'] + WEIGHTS + ['loss_target'] + ['m_' + n for n in WEIGHTS] + ['v_' + n for n in WEIGHTS]
    A = dict(zip(names, vals))
    q = 2 * lax.axis_index("x") + lax.axis_index("y")

    big_shapes = [A[n].shape for n in BIG]
    small_shapes = [A[n].shape for n in SMALL_SHARDED]
    got_big = gather_chips("gather_big", _pack([A[n] for n in BIG], BF16, BIG_ROW_MULT))
    got_small = gather_chips("gather_small", _pack([A[n] for n in SMALL_SHARDED], F32, 8))
    W = {n: A[n] for n in REPLICATED}
    parts_big = [_unpack(got_big[p], big_shapes) for p in range(4)]
    parts_small = [_unpack(got_small[p], small_shapes) for p in range(4)]
    for k, n in enumerate(BIG):
        W[n] = jnp.concatenate([parts_big[p][k] for p in range(4)], axis=SHARD_AXIS[n])
    for k, n in enumerate(SMALL_SHARDED):
        W[n] = jnp.concatenate([parts_small[p][k] for p in range(4)], axis=SHARD_AXIS[n])

    loss, grad_x, G = local_step(A['x'][0], A['loss_target'][0], W)
    loss = lax.psum(loss, ("x", "y", "c"))

    send = jnp.stack([_pack([_shard(G[n], n, p) for n in BIG], BF16, BIG_ROW_MULT) for p in range(4)])
    part = sum_slots("grad_big_sum", scatter_chips("grad_big_scatter", send))
    sib = swap_sibling("grad_big_sibling", part)
    pk = lambda pre: _pack([A[pre + n] for n in BIG], F32, BIG_ROW_MULT)
    res_big = [_unpack(r, big_shapes) for r in adamw("adam_big", [part, sib], pk(''), pk('m_'), pk('v_'))]

    small_names = REPLICATED + SMALL_SHARDED
    full_shapes = [G[n].shape for n in small_names]
    total = sum_slots("grad_small_sum", gather_all("grad_small_gather", _pack([G[n] for n in small_names], F32, 8)))
    gs = dict(zip(small_names, _unpack(total, full_shapes)))
    for n in SMALL_SHARDED:
        ax = SHARD_AXIS[n]
        size = gs[n].shape[ax] // 4
        gs[n] = lax.dynamic_slice_in_dim(gs[n], q * size, size, axis=ax)
    pk = lambda pre: _pack([A[pre + n] for n in small_names], F32, 8)
    own_shapes = [A[n].shape for n in small_names]
    res_small = [_unpack(r, own_shapes) for r in
                 adamw("adam_small", [_pack([gs[n] for n in small_names], F32, 8)], pk(''), pk('m_'), pk('v_'))]

    out = {}
    for j, kind in enumerate(('grad_', 'delta_', 'new_m_', 'new_v_')):
        for k, n in enumerate(BIG):
            out[kind + n] = res_big[j][k]
        for k, n in enumerate(small_names):
            out[kind + n] = res_small[j][k]
    return (loss, grad_x[None]) + tuple(out[kind + n] for kind in ('grad_', 'delta_', 'new_m_', 'new_v_')
                                        for n in WEIGHTS)
```

```python
import functools
import math

import jax
import jax.numpy as jnp
from jax import lax
from jax.experimental import pallas as pl
from jax.experimental.pallas import tpu as pltpu

F32, BF16 = jnp.float32, jnp.bfloat16
HIGHEST = lax.Precision.HIGHEST
MESH_ID = pl.DeviceIdType.MESH

D = 1024
N_META = 16
PAD = 112
LEAD = PAD + N_META
EPS = 1e-6
NEG_INF = -1e30
CHUNK = 64
VMEM_LIMIT_V7X = 56 * 1024 * 1024
MM_VMEM_BUDGET = 36 * 1024 * 1024

MLA_H, MLA_NOPE, MLA_ROPE, MLA_V = 8, 128, 64, 128
MLA_QK = MLA_NOPE + MLA_ROPE
MLA_QL, MLA_KVL = 384, 256
HG_H, HG_D, HG_C = 8, 128, 16
S5_G, S5_P, S5_K = 64, 64, 16
RET_H, RET_DK, RET_DV = 4, 256, 512
FFN_F = 2816

ADAM_LR, ADAM_B1, ADAM_B2, ADAM_EPS, ADAM_WD, ADAM_STEP = 0.001, 0.9, 0.999, 1e-08, 0.01, 10

WEIGHTS = ['meta_tokens', 'norm_mix_g', 'norm_ffn_g', 'mla_w_down', 'mla_cq_norm_g', 'mla_ckv_norm_g', 'mla_w_uq',
           'mla_w_ukv', 'mla_q_head_g', 'mla_k_head_g', 'mla_w_o', 'hgrn_w_in', 'hgrn_lb_logits', 'hgrn_o_norm_g',
           'hgrn_w_o', 's5_lam_re', 's5_lam_im', 's5_log_dt', 's5_b_re', 's5_b_im', 's5_c_re', 's5_c_im', 's5_d',
           's5_w_glu', 'ret_w_in', 'ret_gn_g', 'ret_w_o', 'ffn_w_up', 'ffn_conv_w', 'ffn_conv_b', 'ffn_w_down']
SHARD_AXIS = {'meta_tokens': 1, 'mla_w_down': 1, 'mla_w_uq': 2, 'mla_w_ukv': 2, 'mla_w_o': 1, 'hgrn_w_in': 2,
              'hgrn_w_o': 1, 's5_d': 1, 's5_w_glu': 2, 'ret_w_in': 2, 'ret_gn_g': 1, 'ret_w_o': 1, 'ffn_w_up': 2,
              'ffn_conv_w': 2, 'ffn_w_down': 1}
BIG = ['mla_w_down', 'mla_w_uq', 'mla_w_ukv', 'mla_w_o', 'hgrn_w_in', 'hgrn_w_o', 's5_w_glu', 'ret_w_in', 'ret_w_o',
       'ffn_w_up', 'ffn_w_down']
SMALL_SHARDED = ['meta_tokens', 's5_d', 'ret_gn_g', 'ffn_conv_w']
REPLICATED = [n for n in WEIGHTS if n not in SHARD_AXIS]


def _cparams():
    return pltpu.CompilerParams(vmem_limit_bytes=VMEM_LIMIT_V7X)


def _dg(a, b, ca, cb):
    return lax.dot_general(a.astype(BF16), b.astype(BF16), (((ca,), (cb,)), ((), ())),
                           preferred_element_type=F32)


@jax.custom_vjp
def mm_nn(a, b):
    return _dg(a, b, 1, 0)


@jax.custom_vjp
def mm_nt(a, b):
    return _dg(a, b, 1, 1)


@jax.custom_vjp
def mm_tn(a, b):
    return _dg(a, b, 0, 0)


mm_nn.defvjp(lambda a, b: (mm_nn(a, b), (a, b)),
             lambda r, g: (mm_nt(g, r[1]).astype(r[0].dtype), mm_tn(r[0], g).astype(r[1].dtype)))
mm_nt.defvjp(lambda a, b: (mm_nt(a, b), (a, b)),
             lambda r, g: (mm_nn(g, r[1]).astype(r[0].dtype), mm_tn(g, r[0]).astype(r[1].dtype)))
mm_tn.defvjp(lambda a, b: (mm_tn(a, b), (a, b)),
             lambda r, g: (mm_nt(r[1], g).astype(r[0].dtype), mm_nn(r[0], g).astype(r[1].dtype)))


def _dot_f32(a, b):
    return jnp.dot(a, b, precision=HIGHEST, preferred_element_type=F32)


def _shift_rows(x, s, up):
    n = x.shape[0]
    r = lax.broadcasted_iota(jnp.int32, x.shape, 0)
    if up:
        return jnp.where(r < n - s, pltpu.roll(x, n - s, 0), 0.0)
    return jnp.where(r >= s, pltpu.roll(x, s, 0), 0.0)


@functools.partial(jax.custom_vjp, nondiff_argnums=(1,))
def shift_down(x, s):
    return _shift_rows(x, s, False)


shift_down.defvjp(lambda x, s: (_shift_rows(x, s, False), None), lambda s, _, g: (_shift_rows(g, s, True),))


def _swap32_impl(x):
    ax = x.ndim - 1
    lane = lax.broadcasted_iota(jnp.int32, x.shape, ax)
    return jnp.where(lane < 32, pltpu.roll(x, 96, ax), jnp.where(lane < 64, pltpu.roll(x, 32, ax), 0.0))


@jax.custom_vjp
def swap32(x):
    return _swap32_impl(x)


swap32.defvjp(lambda x: (_swap32_impl(x), None), lambda _, g: (_swap32_impl(g),))


def _rms(x, g):
    return x * lax.rsqrt(jnp.mean(x * x, axis=-1, keepdims=True) + EPS) * g


def _silu(x):
    return x * jax.nn.sigmoid(x)


def _row_ids(pid, n, shape, axis=0):
    return pid * n + lax.broadcasted_iota(jnp.int32, shape, axis)


class Arg:
    def __init__(self, arr, block, imap, diff=True):
        self.arr, self.block, self.imap, self.diff = arr, block, imap, diff


class Out:
    def __init__(self, shape, dtype, block, imap):
        self.shape, self.dtype, self.block, self.imap = shape, dtype, block, imap


def _free_axes(imap, grid):
    ng = len(grid)
    base = tuple(imap(*([0] * ng)))
    free = []
    for ax in range(ng):
        p = [0] * ng
        p[ax] = 1
        if grid[ax] > 1 and tuple(imap(*p)) == base:
            free.append(ax)
    assert free == list(range(ng - len(free), ng)), "revisited blocks must be revisited on the innermost axes"
    return free


def stage_fwd(name, fn, grid, args, outs, state_shape=None):
    n_in, n_out, ng = len(args), len(outs), len(grid)

    def body(*refs):
        pids = tuple(pl.program_id(a) for a in range(ng))
        vals = [r[...] for r in refs[:n_in]]
        o_refs = refs[n_in:n_in + n_out]
        if state_shape is None:
            res = fn(pids, *vals)
        else:
            sv_ref, st_ref = refs[n_in + n_out], refs[n_in + n_out + 1]

            @pl.when(pids[-1] == 0)
            def _():
                st_ref[...] = jnp.zeros(state_shape, F32)

            s = st_ref[...]
            sv_ref[...] = s
            res = fn(pids, *vals, s)
            st_ref[...] = res[-1]
            res = res[:-1]
        for r, v in zip(o_refs, res):
            r[...] = v.astype(r.dtype)

    in_specs = [pl.BlockSpec(a.block, a.imap) for a in args]
    out_specs = [pl.BlockSpec(o.block, o.imap) for o in outs]
    out_shape = [jax.ShapeDtypeStruct(o.shape, o.dtype) for o in outs]
    scratch = []
    if state_shape is not None:
        nz = len(state_shape)
        out_specs.append(pl.BlockSpec((None, None) + tuple(state_shape), lambda i, j: (i, j) + (0,) * nz))
        out_shape.append(jax.ShapeDtypeStruct(tuple(grid) + tuple(state_shape), F32))
        scratch = [pltpu.VMEM(state_shape, F32)]
    return pl.pallas_call(body, grid=grid, in_specs=in_specs, out_specs=out_specs, out_shape=out_shape,
                          scratch_shapes=scratch, name=name, compiler_params=_cparams())(*[a.arr for a in args])


def stage_bwd(name, fn, grid, args, outs, cots, state_shape=None, states=None):
    n_in, n_out, ng = len(args), len(outs), len(grid)
    nb = grid[-1]
    rev = state_shape is not None
    didx = [k for k, a in enumerate(args) if a.diff]
    frees = [_free_axes(args[k].imap, grid) for k in didx]

    def eff(p):
        return tuple(p[:-1]) + (nb - 1 - p[-1],) if rev else tuple(p)

    def wrap(imap):
        return lambda *p: imap(*eff(p))

    def body(*refs):
        pids = tuple(pl.program_id(a) for a in range(ng))
        e = eff(pids)
        vals = [r[...] for r in refs[:n_in]]
        cts = tuple(r[...].astype(F32) for r in refs[n_in:n_in + n_out])
        pos = n_in + n_out
        if rev:
            st_in_ref = refs[pos]
            pos += 1
        g_refs = refs[pos:pos + len(didx)]
        pos += len(didx)
        dvals = [vals[k].astype(F32) for k in didx]

        def f(*dv):
            full = list(vals)
            for k, v in zip(didx, dv[:len(didx)]):
                full[k] = v
            return tuple(fn(e, *full, *dv[len(didx):]))

        if rev:
            ds_ref = refs[pos]

            @pl.when(pids[-1] == 0)
            def _():
                ds_ref[...] = jnp.zeros(state_shape, F32)

            _, vjp = jax.vjp(f, *dvals, st_in_ref[...])
            grads = vjp(cts + (ds_ref[...],))
            ds_ref[...] = grads[-1]
            grads = grads[:-1]
        else:
            _, vjp = jax.vjp(f, *dvals)
            grads = vjp(cts)
        for gref, g, free in zip(g_refs, grads, frees):
            g = g.astype(F32)
            if not free:
                gref[...] = g
            else:
                first = functools.reduce(jnp.logical_and, [pids[ax] == 0 for ax in free])

                @pl.when(first)
                def _():
                    gref[...] = g

                @pl.when(jnp.logical_not(first))
                def _():
                    gref[...] += g

    in_specs = [pl.BlockSpec(a.block, wrap(a.imap)) for a in args]
    in_specs += [pl.BlockSpec(o.block, wrap(o.imap)) for o in outs]
    operands = [a.arr for a in args] + list(cots)
    scratch = []
    if rev:
        nz = len(state_shape)
        in_specs.append(pl.BlockSpec((None, None) + tuple(state_shape), lambda i, j: (i, nb - 1 - j) + (0,) * nz))
        operands.append(states)
        scratch = [pltpu.VMEM(state_shape, F32)]
    out_specs = [pl.BlockSpec(args[k].block, wrap(args[k].imap)) for k in didx]
    out_shape = [jax.ShapeDtypeStruct(args[k].arr.shape, F32) for k in didx]
    return pl.pallas_call(body, grid=grid, in_specs=in_specs, out_specs=out_specs, out_shape=out_shape,
                          scratch_shapes=scratch, name=name, compiler_params=_cparams())(*operands)


def _divisors(n, cands):
    return [c for c in cands if n % c == 0] or [n]


def _nbytes(dt):
    return jnp.dtype(dt).itemsize


def matmul(name, a, b, mode, out_dtype=F32, res=None, mask=False):
    sa, sb, so = _nbytes(a.dtype), _nbytes(b.dtype), _nbytes(out_dtype)
    if mode in ('nn', 'nt'):
        M, K = a.shape
        N = b.shape[1] if mode == 'nn' else b.shape[0]
        best = None
        for tm in _divisors(M, (1408, 1056, 768, 384, 128)):
            for tn in _divisors(N, (1408, 1024, 768, 512, 384, 256, 128)):
                est = 2 * (tm * K * sa + tn * K * sb + tm * tn * (so + (4 if res is not None else 0)))
                if est <= MM_VMEM_BUDGET and (best is None or tm * tn > best[0] * best[1]):
                    best = (tm, tn)
        tm, tn = best
        grid = (M // tm, N // tn)

        def body(*refs):
            a_ref, b_ref = refs[0], refs[1]
            o_ref = refs[-1]
            x = a_ref[...]
            rows = _row_ids(pl.program_id(0), tm, (tm, 1))
            if mask:
                x = jnp.where(rows >= PAD, x, jnp.zeros_like(x))
            acc = _dg(x, b_ref[...], 1, 0 if mode == 'nn' else 1)
            if res is not None:
                acc = refs[2][...] + jnp.where(rows >= PAD, acc, 0.0)
            o_ref[...] = acc.astype(o_ref.dtype)

        in_specs = [pl.BlockSpec((tm, K), lambda i, j: (i, 0)),
                    pl.BlockSpec((K, tn), lambda i, j: (0, j)) if mode == 'nn' else
                    pl.BlockSpec((tn, K), lambda i, j: (j, 0))]
        ops = [a, b]
        if res is not None:
            in_specs.append(pl.BlockSpec((tm, tn), lambda i, j: (i, j)))
            ops.append(res)
        return pl.pallas_call(body, grid=grid, in_specs=in_specs,
                              out_specs=pl.BlockSpec((tm, tn), lambda i, j: (i, j)),
                              out_shape=jax.ShapeDtypeStruct((M, N), out_dtype), name=name,
                              compiler_params=_cparams())(*ops)
    assert mode == 'tn' and res is None
    M, K = a.shape
    N = b.shape[1]
    best = None
    for tk in _divisors(K, (1408, 1024, 768, 512, 384, 256, 128)):
        for tn in _divisors(N, (1408, 1024, 768, 512, 384, 256, 128)):
            est = 2 * (M * tk * sa + M * tn * sb + tk * tn * so)
            if est <= MM_VMEM_BUDGET and (best is None or tk * tn > best[0] * best[1]):
                best = (tk, tn)
    tk, tn = best

    def body_t(a_ref, b_ref, o_ref):
        y = b_ref[...]
        if mask:
            rows = lax.broadcasted_iota(jnp.int32, (M, 1), 0)
            y = jnp.where(rows >= PAD, y, jnp.zeros_like(y))
        o_ref[...] = _dg(a_ref[...], y, 0, 0).astype(o_ref.dtype)

    return pl.pallas_call(body_t, grid=(K // tk, N // tn),
                          in_specs=[pl.BlockSpec((M, tk), lambda i, j: (0, i)),
                                    pl.BlockSpec((M, tn), lambda i, j: (0, j))],
                          out_specs=pl.BlockSpec((tk, tn), lambda i, j: (i, j)),
                          out_shape=jax.ShapeDtypeStruct((K, N), out_dtype), name=name,
                          compiler_params=_cparams())(a, b)


def linear_bwd(name, act, w, dy, mask=False):
    return (matmul(name + "_da", dy, w, 'nt', mask=mask),
            matmul(name + "_dw", act, dy, 'tn', out_dtype=BF16, mask=mask))


def _row_tile(T):
    return _divisors(T, (384, 128))[0]


def _rows(arr, tm):
    return Arg(arr, (tm, arr.shape[1]), lambda i: (i, 0))


def _const(arr, diff=True):
    return Arg(arr, arr.shape, lambda *p: (0,) * arr.ndim, diff)


def _norm_fn(pids, h, g):
    return (_rms(h, g),)


def _norm_bwd_fn(pids, h, g):
    return (_rms(h, g), h)


def norm_fwd(name, h, g, dtype):
    T = h.shape[0]
    tm = _row_tile(T)
    return stage_fwd(name, _norm_fn, (T // tm,), [_rows(h, tm), _const(g)],
                     [Out((T, D), dtype, (tm, D), lambda i: (i, 0))])[0]


def norm_bwd(name, h, g, da, dh):
    T = h.shape[0]
    tm = _row_tile(T)
    o = Out((T, D), F32, (tm, D), lambda i: (i, 0))
    return stage_bwd(name, _norm_bwd_fn, (T // tm,), [_rows(h, tm), _const(g)], [o, o], [da, dh])


def _ffn_act_fn(pids, u, cw, cb):
    c = cw[2:3] * u + cw[1:2] * shift_down(u, 1) + cw[0:1] * shift_down(u, 2) + cb
    return (_silu(c[:, :128]) * c[:, 128:],)


def _ffn_act_args(u, cw, cb):
    T = u.shape[0]
    nt = FFN_F // 128
    args = [Arg(u, (T, 256), lambda j: (0, j)), Arg(cw, (3, 256), lambda j: (0, j)),
            Arg(cb, (1, 256), lambda j: (0, j))]
    outs = [Out((T, FFN_F), BF16, (T, 128), lambda j: (0, j))]
    return (nt,), args, outs


def _interleave_cols(w, n_parts, tile=128):
    lead = w.shape[:-1]
    n = w.shape[-1] // (n_parts * tile)
    k = len(lead)
    return w.reshape(lead + (n_parts, n, tile)).transpose(tuple(range(k)) + (k + 1, k, k + 2)).reshape(w.shape)


def _deinterleave_cols(w, n_parts, tile=128):
    lead = w.shape[:-1]
    n = w.shape[-1] // (n_parts * tile)
    k = len(lead)
    return w.reshape(lead + (n, n_parts, tile)).transpose(tuple(range(k)) + (k + 1, k, k + 2)).reshape(w.shape)


def ffn_layer(i, h, g, w_up, cw, cb, w_down):
    b = norm_fwd(f"ffn{i}_norm", h, g, BF16)
    u = matmul(f"ffn{i}_up", b, w_up, 'nn')
    grid, args, outs = _ffn_act_args(u, cw, cb)
    p = stage_fwd(f"ffn{i}_act", _ffn_act_fn, grid, args, outs)[0]
    h_new = matmul(f"ffn{i}_down", p, w_down, 'nn', res=h)

    def bwd(dh):
        dp, dwd = linear_bwd(f"ffn{i}_down_b", p, w_down, dh, mask=True)
        du, dcw, dcb = stage_bwd(f"ffn{i}_act_b", _ffn_act_fn, grid, args, outs, [dp])
        db, dwu = linear_bwd(f"ffn{i}_up_b", b, w_up, du)
        dh2, dg = norm_bwd(f"ffn{i}_norm_b", h, g, db, dh)
        return dh2, dict(g=dg, w_up=dwu, cw=dcw, cb=dcb, w_down=dwd)

    return h_new, bwd


def _mla_latent_fn(pids, down, gcq, gckv):
    cq = _rms(down[:, :MLA_QL], gcq)
    ckv = _rms(down[:, MLA_QL:MLA_QL + MLA_KVL], gckv)
    return cq, ckv, down[:, MLA_QL + MLA_KVL:]


def _rope64(x, cos, sin_signed):
    return x * cos + swap32(x) * sin_signed


def _mla_heads_fn(pids, qraw, kv, kpe, gqn, gqr, gkn, gkr, cos, sin_signed):
    qn, qr = qraw[:, :128], qraw[:, 128:]
    rq = lax.rsqrt((jnp.sum(qn * qn, -1, keepdims=True) + jnp.sum(qr * qr, -1, keepdims=True)) / MLA_QK + EPS)
    q = jnp.concatenate([qn * rq * gqn, _rope64(qr * rq * gqr, cos, sin_signed)], axis=1)
    kn, v = kv[:, :128], kv[:, 128:]
    rk = lax.rsqrt((jnp.sum(kn * kn, -1, keepdims=True) + jnp.sum(kpe * kpe, -1, keepdims=True)) / MLA_QK + EPS)
    k = jnp.concatenate([kn * rk * gkn, _rope64(kpe * rk * gkr, cos, sin_signed)], axis=1)
    return q, k, v


def _chunk_id(r):
    return jnp.where(r < LEAD, 0, 1 + lax.shift_right_arithmetic(r - LEAD, 6))


def _make_attn_fn(tq, T):
    def attn_fn(pids, q, k, v):
        s = mm_nt(q, k) * (MLA_QK ** -0.5)
        qrow = _row_ids(pids[1], tq, (tq, 1))
        krow = lax.broadcasted_iota(jnp.int32, (1, T), 1)
        ok = jnp.logical_and(_chunk_id(krow) <= _chunk_id(qrow), krow >= PAD)
        s = jnp.where(ok, s, NEG_INF)
        m = lax.stop_gradient(jnp.max(s, axis=-1, keepdims=True))
        p = jnp.exp(s - m)
        p = p / jnp.sum(p, axis=-1, keepdims=True)
        return (mm_nn(p, v),)
    return attn_fn


def mla_mixer(h, g, w, tabs):
    T = h.shape[0]
    tm = _row_tile(T)
    nt = T // tm
    a = norm_fwd("mla_norm", h, g, BF16)
    down = matmul("mla_down", a, w['w_down'], 'nn')
    lat_args = [_rows(down, tm), _const(w['gcq']), _const(w['gckv'])]
    lat_outs = [Out((T, MLA_QL), BF16, (tm, MLA_QL), lambda i: (i, 0)),
                Out((T, MLA_KVL), BF16, (tm, MLA_KVL), lambda i: (i, 0)),
                Out((T, 128), F32, (tm, 128), lambda i: (i, 0))]
    cq, ckv, kpe = stage_fwd("mla_latent", _mla_latent_fn, (nt,), lat_args, lat_outs)
    qraw = matmul("mla_uq", cq, w['w_uq'], 'nn')
    kv = matmul("mla_ukv", ckv, w['w_ukv'], 'nn')
    hd_args = [Arg(qraw, (tm, 256), lambda i, hh: (i, hh)), Arg(kv, (tm, 256), lambda i, hh: (i, hh)),
               Arg(kpe, (tm, 128), lambda i, hh: (i, 0)),
               _const(w['gqn']), _const(w['gqr']), _const(w['gkn']), _const(w['gkr']),
               Arg(tabs['cos_a'], (tm, 128), lambda i, hh: (i, 0), False),
               Arg(tabs['sin_a'], (tm, 128), lambda i, hh: (i, 0), False)]
    hd_outs = [Out((MLA_H, T, 256), BF16, (None, tm, 256), lambda i, hh: (hh, i, 0)),
               Out((MLA_H, T, 256), BF16, (None, tm, 256), lambda i, hh: (hh, i, 0)),
               Out((MLA_H, T, 128), BF16, (None, tm, 128), lambda i, hh: (hh, i, 0))]
    q, k, v = stage_fwd("mla_heads", _mla_heads_fn, (nt, MLA_H), hd_args, hd_outs)
    tq = 128
    attn_fn = _make_attn_fn(tq, T)
    at_args = [Arg(q, (None, tq, 256), lambda hh, j: (hh, j, 0)), Arg(k, (None, T, 256), lambda hh, j: (hh, 0, 0)),
               Arg(v, (None, T, 128), lambda hh, j: (hh, 0, 0))]
    at_outs = [Out((T, D), BF16, (tq, 128), lambda hh, j: (j, hh))]
    o = stage_fwd("mla_attn", attn_fn, (MLA_H, T // tq), at_args, at_outs)[0]
    h_new = matmul("mla_o", o, w['w_o'], 'nn', res=h)

    def bwd(dh):
        do, dwo = linear_bwd("mla_o_b", o, w['w_o'], dh, mask=True)
        dq, dk, dv = stage_bwd("mla_attn_b", attn_fn, (MLA_H, T // tq), at_args, at_outs, [do])
        dqraw, dkv, dkpe, dgqn, dgqr, dgkn, dgkr = stage_bwd("mla_heads_b", _mla_heads_fn, (nt, MLA_H), hd_args,
                                                             hd_outs, [dq, dk, dv])
        dcq, dwuq = linear_bwd("mla_uq_b", cq, w['w_uq'], dqraw)
        dckv, dwukv = linear_bwd("mla_ukv_b", ckv, w['w_ukv'], dkv)
        ddown, dgcq, dgckv = stage_bwd("mla_latent_b", _mla_latent_fn, (nt,), lat_args, lat_outs, [dcq, dckv, dkpe])
        da, dwdown = linear_bwd("mla_down_b", a, w['w_down'], ddown)
        dh2, dg = norm_bwd("mla_norm_b", h, g, da, dh)
        return dh2, dict(g=dg, w_down=dwdown, gcq=dgcq, gckv=dgckv, w_uq=dwuq, w_ukv=dwukv, gqn=dgqn, gqr=dgqr,
                         gkn=dgkn, gkr=dgkr, w_o=dwo)

    return h_new, bwd


HG_R = 128


def _hgrn_fn(pids, z, lb, go, st):
    R = z.shape[0]
    zq, zf, zi, zg = z[:, :128], z[:, 128:256], z[:, 256:384], z[:, 384:]
    q = _silu(zq)
    fg = lb + (1.0 - lb) * jax.nn.sigmoid(zf)
    logf = jnp.log(fg)
    k = 1.0 - fg
    r = lax.broadcasted_iota(jnp.int32, (R, R), 0)
    c = lax.broadcasted_iota(jnp.int32, (R, R), 1)
    same = lax.shift_right_arithmetic(r, 4) == lax.shift_right_arithmetic(c, 4)
    causal = jnp.logical_and(same, c <= r)
    gcum = _dot_f32(jnp.where(causal, 1.0, 0.0), logf)
    glast = _dot_f32(jnp.where(same, 1.0, 0.0), logf)
    qd = q * jnp.exp(gcum)
    ki = k * jnp.exp(-gcum)
    kt = k * jnp.exp(glast - gcum)
    o = mm_nn(jnp.where(causal, mm_nt(qd, ki), 0.0), zi)
    inter = []
    for cc in range(R // HG_C):
        lo = cc * HG_C
        inter.append(mm_nt(qd[lo:lo + HG_C], st))
        st = st * jnp.exp(glast[lo:lo + 1, :]) + mm_tn(zi[lo:lo + HG_C], kt[lo:lo + HG_C])
    o = o + jnp.concatenate(inter, axis=0)
    return _rms(o, go) * _silu(zg), st


def hgrn_mixer(h, g, w):
    T = h.shape[0]
    a = norm_fwd("hgrn_norm", h, g, BF16)
    z = matmul("hgrn_in", a, w['w_in'], 'nn')
    grid = (HG_H, T // HG_R)
    args = [Arg(z, (HG_R, 512), lambda hh, j: (j, hh)), Arg(w['lb'], (1, 128), lambda hh, j: (0, hh)),
            _const(w['go'])]
    outs = [Out((T, D), BF16, (HG_R, 128), lambda hh, j: (j, hh))]
    o, states = stage_fwd("hgrn_gla", _hgrn_fn, grid, args, outs, state_shape=(HG_D, HG_D))
    h_new = matmul("hgrn_o", o, w['w_o'], 'nn', res=h)

    def bwd(dh):
        do, dwo = linear_bwd("hgrn_o_b", o, w['w_o'], dh, mask=True)
        dz, dlb, dgo = stage_bwd("hgrn_gla_b", _hgrn_fn, grid, args, outs, [do], state_shape=(HG_D, HG_D),
                                 states=states)
        da, dwin = linear_bwd("hgrn_in_b", a, w['w_in'], dz)
        dh2, dg = norm_bwd("hgrn_norm_b", h, g, da, dh)
        return dh2, dict(g=dg, w_in=dwin, lb=dlb, go=dgo, w_o=dwo)

    return h_new, bwd


S5_R = 128
S5_W = 512
S5_SLABS = D // 128


def _cmul(ar, ai, br, bi):
    return ar * br - ai * bi, ar * bi + ai * br


def _s5_scan(br, bi, tab, cr, ci, reverse):
    R, W = br.shape
    G = R // 8
    xr, xi = br.reshape(G, 8, W), bi.reshape(G, 8, W)
    for n, d in enumerate((1, 2, 4)):
        sh = (8 - d) if reverse else d
        mr, mi = _cmul(tab[2 * n][None], tab[2 * n + 1][None], pltpu.roll(xr, sh, 1), pltpu.roll(xi, sh, 1))
        xr, xi = xr + mr, xi + mi
    pr, pi = tab[6], tab[7]
    edge = 0 if reverse else 7
    out_r, out_i = [None] * G, [None] * G
    for g in (range(G - 1, -1, -1) if reverse else range(G)):
        ar, ai = _cmul(pr, pi, cr, ci)
        gr, gi = xr[g] + ar, xi[g] + ai
        cr, ci = gr[edge:edge + 1], gi[edge:edge + 1]
        out_r[g], out_i[g] = gr, gi
    return jnp.concatenate(out_r, axis=0), jnp.concatenate(out_i, axis=0), cr, ci


def s5_scan_fwd(a, bb, cb, tab):
    T = a.shape[0]
    nb = T // S5_R

    def body(a_ref, bb_ref, cb_ref, tab_ref, y_ref, xs_ref, c_ref):
        @pl.when(pl.program_id(1) == 0)
        def _():
            c_ref[...] = jnp.zeros(c_ref.shape, F32)

        bu = _dg(a_ref[...], bb_ref[...], 1, 0)
        t = tab_ref[...]
        xr, xi, cr, ci = _s5_scan(bu[:, :S5_W], bu[:, S5_W:], t, c_ref[0:1, :S5_W], c_ref[0:1, S5_W:], False)
        x = jnp.concatenate([xr, xi], axis=1)
        xs_ref[...] = x
        y_ref[...] = _dg(x, cb_ref[...], 1, 0)
        c_ref[0:1, :] = jnp.concatenate([cr, ci], axis=1)

    return pl.pallas_call(
        body, grid=(S5_SLABS, nb),
        in_specs=[pl.BlockSpec((S5_R, 128), lambda j, i: (i, j)),
                  pl.BlockSpec((None, 128, 2 * S5_W), lambda j, i: (j, 0, 0)),
                  pl.BlockSpec((None, 2 * S5_W, 128), lambda j, i: (j, 0, 0)),
                  pl.BlockSpec((None, 10, 8, S5_W), lambda j, i: (j, 0, 0, 0))],
        out_specs=[pl.BlockSpec((S5_R, 128), lambda j, i: (i, j)),
                   pl.BlockSpec((None, S5_R, 2 * S5_W), lambda j, i: (j, i, 0))],
        out_shape=[jax.ShapeDtypeStruct((T, D), F32), jax.ShapeDtypeStruct((S5_SLABS, T, 2 * S5_W), F32)],
        scratch_shapes=[pltpu.VMEM((8, 2 * S5_W), F32)], name="s5_scan", compiler_params=_cparams())(a, bb, cb, tab)


def s5_scan_bwd(a, bb, cb, tab_rev, xs, dy):
    T = a.shape[0]
    nb = T // S5_R
    rg = S5_R // 8

    def body(a_ref, dy_ref, xs_ref, xp_ref, bb_ref, cb_ref, tab_ref, da_ref, dbb_ref, dcb_ref, dab_ref, c_ref):
        i = pl.program_id(1)

        @pl.when(i == 0)
        def _():
            c_ref[...] = jnp.zeros(c_ref.shape, F32)

        dy_v = dy_ref[...]
        x = xs_ref[...]
        dxo = _dg(dy_v, cb_ref[...], 1, 1)
        gr, gi, cr, ci = _s5_scan(dxo[:, :S5_W], dxo[:, S5_W:], tab_ref[...], c_ref[0:1, :S5_W], c_ref[0:1, S5_W:], True)
        c_ref[0:1, :] = jnp.concatenate([cr, ci], axis=1)
        g = jnp.concatenate([gr, gi], axis=1)
        da_ref[...] = _dg(g, bb_ref[...], 1, 1)
        dbb = _dg(a_ref[...], g, 0, 0)
        dcb = _dg(x, dy_v, 0, 0)
        first_tile = i == nb - 1
        prev_last = jnp.where(first_tile, 0.0, xp_ref[7:8, :])
        rows = lax.broadcasted_iota(jnp.int32, x.shape, 0)
        xp = jnp.where(rows == 0, prev_last, pltpu.roll(x, 1, 0))
        xpr, xpi = xp[:, :S5_W], xp[:, S5_W:]
        dar = (gr * xpr + gi * xpi).reshape(rg, 8, S5_W).sum(axis=0)
        dai = (gi * xpr - gr * xpi).reshape(rg, 8, S5_W).sum(axis=0)
        dab = jnp.concatenate([dar, dai], axis=1)

        @pl.when(i == 0)
        def _():
            dbb_ref[...] = dbb
            dcb_ref[...] = dcb
            dab_ref[...] = dab

        @pl.when(i != 0)
        def _():
            dbb_ref[...] += dbb
            dcb_ref[...] += dcb
            dab_ref[...] += dab

    def prev_rows(j, i):
        return (j, jnp.maximum((nb - 1 - i) * rg - 1, 0), 0)

    return pl.pallas_call(
        body, grid=(S5_SLABS, nb),
        in_specs=[pl.BlockSpec((S5_R, 128), lambda j, i: (nb - 1 - i, j)),
                  pl.BlockSpec((S5_R, 128), lambda j, i: (nb - 1 - i, j)),
                  pl.BlockSpec((None, S5_R, 2 * S5_W), lambda j, i: (j, nb - 1 - i, 0)),
                  pl.BlockSpec((None, 8, 2 * S5_W), prev_rows),
                  pl.BlockSpec((None, 128, 2 * S5_W), lambda j, i: (j, 0, 0)),
                  pl.BlockSpec((None, 2 * S5_W, 128), lambda j, i: (j, 0, 0)),
                  pl.BlockSpec((None, 10, 8, S5_W), lambda j, i: (j, 0, 0, 0))],
        out_specs=[pl.BlockSpec((S5_R, 128), lambda j, i: (nb - 1 - i, j)),
                   pl.BlockSpec((None, 128, 2 * S5_W), lambda j, i: (j, 0, 0)),
                   pl.BlockSpec((None, 2 * S5_W, 128), lambda j, i: (j, 0, 0)),
                   pl.BlockSpec((None, 8, 2 * S5_W), lambda j, i: (j, 0, 0))],
        out_shape=[jax.ShapeDtypeStruct((T, D), F32), jax.ShapeDtypeStruct((S5_SLABS, 128, 2 * S5_W), F32),
                   jax.ShapeDtypeStruct((S5_SLABS, 2 * S5_W, 128), F32),
                   jax.ShapeDtypeStruct((S5_SLABS, 8, 2 * S5_W), F32)],
        scratch_shapes=[pltpu.VMEM((8, 2 * S5_W), F32)], name="s5_scan_b",
        compiler_params=_cparams())(a, dy, xs, xs, bb, cb, tab_rev)


def _s5_discretise(lam_re, lam_im, log_dt, b_re, b_im, c_re, c_im):
    dt = jnp.exp(log_dt)[:, None]
    mag = jnp.exp(lam_re * dt)
    abar_re = mag * jnp.cos(lam_im * dt)
    abar_im = mag * jnp.sin(lam_im * dt)
    den = lam_re * lam_re + lam_im * lam_im
    zoh_re = ((abar_re - 1.0) * lam_re + abar_im * lam_im) / den
    zoh_im = (abar_im * lam_re - (abar_re - 1.0) * lam_im) / den
    bbar_re = zoh_re[..., None] * b_re - zoh_im[..., None] * b_im
    bbar_im = zoh_re[..., None] * b_im + zoh_im[..., None] * b_re
    eye = jnp.eye(8, dtype=F32)

    def in_map(bbar):
        t = bbar.reshape(8, 8, S5_P, S5_K).transpose(0, 1, 3, 2)
        return (t[:, :, :, None, :] * eye[None, :, None, :, None]).reshape(8, 8 * S5_K, 8 * S5_P)

    def out_map(c):
        t = c.reshape(8, 8, S5_K, S5_P).transpose(0, 1, 3, 2)
        return (t[:, :, :, None, :] * eye[None, :, None, :, None]).reshape(8, 8 * S5_P, 8 * S5_K)

    bb = jnp.concatenate([in_map(bbar_re), in_map(bbar_im)], axis=2)
    cb = jnp.concatenate([out_map(c_re), -out_map(c_im)], axis=1)
    return bb, cb, abar_re.reshape(8, S5_W), abar_im.reshape(8, S5_W)


def _s5_tables(ar, ai, reverse):
    if reverse:
        ai = -ai
    pw = [(jnp.ones_like(ar), jnp.zeros_like(ar))]
    for _ in range(8):
        pw.append(_cmul(pw[-1][0], pw[-1][1], ar, ai))
    r = jnp.arange(8)[None, :, None]
    rows = []
    for d in (1, 2, 4):
        keep = (r <= 7 - d) if reverse else (r >= d)
        rows += [jnp.where(keep, pw[d][0][:, None, :], 0.0), jnp.where(keep, pw[d][1][:, None, :], 0.0)]
    order = [8 - k for k in range(8)] if reverse else [k + 1 for k in range(8)]
    rows += [jnp.stack([pw[n][0] for n in order], axis=1), jnp.stack([pw[n][1] for n in order], axis=1)]
    rows += [jnp.broadcast_to(pw[8][0][:, None, :], (8, 8, S5_W)), jnp.broadcast_to(pw[8][1][:, None, :], (8, 8, S5_W))]
    return jnp.stack(rows, axis=1)


def _s5_act_fn(pids, yc, a, dskip):
    return (jax.nn.gelu(yc + dskip * a),)


def _make_glu_res_fn(tm):
    def glu_res_fn(pids, zz, h):
        rows = _row_ids(pids[0], tm, (tm, 1))
        return (h + jnp.where(rows >= PAD, zz[:, :D] * jax.nn.sigmoid(zz[:, D:]), 0.0),)
    return glu_res_fn


def s5_mixer(h, g, w):
    T = h.shape[0]
    tm = _row_tile(T)
    nt = T // tm
    a = norm_fwd("s5_norm", h, g, F32)
    ssm = [w[n] for n in ('lam_re', 'lam_im', 'log_dt', 'b_re', 'b_im', 'c_re', 'c_im')]
    (bb, cb, ar, ai), disc_vjp = jax.vjp(_s5_discretise, *ssm)
    yc, xs = s5_scan_fwd(a, bb, cb, _s5_tables(ar, ai, False))
    row = lambda arr: _rows(arr, tm)
    act_args = [row(yc), row(a), _const(w['dskip'])]
    act_outs = [Out((T, D), BF16, (tm, D), lambda i: (i, 0))]
    y = stage_fwd("s5_act", _s5_act_fn, (nt,), act_args, act_outs)[0]
    zz = matmul("s5_glu", y, w['w_glu'], 'nn')
    glu_fn = _make_glu_res_fn(tm)
    glu_args = [row(zz), row(h)]
    glu_outs = [Out((T, D), F32, (tm, D), lambda i: (i, 0))]
    h_new = stage_fwd("s5_gate", glu_fn, (nt,), glu_args, glu_outs)[0]

    def bwd(dh):
        dzz, dh_res = stage_bwd("s5_gate_b", glu_fn, (nt,), glu_args, glu_outs, [dh])
        dy, dwglu = linear_bwd("s5_glu_b", y, w['w_glu'], dzz)
        dyc, da1, ddskip = stage_bwd("s5_act_b", _s5_act_fn, (nt,), act_args, act_outs, [dy])
        da2, dbb, dcb, dab = s5_scan_bwd(a, bb, cb, _s5_tables(ar, ai, True), xs, dyc)
        dab = dab.sum(axis=1)
        dssm = disc_vjp((dbb, dcb, dab[:, :S5_W], dab[:, S5_W:]))
        dh2, dg = _s5_norm_bwd(h, g, da1, da2, dh_res, tm)
        grads = dict(zip(('lam_re', 'lam_im', 'log_dt', 'b_re', 'b_im', 'c_re', 'c_im'), dssm))
        grads.update(g=dg, dskip=ddskip, w_glu=dwglu)
        return dh2, grads

    return h_new, bwd


def _norm3_bwd_fn(pids, h, g):
    a = _rms(h, g)
    return a, a, h


def _s5_norm_bwd(h, g, da1, da2, dh, tm):
    T = h.shape[0]
    o = Out((T, D), F32, (tm, D), lambda i: (i, 0))
    return stage_bwd("s5_norm_b", _norm3_bwd_fn, (T // tm,), [_rows(h, tm), _const(g)], [o, o, o], [da1, da2, dh])


RET_R = 128


def _rope256(x, cos, sin):
    x1, x2 = x[:, :128], x[:, 128:]
    return jnp.concatenate([x1 * cos - x2 * sin, x1 * sin + x2 * cos], axis=1)


def _ret_fn(pids, z, gn, cos, sin, dmat, qdec, kdec, cdec, st):
    R = z.shape[0]
    q = _rope256(z[:, :256], cos, sin)
    k = _rope256(z[:, 256:512], cos, sin) * (RET_DK ** -0.5)
    v, gate = z[:, 512:1024], z[:, 1024:]
    outs = []
    for cc in range(R // CHUNK):
        lo = cc * CHUNK
        qc, kc, vc = q[lo:lo + CHUNK], k[lo:lo + CHUNK], v[lo:lo + CHUNK]
        outs.append(mm_nn(mm_nt(qc, kc) * dmat, vc) + mm_nn(qc * qdec, st))
        st = st * cdec + mm_tn(kc * kdec, vc)
    o = jnp.concatenate(outs, axis=0)
    mu = jnp.mean(o, axis=-1, keepdims=True)
    var = jnp.mean(jnp.square(o - mu), axis=-1, keepdims=True)
    o = (o - mu) * lax.rsqrt(var + EPS)
    return o * gn * _silu(gate), st


def ret_mixer(h, g, w, tabs):
    T = h.shape[0]
    a = norm_fwd("ret_norm", h, g, BF16)
    z = matmul("ret_in", a, w['w_in'], 'nn')
    grid = (RET_H, T // RET_R)
    hw = RET_DK * 2 + RET_DV * 2
    args = [Arg(z, (RET_R, hw), lambda hh, j: (j, hh)), Arg(w['gn'], (1, RET_DV), lambda hh, j: (0, hh)),
            Arg(tabs['cos_d'], (RET_R, 128), lambda hh, j: (j, 0), False),
            Arg(tabs['sin_d'], (RET_R, 128), lambda hh, j: (j, 0), False),
            Arg(tabs['ret_dmat'], (None, CHUNK, CHUNK), lambda hh, j: (hh, 0, 0), False),
            Arg(tabs['ret_qdec'], (None, CHUNK, 1), lambda hh, j: (hh, 0, 0), False),
            Arg(tabs['ret_kdec'], (None, CHUNK, 1), lambda hh, j: (hh, 0, 0), False),
            Arg(tabs['ret_cdec'], (None, 1, 1), lambda hh, j: (hh, 0, 0), False)]
    outs = [Out((T, RET_H * RET_DV), BF16, (RET_R, RET_DV), lambda hh, j: (j, hh))]
    o, states = stage_fwd("ret_chunks", _ret_fn, grid, args, outs, state_shape=(RET_DK, RET_DV))
    h_new = matmul("ret_o", o, w['w_o'], 'nn', res=h)

    def bwd(dh):
        do, dwo = linear_bwd("ret_o_b", o, w['w_o'], dh, mask=True)
        dz, dgn = stage_bwd("ret_chunks_b", _ret_fn, grid, args, outs, [do], state_shape=(RET_DK, RET_DV),
                            states=states)
        da, dwin = linear_bwd("ret_in_b", a, w['w_in'], dz)
        dh2, dg = norm_bwd("ret_norm_b", h, g, da, dh)
        return dh2, dict(g=dg, w_in=dwin, gn=dgn, w_o=dwo)

    return h_new, bwd


def loss_head(h, tgt):
    T = h.shape[0]
    tm = _row_tile(T)

    def body(h_ref, t_ref, loss_ref, dh_ref):
        i = pl.program_id(0)
        rows = _row_ids(i, tm, (tm, 1))
        err = jnp.where(rows >= LEAD, h_ref[...] - t_ref[...], 0.0)
        dh_ref[...] = err * (1.0 / D)
        part = jnp.full((8, 128), 0.5 * jnp.sum(jnp.sum(err * err, axis=1, keepdims=True) * (1.0 / D)), F32)

        @pl.when(i == 0)
        def _():
            loss_ref[...] = part

        @pl.when(i != 0)
        def _():
            loss_ref[...] += part

    loss, dh = pl.pallas_call(
        body, grid=(T // tm,),
        in_specs=[pl.BlockSpec((tm, D), lambda i: (i, 0)), pl.BlockSpec((tm, D), lambda i: (i, 0))],
        out_specs=[pl.BlockSpec((8, 128), lambda i: (0, 0)), pl.BlockSpec((tm, D), lambda i: (i, 0))],
        out_shape=[jax.ShapeDtypeStruct((8, 128), F32), jax.ShapeDtypeStruct((T, D), F32)], name="loss_head",
        compiler_params=_cparams())(h, tgt)
    return loss[0, 0], dh


def _tables(T):
    pos = jnp.maximum(jnp.arange(T, dtype=jnp.int32) - PAD, 0).astype(F32)

    def cs(dim):
        inv_freq = 1.0 / (10000.0 ** (jnp.arange(0, dim, 2, dtype=F32) / dim))
        ang = pos[:, None] * inv_freq[None, :]
        return jnp.cos(ang), jnp.sin(ang)

    ca, sa = cs(MLA_ROPE)
    zeros = jnp.zeros((T, 64), F32)
    cd, sd = cs(RET_DK)
    log_gamma = jnp.log(1.0 - jnp.exp2(-5.0 - jnp.arange(RET_H, dtype=F32)))
    p = jnp.arange(CHUNK, dtype=F32)
    diff = p[:, None] - p[None, :]
    dmat = jnp.where(diff >= 0, jnp.exp(diff[None] * log_gamma[:, None, None]), 0.0)
    return dict(cos_a=jnp.concatenate([ca, ca, zeros], axis=1), sin_a=jnp.concatenate([-sa, sa, zeros], axis=1),
                cos_d=cd, sin_d=sd, ret_dmat=dmat,
                ret_qdec=jnp.exp((p[None, :] + 1.0) * log_gamma[:, None])[..., None],
                ret_kdec=jnp.exp((CHUNK - 1.0 - p[None, :]) * log_gamma[:, None])[..., None],
                ret_cdec=jnp.exp(CHUNK * log_gamma)[:, None, None])


def _hgrn_lower_bound(logits):
    lb_cum = jnp.cumsum(jax.nn.softmax(logits, axis=0), axis=0)
    return (lb_cum - lb_cum[0:1])[1:2]


def _uq_to_heads(w):
    t = w.reshape(w.shape[0], MLA_H, MLA_QK)
    return jnp.pad(t, ((0, 0), (0, 0), (0, 256 - MLA_QK))).reshape(w.shape[0], MLA_H * 256)


def _uq_from_heads(g):
    return g.reshape(g.shape[0], MLA_H, 256)[:, :, :MLA_QK].reshape(g.shape[0], MLA_H * MLA_QK)


def _head_interleave(w, widths, heads):
    parts, lo = [], 0
    for wd in widths:
        parts.append(w[:, lo:lo + heads * wd].reshape(w.shape[0], heads, wd))
        lo += heads * wd
    return jnp.concatenate(parts, axis=2).reshape(w.shape[0], -1)


def _head_deinterleave(g, widths, heads):
    t = g.reshape(g.shape[0], heads, sum(widths))
    parts, lo = [], 0
    for wd in widths:
        parts.append(t[:, :, lo:lo + wd].reshape(g.shape[0], heads * wd))
        lo += wd
    return jnp.concatenate(parts, axis=1)


HG_WIDTHS = (128, 128, 128, 128)
RET_WIDTHS = (RET_DK, RET_DK, RET_DV, RET_DV)


def _split_head_gain(g):
    return g[:, :128], jnp.pad(g[:, 128:], ((0, 0), (0, 64)))


def _join_head_gain(dn, dr):
    return jnp.concatenate([dn, dr[:, :64]], axis=1)


def local_step(x, target, W):
    S = x.shape[0]
    T = S + LEAD
    tabs = _tables(T)
    h = jnp.concatenate([jnp.zeros((PAD, D), F32), W['meta_tokens'], x], axis=0)
    tgt = jnp.concatenate([jnp.zeros((LEAD, D), F32), target], axis=0)
    bwds = []

    gqn, gqr = _split_head_gain(W['mla_q_head_g'])
    gkn, gkr = _split_head_gain(W['mla_k_head_g'])
    lb, lb_vjp = jax.vjp(_hgrn_lower_bound, W['hgrn_lb_logits'])
    mix_w = [
        dict(w_down=jnp.pad(W['mla_w_down'][0], ((0, 0), (0, 64))), gcq=W['mla_cq_norm_g'], gckv=W['mla_ckv_norm_g'],
             w_uq=_uq_to_heads(W['mla_w_uq'][0]), w_ukv=W['mla_w_ukv'][0], gqn=gqn, gqr=gqr, gkn=gkn, gkr=gkr,
             w_o=W['mla_w_o'][0]),
        dict(w_in=_head_interleave(W['hgrn_w_in'][0], HG_WIDTHS, HG_H), lb=lb, go=W['hgrn_o_norm_g'],
             w_o=W['hgrn_w_o'][0]),
        dict(lam_re=W['s5_lam_re'][0], lam_im=W['s5_lam_im'][0], log_dt=W['s5_log_dt'][0], b_re=W['s5_b_re'][0],
             b_im=W['s5_b_im'][0], c_re=W['s5_c_re'][0], c_im=W['s5_c_im'][0], dskip=W['s5_d'], w_glu=W['s5_w_glu'][0]),
        dict(w_in=_head_interleave(W['ret_w_in'][0], RET_WIDTHS, RET_H), gn=W['ret_gn_g'], w_o=W['ret_w_o'][0]),
    ]
    for i in range(4):
        g_mix = W['norm_mix_g'][i:i + 1]
        if i == 0:
            h, b = mla_mixer(h, g_mix, mix_w[0], tabs)
        elif i == 1:
            h, b = hgrn_mixer(h, g_mix, mix_w[1])
        elif i == 2:
            h, b = s5_mixer(h, g_mix, mix_w[2])
        else:
            h, b = ret_mixer(h, g_mix, mix_w[3], tabs)
        bwds.append(b)
        h, b = ffn_layer(i, h, W['norm_ffn_g'][i:i + 1], _interleave_cols(W['ffn_w_up'][i], 2),
                         _interleave_cols(W['ffn_conv_w'][i], 2), _interleave_cols(W['ffn_conv_b'][i:i + 1], 2),
                         W['ffn_w_down'][i])
        bwds.append(b)

    loss, dh = loss_head(h, tgt)
    gm, gf = [None] * 4, [None] * 4
    for i in range(3, -1, -1):
        dh, gf[i] = bwds[2 * i + 1](dh)
        dh, gm[i] = bwds[2 * i](dh)

    G = {}
    G['meta_tokens'] = dh[PAD:LEAD]
    G['norm_mix_g'] = jnp.concatenate([gm[i]['g'] for i in range(4)], axis=0)
    G['norm_ffn_g'] = jnp.concatenate([gf[i]['g'] for i in range(4)], axis=0)
    a = gm[0]
    G['mla_w_down'] = a['w_down'][None, :, :MLA_QL + MLA_KVL + MLA_ROPE]
    G['mla_cq_norm_g'], G['mla_ckv_norm_g'] = a['gcq'], a['gckv']
    G['mla_w_uq'] = _uq_from_heads(a['w_uq'])[None]
    G['mla_w_ukv'] = a['w_ukv'][None]
    G['mla_q_head_g'] = _join_head_gain(a['gqn'], a['gqr'])
    G['mla_k_head_g'] = _join_head_gain(a['gkn'], a['gkr'])
    G['mla_w_o'] = a['w_o'][None]
    b = gm[1]
    G['hgrn_w_in'] = _head_deinterleave(b['w_in'], HG_WIDTHS, HG_H)[None]
    G['hgrn_lb_logits'] = lb_vjp(b['lb'])[0]
    G['hgrn_o_norm_g'] = b['go']
    G['hgrn_w_o'] = b['w_o'][None]
    c = gm[2]
    for n in ('lam_re', 'lam_im', 'log_dt', 'b_re', 'b_im', 'c_re', 'c_im'):
        G['s5_' + n] = c[n][None]
    G['s5_d'] = c['dskip']
    G['s5_w_glu'] = c['w_glu'][None]
    d = gm[3]
    G['ret_w_in'] = _head_deinterleave(d['w_in'], RET_WIDTHS, RET_H)[None]
    G['ret_gn_g'] = d['gn']
    G['ret_w_o'] = d['w_o'][None]
    G['ffn_w_up'] = jnp.stack([_deinterleave_cols(gf[i]['w_up'], 2) for i in range(4)])
    G['ffn_conv_w'] = jnp.stack([_deinterleave_cols(gf[i]['cw'], 2) for i in range(4)])
    G['ffn_conv_b'] = jnp.concatenate([_deinterleave_cols(gf[i]['cb'], 2) for i in range(4)], axis=0)
    G['ffn_w_down'] = jnp.stack([gf[i]['w_down'] for i in range(4)])
    return loss, dh[LEAD:], G


PACK_W = 1024
ANY = pl.BlockSpec(memory_space=pl.ANY)


def _pack(arrs, dtype, row_mult):
    flat = jnp.concatenate([a.reshape(-1).astype(dtype) for a in arrs])
    n = flat.shape[0]
    rows = -(-n // (PACK_W * row_mult)) * row_mult
    return jnp.pad(flat, (0, rows * PACK_W - n)).reshape(rows, PACK_W)


def _unpack(buf, shapes):
    flat = buf.reshape(-1)
    out, off = [], 0
    for s in shapes:
        n = math.prod(s)
        out.append(flat[off:off + n].reshape(s))
        off += n
    return out


def _my_pos():
    return lax.axis_index("x"), lax.axis_index("y"), lax.axis_index("c")


def _other_chips(x, y):
    return [(1 - x, y), (x, 1 - y), (1 - x, 1 - y)]


def gather_chips(name, src):
    def body(src_ref, out_ref, send_sems, recv_sems, local_sem):
        x, y, c = _my_pos()
        q = 2 * x + y
        mine = pltpu.make_async_copy(src_ref, out_ref.at[q], local_sem)
        mine.start()
        peers = _other_chips(x, y)

        def copy(k, slot, peer):
            return pltpu.make_async_remote_copy(src_ref=src_ref, dst_ref=out_ref.at[slot], send_sem=send_sems.at[k],
                                                recv_sem=recv_sems.at[k], device_id=(peer[0], peer[1], c),
                                                device_id_type=MESH_ID)
        sends = [copy(k, q, p) for k, p in enumerate(peers)]
        for cp in sends:
            cp.start()
        for k, p in enumerate(peers):
            copy(k, 2 * p[0] + p[1], p).wait_recv()
        for cp in sends:
            cp.wait_send()
        mine.wait()

    return pl.pallas_call(body, out_shape=jax.ShapeDtypeStruct((4,) + src.shape, src.dtype), in_specs=[ANY],
                          out_specs=ANY, name=name,
                          scratch_shapes=[pltpu.SemaphoreType.DMA((3,)), pltpu.SemaphoreType.DMA((3,)),
                                          pltpu.SemaphoreType.DMA(())])(src)


def scatter_chips(name, src):
    def body(src_ref, out_ref, send_sems, recv_sems, local_sem):
        x, y, c = _my_pos()
        q = 2 * x + y
        mine = pltpu.make_async_copy(src_ref.at[q], out_ref.at[q], local_sem)
        mine.start()
        peers = _other_chips(x, y)

        def copy(k, peer):
            slot = 2 * peer[0] + peer[1]
            return pltpu.make_async_remote_copy(src_ref=src_ref.at[slot], dst_ref=out_ref.at[q], send_sem=send_sems.at[k],
                                                recv_sem=recv_sems.at[k], device_id=(peer[0], peer[1], c),
                                                device_id_type=MESH_ID)

        def landing(k, peer):
            slot = 2 * peer[0] + peer[1]
            return pltpu.make_async_remote_copy(src_ref=src_ref.at[slot], dst_ref=out_ref.at[slot],
                                                send_sem=send_sems.at[k], recv_sem=recv_sems.at[k],
                                                device_id=(peer[0], peer[1], c), device_id_type=MESH_ID)
        sends = [copy(k, p) for k, p in enumerate(peers)]
        for cp in sends:
            cp.start()
        for k, p in enumerate(peers):
            landing(k, p).wait_recv()
        for cp in sends:
            cp.wait_send()
        mine.wait()

    return pl.pallas_call(body, out_shape=jax.ShapeDtypeStruct(src.shape, src.dtype), in_specs=[ANY], out_specs=ANY,
                          name=name, scratch_shapes=[pltpu.SemaphoreType.DMA((3,)), pltpu.SemaphoreType.DMA((3,)),
                                                     pltpu.SemaphoreType.DMA(())])(src)


def swap_sibling(name, src):
    def body(src_ref, out_ref, send_sem, recv_sem):
        x, y, c = _my_pos()
        cp = pltpu.make_async_remote_copy(src_ref=src_ref, dst_ref=out_ref, send_sem=send_sem, recv_sem=recv_sem,
                                          device_id=(x, y, 1 - c), device_id_type=MESH_ID)
        cp.start()
        cp.wait()

    return pl.pallas_call(body, out_shape=jax.ShapeDtypeStruct(src.shape, src.dtype), in_specs=[ANY], out_specs=ANY,
                          name=name, scratch_shapes=[pltpu.SemaphoreType.DMA(()), pltpu.SemaphoreType.DMA(())])(src)


def gather_all(name, src):
    def body(src_ref, out_ref, send_sems, recv_sems, local_sem):
        x, y, c = _my_pos()
        me = 4 * x + 2 * y + c
        mine = pltpu.make_async_copy(src_ref, out_ref.at[me], local_sem)
        mine.start()
        peers = [((1 - x) if m & 4 else x, (1 - y) if m & 2 else y, (1 - c) if m & 1 else c) for m in range(1, 8)]

        def copy(k, slot, peer):
            return pltpu.make_async_remote_copy(src_ref=src_ref, dst_ref=out_ref.at[slot], send_sem=send_sems.at[k],
                                                recv_sem=recv_sems.at[k], device_id=peer, device_id_type=MESH_ID)
        sends = [copy(k, me, p) for k, p in enumerate(peers)]
        for cp in sends:
            cp.start()
        for k, p in enumerate(peers):
            copy(k, 4 * p[0] + 2 * p[1] + p[2], p).wait_recv()
        for cp in sends:
            cp.wait_send()
        mine.wait()

    return pl.pallas_call(body, out_shape=jax.ShapeDtypeStruct((8,) + src.shape, src.dtype), in_specs=[ANY],
                          out_specs=ANY, name=name,
                          scratch_shapes=[pltpu.SemaphoreType.DMA((7,)), pltpu.SemaphoreType.DMA((7,)),
                                          pltpu.SemaphoreType.DMA(())])(src)


def _pack_tile(rows):
    return _divisors(rows, (256, 128, 64, 32, 16, 8))[0] if rows > 512 else rows


def sum_slots(name, slots):
    n, rows, w = slots.shape
    tr = _pack_tile(rows)

    def body(s_ref, o_ref):
        acc = s_ref[0].astype(F32)
        for k in range(1, n):
            acc = acc + s_ref[k].astype(F32)
        o_ref[...] = acc

    return pl.pallas_call(body, grid=(rows // tr,), in_specs=[pl.BlockSpec((n, tr, w), lambda i: (0, i, 0))],
                          out_specs=pl.BlockSpec((tr, w), lambda i: (i, 0)),
                          out_shape=jax.ShapeDtypeStruct((rows, w), F32), name=name, compiler_params=_cparams())(slots)


def adamw(name, grads, w, m, v):
    rows, wd = w.shape
    tr = _pack_tile(rows)
    ng = len(grads)

    def body(*refs):
        g = refs[0][...]
        for r in refs[1:ng]:
            g = g + r[...]
        w_ref, m_ref, v_ref = refs[ng:ng + 3]
        g_out, d_out, m_out, v_out = refs[ng + 3:]
        m_new = ADAM_B1 * m_ref[...] + (1.0 - ADAM_B1) * g
        v_new = ADAM_B2 * v_ref[...] + (1.0 - ADAM_B2) * jnp.square(g)
        m_hat = m_new / (1.0 - ADAM_B1 ** ADAM_STEP)
        v_hat = v_new / (1.0 - ADAM_B2 ** ADAM_STEP)
        g_out[...] = g
        d_out[...] = -ADAM_LR * (m_hat / (jnp.sqrt(v_hat) + ADAM_EPS) + ADAM_WD * w_ref[...])
        m_out[...] = m_new
        v_out[...] = v_new

    spec = pl.BlockSpec((tr, wd), lambda i: (i, 0))
    shape = jax.ShapeDtypeStruct((rows, wd), F32)
    return pl.pallas_call(body, grid=(rows // tr,), in_specs=[spec] * (ng + 3), out_specs=[spec] * 4,
                          out_shape=[shape] * 4, name=name, compiler_params=_cparams())(*grads, w, m, v)


def _shard_of(ref, name, p):
    ax = SHARD_AXIS[name]
    n = ref.shape[ax] // 4
    idx = [slice(None)] * 3
    idx[ax] = pl.ds(pl.multiple_of(p * n, 128 if ax == 2 else 16), n)
    return ref.at[tuple(idx)]


def _sem_scratch(nw):
    return [pltpu.SemaphoreType.DMA((3 * nw,)), pltpu.SemaphoreType.DMA((3 * nw,)), pltpu.SemaphoreType.DMA((nw,))]


def gather_weights(name, names, shards):
    nw = len(shards)

    def full_shape(n, s):
        return tuple(d * 4 if ax == SHARD_AXIS[n] else d for ax, d in enumerate(s.shape))

    def body(*refs):
        src, dst = refs[:nw], refs[nw:2 * nw]
        send_sems, recv_sems, local_sems = refs[2 * nw:]
        x, y, c = _my_pos()
        q = 2 * x + y
        peers = _other_chips(x, y)
        local = [pltpu.make_async_copy(src[w], _shard_of(dst[w], names[w], q), local_sems.at[w]) for w in range(nw)]
        for cp in local:
            cp.start()

        def copy(w, k, slot):
            p = peers[k]
            return pltpu.make_async_remote_copy(src_ref=src[w], dst_ref=_shard_of(dst[w], names[w], slot),
                                                send_sem=send_sems.at[3 * w + k], recv_sem=recv_sems.at[3 * w + k],
                                                device_id=(p[0], p[1], c), device_id_type=MESH_ID)
        sends = [copy(w, k, q) for w in range(nw) for k in range(3)]
        for cp in sends:
            cp.start()
        for w in range(nw):
            for k in range(3):
                copy(w, k, 2 * peers[k][0] + peers[k][1]).wait_recv()
        for cp in sends:
            cp.wait_send()
        for cp in local:
            cp.wait()

    return pl.pallas_call(body, out_shape=[jax.ShapeDtypeStruct(full_shape(n, s), s.dtype) for n, s in zip(names, shards)],
                          in_specs=[ANY] * nw, out_specs=[ANY] * nw, name=name, scratch_shapes=_sem_scratch(nw))(*shards)


def scatter_grads(name, names, grads):
    nw = len(grads)

    def shard_shape(n, s):
        return tuple(d // 4 if ax == SHARD_AXIS[n] else d for ax, d in enumerate(s.shape))

    def body(*refs):
        src, dst = refs[:nw], refs[nw:2 * nw]
        send_sems, recv_sems, local_sems = refs[2 * nw:]
        x, y, c = _my_pos()
        q = 2 * x + y
        peers = _other_chips(x, y)
        local = [pltpu.make_async_copy(_shard_of(src[w], names[w], q), dst[w].at[q], local_sems.at[w])
                 for w in range(nw)]
        for cp in local:
            cp.start()

        def copy(w, k, slot):
            p = peers[k]
            return pltpu.make_async_remote_copy(src_ref=_shard_of(src[w], names[w], 2 * p[0] + p[1]),
                                                dst_ref=dst[w].at[slot], send_sem=send_sems.at[3 * w + k],
                                                recv_sem=recv_sems.at[3 * w + k], device_id=(p[0], p[1], c),
                                                device_id_type=MESH_ID)
        sends = [copy(w, k, q) for w in range(nw) for k in range(3)]
        for cp in sends:
            cp.start()
        for w in range(nw):
            for k in range(3):
                copy(w, k, 2 * peers[k][0] + peers[k][1]).wait_recv()
        for cp in sends:
            cp.wait_send()
        for cp in local:
            cp.wait()

    return pl.pallas_call(body, out_shape=[jax.ShapeDtypeStruct((4,) + shard_shape(n, g), g.dtype)
                                           for n, g in zip(names, grads)],
                          in_specs=[ANY] * nw, out_specs=[ANY] * nw, name=name, scratch_shapes=_sem_scratch(nw))(*grads)


def swap_siblings(name, arrs):
    nw = len(arrs)

    def body(*refs):
        src, dst = refs[:nw], refs[nw:2 * nw]
        send_sems, recv_sems = refs[2 * nw:]
        x, y, c = _my_pos()
        cps = [pltpu.make_async_remote_copy(src_ref=src[w], dst_ref=dst[w], send_sem=send_sems.at[w],
                                            recv_sem=recv_sems.at[w], device_id=(x, y, 1 - c), device_id_type=MESH_ID)
               for w in range(nw)]
        for cp in cps:
            cp.start()
        for cp in cps:
            cp.wait()

    return pl.pallas_call(body, out_shape=[jax.ShapeDtypeStruct(a.shape, a.dtype) for a in arrs], in_specs=[ANY] * nw,
                          out_specs=[ANY] * nw, name=name,
                          scratch_shapes=[pltpu.SemaphoreType.DMA((nw,)), pltpu.SemaphoreType.DMA((nw,))])(*arrs)


ADAM_BLOCK_ELEMS = 256 * 1024


def adamw_shard(name, mine, sib, w, m, v):
    nl, rows, cols = w.shape
    tr = [t for t in (512, 384, 352, 256, 176, 128, 64, 32, 16) if rows % t == 0 and t * cols <= ADAM_BLOCK_ELEMS][0]

    def body(a_ref, b_ref, w_ref, m_ref, v_ref, g_out, d_out, m_out, v_out):
        def total(r):
            acc = r[0].astype(F32)
            for k in range(1, 4):
                acc = acc + r[k].astype(F32)
            return acc
        g = total(a_ref) + total(b_ref)
        m_new = ADAM_B1 * m_ref[...] + (1.0 - ADAM_B1) * g
        v_new = ADAM_B2 * v_ref[...] + (1.0 - ADAM_B2) * jnp.square(g)
        m_hat = m_new / (1.0 - ADAM_B1 ** ADAM_STEP)
        v_hat = v_new / (1.0 - ADAM_B2 ** ADAM_STEP)
        g_out[...] = g
        d_out[...] = -ADAM_LR * (m_hat / (jnp.sqrt(v_hat) + ADAM_EPS) + ADAM_WD * w_ref[...])
        m_out[...] = m_new
        v_out[...] = v_new

    slots = pl.BlockSpec((4, None, tr, cols), lambda l, i: (0, l, i, 0))
    spec = pl.BlockSpec((None, tr, cols), lambda l, i: (l, i, 0))
    shape = jax.ShapeDtypeStruct(w.shape, F32)
    return pl.pallas_call(body, grid=(nl, rows // tr), in_specs=[slots, slots, spec, spec, spec], out_specs=[spec] * 4,
                          out_shape=[shape] * 4, name=name, compiler_params=_cparams())(mine, sib, w, m, v)


def kernel(x, meta_tokens, norm_mix_g, norm_ffn_g, mla_w_down, mla_cq_norm_g, mla_ckv_norm_g, mla_w_uq, mla_w_ukv, mla_q_head_g, mla_k_head_g, mla_w_o, hgrn_w_in, hgrn_lb_logits, hgrn_o_norm_g, hgrn_w_o, s5_lam_re, s5_lam_im, s5_log_dt, s5_b_re, s5_b_im, s5_c_re, s5_c_im, s5_d, s5_w_glu, ret_w_in, ret_gn_g, ret_w_o, ffn_w_up, ffn_conv_w, ffn_conv_b, ffn_w_down, loss_target, m_meta_tokens, m_norm_mix_g, m_norm_ffn_g, m_mla_w_down, m_mla_cq_norm_g, m_mla_ckv_norm_g, m_mla_w_uq, m_mla_w_ukv, m_mla_q_head_g, m_mla_k_head_g, m_mla_w_o, m_hgrn_w_in, m_hgrn_lb_logits, m_hgrn_o_norm_g, m_hgrn_w_o, m_s5_lam_re, m_s5_lam_im, m_s5_log_dt, m_s5_b_re, m_s5_b_im, m_s5_c_re, m_s5_c_im, m_s5_d, m_s5_w_glu, m_ret_w_in, m_ret_gn_g, m_ret_w_o, m_ffn_w_up, m_ffn_conv_w, m_ffn_conv_b, m_ffn_w_down, v_meta_tokens, v_norm_mix_g, v_norm_ffn_g, v_mla_w_down, v_mla_cq_norm_g, v_mla_ckv_norm_g, v_mla_w_uq, v_mla_w_ukv, v_mla_q_head_g, v_mla_k_head_g, v_mla_w_o, v_hgrn_w_in, v_hgrn_lb_logits, v_hgrn_o_norm_g, v_hgrn_w_o, v_s5_lam_re, v_s5_lam_im, v_s5_log_dt, v_s5_b_re, v_s5_b_im, v_s5_c_re, v_s5_c_im, v_s5_d, v_s5_w_glu, v_ret_w_in, v_ret_gn_g, v_ret_w_o, v_ffn_w_up, v_ffn_conv_w, v_ffn_conv_b, v_ffn_w_down):
    vals = (x, meta_tokens, norm_mix_g, norm_ffn_g, mla_w_down, mla_cq_norm_g, mla_ckv_norm_g, mla_w_uq, mla_w_ukv, mla_q_head_g, mla_k_head_g, mla_w_o, hgrn_w_in, hgrn_lb_logits, hgrn_o_norm_g, hgrn_w_o, s5_lam_re, s5_lam_im, s5_log_dt, s5_b_re, s5_b_im, s5_c_re, s5_c_im, s5_d, s5_w_glu, ret_w_in, ret_gn_g, ret_w_o, ffn_w_up, ffn_conv_w, ffn_conv_b, ffn_w_down, loss_target, m_meta_tokens, m_norm_mix_g, m_norm_ffn_g, m_mla_w_down, m_mla_cq_norm_g, m_mla_ckv_norm_g, m_mla_w_uq, m_mla_w_ukv, m_mla_q_head_g, m_mla_k_head_g, m_mla_w_o, m_hgrn_w_in, m_hgrn_lb_logits, m_hgrn_o_norm_g, m_hgrn_w_o, m_s5_lam_re, m_s5_lam_im, m_s5_log_dt, m_s5_b_re, m_s5_b_im, m_s5_c_re, m_s5_c_im, m_s5_d, m_s5_w_glu, m_ret_w_in, m_ret_gn_g, m_ret_w_o, m_ffn_w_up, m_ffn_conv_w, m_ffn_conv_b, m_ffn_w_down, v_meta_tokens, v_norm_mix_g, v_norm_ffn_g, v_mla_w_down, v_mla_cq_norm_g, v_mla_ckv_norm_g, v_mla_w_uq, v_mla_w_ukv, v_mla_q_head_g, v_mla_k_head_g, v_mla_w_o, v_hgrn_w_in, v_hgrn_lb_logits, v_hgrn_o_norm_g, v_hgrn_w_o, v_s5_lam_re, v_s5_lam_im, v_s5_log_dt, v_s5_b_re, v_s5_b_im, v_s5_c_re, v_s5_c_im, v_s5_d, v_s5_w_glu, v_ret_w_in, v_ret_gn_g, v_ret_w_o, v_ffn_w_up, v_ffn_conv_w, v_ffn_conv_b, v_ffn_w_down)
    names = ['x'] + WEIGHTS + ['loss_target'] + ['m_' + n for n in WEIGHTS] + ['v_' + n for n in WEIGHTS]
    A = dict(zip(names, vals))
    q = 2 * lax.axis_index("x") + lax.axis_index("y")

    small_shapes = [A[n].shape for n in SMALL_SHARDED]
    got_small = gather_chips("gather_small", _pack([A[n] for n in SMALL_SHARDED], F32, 8))
    W = {n: A[n] for n in REPLICATED}
    W.update(zip(BIG, gather_weights("gather_big", BIG, [A[n].astype(BF16) for n in BIG])))
    parts_small = [_unpack(got_small[p], small_shapes) for p in range(4)]
    for k, n in enumerate(SMALL_SHARDED):
        W[n] = jnp.concatenate([parts_small[p][k] for p in range(4)], axis=SHARD_AXIS[n])

    loss, grad_x, G = local_step(A['x'][0], A['loss_target'][0], W)
    loss = lax.psum(loss, ("x", "y", "c"))

    mine = scatter_grads("grad_big_scatter", BIG, [G[n].astype(BF16) for n in BIG])
    sib = swap_siblings("grad_big_sibling", mine)
    res_big = [adamw_shard("adam_" + n, mine[k], sib[k], A[n], A['m_' + n], A['v_' + n]) for k, n in enumerate(BIG)]

    small_names = REPLICATED + SMALL_SHARDED
    full_shapes = [G[n].shape for n in small_names]
    total = sum_slots("grad_small_sum", gather_all("grad_small_gather", _pack([G[n] for n in small_names], F32, 8)))
    gs = dict(zip(small_names, _unpack(total, full_shapes)))
    for n in SMALL_SHARDED:
        ax = SHARD_AXIS[n]
        size = gs[n].shape[ax] // 4
        gs[n] = lax.dynamic_slice_in_dim(gs[n], q * size, size, axis=ax)
    pk = lambda pre: _pack([A[pre + n] for n in small_names], F32, 8)
    own_shapes = [A[n].shape for n in small_names]
    res_small = [_unpack(r, own_shapes) for r in
                 adamw("adam_small", [_pack([gs[n] for n in small_names], F32, 8)], pk(''), pk('m_'), pk('v_'))]

    out = {}
    for j, kind in enumerate(('grad_', 'delta_', 'new_m_', 'new_v_')):
        for k, n in enumerate(BIG):
            out[kind + n] = res_big[k][j]
        for k, n in enumerate(small_names):
            out[kind + n] = res_small[j][k]
    return (loss, grad_x[None]) + tuple(out[kind + n] for kind in ('grad_', 'delta_', 'new_m_', 'new_v_')
                                        for n in WEIGHTS)
```

```python
import functools
import math

import jax
import jax.numpy as jnp
from jax import lax
from jax.experimental import pallas as pl
from jax.experimental.pallas import tpu as pltpu

F32, BF16 = jnp.float32, jnp.bfloat16
HIGHEST = lax.Precision.HIGHEST
MESH_ID = pl.DeviceIdType.MESH

D = 1024
N_META = 16
PAD = 112
LEAD = PAD + N_META
EPS = 1e-6
NEG_INF = -1e30
CHUNK = 64
VMEM_LIMIT_V7X = 56 * 1024 * 1024
MM_VMEM_BUDGET = 36 * 1024 * 1024

MLA_H, MLA_NOPE, MLA_ROPE, MLA_V = 8, 128, 64, 128
MLA_QK = MLA_NOPE + MLA_ROPE
MLA_QL, MLA_KVL = 384, 256
HG_H, HG_D, HG_C = 8, 128, 16
S5_G, S5_P, S5_K = 64, 64, 16
RET_H, RET_DK, RET_DV = 4, 256, 512
FFN_F = 2816

ADAM_LR, ADAM_B1, ADAM_B2, ADAM_EPS, ADAM_WD, ADAM_STEP = 0.001, 0.9, 0.999, 1e-08, 0.01, 10

WEIGHTS = ['meta_tokens', 'norm_mix_g', 'norm_ffn_g', 'mla_w_down', 'mla_cq_norm_g', 'mla_ckv_norm_g', 'mla_w_uq',
           'mla_w_ukv', 'mla_q_head_g', 'mla_k_head_g', 'mla_w_o', 'hgrn_w_in', 'hgrn_lb_logits', 'hgrn_o_norm_g',
           'hgrn_w_o', 's5_lam_re', 's5_lam_im', 's5_log_dt', 's5_b_re', 's5_b_im', 's5_c_re', 's5_c_im', 's5_d',
           's5_w_glu', 'ret_w_in', 'ret_gn_g', 'ret_w_o', 'ffn_w_up', 'ffn_conv_w', 'ffn_conv_b', 'ffn_w_down']
SHARD_AXIS = {'meta_tokens': 1, 'mla_w_down': 1, 'mla_w_uq': 2, 'mla_w_ukv': 2, 'mla_w_o': 1, 'hgrn_w_in': 2,
              'hgrn_w_o': 1, 's5_d': 1, 's5_w_glu': 2, 'ret_w_in': 2, 'ret_gn_g': 1, 'ret_w_o': 1, 'ffn_w_up': 2,
              'ffn_conv_w': 2, 'ffn_w_down': 1}
BIG = ['mla_w_down', 'mla_w_uq', 'mla_w_ukv', 'mla_w_o', 'hgrn_w_in', 'hgrn_w_o', 's5_w_glu', 'ret_w_in', 'ret_w_o',
       'ffn_w_up', 'ffn_w_down']
SMALL_SHARDED = ['meta_tokens', 's5_d', 'ret_gn_g', 'ffn_conv_w']
REPLICATED = [n for n in WEIGHTS if n not in SHARD_AXIS]


def _cparams():
    return pltpu.CompilerParams(vmem_limit_bytes=VMEM_LIMIT_V7X)


def _dg(a, b, ca, cb):
    return lax.dot_general(a.astype(BF16), b.astype(BF16), (((ca,), (cb,)), ((), ())),
                           preferred_element_type=F32)


@jax.custom_vjp
def mm_nn(a, b):
    return _dg(a, b, 1, 0)


@jax.custom_vjp
def mm_nt(a, b):
    return _dg(a, b, 1, 1)


@jax.custom_vjp
def mm_tn(a, b):
    return _dg(a, b, 0, 0)


mm_nn.defvjp(lambda a, b: (mm_nn(a, b), (a, b)),
             lambda r, g: (mm_nt(g, r[1]).astype(r[0].dtype), mm_tn(r[0], g).astype(r[1].dtype)))
mm_nt.defvjp(lambda a, b: (mm_nt(a, b), (a, b)),
             lambda r, g: (mm_nn(g, r[1]).astype(r[0].dtype), mm_tn(g, r[0]).astype(r[1].dtype)))
mm_tn.defvjp(lambda a, b: (mm_tn(a, b), (a, b)),
             lambda r, g: (mm_nt(r[1], g).astype(r[0].dtype), mm_nn(r[0], g).astype(r[1].dtype)))


def _dot_f32(a, b):
    return jnp.dot(a, b, precision=HIGHEST, preferred_element_type=F32)


def _shift_rows(x, s, up):
    n = x.shape[0]
    r = lax.broadcasted_iota(jnp.int32, x.shape, 0)
    if up:
        return jnp.where(r < n - s, pltpu.roll(x, n - s, 0), 0.0)
    return jnp.where(r >= s, pltpu.roll(x, s, 0), 0.0)


@functools.partial(jax.custom_vjp, nondiff_argnums=(1,))
def shift_down(x, s):
    return _shift_rows(x, s, False)


shift_down.defvjp(lambda x, s: (_shift_rows(x, s, False), None), lambda s, _, g: (_shift_rows(g, s, True),))


@functools.partial(jax.custom_vjp, nondiff_argnums=(1,))
def shift_up(x, s):
    return _shift_rows(x, s, True)


shift_up.defvjp(lambda x, s: (_shift_rows(x, s, True), None), lambda s, _, g: (_shift_rows(g, s, False),))


def _swap32_impl(x):
    ax = x.ndim - 1
    lane = lax.broadcasted_iota(jnp.int32, x.shape, ax)
    return jnp.where(lane < 32, pltpu.roll(x, 96, ax), jnp.where(lane < 64, pltpu.roll(x, 32, ax), 0.0))


@jax.custom_vjp
def swap32(x):
    return _swap32_impl(x)


swap32.defvjp(lambda x: (_swap32_impl(x), None), lambda _, g: (_swap32_impl(g),))


def _rms(x, g):
    return x * lax.rsqrt(jnp.mean(x * x, axis=-1, keepdims=True) + EPS) * g


def _silu(x):
    return x * jax.nn.sigmoid(x)


def _row_ids(pid, n, shape, axis=0):
    return pid * n + lax.broadcasted_iota(jnp.int32, shape, axis)


class Arg:
    def __init__(self, arr, block, imap, diff=True, gdtype=F32):
        self.arr, self.block, self.imap, self.diff, self.gdtype = arr, block, imap, diff, gdtype


class Out:
    def __init__(self, shape, dtype, block, imap):
        self.shape, self.dtype, self.block, self.imap = shape, dtype, block, imap


def _free_axes(imap, grid):
    ng = len(grid)
    base = tuple(imap(*([0] * ng)))
    free = []
    for ax in range(ng):
        p = [0] * ng
        p[ax] = 1
        if grid[ax] > 1 and tuple(imap(*p)) == base:
            free.append(ax)
    assert free == list(range(ng - len(free), ng)), "revisited blocks must be revisited on the innermost axes"
    return free


def _pallas(name, body, grid, in_specs, out_specs, out_shape, scratch, operands, rider=None):
    if rider is None:
        return pl.pallas_call(body, grid=grid, in_specs=in_specs, out_specs=out_specs, out_shape=out_shape,
                              scratch_shapes=scratch, name=name, compiler_params=_cparams())(*operands)
    n_in, n_out, n_sc = len(in_specs), len(out_specs), len(scratch)
    r_in, r_out = len(rider.operands), len(rider.out_shapes)

    def body_with_rider(*refs):
        ins, refs = refs[:n_in], refs[n_in:]
        r_ins, refs = refs[:r_in], refs[r_in:]
        outs, refs = refs[:n_out], refs[n_out:]
        r_outs, refs = refs[:r_out], refs[r_out:]
        sc, r_sc = refs[:n_sc], refs[n_sc:]
        pids = [pl.program_id(a) for a in range(len(grid))]
        first = functools.reduce(jnp.logical_and, [p == 0 for p in pids])
        last = functools.reduce(jnp.logical_and, [p == g - 1 for p, g in zip(pids, grid)])

        @pl.when(first)
        def _():
            rider.start(r_ins, r_outs, r_sc)

        body(*ins, *outs, *sc)

        @pl.when(last)
        def _():
            rider.finish(r_ins, r_outs, r_sc)

    res = pl.pallas_call(body_with_rider, grid=grid, in_specs=list(in_specs) + [ANY] * r_in,
                         out_specs=list(out_specs) + [ANY] * r_out, out_shape=list(out_shape) + rider.out_shapes,
                         scratch_shapes=list(scratch) + rider.scratch, name=name,
                         compiler_params=_cparams())(*operands, *rider.operands)
    rider.results = list(res[n_out:])
    return res[:n_out]


def stage_fwd(name, fn, grid, args, outs, state_shape=None, rider=None):
    n_in, n_out, ng = len(args), len(outs), len(grid)

    def body(*refs):
        pids = tuple(pl.program_id(a) for a in range(ng))
        vals = [r[...] for r in refs[:n_in]]
        o_refs = refs[n_in:n_in + n_out]
        if state_shape is None:
            res = fn(pids, *vals)
        else:
            sv_ref, st_ref = refs[n_in + n_out], refs[n_in + n_out + 1]

            @pl.when(pids[-1] == 0)
            def _():
                st_ref[...] = jnp.zeros(state_shape, F32)

            s = st_ref[...]
            sv_ref[...] = s
            res = fn(pids, *vals, s)
            st_ref[...] = res[-1]
            res = res[:-1]
        for r, v in zip(o_refs, res):
            r[...] = v.astype(r.dtype)

    in_specs = [pl.BlockSpec(a.block, a.imap) for a in args]
    out_specs = [pl.BlockSpec(o.block, o.imap) for o in outs]
    out_shape = [jax.ShapeDtypeStruct(o.shape, o.dtype) for o in outs]
    scratch = []
    if state_shape is not None:
        nz = len(state_shape)
        out_specs.append(pl.BlockSpec((None, None) + tuple(state_shape), lambda i, j: (i, j) + (0,) * nz))
        out_shape.append(jax.ShapeDtypeStruct(tuple(grid) + tuple(state_shape), F32))
        scratch = [pltpu.VMEM(state_shape, F32)]
    return _pallas(name, body, grid, in_specs, out_specs, out_shape, scratch, [a.arr for a in args], rider)


def stage_bwd(name, fn, grid, args, outs, cots, state_shape=None, states=None, rider=None):
    n_in, n_out, ng = len(args), len(outs), len(grid)
    nb = grid[-1]
    rev = state_shape is not None
    didx = [k for k, a in enumerate(args) if a.diff]
    frees = [_free_axes(args[k].imap, grid) for k in didx]

    def eff(p):
        return tuple(p[:-1]) + (nb - 1 - p[-1],) if rev else tuple(p)

    def wrap(imap):
        return lambda *p: imap(*eff(p))

    def body(*refs):
        pids = tuple(pl.program_id(a) for a in range(ng))
        e = eff(pids)
        vals = [r[...] for r in refs[:n_in]]
        cts = tuple(r[...].astype(F32) for r in refs[n_in:n_in + n_out])
        pos = n_in + n_out
        if rev:
            st_in_ref = refs[pos]
            pos += 1
        g_refs = refs[pos:pos + len(didx)]
        pos += len(didx)
        dvals = [vals[k].astype(F32) for k in didx]

        def f(*dv):
            full = list(vals)
            for k, v in zip(didx, dv[:len(didx)]):
                full[k] = v
            return tuple(fn(e, *full, *dv[len(didx):]))

        if rev:
            ds_ref = refs[pos]

            @pl.when(pids[-1] == 0)
            def _():
                ds_ref[...] = jnp.zeros(state_shape, F32)

            _, vjp = jax.vjp(f, *dvals, st_in_ref[...])
            grads = vjp(cts + (ds_ref[...],))
            ds_ref[...] = grads[-1]
            grads = grads[:-1]
        else:
            _, vjp = jax.vjp(f, *dvals)
            grads = vjp(cts)
        for gref, g, free in zip(g_refs, grads, frees):
            g = g.astype(F32)
            if not free:
                gref[...] = g.astype(gref.dtype)
            else:
                first = functools.reduce(jnp.logical_and, [pids[ax] == 0 for ax in free])

                @pl.when(first)
                def _():
                    gref[...] = g

                @pl.when(jnp.logical_not(first))
                def _():
                    gref[...] += g

    in_specs = [pl.BlockSpec(a.block, wrap(a.imap)) for a in args]
    in_specs += [pl.BlockSpec(o.block, wrap(o.imap)) for o in outs]
    operands = [a.arr for a in args] + list(cots)
    scratch = []
    if rev:
        nz = len(state_shape)
        in_specs.append(pl.BlockSpec((None, None) + tuple(state_shape), lambda i, j: (i, nb - 1 - j) + (0,) * nz))
        operands.append(states)
        scratch = [pltpu.VMEM(state_shape, F32)]
    out_specs = [pl.BlockSpec(args[k].block, wrap(args[k].imap)) for k in didx]
    assert all(args[k].gdtype == F32 or not free for k, free in zip(didx, frees))
    out_shape = [jax.ShapeDtypeStruct(args[k].arr.shape, args[k].gdtype) for k in didx]
    return _pallas(name, body, grid, in_specs, out_specs, out_shape, scratch, operands, rider)


def _divisors(n, cands):
    return [c for c in cands if n % c == 0] or [n]


def _nbytes(dt):
    return jnp.dtype(dt).itemsize


def matmul(name, a, b, mode, out_dtype=F32, res=None, mask=False):
    sa, sb, so = _nbytes(a.dtype), _nbytes(b.dtype), _nbytes(out_dtype)
    if mode in ('nn', 'nt'):
        M, K = a.shape
        N = b.shape[1] if mode == 'nn' else b.shape[0]
        best = None
        for tm in _divisors(M, (1408, 1056, 768, 384, 128)):
            for tn in _divisors(N, (1408, 1024, 768, 512, 384, 256, 128)):
                est = 2 * (tm * K * sa + tn * K * sb + tm * tn * (so + (4 if res is not None else 0)))
                if est <= MM_VMEM_BUDGET and (best is None or tm * tn > best[0] * best[1]):
                    best = (tm, tn)
        tm, tn = best
        grid = (M // tm, N // tn)

        def body(*refs):
            a_ref, b_ref = refs[0], refs[1]
            o_ref = refs[-1]
            x = a_ref[...]
            rows = _row_ids(pl.program_id(0), tm, (tm, 1))
            if mask:
                x = jnp.where(rows >= PAD, x, jnp.zeros_like(x))
            acc = _dg(x, b_ref[...], 1, 0 if mode == 'nn' else 1)
            if res is not None:
                acc = refs[2][...] + jnp.where(rows >= PAD, acc, 0.0)
            o_ref[...] = acc.astype(o_ref.dtype)

        in_specs = [pl.BlockSpec((tm, K), lambda i, j: (i, 0)),
                    pl.BlockSpec((K, tn), lambda i, j: (0, j)) if mode == 'nn' else
                    pl.BlockSpec((tn, K), lambda i, j: (j, 0))]
        ops = [a, b]
        if res is not None:
            in_specs.append(pl.BlockSpec((tm, tn), lambda i, j: (i, j)))
            ops.append(res)
        return pl.pallas_call(body, grid=grid, in_specs=in_specs,
                              out_specs=pl.BlockSpec((tm, tn), lambda i, j: (i, j)),
                              out_shape=jax.ShapeDtypeStruct((M, N), out_dtype), name=name,
                              compiler_params=_cparams())(*ops)
    assert mode == 'tn' and res is None
    M, K = a.shape
    N = b.shape[1]
    best = None
    for tk in _divisors(K, (1408, 1024, 768, 512, 384, 256, 128)):
        for tn in _divisors(N, (1408, 1024, 768, 512, 384, 256, 128)):
            est = 2 * (M * tk * sa + M * tn * sb + tk * tn * so)
            if est <= MM_VMEM_BUDGET and (best is None or tk * tn > best[0] * best[1]):
                best = (tk, tn)
    tk, tn = best

    def body_t(a_ref, b_ref, o_ref):
        y = b_ref[...]
        if mask:
            rows = lax.broadcasted_iota(jnp.int32, (M, 1), 0)
            y = jnp.where(rows >= PAD, y, jnp.zeros_like(y))
        o_ref[...] = _dg(a_ref[...], y, 0, 0).astype(o_ref.dtype)

    return pl.pallas_call(body_t, grid=(K // tk, N // tn),
                          in_specs=[pl.BlockSpec((M, tk), lambda i, j: (0, i)),
                                    pl.BlockSpec((M, tn), lambda i, j: (0, j))],
                          out_specs=pl.BlockSpec((tk, tn), lambda i, j: (i, j)),
                          out_shape=jax.ShapeDtypeStruct((K, N), out_dtype), name=name,
                          compiler_params=_cparams())(a, b)


def linear_bwd(name, act, w, dy, mask=False):
    return (matmul(name + "_da", dy, w, 'nt', mask=mask),
            matmul(name + "_dw", act, dy, 'tn', out_dtype=BF16, mask=mask))


def _row_tile(T):
    return _divisors(T, (384, 128))[0]


def _rows(arr, tm, gdtype=F32):
    return Arg(arr, (tm, arr.shape[1]), lambda i: (i, 0), gdtype=gdtype)


def _const(arr, diff=True):
    return Arg(arr, arr.shape, lambda *p: (0,) * arr.ndim, diff)


def _norm_fn(pids, h, g):
    return (_rms(h, g),)


def _norm_bwd_fn(pids, h, g):
    return (_rms(h, g), h)


def norm_fwd(name, h, g, dtype):
    T = h.shape[0]
    tm = _row_tile(T)
    return stage_fwd(name, _norm_fn, (T // tm,), [_rows(h, tm), _const(g)],
                     [Out((T, D), dtype, (tm, D), lambda i: (i, 0))])[0]


def norm_bwd(name, h, g, da, dh):
    T = h.shape[0]
    tm = _row_tile(T)
    o = Out((T, D), F32, (tm, D), lambda i: (i, 0))
    return stage_bwd(name, _norm_bwd_fn, (T // tm,), [_rows(h, tm), _const(g)], [o, o], [da, dh])


def _ffn_act_fn(pids, u, cw, cb):
    c = cw[2:3] * u + cw[1:2] * shift_down(u, 1) + cw[0:1] * shift_down(u, 2) + cb
    return (_silu(c[:, :128]) * c[:, 128:],)


def _ffn_act_args(u, cw, cb):
    T = u.shape[0]
    nt = FFN_F // 128
    args = [Arg(u, (T, 256), lambda j: (0, j), gdtype=BF16), Arg(cw, (3, 256), lambda j: (0, j)),
            Arg(cb, (1, 256), lambda j: (0, j))]
    outs = [Out((T, FFN_F), BF16, (T, 128), lambda j: (0, j))]
    return (nt,), args, outs


def _interleave_cols(w, n_parts, tile=128):
    lead = w.shape[:-1]
    n = w.shape[-1] // (n_parts * tile)
    k = len(lead)
    return w.reshape(lead + (n_parts, n, tile)).transpose(tuple(range(k)) + (k + 1, k, k + 2)).reshape(w.shape)


def _deinterleave_cols(w, n_parts, tile=128):
    lead = w.shape[:-1]
    n = w.shape[-1] // (n_parts * tile)
    k = len(lead)
    return w.reshape(lead + (n, n_parts, tile)).transpose(tuple(range(k)) + (k + 1, k, k + 2)).reshape(w.shape)


def ffn_layer(i, h, g, w_up, cw, cb, w_down):
    b = norm_fwd(f"ffn{i}_norm", h, g, BF16)
    u = matmul(f"ffn{i}_up", b, w_up, 'nn')
    grid, args, outs = _ffn_act_args(u, cw, cb)
    p = stage_fwd(f"ffn{i}_act", _ffn_act_fn, grid, args, outs)[0]
    h_new = matmul(f"ffn{i}_down", p, w_down, 'nn', res=h)

    def bwd(dh):
        dp, dwd = linear_bwd(f"ffn{i}_down_b", p, w_down, dh, mask=True)
        du, dcw, dcb = stage_bwd(f"ffn{i}_act_b", _ffn_act_fn, grid, args, outs, [dp])
        db, dwu = linear_bwd(f"ffn{i}_up_b", b, w_up, du)
        dh2, dg = norm_bwd(f"ffn{i}_norm_b", h, g, db, dh)
        return dh2, dict(g=dg, w_up=dwu, cw=dcw, cb=dcb, w_down=dwd)

    return h_new, bwd


def _mla_latent_fn(pids, down, gcq, gckv):
    cq = _rms(down[:, :MLA_QL], gcq)
    ckv = _rms(down[:, MLA_QL:MLA_QL + MLA_KVL], gckv)
    return cq, ckv, down[:, MLA_QL + MLA_KVL:]


def _rope64(x, cos, sin_signed):
    return x * cos + swap32(x) * sin_signed


def _mla_heads_fn(pids, qraw, kv, kpe, gqn, gqr, gkn, gkr, cos, sin_signed):
    qn, qr = qraw[:, :128], qraw[:, 128:]
    rq = lax.rsqrt((jnp.sum(qn * qn, -1, keepdims=True) + jnp.sum(qr * qr, -1, keepdims=True)) / MLA_QK + EPS)
    q = jnp.concatenate([qn * rq * gqn, _rope64(qr * rq * gqr, cos, sin_signed)], axis=1)
    kn, v = kv[:, :128], kv[:, 128:]
    rk = lax.rsqrt((jnp.sum(kn * kn, -1, keepdims=True) + jnp.sum(kpe * kpe, -1, keepdims=True)) / MLA_QK + EPS)
    k = jnp.concatenate([kn * rk * gkn, _rope64(kpe * rk * gkr, cos, sin_signed)], axis=1)
    return q, k, v


def _chunk_id(r):
    return jnp.where(r < LEAD, 0, 1 + lax.shift_right_arithmetic(r - LEAD, 6))


def _make_attn_fn(tq, T):
    def attn_fn(pids, q, k, v):
        s = mm_nt(q, k) * (MLA_QK ** -0.5)
        qrow = _row_ids(pids[1], tq, (tq, 1))
        krow = lax.broadcasted_iota(jnp.int32, (1, T), 1)
        ok = jnp.logical_and(_chunk_id(krow) <= _chunk_id(qrow), krow >= PAD)
        s = jnp.where(ok, s, NEG_INF)
        m = lax.stop_gradient(jnp.max(s, axis=-1, keepdims=True))
        p = jnp.exp(s - m)
        p = p / jnp.sum(p, axis=-1, keepdims=True)
        return (mm_nn(p, v),)
    return attn_fn


def mla_mixer(h, g, w, tabs, rider=None):
    T = h.shape[0]
    tm = _row_tile(T)
    nt = T // tm
    a = norm_fwd("mla_norm", h, g, BF16)
    down = matmul("mla_down", a, w['w_down'], 'nn')
    lat_args = [_rows(down, tm, BF16), _const(w['gcq']), _const(w['gckv'])]
    lat_outs = [Out((T, MLA_QL), BF16, (tm, MLA_QL), lambda i: (i, 0)),
                Out((T, MLA_KVL), BF16, (tm, MLA_KVL), lambda i: (i, 0)),
                Out((T, 128), F32, (tm, 128), lambda i: (i, 0))]
    cq, ckv, kpe = stage_fwd("mla_latent", _mla_latent_fn, (nt,), lat_args, lat_outs)
    qraw = matmul("mla_uq", cq, w['w_uq'], 'nn')
    kv = matmul("mla_ukv", ckv, w['w_ukv'], 'nn')
    hd_args = [Arg(qraw, (tm, 256), lambda i, hh: (i, hh), gdtype=BF16),
               Arg(kv, (tm, 256), lambda i, hh: (i, hh), gdtype=BF16),
               Arg(kpe, (tm, 128), lambda i, hh: (i, 0)),
               _const(w['gqn']), _const(w['gqr']), _const(w['gkn']), _const(w['gkr']),
               Arg(tabs['cos_a'], (tm, 128), lambda i, hh: (i, 0), False),
               Arg(tabs['sin_a'], (tm, 128), lambda i, hh: (i, 0), False)]
    hd_outs = [Out((MLA_H, T, 256), BF16, (None, tm, 256), lambda i, hh: (hh, i, 0)),
               Out((MLA_H, T, 256), BF16, (None, tm, 256), lambda i, hh: (hh, i, 0)),
               Out((MLA_H, T, 128), BF16, (None, tm, 128), lambda i, hh: (hh, i, 0))]
    q, k, v = stage_fwd("mla_heads", _mla_heads_fn, (nt, MLA_H), hd_args, hd_outs)
    tq = 128
    attn_fn = _make_attn_fn(tq, T)
    at_args = [Arg(q, (None, tq, 256), lambda hh, j: (hh, j, 0)), Arg(k, (None, T, 256), lambda hh, j: (hh, 0, 0)),
               Arg(v, (None, T, 128), lambda hh, j: (hh, 0, 0))]
    at_outs = [Out((T, D), BF16, (tq, 128), lambda hh, j: (j, hh))]
    o = stage_fwd("mla_attn", attn_fn, (MLA_H, T // tq), at_args, at_outs, rider=rider)[0]
    h_new = matmul("mla_o", o, w['w_o'], 'nn', res=h)

    def bwd(dh, rider=None):
        do, dwo = linear_bwd("mla_o_b", o, w['w_o'], dh, mask=True)
        dq, dk, dv = stage_bwd("mla_attn_b", attn_fn, (MLA_H, T // tq), at_args, at_outs, [do], rider=rider)
        dqraw, dkv, dkpe, dgqn, dgqr, dgkn, dgkr = stage_bwd("mla_heads_b", _mla_heads_fn, (nt, MLA_H), hd_args,
                                                             hd_outs, [dq, dk, dv])
        dcq, dwuq = linear_bwd("mla_uq_b", cq, w['w_uq'], dqraw)
        dckv, dwukv = linear_bwd("mla_ukv_b", ckv, w['w_ukv'], dkv)
        ddown, dgcq, dgckv = stage_bwd("mla_latent_b", _mla_latent_fn, (nt,), lat_args, lat_outs, [dcq, dckv, dkpe])
        da, dwdown = linear_bwd("mla_down_b", a, w['w_down'], ddown)
        dh2, dg = norm_bwd("mla_norm_b", h, g, da, dh)
        return dh2, dict(g=dg, w_down=dwdown, gcq=dgcq, gckv=dgckv, w_uq=dwuq, w_ukv=dwukv, gqn=dgqn, gqr=dgqr,
                         gkn=dgkn, gkr=dgkr, w_o=dwo)

    return h_new, bwd


HG_R = 128


def _hgrn_fn(pids, z, lb, go, st):
    R = z.shape[0]
    zq, zf, zi, zg = z[:, :128], z[:, 128:256], z[:, 256:384], z[:, 384:]
    assert R == 128
    q = _silu(zq)
    fg = lb + (1.0 - lb) * jax.nn.sigmoid(zf)
    logf = jnp.log(fg)
    k = 1.0 - fg
    row = lax.broadcasted_iota(jnp.int32, logf.shape, 0)
    pos = row & (HG_C - 1)
    cum, rev = logf, logf
    for d in (1, 2, 4, 8):
        cum = cum + jnp.where(pos >= d, shift_down(cum, d), 0.0)
        rev = rev + jnp.where(pos < HG_C - d, shift_up(rev, d), 0.0)
    cums, tots = [cum], [cum + rev - logf]
    for s in (16, 32, 64):
        odd = (row & s) != 0
        before = shift_down(tots[-1], s)
        cums.append(cums[-1] + jnp.where(odd, before, 0.0))
        tots.append(tots[-1] + jnp.where(odd, before, shift_up(tots[-1], s)))
    t = lax.broadcasted_iota(jnp.int32, (R, R), 0)
    j = lax.broadcasted_iota(jnp.int32, (R, R), 1)
    sh = lax.shift_right_arithmetic
    a = jnp.where(jnp.logical_and(sh(t, 4) == sh(j, 4), j <= t), mm_nt(q * jnp.exp(cum), k * jnp.exp(-cum)), 0.0)
    for n, s in enumerate((16, 32, 64)):
        m = jnp.logical_and(sh(t, 5 + n) == sh(j, 5 + n), jnp.logical_and((t & s) != 0, (j & s) == 0))
        a = a + jnp.where(m, mm_nt(q * jnp.exp(cums[n]), k * jnp.exp(tots[n] - cums[n])), 0.0)
    o = mm_nn(a, zi) + mm_nt(q * jnp.exp(cums[3]), st)
    st = st * jnp.exp(tots[3][0:1, :]) + mm_tn(zi, k * jnp.exp(tots[3] - cums[3]))
    return _rms(o, go) * _silu(zg), st


def hgrn_mixer(h, g, w, rider=None):
    T = h.shape[0]
    a = norm_fwd("hgrn_norm", h, g, BF16)
    z = matmul("hgrn_in", a, w['w_in'], 'nn')
    grid = (HG_H, T // HG_R)
    args = [Arg(z, (HG_R, 512), lambda hh, j: (j, hh), gdtype=BF16), Arg(w['lb'], (1, 128), lambda hh, j: (0, hh)),
            _const(w['go'])]
    outs = [Out((T, D), BF16, (HG_R, 128), lambda hh, j: (j, hh))]
    o, states = stage_fwd("hgrn_gla", _hgrn_fn, grid, args, outs, state_shape=(HG_D, HG_D), rider=rider)
    h_new = matmul("hgrn_o", o, w['w_o'], 'nn', res=h)

    def bwd(dh, rider=None):
        do, dwo = linear_bwd("hgrn_o_b", o, w['w_o'], dh, mask=True)
        dz, dlb, dgo = stage_bwd("hgrn_gla_b", _hgrn_fn, grid, args, outs, [do], state_shape=(HG_D, HG_D),
                                 states=states, rider=rider)
        da, dwin = linear_bwd("hgrn_in_b", a, w['w_in'], dz)
        dh2, dg = norm_bwd("hgrn_norm_b", h, g, da, dh)
        return dh2, dict(g=dg, w_in=dwin, lb=dlb, go=dgo, w_o=dwo)

    return h_new, bwd


S5_R = 128
S5_W = 512
S5_SLABS = D // 128


def _cmul(ar, ai, br, bi):
    return ar * br - ai * bi, ar * bi + ai * br


def _s5_scan(br, bi, tab, cr, ci, reverse):
    R, W = br.shape
    G = R // 8
    xr, xi = br.reshape(G, 8, W), bi.reshape(G, 8, W)
    for n, d in enumerate((1, 2, 4)):
        sh = (8 - d) if reverse else d
        mr, mi = _cmul(tab[2 * n][None], tab[2 * n + 1][None], pltpu.roll(xr, sh, 1), pltpu.roll(xi, sh, 1))
        xr, xi = xr + mr, xi + mi
    pr, pi = tab[6], tab[7]
    edge = 0 if reverse else 7
    out_r, out_i = [None] * G, [None] * G
    for g in (range(G - 1, -1, -1) if reverse else range(G)):
        ar, ai = _cmul(pr, pi, cr, ci)
        gr, gi = xr[g] + ar, xi[g] + ai
        cr, ci = gr[edge:edge + 1], gi[edge:edge + 1]
        out_r[g], out_i[g] = gr, gi
    return jnp.concatenate(out_r, axis=0), jnp.concatenate(out_i, axis=0), cr, ci


def s5_scan_fwd(a, bb, cb, tab, rider=None):
    T = a.shape[0]
    nb = T // S5_R

    def body(a_ref, bb_ref, cb_ref, tab_ref, y_ref, xs_ref, c_ref):
        @pl.when(pl.program_id(1) == 0)
        def _():
            c_ref[...] = jnp.zeros(c_ref.shape, F32)

        bu = _dg(a_ref[...], bb_ref[...], 1, 0)
        t = tab_ref[...]
        xr, xi, cr, ci = _s5_scan(bu[:, :S5_W], bu[:, S5_W:], t, c_ref[0:1, :S5_W], c_ref[0:1, S5_W:], False)
        x = jnp.concatenate([xr, xi], axis=1)
        xs_ref[...] = x
        y_ref[...] = _dg(x, cb_ref[...], 1, 0)
        c_ref[0:1, :] = jnp.concatenate([cr, ci], axis=1)

    return _pallas(
        "s5_scan", body, (S5_SLABS, nb),
        [pl.BlockSpec((S5_R, 128), lambda j, i: (i, j)),
         pl.BlockSpec((None, 128, 2 * S5_W), lambda j, i: (j, 0, 0)),
         pl.BlockSpec((None, 2 * S5_W, 128), lambda j, i: (j, 0, 0)),
         pl.BlockSpec((None, 10, 8, S5_W), lambda j, i: (j, 0, 0, 0))],
        [pl.BlockSpec((S5_R, 128), lambda j, i: (i, j)),
         pl.BlockSpec((None, S5_R, 2 * S5_W), lambda j, i: (j, i, 0))],
        [jax.ShapeDtypeStruct((T, D), F32), jax.ShapeDtypeStruct((S5_SLABS, T, 2 * S5_W), F32)],
        [pltpu.VMEM((8, 2 * S5_W), F32)], [a, bb, cb, tab], rider)


def s5_scan_bwd(a, bb, cb, tab_rev, xs, dy, rider=None):
    T = a.shape[0]
    nb = T // S5_R
    rg = S5_R // 8

    def body(a_ref, dy_ref, xs_ref, xp_ref, bb_ref, cb_ref, tab_ref, da_ref, dbb_ref, dcb_ref, dab_ref, c_ref):
        i = pl.program_id(1)

        @pl.when(i == 0)
        def _():
            c_ref[...] = jnp.zeros(c_ref.shape, F32)

        dy_v = dy_ref[...]
        x = xs_ref[...]
        dxo = _dg(dy_v, cb_ref[...], 1, 1)
        gr, gi, cr, ci = _s5_scan(dxo[:, :S5_W], dxo[:, S5_W:], tab_ref[...], c_ref[0:1, :S5_W], c_ref[0:1, S5_W:], True)
        c_ref[0:1, :] = jnp.concatenate([cr, ci], axis=1)
        g = jnp.concatenate([gr, gi], axis=1)
        da_ref[...] = _dg(g, bb_ref[...], 1, 1)
        dbb = _dg(a_ref[...], g, 0, 0)
        dcb = _dg(x, dy_v, 0, 0)
        first_tile = i == nb - 1
        prev_last = jnp.where(first_tile, 0.0, xp_ref[7:8, :])
        rows = lax.broadcasted_iota(jnp.int32, x.shape, 0)
        xp = jnp.where(rows == 0, prev_last, pltpu.roll(x, 1, 0))
        xpr, xpi = xp[:, :S5_W], xp[:, S5_W:]
        dar = (gr * xpr + gi * xpi).reshape(rg, 8, S5_W).sum(axis=0)
        dai = (gi * xpr - gr * xpi).reshape(rg, 8, S5_W).sum(axis=0)
        dab = jnp.concatenate([dar, dai], axis=1)

        @pl.when(i == 0)
        def _():
            dbb_ref[...] = dbb
            dcb_ref[...] = dcb
            dab_ref[...] = dab

        @pl.when(i != 0)
        def _():
            dbb_ref[...] += dbb
            dcb_ref[...] += dcb
            dab_ref[...] += dab

    def prev_rows(j, i):
        return (j, jnp.maximum((nb - 1 - i) * rg - 1, 0), 0)

    return _pallas(
        "s5_scan_b", body, (S5_SLABS, nb),
        [pl.BlockSpec((S5_R, 128), lambda j, i: (nb - 1 - i, j)),
         pl.BlockSpec((S5_R, 128), lambda j, i: (nb - 1 - i, j)),
         pl.BlockSpec((None, S5_R, 2 * S5_W), lambda j, i: (j, nb - 1 - i, 0)),
         pl.BlockSpec((None, 8, 2 * S5_W), prev_rows),
         pl.BlockSpec((None, 128, 2 * S5_W), lambda j, i: (j, 0, 0)),
         pl.BlockSpec((None, 2 * S5_W, 128), lambda j, i: (j, 0, 0)),
         pl.BlockSpec((None, 10, 8, S5_W), lambda j, i: (j, 0, 0, 0))],
        [pl.BlockSpec((S5_R, 128), lambda j, i: (nb - 1 - i, j)),
         pl.BlockSpec((None, 128, 2 * S5_W), lambda j, i: (j, 0, 0)),
         pl.BlockSpec((None, 2 * S5_W, 128), lambda j, i: (j, 0, 0)),
         pl.BlockSpec((None, 8, 2 * S5_W), lambda j, i: (j, 0, 0))],
        [jax.ShapeDtypeStruct((T, D), F32), jax.ShapeDtypeStruct((S5_SLABS, 128, 2 * S5_W), F32),
         jax.ShapeDtypeStruct((S5_SLABS, 2 * S5_W, 128), F32), jax.ShapeDtypeStruct((S5_SLABS, 8, 2 * S5_W), F32)],
        [pltpu.VMEM((8, 2 * S5_W), F32)], [a, dy, xs, xs, bb, cb, tab_rev], rider)


def _s5_discretise(lam_re, lam_im, log_dt, b_re, b_im, c_re, c_im):
    dt = jnp.exp(log_dt)[:, None]
    mag = jnp.exp(lam_re * dt)
    abar_re = mag * jnp.cos(lam_im * dt)
    abar_im = mag * jnp.sin(lam_im * dt)
    den = lam_re * lam_re + lam_im * lam_im
    zoh_re = ((abar_re - 1.0) * lam_re + abar_im * lam_im) / den
    zoh_im = (abar_im * lam_re - (abar_re - 1.0) * lam_im) / den
    bbar_re = zoh_re[..., None] * b_re - zoh_im[..., None] * b_im
    bbar_im = zoh_re[..., None] * b_im + zoh_im[..., None] * b_re
    eye = jnp.eye(8, dtype=F32)

    def in_map(bbar):
        t = bbar.reshape(8, 8, S5_P, S5_K).transpose(0, 1, 3, 2)
        return (t[:, :, :, None, :] * eye[None, :, None, :, None]).reshape(8, 8 * S5_K, 8 * S5_P)

    def out_map(c):
        t = c.reshape(8, 8, S5_K, S5_P).transpose(0, 1, 3, 2)
        return (t[:, :, :, None, :] * eye[None, :, None, :, None]).reshape(8, 8 * S5_P, 8 * S5_K)

    bb = jnp.concatenate([in_map(bbar_re), in_map(bbar_im)], axis=2)
    cb = jnp.concatenate([out_map(c_re), -out_map(c_im)], axis=1)
    return bb, cb, abar_re.reshape(8, S5_W), abar_im.reshape(8, S5_W)


def _s5_tables(ar, ai, reverse):
    if reverse:
        ai = -ai
    pw = [(jnp.ones_like(ar), jnp.zeros_like(ar))]
    for _ in range(8):
        pw.append(_cmul(pw[-1][0], pw[-1][1], ar, ai))
    r = jnp.arange(8)[None, :, None]
    rows = []
    for d in (1, 2, 4):
        keep = (r <= 7 - d) if reverse else (r >= d)
        rows += [jnp.where(keep, pw[d][0][:, None, :], 0.0), jnp.where(keep, pw[d][1][:, None, :], 0.0)]
    order = [8 - k for k in range(8)] if reverse else [k + 1 for k in range(8)]
    rows += [jnp.stack([pw[n][0] for n in order], axis=1), jnp.stack([pw[n][1] for n in order], axis=1)]
    rows += [jnp.broadcast_to(pw[8][0][:, None, :], (8, 8, S5_W)), jnp.broadcast_to(pw[8][1][:, None, :], (8, 8, S5_W))]
    return jnp.stack(rows, axis=1)


def _s5_act_fn(pids, yc, a, dskip):
    return (jax.nn.gelu(yc + dskip * a),)


def _make_glu_res_fn(tm):
    def glu_res_fn(pids, zz, h):
        rows = _row_ids(pids[0], tm, (tm, 1))
        return (h + jnp.where(rows >= PAD, zz[:, :D] * jax.nn.sigmoid(zz[:, D:]), 0.0),)
    return glu_res_fn


def s5_mixer(h, g, w, rider=None):
    T = h.shape[0]
    tm = _row_tile(T)
    nt = T // tm
    a = norm_fwd("s5_norm", h, g, F32)
    ssm = [w[n] for n in ('lam_re', 'lam_im', 'log_dt', 'b_re', 'b_im', 'c_re', 'c_im')]
    (bb, cb, ar, ai), disc_vjp = jax.vjp(_s5_discretise, *ssm)
    yc, xs = s5_scan_fwd(a, bb, cb, _s5_tables(ar, ai, False), rider=rider)
    row = lambda arr: _rows(arr, tm)
    act_args = [row(yc), row(a), _const(w['dskip'])]
    act_outs = [Out((T, D), BF16, (tm, D), lambda i: (i, 0))]
    y = stage_fwd("s5_act", _s5_act_fn, (nt,), act_args, act_outs)[0]
    zz = matmul("s5_glu", y, w['w_glu'], 'nn')
    glu_fn = _make_glu_res_fn(tm)
    glu_args = [_rows(zz, tm, BF16), row(h)]
    glu_outs = [Out((T, D), F32, (tm, D), lambda i: (i, 0))]
    h_new = stage_fwd("s5_gate", glu_fn, (nt,), glu_args, glu_outs)[0]

    def bwd(dh, rider=None):
        dzz, dh_res = stage_bwd("s5_gate_b", glu_fn, (nt,), glu_args, glu_outs, [dh])
        dy, dwglu = linear_bwd("s5_glu_b", y, w['w_glu'], dzz)
        dyc, da1, ddskip = stage_bwd("s5_act_b", _s5_act_fn, (nt,), act_args, act_outs, [dy])
        da2, dbb, dcb, dab = s5_scan_bwd(a, bb, cb, _s5_tables(ar, ai, True), xs, dyc, rider=rider)
        dab = dab.sum(axis=1)
        dssm = disc_vjp((dbb, dcb, dab[:, :S5_W], dab[:, S5_W:]))
        dh2, dg = _s5_norm_bwd(h, g, da1, da2, dh_res, tm)
        grads = dict(zip(('lam_re', 'lam_im', 'log_dt', 'b_re', 'b_im', 'c_re', 'c_im'), dssm))
        grads.update(g=dg, dskip=ddskip, w_glu=dwglu)
        return dh2, grads

    return h_new, bwd


def _norm3_bwd_fn(pids, h, g):
    a = _rms(h, g)
    return a, a, h


def _s5_norm_bwd(h, g, da1, da2, dh, tm):
    T = h.shape[0]
    o = Out((T, D), F32, (tm, D), lambda i: (i, 0))
    return stage_bwd("s5_norm_b", _norm3_bwd_fn, (T // tm,), [_rows(h, tm), _const(g)], [o, o, o], [da1, da2, dh])


RET_R = 128


def _rope256(x, cos, sin):
    x1, x2 = x[:, :128], x[:, 128:]
    return jnp.concatenate([x1 * cos - x2 * sin, x1 * sin + x2 * cos], axis=1)


def _ret_fn(pids, z, gn, cos, sin, dmat, qdec, kdec, cdec, st):
    R = z.shape[0]
    q = _rope256(z[:, :256], cos, sin)
    k = _rope256(z[:, 256:512], cos, sin) * (RET_DK ** -0.5)
    v, gate = z[:, 512:1024], z[:, 1024:]
    outs = []
    for cc in range(R // CHUNK):
        lo = cc * CHUNK
        qc, kc, vc = q[lo:lo + CHUNK], k[lo:lo + CHUNK], v[lo:lo + CHUNK]
        outs.append(mm_nn(mm_nt(qc, kc) * dmat, vc) + mm_nn(qc * qdec, st))
        st = st * cdec + mm_tn(kc * kdec, vc)
    o = jnp.concatenate(outs, axis=0)
    mu = jnp.mean(o, axis=-1, keepdims=True)
    var = jnp.mean(jnp.square(o - mu), axis=-1, keepdims=True)
    o = (o - mu) * lax.rsqrt(var + EPS)
    return o * gn * _silu(gate), st


def ret_mixer(h, g, w, tabs, rider=None):
    T = h.shape[0]
    a = norm_fwd("ret_norm", h, g, BF16)
    z = matmul("ret_in", a, w['w_in'], 'nn')
    grid = (RET_H, T // RET_R)
    hw = RET_DK * 2 + RET_DV * 2
    args = [Arg(z, (RET_R, hw), lambda hh, j: (j, hh), gdtype=BF16), Arg(w['gn'], (1, RET_DV), lambda hh, j: (0, hh)),
            Arg(tabs['cos_d'], (RET_R, 128), lambda hh, j: (j, 0), False),
            Arg(tabs['sin_d'], (RET_R, 128), lambda hh, j: (j, 0), False),
            Arg(tabs['ret_dmat'], (None, CHUNK, CHUNK), lambda hh, j: (hh, 0, 0), False),
            Arg(tabs['ret_qdec'], (None, CHUNK, 1), lambda hh, j: (hh, 0, 0), False),
            Arg(tabs['ret_kdec'], (None, CHUNK, 1), lambda hh, j: (hh, 0, 0), False),
            Arg(tabs['ret_cdec'], (None, 1, 1), lambda hh, j: (hh, 0, 0), False)]
    outs = [Out((T, RET_H * RET_DV), BF16, (RET_R, RET_DV), lambda hh, j: (j, hh))]
    o, states = stage_fwd("ret_chunks", _ret_fn, grid, args, outs, state_shape=(RET_DK, RET_DV), rider=rider)
    h_new = matmul("ret_o", o, w['w_o'], 'nn', res=h)

    def bwd(dh, rider=None):
        do, dwo = linear_bwd("ret_o_b", o, w['w_o'], dh, mask=True)
        dz, dgn = stage_bwd("ret_chunks_b", _ret_fn, grid, args, outs, [do], state_shape=(RET_DK, RET_DV),
                            states=states, rider=rider)
        da, dwin = linear_bwd("ret_in_b", a, w['w_in'], dz)
        dh2, dg = norm_bwd("ret_norm_b", h, g, da, dh)
        return dh2, dict(g=dg, w_in=dwin, gn=dgn, w_o=dwo)

    return h_new, bwd


def loss_head(h, tgt):
    T = h.shape[0]
    tm = _row_tile(T)

    def body(h_ref, t_ref, loss_ref, dh_ref):
        i = pl.program_id(0)
        rows = _row_ids(i, tm, (tm, 1))
        err = jnp.where(rows >= LEAD, h_ref[...] - t_ref[...], 0.0)
        dh_ref[...] = err * (1.0 / D)
        part = jnp.full((8, 128), 0.5 * jnp.sum(jnp.sum(err * err, axis=1, keepdims=True) * (1.0 / D)), F32)

        @pl.when(i == 0)
        def _():
            loss_ref[...] = part

        @pl.when(i != 0)
        def _():
            loss_ref[...] += part

    loss, dh = pl.pallas_call(
        body, grid=(T // tm,),
        in_specs=[pl.BlockSpec((tm, D), lambda i: (i, 0)), pl.BlockSpec((tm, D), lambda i: (i, 0))],
        out_specs=[pl.BlockSpec((8, 128), lambda i: (0, 0)), pl.BlockSpec((tm, D), lambda i: (i, 0))],
        out_shape=[jax.ShapeDtypeStruct((8, 128), F32), jax.ShapeDtypeStruct((T, D), F32)], name="loss_head",
        compiler_params=_cparams())(h, tgt)
    return loss[0, 0], dh


def _tables(T):
    pos = jnp.maximum(jnp.arange(T, dtype=jnp.int32) - PAD, 0).astype(F32)

    def cs(dim):
        inv_freq = 1.0 / (10000.0 ** (jnp.arange(0, dim, 2, dtype=F32) / dim))
        ang = pos[:, None] * inv_freq[None, :]
        return jnp.cos(ang), jnp.sin(ang)

    ca, sa = cs(MLA_ROPE)
    zeros = jnp.zeros((T, 64), F32)
    cd, sd = cs(RET_DK)
    log_gamma = jnp.log(1.0 - jnp.exp2(-5.0 - jnp.arange(RET_H, dtype=F32)))
    p = jnp.arange(CHUNK, dtype=F32)
    diff = p[:, None] - p[None, :]
    dmat = jnp.where(diff >= 0, jnp.exp(diff[None] * log_gamma[:, None, None]), 0.0)
    return dict(cos_a=jnp.concatenate([ca, ca, zeros], axis=1), sin_a=jnp.concatenate([-sa, sa, zeros], axis=1),
                cos_d=cd, sin_d=sd, ret_dmat=dmat,
                ret_qdec=jnp.exp((p[None, :] + 1.0) * log_gamma[:, None])[..., None],
                ret_kdec=jnp.exp((CHUNK - 1.0 - p[None, :]) * log_gamma[:, None])[..., None],
                ret_cdec=jnp.exp(CHUNK * log_gamma)[:, None, None])


def _hgrn_lower_bound(logits):
    lb_cum = jnp.cumsum(jax.nn.softmax(logits, axis=0), axis=0)
    return (lb_cum - lb_cum[0:1])[1:2]


def _uq_to_heads(w):
    t = w.reshape(w.shape[0], MLA_H, MLA_QK)
    return jnp.pad(t, ((0, 0), (0, 0), (0, 256 - MLA_QK))).reshape(w.shape[0], MLA_H * 256)


def _uq_from_heads(g):
    return g.reshape(g.shape[0], MLA_H, 256)[:, :, :MLA_QK].reshape(g.shape[0], MLA_H * MLA_QK)


def _head_interleave(w, widths, heads):
    parts, lo = [], 0
    for wd in widths:
        parts.append(w[:, lo:lo + heads * wd].reshape(w.shape[0], heads, wd))
        lo += heads * wd
    return jnp.concatenate(parts, axis=2).reshape(w.shape[0], -1)


def _head_deinterleave(g, widths, heads):
    t = g.reshape(g.shape[0], heads, sum(widths))
    parts, lo = [], 0
    for wd in widths:
        parts.append(t[:, :, lo:lo + wd].reshape(g.shape[0], heads * wd))
        lo += wd
    return jnp.concatenate(parts, axis=1)


HG_WIDTHS = (128, 128, 128, 128)
RET_WIDTHS = (RET_DK, RET_DK, RET_DV, RET_DV)


def _split_head_gain(g):
    return g[:, :128], jnp.pad(g[:, 128:], ((0, 0), (0, 64)))


def _join_head_gain(dn, dr):
    return jnp.concatenate([dn, dr[:, :64]], axis=1)


def local_step(x, target, W, ex):
    S = x.shape[0]
    T = S + LEAD
    tabs = _tables(T)
    h = jnp.concatenate([jnp.zeros((PAD, D), F32), W['meta_tokens'], x], axis=0)
    tgt = jnp.concatenate([jnp.zeros((LEAD, D), F32), target], axis=0)

    gqn, gqr = _split_head_gain(W['mla_q_head_g'])
    gkn, gkr = _split_head_gain(W['mla_k_head_g'])
    lb, lb_vjp = jax.vjp(_hgrn_lower_bound, W['hgrn_lb_logits'])

    def ffn(i, hh):
        return ffn_layer(i, hh, W['norm_ffn_g'][i:i + 1], _interleave_cols(ex.weight('ffn_w_up', i), 2),
                         _interleave_cols(W['ffn_conv_w'][i], 2), _interleave_cols(W['ffn_conv_b'][i:i + 1], 2),
                         ex.weight('ffn_w_down', i))

    bm, bf = [None] * 4, [None] * 4
    ex.gather(['mla'], name="gather_mla")
    w0 = dict(w_down=jnp.pad(ex.weight('mla_w_down'), ((0, 0), (0, 64))), gcq=W['mla_cq_norm_g'],
              gckv=W['mla_ckv_norm_g'], w_uq=_uq_to_heads(ex.weight('mla_w_uq')), w_ukv=ex.weight('mla_w_ukv'),
              gqn=gqn, gqr=gqr, gkn=gkn, gkr=gkr, w_o=ex.weight('mla_w_o'))
    h, bm[0] = mla_mixer(h, W['norm_mix_g'][0:1], w0, tabs, rider=ex.gather(['ffn0', 'hgrn', 'ffn1']))
    h, bf[0] = ffn(0, h)
    w1 = dict(w_in=_head_interleave(ex.weight('hgrn_w_in'), HG_WIDTHS, HG_H), lb=lb, go=W['hgrn_o_norm_g'],
              w_o=ex.weight('hgrn_w_o'))
    h, bm[1] = hgrn_mixer(h, W['norm_mix_g'][1:2], w1, rider=ex.gather(['s5', 'ffn2']))
    h, bf[1] = ffn(1, h)
    w2 = dict(lam_re=W['s5_lam_re'][0], lam_im=W['s5_lam_im'][0], log_dt=W['s5_log_dt'][0], b_re=W['s5_b_re'][0],
              b_im=W['s5_b_im'][0], c_re=W['s5_c_re'][0], c_im=W['s5_c_im'][0], dskip=W['s5_d'],
              w_glu=ex.weight('s5_w_glu'))
    h, bm[2] = s5_mixer(h, W['norm_mix_g'][2:3], w2, rider=ex.gather(['ret']))
    h, bf[2] = ffn(2, h)
    w3 = dict(w_in=_head_interleave(ex.weight('ret_w_in'), RET_WIDTHS, RET_H), gn=W['ret_gn_g'],
              w_o=ex.weight('ret_w_o'))
    h, bm[3] = ret_mixer(h, W['norm_mix_g'][3:4], w3, tabs, rider=ex.gather(['ffn3']))
    h, bf[3] = ffn(3, h)

    loss, dh = loss_head(h, tgt)

    def ffn_grads(i, g):
        return {('ffn_w_up', i): _deinterleave_cols(g['w_up'], 2), ('ffn_w_down', i): g['w_down']}

    gm, gf = [None] * 4, [None] * 4
    dh, gf[3] = bf[3](dh)
    dh, gm[3] = bm[3](dh, rider=ex.scatter(ffn_grads(3, gf[3])))
    dh, gf[2] = bf[2](dh)
    ret_grads = {('ret_w_in', 0): _head_deinterleave(gm[3]['w_in'], RET_WIDTHS, RET_H), ('ret_w_o', 0): gm[3]['w_o']}
    dh, gm[2] = bm[2](dh, rider=ex.scatter({**ret_grads, **ffn_grads(2, gf[2])}))
    dh, gf[1] = bf[1](dh)
    dh, gm[1] = bm[1](dh, rider=ex.scatter({('s5_w_glu', 0): gm[2]['w_glu'], **ffn_grads(1, gf[1])}))
    dh, gf[0] = bf[0](dh)
    hgrn_grads = {('hgrn_w_in', 0): _head_deinterleave(gm[1]['w_in'], HG_WIDTHS, HG_H), ('hgrn_w_o', 0): gm[1]['w_o']}
    dh, gm[0] = bm[0](dh, rider=ex.scatter({**hgrn_grads, **ffn_grads(0, gf[0])}))
    a = gm[0]
    ex.scatter({('mla_w_down', 0): a['w_down'][:, :MLA_QL + MLA_KVL + MLA_ROPE], ('mla_w_uq', 0): _uq_from_heads(a['w_uq']),
                ('mla_w_ukv', 0): a['w_ukv'], ('mla_w_o', 0): a['w_o']}, name="scatter_mla")

    G = {}
    G['meta_tokens'] = dh[PAD:LEAD]
    G['norm_mix_g'] = jnp.concatenate([gm[i]['g'] for i in range(4)], axis=0)
    G['norm_ffn_g'] = jnp.concatenate([gf[i]['g'] for i in range(4)], axis=0)
    G['mla_cq_norm_g'], G['mla_ckv_norm_g'] = a['gcq'], a['gckv']
    G['mla_q_head_g'] = _join_head_gain(a['gqn'], a['gqr'])
    G['mla_k_head_g'] = _join_head_gain(a['gkn'], a['gkr'])
    G['hgrn_lb_logits'] = lb_vjp(gm[1]['lb'])[0]
    G['hgrn_o_norm_g'] = gm[1]['go']
    for n in ('lam_re', 'lam_im', 'log_dt', 'b_re', 'b_im', 'c_re', 'c_im'):
        G['s5_' + n] = gm[2][n][None]
    G['s5_d'] = gm[2]['dskip']
    G['ret_gn_g'] = gm[3]['gn']
    G['ffn_conv_w'] = jnp.stack([_deinterleave_cols(gf[i]['cw'], 2) for i in range(4)])
    G['ffn_conv_b'] = jnp.concatenate([_deinterleave_cols(gf[i]['cb'], 2) for i in range(4)], axis=0)
    return loss, dh[LEAD:], G


PACK_W = 1024
ANY = pl.BlockSpec(memory_space=pl.ANY)


def _pack(arrs, dtype, row_mult):
    flat = jnp.concatenate([a.reshape(-1).astype(dtype) for a in arrs])
    n = flat.shape[0]
    rows = -(-n // (PACK_W * row_mult)) * row_mult
    return jnp.pad(flat, (0, rows * PACK_W - n)).reshape(rows, PACK_W)


def _unpack(buf, shapes):
    flat = buf.reshape(-1)
    out, off = [], 0
    for s in shapes:
        n = math.prod(s)
        out.append(flat[off:off + n].reshape(s))
        off += n
    return out


def _my_pos():
    return lax.axis_index("x"), lax.axis_index("y"), lax.axis_index("c")


def _other_chips(x, y):
    return [(1 - x, y), (x, 1 - y), (1 - x, 1 - y)]


def gather_chips(name, src):
    def body(src_ref, out_ref, send_sems, recv_sems, local_sem):
        x, y, c = _my_pos()
        q = 2 * x + y
        mine = pltpu.make_async_copy(src_ref, out_ref.at[q], local_sem)
        mine.start()
        peers = _other_chips(x, y)

        def copy(k, slot, peer):
            return pltpu.make_async_remote_copy(src_ref=src_ref, dst_ref=out_ref.at[slot], send_sem=send_sems.at[k],
                                                recv_sem=recv_sems.at[k], device_id=(peer[0], peer[1], c),
                                                device_id_type=MESH_ID)
        sends = [copy(k, q, p) for k, p in enumerate(peers)]
        for cp in sends:
            cp.start()
        for k, p in enumerate(peers):
            copy(k, 2 * p[0] + p[1], p).wait_recv()
        for cp in sends:
            cp.wait_send()
        mine.wait()

    return pl.pallas_call(body, out_shape=jax.ShapeDtypeStruct((4,) + src.shape, src.dtype), in_specs=[ANY],
                          out_specs=ANY, name=name,
                          scratch_shapes=[pltpu.SemaphoreType.DMA((3,)), pltpu.SemaphoreType.DMA((3,)),
                                          pltpu.SemaphoreType.DMA(())])(src)


def scatter_chips(name, src):
    def body(src_ref, out_ref, send_sems, recv_sems, local_sem):
        x, y, c = _my_pos()
        q = 2 * x + y
        mine = pltpu.make_async_copy(src_ref.at[q], out_ref.at[q], local_sem)
        mine.start()
        peers = _other_chips(x, y)

        def copy(k, peer):
            slot = 2 * peer[0] + peer[1]
            return pltpu.make_async_remote_copy(src_ref=src_ref.at[slot], dst_ref=out_ref.at[q], send_sem=send_sems.at[k],
                                                recv_sem=recv_sems.at[k], device_id=(peer[0], peer[1], c),
                                                device_id_type=MESH_ID)

        def landing(k, peer):
            slot = 2 * peer[0] + peer[1]
            return pltpu.make_async_remote_copy(src_ref=src_ref.at[slot], dst_ref=out_ref.at[slot],
                                                send_sem=send_sems.at[k], recv_sem=recv_sems.at[k],
                                                device_id=(peer[0], peer[1], c), device_id_type=MESH_ID)
        sends = [copy(k, p) for k, p in enumerate(peers)]
        for cp in sends:
            cp.start()
        for k, p in enumerate(peers):
            landing(k, p).wait_recv()
        for cp in sends:
            cp.wait_send()
        mine.wait()

    return pl.pallas_call(body, out_shape=jax.ShapeDtypeStruct(src.shape, src.dtype), in_specs=[ANY], out_specs=ANY,
                          name=name, scratch_shapes=[pltpu.SemaphoreType.DMA((3,)), pltpu.SemaphoreType.DMA((3,)),
                                                     pltpu.SemaphoreType.DMA(())])(src)


def swap_sibling(name, src):
    def body(src_ref, out_ref, send_sem, recv_sem):
        x, y, c = _my_pos()
        cp = pltpu.make_async_remote_copy(src_ref=src_ref, dst_ref=out_ref, send_sem=send_sem, recv_sem=recv_sem,
                                          device_id=(x, y, 1 - c), device_id_type=MESH_ID)
        cp.start()
        cp.wait()

    return pl.pallas_call(body, out_shape=jax.ShapeDtypeStruct(src.shape, src.dtype), in_specs=[ANY], out_specs=ANY,
                          name=name, scratch_shapes=[pltpu.SemaphoreType.DMA(()), pltpu.SemaphoreType.DMA(())])(src)


def gather_all(name, src):
    def body(src_ref, out_ref, send_sems, recv_sems, local_sem):
        x, y, c = _my_pos()
        me = 4 * x + 2 * y + c
        mine = pltpu.make_async_copy(src_ref, out_ref.at[me], local_sem)
        mine.start()
        peers = [((1 - x) if m & 4 else x, (1 - y) if m & 2 else y, (1 - c) if m & 1 else c) for m in range(1, 8)]

        def copy(k, slot, peer):
            return pltpu.make_async_remote_copy(src_ref=src_ref, dst_ref=out_ref.at[slot], send_sem=send_sems.at[k],
                                                recv_sem=recv_sems.at[k], device_id=peer, device_id_type=MESH_ID)
        sends = [copy(k, me, p) for k, p in enumerate(peers)]
        for cp in sends:
            cp.start()
        for k, p in enumerate(peers):
            copy(k, 4 * p[0] + 2 * p[1] + p[2], p).wait_recv()
        for cp in sends:
            cp.wait_send()
        mine.wait()

    return pl.pallas_call(body, out_shape=jax.ShapeDtypeStruct((8,) + src.shape, src.dtype), in_specs=[ANY],
                          out_specs=ANY, name=name,
                          scratch_shapes=[pltpu.SemaphoreType.DMA((7,)), pltpu.SemaphoreType.DMA((7,)),
                                          pltpu.SemaphoreType.DMA(())])(src)


def _pack_tile(rows):
    return _divisors(rows, (256, 128, 64, 32, 16, 8))[0] if rows > 512 else rows


def sum_slots(name, slots):
    n, rows, w = slots.shape
    tr = _pack_tile(rows)

    def body(s_ref, o_ref):
        acc = s_ref[0].astype(F32)
        for k in range(1, n):
            acc = acc + s_ref[k].astype(F32)
        o_ref[...] = acc

    return pl.pallas_call(body, grid=(rows // tr,), in_specs=[pl.BlockSpec((n, tr, w), lambda i: (0, i, 0))],
                          out_specs=pl.BlockSpec((tr, w), lambda i: (i, 0)),
                          out_shape=jax.ShapeDtypeStruct((rows, w), F32), name=name, compiler_params=_cparams())(slots)


def adamw(name, grads, w, m, v):
    rows, wd = w.shape
    tr = _pack_tile(rows)
    ng = len(grads)

    def body(*refs):
        g = refs[0][...]
        for r in refs[1:ng]:
            g = g + r[...]
        w_ref, m_ref, v_ref = refs[ng:ng + 3]
        g_out, d_out, m_out, v_out = refs[ng + 3:]
        m_new = ADAM_B1 * m_ref[...] + (1.0 - ADAM_B1) * g
        v_new = ADAM_B2 * v_ref[...] + (1.0 - ADAM_B2) * jnp.square(g)
        m_hat = m_new / (1.0 - ADAM_B1 ** ADAM_STEP)
        v_hat = v_new / (1.0 - ADAM_B2 ** ADAM_STEP)
        g_out[...] = g
        d_out[...] = -ADAM_LR * (m_hat / (jnp.sqrt(v_hat) + ADAM_EPS) + ADAM_WD * w_ref[...])
        m_out[...] = m_new
        v_out[...] = v_new

    spec = pl.BlockSpec((tr, wd), lambda i: (i, 0))
    shape = jax.ShapeDtypeStruct((rows, wd), F32)
    return pl.pallas_call(body, grid=(rows // tr,), in_specs=[spec] * (ng + 3), out_specs=[spec] * 4,
                          out_shape=[shape] * 4, name=name, compiler_params=_cparams())(*grads, w, m, v)


def _shard_of(ref, name, p):
    ax = SHARD_AXIS[name]
    n = ref.shape[ax] // 4
    idx = [slice(None)] * 3
    idx[ax] = pl.ds(pl.multiple_of(p * n, 128 if ax == 2 else 16), n)
    return ref.at[tuple(idx)]


def _sem_scratch(nw):
    return [pltpu.SemaphoreType.DMA((3 * nw,)), pltpu.SemaphoreType.DMA((3 * nw,)), pltpu.SemaphoreType.DMA((nw,))]


def gather_weights(name, names, shards):
    nw = len(shards)

    def full_shape(n, s):
        return tuple(d * 4 if ax == SHARD_AXIS[n] else d for ax, d in enumerate(s.shape))

    def body(*refs):
        src, dst = refs[:nw], refs[nw:2 * nw]
        send_sems, recv_sems, local_sems = refs[2 * nw:]
        x, y, c = _my_pos()
        q = 2 * x + y
        peers = _other_chips(x, y)
        local = [pltpu.make_async_copy(src[w], _shard_of(dst[w], names[w], q), local_sems.at[w]) for w in range(nw)]
        for cp in local:
            cp.start()

        def copy(w, k, slot):
            p = peers[k]
            return pltpu.make_async_remote_copy(src_ref=src[w], dst_ref=_shard_of(dst[w], names[w], slot),
                                                send_sem=send_sems.at[3 * w + k], recv_sem=recv_sems.at[3 * w + k],
                                                device_id=(p[0], p[1], c), device_id_type=MESH_ID)
        sends = [copy(w, k, q) for w in range(nw) for k in range(3)]
        for cp in sends:
            cp.start()
        for w in range(nw):
            for k in range(3):
                copy(w, k, 2 * peers[k][0] + peers[k][1]).wait_recv()
        for cp in sends:
            cp.wait_send()
        for cp in local:
            cp.wait()

    return pl.pallas_call(body, out_shape=[jax.ShapeDtypeStruct(full_shape(n, s), s.dtype) for n, s in zip(names, shards)],
                          in_specs=[ANY] * nw, out_specs=[ANY] * nw, name=name, scratch_shapes=_sem_scratch(nw))(*shards)


def scatter_grads(name, names, grads):
    nw = len(grads)

    def shard_shape(n, s):
        return tuple(d // 4 if ax == SHARD_AXIS[n] else d for ax, d in enumerate(s.shape))

    def body(*refs):
        src, dst = refs[:nw], refs[nw:2 * nw]
        send_sems, recv_sems, local_sems = refs[2 * nw:]
        x, y, c = _my_pos()
        q = 2 * x + y
        peers = _other_chips(x, y)
        local = [pltpu.make_async_copy(_shard_of(src[w], names[w], q), dst[w].at[q], local_sems.at[w])
                 for w in range(nw)]
        for cp in local:
            cp.start()

        def copy(w, k, slot):
            p = peers[k]
            return pltpu.make_async_remote_copy(src_ref=_shard_of(src[w], names[w], 2 * p[0] + p[1]),
                                                dst_ref=dst[w].at[slot], send_sem=send_sems.at[3 * w + k],
                                                recv_sem=recv_sems.at[3 * w + k], device_id=(p[0], p[1], c),
                                                device_id_type=MESH_ID)
        sends = [copy(w, k, q) for w in range(nw) for k in range(3)]
        for cp in sends:
            cp.start()
        for w in range(nw):
            for k in range(3):
                copy(w, k, 2 * peers[k][0] + peers[k][1]).wait_recv()
        for cp in sends:
            cp.wait_send()
        for cp in local:
            cp.wait()

    return pl.pallas_call(body, out_shape=[jax.ShapeDtypeStruct((4,) + shard_shape(n, g), g.dtype)
                                           for n, g in zip(names, grads)],
                          in_specs=[ANY] * nw, out_specs=[ANY] * nw, name=name, scratch_shapes=_sem_scratch(nw))(*grads)


def swap_siblings(name, arrs):
    nw = len(arrs)

    def body(*refs):
        src, dst = refs[:nw], refs[nw:2 * nw]
        send_sems, recv_sems = refs[2 * nw:]
        x, y, c = _my_pos()
        cps = [pltpu.make_async_remote_copy(src_ref=src[w], dst_ref=dst[w], send_sem=send_sems.at[w],
                                            recv_sem=recv_sems.at[w], device_id=(x, y, 1 - c), device_id_type=MESH_ID)
               for w in range(nw)]
        for cp in cps:
            cp.start()
        for cp in cps:
            cp.wait()

    return pl.pallas_call(body, out_shape=[jax.ShapeDtypeStruct(a.shape, a.dtype) for a in arrs], in_specs=[ANY] * nw,
                          out_specs=[ANY] * nw, name=name,
                          scratch_shapes=[pltpu.SemaphoreType.DMA((nw,)), pltpu.SemaphoreType.DMA((nw,))])(*arrs)


def _block2d(ref, axis, p, n):
    if axis == 0:
        return ref.at[pl.ds(pl.multiple_of(p * n, 16), n), :]
    return ref.at[:, pl.ds(pl.multiple_of(p * n, 128), n)]


class Rider:
    def __init__(self, kind, items):
        self.kind, self.items = kind, items
        self.operands = [it[0] for it in items]
        self.results = None
        self.out_shapes = []
        for it in items:
            if kind == 'gather':
                arr, _, axis = it
                r, c = arr.shape[1:]
                shape = (4 * r, c) if axis == 0 else (r, 4 * c)
            else:
                arr, axis = it
                r, c = arr.shape
                shape = (4, r // 4, c) if axis == 0 else (4, r, c // 4)
            self.out_shapes.append(jax.ShapeDtypeStruct(shape, arr.dtype))
        self.scratch = _sem_scratch(len(items))

    def _copies(self, ins, outs, sems):
        send_sems, recv_sems, local_sems = sems
        x, y, c = _my_pos()
        q = 2 * x + y
        peers = _other_chips(x, y)
        local, sends, lands = [], [], []
        for w, it in enumerate(self.items):
            axis = it[-1]
            if self.kind == 'gather':
                src_of = lambda p, w=w, it=it: ins[w].at[it[1]]
                n = it[0].shape[1 + axis]
                dst_of = lambda p, w=w, axis=axis, n=n: _block2d(outs[w], axis, p, n)
                mine, theirs = q, (lambda p: q)
                landed = lambda p: p
            else:
                n = it[0].shape[axis] // 4
                src_of = lambda p, w=w, axis=axis, n=n: _block2d(ins[w], axis, p, n)
                dst_of = lambda p, w=w: outs[w].at[p]
                mine, theirs = q, (lambda p: q)
                landed = lambda p: p
            local.append(pltpu.make_async_copy(src_of(q), dst_of(mine), local_sems.at[w]))
            for k, (px, py) in enumerate(peers):
                p = 2 * px + py
                sems_k = dict(send_sem=send_sems.at[3 * w + k], recv_sem=recv_sems.at[3 * w + k],
                              device_id=(px, py, c), device_id_type=MESH_ID)
                sends.append(pltpu.make_async_remote_copy(src_ref=src_of(p), dst_ref=dst_of(theirs(p)), **sems_k))
                lands.append(pltpu.make_async_remote_copy(src_ref=src_of(p), dst_ref=dst_of(landed(p)), **sems_k))
        return local, sends, lands

    def start(self, ins, outs, sems):
        local, sends, _ = self._copies(ins, outs, sems)
        for cp in local + sends:
            cp.start()

    def finish(self, ins, outs, sems):
        local, sends, lands = self._copies(ins, outs, sems)
        for cp in lands:
            cp.wait_recv()
        for cp in sends:
            cp.wait_send()
        for cp in local:
            cp.wait()


def run_rider(name, rider):
    n_in, n_out = len(rider.operands), len(rider.out_shapes)

    def body(*refs):
        ins, outs, sems = refs[:n_in], refs[n_in:n_in + n_out], refs[n_in + n_out:]
        rider.start(ins, outs, sems)
        rider.finish(ins, outs, sems)

    rider.results = list(pl.pallas_call(body, out_shape=rider.out_shapes, in_specs=[ANY] * n_in, out_specs=[ANY] * n_out,
                                        name=name, scratch_shapes=rider.scratch)(*rider.operands))


WEIGHT_GROUPS = {'mla': [('mla_w_down', 0), ('mla_w_uq', 0), ('mla_w_ukv', 0), ('mla_w_o', 0)],
                 'hgrn': [('hgrn_w_in', 0), ('hgrn_w_o', 0)], 's5': [('s5_w_glu', 0)],
                 'ret': [('ret_w_in', 0), ('ret_w_o', 0)]}
WEIGHT_GROUPS.update({f'ffn{i}': [('ffn_w_up', i), ('ffn_w_down', i)] for i in range(4)})


class Exchange:
    def __init__(self, shards=None, full=None):
        self.shards, self.full = shards, dict(full or {})
        self.got, self.recv, self.grads = {}, {}, {}

    def gather(self, groups, name=None):
        if self.shards is None:
            return None
        keys = [k for g in groups for k in WEIGHT_GROUPS[g]]
        rider = Rider('gather', [(self.shards[n], layer, SHARD_AXIS[n] - 1) for n, layer in keys])
        self.got.update({k: (rider, j) for j, k in enumerate(keys)})
        if name is not None:
            run_rider(name, rider)
        return rider

    def weight(self, n, layer=0):
        if self.shards is None:
            return self.full[n][layer]
        rider, j = self.got[(n, layer)]
        return rider.results[j]

    def scatter(self, grads, name=None):
        if self.shards is None:
            self.grads.update(grads)
            return None
        keys = list(grads)
        rider = Rider('scatter', [(grads[k], SHARD_AXIS[k[0]] - 1) for k in keys])
        self.recv.update({k: (rider, j) for j, k in enumerate(keys)})
        if name is not None:
            run_rider(name, rider)
        return rider

    def received(self, n, layer):
        rider, j = self.recv[(n, layer)]
        return rider.results[j]


ADAM_BLOCK_ELEMS = 256 * 1024


def adamw_shard(name, mine, sib, w, m, v):
    nl, rows, cols = w.shape
    tr = [t for t in (512, 384, 352, 256, 176, 128, 64, 32, 16) if rows % t == 0 and t * cols <= ADAM_BLOCK_ELEMS][0]

    def body(a_ref, b_ref, w_ref, m_ref, v_ref, g_out, d_out, m_out, v_out):
        def total(r):
            acc = r[0].astype(F32)
            for k in range(1, 4):
                acc = acc + r[k].astype(F32)
            return acc
        g = total(a_ref) + total(b_ref)
        m_new = ADAM_B1 * m_ref[...] + (1.0 - ADAM_B1) * g
        v_new = ADAM_B2 * v_ref[...] + (1.0 - ADAM_B2) * jnp.square(g)
        m_hat = m_new / (1.0 - ADAM_B1 ** ADAM_STEP)
        v_hat = v_new / (1.0 - ADAM_B2 ** ADAM_STEP)
        g_out[...] = g
        d_out[...] = -ADAM_LR * (m_hat / (jnp.sqrt(v_hat) + ADAM_EPS) + ADAM_WD * w_ref[...])
        m_out[...] = m_new
        v_out[...] = v_new

    slots = pl.BlockSpec((4, None, tr, cols), lambda l, i: (0, l, i, 0))
    spec = pl.BlockSpec((None, tr, cols), lambda l, i: (l, i, 0))
    shape = jax.ShapeDtypeStruct(w.shape, F32)
    return pl.pallas_call(body, grid=(nl, rows // tr), in_specs=[slots, slots, spec, spec, spec], out_specs=[spec] * 4,
                          out_shape=[shape] * 4, name=name, compiler_params=_cparams())(mine, sib, w, m, v)


def kernel(x, meta_tokens, norm_mix_g, norm_ffn_g, mla_w_down, mla_cq_norm_g, mla_ckv_norm_g, mla_w_uq, mla_w_ukv, mla_q_head_g, mla_k_head_g, mla_w_o, hgrn_w_in, hgrn_lb_logits, hgrn_o_norm_g, hgrn_w_o, s5_lam_re, s5_lam_im, s5_log_dt, s5_b_re, s5_b_im, s5_c_re, s5_c_im, s5_d, s5_w_glu, ret_w_in, ret_gn_g, ret_w_o, ffn_w_up, ffn_conv_w, ffn_conv_b, ffn_w_down, loss_target, m_meta_tokens, m_norm_mix_g, m_norm_ffn_g, m_mla_w_down, m_mla_cq_norm_g, m_mla_ckv_norm_g, m_mla_w_uq, m_mla_w_ukv, m_mla_q_head_g, m_mla_k_head_g, m_mla_w_o, m_hgrn_w_in, m_hgrn_lb_logits, m_hgrn_o_norm_g, m_hgrn_w_o, m_s5_lam_re, m_s5_lam_im, m_s5_log_dt, m_s5_b_re, m_s5_b_im, m_s5_c_re, m_s5_c_im, m_s5_d, m_s5_w_glu, m_ret_w_in, m_ret_gn_g, m_ret_w_o, m_ffn_w_up, m_ffn_conv_w, m_ffn_conv_b, m_ffn_w_down, v_meta_tokens, v_norm_mix_g, v_norm_ffn_g, v_mla_w_down, v_mla_cq_norm_g, v_mla_ckv_norm_g, v_mla_w_uq, v_mla_w_ukv, v_mla_q_head_g, v_mla_k_head_g, v_mla_w_o, v_hgrn_w_in, v_hgrn_lb_logits, v_hgrn_o_norm_g, v_hgrn_w_o, v_s5_lam_re, v_s5_lam_im, v_s5_log_dt, v_s5_b_re, v_s5_b_im, v_s5_c_re, v_s5_c_im, v_s5_d, v_s5_w_glu, v_ret_w_in, v_ret_gn_g, v_ret_w_o, v_ffn_w_up, v_ffn_conv_w, v_ffn_conv_b, v_ffn_w_down):
    vals = (x, meta_tokens, norm_mix_g, norm_ffn_g, mla_w_down, mla_cq_norm_g, mla_ckv_norm_g, mla_w_uq, mla_w_ukv, mla_q_head_g, mla_k_head_g, mla_w_o, hgrn_w_in, hgrn_lb_logits, hgrn_o_norm_g, hgrn_w_o, s5_lam_re, s5_lam_im, s5_log_dt, s5_b_re, s5_b_im, s5_c_re, s5_c_im, s5_d, s5_w_glu, ret_w_in, ret_gn_g, ret_w_o, ffn_w_up, ffn_conv_w, ffn_conv_b, ffn_w_down, loss_target, m_meta_tokens, m_norm_mix_g, m_norm_ffn_g, m_mla_w_down, m_mla_cq_norm_g, m_mla_ckv_norm_g, m_mla_w_uq, m_mla_w_ukv, m_mla_q_head_g, m_mla_k_head_g, m_mla_w_o, m_hgrn_w_in, m_hgrn_lb_logits, m_hgrn_o_norm_g, m_hgrn_w_o, m_s5_lam_re, m_s5_lam_im, m_s5_log_dt, m_s5_b_re, m_s5_b_im, m_s5_c_re, m_s5_c_im, m_s5_d, m_s5_w_glu, m_ret_w_in, m_ret_gn_g, m_ret_w_o, m_ffn_w_up, m_ffn_conv_w, m_ffn_conv_b, m_ffn_w_down, v_meta_tokens, v_norm_mix_g, v_norm_ffn_g, v_mla_w_down, v_mla_cq_norm_g, v_mla_ckv_norm_g, v_mla_w_uq, v_mla_w_ukv, v_mla_q_head_g, v_mla_k_head_g, v_mla_w_o, v_hgrn_w_in, v_hgrn_lb_logits, v_hgrn_o_norm_g, v_hgrn_w_o, v_s5_lam_re, v_s5_lam_im, v_s5_log_dt, v_s5_b_re, v_s5_b_im, v_s5_c_re, v_s5_c_im, v_s5_d, v_s5_w_glu, v_ret_w_in, v_ret_gn_g, v_ret_w_o, v_ffn_w_up, v_ffn_conv_w, v_ffn_conv_b, v_ffn_w_down)
    names = ['x'] + WEIGHTS + ['loss_target'] + ['m_' + n for n in WEIGHTS] + ['v_' + n for n in WEIGHTS]
    A = dict(zip(names, vals))
    q = 2 * lax.axis_index("x") + lax.axis_index("y")

    small_shapes = [A[n].shape for n in SMALL_SHARDED]
    got_small = gather_chips("gather_small", _pack([A[n] for n in SMALL_SHARDED], F32, 8))
    W = {n: A[n] for n in REPLICATED}
    parts_small = [_unpack(got_small[p], small_shapes) for p in range(4)]
    for k, n in enumerate(SMALL_SHARDED):
        W[n] = jnp.concatenate([parts_small[p][k] for p in range(4)], axis=SHARD_AXIS[n])

    ex = Exchange(shards={n: A[n].astype(BF16) for n in BIG})
    loss, grad_x, G = local_step(A['x'][0], A['loss_target'][0], W, ex)
    loss = lax.psum(loss, ("x", "y", "c"))

    keys = [(n, layer) for n in BIG for layer in range(A[n].shape[0])]
    mine = [ex.received(n, layer) for n, layer in keys]
    sib = dict(zip(keys, swap_siblings("grad_big_sibling", mine)))
    res_big = []
    for n in BIG:
        layers = range(A[n].shape[0])
        res_big.append(adamw_shard("adam_" + n, jnp.stack([ex.received(n, layer) for layer in layers], axis=1),
                                   jnp.stack([sib[(n, layer)] for layer in layers], axis=1),
                                   A[n], A['m_' + n], A['v_' + n]))

    small_names = REPLICATED + SMALL_SHARDED
    full_shapes = [G[n].shape for n in small_names]
    total = sum_slots("grad_small_sum", gather_all("grad_small_gather", _pack([G[n] for n in small_names], F32, 8)))
    gs = dict(zip(small_names, _unpack(total, full_shapes)))
    for n in SMALL_SHARDED:
        ax = SHARD_AXIS[n]
        size = gs[n].shape[ax] // 4
        gs[n] = lax.dynamic_slice_in_dim(gs[n], q * size, size, axis=ax)
    pk = lambda pre: _pack([A[pre + n] for n in small_names], F32, 8)
    own_shapes = [A[n].shape for n in small_names]
    res_small = [_unpack(r, own_shapes) for r in
                 adamw("adam_small", [_pack([gs[n] for n in small_names], F32, 8)], pk(''), pk('m_'), pk('v_'))]

    out = {}
    for j, kind in enumerate(('grad_', 'delta_', 'new_m_', 'new_v_')):
        for k, n in enumerate(BIG):
            out[kind + n] = res_big[k][j]
        for k, n in enumerate(small_names):
            out[kind + n] = res_small[j][k]
    return (loss, grad_x[None]) + tuple(out[kind + n] for kind in ('grad_', 'delta_', 'new_m_', 'new_v_')
                                        for n in WEIGHTS)
```

```python
import functools
import math

import jax
import jax.numpy as jnp
from jax import lax
from jax.experimental import pallas as pl
from jax.experimental.pallas import tpu as pltpu

F32, BF16 = jnp.float32, jnp.bfloat16
HIGHEST = lax.Precision.HIGHEST
MESH_ID = pl.DeviceIdType.MESH

D = 1024
N_META = 16
PAD = 112
LEAD = PAD + N_META
EPS = 1e-6
NEG_INF = -1e30
CHUNK = 64
VMEM_LIMIT_V7X = 56 * 1024 * 1024
MM_VMEM_BUDGET = 36 * 1024 * 1024

MLA_H, MLA_NOPE, MLA_ROPE, MLA_V = 8, 128, 64, 128
MLA_QK = MLA_NOPE + MLA_ROPE
MLA_QL, MLA_KVL = 384, 256
HG_H, HG_D, HG_C = 8, 128, 16
S5_G, S5_P, S5_K = 64, 64, 16
RET_H, RET_DK, RET_DV = 4, 256, 512
FFN_F = 2816

ADAM_LR, ADAM_B1, ADAM_B2, ADAM_EPS, ADAM_WD, ADAM_STEP = 0.001, 0.9, 0.999, 1e-08, 0.01, 10

WEIGHTS = ['meta_tokens', 'norm_mix_g', 'norm_ffn_g', 'mla_w_down', 'mla_cq_norm_g', 'mla_ckv_norm_g', 'mla_w_uq',
           'mla_w_ukv', 'mla_q_head_g', 'mla_k_head_g', 'mla_w_o', 'hgrn_w_in', 'hgrn_lb_logits', 'hgrn_o_norm_g',
           'hgrn_w_o', 's5_lam_re', 's5_lam_im', 's5_log_dt', 's5_b_re', 's5_b_im', 's5_c_re', 's5_c_im', 's5_d',
           's5_w_glu', 'ret_w_in', 'ret_gn_g', 'ret_w_o', 'ffn_w_up', 'ffn_conv_w', 'ffn_conv_b', 'ffn_w_down']
SHARD_AXIS = {'meta_tokens': 1, 'mla_w_down': 1, 'mla_w_uq': 2, 'mla_w_ukv': 2, 'mla_w_o': 1, 'hgrn_w_in': 2,
              'hgrn_w_o': 1, 's5_d': 1, 's5_w_glu': 2, 'ret_w_in': 2, 'ret_gn_g': 1, 'ret_w_o': 1, 'ffn_w_up': 2,
              'ffn_conv_w': 2, 'ffn_w_down': 1}
BIG = ['mla_w_down', 'mla_w_uq', 'mla_w_ukv', 'mla_w_o', 'hgrn_w_in', 'hgrn_w_o', 's5_w_glu', 'ret_w_in', 'ret_w_o',
       'ffn_w_up', 'ffn_w_down']
SMALL_SHARDED = ['meta_tokens', 's5_d', 'ret_gn_g', 'ffn_conv_w']
REPLICATED = [n for n in WEIGHTS if n not in SHARD_AXIS]


def _cparams():
    return pltpu.CompilerParams(vmem_limit_bytes=VMEM_LIMIT_V7X)


def _dg(a, b, ca, cb):
    return lax.dot_general(a.astype(BF16), b.astype(BF16), (((ca,), (cb,)), ((), ())),
                           preferred_element_type=F32)


@jax.custom_vjp
def mm_nn(a, b):
    return _dg(a, b, 1, 0)


@jax.custom_vjp
def mm_nt(a, b):
    return _dg(a, b, 1, 1)


@jax.custom_vjp
def mm_tn(a, b):
    return _dg(a, b, 0, 0)


mm_nn.defvjp(lambda a, b: (mm_nn(a, b), (a, b)),
             lambda r, g: (mm_nt(g, r[1]).astype(r[0].dtype), mm_tn(r[0], g).astype(r[1].dtype)))
mm_nt.defvjp(lambda a, b: (mm_nt(a, b), (a, b)),
             lambda r, g: (mm_nn(g, r[1]).astype(r[0].dtype), mm_tn(g, r[0]).astype(r[1].dtype)))
mm_tn.defvjp(lambda a, b: (mm_tn(a, b), (a, b)),
             lambda r, g: (mm_nt(r[1], g).astype(r[0].dtype), mm_nn(r[0], g).astype(r[1].dtype)))


def _dot_f32(a, b):
    return jnp.dot(a, b, precision=HIGHEST, preferred_element_type=F32)


def _shift_rows(x, s, up):
    n = x.shape[0]
    r = lax.broadcasted_iota(jnp.int32, x.shape, 0)
    if up:
        return jnp.where(r < n - s, pltpu.roll(x, n - s, 0), 0.0)
    return jnp.where(r >= s, pltpu.roll(x, s, 0), 0.0)


@functools.partial(jax.custom_vjp, nondiff_argnums=(1,))
def shift_down(x, s):
    return _shift_rows(x, s, False)


shift_down.defvjp(lambda x, s: (_shift_rows(x, s, False), None), lambda s, _, g: (_shift_rows(g, s, True),))


@functools.partial(jax.custom_vjp, nondiff_argnums=(1,))
def shift_up(x, s):
    return _shift_rows(x, s, True)


shift_up.defvjp(lambda x, s: (_shift_rows(x, s, True), None), lambda s, _, g: (_shift_rows(g, s, False),))


def _swap32_impl(x):
    ax = x.ndim - 1
    lane = lax.broadcasted_iota(jnp.int32, x.shape, ax)
    return jnp.where(lane < 32, pltpu.roll(x, 96, ax), jnp.where(lane < 64, pltpu.roll(x, 32, ax), 0.0))


@jax.custom_vjp
def swap32(x):
    return _swap32_impl(x)


swap32.defvjp(lambda x: (_swap32_impl(x), None), lambda _, g: (_swap32_impl(g),))


def _rms(x, g):
    return x * lax.rsqrt(jnp.mean(x * x, axis=-1, keepdims=True) + EPS) * g


def _silu(x):
    return x * jax.nn.sigmoid(x)


def _row_ids(pid, n, shape, axis=0):
    return pid * n + lax.broadcasted_iota(jnp.int32, shape, axis)


class Arg:
    def __init__(self, arr, block, imap, diff=True, gdtype=F32):
        self.arr, self.block, self.imap, self.diff, self.gdtype = arr, block, imap, diff, gdtype


class Out:
    def __init__(self, shape, dtype, block, imap):
        self.shape, self.dtype, self.block, self.imap = shape, dtype, block, imap


def _free_axes(imap, grid):
    ng = len(grid)
    base = tuple(imap(*([0] * ng)))
    free = []
    for ax in range(ng):
        p = [0] * ng
        p[ax] = 1
        if grid[ax] > 1 and tuple(imap(*p)) == base:
            free.append(ax)
    assert free == list(range(ng - len(free), ng)), "revisited blocks must be revisited on the innermost axes"
    return free


def _pallas(name, body, grid, in_specs, out_specs, out_shape, scratch, operands, rider=None):
    if isinstance(rider, (list, tuple)):
        riders = [r for r in rider if r is not None]
        rider = RiderGroup(riders) if riders else None
    if rider is None:
        return pl.pallas_call(body, grid=grid, in_specs=in_specs, out_specs=out_specs, out_shape=out_shape,
                              scratch_shapes=scratch, name=name, compiler_params=_cparams())(*operands)
    n_in, n_out, n_sc = len(in_specs), len(out_specs), len(scratch)
    r_in, r_out = len(rider.operands), len(rider.out_shapes)

    def body_with_rider(*refs):
        ins, refs = refs[:n_in], refs[n_in:]
        r_ins, refs = refs[:r_in], refs[r_in:]
        outs, refs = refs[:n_out], refs[n_out:]
        r_outs, refs = refs[:r_out], refs[r_out:]
        sc, r_sc = refs[:n_sc], refs[n_sc:]
        pids = [pl.program_id(a) for a in range(len(grid))]
        first = functools.reduce(jnp.logical_and, [p == 0 for p in pids])
        last = functools.reduce(jnp.logical_and, [p == g - 1 for p, g in zip(pids, grid)])

        @pl.when(first)
        def _():
            rider.start(r_ins, r_outs, r_sc)

        body(*ins, *outs, *sc)

        @pl.when(last)
        def _():
            rider.finish(r_ins, r_outs, r_sc)

    res = pl.pallas_call(body_with_rider, grid=grid, in_specs=list(in_specs) + [ANY] * r_in,
                         out_specs=list(out_specs) + [ANY] * r_out, out_shape=list(out_shape) + rider.out_shapes,
                         scratch_shapes=list(scratch) + rider.scratch, name=name,
                         compiler_params=_cparams())(*operands, *rider.operands)
    rider.results = list(res[n_out:])
    return res[:n_out]


def stage_fwd(name, fn, grid, args, outs, state_shape=None, rider=None):
    n_in, n_out, ng = len(args), len(outs), len(grid)

    def body(*refs):
        pids = tuple(pl.program_id(a) for a in range(ng))
        vals = [r[...] for r in refs[:n_in]]
        o_refs = refs[n_in:n_in + n_out]
        if state_shape is None:
            res = fn(pids, *vals)
        else:
            sv_ref, st_ref = refs[n_in + n_out], refs[n_in + n_out + 1]

            @pl.when(pids[-1] == 0)
            def _():
                st_ref[...] = jnp.zeros(state_shape, F32)

            s = st_ref[...]
            sv_ref[...] = s
            res = fn(pids, *vals, s)
            st_ref[...] = res[-1]
            res = res[:-1]
        for r, v in zip(o_refs, res):
            r[...] = v.astype(r.dtype)

    in_specs = [pl.BlockSpec(a.block, a.imap) for a in args]
    out_specs = [pl.BlockSpec(o.block, o.imap) for o in outs]
    out_shape = [jax.ShapeDtypeStruct(o.shape, o.dtype) for o in outs]
    scratch = []
    if state_shape is not None:
        nz = len(state_shape)
        out_specs.append(pl.BlockSpec((None, None) + tuple(state_shape), lambda i, j: (i, j) + (0,) * nz))
        out_shape.append(jax.ShapeDtypeStruct(tuple(grid) + tuple(state_shape), F32))
        scratch = [pltpu.VMEM(state_shape, F32)]
    return _pallas(name, body, grid, in_specs, out_specs, out_shape, scratch, [a.arr for a in args], rider)


def stage_bwd(name, fn, grid, args, outs, cots, state_shape=None, states=None, rider=None):
    n_in, n_out, ng = len(args), len(outs), len(grid)
    nb = grid[-1]
    rev = state_shape is not None
    didx = [k for k, a in enumerate(args) if a.diff]
    frees = [_free_axes(args[k].imap, grid) for k in didx]

    def eff(p):
        return tuple(p[:-1]) + (nb - 1 - p[-1],) if rev else tuple(p)

    def wrap(imap):
        return lambda *p: imap(*eff(p))

    def body(*refs):
        pids = tuple(pl.program_id(a) for a in range(ng))
        e = eff(pids)
        vals = [r[...] for r in refs[:n_in]]
        cts = tuple(r[...].astype(F32) for r in refs[n_in:n_in + n_out])
        pos = n_in + n_out
        if rev:
            st_in_ref = refs[pos]
            pos += 1
        g_refs = refs[pos:pos + len(didx)]
        pos += len(didx)
        dvals = [vals[k].astype(F32) for k in didx]

        def f(*dv):
            full = list(vals)
            for k, v in zip(didx, dv[:len(didx)]):
                full[k] = v
            return tuple(fn(e, *full, *dv[len(didx):]))

        if rev:
            ds_ref = refs[pos]

            @pl.when(pids[-1] == 0)
            def _():
                ds_ref[...] = jnp.zeros(state_shape, F32)

            _, vjp = jax.vjp(f, *dvals, st_in_ref[...])
            grads = vjp(cts + (ds_ref[...],))
            ds_ref[...] = grads[-1]
            grads = grads[:-1]
        else:
            _, vjp = jax.vjp(f, *dvals)
            grads = vjp(cts)
        for gref, g, free in zip(g_refs, grads, frees):
            g = g.astype(F32)
            if not free:
                gref[...] = g.astype(gref.dtype)
            else:
                first = functools.reduce(jnp.logical_and, [pids[ax] == 0 for ax in free])

                @pl.when(first)
                def _():
                    gref[...] = g

                @pl.when(jnp.logical_not(first))
                def _():
                    gref[...] += g

    in_specs = [pl.BlockSpec(a.block, wrap(a.imap)) for a in args]
    in_specs += [pl.BlockSpec(o.block, wrap(o.imap)) for o in outs]
    operands = [a.arr for a in args] + list(cots)
    scratch = []
    if rev:
        nz = len(state_shape)
        in_specs.append(pl.BlockSpec((None, None) + tuple(state_shape), lambda i, j: (i, nb - 1 - j) + (0,) * nz))
        operands.append(states)
        scratch = [pltpu.VMEM(state_shape, F32)]
    out_specs = [pl.BlockSpec(args[k].block, wrap(args[k].imap)) for k in didx]
    assert all(args[k].gdtype == F32 or not free for k, free in zip(didx, frees))
    out_shape = [jax.ShapeDtypeStruct(args[k].arr.shape, args[k].gdtype) for k in didx]
    return _pallas(name, body, grid, in_specs, out_specs, out_shape, scratch, operands, rider)


def _divisors(n, cands):
    return [c for c in cands if n % c == 0] or [n]


def _nbytes(dt):
    return jnp.dtype(dt).itemsize


def matmul(name, a, b, mode, out_dtype=F32, res=None, mask=False, res_mask=True, window=None, into=None):
    sa, sb, so = _nbytes(a.dtype), _nbytes(b.dtype), _nbytes(out_dtype)
    off, width = (window[0], window[1]) if window is not None else (0, None)
    if mode in ('nn', 'nt'):
        M, K = a.shape
        N = (width or b.shape[1]) if mode == 'nn' else b.shape[0]
        assert mode == 'nn' or width is None or width == K
        best = None
        for tm in _divisors(M, (1408, 1056, 768, 384, 128)):
            for tn in _divisors(N, (1408, 1024, 768, 512, 384, 256, 128)):
                est = 2 * (tm * K * sa + tn * K * sb + tm * tn * (so + (4 if res is not None else 0)))
                if est <= MM_VMEM_BUDGET and (best is None or tm * tn > best[0] * best[1]):
                    best = (tm, tn)
        tm, tn = best
        grid = (M // tm, N // tn)

        def body(*refs):
            a_ref, b_ref = refs[0], refs[1]
            o_ref = refs[-1]
            x = a_ref[...]
            rows = _row_ids(pl.program_id(0), tm, (tm, 1))
            if mask:
                x = jnp.where(rows >= PAD, x, jnp.zeros_like(x))
            acc = _dg(x, b_ref[...], 1, 0 if mode == 'nn' else 1)
            if res is not None:
                acc = refs[2][...] + (jnp.where(rows >= PAD, acc, 0.0) if res_mask else acc)
            o_ref[...] = acc.astype(o_ref.dtype)

        assert off % (tn if mode == 'nn' else K) == 0
        cb, kb = off // tn, off // K
        in_specs = [pl.BlockSpec((tm, K), lambda i, j: (i, 0)),
                    pl.BlockSpec((K, tn), lambda i, j: (0, j + cb)) if mode == 'nn' else
                    pl.BlockSpec((tn, K), lambda i, j: (j, kb))]
        ops = [a, b]
        if res is not None:
            in_specs.append(pl.BlockSpec((tm, tn), lambda i, j: (i, j)))
            ops.append(res)
        return pl.pallas_call(body, grid=grid, in_specs=in_specs,
                              out_specs=pl.BlockSpec((tm, tn), lambda i, j: (i, j)),
                              out_shape=jax.ShapeDtypeStruct((M, N), out_dtype), name=name,
                              compiler_params=_cparams())(*ops)
    assert mode == 'tn' and res is None
    M, K = a.shape
    N = b.shape[1]
    best = None
    for tk in _divisors(K, (1408, 1024, 768, 512, 384, 256, 128)):
        for tn in _divisors(N, (1408, 1024, 768, 512, 384, 256, 128)):
            est = 2 * (M * tk * sa + M * tn * sb + tk * tn * so)
            if est <= MM_VMEM_BUDGET and (best is None or tk * tn > best[0] * best[1]):
                best = (tk, tn)
    tk, tn = best

    def body_t(*refs):
        a_ref, b_ref, o_ref = refs[0], refs[1], refs[-1]
        y = b_ref[...]
        if mask:
            rows = lax.broadcasted_iota(jnp.int32, (M, 1), 0)
            y = jnp.where(rows >= PAD, y, jnp.zeros_like(y))
        o_ref[...] = _dg(a_ref[...], y, 0, 0).astype(o_ref.dtype)

    assert off % tn == 0
    cb = off // tn
    total = window[2] if window is not None else N
    in_specs = [pl.BlockSpec((M, tk), lambda i, j: (0, i)), pl.BlockSpec((M, tn), lambda i, j: (0, j))]
    ops, alias = [a, b], {}
    if into is not None:
        in_specs.append(ANY)
        ops.append(into)
        alias = {2: 0}
    return pl.pallas_call(body_t, grid=(K // tk, N // tn), in_specs=in_specs,
                          out_specs=pl.BlockSpec((tk, tn), lambda i, j: (i, j + cb)),
                          out_shape=jax.ShapeDtypeStruct((K, total), out_dtype), name=name,
                          input_output_aliases=alias, compiler_params=_cparams())(*ops)


def linear_bwd(name, act, w, dy, mask=False):
    return (matmul(name + "_da", dy, w, 'nt', mask=mask),
            matmul(name + "_dw", act, dy, 'tn', out_dtype=BF16, mask=mask))


def _row_tile(T):
    return _divisors(T, (384, 128))[0]


def _rows(arr, tm, gdtype=F32):
    return Arg(arr, (tm, arr.shape[1]), lambda i: (i, 0), gdtype=gdtype)


def _const(arr, diff=True):
    return Arg(arr, arr.shape, lambda *p: (0,) * arr.ndim, diff)


def _norm_fn(pids, h, g):
    return (_rms(h, g),)


def _norm_bwd_fn(pids, h, g):
    return (_rms(h, g), h)


def norm_fwd(name, h, g, dtype):
    T = h.shape[0]
    tm = _row_tile(T)
    return stage_fwd(name, _norm_fn, (T // tm,), [_rows(h, tm), _const(g)],
                     [Out((T, D), dtype, (tm, D), lambda i: (i, 0))])[0]


def norm_bwd(name, h, g, da, dh):
    T = h.shape[0]
    tm = _row_tile(T)
    o = Out((T, D), F32, (tm, D), lambda i: (i, 0))
    return stage_bwd(name, _norm_bwd_fn, (T // tm,), [_rows(h, tm), _const(g)], [o, o], [da, dh])


def _causal_conv3(u, cw, cb):
    return cw[2:3] * u + cw[1:2] * shift_down(u, 1) + cw[0:1] * shift_down(u, 2) + cb


def _ffn_act_fn(pids, ug, uv, cwg, cwv, cbg, cbv):
    return (_silu(_causal_conv3(ug, cwg, cbg)) * _causal_conv3(uv, cwv, cbv),)


def _ffn_act_args(ug, uv, cw, cb):
    T = ug.shape[0]
    col = lambda j: (0, j)
    args = [Arg(ug, (T, 128), col, gdtype=BF16), Arg(uv, (T, 128), col, gdtype=BF16),
            Arg(cw[:, :FFN_F], (3, 128), col), Arg(cw[:, FFN_F:], (3, 128), col),
            Arg(cb[:, :FFN_F], (1, 128), col), Arg(cb[:, FFN_F:], (1, 128), col)]
    outs = [Out((T, FFN_F), BF16, (T, 128), col)]
    return (FFN_F // 128,), args, outs


def _interleave_cols(w, n_parts, tile=128):
    lead = w.shape[:-1]
    n = w.shape[-1] // (n_parts * tile)
    k = len(lead)
    return w.reshape(lead + (n_parts, n, tile)).transpose(tuple(range(k)) + (k + 1, k, k + 2)).reshape(w.shape)


def _deinterleave_cols(w, n_parts, tile=128):
    lead = w.shape[:-1]
    n = w.shape[-1] // (n_parts * tile)
    k = len(lead)
    return w.reshape(lead + (n, n_parts, tile)).transpose(tuple(range(k)) + (k + 1, k, k + 2)).reshape(w.shape)


def ffn_layer(i, h, g, w_up, cw, cb, w_down):
    gate_w, val_w = (0, FFN_F, 2 * FFN_F), (FFN_F, FFN_F, 2 * FFN_F)
    b = norm_fwd(f"ffn{i}_norm", h, g, BF16)
    ug = matmul(f"ffn{i}_up_g", b, w_up, 'nn', window=gate_w)
    uv = matmul(f"ffn{i}_up_v", b, w_up, 'nn', window=val_w)
    grid, args, outs = _ffn_act_args(ug, uv, cw, cb)
    p = stage_fwd(f"ffn{i}_act", _ffn_act_fn, grid, args, outs)[0]
    h_new = matmul(f"ffn{i}_down", p, w_down, 'nn', res=h)

    def bwd(dh):
        dp, dwd = linear_bwd(f"ffn{i}_down_b", p, w_down, dh, mask=True)
        dug, duv, dcwg, dcwv, dcbg, dcbv = stage_bwd(f"ffn{i}_act_b", _ffn_act_fn, grid, args, outs, [dp])
        db = matmul(f"ffn{i}_up_b_da_g", dug, w_up, 'nt', window=gate_w)
        db = matmul(f"ffn{i}_up_b_da_v", duv, w_up, 'nt', window=val_w, res=db, res_mask=False)
        dwu = matmul(f"ffn{i}_up_b_dw_g", b, dug, 'tn', out_dtype=BF16, window=gate_w)
        dwu = matmul(f"ffn{i}_up_b_dw_v", b, duv, 'tn', out_dtype=BF16, window=val_w, into=dwu)
        dh2, dg = norm_bwd(f"ffn{i}_norm_b", h, g, db, dh)
        return dh2, dict(g=dg, w_up=dwu, cw=jnp.concatenate([dcwg, dcwv], axis=1),
                         cb=jnp.concatenate([dcbg, dcbv], axis=1), w_down=dwd)

    return h_new, bwd


def _mla_latent_fn(pids, down, gcq, gckv):
    cq = _rms(down[:, :MLA_QL], gcq)
    ckv = _rms(down[:, MLA_QL:MLA_QL + MLA_KVL], gckv)
    return cq, ckv, down[:, MLA_QL + MLA_KVL:]


def _rope64(x, cos, sin_signed):
    return x * cos + swap32(x) * sin_signed


def _mla_heads_fn(pids, qraw, kv, kpe, gqn, gqr, gkn, gkr, cos, sin_signed):
    qn, qr = qraw[:, :128], qraw[:, 128:]
    rq = lax.rsqrt((jnp.sum(qn * qn, -1, keepdims=True) + jnp.sum(qr * qr, -1, keepdims=True)) / MLA_QK + EPS)
    q = jnp.concatenate([qn * rq * gqn, _rope64(qr * rq * gqr, cos, sin_signed)], axis=1)
    kn, v = kv[:, :128], kv[:, 128:]
    rk = lax.rsqrt((jnp.sum(kn * kn, -1, keepdims=True) + jnp.sum(kpe * kpe, -1, keepdims=True)) / MLA_QK + EPS)
    k = jnp.concatenate([kn * rk * gkn, _rope64(kpe * rk * gkr, cos, sin_signed)], axis=1)
    return q, k, v


def _chunk_id(r):
    return jnp.where(r < LEAD, 0, 1 + lax.shift_right_arithmetic(r - LEAD, 6))


ATT_T = 384
ATT_SCALE = MLA_QK ** -0.5


def _att_scores(q, k_ref, qb, kb):
    ks = k_ref[pl.ds(pl.multiple_of(kb * ATT_T, ATT_T), ATT_T), :]
    s = _dg(q, ks, 1, 1) * ATT_SCALE
    qrow = _row_ids(qb, ATT_T, (ATT_T, 1))
    krow = _row_ids(kb, ATT_T, (1, ATT_T), axis=1)
    ok = jnp.logical_and(_chunk_id(krow) <= _chunk_id(qrow), krow >= PAD)
    return jnp.where(ok, s, NEG_INF), ks


def _att_rows(ref, b):
    return ref[pl.ds(pl.multiple_of(b * ATT_T, ATT_T), ATT_T), :]


def attention_fwd(q, k, v, rider=None):
    H, T, _ = q.shape
    nq = T // ATT_T

    def body(q_ref, k_ref, v_ref, o_ref, lse_ref):
        j = pl.program_id(1)
        qv = q_ref[...]

        def step(kb, carry):
            m, l, acc = carry
            s, _ = _att_scores(qv, k_ref, j, kb)
            m_new = jnp.maximum(m, jnp.max(s, axis=-1, keepdims=True))
            p = jnp.exp(s - m_new)
            alpha = jnp.exp(m - m_new)
            return (m_new, alpha * l + jnp.sum(p, axis=-1, keepdims=True),
                    alpha * acc + _dg(p, _att_rows(v_ref, kb), 1, 0))

        init = (jnp.full((ATT_T, 1), NEG_INF, F32), jnp.zeros((ATT_T, 1), F32), jnp.zeros((ATT_T, MLA_V), F32))
        m, l, acc = lax.fori_loop(0, j + 1, step, init)
        o_ref[...] = acc / l
        lse_ref[...] = m + jnp.log(l)

    return _pallas("mla_attn", body, (H, nq),
                   [pl.BlockSpec((None, ATT_T, 256), lambda hh, j: (hh, j, 0)),
                    pl.BlockSpec((None, T, 256), lambda hh, j: (hh, 0, 0)),
                    pl.BlockSpec((None, T, MLA_V), lambda hh, j: (hh, 0, 0))],
                   [pl.BlockSpec((ATT_T, MLA_V), lambda hh, j: (j, hh)),
                    pl.BlockSpec((None, ATT_T, 1), lambda hh, j: (hh, j, 0))],
                   [jax.ShapeDtypeStruct((T, H * MLA_V), F32), jax.ShapeDtypeStruct((H, T, 1), F32)], [],
                   [q, k, v], rider)


def attention_bwd(q, k, v, o, lse, do, rider=None):
    H, T, _ = q.shape
    nq = T // ATT_T

    def dq_body(q_ref, k_ref, v_ref, o_ref, do_ref, lse_ref, dq_ref, delta_ref):
        j = pl.program_id(1)
        qv, dov, lsev = q_ref[...], do_ref[...], lse_ref[...]
        delta = jnp.sum(dov * o_ref[...], axis=-1, keepdims=True)
        delta_ref[...] = delta

        def step(kb, acc):
            s, ks = _att_scores(qv, k_ref, j, kb)
            p = jnp.exp(s - lsev)
            ds = p * (_dg(dov, _att_rows(v_ref, kb), 1, 1) - delta) * ATT_SCALE
            return acc + _dg(ds, ks, 1, 0)

        dq_ref[...] = lax.fori_loop(0, j + 1, step, jnp.zeros((ATT_T, 256), F32))

    q_blk = pl.BlockSpec((None, ATT_T, 256), lambda hh, j: (hh, j, 0))
    k_all = pl.BlockSpec((None, T, 256), lambda hh, j: (hh, 0, 0))
    v_all = pl.BlockSpec((None, T, MLA_V), lambda hh, j: (hh, 0, 0))
    o_blk = pl.BlockSpec((ATT_T, MLA_V), lambda hh, j: (j, hh))
    col_blk = pl.BlockSpec((None, ATT_T, 1), lambda hh, j: (hh, j, 0))
    dq, delta = pl.pallas_call(dq_body, grid=(H, nq), in_specs=[q_blk, k_all, v_all, o_blk, o_blk, col_blk],
                               out_specs=[q_blk, col_blk],
                               out_shape=[jax.ShapeDtypeStruct((H, T, 256), F32), jax.ShapeDtypeStruct((H, T, 1), F32)],
                               name="mla_attn_dq", compiler_params=_cparams())(q, k, v, o, do, lse)

    def dkv_body(q_ref, k_ref, v_ref, do_ref, lse_ref, delta_ref, dk_ref, dv_ref):
        kb = pl.program_id(1)
        kv = k_ref[...]
        vv = v_ref[...]
        qrow0 = lax.broadcasted_iota(jnp.int32, (ATT_T, 1), 0)
        krow = _row_ids(kb, ATT_T, (1, ATT_T), axis=1)

        def step(qb, carry):
            dk, dv = carry
            qv, dov = _att_rows(q_ref, qb), _att_rows(do_ref, qb)
            s = _dg(qv, kv, 1, 1) * ATT_SCALE
            qrow = qb * ATT_T + qrow0
            ok = jnp.logical_and(_chunk_id(krow) <= _chunk_id(qrow), krow >= PAD)
            p = jnp.exp(jnp.where(ok, s, NEG_INF) - _att_rows(lse_ref, qb))
            ds = p * (_dg(dov, vv, 1, 1) - _att_rows(delta_ref, qb)) * ATT_SCALE
            return dk + _dg(ds, qv, 0, 0), dv + _dg(p, dov, 0, 0)

        dk, dv = lax.fori_loop(kb, nq, step, (jnp.zeros((ATT_T, 256), F32), jnp.zeros((ATT_T, MLA_V), F32)))
        dk_ref[...] = dk
        dv_ref[...] = dv

    q_all = pl.BlockSpec((None, T, 256), lambda hh, j: (hh, 0, 0))
    v_blk = pl.BlockSpec((None, ATT_T, MLA_V), lambda hh, j: (hh, j, 0))
    do_all = pl.BlockSpec((T, MLA_V), lambda hh, j: (0, hh))
    col_all = pl.BlockSpec((None, T, 1), lambda hh, j: (hh, 0, 0))
    dk, dv = _pallas("mla_attn_dkv", dkv_body, (H, nq), [q_all, q_blk, v_blk, do_all, col_all, col_all],
                     [q_blk, v_blk],
                     [jax.ShapeDtypeStruct((H, T, 256), F32), jax.ShapeDtypeStruct((H, T, MLA_V), F32)], [],
                     [q, k, v, do, lse, delta], rider)
    return dq, dk, dv


def mla_mixer(h, g, w, tabs, rider=None):
    T = h.shape[0]
    tm = _row_tile(T)
    nt = T // tm
    a = norm_fwd("mla_norm", h, g, BF16)
    down = matmul("mla_down", a, w['w_down'], 'nn')
    lat_args = [_rows(down, tm, BF16), _const(w['gcq']), _const(w['gckv'])]
    lat_outs = [Out((T, MLA_QL), BF16, (tm, MLA_QL), lambda i: (i, 0)),
                Out((T, MLA_KVL), BF16, (tm, MLA_KVL), lambda i: (i, 0)),
                Out((T, 128), F32, (tm, 128), lambda i: (i, 0))]
    cq, ckv, kpe = stage_fwd("mla_latent", _mla_latent_fn, (nt,), lat_args, lat_outs)
    qraw = matmul("mla_uq", cq, w['w_uq'], 'nn')
    kv = matmul("mla_ukv", ckv, w['w_ukv'], 'nn')
    hd_args = [Arg(qraw, (tm, 256), lambda i, hh: (i, hh), gdtype=BF16),
               Arg(kv, (tm, 256), lambda i, hh: (i, hh), gdtype=BF16),
               Arg(kpe, (tm, 128), lambda i, hh: (i, 0)),
               _const(w['gqn']), _const(w['gqr']), _const(w['gkn']), _const(w['gkr']),
               Arg(tabs['cos_a'], (tm, 128), lambda i, hh: (i, 0), False),
               Arg(tabs['sin_a'], (tm, 128), lambda i, hh: (i, 0), False)]
    hd_outs = [Out((MLA_H, T, 256), BF16, (None, tm, 256), lambda i, hh: (hh, i, 0)),
               Out((MLA_H, T, 256), BF16, (None, tm, 256), lambda i, hh: (hh, i, 0)),
               Out((MLA_H, T, 128), BF16, (None, tm, 128), lambda i, hh: (hh, i, 0))]
    q, k, v = stage_fwd("mla_heads", _mla_heads_fn, (nt, MLA_H), hd_args, hd_outs)
    o, lse = attention_fwd(q, k, v, rider=rider)
    h_new = matmul("mla_o", o, w['w_o'], 'nn', res=h)

    def bwd(dh, rider=None):
        do, dwo = linear_bwd("mla_o_b", o, w['w_o'], dh, mask=True)
        dq, dk, dv = attention_bwd(q, k, v, o, lse, do, rider=rider)
        dqraw, dkv, dkpe, dgqn, dgqr, dgkn, dgkr = stage_bwd("mla_heads_b", _mla_heads_fn, (nt, MLA_H), hd_args,
                                                             hd_outs, [dq, dk, dv])
        dcq, dwuq = linear_bwd("mla_uq_b", cq, w['w_uq'], dqraw)
        dckv, dwukv = linear_bwd("mla_ukv_b", ckv, w['w_ukv'], dkv)
        ddown, dgcq, dgckv = stage_bwd("mla_latent_b", _mla_latent_fn, (nt,), lat_args, lat_outs, [dcq, dckv, dkpe])
        da, dwdown = linear_bwd("mla_down_b", a, w['w_down'], ddown)
        dh2, dg = norm_bwd("mla_norm_b", h, g, da, dh)
        return dh2, dict(g=dg, w_down=dwdown, gcq=dgcq, gckv=dgckv, w_uq=dwuq, w_ukv=dwukv, gqn=dgqn, gqr=dgqr,
                         gkn=dgkn, gkr=dgkr, w_o=dwo)

    return h_new, bwd


HG_R = 128


def _hgrn_fn(pids, z, lb, go, st):
    R = z.shape[0]
    zq, zf, zi, zg = z[:, :128], z[:, 128:256], z[:, 256:384], z[:, 384:]
    assert R == 128
    q = _silu(zq)
    fg = lb + (1.0 - lb) * jax.nn.sigmoid(zf)
    logf = jnp.log(fg)
    k = 1.0 - fg
    row = lax.broadcasted_iota(jnp.int32, logf.shape, 0)
    pos = row & (HG_C - 1)
    cum, rev = logf, logf
    for d in (1, 2, 4, 8):
        cum = cum + jnp.where(pos >= d, shift_down(cum, d), 0.0)
        rev = rev + jnp.where(pos < HG_C - d, shift_up(rev, d), 0.0)
    cums, tots = [cum], [cum + rev - logf]
    for s in (16, 32, 64):
        odd = (row & s) != 0
        before = shift_down(tots[-1], s)
        cums.append(cums[-1] + jnp.where(odd, before, 0.0))
        tots.append(tots[-1] + jnp.where(odd, before, shift_up(tots[-1], s)))
    t = lax.broadcasted_iota(jnp.int32, (R, R), 0)
    j = lax.broadcasted_iota(jnp.int32, (R, R), 1)
    sh = lax.shift_right_arithmetic
    a = jnp.where(jnp.logical_and(sh(t, 4) == sh(j, 4), j <= t), mm_nt(q * jnp.exp(cum), k * jnp.exp(-cum)), 0.0)
    for n, s in enumerate((16, 32, 64)):
        m = jnp.logical_and(sh(t, 5 + n) == sh(j, 5 + n), jnp.logical_and((t & s) != 0, (j & s) == 0))
        a = a + jnp.where(m, mm_nt(q * jnp.exp(cums[n]), k * jnp.exp(tots[n] - cums[n])), 0.0)
    o = mm_nn(a, zi) + mm_nt(q * jnp.exp(cums[3]), st)
    st = st * jnp.exp(tots[3][0:1, :]) + mm_tn(zi, k * jnp.exp(tots[3] - cums[3]))
    return _rms(o, go) * _silu(zg), st


def hgrn_mixer(h, g, w, rider=None):
    T = h.shape[0]
    a = norm_fwd("hgrn_norm", h, g, BF16)
    z = matmul("hgrn_in", a, w['w_in'], 'nn')
    grid = (HG_H, T // HG_R)
    args = [Arg(z, (HG_R, 512), lambda hh, j: (j, hh), gdtype=BF16), Arg(w['lb'], (1, 128), lambda hh, j: (0, hh)),
            _const(w['go'])]
    outs = [Out((T, D), BF16, (HG_R, 128), lambda hh, j: (j, hh))]
    o, states = stage_fwd("hgrn_gla", _hgrn_fn, grid, args, outs, state_shape=(HG_D, HG_D), rider=rider)
    h_new = matmul("hgrn_o", o, w['w_o'], 'nn', res=h)

    def bwd(dh, rider=None):
        do, dwo = linear_bwd("hgrn_o_b", o, w['w_o'], dh, mask=True)
        dz, dlb, dgo = stage_bwd("hgrn_gla_b", _hgrn_fn, grid, args, outs, [do], state_shape=(HG_D, HG_D),
                                 states=states, rider=rider)
        da, dwin = linear_bwd("hgrn_in_b", a, w['w_in'], dz)
        dh2, dg = norm_bwd("hgrn_norm_b", h, g, da, dh)
        return dh2, dict(g=dg, w_in=dwin, lb=dlb, go=dgo, w_o=dwo)

    return h_new, bwd


S5_R = 128
S5_W = 512
S5_SLABS = D // 128


def _cmul(ar, ai, br, bi):
    return ar * br - ai * bi, ar * bi + ai * br


def _s5_scan(br, bi, tab, cr, ci, reverse):
    R, W = br.shape
    G = R // 8
    xr, xi = br.reshape(G, 8, W), bi.reshape(G, 8, W)
    for n, d in enumerate((1, 2, 4)):
        sh = (8 - d) if reverse else d
        mr, mi = _cmul(tab[2 * n][None], tab[2 * n + 1][None], pltpu.roll(xr, sh, 1), pltpu.roll(xi, sh, 1))
        xr, xi = xr + mr, xi + mi
    pr, pi = tab[6], tab[7]
    edge = 0 if reverse else 7
    out_r, out_i = [None] * G, [None] * G
    for g in (range(G - 1, -1, -1) if reverse else range(G)):
        ar, ai = _cmul(pr, pi, cr, ci)
        gr, gi = xr[g] + ar, xi[g] + ai
        cr, ci = gr[edge:edge + 1], gi[edge:edge + 1]
        out_r[g], out_i[g] = gr, gi
    return jnp.concatenate(out_r, axis=0), jnp.concatenate(out_i, axis=0), cr, ci


def s5_scan_fwd(a, bb, cb, tab, rider=None):
    T = a.shape[0]
    nb = T // S5_R

    def body(a_ref, bb_ref, cb_ref, tab_ref, y_ref, xs_ref, c_ref):
        @pl.when(pl.program_id(1) == 0)
        def _():
            c_ref[...] = jnp.zeros(c_ref.shape, F32)

        bu = _dg(a_ref[...], bb_ref[...], 1, 0)
        t = tab_ref[...]
        xr, xi, cr, ci = _s5_scan(bu[:, :S5_W], bu[:, S5_W:], t, c_ref[0:1, :S5_W], c_ref[0:1, S5_W:], False)
        x = jnp.concatenate([xr, xi], axis=1)
        xs_ref[...] = x
        y_ref[...] = _dg(x, cb_ref[...], 1, 0)
        c_ref[0:1, :] = jnp.concatenate([cr, ci], axis=1)

    return _pallas(
        "s5_scan", body, (S5_SLABS, nb),
        [pl.BlockSpec((S5_R, 128), lambda j, i: (i, j)),
         pl.BlockSpec((None, 128, 2 * S5_W), lambda j, i: (j, 0, 0)),
         pl.BlockSpec((None, 2 * S5_W, 128), lambda j, i: (j, 0, 0)),
         pl.BlockSpec((None, 10, 8, S5_W), lambda j, i: (j, 0, 0, 0))],
        [pl.BlockSpec((S5_R, 128), lambda j, i: (i, j)),
         pl.BlockSpec((None, S5_R, 2 * S5_W), lambda j, i: (j, i, 0))],
        [jax.ShapeDtypeStruct((T, D), F32), jax.ShapeDtypeStruct((S5_SLABS, T, 2 * S5_W), F32)],
        [pltpu.VMEM((8, 2 * S5_W), F32)], [a, bb, cb, tab], rider)


def s5_scan_bwd(a, bb, cb, tab_rev, xs, dy, rider=None):
    T = a.shape[0]
    nb = T // S5_R
    rg = S5_R // 8

    def body(a_ref, dy_ref, xs_ref, xp_ref, bb_ref, cb_ref, tab_ref, da_ref, dbb_ref, dcb_ref, dab_ref, c_ref):
        i = pl.program_id(1)

        @pl.when(i == 0)
        def _():
            c_ref[...] = jnp.zeros(c_ref.shape, F32)

        dy_v = dy_ref[...]
        x = xs_ref[...]
        dxo = _dg(dy_v, cb_ref[...], 1, 1)
        gr, gi, cr, ci = _s5_scan(dxo[:, :S5_W], dxo[:, S5_W:], tab_ref[...], c_ref[0:1, :S5_W], c_ref[0:1, S5_W:], True)
        c_ref[0:1, :] = jnp.concatenate([cr, ci], axis=1)
        g = jnp.concatenate([gr, gi], axis=1)
        da_ref[...] = _dg(g, bb_ref[...], 1, 1)
        dbb = _dg(a_ref[...], g, 0, 0)
        dcb = _dg(x, dy_v, 0, 0)
        first_tile = i == nb - 1
        prev_last = jnp.where(first_tile, 0.0, xp_ref[7:8, :])
        rows = lax.broadcasted_iota(jnp.int32, x.shape, 0)
        xp = jnp.where(rows == 0, prev_last, pltpu.roll(x, 1, 0))
        xpr, xpi = xp[:, :S5_W], xp[:, S5_W:]
        dar = (gr * xpr + gi * xpi).reshape(rg, 8, S5_W).sum(axis=0)
        dai = (gi * xpr - gr * xpi).reshape(rg, 8, S5_W).sum(axis=0)
        dab = jnp.concatenate([dar, dai], axis=1)

        @pl.when(i == 0)
        def _():
            dbb_ref[...] = dbb
            dcb_ref[...] = dcb
            dab_ref[...] = dab

        @pl.when(i != 0)
        def _():
            dbb_ref[...] += dbb
            dcb_ref[...] += dcb
            dab_ref[...] += dab

    def prev_rows(j, i):
        return (j, jnp.maximum((nb - 1 - i) * rg - 1, 0), 0)

    return _pallas(
        "s5_scan_b", body, (S5_SLABS, nb),
        [pl.BlockSpec((S5_R, 128), lambda j, i: (nb - 1 - i, j)),
         pl.BlockSpec((S5_R, 128), lambda j, i: (nb - 1 - i, j)),
         pl.BlockSpec((None, S5_R, 2 * S5_W), lambda j, i: (j, nb - 1 - i, 0)),
         pl.BlockSpec((None, 8, 2 * S5_W), prev_rows),
         pl.BlockSpec((None, 128, 2 * S5_W), lambda j, i: (j, 0, 0)),
         pl.BlockSpec((None, 2 * S5_W, 128), lambda j, i: (j, 0, 0)),
         pl.BlockSpec((None, 10, 8, S5_W), lambda j, i: (j, 0, 0, 0))],
        [pl.BlockSpec((S5_R, 128), lambda j, i: (nb - 1 - i, j)),
         pl.BlockSpec((None, 128, 2 * S5_W), lambda j, i: (j, 0, 0)),
         pl.BlockSpec((None, 2 * S5_W, 128), lambda j, i: (j, 0, 0)),
         pl.BlockSpec((None, 8, 2 * S5_W), lambda j, i: (j, 0, 0))],
        [jax.ShapeDtypeStruct((T, D), F32), jax.ShapeDtypeStruct((S5_SLABS, 128, 2 * S5_W), F32),
         jax.ShapeDtypeStruct((S5_SLABS, 2 * S5_W, 128), F32), jax.ShapeDtypeStruct((S5_SLABS, 8, 2 * S5_W), F32)],
        [pltpu.VMEM((8, 2 * S5_W), F32)], [a, dy, xs, xs, bb, cb, tab_rev], rider)


def _s5_discretise(lam_re, lam_im, log_dt, b_re, b_im, c_re, c_im):
    dt = jnp.exp(log_dt)[:, None]
    mag = jnp.exp(lam_re * dt)
    abar_re = mag * jnp.cos(lam_im * dt)
    abar_im = mag * jnp.sin(lam_im * dt)
    den = lam_re * lam_re + lam_im * lam_im
    zoh_re = ((abar_re - 1.0) * lam_re + abar_im * lam_im) / den
    zoh_im = (abar_im * lam_re - (abar_re - 1.0) * lam_im) / den
    bbar_re = zoh_re[..., None] * b_re - zoh_im[..., None] * b_im
    bbar_im = zoh_re[..., None] * b_im + zoh_im[..., None] * b_re
    eye = jnp.eye(8, dtype=F32)

    def in_map(bbar):
        t = bbar.reshape(8, 8, S5_P, S5_K).transpose(0, 1, 3, 2)
        return (t[:, :, :, None, :] * eye[None, :, None, :, None]).reshape(8, 8 * S5_K, 8 * S5_P)

    def out_map(c):
        t = c.reshape(8, 8, S5_K, S5_P).transpose(0, 1, 3, 2)
        return (t[:, :, :, None, :] * eye[None, :, None, :, None]).reshape(8, 8 * S5_P, 8 * S5_K)

    bb = jnp.concatenate([in_map(bbar_re), in_map(bbar_im)], axis=2)
    cb = jnp.concatenate([out_map(c_re), -out_map(c_im)], axis=1)
    return bb, cb, abar_re.reshape(8, S5_W), abar_im.reshape(8, S5_W)


def _s5_tables(ar, ai, reverse):
    if reverse:
        ai = -ai
    pw = [(jnp.ones_like(ar), jnp.zeros_like(ar))]
    for _ in range(8):
        pw.append(_cmul(pw[-1][0], pw[-1][1], ar, ai))
    r = jnp.arange(8)[None, :, None]
    rows = []
    for d in (1, 2, 4):
        keep = (r <= 7 - d) if reverse else (r >= d)
        rows += [jnp.where(keep, pw[d][0][:, None, :], 0.0), jnp.where(keep, pw[d][1][:, None, :], 0.0)]
    order = [8 - k for k in range(8)] if reverse else [k + 1 for k in range(8)]
    rows += [jnp.stack([pw[n][0] for n in order], axis=1), jnp.stack([pw[n][1] for n in order], axis=1)]
    rows += [jnp.broadcast_to(pw[8][0][:, None, :], (8, 8, S5_W)), jnp.broadcast_to(pw[8][1][:, None, :], (8, 8, S5_W))]
    return jnp.stack(rows, axis=1)


def _s5_act_fn(pids, yc, a, dskip):
    return (jax.nn.gelu(yc + dskip * a),)


def _make_glu_res_fn(tm):
    def glu_res_fn(pids, zz, h):
        rows = _row_ids(pids[0], tm, (tm, 1))
        return (h + jnp.where(rows >= PAD, zz[:, :D] * jax.nn.sigmoid(zz[:, D:]), 0.0),)
    return glu_res_fn


def s5_mixer(h, g, w, rider=None):
    T = h.shape[0]
    tm = _row_tile(T)
    nt = T // tm
    a = norm_fwd("s5_norm", h, g, F32)
    ssm = [w[n] for n in ('lam_re', 'lam_im', 'log_dt', 'b_re', 'b_im', 'c_re', 'c_im')]
    (bb, cb, ar, ai), disc_vjp = jax.vjp(_s5_discretise, *ssm)
    yc, xs = s5_scan_fwd(a, bb, cb, _s5_tables(ar, ai, False), rider=rider)
    row = lambda arr: _rows(arr, tm)
    act_args = [row(yc), row(a), _const(w['dskip'])]
    act_outs = [Out((T, D), BF16, (tm, D), lambda i: (i, 0))]
    y = stage_fwd("s5_act", _s5_act_fn, (nt,), act_args, act_outs)[0]
    zz = matmul("s5_glu", y, w['w_glu'], 'nn')
    glu_fn = _make_glu_res_fn(tm)
    glu_args = [_rows(zz, tm, BF16), row(h)]
    glu_outs = [Out((T, D), F32, (tm, D), lambda i: (i, 0))]
    h_new = stage_fwd("s5_gate", glu_fn, (nt,), glu_args, glu_outs)[0]

    def bwd(dh, rider=None):
        dzz, dh_res = stage_bwd("s5_gate_b", glu_fn, (nt,), glu_args, glu_outs, [dh])
        dy, dwglu = linear_bwd("s5_glu_b", y, w['w_glu'], dzz)
        dyc, da1, ddskip = stage_bwd("s5_act_b", _s5_act_fn, (nt,), act_args, act_outs, [dy])
        da2, dbb, dcb, dab = s5_scan_bwd(a, bb, cb, _s5_tables(ar, ai, True), xs, dyc, rider=rider)
        dab = dab.sum(axis=1)
        dssm = disc_vjp((dbb, dcb, dab[:, :S5_W], dab[:, S5_W:]))
        dh2, dg = _s5_norm_bwd(h, g, da1, da2, dh_res, tm)
        grads = dict(zip(('lam_re', 'lam_im', 'log_dt', 'b_re', 'b_im', 'c_re', 'c_im'), dssm))
        grads.update(g=dg, dskip=ddskip, w_glu=dwglu)
        return dh2, grads

    return h_new, bwd


def _norm3_bwd_fn(pids, h, g):
    a = _rms(h, g)
    return a, a, h


def _s5_norm_bwd(h, g, da1, da2, dh, tm):
    T = h.shape[0]
    o = Out((T, D), F32, (tm, D), lambda i: (i, 0))
    return stage_bwd("s5_norm_b", _norm3_bwd_fn, (T // tm,), [_rows(h, tm), _const(g)], [o, o, o], [da1, da2, dh])


RET_R = 128


def _rope256(x, cos, sin):
    x1, x2 = x[:, :128], x[:, 128:]
    return jnp.concatenate([x1 * cos - x2 * sin, x1 * sin + x2 * cos], axis=1)


def _ret_fn(pids, z, gn, cos, sin, dmat, qdec, kdec, cdec, st):
    R = z.shape[0]
    q = _rope256(z[:, :256], cos, sin)
    k = _rope256(z[:, 256:512], cos, sin) * (RET_DK ** -0.5)
    v, gate = z[:, 512:1024], z[:, 1024:]
    outs = []
    for cc in range(R // CHUNK):
        lo = cc * CHUNK
        qc, kc, vc = q[lo:lo + CHUNK], k[lo:lo + CHUNK], v[lo:lo + CHUNK]
        outs.append(mm_nn(mm_nt(qc, kc) * dmat, vc) + mm_nn(qc * qdec, st))
        st = st * cdec + mm_tn(kc * kdec, vc)
    o = jnp.concatenate(outs, axis=0)
    mu = jnp.mean(o, axis=-1, keepdims=True)
    var = jnp.mean(jnp.square(o - mu), axis=-1, keepdims=True)
    o = (o - mu) * lax.rsqrt(var + EPS)
    return o * gn * _silu(gate), st


def ret_mixer(h, g, w, tabs, rider=None):
    T = h.shape[0]
    a = norm_fwd("ret_norm", h, g, BF16)
    z = matmul("ret_in", a, w['w_in'], 'nn')
    grid = (RET_H, T // RET_R)
    hw = RET_DK * 2 + RET_DV * 2
    args = [Arg(z, (RET_R, hw), lambda hh, j: (j, hh), gdtype=BF16), Arg(w['gn'], (1, RET_DV), lambda hh, j: (0, hh)),
            Arg(tabs['cos_d'], (RET_R, 128), lambda hh, j: (j, 0), False),
            Arg(tabs['sin_d'], (RET_R, 128), lambda hh, j: (j, 0), False),
            Arg(tabs['ret_dmat'], (None, CHUNK, CHUNK), lambda hh, j: (hh, 0, 0), False),
            Arg(tabs['ret_qdec'], (None, CHUNK, 1), lambda hh, j: (hh, 0, 0), False),
            Arg(tabs['ret_kdec'], (None, CHUNK, 1), lambda hh, j: (hh, 0, 0), False),
            Arg(tabs['ret_cdec'], (None, 1, 1), lambda hh, j: (hh, 0, 0), False)]
    outs = [Out((T, RET_H * RET_DV), BF16, (RET_R, RET_DV), lambda hh, j: (j, hh))]
    o, states = stage_fwd("ret_chunks", _ret_fn, grid, args, outs, state_shape=(RET_DK, RET_DV), rider=rider)
    h_new = matmul("ret_o", o, w['w_o'], 'nn', res=h)

    def bwd(dh, rider=None):
        do, dwo = linear_bwd("ret_o_b", o, w['w_o'], dh, mask=True)
        dz, dgn = stage_bwd("ret_chunks_b", _ret_fn, grid, args, outs, [do], state_shape=(RET_DK, RET_DV),
                            states=states, rider=rider)
        da, dwin = linear_bwd("ret_in_b", a, w['w_in'], dz)
        dh2, dg = norm_bwd("ret_norm_b", h, g, da, dh)
        return dh2, dict(g=dg, w_in=dwin, gn=dgn, w_o=dwo)

    return h_new, bwd


def loss_head(h, tgt):
    T = h.shape[0]
    tm = _row_tile(T)

    def body(h_ref, t_ref, loss_ref, dh_ref):
        i = pl.program_id(0)
        rows = _row_ids(i, tm, (tm, 1))
        err = jnp.where(rows >= LEAD, h_ref[...] - t_ref[...], 0.0)
        dh_ref[...] = err * (1.0 / D)
        part = jnp.full((8, 128), 0.5 * jnp.sum(jnp.sum(err * err, axis=1, keepdims=True) * (1.0 / D)), F32)

        @pl.when(i == 0)
        def _():
            loss_ref[...] = part

        @pl.when(i != 0)
        def _():
            loss_ref[...] += part

    loss, dh = pl.pallas_call(
        body, grid=(T // tm,),
        in_specs=[pl.BlockSpec((tm, D), lambda i: (i, 0)), pl.BlockSpec((tm, D), lambda i: (i, 0))],
        out_specs=[pl.BlockSpec((8, 128), lambda i: (0, 0)), pl.BlockSpec((tm, D), lambda i: (i, 0))],
        out_shape=[jax.ShapeDtypeStruct((8, 128), F32), jax.ShapeDtypeStruct((T, D), F32)], name="loss_head",
        compiler_params=_cparams())(h, tgt)
    return loss[0, 0], dh


def _tables(T):
    pos = jnp.maximum(jnp.arange(T, dtype=jnp.int32) - PAD, 0).astype(F32)

    def cs(dim):
        inv_freq = 1.0 / (10000.0 ** (jnp.arange(0, dim, 2, dtype=F32) / dim))
        ang = pos[:, None] * inv_freq[None, :]
        return jnp.cos(ang), jnp.sin(ang)

    ca, sa = cs(MLA_ROPE)
    zeros = jnp.zeros((T, 64), F32)
    cd, sd = cs(RET_DK)
    log_gamma = jnp.log(1.0 - jnp.exp2(-5.0 - jnp.arange(RET_H, dtype=F32)))
    p = jnp.arange(CHUNK, dtype=F32)
    diff = p[:, None] - p[None, :]
    dmat = jnp.where(diff >= 0, jnp.exp(diff[None] * log_gamma[:, None, None]), 0.0)
    return dict(cos_a=jnp.concatenate([ca, ca, zeros], axis=1), sin_a=jnp.concatenate([-sa, sa, zeros], axis=1),
                cos_d=cd, sin_d=sd, ret_dmat=dmat,
                ret_qdec=jnp.exp((p[None, :] + 1.0) * log_gamma[:, None])[..., None],
                ret_kdec=jnp.exp((CHUNK - 1.0 - p[None, :]) * log_gamma[:, None])[..., None],
                ret_cdec=jnp.exp(CHUNK * log_gamma)[:, None, None])


def _hgrn_lower_bound(logits):
    lb_cum = jnp.cumsum(jax.nn.softmax(logits, axis=0), axis=0)
    return (lb_cum - lb_cum[0:1])[1:2]


def _uq_to_heads(w):
    t = w.reshape(w.shape[0], MLA_H, MLA_QK)
    return jnp.pad(t, ((0, 0), (0, 0), (0, 256 - MLA_QK))).reshape(w.shape[0], MLA_H * 256)


def _uq_from_heads(g):
    return g.reshape(g.shape[0], MLA_H, 256)[:, :, :MLA_QK].reshape(g.shape[0], MLA_H * MLA_QK)


def _head_interleave(w, widths, heads):
    parts, lo = [], 0
    for wd in widths:
        parts.append(w[:, lo:lo + heads * wd].reshape(w.shape[0], heads, wd))
        lo += heads * wd
    return jnp.concatenate(parts, axis=2).reshape(w.shape[0], -1)


def _head_deinterleave(g, widths, heads):
    t = g.reshape(g.shape[0], heads, sum(widths))
    parts, lo = [], 0
    for wd in widths:
        parts.append(t[:, :, lo:lo + wd].reshape(g.shape[0], heads * wd))
        lo += wd
    return jnp.concatenate(parts, axis=1)


HG_WIDTHS = (128, 128, 128, 128)
RET_WIDTHS = (RET_DK, RET_DK, RET_DV, RET_DV)


def _split_head_gain(g):
    return g[:, :128], jnp.pad(g[:, 128:], ((0, 0), (0, 64)))


def _join_head_gain(dn, dr):
    return jnp.concatenate([dn, dr[:, :64]], axis=1)


def local_step(x, target, W, ex):
    S = x.shape[0]
    T = S + LEAD
    tabs = _tables(T)
    h = jnp.concatenate([jnp.zeros((PAD, D), F32), W['meta_tokens'], x], axis=0)
    tgt = jnp.concatenate([jnp.zeros((LEAD, D), F32), target], axis=0)

    gqn, gqr = _split_head_gain(W['mla_q_head_g'])
    gkn, gkr = _split_head_gain(W['mla_k_head_g'])
    lb, lb_vjp = jax.vjp(_hgrn_lower_bound, W['hgrn_lb_logits'])

    def ffn(i, hh):
        return ffn_layer(i, hh, W['norm_ffn_g'][i:i + 1], ex.weight('ffn_w_up', i), W['ffn_conv_w'][i],
                         W['ffn_conv_b'][i:i + 1], ex.weight('ffn_w_down', i))

    bm, bf = [None] * 4, [None] * 4
    ex.gather(['mla'], name="gather_mla")
    w0 = dict(w_down=jnp.pad(ex.weight('mla_w_down'), ((0, 0), (0, 64))), gcq=W['mla_cq_norm_g'],
              gckv=W['mla_ckv_norm_g'], w_uq=_uq_to_heads(ex.weight('mla_w_uq')), w_ukv=ex.weight('mla_w_ukv'),
              gqn=gqn, gqr=gqr, gkn=gkn, gkr=gkr, w_o=ex.weight('mla_w_o'))
    h, bm[0] = mla_mixer(h, W['norm_mix_g'][0:1], w0, tabs, rider=ex.gather(['ffn0', 'hgrn', 'ffn1']))
    h, bf[0] = ffn(0, h)
    w1 = dict(w_in=_head_interleave(ex.weight('hgrn_w_in'), HG_WIDTHS, HG_H), lb=lb, go=W['hgrn_o_norm_g'],
              w_o=ex.weight('hgrn_w_o'))
    h, bm[1] = hgrn_mixer(h, W['norm_mix_g'][1:2], w1, rider=ex.gather(['s5', 'ffn2']))
    h, bf[1] = ffn(1, h)
    w2 = dict(lam_re=W['s5_lam_re'][0], lam_im=W['s5_lam_im'][0], log_dt=W['s5_log_dt'][0], b_re=W['s5_b_re'][0],
              b_im=W['s5_b_im'][0], c_re=W['s5_c_re'][0], c_im=W['s5_c_im'][0], dskip=W['s5_d'],
              w_glu=ex.weight('s5_w_glu'))
    h, bm[2] = s5_mixer(h, W['norm_mix_g'][2:3], w2, rider=ex.gather(['ret']))
    h, bf[2] = ffn(2, h)
    w3 = dict(w_in=_head_interleave(ex.weight('ret_w_in'), RET_WIDTHS, RET_H), gn=W['ret_gn_g'],
              w_o=ex.weight('ret_w_o'))
    h, bm[3] = ret_mixer(h, W['norm_mix_g'][3:4], w3, tabs, rider=ex.gather(['ffn3']))
    h, bf[3] = ffn(3, h)

    loss, dh = loss_head(h, tgt)

    def ffn_grads(i, g):
        return {('ffn_w_up', i): g['w_up'], ('ffn_w_down', i): g['w_down']}

    gm, gf = [None] * 4, [None] * 4
    dh, gf[3] = bf[3](dh)
    dh, gm[3] = bm[3](dh, rider=ex.scatter(ffn_grads(3, gf[3])))
    dh, gf[2] = bf[2](dh)
    ret_grads = {('ret_w_in', 0): _head_deinterleave(gm[3]['w_in'], RET_WIDTHS, RET_H), ('ret_w_o', 0): gm[3]['w_o']}
    dh, gm[2] = bm[2](dh, rider=ex.scatter({**ret_grads, **ffn_grads(2, gf[2])}))
    dh, gf[1] = bf[1](dh)
    dh, gm[1] = bm[1](dh, rider=ex.scatter({('s5_w_glu', 0): gm[2]['w_glu'], **ffn_grads(1, gf[1])}))
    dh, gf[0] = bf[0](dh)
    hgrn_grads = {('hgrn_w_in', 0): _head_deinterleave(gm[1]['w_in'], HG_WIDTHS, HG_H), ('hgrn_w_o', 0): gm[1]['w_o']}
    dh, gm[0] = bm[0](dh, rider=[ex.scatter({**hgrn_grads, **ffn_grads(0, gf[0])}), ex.swap()])
    a = gm[0]
    ex.scatter({('mla_w_down', 0): a['w_down'][:, :MLA_QL + MLA_KVL + MLA_ROPE], ('mla_w_uq', 0): _uq_from_heads(a['w_uq']),
                ('mla_w_ukv', 0): a['w_ukv'], ('mla_w_o', 0): a['w_o']}, name="scatter_mla")

    G = {}
    G['meta_tokens'] = dh[PAD:LEAD]
    G['norm_mix_g'] = jnp.concatenate([gm[i]['g'] for i in range(4)], axis=0)
    G['norm_ffn_g'] = jnp.concatenate([gf[i]['g'] for i in range(4)], axis=0)
    G['mla_cq_norm_g'], G['mla_ckv_norm_g'] = a['gcq'], a['gckv']
    G['mla_q_head_g'] = _join_head_gain(a['gqn'], a['gqr'])
    G['mla_k_head_g'] = _join_head_gain(a['gkn'], a['gkr'])
    G['hgrn_lb_logits'] = lb_vjp(gm[1]['lb'])[0]
    G['hgrn_o_norm_g'] = gm[1]['go']
    for n in ('lam_re', 'lam_im', 'log_dt', 'b_re', 'b_im', 'c_re', 'c_im'):
        G['s5_' + n] = gm[2][n][None]
    G['s5_d'] = gm[2]['dskip']
    G['ret_gn_g'] = gm[3]['gn']
    G['ffn_conv_w'] = jnp.stack([gf[i]['cw'] for i in range(4)])
    G['ffn_conv_b'] = jnp.concatenate([gf[i]['cb'] for i in range(4)], axis=0)
    return loss, dh[LEAD:], G


PACK_W = 1024
ANY = pl.BlockSpec(memory_space=pl.ANY)


def _pack(arrs, dtype, row_mult):
    flat = jnp.concatenate([a.reshape(-1).astype(dtype) for a in arrs])
    n = flat.shape[0]
    rows = -(-n // (PACK_W * row_mult)) * row_mult
    return jnp.pad(flat, (0, rows * PACK_W - n)).reshape(rows, PACK_W)


def _unpack(buf, shapes):
    flat = buf.reshape(-1)
    out, off = [], 0
    for s in shapes:
        n = math.prod(s)
        out.append(flat[off:off + n].reshape(s))
        off += n
    return out


def _my_pos():
    return lax.axis_index("x"), lax.axis_index("y"), lax.axis_index("c")


def _other_chips(x, y):
    return [(1 - x, y), (x, 1 - y), (1 - x, 1 - y)]


def gather_chips(name, src):
    def body(src_ref, out_ref, send_sems, recv_sems, local_sem):
        x, y, c = _my_pos()
        q = 2 * x + y
        mine = pltpu.make_async_copy(src_ref, out_ref.at[q], local_sem)
        mine.start()
        peers = _other_chips(x, y)

        def copy(k, slot, peer):
            return pltpu.make_async_remote_copy(src_ref=src_ref, dst_ref=out_ref.at[slot], send_sem=send_sems.at[k],
                                                recv_sem=recv_sems.at[k], device_id=(peer[0], peer[1], c),
                                                device_id_type=MESH_ID)
        sends = [copy(k, q, p) for k, p in enumerate(peers)]
        for cp in sends:
            cp.start()
        for k, p in enumerate(peers):
            copy(k, 2 * p[0] + p[1], p).wait_recv()
        for cp in sends:
            cp.wait_send()
        mine.wait()

    return pl.pallas_call(body, out_shape=jax.ShapeDtypeStruct((4,) + src.shape, src.dtype), in_specs=[ANY],
                          out_specs=ANY, name=name,
                          scratch_shapes=[pltpu.SemaphoreType.DMA((3,)), pltpu.SemaphoreType.DMA((3,)),
                                          pltpu.SemaphoreType.DMA(())])(src)


def scatter_chips(name, src):
    def body(src_ref, out_ref, send_sems, recv_sems, local_sem):
        x, y, c = _my_pos()
        q = 2 * x + y
        mine = pltpu.make_async_copy(src_ref.at[q], out_ref.at[q], local_sem)
        mine.start()
        peers = _other_chips(x, y)

        def copy(k, peer):
            slot = 2 * peer[0] + peer[1]
            return pltpu.make_async_remote_copy(src_ref=src_ref.at[slot], dst_ref=out_ref.at[q], send_sem=send_sems.at[k],
                                                recv_sem=recv_sems.at[k], device_id=(peer[0], peer[1], c),
                                                device_id_type=MESH_ID)

        def landing(k, peer):
            slot = 2 * peer[0] + peer[1]
            return pltpu.make_async_remote_copy(src_ref=src_ref.at[slot], dst_ref=out_ref.at[slot],
                                                send_sem=send_sems.at[k], recv_sem=recv_sems.at[k],
                                                device_id=(peer[0], peer[1], c), device_id_type=MESH_ID)
        sends = [copy(k, p) for k, p in enumerate(peers)]
        for cp in sends:
            cp.start()
        for k, p in enumerate(peers):
            landing(k, p).wait_recv()
        for cp in sends:
            cp.wait_send()
        mine.wait()

    return pl.pallas_call(body, out_shape=jax.ShapeDtypeStruct(src.shape, src.dtype), in_specs=[ANY], out_specs=ANY,
                          name=name, scratch_shapes=[pltpu.SemaphoreType.DMA((3,)), pltpu.SemaphoreType.DMA((3,)),
                                                     pltpu.SemaphoreType.DMA(())])(src)


def swap_sibling(name, src):
    def body(src_ref, out_ref, send_sem, recv_sem):
        x, y, c = _my_pos()
        cp = pltpu.make_async_remote_copy(src_ref=src_ref, dst_ref=out_ref, send_sem=send_sem, recv_sem=recv_sem,
                                          device_id=(x, y, 1 - c), device_id_type=MESH_ID)
        cp.start()
        cp.wait()

    return pl.pallas_call(body, out_shape=jax.ShapeDtypeStruct(src.shape, src.dtype), in_specs=[ANY], out_specs=ANY,
                          name=name, scratch_shapes=[pltpu.SemaphoreType.DMA(()), pltpu.SemaphoreType.DMA(())])(src)


def gather_all(name, src):
    def body(src_ref, out_ref, send_sems, recv_sems, local_sem):
        x, y, c = _my_pos()
        me = 4 * x + 2 * y + c
        mine = pltpu.make_async_copy(src_ref, out_ref.at[me], local_sem)
        mine.start()
        peers = [((1 - x) if m & 4 else x, (1 - y) if m & 2 else y, (1 - c) if m & 1 else c) for m in range(1, 8)]

        def copy(k, slot, peer):
            return pltpu.make_async_remote_copy(src_ref=src_ref, dst_ref=out_ref.at[slot], send_sem=send_sems.at[k],
                                                recv_sem=recv_sems.at[k], device_id=peer, device_id_type=MESH_ID)
        sends = [copy(k, me, p) for k, p in enumerate(peers)]
        for cp in sends:
            cp.start()
        for k, p in enumerate(peers):
            copy(k, 4 * p[0] + 2 * p[1] + p[2], p).wait_recv()
        for cp in sends:
            cp.wait_send()
        mine.wait()

    return pl.pallas_call(body, out_shape=jax.ShapeDtypeStruct((8,) + src.shape, src.dtype), in_specs=[ANY],
                          out_specs=ANY, name=name,
                          scratch_shapes=[pltpu.SemaphoreType.DMA((7,)), pltpu.SemaphoreType.DMA((7,)),
                                          pltpu.SemaphoreType.DMA(())])(src)


def _pack_tile(rows):
    return _divisors(rows, (256, 128, 64, 32, 16, 8))[0] if rows > 512 else rows


def sum_slots(name, slots):
    n, rows, w = slots.shape
    tr = _pack_tile(rows)

    def body(s_ref, o_ref):
        acc = s_ref[0].astype(F32)
        for k in range(1, n):
            acc = acc + s_ref[k].astype(F32)
        o_ref[...] = acc

    return pl.pallas_call(body, grid=(rows // tr,), in_specs=[pl.BlockSpec((n, tr, w), lambda i: (0, i, 0))],
                          out_specs=pl.BlockSpec((tr, w), lambda i: (i, 0)),
                          out_shape=jax.ShapeDtypeStruct((rows, w), F32), name=name, compiler_params=_cparams())(slots)


def adamw(name, grads, w, m, v):
    rows, wd = w.shape
    tr = _pack_tile(rows)
    ng = len(grads)

    def body(*refs):
        g = refs[0][...]
        for r in refs[1:ng]:
            g = g + r[...]
        w_ref, m_ref, v_ref = refs[ng:ng + 3]
        g_out, d_out, m_out, v_out = refs[ng + 3:]
        m_new = ADAM_B1 * m_ref[...] + (1.0 - ADAM_B1) * g
        v_new = ADAM_B2 * v_ref[...] + (1.0 - ADAM_B2) * jnp.square(g)
        m_hat = m_new / (1.0 - ADAM_B1 ** ADAM_STEP)
        v_hat = v_new / (1.0 - ADAM_B2 ** ADAM_STEP)
        g_out[...] = g
        d_out[...] = -ADAM_LR * (m_hat / (jnp.sqrt(v_hat) + ADAM_EPS) + ADAM_WD * w_ref[...])
        m_out[...] = m_new
        v_out[...] = v_new

    spec = pl.BlockSpec((tr, wd), lambda i: (i, 0))
    shape = jax.ShapeDtypeStruct((rows, wd), F32)
    return pl.pallas_call(body, grid=(rows // tr,), in_specs=[spec] * (ng + 3), out_specs=[spec] * 4,
                          out_shape=[shape] * 4, name=name, compiler_params=_cparams())(*grads, w, m, v)


def _shard_of(ref, name, p):
    ax = SHARD_AXIS[name]
    n = ref.shape[ax] // 4
    idx = [slice(None)] * 3
    idx[ax] = pl.ds(pl.multiple_of(p * n, 128 if ax == 2 else 16), n)
    return ref.at[tuple(idx)]


def _sem_scratch(nw):
    return [pltpu.SemaphoreType.DMA((3 * nw,)), pltpu.SemaphoreType.DMA((3 * nw,)), pltpu.SemaphoreType.DMA((nw,))]


def gather_weights(name, names, shards):
    nw = len(shards)

    def full_shape(n, s):
        return tuple(d * 4 if ax == SHARD_AXIS[n] else d for ax, d in enumerate(s.shape))

    def body(*refs):
        src, dst = refs[:nw], refs[nw:2 * nw]
        send_sems, recv_sems, local_sems = refs[2 * nw:]
        x, y, c = _my_pos()
        q = 2 * x + y
        peers = _other_chips(x, y)
        local = [pltpu.make_async_copy(src[w], _shard_of(dst[w], names[w], q), local_sems.at[w]) for w in range(nw)]
        for cp in local:
            cp.start()

        def copy(w, k, slot):
            p = peers[k]
            return pltpu.make_async_remote_copy(src_ref=src[w], dst_ref=_shard_of(dst[w], names[w], slot),
                                                send_sem=send_sems.at[3 * w + k], recv_sem=recv_sems.at[3 * w + k],
                                                device_id=(p[0], p[1], c), device_id_type=MESH_ID)
        sends = [copy(w, k, q) for w in range(nw) for k in range(3)]
        for cp in sends:
            cp.start()
        for w in range(nw):
            for k in range(3):
                copy(w, k, 2 * peers[k][0] + peers[k][1]).wait_recv()
        for cp in sends:
            cp.wait_send()
        for cp in local:
            cp.wait()

    return pl.pallas_call(body, out_shape=[jax.ShapeDtypeStruct(full_shape(n, s), s.dtype) for n, s in zip(names, shards)],
                          in_specs=[ANY] * nw, out_specs=[ANY] * nw, name=name, scratch_shapes=_sem_scratch(nw))(*shards)


def scatter_grads(name, names, grads):
    nw = len(grads)

    def shard_shape(n, s):
        return tuple(d // 4 if ax == SHARD_AXIS[n] else d for ax, d in enumerate(s.shape))

    def body(*refs):
        src, dst = refs[:nw], refs[nw:2 * nw]
        send_sems, recv_sems, local_sems = refs[2 * nw:]
        x, y, c = _my_pos()
        q = 2 * x + y
        peers = _other_chips(x, y)
        local = [pltpu.make_async_copy(_shard_of(src[w], names[w], q), dst[w].at[q], local_sems.at[w])
                 for w in range(nw)]
        for cp in local:
            cp.start()

        def copy(w, k, slot):
            p = peers[k]
            return pltpu.make_async_remote_copy(src_ref=_shard_of(src[w], names[w], 2 * p[0] + p[1]),
                                                dst_ref=dst[w].at[slot], send_sem=send_sems.at[3 * w + k],
                                                recv_sem=recv_sems.at[3 * w + k], device_id=(p[0], p[1], c),
                                                device_id_type=MESH_ID)
        sends = [copy(w, k, q) for w in range(nw) for k in range(3)]
        for cp in sends:
            cp.start()
        for w in range(nw):
            for k in range(3):
                copy(w, k, 2 * peers[k][0] + peers[k][1]).wait_recv()
        for cp in sends:
            cp.wait_send()
        for cp in local:
            cp.wait()

    return pl.pallas_call(body, out_shape=[jax.ShapeDtypeStruct((4,) + shard_shape(n, g), g.dtype)
                                           for n, g in zip(names, grads)],
                          in_specs=[ANY] * nw, out_specs=[ANY] * nw, name=name, scratch_shapes=_sem_scratch(nw))(*grads)


def swap_siblings(name, arrs):
    nw = len(arrs)

    def body(*refs):
        src, dst = refs[:nw], refs[nw:2 * nw]
        send_sems, recv_sems = refs[2 * nw:]
        x, y, c = _my_pos()
        cps = [pltpu.make_async_remote_copy(src_ref=src[w], dst_ref=dst[w], send_sem=send_sems.at[w],
                                            recv_sem=recv_sems.at[w], device_id=(x, y, 1 - c), device_id_type=MESH_ID)
               for w in range(nw)]
        for cp in cps:
            cp.start()
        for cp in cps:
            cp.wait()

    return pl.pallas_call(body, out_shape=[jax.ShapeDtypeStruct(a.shape, a.dtype) for a in arrs], in_specs=[ANY] * nw,
                          out_specs=[ANY] * nw, name=name,
                          scratch_shapes=[pltpu.SemaphoreType.DMA((nw,)), pltpu.SemaphoreType.DMA((nw,))])(*arrs)


def _block2d(ref, axis, p, n):
    if axis == 0:
        return ref.at[pl.ds(pl.multiple_of(p * n, 16), n), :]
    return ref.at[:, pl.ds(pl.multiple_of(p * n, 128), n)]


class Rider:
    def __init__(self, kind, items):
        self.kind, self.items = kind, items
        self.operands = [it[0] for it in items]
        self.results = None
        self.out_shapes = []
        for it in items:
            if kind == 'gather':
                arr, _, axis = it
                r, c = arr.shape[1:]
                shape = (4 * r, c) if axis == 0 else (r, 4 * c)
            else:
                arr, axis = it
                r, c = arr.shape
                shape = (4, r // 4, c) if axis == 0 else (4, r, c // 4)
            self.out_shapes.append(jax.ShapeDtypeStruct(shape, arr.dtype))
        self.scratch = _sem_scratch(len(items))

    def _copies(self, ins, outs, sems):
        send_sems, recv_sems, local_sems = sems
        x, y, c = _my_pos()
        q = 2 * x + y
        peers = _other_chips(x, y)
        local, sends, lands = [], [], []
        for w, it in enumerate(self.items):
            axis = it[-1]
            if self.kind == 'gather':
                src_of = lambda p, w=w, it=it: ins[w].at[it[1]]
                n = it[0].shape[1 + axis]
                dst_of = lambda p, w=w, axis=axis, n=n: _block2d(outs[w], axis, p, n)
                mine, theirs = q, (lambda p: q)
                landed = lambda p: p
            else:
                n = it[0].shape[axis] // 4
                src_of = lambda p, w=w, axis=axis, n=n: _block2d(ins[w], axis, p, n)
                dst_of = lambda p, w=w: outs[w].at[p]
                mine, theirs = q, (lambda p: q)
                landed = lambda p: p
            local.append(pltpu.make_async_copy(src_of(q), dst_of(mine), local_sems.at[w]))
            for k, (px, py) in enumerate(peers):
                p = 2 * px + py
                sems_k = dict(send_sem=send_sems.at[3 * w + k], recv_sem=recv_sems.at[3 * w + k],
                              device_id=(px, py, c), device_id_type=MESH_ID)
                sends.append(pltpu.make_async_remote_copy(src_ref=src_of(p), dst_ref=dst_of(theirs(p)), **sems_k))
                lands.append(pltpu.make_async_remote_copy(src_ref=src_of(p), dst_ref=dst_of(landed(p)), **sems_k))
        return local, sends, lands

    def start(self, ins, outs, sems):
        local, sends, _ = self._copies(ins, outs, sems)
        for cp in local + sends:
            cp.start()

    def finish(self, ins, outs, sems):
        local, sends, lands = self._copies(ins, outs, sems)
        for cp in lands:
            cp.wait_recv()
        for cp in sends:
            cp.wait_send()
        for cp in local:
            cp.wait()


class SwapRider:
    def __init__(self, arrs):
        self.operands = list(arrs)
        self.out_shapes = [jax.ShapeDtypeStruct(a.shape, a.dtype) for a in arrs]
        self.scratch = [pltpu.SemaphoreType.DMA((len(arrs),)), pltpu.SemaphoreType.DMA((len(arrs),))]
        self.results = None

    def _copies(self, ins, outs, sems):
        x, y, c = _my_pos()
        return [pltpu.make_async_remote_copy(src_ref=ins[w], dst_ref=outs[w], send_sem=sems[0].at[w],
                                             recv_sem=sems[1].at[w], device_id=(x, y, 1 - c), device_id_type=MESH_ID)
                for w in range(len(self.operands))]

    def start(self, ins, outs, sems):
        for cp in self._copies(ins, outs, sems):
            cp.start()

    def finish(self, ins, outs, sems):
        for cp in self._copies(ins, outs, sems):
            cp.wait()


class RiderGroup:
    def __init__(self, riders):
        self.riders = riders
        self.operands = [a for r in riders for a in r.operands]
        self.out_shapes = [s for r in riders for s in r.out_shapes]
        self.scratch = [s for r in riders for s in r.scratch]

    def _split(self, ins, outs, sems):
        for r in self.riders:
            ni, no, ns = len(r.operands), len(r.out_shapes), len(r.scratch)
            yield r, ins[:ni], outs[:no], sems[:ns]
            ins, outs, sems = ins[ni:], outs[no:], sems[ns:]

    def start(self, ins, outs, sems):
        for r, i, o, s in self._split(ins, outs, sems):
            r.start(i, o, s)

    def finish(self, ins, outs, sems):
        for r, i, o, s in self._split(ins, outs, sems):
            r.finish(i, o, s)

    @property
    def results(self):
        return None

    @results.setter
    def results(self, res):
        for r in self.riders:
            no = len(r.out_shapes)
            r.results, res = list(res[:no]), res[no:]


def run_rider(name, rider):
    n_in, n_out = len(rider.operands), len(rider.out_shapes)

    def body(*refs):
        ins, outs, sems = refs[:n_in], refs[n_in:n_in + n_out], refs[n_in + n_out:]
        rider.start(ins, outs, sems)
        rider.finish(ins, outs, sems)

    rider.results = list(pl.pallas_call(body, out_shape=rider.out_shapes, in_specs=[ANY] * n_in, out_specs=[ANY] * n_out,
                                        name=name, scratch_shapes=rider.scratch)(*rider.operands))


WEIGHT_GROUPS = {'mla': [('mla_w_down', 0), ('mla_w_uq', 0), ('mla_w_ukv', 0), ('mla_w_o', 0)],
                 'hgrn': [('hgrn_w_in', 0), ('hgrn_w_o', 0)], 's5': [('s5_w_glu', 0)],
                 'ret': [('ret_w_in', 0), ('ret_w_o', 0)]}
WEIGHT_GROUPS.update({f'ffn{i}': [('ffn_w_up', i), ('ffn_w_down', i)] for i in range(4)})


class Exchange:
    def __init__(self, shards=None, full=None):
        self.shards, self.full = shards, dict(full or {})
        self.got, self.recv, self.sib, self.grads = {}, {}, {}, {}

    def gather(self, groups, name=None):
        if self.shards is None:
            return None
        keys = [k for g in groups for k in WEIGHT_GROUPS[g]]
        rider = Rider('gather', [(self.shards[n], layer, SHARD_AXIS[n] - 1) for n, layer in keys])
        self.got.update({k: (rider, j) for j, k in enumerate(keys)})
        if name is not None:
            run_rider(name, rider)
        return rider

    def weight(self, n, layer=0):
        if self.shards is None:
            return self.full[n][layer]
        rider, j = self.got[(n, layer)]
        return rider.results[j]

    def scatter(self, grads, name=None):
        if self.shards is None:
            self.grads.update(grads)
            return None
        keys = list(grads)
        rider = Rider('scatter', [(grads[k], SHARD_AXIS[k[0]] - 1) for k in keys])
        self.recv.update({k: (rider, j) for j, k in enumerate(keys)})
        if name is not None:
            run_rider(name, rider)
        return rider

    def received(self, n, layer):
        rider, j = self.recv[(n, layer)]
        return rider.results[j]

    def swap(self, name=None):
        if self.shards is None:
            return None
        keys = [k for k, (r, _) in self.recv.items() if k not in self.sib and r.results is not None]
        rider = SwapRider([self.received(*k) for k in keys])
        self.sib.update({k: (rider, j) for j, k in enumerate(keys)})
        if name is not None:
            run_rider(name, rider)
        return rider

    def sibling(self, n, layer):
        rider, j = self.sib[(n, layer)]
        return rider.results[j]


ADAM_BLOCK_ELEMS = 256 * 1024


def adamw_shard(name, mine, sib, w, m, v):
    nl, rows, cols = w.shape
    tr = [t for t in (512, 384, 352, 256, 176, 128, 64, 32, 16) if rows % t == 0 and t * cols <= ADAM_BLOCK_ELEMS][0]

    def body(a_ref, b_ref, w_ref, m_ref, v_ref, g_out, d_out, m_out, v_out):
        def total(r):
            acc = r[0].astype(F32)
            for k in range(1, 4):
                acc = acc + r[k].astype(F32)
            return acc
        g = total(a_ref) + total(b_ref)
        m_new = ADAM_B1 * m_ref[...] + (1.0 - ADAM_B1) * g
        v_new = ADAM_B2 * v_ref[...] + (1.0 - ADAM_B2) * jnp.square(g)
        m_hat = m_new / (1.0 - ADAM_B1 ** ADAM_STEP)
        v_hat = v_new / (1.0 - ADAM_B2 ** ADAM_STEP)
        g_out[...] = g
        d_out[...] = -ADAM_LR * (m_hat / (jnp.sqrt(v_hat) + ADAM_EPS) + ADAM_WD * w_ref[...])
        m_out[...] = m_new
        v_out[...] = v_new

    slots = pl.BlockSpec((4, None, tr, cols), lambda l, i: (0, l, i, 0))
    spec = pl.BlockSpec((None, tr, cols), lambda l, i: (l, i, 0))
    shape = jax.ShapeDtypeStruct(w.shape, F32)
    return pl.pallas_call(body, grid=(nl, rows // tr), in_specs=[slots, slots, spec, spec, spec], out_specs=[spec] * 4,
                          out_shape=[shape] * 4, name=name, compiler_params=_cparams())(mine, sib, w, m, v)


def kernel(x, meta_tokens, norm_mix_g, norm_ffn_g, mla_w_down, mla_cq_norm_g, mla_ckv_norm_g, mla_w_uq, mla_w_ukv, mla_q_head_g, mla_k_head_g, mla_w_o, hgrn_w_in, hgrn_lb_logits, hgrn_o_norm_g, hgrn_w_o, s5_lam_re, s5_lam_im, s5_log_dt, s5_b_re, s5_b_im, s5_c_re, s5_c_im, s5_d, s5_w_glu, ret_w_in, ret_gn_g, ret_w_o, ffn_w_up, ffn_conv_w, ffn_conv_b, ffn_w_down, loss_target, m_meta_tokens, m_norm_mix_g, m_norm_ffn_g, m_mla_w_down, m_mla_cq_norm_g, m_mla_ckv_norm_g, m_mla_w_uq, m_mla_w_ukv, m_mla_q_head_g, m_mla_k_head_g, m_mla_w_o, m_hgrn_w_in, m_hgrn_lb_logits, m_hgrn_o_norm_g, m_hgrn_w_o, m_s5_lam_re, m_s5_lam_im, m_s5_log_dt, m_s5_b_re, m_s5_b_im, m_s5_c_re, m_s5_c_im, m_s5_d, m_s5_w_glu, m_ret_w_in, m_ret_gn_g, m_ret_w_o, m_ffn_w_up, m_ffn_conv_w, m_ffn_conv_b, m_ffn_w_down, v_meta_tokens, v_norm_mix_g, v_norm_ffn_g, v_mla_w_down, v_mla_cq_norm_g, v_mla_ckv_norm_g, v_mla_w_uq, v_mla_w_ukv, v_mla_q_head_g, v_mla_k_head_g, v_mla_w_o, v_hgrn_w_in, v_hgrn_lb_logits, v_hgrn_o_norm_g, v_hgrn_w_o, v_s5_lam_re, v_s5_lam_im, v_s5_log_dt, v_s5_b_re, v_s5_b_im, v_s5_c_re, v_s5_c_im, v_s5_d, v_s5_w_glu, v_ret_w_in, v_ret_gn_g, v_ret_w_o, v_ffn_w_up, v_ffn_conv_w, v_ffn_conv_b, v_ffn_w_down):
    vals = (x, meta_tokens, norm_mix_g, norm_ffn_g, mla_w_down, mla_cq_norm_g, mla_ckv_norm_g, mla_w_uq, mla_w_ukv, mla_q_head_g, mla_k_head_g, mla_w_o, hgrn_w_in, hgrn_lb_logits, hgrn_o_norm_g, hgrn_w_o, s5_lam_re, s5_lam_im, s5_log_dt, s5_b_re, s5_b_im, s5_c_re, s5_c_im, s5_d, s5_w_glu, ret_w_in, ret_gn_g, ret_w_o, ffn_w_up, ffn_conv_w, ffn_conv_b, ffn_w_down, loss_target, m_meta_tokens, m_norm_mix_g, m_norm_ffn_g, m_mla_w_down, m_mla_cq_norm_g, m_mla_ckv_norm_g, m_mla_w_uq, m_mla_w_ukv, m_mla_q_head_g, m_mla_k_head_g, m_mla_w_o, m_hgrn_w_in, m_hgrn_lb_logits, m_hgrn_o_norm_g, m_hgrn_w_o, m_s5_lam_re, m_s5_lam_im, m_s5_log_dt, m_s5_b_re, m_s5_b_im, m_s5_c_re, m_s5_c_im, m_s5_d, m_s5_w_glu, m_ret_w_in, m_ret_gn_g, m_ret_w_o, m_ffn_w_up, m_ffn_conv_w, m_ffn_conv_b, m_ffn_w_down, v_meta_tokens, v_norm_mix_g, v_norm_ffn_g, v_mla_w_down, v_mla_cq_norm_g, v_mla_ckv_norm_g, v_mla_w_uq, v_mla_w_ukv, v_mla_q_head_g, v_mla_k_head_g, v_mla_w_o, v_hgrn_w_in, v_hgrn_lb_logits, v_hgrn_o_norm_g, v_hgrn_w_o, v_s5_lam_re, v_s5_lam_im, v_s5_log_dt, v_s5_b_re, v_s5_b_im, v_s5_c_re, v_s5_c_im, v_s5_d, v_s5_w_glu, v_ret_w_in, v_ret_gn_g, v_ret_w_o, v_ffn_w_up, v_ffn_conv_w, v_ffn_conv_b, v_ffn_w_down)
    names = ['x'] + WEIGHTS + ['loss_target'] + ['m_' + n for n in WEIGHTS] + ['v_' + n for n in WEIGHTS]
    A = dict(zip(names, vals))
    q = 2 * lax.axis_index("x") + lax.axis_index("y")

    small_shapes = [A[n].shape for n in SMALL_SHARDED]
    got_small = gather_chips("gather_small", _pack([A[n] for n in SMALL_SHARDED], F32, 8))
    W = {n: A[n] for n in REPLICATED}
    parts_small = [_unpack(got_small[p], small_shapes) for p in range(4)]
    for k, n in enumerate(SMALL_SHARDED):
        W[n] = jnp.concatenate([parts_small[p][k] for p in range(4)], axis=SHARD_AXIS[n])

    ex = Exchange(shards={n: A[n].astype(BF16) for n in BIG})
    loss, grad_x, G = local_step(A['x'][0], A['loss_target'][0], W, ex)
    loss = lax.psum(loss, ("x", "y", "c"))

    ex.swap(name="grad_big_sibling")
    res_big = []
    for n in BIG:
        layers = range(A[n].shape[0])
        res_big.append(adamw_shard("adam_" + n, jnp.stack([ex.received(n, layer) for layer in layers], axis=1),
                                   jnp.stack([ex.sibling(n, layer) for layer in layers], axis=1),
                                   A[n], A['m_' + n], A['v_' + n]))

    small_names = REPLICATED + SMALL_SHARDED
    full_shapes = [G[n].shape for n in small_names]
    total = sum_slots("grad_small_sum", gather_all("grad_small_gather", _pack([G[n] for n in small_names], F32, 8)))
    gs = dict(zip(small_names, _unpack(total, full_shapes)))
    for n in SMALL_SHARDED:
        ax = SHARD_AXIS[n]
        size = gs[n].shape[ax] // 4
        gs[n] = lax.dynamic_slice_in_dim(gs[n], q * size, size, axis=ax)
    pk = lambda pre: _pack([A[pre + n] for n in small_names], F32, 8)
    own_shapes = [A[n].shape for n in small_names]
    res_small = [_unpack(r, own_shapes) for r in
                 adamw("adam_small", [_pack([gs[n] for n in small_names], F32, 8)], pk(''), pk('m_'), pk('v_'))]

    out = {}
    for j, kind in enumerate(('grad_', 'delta_', 'new_m_', 'new_v_')):
        for k, n in enumerate(BIG):
            out[kind + n] = res_big[k][j]
        for k, n in enumerate(small_names):
            out[kind + n] = res_small[j][k]
    return (loss, grad_x[None]) + tuple(out[kind + n] for kind in ('grad_', 'delta_', 'new_m_', 'new_v_')
                                        for n in WEIGHTS)
```

```python
import functools
import math

import jax
import jax.numpy as jnp
from jax import lax
from jax.experimental import pallas as pl
from jax.experimental.pallas import tpu as pltpu

F32, BF16 = jnp.float32, jnp.bfloat16
HIGHEST = lax.Precision.HIGHEST
MESH_ID = pl.DeviceIdType.MESH

D = 1024
N_META = 16
PAD = 112
LEAD = PAD + N_META
EPS = 1e-6
NEG_INF = -1e30
CHUNK = 64
VMEM_LIMIT_V7X = 56 * 1024 * 1024
MM_VMEM_BUDGET = 36 * 1024 * 1024

MLA_H, MLA_NOPE, MLA_ROPE, MLA_V = 8, 128, 64, 128
MLA_QK = MLA_NOPE + MLA_ROPE
MLA_QL, MLA_KVL = 384, 256
HG_H, HG_D, HG_C = 8, 128, 16
S5_G, S5_P, S5_K = 64, 64, 16
RET_H, RET_DK, RET_DV = 4, 256, 512
FFN_F = 2816

ADAM_LR, ADAM_B1, ADAM_B2, ADAM_EPS, ADAM_WD, ADAM_STEP = 0.001, 0.9, 0.999, 1e-08, 0.01, 10

WEIGHTS = ['meta_tokens', 'norm_mix_g', 'norm_ffn_g', 'mla_w_down', 'mla_cq_norm_g', 'mla_ckv_norm_g', 'mla_w_uq',
           'mla_w_ukv', 'mla_q_head_g', 'mla_k_head_g', 'mla_w_o', 'hgrn_w_in', 'hgrn_lb_logits', 'hgrn_o_norm_g',
           'hgrn_w_o', 's5_lam_re', 's5_lam_im', 's5_log_dt', 's5_b_re', 's5_b_im', 's5_c_re', 's5_c_im', 's5_d',
           's5_w_glu', 'ret_w_in', 'ret_gn_g', 'ret_w_o', 'ffn_w_up', 'ffn_conv_w', 'ffn_conv_b', 'ffn_w_down']
SHARD_AXIS = {'meta_tokens': 1, 'mla_w_down': 1, 'mla_w_uq': 2, 'mla_w_ukv': 2, 'mla_w_o': 1, 'hgrn_w_in': 2,
              'hgrn_w_o': 1, 's5_d': 1, 's5_w_glu': 2, 'ret_w_in': 2, 'ret_gn_g': 1, 'ret_w_o': 1, 'ffn_w_up': 2,
              'ffn_conv_w': 2, 'ffn_w_down': 1}
BIG = ['mla_w_down', 'mla_w_uq', 'mla_w_ukv', 'mla_w_o', 'hgrn_w_in', 'hgrn_w_o', 's5_w_glu', 'ret_w_in', 'ret_w_o',
       'ffn_w_up', 'ffn_w_down']
SMALL_SHARDED = ['meta_tokens', 's5_d', 'ret_gn_g', 'ffn_conv_w']
REPLICATED = [n for n in WEIGHTS if n not in SHARD_AXIS]


def _cparams():
    return pltpu.CompilerParams(vmem_limit_bytes=VMEM_LIMIT_V7X)


def _dg(a, b, ca, cb):
    return lax.dot_general(a.astype(BF16), b.astype(BF16), (((ca,), (cb,)), ((), ())),
                           preferred_element_type=F32)


@jax.custom_vjp
def mm_nn(a, b):
    return _dg(a, b, 1, 0)


@jax.custom_vjp
def mm_nt(a, b):
    return _dg(a, b, 1, 1)


@jax.custom_vjp
def mm_tn(a, b):
    return _dg(a, b, 0, 0)


mm_nn.defvjp(lambda a, b: (mm_nn(a, b), (a, b)),
             lambda r, g: (mm_nt(g, r[1]).astype(r[0].dtype), mm_tn(r[0], g).astype(r[1].dtype)))
mm_nt.defvjp(lambda a, b: (mm_nt(a, b), (a, b)),
             lambda r, g: (mm_nn(g, r[1]).astype(r[0].dtype), mm_tn(g, r[0]).astype(r[1].dtype)))
mm_tn.defvjp(lambda a, b: (mm_tn(a, b), (a, b)),
             lambda r, g: (mm_nt(r[1], g).astype(r[0].dtype), mm_nn(r[0], g).astype(r[1].dtype)))


def _dot_f32(a, b):
    return jnp.dot(a, b, precision=HIGHEST, preferred_element_type=F32)


def _shift_rows(x, s, up):
    n = x.shape[0]
    r = lax.broadcasted_iota(jnp.int32, x.shape, 0)
    if up:
        return jnp.where(r < n - s, pltpu.roll(x, n - s, 0), 0.0)
    return jnp.where(r >= s, pltpu.roll(x, s, 0), 0.0)


@functools.partial(jax.custom_vjp, nondiff_argnums=(1,))
def shift_down(x, s):
    return _shift_rows(x, s, False)


shift_down.defvjp(lambda x, s: (_shift_rows(x, s, False), None), lambda s, _, g: (_shift_rows(g, s, True),))


@functools.partial(jax.custom_vjp, nondiff_argnums=(1,))
def shift_up(x, s):
    return _shift_rows(x, s, True)


shift_up.defvjp(lambda x, s: (_shift_rows(x, s, True), None), lambda s, _, g: (_shift_rows(g, s, False),))


def _swap32_impl(x):
    ax = x.ndim - 1
    lane = lax.broadcasted_iota(jnp.int32, x.shape, ax)
    return jnp.where(lane < 32, pltpu.roll(x, 96, ax), jnp.where(lane < 64, pltpu.roll(x, 32, ax), 0.0))


@jax.custom_vjp
def swap32(x):
    return _swap32_impl(x)


swap32.defvjp(lambda x: (_swap32_impl(x), None), lambda _, g: (_swap32_impl(g),))


def _rms(x, g):
    return x * lax.rsqrt(jnp.mean(x * x, axis=-1, keepdims=True) + EPS) * g


def _silu(x):
    return x * jax.nn.sigmoid(x)


def _row_ids(pid, n, shape, axis=0):
    return pid * n + lax.broadcasted_iota(jnp.int32, shape, axis)


class Arg:
    def __init__(self, arr, block, imap, diff=True, gdtype=F32):
        self.arr, self.block, self.imap, self.diff, self.gdtype = arr, block, imap, diff, gdtype


class Out:
    def __init__(self, shape, dtype, block, imap):
        self.shape, self.dtype, self.block, self.imap = shape, dtype, block, imap


def _free_axes(imap, grid):
    ng = len(grid)
    base = tuple(imap(*([0] * ng)))
    free = []
    for ax in range(ng):
        p = [0] * ng
        p[ax] = 1
        if grid[ax] > 1 and tuple(imap(*p)) == base:
            free.append(ax)
    assert free == list(range(ng - len(free), ng)), "revisited blocks must be revisited on the innermost axes"
    return free


def _pallas(name, body, grid, in_specs, out_specs, out_shape, scratch, operands, rider=None):
    if isinstance(rider, (list, tuple)):
        riders = [r for r in rider if r is not None]
        rider = RiderGroup(riders) if riders else None
    if rider is None:
        return pl.pallas_call(body, grid=grid, in_specs=in_specs, out_specs=out_specs, out_shape=out_shape,
                              scratch_shapes=scratch, name=name, compiler_params=_cparams())(*operands)
    n_in, n_out, n_sc = len(in_specs), len(out_specs), len(scratch)
    r_in, r_out = len(rider.operands), len(rider.out_shapes)

    def body_with_rider(*refs):
        ins, refs = refs[:n_in], refs[n_in:]
        r_ins, refs = refs[:r_in], refs[r_in:]
        outs, refs = refs[:n_out], refs[n_out:]
        r_outs, refs = refs[:r_out], refs[r_out:]
        sc, r_sc = refs[:n_sc], refs[n_sc:]
        pids = [pl.program_id(a) for a in range(len(grid))]
        first = functools.reduce(jnp.logical_and, [p == 0 for p in pids])
        last = functools.reduce(jnp.logical_and, [p == g - 1 for p, g in zip(pids, grid)])

        @pl.when(first)
        def _():
            rider.start(r_ins, r_outs, r_sc)

        if hasattr(rider, 'middle'):
            step = functools.reduce(lambda acc, pg: acc * pg[1] + pg[0], zip(pids, grid), 0)

            @pl.when(step == (math.prod(grid) * 3) // 5)
            def _():
                rider.middle(r_ins, r_outs, r_sc)

        body(*ins, *outs, *sc)

        @pl.when(last)
        def _():
            rider.finish(r_ins, r_outs, r_sc)

    res = pl.pallas_call(body_with_rider, grid=grid, in_specs=list(in_specs) + [ANY] * r_in,
                         out_specs=list(out_specs) + [ANY] * r_out, out_shape=list(out_shape) + rider.out_shapes,
                         scratch_shapes=list(scratch) + rider.scratch, name=name,
                         compiler_params=_cparams())(*operands, *rider.operands)
    rider.results = list(res[n_out:])
    return res[:n_out]


def stage_fwd(name, fn, grid, args, outs, state_shape=None, rider=None):
    n_in, n_out, ng = len(args), len(outs), len(grid)

    def body(*refs):
        pids = tuple(pl.program_id(a) for a in range(ng))
        vals = [r[...] for r in refs[:n_in]]
        o_refs = refs[n_in:n_in + n_out]
        if state_shape is None:
            res = fn(pids, *vals)
        else:
            sv_ref, st_ref = refs[n_in + n_out], refs[n_in + n_out + 1]

            @pl.when(pids[-1] == 0)
            def _():
                st_ref[...] = jnp.zeros(state_shape, F32)

            s = st_ref[...]
            sv_ref[...] = s
            res = fn(pids, *vals, s)
            st_ref[...] = res[-1]
            res = res[:-1]
        for r, v in zip(o_refs, res):
            r[...] = v.astype(r.dtype)

    in_specs = [pl.BlockSpec(a.block, a.imap) for a in args]
    out_specs = [pl.BlockSpec(o.block, o.imap) for o in outs]
    out_shape = [jax.ShapeDtypeStruct(o.shape, o.dtype) for o in outs]
    scratch = []
    if state_shape is not None:
        nz = len(state_shape)
        out_specs.append(pl.BlockSpec((None, None) + tuple(state_shape), lambda i, j: (i, j) + (0,) * nz))
        out_shape.append(jax.ShapeDtypeStruct(tuple(grid) + tuple(state_shape), F32))
        scratch = [pltpu.VMEM(state_shape, F32)]
    return _pallas(name, body, grid, in_specs, out_specs, out_shape, scratch, [a.arr for a in args], rider)


def stage_bwd(name, fn, grid, args, outs, cots, state_shape=None, states=None, rider=None):
    n_in, n_out, ng = len(args), len(outs), len(grid)
    nb = grid[-1]
    rev = state_shape is not None
    didx = [k for k, a in enumerate(args) if a.diff]
    frees = [_free_axes(args[k].imap, grid) for k in didx]

    def eff(p):
        return tuple(p[:-1]) + (nb - 1 - p[-1],) if rev else tuple(p)

    def wrap(imap):
        return lambda *p: imap(*eff(p))

    def body(*refs):
        pids = tuple(pl.program_id(a) for a in range(ng))
        e = eff(pids)
        vals = [r[...] for r in refs[:n_in]]
        cts = tuple(r[...].astype(F32) for r in refs[n_in:n_in + n_out])
        pos = n_in + n_out
        if rev:
            st_in_ref = refs[pos]
            pos += 1
        g_refs = refs[pos:pos + len(didx)]
        pos += len(didx)
        dvals = [vals[k].astype(F32) for k in didx]

        def f(*dv):
            full = list(vals)
            for k, v in zip(didx, dv[:len(didx)]):
                full[k] = v
            return tuple(fn(e, *full, *dv[len(didx):]))

        if rev:
            ds_ref = refs[pos]

            @pl.when(pids[-1] == 0)
            def _():
                ds_ref[...] = jnp.zeros(state_shape, F32)

            _, vjp = jax.vjp(f, *dvals, st_in_ref[...])
            grads = vjp(cts + (ds_ref[...],))
            ds_ref[...] = grads[-1]
            grads = grads[:-1]
        else:
            _, vjp = jax.vjp(f, *dvals)
            grads = vjp(cts)
        for gref, g, free in zip(g_refs, grads, frees):
            g = g.astype(F32)
            if not free:
                gref[...] = g.astype(gref.dtype)
            else:
                first = functools.reduce(jnp.logical_and, [pids[ax] == 0 for ax in free])

                @pl.when(first)
                def _():
                    gref[...] = g

                @pl.when(jnp.logical_not(first))
                def _():
                    gref[...] += g

    in_specs = [pl.BlockSpec(a.block, wrap(a.imap)) for a in args]
    in_specs += [pl.BlockSpec(o.block, wrap(o.imap)) for o in outs]
    operands = [a.arr for a in args] + list(cots)
    scratch = []
    if rev:
        nz = len(state_shape)
        in_specs.append(pl.BlockSpec((None, None) + tuple(state_shape), lambda i, j: (i, nb - 1 - j) + (0,) * nz))
        operands.append(states)
        scratch = [pltpu.VMEM(state_shape, F32)]
    out_specs = [pl.BlockSpec(args[k].block, wrap(args[k].imap)) for k in didx]
    assert all(args[k].gdtype == F32 or not free for k, free in zip(didx, frees))
    out_shape = [jax.ShapeDtypeStruct(args[k].arr.shape, args[k].gdtype) for k in didx]
    return _pallas(name, body, grid, in_specs, out_specs, out_shape, scratch, operands, rider)


def _divisors(n, cands):
    return [c for c in cands if n % c == 0] or [n]


def _nbytes(dt):
    return jnp.dtype(dt).itemsize


def matmul(name, a, b, mode, out_dtype=F32, res=None, mask=False, res_mask=True, window=None, into=None):
    sa, sb, so = _nbytes(a.dtype), _nbytes(b.dtype), _nbytes(out_dtype)
    off, width = (window[0], window[1]) if window is not None else (0, None)
    if mode in ('nn', 'nt'):
        M, K = a.shape
        N = (width or b.shape[1]) if mode == 'nn' else b.shape[0]
        assert mode == 'nn' or width is None or width == K
        best = None
        for tm in _divisors(M, (1408, 1056, 768, 384, 128)):
            for tn in _divisors(N, (1408, 1024, 768, 512, 384, 256, 128)):
                est = 2 * (tm * K * sa + tn * K * sb + tm * tn * (so + (4 if res is not None else 0)))
                if est <= MM_VMEM_BUDGET and (best is None or tm * tn > best[0] * best[1]):
                    best = (tm, tn)
        tm, tn = best
        grid = (M // tm, N // tn)

        def body(*refs):
            a_ref, b_ref = refs[0], refs[1]
            o_ref = refs[-1]
            x = a_ref[...]
            rows = _row_ids(pl.program_id(0), tm, (tm, 1))
            if mask:
                x = jnp.where(rows >= PAD, x, jnp.zeros_like(x))
            acc = _dg(x, b_ref[...], 1, 0 if mode == 'nn' else 1)
            if res is not None:
                acc = refs[2][...] + (jnp.where(rows >= PAD, acc, 0.0) if res_mask else acc)
            o_ref[...] = acc.astype(o_ref.dtype)

        assert off % (tn if mode == 'nn' else K) == 0
        cb, kb = off // tn, off // K
        in_specs = [pl.BlockSpec((tm, K), lambda i, j: (i, 0)),
                    pl.BlockSpec((K, tn), lambda i, j: (0, j + cb)) if mode == 'nn' else
                    pl.BlockSpec((tn, K), lambda i, j: (j, kb))]
        ops = [a, b]
        if res is not None:
            in_specs.append(pl.BlockSpec((tm, tn), lambda i, j: (i, j)))
            ops.append(res)
        return pl.pallas_call(body, grid=grid, in_specs=in_specs,
                              out_specs=pl.BlockSpec((tm, tn), lambda i, j: (i, j)),
                              out_shape=jax.ShapeDtypeStruct((M, N), out_dtype), name=name,
                              compiler_params=_cparams())(*ops)
    assert mode == 'tn' and res is None
    M, K = a.shape
    N = b.shape[1]
    best = None
    for tk in _divisors(K, (1408, 1024, 768, 512, 384, 256, 128)):
        for tn in _divisors(N, (1408, 1024, 768, 512, 384, 256, 128)):
            est = 2 * (M * tk * sa + M * tn * sb + tk * tn * so)
            if est <= MM_VMEM_BUDGET and (best is None or tk * tn > best[0] * best[1]):
                best = (tk, tn)
    tk, tn = best

    def body_t(*refs):
        a_ref, b_ref, o_ref = refs[0], refs[1], refs[-1]
        y = b_ref[...]
        if mask:
            rows = lax.broadcasted_iota(jnp.int32, (M, 1), 0)
            y = jnp.where(rows >= PAD, y, jnp.zeros_like(y))
        o_ref[...] = _dg(a_ref[...], y, 0, 0).astype(o_ref.dtype)

    assert off % tn == 0
    cb = off // tn
    total = window[2] if window is not None else N
    in_specs = [pl.BlockSpec((M, tk), lambda i, j: (0, i)), pl.BlockSpec((M, tn), lambda i, j: (0, j))]
    ops, alias = [a, b], {}
    if into is not None:
        in_specs.append(ANY)
        ops.append(into)
        alias = {2: 0}
    return pl.pallas_call(body_t, grid=(K // tk, N // tn), in_specs=in_specs,
                          out_specs=pl.BlockSpec((tk, tn), lambda i, j: (i, j + cb)),
                          out_shape=jax.ShapeDtypeStruct((K, total), out_dtype), name=name,
                          input_output_aliases=alias, compiler_params=_cparams())(*ops)


def linear_bwd(name, act, w, dy, mask=False):
    return (matmul(name + "_da", dy, w, 'nt', mask=mask),
            matmul(name + "_dw", act, dy, 'tn', out_dtype=BF16, mask=mask))


def _row_tile(T):
    return _divisors(T, (384, 128))[0]


def _rows(arr, tm, gdtype=F32):
    return Arg(arr, (tm, arr.shape[1]), lambda i: (i, 0), gdtype=gdtype)


def _const(arr, diff=True):
    return Arg(arr, arr.shape, lambda *p: (0,) * arr.ndim, diff)


def _norm_fn(pids, h, g):
    return (_rms(h, g),)


def _norm_bwd_fn(pids, h, g):
    return (_rms(h, g), h)


def norm_fwd(name, h, g, dtype):
    T = h.shape[0]
    tm = _row_tile(T)
    return stage_fwd(name, _norm_fn, (T // tm,), [_rows(h, tm), _const(g)],
                     [Out((T, D), dtype, (tm, D), lambda i: (i, 0))])[0]


def norm_bwd(name, h, g, da, dh):
    T = h.shape[0]
    tm = _row_tile(T)
    o = Out((T, D), F32, (tm, D), lambda i: (i, 0))
    return stage_bwd(name, _norm_bwd_fn, (T // tm,), [_rows(h, tm), _const(g)], [o, o], [da, dh])


def _causal_conv3(u, cw, cb):
    return cw[2:3] * u + cw[1:2] * shift_down(u, 1) + cw[0:1] * shift_down(u, 2) + cb


def _ffn_act_fn(pids, ug, uv, cwg, cwv, cbg, cbv):
    return (_silu(_causal_conv3(ug, cwg, cbg)) * _causal_conv3(uv, cwv, cbv),)


def _ffn_act_args(ug, uv, cw, cb):
    T = ug.shape[0]
    col = lambda j: (0, j)
    args = [Arg(ug, (T, 128), col, gdtype=BF16), Arg(uv, (T, 128), col, gdtype=BF16),
            Arg(cw[:, :FFN_F], (3, 128), col), Arg(cw[:, FFN_F:], (3, 128), col),
            Arg(cb[:, :FFN_F], (1, 128), col), Arg(cb[:, FFN_F:], (1, 128), col)]
    outs = [Out((T, FFN_F), BF16, (T, 128), col)]
    return (FFN_F // 128,), args, outs


def _interleave_cols(w, n_parts, tile=128):
    lead = w.shape[:-1]
    n = w.shape[-1] // (n_parts * tile)
    k = len(lead)
    return w.reshape(lead + (n_parts, n, tile)).transpose(tuple(range(k)) + (k + 1, k, k + 2)).reshape(w.shape)


def _deinterleave_cols(w, n_parts, tile=128):
    lead = w.shape[:-1]
    n = w.shape[-1] // (n_parts * tile)
    k = len(lead)
    return w.reshape(lead + (n, n_parts, tile)).transpose(tuple(range(k)) + (k + 1, k, k + 2)).reshape(w.shape)


def ffn_layer(i, h, g, w_up, cw, cb, w_down):
    gate_w, val_w = (0, FFN_F, 2 * FFN_F), (FFN_F, FFN_F, 2 * FFN_F)
    b = norm_fwd(f"ffn{i}_norm", h, g, BF16)
    ug = matmul(f"ffn{i}_up_g", b, w_up, 'nn', window=gate_w)
    uv = matmul(f"ffn{i}_up_v", b, w_up, 'nn', window=val_w)
    grid, args, outs = _ffn_act_args(ug, uv, cw, cb)
    p = stage_fwd(f"ffn{i}_act", _ffn_act_fn, grid, args, outs)[0]
    h_new = matmul(f"ffn{i}_down", p, w_down, 'nn', res=h)

    def bwd(dh):
        dp, dwd = linear_bwd(f"ffn{i}_down_b", p, w_down, dh, mask=True)
        dug, duv, dcwg, dcwv, dcbg, dcbv = stage_bwd(f"ffn{i}_act_b", _ffn_act_fn, grid, args, outs, [dp])
        db = matmul(f"ffn{i}_up_b_da_g", dug, w_up, 'nt', window=gate_w)
        db = matmul(f"ffn{i}_up_b_da_v", duv, w_up, 'nt', window=val_w, res=db, res_mask=False)
        dwu = matmul(f"ffn{i}_up_b_dw_g", b, dug, 'tn', out_dtype=BF16, window=gate_w)
        dwu = matmul(f"ffn{i}_up_b_dw_v", b, duv, 'tn', out_dtype=BF16, window=val_w, into=dwu)
        dh2, dg = norm_bwd(f"ffn{i}_norm_b", h, g, db, dh)
        return dh2, dict(g=dg, w_up=dwu, cw=jnp.concatenate([dcwg, dcwv], axis=1),
                         cb=jnp.concatenate([dcbg, dcbv], axis=1), w_down=dwd)

    return h_new, bwd


def _mla_latent_fn(pids, down, gcq, gckv):
    cq = _rms(down[:, :MLA_QL], gcq)
    ckv = _rms(down[:, MLA_QL:MLA_QL + MLA_KVL], gckv)
    return cq, ckv, down[:, MLA_QL + MLA_KVL:]


def _rope64(x, cos, sin_signed):
    return x * cos + swap32(x) * sin_signed


def _mla_heads_fn(pids, qraw, kv, kpe, gqn, gqr, gkn, gkr, cos, sin_signed):
    qn, qr = qraw[:, :128], qraw[:, 128:]
    rq = lax.rsqrt((jnp.sum(qn * qn, -1, keepdims=True) + jnp.sum(qr * qr, -1, keepdims=True)) / MLA_QK + EPS)
    q = jnp.concatenate([qn * rq * gqn, _rope64(qr * rq * gqr, cos, sin_signed)], axis=1)
    kn, v = kv[:, :128], kv[:, 128:]
    rk = lax.rsqrt((jnp.sum(kn * kn, -1, keepdims=True) + jnp.sum(kpe * kpe, -1, keepdims=True)) / MLA_QK + EPS)
    k = jnp.concatenate([kn * rk * gkn, _rope64(kpe * rk * gkr, cos, sin_signed)], axis=1)
    return q, k, v


def _chunk_id(r):
    return jnp.where(r < LEAD, 0, 1 + lax.shift_right_arithmetic(r - LEAD, 6))


ATT_T = 384
ATT_SCALE = MLA_QK ** -0.5


def _att_scores(q, k_ref, qb, kb):
    ks = k_ref[pl.ds(pl.multiple_of(kb * ATT_T, ATT_T), ATT_T), :]
    s = _dg(q, ks, 1, 1) * ATT_SCALE
    qrow = _row_ids(qb, ATT_T, (ATT_T, 1))
    krow = _row_ids(kb, ATT_T, (1, ATT_T), axis=1)
    ok = jnp.logical_and(_chunk_id(krow) <= _chunk_id(qrow), krow >= PAD)
    return jnp.where(ok, s, NEG_INF), ks


def _att_rows(ref, b):
    return ref[pl.ds(pl.multiple_of(b * ATT_T, ATT_T), ATT_T), :]


def attention_fwd(q, k, v, rider=None):
    H, T, _ = q.shape
    nq = T // ATT_T

    def body(q_ref, k_ref, v_ref, o_ref, lse_ref):
        j = pl.program_id(1)
        qv = q_ref[...]

        def step(kb, carry):
            m, l, acc = carry
            s, _ = _att_scores(qv, k_ref, j, kb)
            m_new = jnp.maximum(m, jnp.max(s, axis=-1, keepdims=True))
            p = jnp.exp(s - m_new)
            alpha = jnp.exp(m - m_new)
            return (m_new, alpha * l + jnp.sum(p, axis=-1, keepdims=True),
                    alpha * acc + _dg(p, _att_rows(v_ref, kb), 1, 0))

        init = (jnp.full((ATT_T, 1), NEG_INF, F32), jnp.zeros((ATT_T, 1), F32), jnp.zeros((ATT_T, MLA_V), F32))
        m, l, acc = lax.fori_loop(0, j + 1, step, init)
        o_ref[...] = acc / l
        lse_ref[...] = m + jnp.log(l)

    return _pallas("mla_attn", body, (H, nq),
                   [pl.BlockSpec((None, ATT_T, 256), lambda hh, j: (hh, j, 0)),
                    pl.BlockSpec((None, T, 256), lambda hh, j: (hh, 0, 0)),
                    pl.BlockSpec((None, T, MLA_V), lambda hh, j: (hh, 0, 0))],
                   [pl.BlockSpec((ATT_T, MLA_V), lambda hh, j: (j, hh)),
                    pl.BlockSpec((None, ATT_T, 1), lambda hh, j: (hh, j, 0))],
                   [jax.ShapeDtypeStruct((T, H * MLA_V), F32), jax.ShapeDtypeStruct((H, T, 1), F32)], [],
                   [q, k, v], rider)


def attention_bwd(q, k, v, o, lse, do, rider=None):
    H, T, _ = q.shape
    nq = T // ATT_T

    def dq_body(q_ref, k_ref, v_ref, o_ref, do_ref, lse_ref, dq_ref, delta_ref):
        j = pl.program_id(1)
        qv, dov, lsev = q_ref[...], do_ref[...], lse_ref[...]
        delta = jnp.sum(dov * o_ref[...], axis=-1, keepdims=True)
        delta_ref[...] = delta

        def step(kb, acc):
            s, ks = _att_scores(qv, k_ref, j, kb)
            p = jnp.exp(s - lsev)
            ds = p * (_dg(dov, _att_rows(v_ref, kb), 1, 1) - delta) * ATT_SCALE
            return acc + _dg(ds, ks, 1, 0)

        dq_ref[...] = lax.fori_loop(0, j + 1, step, jnp.zeros((ATT_T, 256), F32))

    q_blk = pl.BlockSpec((None, ATT_T, 256), lambda hh, j: (hh, j, 0))
    k_all = pl.BlockSpec((None, T, 256), lambda hh, j: (hh, 0, 0))
    v_all = pl.BlockSpec((None, T, MLA_V), lambda hh, j: (hh, 0, 0))
    o_blk = pl.BlockSpec((ATT_T, MLA_V), lambda hh, j: (j, hh))
    col_blk = pl.BlockSpec((None, ATT_T, 1), lambda hh, j: (hh, j, 0))
    dq, delta = pl.pallas_call(dq_body, grid=(H, nq), in_specs=[q_blk, k_all, v_all, o_blk, o_blk, col_blk],
                               out_specs=[q_blk, col_blk],
                               out_shape=[jax.ShapeDtypeStruct((H, T, 256), F32), jax.ShapeDtypeStruct((H, T, 1), F32)],
                               name="mla_attn_dq", compiler_params=_cparams())(q, k, v, o, do, lse)

    def dkv_body(q_ref, k_ref, v_ref, do_ref, lse_ref, delta_ref, dk_ref, dv_ref):
        kb = pl.program_id(1)
        kv = k_ref[...]
        vv = v_ref[...]
        qrow0 = lax.broadcasted_iota(jnp.int32, (ATT_T, 1), 0)
        krow = _row_ids(kb, ATT_T, (1, ATT_T), axis=1)

        def step(qb, carry):
            dk, dv = carry
            qv, dov = _att_rows(q_ref, qb), _att_rows(do_ref, qb)
            s = _dg(qv, kv, 1, 1) * ATT_SCALE
            qrow = qb * ATT_T + qrow0
            ok = jnp.logical_and(_chunk_id(krow) <= _chunk_id(qrow), krow >= PAD)
            p = jnp.exp(jnp.where(ok, s, NEG_INF) - _att_rows(lse_ref, qb))
            ds = p * (_dg(dov, vv, 1, 1) - _att_rows(delta_ref, qb)) * ATT_SCALE
            return dk + _dg(ds, qv, 0, 0), dv + _dg(p, dov, 0, 0)

        dk, dv = lax.fori_loop(kb, nq, step, (jnp.zeros((ATT_T, 256), F32), jnp.zeros((ATT_T, MLA_V), F32)))
        dk_ref[...] = dk
        dv_ref[...] = dv

    q_all = pl.BlockSpec((None, T, 256), lambda hh, j: (hh, 0, 0))
    v_blk = pl.BlockSpec((None, ATT_T, MLA_V), lambda hh, j: (hh, j, 0))
    do_all = pl.BlockSpec((T, MLA_V), lambda hh, j: (0, hh))
    col_all = pl.BlockSpec((None, T, 1), lambda hh, j: (hh, 0, 0))
    dk, dv = _pallas("mla_attn_dkv", dkv_body, (H, nq), [q_all, q_blk, v_blk, do_all, col_all, col_all],
                     [q_blk, v_blk],
                     [jax.ShapeDtypeStruct((H, T, 256), F32), jax.ShapeDtypeStruct((H, T, MLA_V), F32)], [],
                     [q, k, v, do, lse, delta], rider)
    return dq, dk, dv


def mla_mixer(h, g, w, tabs, rider=None):
    T = h.shape[0]
    tm = _row_tile(T)
    nt = T // tm
    a = norm_fwd("mla_norm", h, g, BF16)
    down = matmul("mla_down", a, w['w_down'], 'nn')
    lat_args = [_rows(down, tm, BF16), _const(w['gcq']), _const(w['gckv'])]
    lat_outs = [Out((T, MLA_QL), BF16, (tm, MLA_QL), lambda i: (i, 0)),
                Out((T, MLA_KVL), BF16, (tm, MLA_KVL), lambda i: (i, 0)),
                Out((T, 128), F32, (tm, 128), lambda i: (i, 0))]
    cq, ckv, kpe = stage_fwd("mla_latent", _mla_latent_fn, (nt,), lat_args, lat_outs)
    qraw = matmul("mla_uq", cq, w['w_uq'], 'nn')
    kv = matmul("mla_ukv", ckv, w['w_ukv'], 'nn')
    hd_args = [Arg(qraw, (tm, 256), lambda i, hh: (i, hh), gdtype=BF16),
               Arg(kv, (tm, 256), lambda i, hh: (i, hh), gdtype=BF16),
               Arg(kpe, (tm, 128), lambda i, hh: (i, 0)),
               _const(w['gqn']), _const(w['gqr']), _const(w['gkn']), _const(w['gkr']),
               Arg(tabs['cos_a'], (tm, 128), lambda i, hh: (i, 0), False),
               Arg(tabs['sin_a'], (tm, 128), lambda i, hh: (i, 0), False)]
    hd_outs = [Out((MLA_H, T, 256), BF16, (None, tm, 256), lambda i, hh: (hh, i, 0)),
               Out((MLA_H, T, 256), BF16, (None, tm, 256), lambda i, hh: (hh, i, 0)),
               Out((MLA_H, T, 128), BF16, (None, tm, 128), lambda i, hh: (hh, i, 0))]
    q, k, v = stage_fwd("mla_heads", _mla_heads_fn, (nt, MLA_H), hd_args, hd_outs)
    o, lse = attention_fwd(q, k, v, rider=rider)
    h_new = matmul("mla_o", o, w['w_o'], 'nn', res=h)

    def bwd(dh, rider=None):
        do, dwo = linear_bwd("mla_o_b", o, w['w_o'], dh, mask=True)
        dq, dk, dv = attention_bwd(q, k, v, o, lse, do, rider=rider)
        dqraw, dkv, dkpe, dgqn, dgqr, dgkn, dgkr = stage_bwd("mla_heads_b", _mla_heads_fn, (nt, MLA_H), hd_args,
                                                             hd_outs, [dq, dk, dv])
        dcq, dwuq = linear_bwd("mla_uq_b", cq, w['w_uq'], dqraw)
        dckv, dwukv = linear_bwd("mla_ukv_b", ckv, w['w_ukv'], dkv)
        ddown, dgcq, dgckv = stage_bwd("mla_latent_b", _mla_latent_fn, (nt,), lat_args, lat_outs, [dcq, dckv, dkpe])
        da, dwdown = linear_bwd("mla_down_b", a, w['w_down'], ddown)
        dh2, dg = norm_bwd("mla_norm_b", h, g, da, dh)
        return dh2, dict(g=dg, w_down=dwdown, gcq=dgcq, gckv=dgckv, w_uq=dwuq, w_ukv=dwukv, gqn=dgqn, gqr=dgqr,
                         gkn=dgkn, gkr=dgkr, w_o=dwo)

    return h_new, bwd


HG_R = 128


def _hgrn_fn(pids, z, lb, go, st):
    R = z.shape[0]
    zq, zf, zi, zg = z[:, :128], z[:, 128:256], z[:, 256:384], z[:, 384:]
    assert R == 128
    q = _silu(zq)
    fg = lb + (1.0 - lb) * jax.nn.sigmoid(zf)
    logf = jnp.log(fg)
    k = 1.0 - fg
    row = lax.broadcasted_iota(jnp.int32, logf.shape, 0)
    pos = row & (HG_C - 1)
    cum, rev = logf, logf
    for d in (1, 2, 4, 8):
        cum = cum + jnp.where(pos >= d, shift_down(cum, d), 0.0)
        rev = rev + jnp.where(pos < HG_C - d, shift_up(rev, d), 0.0)
    cums, tots = [cum], [cum + rev - logf]
    for s in (16, 32, 64):
        odd = (row & s) != 0
        before = shift_down(tots[-1], s)
        cums.append(cums[-1] + jnp.where(odd, before, 0.0))
        tots.append(tots[-1] + jnp.where(odd, before, shift_up(tots[-1], s)))
    t = lax.broadcasted_iota(jnp.int32, (R, R), 0)
    j = lax.broadcasted_iota(jnp.int32, (R, R), 1)
    sh = lax.shift_right_arithmetic
    a = jnp.where(jnp.logical_and(sh(t, 4) == sh(j, 4), j <= t), mm_nt(q * jnp.exp(cum), k * jnp.exp(-cum)), 0.0)
    for n, s in enumerate((16, 32, 64)):
        m = jnp.logical_and(sh(t, 5 + n) == sh(j, 5 + n), jnp.logical_and((t & s) != 0, (j & s) == 0))
        a = a + jnp.where(m, mm_nt(q * jnp.exp(cums[n]), k * jnp.exp(tots[n] - cums[n])), 0.0)
    o = mm_nn(a, zi) + mm_nt(q * jnp.exp(cums[3]), st)
    st = st * jnp.exp(tots[3][0:1, :]) + mm_tn(zi, k * jnp.exp(tots[3] - cums[3]))
    return _rms(o, go) * _silu(zg), st


def hgrn_mixer(h, g, w, rider=None):
    T = h.shape[0]
    a = norm_fwd("hgrn_norm", h, g, BF16)
    z = matmul("hgrn_in", a, w['w_in'], 'nn')
    grid = (HG_H, T // HG_R)
    args = [Arg(z, (HG_R, 512), lambda hh, j: (j, hh), gdtype=BF16), Arg(w['lb'], (1, 128), lambda hh, j: (0, hh)),
            _const(w['go'])]
    outs = [Out((T, D), BF16, (HG_R, 128), lambda hh, j: (j, hh))]
    o, states = stage_fwd("hgrn_gla", _hgrn_fn, grid, args, outs, state_shape=(HG_D, HG_D), rider=rider)
    h_new = matmul("hgrn_o", o, w['w_o'], 'nn', res=h)

    def bwd(dh, rider=None):
        do, dwo = linear_bwd("hgrn_o_b", o, w['w_o'], dh, mask=True)
        dz, dlb, dgo = stage_bwd("hgrn_gla_b", _hgrn_fn, grid, args, outs, [do], state_shape=(HG_D, HG_D),
                                 states=states, rider=rider)
        da, dwin = linear_bwd("hgrn_in_b", a, w['w_in'], dz)
        dh2, dg = norm_bwd("hgrn_norm_b", h, g, da, dh)
        return dh2, dict(g=dg, w_in=dwin, lb=dlb, go=dgo, w_o=dwo)

    return h_new, bwd


S5_R = 128
S5_W = 512
S5_SLABS = D // 128


def _cmul(ar, ai, br, bi):
    return ar * br - ai * bi, ar * bi + ai * br


def _s5_scan(br, bi, tab, cr, ci, reverse):
    R, W = br.shape
    G = R // 8
    xr, xi = br.reshape(G, 8, W), bi.reshape(G, 8, W)
    for n, d in enumerate((1, 2, 4)):
        sh = (8 - d) if reverse else d
        mr, mi = _cmul(tab[2 * n][None], tab[2 * n + 1][None], pltpu.roll(xr, sh, 1), pltpu.roll(xi, sh, 1))
        xr, xi = xr + mr, xi + mi
    pr, pi = tab[6], tab[7]
    edge = 0 if reverse else 7
    out_r, out_i = [None] * G, [None] * G
    for g in (range(G - 1, -1, -1) if reverse else range(G)):
        ar, ai = _cmul(pr, pi, cr, ci)
        gr, gi = xr[g] + ar, xi[g] + ai
        cr, ci = gr[edge:edge + 1], gi[edge:edge + 1]
        out_r[g], out_i[g] = gr, gi
    return jnp.concatenate(out_r, axis=0), jnp.concatenate(out_i, axis=0), cr, ci


def s5_scan_fwd(a, bb, cb, tab, rider=None):
    T = a.shape[0]
    nb = T // S5_R

    def body(a_ref, bb_ref, cb_ref, tab_ref, y_ref, xs_ref, c_ref):
        @pl.when(pl.program_id(1) == 0)
        def _():
            c_ref[...] = jnp.zeros(c_ref.shape, F32)

        bu = _dg(a_ref[...], bb_ref[...], 1, 0)
        t = tab_ref[...]
        xr, xi, cr, ci = _s5_scan(bu[:, :S5_W], bu[:, S5_W:], t, c_ref[0:1, :S5_W], c_ref[0:1, S5_W:], False)
        x = jnp.concatenate([xr, xi], axis=1)
        xs_ref[...] = x
        y_ref[...] = _dg(x, cb_ref[...], 1, 0)
        c_ref[0:1, :] = jnp.concatenate([cr, ci], axis=1)

    return _pallas(
        "s5_scan", body, (S5_SLABS, nb),
        [pl.BlockSpec((S5_R, 128), lambda j, i: (i, j)),
         pl.BlockSpec((None, 128, 2 * S5_W), lambda j, i: (j, 0, 0)),
         pl.BlockSpec((None, 2 * S5_W, 128), lambda j, i: (j, 0, 0)),
         pl.BlockSpec((None, 10, 8, S5_W), lambda j, i: (j, 0, 0, 0))],
        [pl.BlockSpec((S5_R, 128), lambda j, i: (i, j)),
         pl.BlockSpec((None, S5_R, 2 * S5_W), lambda j, i: (j, i, 0))],
        [jax.ShapeDtypeStruct((T, D), F32), jax.ShapeDtypeStruct((S5_SLABS, T, 2 * S5_W), F32)],
        [pltpu.VMEM((8, 2 * S5_W), F32)], [a, bb, cb, tab], rider)


def s5_scan_bwd(a, bb, cb, tab_rev, xs, dy, rider=None):
    T = a.shape[0]
    nb = T // S5_R
    rg = S5_R // 8

    def body(a_ref, dy_ref, xs_ref, xp_ref, bb_ref, cb_ref, tab_ref, da_ref, dbb_ref, dcb_ref, dab_ref, c_ref):
        i = pl.program_id(1)

        @pl.when(i == 0)
        def _():
            c_ref[...] = jnp.zeros(c_ref.shape, F32)

        dy_v = dy_ref[...]
        x = xs_ref[...]
        dxo = _dg(dy_v, cb_ref[...], 1, 1)
        gr, gi, cr, ci = _s5_scan(dxo[:, :S5_W], dxo[:, S5_W:], tab_ref[...], c_ref[0:1, :S5_W], c_ref[0:1, S5_W:], True)
        c_ref[0:1, :] = jnp.concatenate([cr, ci], axis=1)
        g = jnp.concatenate([gr, gi], axis=1)
        da_ref[...] = _dg(g, bb_ref[...], 1, 1)
        dbb = _dg(a_ref[...], g, 0, 0)
        dcb = _dg(x, dy_v, 0, 0)
        first_tile = i == nb - 1
        prev_last = jnp.where(first_tile, 0.0, xp_ref[7:8, :])
        rows = lax.broadcasted_iota(jnp.int32, x.shape, 0)
        xp = jnp.where(rows == 0, prev_last, pltpu.roll(x, 1, 0))
        xpr, xpi = xp[:, :S5_W], xp[:, S5_W:]
        dar = (gr * xpr + gi * xpi).reshape(rg, 8, S5_W).sum(axis=0)
        dai = (gi * xpr - gr * xpi).reshape(rg, 8, S5_W).sum(axis=0)
        dab = jnp.concatenate([dar, dai], axis=1)

        @pl.when(i == 0)
        def _():
            dbb_ref[...] = dbb
            dcb_ref[...] = dcb
            dab_ref[...] = dab

        @pl.when(i != 0)
        def _():
            dbb_ref[...] += dbb
            dcb_ref[...] += dcb
            dab_ref[...] += dab

    def prev_rows(j, i):
        return (j, jnp.maximum((nb - 1 - i) * rg - 1, 0), 0)

    return _pallas(
        "s5_scan_b", body, (S5_SLABS, nb),
        [pl.BlockSpec((S5_R, 128), lambda j, i: (nb - 1 - i, j)),
         pl.BlockSpec((S5_R, 128), lambda j, i: (nb - 1 - i, j)),
         pl.BlockSpec((None, S5_R, 2 * S5_W), lambda j, i: (j, nb - 1 - i, 0)),
         pl.BlockSpec((None, 8, 2 * S5_W), prev_rows),
         pl.BlockSpec((None, 128, 2 * S5_W), lambda j, i: (j, 0, 0)),
         pl.BlockSpec((None, 2 * S5_W, 128), lambda j, i: (j, 0, 0)),
         pl.BlockSpec((None, 10, 8, S5_W), lambda j, i: (j, 0, 0, 0))],
        [pl.BlockSpec((S5_R, 128), lambda j, i: (nb - 1 - i, j)),
         pl.BlockSpec((None, 128, 2 * S5_W), lambda j, i: (j, 0, 0)),
         pl.BlockSpec((None, 2 * S5_W, 128), lambda j, i: (j, 0, 0)),
         pl.BlockSpec((None, 8, 2 * S5_W), lambda j, i: (j, 0, 0))],
        [jax.ShapeDtypeStruct((T, D), F32), jax.ShapeDtypeStruct((S5_SLABS, 128, 2 * S5_W), F32),
         jax.ShapeDtypeStruct((S5_SLABS, 2 * S5_W, 128), F32), jax.ShapeDtypeStruct((S5_SLABS, 8, 2 * S5_W), F32)],
        [pltpu.VMEM((8, 2 * S5_W), F32)], [a, dy, xs, xs, bb, cb, tab_rev], rider)


def _s5_discretise(lam_re, lam_im, log_dt, b_re, b_im, c_re, c_im):
    dt = jnp.exp(log_dt)[:, None]
    mag = jnp.exp(lam_re * dt)
    abar_re = mag * jnp.cos(lam_im * dt)
    abar_im = mag * jnp.sin(lam_im * dt)
    den = lam_re * lam_re + lam_im * lam_im
    zoh_re = ((abar_re - 1.0) * lam_re + abar_im * lam_im) / den
    zoh_im = (abar_im * lam_re - (abar_re - 1.0) * lam_im) / den
    bbar_re = zoh_re[..., None] * b_re - zoh_im[..., None] * b_im
    bbar_im = zoh_re[..., None] * b_im + zoh_im[..., None] * b_re
    eye = jnp.eye(8, dtype=F32)

    def in_map(bbar):
        t = bbar.reshape(8, 8, S5_P, S5_K).transpose(0, 1, 3, 2)
        return (t[:, :, :, None, :] * eye[None, :, None, :, None]).reshape(8, 8 * S5_K, 8 * S5_P)

    def out_map(c):
        t = c.reshape(8, 8, S5_K, S5_P).transpose(0, 1, 3, 2)
        return (t[:, :, :, None, :] * eye[None, :, None, :, None]).reshape(8, 8 * S5_P, 8 * S5_K)

    bb = jnp.concatenate([in_map(bbar_re), in_map(bbar_im)], axis=2)
    cb = jnp.concatenate([out_map(c_re), -out_map(c_im)], axis=1)
    return bb, cb, abar_re.reshape(8, S5_W), abar_im.reshape(8, S5_W)


def _s5_tables(ar, ai, reverse):
    if reverse:
        ai = -ai
    pw = [(jnp.ones_like(ar), jnp.zeros_like(ar))]
    for _ in range(8):
        pw.append(_cmul(pw[-1][0], pw[-1][1], ar, ai))
    r = jnp.arange(8)[None, :, None]
    rows = []
    for d in (1, 2, 4):
        keep = (r <= 7 - d) if reverse else (r >= d)
        rows += [jnp.where(keep, pw[d][0][:, None, :], 0.0), jnp.where(keep, pw[d][1][:, None, :], 0.0)]
    order = [8 - k for k in range(8)] if reverse else [k + 1 for k in range(8)]
    rows += [jnp.stack([pw[n][0] for n in order], axis=1), jnp.stack([pw[n][1] for n in order], axis=1)]
    rows += [jnp.broadcast_to(pw[8][0][:, None, :], (8, 8, S5_W)), jnp.broadcast_to(pw[8][1][:, None, :], (8, 8, S5_W))]
    return jnp.stack(rows, axis=1)


def _s5_act_fn(pids, yc, a, dskip):
    return (jax.nn.gelu(yc + dskip * a),)


def _make_glu_res_fn(tm):
    def glu_res_fn(pids, zz, h):
        rows = _row_ids(pids[0], tm, (tm, 1))
        return (h + jnp.where(rows >= PAD, zz[:, :D] * jax.nn.sigmoid(zz[:, D:]), 0.0),)
    return glu_res_fn


def s5_mixer(h, g, w, rider=None):
    T = h.shape[0]
    tm = _row_tile(T)
    nt = T // tm
    a = norm_fwd("s5_norm", h, g, F32)
    ssm = [w[n] for n in ('lam_re', 'lam_im', 'log_dt', 'b_re', 'b_im', 'c_re', 'c_im')]
    (bb, cb, ar, ai), disc_vjp = jax.vjp(_s5_discretise, *ssm)
    yc, xs = s5_scan_fwd(a, bb, cb, _s5_tables(ar, ai, False), rider=rider)
    row = lambda arr: _rows(arr, tm)
    act_args = [row(yc), row(a), _const(w['dskip'])]
    act_outs = [Out((T, D), BF16, (tm, D), lambda i: (i, 0))]
    y = stage_fwd("s5_act", _s5_act_fn, (nt,), act_args, act_outs)[0]
    zz = matmul("s5_glu", y, w['w_glu'], 'nn')
    glu_fn = _make_glu_res_fn(tm)
    glu_args = [_rows(zz, tm, BF16), row(h)]
    glu_outs = [Out((T, D), F32, (tm, D), lambda i: (i, 0))]
    h_new = stage_fwd("s5_gate", glu_fn, (nt,), glu_args, glu_outs)[0]

    def bwd(dh, rider=None):
        dzz, dh_res = stage_bwd("s5_gate_b", glu_fn, (nt,), glu_args, glu_outs, [dh])
        dy, dwglu = linear_bwd("s5_glu_b", y, w['w_glu'], dzz)
        dyc, da1, ddskip = stage_bwd("s5_act_b", _s5_act_fn, (nt,), act_args, act_outs, [dy])
        da2, dbb, dcb, dab = s5_scan_bwd(a, bb, cb, _s5_tables(ar, ai, True), xs, dyc, rider=rider)
        dab = dab.sum(axis=1)
        dssm = disc_vjp((dbb, dcb, dab[:, :S5_W], dab[:, S5_W:]))
        dh2, dg = _s5_norm_bwd(h, g, da1, da2, dh_res, tm)
        grads = dict(zip(('lam_re', 'lam_im', 'log_dt', 'b_re', 'b_im', 'c_re', 'c_im'), dssm))
        grads.update(g=dg, dskip=ddskip, w_glu=dwglu)
        return dh2, grads

    return h_new, bwd


def _norm3_bwd_fn(pids, h, g):
    a = _rms(h, g)
    return a, a, h


def _s5_norm_bwd(h, g, da1, da2, dh, tm):
    T = h.shape[0]
    o = Out((T, D), F32, (tm, D), lambda i: (i, 0))
    return stage_bwd("s5_norm_b", _norm3_bwd_fn, (T // tm,), [_rows(h, tm), _const(g)], [o, o, o], [da1, da2, dh])


RET_R = 128


def _rope256(x, cos, sin):
    x1, x2 = x[:, :128], x[:, 128:]
    return jnp.concatenate([x1 * cos - x2 * sin, x1 * sin + x2 * cos], axis=1)


def _ret_fn(pids, z, gn, cos, sin, dmat, qdec, kdec, cdec, st):
    R = z.shape[0]
    q = _rope256(z[:, :256], cos, sin)
    k = _rope256(z[:, 256:512], cos, sin) * (RET_DK ** -0.5)
    v, gate = z[:, 512:1024], z[:, 1024:]
    outs = []
    for cc in range(R // CHUNK):
        lo = cc * CHUNK
        qc, kc, vc = q[lo:lo + CHUNK], k[lo:lo + CHUNK], v[lo:lo + CHUNK]
        outs.append(mm_nn(mm_nt(qc, kc) * dmat, vc) + mm_nn(qc * qdec, st))
        st = st * cdec + mm_tn(kc * kdec, vc)
    o = jnp.concatenate(outs, axis=0)
    mu = jnp.mean(o, axis=-1, keepdims=True)
    var = jnp.mean(jnp.square(o - mu), axis=-1, keepdims=True)
    o = (o - mu) * lax.rsqrt(var + EPS)
    return o * gn * _silu(gate), st


def ret_mixer(h, g, w, tabs, rider=None):
    T = h.shape[0]
    a = norm_fwd("ret_norm", h, g, BF16)
    z = matmul("ret_in", a, w['w_in'], 'nn')
    grid = (RET_H, T // RET_R)
    hw = RET_DK * 2 + RET_DV * 2
    args = [Arg(z, (RET_R, hw), lambda hh, j: (j, hh), gdtype=BF16), Arg(w['gn'], (1, RET_DV), lambda hh, j: (0, hh)),
            Arg(tabs['cos_d'], (RET_R, 128), lambda hh, j: (j, 0), False),
            Arg(tabs['sin_d'], (RET_R, 128), lambda hh, j: (j, 0), False),
            Arg(tabs['ret_dmat'], (None, CHUNK, CHUNK), lambda hh, j: (hh, 0, 0), False),
            Arg(tabs['ret_qdec'], (None, CHUNK, 1), lambda hh, j: (hh, 0, 0), False),
            Arg(tabs['ret_kdec'], (None, CHUNK, 1), lambda hh, j: (hh, 0, 0), False),
            Arg(tabs['ret_cdec'], (None, 1, 1), lambda hh, j: (hh, 0, 0), False)]
    outs = [Out((T, RET_H * RET_DV), BF16, (RET_R, RET_DV), lambda hh, j: (j, hh))]
    o, states = stage_fwd("ret_chunks", _ret_fn, grid, args, outs, state_shape=(RET_DK, RET_DV), rider=rider)
    h_new = matmul("ret_o", o, w['w_o'], 'nn', res=h)

    def bwd(dh, rider=None):
        do, dwo = linear_bwd("ret_o_b", o, w['w_o'], dh, mask=True)
        dz, dgn = stage_bwd("ret_chunks_b", _ret_fn, grid, args, outs, [do], state_shape=(RET_DK, RET_DV),
                            states=states, rider=rider)
        da, dwin = linear_bwd("ret_in_b", a, w['w_in'], dz)
        dh2, dg = norm_bwd("ret_norm_b", h, g, da, dh)
        return dh2, dict(g=dg, w_in=dwin, gn=dgn, w_o=dwo)

    return h_new, bwd


def loss_head(h, tgt):
    T = h.shape[0]
    tm = _row_tile(T)

    def body(h_ref, t_ref, loss_ref, dh_ref):
        i = pl.program_id(0)
        rows = _row_ids(i, tm, (tm, 1))
        err = jnp.where(rows >= LEAD, h_ref[...] - t_ref[...], 0.0)
        dh_ref[...] = err * (1.0 / D)
        part = jnp.full((8, 128), 0.5 * jnp.sum(jnp.sum(err * err, axis=1, keepdims=True) * (1.0 / D)), F32)

        @pl.when(i == 0)
        def _():
            loss_ref[...] = part

        @pl.when(i != 0)
        def _():
            loss_ref[...] += part

    loss, dh = pl.pallas_call(
        body, grid=(T // tm,),
        in_specs=[pl.BlockSpec((tm, D), lambda i: (i, 0)), pl.BlockSpec((tm, D), lambda i: (i, 0))],
        out_specs=[pl.BlockSpec((8, 128), lambda i: (0, 0)), pl.BlockSpec((tm, D), lambda i: (i, 0))],
        out_shape=[jax.ShapeDtypeStruct((8, 128), F32), jax.ShapeDtypeStruct((T, D), F32)], name="loss_head",
        compiler_params=_cparams())(h, tgt)
    return loss[0, 0], dh


def _tables(T):
    pos = jnp.maximum(jnp.arange(T, dtype=jnp.int32) - PAD, 0).astype(F32)

    def cs(dim):
        inv_freq = 1.0 / (10000.0 ** (jnp.arange(0, dim, 2, dtype=F32) / dim))
        ang = pos[:, None] * inv_freq[None, :]
        return jnp.cos(ang), jnp.sin(ang)

    ca, sa = cs(MLA_ROPE)
    zeros = jnp.zeros((T, 64), F32)
    cd, sd = cs(RET_DK)
    log_gamma = jnp.log(1.0 - jnp.exp2(-5.0 - jnp.arange(RET_H, dtype=F32)))
    p = jnp.arange(CHUNK, dtype=F32)
    diff = p[:, None] - p[None, :]
    dmat = jnp.where(diff >= 0, jnp.exp(diff[None] * log_gamma[:, None, None]), 0.0)
    return dict(cos_a=jnp.concatenate([ca, ca, zeros], axis=1), sin_a=jnp.concatenate([-sa, sa, zeros], axis=1),
                cos_d=cd, sin_d=sd, ret_dmat=dmat,
                ret_qdec=jnp.exp((p[None, :] + 1.0) * log_gamma[:, None])[..., None],
                ret_kdec=jnp.exp((CHUNK - 1.0 - p[None, :]) * log_gamma[:, None])[..., None],
                ret_cdec=jnp.exp(CHUNK * log_gamma)[:, None, None])


def _hgrn_lower_bound(logits):
    lb_cum = jnp.cumsum(jax.nn.softmax(logits, axis=0), axis=0)
    return (lb_cum - lb_cum[0:1])[1:2]


def _uq_to_heads(w):
    t = w.reshape(w.shape[0], MLA_H, MLA_QK)
    return jnp.pad(t, ((0, 0), (0, 0), (0, 256 - MLA_QK))).reshape(w.shape[0], MLA_H * 256)


def _uq_from_heads(g):
    return g.reshape(g.shape[0], MLA_H, 256)[:, :, :MLA_QK].reshape(g.shape[0], MLA_H * MLA_QK)


def _head_interleave(w, widths, heads):
    parts, lo = [], 0
    for wd in widths:
        parts.append(w[:, lo:lo + heads * wd].reshape(w.shape[0], heads, wd))
        lo += heads * wd
    return jnp.concatenate(parts, axis=2).reshape(w.shape[0], -1)


def _head_deinterleave(g, widths, heads):
    t = g.reshape(g.shape[0], heads, sum(widths))
    parts, lo = [], 0
    for wd in widths:
        parts.append(t[:, :, lo:lo + wd].reshape(g.shape[0], heads * wd))
        lo += wd
    return jnp.concatenate(parts, axis=1)


HG_WIDTHS = (128, 128, 128, 128)
RET_WIDTHS = (RET_DK, RET_DK, RET_DV, RET_DV)


def _split_head_gain(g):
    return g[:, :128], jnp.pad(g[:, 128:], ((0, 0), (0, 64)))


def _join_head_gain(dn, dr):
    return jnp.concatenate([dn, dr[:, :64]], axis=1)


def local_step(x, target, W, ex):
    S = x.shape[0]
    T = S + LEAD
    tabs = _tables(T)
    h = jnp.concatenate([jnp.zeros((PAD, D), F32), W['meta_tokens'], x], axis=0)
    tgt = jnp.concatenate([jnp.zeros((LEAD, D), F32), target], axis=0)

    gqn, gqr = _split_head_gain(W['mla_q_head_g'])
    gkn, gkr = _split_head_gain(W['mla_k_head_g'])
    lb, lb_vjp = jax.vjp(_hgrn_lower_bound, W['hgrn_lb_logits'])

    def ffn(i, hh):
        return ffn_layer(i, hh, W['norm_ffn_g'][i:i + 1], ex.weight('ffn_w_up', i), W['ffn_conv_w'][i],
                         W['ffn_conv_b'][i:i + 1], ex.weight('ffn_w_down', i))

    bm, bf = [None] * 4, [None] * 4
    ex.gather(['mla'], name="gather_mla")
    w0 = dict(w_down=jnp.pad(ex.weight('mla_w_down'), ((0, 0), (0, 64))), gcq=W['mla_cq_norm_g'],
              gckv=W['mla_ckv_norm_g'], w_uq=_uq_to_heads(ex.weight('mla_w_uq')), w_ukv=ex.weight('mla_w_ukv'),
              gqn=gqn, gqr=gqr, gkn=gkn, gkr=gkr, w_o=ex.weight('mla_w_o'))
    h, bm[0] = mla_mixer(h, W['norm_mix_g'][0:1], w0, tabs, rider=ex.gather(['ffn0', 'hgrn', 'ffn1']))
    h, bf[0] = ffn(0, h)
    w1 = dict(w_in=_head_interleave(ex.weight('hgrn_w_in'), HG_WIDTHS, HG_H), lb=lb, go=W['hgrn_o_norm_g'],
              w_o=ex.weight('hgrn_w_o'))
    h, bm[1] = hgrn_mixer(h, W['norm_mix_g'][1:2], w1, rider=ex.gather(['s5', 'ffn2']))
    h, bf[1] = ffn(1, h)
    w2 = dict(lam_re=W['s5_lam_re'][0], lam_im=W['s5_lam_im'][0], log_dt=W['s5_log_dt'][0], b_re=W['s5_b_re'][0],
              b_im=W['s5_b_im'][0], c_re=W['s5_c_re'][0], c_im=W['s5_c_im'][0], dskip=W['s5_d'],
              w_glu=ex.weight('s5_w_glu'))
    h, bm[2] = s5_mixer(h, W['norm_mix_g'][2:3], w2, rider=ex.gather(['ret']))
    h, bf[2] = ffn(2, h)
    w3 = dict(w_in=_head_interleave(ex.weight('ret_w_in'), RET_WIDTHS, RET_H), gn=W['ret_gn_g'],
              w_o=ex.weight('ret_w_o'))
    h, bm[3] = ret_mixer(h, W['norm_mix_g'][3:4], w3, tabs, rider=ex.gather(['ffn3']))
    h, bf[3] = ffn(3, h)

    loss, dh = loss_head(h, tgt)

    def ffn_grads(i, g):
        return {('ffn_w_up', i): g['w_up'], ('ffn_w_down', i): g['w_down']}

    gm, gf = [None] * 4, [None] * 4
    dh, gf[3] = bf[3](dh)
    dh, gm[3] = bm[3](dh, rider=ex.scatter(ffn_grads(3, gf[3])))
    dh, gf[2] = bf[2](dh)
    ret_grads = {('ret_w_in', 0): _head_deinterleave(gm[3]['w_in'], RET_WIDTHS, RET_H), ('ret_w_o', 0): gm[3]['w_o']}
    dh, gm[2] = bm[2](dh, rider=ex.scatter({**ret_grads, **ffn_grads(2, gf[2])}))
    dh, gf[1] = bf[1](dh)
    dh, gm[1] = bm[1](dh, rider=ex.scatter({('s5_w_glu', 0): gm[2]['w_glu'], **ffn_grads(1, gf[1])}))
    dh, gf[0] = bf[0](dh)
    hgrn_grads = {('hgrn_w_in', 0): _head_deinterleave(gm[1]['w_in'], HG_WIDTHS, HG_H), ('hgrn_w_o', 0): gm[1]['w_o']}
    dh, gm[0] = bm[0](dh, rider=[ex.scatter({**hgrn_grads, **ffn_grads(0, gf[0])}), ex.swap()])
    a = gm[0]
    ex.scatter({('mla_w_down', 0): a['w_down'][:, :MLA_QL + MLA_KVL + MLA_ROPE], ('mla_w_uq', 0): _uq_from_heads(a['w_uq']),
                ('mla_w_ukv', 0): a['w_ukv'], ('mla_w_o', 0): a['w_o']}, name="scatter_mla")

    G = {}
    G['meta_tokens'] = dh[PAD:LEAD]
    G['norm_mix_g'] = jnp.concatenate([gm[i]['g'] for i in range(4)], axis=0)
    G['norm_ffn_g'] = jnp.concatenate([gf[i]['g'] for i in range(4)], axis=0)
    G['mla_cq_norm_g'], G['mla_ckv_norm_g'] = a['gcq'], a['gckv']
    G['mla_q_head_g'] = _join_head_gain(a['gqn'], a['gqr'])
    G['mla_k_head_g'] = _join_head_gain(a['gkn'], a['gkr'])
    G['hgrn_lb_logits'] = lb_vjp(gm[1]['lb'])[0]
    G['hgrn_o_norm_g'] = gm[1]['go']
    for n in ('lam_re', 'lam_im', 'log_dt', 'b_re', 'b_im', 'c_re', 'c_im'):
        G['s5_' + n] = gm[2][n][None]
    G['s5_d'] = gm[2]['dskip']
    G['ret_gn_g'] = gm[3]['gn']
    G['ffn_conv_w'] = jnp.stack([gf[i]['cw'] for i in range(4)])
    G['ffn_conv_b'] = jnp.concatenate([gf[i]['cb'] for i in range(4)], axis=0)
    return loss, dh[LEAD:], G


PACK_W = 1024
ANY = pl.BlockSpec(memory_space=pl.ANY)


def _pack(arrs, dtype, row_mult):
    flat = jnp.concatenate([a.reshape(-1).astype(dtype) for a in arrs])
    n = flat.shape[0]
    rows = -(-n // (PACK_W * row_mult)) * row_mult
    return jnp.pad(flat, (0, rows * PACK_W - n)).reshape(rows, PACK_W)


def _unpack(buf, shapes):
    flat = buf.reshape(-1)
    out, off = [], 0
    for s in shapes:
        n = math.prod(s)
        out.append(flat[off:off + n].reshape(s))
        off += n
    return out


def _my_pos():
    return lax.axis_index("x"), lax.axis_index("y"), lax.axis_index("c")


def _other_chips(x, y):
    return [(1 - x, y), (x, 1 - y), (1 - x, 1 - y)]


def gather_chips(name, src):
    def body(src_ref, out_ref, send_sems, recv_sems, local_sem):
        x, y, c = _my_pos()
        q = 2 * x + y
        mine = pltpu.make_async_copy(src_ref, out_ref.at[q], local_sem)
        mine.start()
        peers = _other_chips(x, y)

        def copy(k, slot, peer):
            return pltpu.make_async_remote_copy(src_ref=src_ref, dst_ref=out_ref.at[slot], send_sem=send_sems.at[k],
                                                recv_sem=recv_sems.at[k], device_id=(peer[0], peer[1], c),
                                                device_id_type=MESH_ID)
        sends = [copy(k, q, p) for k, p in enumerate(peers)]
        for cp in sends:
            cp.start()
        for k, p in enumerate(peers):
            copy(k, 2 * p[0] + p[1], p).wait_recv()
        for cp in sends:
            cp.wait_send()
        mine.wait()

    return pl.pallas_call(body, out_shape=jax.ShapeDtypeStruct((4,) + src.shape, src.dtype), in_specs=[ANY],
                          out_specs=ANY, name=name,
                          scratch_shapes=[pltpu.SemaphoreType.DMA((3,)), pltpu.SemaphoreType.DMA((3,)),
                                          pltpu.SemaphoreType.DMA(())])(src)


def scatter_chips(name, src):
    def body(src_ref, out_ref, send_sems, recv_sems, local_sem):
        x, y, c = _my_pos()
        q = 2 * x + y
        mine = pltpu.make_async_copy(src_ref.at[q], out_ref.at[q], local_sem)
        mine.start()
        peers = _other_chips(x, y)

        def copy(k, peer):
            slot = 2 * peer[0] + peer[1]
            return pltpu.make_async_remote_copy(src_ref=src_ref.at[slot], dst_ref=out_ref.at[q], send_sem=send_sems.at[k],
                                                recv_sem=recv_sems.at[k], device_id=(peer[0], peer[1], c),
                                                device_id_type=MESH_ID)

        def landing(k, peer):
            slot = 2 * peer[0] + peer[1]
            return pltpu.make_async_remote_copy(src_ref=src_ref.at[slot], dst_ref=out_ref.at[slot],
                                                send_sem=send_sems.at[k], recv_sem=recv_sems.at[k],
                                                device_id=(peer[0], peer[1], c), device_id_type=MESH_ID)
        sends = [copy(k, p) for k, p in enumerate(peers)]
        for cp in sends:
            cp.start()
        for k, p in enumerate(peers):
            landing(k, p).wait_recv()
        for cp in sends:
            cp.wait_send()
        mine.wait()

    return pl.pallas_call(body, out_shape=jax.ShapeDtypeStruct(src.shape, src.dtype), in_specs=[ANY], out_specs=ANY,
                          name=name, scratch_shapes=[pltpu.SemaphoreType.DMA((3,)), pltpu.SemaphoreType.DMA((3,)),
                                                     pltpu.SemaphoreType.DMA(())])(src)


def swap_sibling(name, src):
    def body(src_ref, out_ref, send_sem, recv_sem):
        x, y, c = _my_pos()
        cp = pltpu.make_async_remote_copy(src_ref=src_ref, dst_ref=out_ref, send_sem=send_sem, recv_sem=recv_sem,
                                          device_id=(x, y, 1 - c), device_id_type=MESH_ID)
        cp.start()
        cp.wait()

    return pl.pallas_call(body, out_shape=jax.ShapeDtypeStruct(src.shape, src.dtype), in_specs=[ANY], out_specs=ANY,
                          name=name, scratch_shapes=[pltpu.SemaphoreType.DMA(()), pltpu.SemaphoreType.DMA(())])(src)


def gather_all(name, src):
    def body(src_ref, out_ref, send_sems, recv_sems, local_sem):
        x, y, c = _my_pos()
        me = 4 * x + 2 * y + c
        mine = pltpu.make_async_copy(src_ref, out_ref.at[me], local_sem)
        mine.start()
        peers = [((1 - x) if m & 4 else x, (1 - y) if m & 2 else y, (1 - c) if m & 1 else c) for m in range(1, 8)]

        def copy(k, slot, peer):
            return pltpu.make_async_remote_copy(src_ref=src_ref, dst_ref=out_ref.at[slot], send_sem=send_sems.at[k],
                                                recv_sem=recv_sems.at[k], device_id=peer, device_id_type=MESH_ID)
        sends = [copy(k, me, p) for k, p in enumerate(peers)]
        for cp in sends:
            cp.start()
        for k, p in enumerate(peers):
            copy(k, 4 * p[0] + 2 * p[1] + p[2], p).wait_recv()
        for cp in sends:
            cp.wait_send()
        mine.wait()

    return pl.pallas_call(body, out_shape=jax.ShapeDtypeStruct((8,) + src.shape, src.dtype), in_specs=[ANY],
                          out_specs=ANY, name=name,
                          scratch_shapes=[pltpu.SemaphoreType.DMA((7,)), pltpu.SemaphoreType.DMA((7,)),
                                          pltpu.SemaphoreType.DMA(())])(src)


def _pack_tile(rows):
    return _divisors(rows, (256, 128, 64, 32, 16, 8))[0] if rows > 512 else rows


def sum_slots(name, slots):
    n, rows, w = slots.shape
    tr = _pack_tile(rows)

    def body(s_ref, o_ref):
        acc = s_ref[0].astype(F32)
        for k in range(1, n):
            acc = acc + s_ref[k].astype(F32)
        o_ref[...] = acc

    return pl.pallas_call(body, grid=(rows // tr,), in_specs=[pl.BlockSpec((n, tr, w), lambda i: (0, i, 0))],
                          out_specs=pl.BlockSpec((tr, w), lambda i: (i, 0)),
                          out_shape=jax.ShapeDtypeStruct((rows, w), F32), name=name, compiler_params=_cparams())(slots)


def adamw(name, grads, w, m, v):
    rows, wd = w.shape
    tr = _pack_tile(rows)
    ng = len(grads)

    def body(*refs):
        g = refs[0][...]
        for r in refs[1:ng]:
            g = g + r[...]
        w_ref, m_ref, v_ref = refs[ng:ng + 3]
        g_out, d_out, m_out, v_out = refs[ng + 3:]
        m_new = ADAM_B1 * m_ref[...] + (1.0 - ADAM_B1) * g
        v_new = ADAM_B2 * v_ref[...] + (1.0 - ADAM_B2) * jnp.square(g)
        m_hat = m_new / (1.0 - ADAM_B1 ** ADAM_STEP)
        v_hat = v_new / (1.0 - ADAM_B2 ** ADAM_STEP)
        g_out[...] = g
        d_out[...] = -ADAM_LR * (m_hat / (jnp.sqrt(v_hat) + ADAM_EPS) + ADAM_WD * w_ref[...])
        m_out[...] = m_new
        v_out[...] = v_new

    spec = pl.BlockSpec((tr, wd), lambda i: (i, 0))
    shape = jax.ShapeDtypeStruct((rows, wd), F32)
    return pl.pallas_call(body, grid=(rows // tr,), in_specs=[spec] * (ng + 3), out_specs=[spec] * 4,
                          out_shape=[shape] * 4, name=name, compiler_params=_cparams())(*grads, w, m, v)


def _shard_of(ref, name, p):
    ax = SHARD_AXIS[name]
    n = ref.shape[ax] // 4
    idx = [slice(None)] * 3
    idx[ax] = pl.ds(pl.multiple_of(p * n, 128 if ax == 2 else 16), n)
    return ref.at[tuple(idx)]


def _sem_scratch(nw):
    return [pltpu.SemaphoreType.DMA((3 * nw,)), pltpu.SemaphoreType.DMA((3 * nw,)), pltpu.SemaphoreType.DMA((nw,))]


def gather_weights(name, names, shards):
    nw = len(shards)

    def full_shape(n, s):
        return tuple(d * 4 if ax == SHARD_AXIS[n] else d for ax, d in enumerate(s.shape))

    def body(*refs):
        src, dst = refs[:nw], refs[nw:2 * nw]
        send_sems, recv_sems, local_sems = refs[2 * nw:]
        x, y, c = _my_pos()
        q = 2 * x + y
        peers = _other_chips(x, y)
        local = [pltpu.make_async_copy(src[w], _shard_of(dst[w], names[w], q), local_sems.at[w]) for w in range(nw)]
        for cp in local:
            cp.start()

        def copy(w, k, slot):
            p = peers[k]
            return pltpu.make_async_remote_copy(src_ref=src[w], dst_ref=_shard_of(dst[w], names[w], slot),
                                                send_sem=send_sems.at[3 * w + k], recv_sem=recv_sems.at[3 * w + k],
                                                device_id=(p[0], p[1], c), device_id_type=MESH_ID)
        sends = [copy(w, k, q) for w in range(nw) for k in range(3)]
        for cp in sends:
            cp.start()
        for w in range(nw):
            for k in range(3):
                copy(w, k, 2 * peers[k][0] + peers[k][1]).wait_recv()
        for cp in sends:
            cp.wait_send()
        for cp in local:
            cp.wait()

    return pl.pallas_call(body, out_shape=[jax.ShapeDtypeStruct(full_shape(n, s), s.dtype) for n, s in zip(names, shards)],
                          in_specs=[ANY] * nw, out_specs=[ANY] * nw, name=name, scratch_shapes=_sem_scratch(nw))(*shards)


def scatter_grads(name, names, grads):
    nw = len(grads)

    def shard_shape(n, s):
        return tuple(d // 4 if ax == SHARD_AXIS[n] else d for ax, d in enumerate(s.shape))

    def body(*refs):
        src, dst = refs[:nw], refs[nw:2 * nw]
        send_sems, recv_sems, local_sems = refs[2 * nw:]
        x, y, c = _my_pos()
        q = 2 * x + y
        peers = _other_chips(x, y)
        local = [pltpu.make_async_copy(_shard_of(src[w], names[w], q), dst[w].at[q], local_sems.at[w])
                 for w in range(nw)]
        for cp in local:
            cp.start()

        def copy(w, k, slot):
            p = peers[k]
            return pltpu.make_async_remote_copy(src_ref=_shard_of(src[w], names[w], 2 * p[0] + p[1]),
                                                dst_ref=dst[w].at[slot], send_sem=send_sems.at[3 * w + k],
                                                recv_sem=recv_sems.at[3 * w + k], device_id=(p[0], p[1], c),
                                                device_id_type=MESH_ID)
        sends = [copy(w, k, q) for w in range(nw) for k in range(3)]
        for cp in sends:
            cp.start()
        for w in range(nw):
            for k in range(3):
                copy(w, k, 2 * peers[k][0] + peers[k][1]).wait_recv()
        for cp in sends:
            cp.wait_send()
        for cp in local:
            cp.wait()

    return pl.pallas_call(body, out_shape=[jax.ShapeDtypeStruct((4,) + shard_shape(n, g), g.dtype)
                                           for n, g in zip(names, grads)],
                          in_specs=[ANY] * nw, out_specs=[ANY] * nw, name=name, scratch_shapes=_sem_scratch(nw))(*grads)


def swap_siblings(name, arrs):
    nw = len(arrs)

    def body(*refs):
        src, dst = refs[:nw], refs[nw:2 * nw]
        send_sems, recv_sems = refs[2 * nw:]
        x, y, c = _my_pos()
        cps = [pltpu.make_async_remote_copy(src_ref=src[w], dst_ref=dst[w], send_sem=send_sems.at[w],
                                            recv_sem=recv_sems.at[w], device_id=(x, y, 1 - c), device_id_type=MESH_ID)
               for w in range(nw)]
        for cp in cps:
            cp.start()
        for cp in cps:
            cp.wait()

    return pl.pallas_call(body, out_shape=[jax.ShapeDtypeStruct(a.shape, a.dtype) for a in arrs], in_specs=[ANY] * nw,
                          out_specs=[ANY] * nw, name=name,
                          scratch_shapes=[pltpu.SemaphoreType.DMA((nw,)), pltpu.SemaphoreType.DMA((nw,))])(*arrs)


def _block2d(ref, axis, p, n):
    if axis == 0:
        return ref.at[pl.ds(pl.multiple_of(p * n, 16), n), :]
    return ref.at[:, pl.ds(pl.multiple_of(p * n, 128), n)]


class ScatterRider:
    def __init__(self, items):
        self.items = items
        self.operands = [it[0] for it in items]
        self.results = None
        self.out_shapes = [jax.ShapeDtypeStruct((4, arr.shape[0] // 4, arr.shape[1]) if axis == 0 else
                                                (4, arr.shape[0], arr.shape[1] // 4), arr.dtype) for arr, axis in items]
        self.scratch = _sem_scratch(len(items))

    def _copies(self, ins, outs, sems):
        send_sems, recv_sems, local_sems = sems
        x, y, c = _my_pos()
        q = 2 * x + y
        local, sends, lands = [], [], []
        for w, (arr, axis) in enumerate(self.items):
            n = arr.shape[axis] // 4
            local.append(pltpu.make_async_copy(_block2d(ins[w], axis, q, n), outs[w].at[q], local_sems.at[w]))
            for k, (px, py) in enumerate(_other_chips(x, y)):
                p = 2 * px + py
                sems_k = dict(send_sem=send_sems.at[3 * w + k], recv_sem=recv_sems.at[3 * w + k],
                              device_id=(px, py, c), device_id_type=MESH_ID)
                theirs = _block2d(ins[w], axis, p, n)
                sends.append(pltpu.make_async_remote_copy(src_ref=theirs, dst_ref=outs[w].at[q], **sems_k))
                lands.append(pltpu.make_async_remote_copy(src_ref=theirs, dst_ref=outs[w].at[p], **sems_k))
        return local, sends, lands

    def start(self, ins, outs, sems):
        local, sends, _ = self._copies(ins, outs, sems)
        for cp in local + sends:
            cp.start()

    def finish(self, ins, outs, sems):
        local, sends, lands = self._copies(ins, outs, sems)
        for cp in lands:
            cp.wait_recv()
        for cp in sends:
            cp.wait_send()
        for cp in local:
            cp.wait()


class GatherRider:
    def __init__(self, items):
        self.items = items
        self.operands = [it[0] for it in items]
        self.results = None
        self.out_shapes = []
        for arr, _, axis in items:
            r, c = arr.shape[1:]
            assert r % 32 == 0
            self.out_shapes.append(jax.ShapeDtypeStruct((4 * r, c) if axis == 0 else (r, 4 * c), arr.dtype))
        n = len(items)
        dma = pltpu.SemaphoreType.DMA
        self.scratch = [dma((3 * n,)), dma((3 * n,)), dma((n,)), dma((3 * n,)), dma((3 * n,))]

    def _copies(self, ins, outs, sems):
        send_sems, recv_sems, local_sems, pass_send_sems, pass_recv_sems = sems
        x, y, c = _my_pos()
        q = 2 * x + y
        local, sends, lands, passes, pass_lands = [], [], [], [], []
        for w, (arr, layer, axis) in enumerate(self.items):
            r, cols = arr.shape[1:]
            half = r // 2
            src = ins[w].at[layer]

            def part(blk, hc, w=w, axis=axis, r=r, cols=cols, half=half):
                if axis == 0:
                    return outs[w].at[pl.ds(pl.multiple_of(blk * r + hc * half, 16), half), :]
                return outs[w].at[pl.ds(pl.multiple_of(hc * half, 16), half), pl.ds(pl.multiple_of(blk * cols, 128), cols)]

            local.append(pltpu.make_async_copy(src, _block2d(outs[w], axis, q, arr.shape[1 + axis]), local_sems.at[w]))
            for k, (px, py) in enumerate(_other_chips(x, y)):
                p = 2 * px + py
                ici = dict(send_sem=send_sems.at[3 * w + k], recv_sem=recv_sems.at[3 * w + k],
                           device_id=(px, py, c), device_id_type=MESH_ID)
                d2d = dict(send_sem=pass_send_sems.at[3 * w + k], recv_sem=pass_recv_sems.at[3 * w + k],
                           device_id=(x, y, 1 - c), device_id_type=MESH_ID)
                mine = src.at[pl.ds(pl.multiple_of(c * half, 16), half), :]
                sends.append(pltpu.make_async_remote_copy(src_ref=mine, dst_ref=part(q, c), **ici))
                lands.append(pltpu.make_async_remote_copy(src_ref=mine, dst_ref=part(p, c), **ici))
                passes.append(pltpu.make_async_remote_copy(src_ref=part(p, c), dst_ref=part(p, c), **d2d))
                pass_lands.append(pltpu.make_async_remote_copy(src_ref=part(p, c), dst_ref=part(p, 1 - c), **d2d))
        return local, sends, lands, passes, pass_lands

    def start(self, ins, outs, sems):
        local, sends, _, _, _ = self._copies(ins, outs, sems)
        for cp in local + sends:
            cp.start()

    def middle(self, ins, outs, sems):
        _, _, lands, passes, _ = self._copies(ins, outs, sems)
        for land, cp in zip(lands, passes):
            land.wait_recv()
            cp.start()

    def finish(self, ins, outs, sems):
        local, sends, _, passes, pass_lands = self._copies(ins, outs, sems)
        for cp in pass_lands:
            cp.wait_recv()
        for cp in sends + passes:
            cp.wait_send()
        for cp in local:
            cp.wait()


class SwapRider:
    def __init__(self, arrs):
        self.operands = list(arrs)
        self.out_shapes = [jax.ShapeDtypeStruct(a.shape, a.dtype) for a in arrs]
        self.scratch = [pltpu.SemaphoreType.DMA((len(arrs),)), pltpu.SemaphoreType.DMA((len(arrs),))]
        self.results = None

    def _copies(self, ins, outs, sems):
        x, y, c = _my_pos()
        return [pltpu.make_async_remote_copy(src_ref=ins[w], dst_ref=outs[w], send_sem=sems[0].at[w],
                                             recv_sem=sems[1].at[w], device_id=(x, y, 1 - c), device_id_type=MESH_ID)
                for w in range(len(self.operands))]

    def start(self, ins, outs, sems):
        for cp in self._copies(ins, outs, sems):
            cp.start()

    def finish(self, ins, outs, sems):
        for cp in self._copies(ins, outs, sems):
            cp.wait()


class RiderGroup:
    def __init__(self, riders):
        self.riders = riders
        self.operands = [a for r in riders for a in r.operands]
        self.out_shapes = [s for r in riders for s in r.out_shapes]
        self.scratch = [s for r in riders for s in r.scratch]

    def _split(self, ins, outs, sems):
        for r in self.riders:
            ni, no, ns = len(r.operands), len(r.out_shapes), len(r.scratch)
            yield r, ins[:ni], outs[:no], sems[:ns]
            ins, outs, sems = ins[ni:], outs[no:], sems[ns:]

    def start(self, ins, outs, sems):
        for r, i, o, s in self._split(ins, outs, sems):
            r.start(i, o, s)

    def middle(self, ins, outs, sems):
        for r, i, o, s in self._split(ins, outs, sems):
            if hasattr(r, 'middle'):
                r.middle(i, o, s)

    def finish(self, ins, outs, sems):
        for r, i, o, s in self._split(ins, outs, sems):
            r.finish(i, o, s)

    @property
    def results(self):
        return None

    @results.setter
    def results(self, res):
        for r in self.riders:
            no = len(r.out_shapes)
            r.results, res = list(res[:no]), res[no:]


def run_rider(name, rider):
    n_in, n_out = len(rider.operands), len(rider.out_shapes)

    def body(*refs):
        ins, outs, sems = refs[:n_in], refs[n_in:n_in + n_out], refs[n_in + n_out:]
        rider.start(ins, outs, sems)
        if hasattr(rider, 'middle'):
            rider.middle(ins, outs, sems)
        rider.finish(ins, outs, sems)

    rider.results = list(pl.pallas_call(body, out_shape=rider.out_shapes, in_specs=[ANY] * n_in, out_specs=[ANY] * n_out,
                                        name=name, scratch_shapes=rider.scratch)(*rider.operands))


WEIGHT_GROUPS = {'mla': [('mla_w_down', 0), ('mla_w_uq', 0), ('mla_w_ukv', 0), ('mla_w_o', 0)],
                 'hgrn': [('hgrn_w_in', 0), ('hgrn_w_o', 0)], 's5': [('s5_w_glu', 0)],
                 'ret': [('ret_w_in', 0), ('ret_w_o', 0)]}
WEIGHT_GROUPS.update({f'ffn{i}': [('ffn_w_up', i), ('ffn_w_down', i)] for i in range(4)})


class Exchange:
    def __init__(self, shards=None, full=None):
        self.shards, self.full = shards, dict(full or {})
        self.got, self.recv, self.sib, self.grads = {}, {}, {}, {}

    def gather(self, groups, name=None):
        if self.shards is None:
            return None
        keys = [k for g in groups for k in WEIGHT_GROUPS[g]]
        rider = GatherRider([(self.shards[n], layer, SHARD_AXIS[n] - 1) for n, layer in keys])
        self.got.update({k: (rider, j) for j, k in enumerate(keys)})
        if name is not None:
            run_rider(name, rider)
        return rider

    def weight(self, n, layer=0):
        if self.shards is None:
            return self.full[n][layer]
        rider, j = self.got[(n, layer)]
        return rider.results[j]

    def scatter(self, grads, name=None):
        if self.shards is None:
            self.grads.update(grads)
            return None
        keys = list(grads)
        rider = ScatterRider([(grads[k], SHARD_AXIS[k[0]] - 1) for k in keys])
        self.recv.update({k: (rider, j) for j, k in enumerate(keys)})
        if name is not None:
            run_rider(name, rider)
        return rider

    def received(self, n, layer):
        rider, j = self.recv[(n, layer)]
        return rider.results[j]

    def swap(self, name=None):
        if self.shards is None:
            return None
        keys = [k for k, (r, _) in self.recv.items() if k not in self.sib and r.results is not None]
        rider = SwapRider([self.received(*k) for k in keys])
        self.sib.update({k: (rider, j) for j, k in enumerate(keys)})
        if name is not None:
            run_rider(name, rider)
        return rider

    def sibling(self, n, layer):
        rider, j = self.sib[(n, layer)]
        return rider.results[j]


ADAM_BLOCK_ELEMS = 256 * 1024


def adamw_shard(name, mine, sib, w, m, v):
    nl, rows, cols = w.shape
    tr = [t for t in (512, 384, 352, 256, 176, 128, 64, 32, 16) if rows % t == 0 and t * cols <= ADAM_BLOCK_ELEMS][0]

    def body(a_ref, b_ref, w_ref, m_ref, v_ref, g_out, d_out, m_out, v_out):
        def total(r):
            acc = r[0].astype(F32)
            for k in range(1, 4):
                acc = acc + r[k].astype(F32)
            return acc
        g = total(a_ref) + total(b_ref)
        m_new = ADAM_B1 * m_ref[...] + (1.0 - ADAM_B1) * g
        v_new = ADAM_B2 * v_ref[...] + (1.0 - ADAM_B2) * jnp.square(g)
        m_hat = m_new / (1.0 - ADAM_B1 ** ADAM_STEP)
        v_hat = v_new / (1.0 - ADAM_B2 ** ADAM_STEP)
        g_out[...] = g
        d_out[...] = -ADAM_LR * (m_hat / (jnp.sqrt(v_hat) + ADAM_EPS) + ADAM_WD * w_ref[...])
        m_out[...] = m_new
        v_out[...] = v_new

    slots = pl.BlockSpec((4, None, tr, cols), lambda l, i: (0, l, i, 0))
    spec = pl.BlockSpec((None, tr, cols), lambda l, i: (l, i, 0))
    shape = jax.ShapeDtypeStruct(w.shape, F32)
    return pl.pallas_call(body, grid=(nl, rows // tr), in_specs=[slots, slots, spec, spec, spec], out_specs=[spec] * 4,
                          out_shape=[shape] * 4, name=name, compiler_params=_cparams())(mine, sib, w, m, v)


def kernel(x, meta_tokens, norm_mix_g, norm_ffn_g, mla_w_down, mla_cq_norm_g, mla_ckv_norm_g, mla_w_uq, mla_w_ukv, mla_q_head_g, mla_k_head_g, mla_w_o, hgrn_w_in, hgrn_lb_logits, hgrn_o_norm_g, hgrn_w_o, s5_lam_re, s5_lam_im, s5_log_dt, s5_b_re, s5_b_im, s5_c_re, s5_c_im, s5_d, s5_w_glu, ret_w_in, ret_gn_g, ret_w_o, ffn_w_up, ffn_conv_w, ffn_conv_b, ffn_w_down, loss_target, m_meta_tokens, m_norm_mix_g, m_norm_ffn_g, m_mla_w_down, m_mla_cq_norm_g, m_mla_ckv_norm_g, m_mla_w_uq, m_mla_w_ukv, m_mla_q_head_g, m_mla_k_head_g, m_mla_w_o, m_hgrn_w_in, m_hgrn_lb_logits, m_hgrn_o_norm_g, m_hgrn_w_o, m_s5_lam_re, m_s5_lam_im, m_s5_log_dt, m_s5_b_re, m_s5_b_im, m_s5_c_re, m_s5_c_im, m_s5_d, m_s5_w_glu, m_ret_w_in, m_ret_gn_g, m_ret_w_o, m_ffn_w_up, m_ffn_conv_w, m_ffn_conv_b, m_ffn_w_down, v_meta_tokens, v_norm_mix_g, v_norm_ffn_g, v_mla_w_down, v_mla_cq_norm_g, v_mla_ckv_norm_g, v_mla_w_uq, v_mla_w_ukv, v_mla_q_head_g, v_mla_k_head_g, v_mla_w_o, v_hgrn_w_in, v_hgrn_lb_logits, v_hgrn_o_norm_g, v_hgrn_w_o, v_s5_lam_re, v_s5_lam_im, v_s5_log_dt, v_s5_b_re, v_s5_b_im, v_s5_c_re, v_s5_c_im, v_s5_d, v_s5_w_glu, v_ret_w_in, v_ret_gn_g, v_ret_w_o, v_ffn_w_up, v_ffn_conv_w, v_ffn_conv_b, v_ffn_w_down):
    vals = (x, meta_tokens, norm_mix_g, norm_ffn_g, mla_w_down, mla_cq_norm_g, mla_ckv_norm_g, mla_w_uq, mla_w_ukv, mla_q_head_g, mla_k_head_g, mla_w_o, hgrn_w_in, hgrn_lb_logits, hgrn_o_norm_g, hgrn_w_o, s5_lam_re, s5_lam_im, s5_log_dt, s5_b_re, s5_b_im, s5_c_re, s5_c_im, s5_d, s5_w_glu, ret_w_in, ret_gn_g, ret_w_o, ffn_w_up, ffn_conv_w, ffn_conv_b, ffn_w_down, loss_target, m_meta_tokens, m_norm_mix_g, m_norm_ffn_g, m_mla_w_down, m_mla_cq_norm_g, m_mla_ckv_norm_g, m_mla_w_uq, m_mla_w_ukv, m_mla_q_head_g, m_mla_k_head_g, m_mla_w_o, m_hgrn_w_in, m_hgrn_lb_logits, m_hgrn_o_norm_g, m_hgrn_w_o, m_s5_lam_re, m_s5_lam_im, m_s5_log_dt, m_s5_b_re, m_s5_b_im, m_s5_c_re, m_s5_c_im, m_s5_d, m_s5_w_glu, m_ret_w_in, m_ret_gn_g, m_ret_w_o, m_ffn_w_up, m_ffn_conv_w, m_ffn_conv_b, m_ffn_w_down, v_meta_tokens, v_norm_mix_g, v_norm_ffn_g, v_mla_w_down, v_mla_cq_norm_g, v_mla_ckv_norm_g, v_mla_w_uq, v_mla_w_ukv, v_mla_q_head_g, v_mla_k_head_g, v_mla_w_o, v_hgrn_w_in, v_hgrn_lb_logits, v_hgrn_o_norm_g, v_hgrn_w_o, v_s5_lam_re, v_s5_lam_im, v_s5_log_dt, v_s5_b_re, v_s5_b_im, v_s5_c_re, v_s5_c_im, v_s5_d, v_s5_w_glu, v_ret_w_in, v_ret_gn_g, v_ret_w_o, v_ffn_w_up, v_ffn_conv_w, v_ffn_conv_b, v_ffn_w_down)
    names = ['x'] + WEIGHTS + ['loss_target'] + ['m_' + n for n in WEIGHTS] + ['v_' + n for n in WEIGHTS]
    A = dict(zip(names, vals))
    q = 2 * lax.axis_index("x") + lax.axis_index("y")

    small_shapes = [A[n].shape for n in SMALL_SHARDED]
    got_small = gather_chips("gather_small", _pack([A[n] for n in SMALL_SHARDED], F32, 8))
    W = {n: A[n] for n in REPLICATED}
    parts_small = [_unpack(got_small[p], small_shapes) for p in range(4)]
    for k, n in enumerate(SMALL_SHARDED):
        W[n] = jnp.concatenate([parts_small[p][k] for p in range(4)], axis=SHARD_AXIS[n])

    ex = Exchange(shards={n: A[n].astype(BF16) for n in BIG})
    loss, grad_x, G = local_step(A['x'][0], A['loss_target'][0], W, ex)
    loss = lax.psum(loss, ("x", "y", "c"))

    ex.swap(name="grad_big_sibling")
    res_big = []
    for n in BIG:
        layers = range(A[n].shape[0])
        res_big.append(adamw_shard("adam_" + n, jnp.stack([ex.received(n, layer) for layer in layers], axis=1),
                                   jnp.stack([ex.sibling(n, layer) for layer in layers], axis=1),
                                   A[n], A['m_' + n], A['v_' + n]))

    small_names = REPLICATED + SMALL_SHARDED
    full_shapes = [G[n].shape for n in small_names]
    total = sum_slots("grad_small_sum", gather_all("grad_small_gather", _pack([G[n] for n in small_names], F32, 8)))
    gs = dict(zip(small_names, _unpack(total, full_shapes)))
    for n in SMALL_SHARDED:
        ax = SHARD_AXIS[n]
        size = gs[n].shape[ax] // 4
        gs[n] = lax.dynamic_slice_in_dim(gs[n], q * size, size, axis=ax)
    pk = lambda pre: _pack([A[pre + n] for n in small_names], F32, 8)
    own_shapes = [A[n].shape for n in small_names]
    res_small = [_unpack(r, own_shapes) for r in
                 adamw("adam_small", [_pack([gs[n] for n in small_names], F32, 8)], pk(''), pk('m_'), pk('v_'))]

    out = {}
    for j, kind in enumerate(('grad_', 'delta_', 'new_m_', 'new_v_')):
        for k, n in enumerate(BIG):
            out[kind + n] = res_big[k][j]
        for k, n in enumerate(small_names):
            out[kind + n] = res_small[j][k]
    return (loss, grad_x[None]) + tuple(out[kind + n] for kind in ('grad_', 'delta_', 'new_m_', 'new_v_')
                                        for n in WEIGHTS)
```

```python
import functools
import math

import jax
import jax.numpy as jnp
from jax import lax
from jax.experimental import pallas as pl
from jax.experimental.pallas import tpu as pltpu

F32, BF16 = jnp.float32, jnp.bfloat16
HIGHEST = lax.Precision.HIGHEST
MESH_ID = pl.DeviceIdType.MESH

D = 1024
N_META = 16
PAD = 112
LEAD = PAD + N_META
EPS = 1e-6
NEG_INF = -1e30
CHUNK = 64
VMEM_LIMIT_V7X = 56 * 1024 * 1024
MM_VMEM_BUDGET = 36 * 1024 * 1024

MLA_H, MLA_NOPE, MLA_ROPE, MLA_V = 8, 128, 64, 128
MLA_QK = MLA_NOPE + MLA_ROPE
MLA_QL, MLA_KVL = 384, 256
HG_H, HG_D, HG_C = 8, 128, 16
S5_G, S5_P, S5_K = 64, 64, 16
RET_H, RET_DK, RET_DV = 4, 256, 512
FFN_F = 2816

ADAM_LR, ADAM_B1, ADAM_B2, ADAM_EPS, ADAM_WD, ADAM_STEP = 0.001, 0.9, 0.999, 1e-08, 0.01, 10

WEIGHTS = ['meta_tokens', 'norm_mix_g', 'norm_ffn_g', 'mla_w_down', 'mla_cq_norm_g', 'mla_ckv_norm_g', 'mla_w_uq',
           'mla_w_ukv', 'mla_q_head_g', 'mla_k_head_g', 'mla_w_o', 'hgrn_w_in', 'hgrn_lb_logits', 'hgrn_o_norm_g',
           'hgrn_w_o', 's5_lam_re', 's5_lam_im', 's5_log_dt', 's5_b_re', 's5_b_im', 's5_c_re', 's5_c_im', 's5_d',
           's5_w_glu', 'ret_w_in', 'ret_gn_g', 'ret_w_o', 'ffn_w_up', 'ffn_conv_w', 'ffn_conv_b', 'ffn_w_down']
SHARD_AXIS = {'meta_tokens': 1, 'mla_w_down': 1, 'mla_w_uq': 2, 'mla_w_ukv': 2, 'mla_w_o': 1, 'hgrn_w_in': 2,
              'hgrn_w_o': 1, 's5_d': 1, 's5_w_glu': 2, 'ret_w_in': 2, 'ret_gn_g': 1, 'ret_w_o': 1, 'ffn_w_up': 2,
              'ffn_conv_w': 2, 'ffn_w_down': 1}
BIG = ['mla_w_down', 'mla_w_uq', 'mla_w_ukv', 'mla_w_o', 'hgrn_w_in', 'hgrn_w_o', 's5_w_glu', 'ret_w_in', 'ret_w_o',
       'ffn_w_up', 'ffn_w_down']
SMALL_SHARDED = ['meta_tokens', 's5_d', 'ret_gn_g', 'ffn_conv_w']
REPLICATED = [n for n in WEIGHTS if n not in SHARD_AXIS]
SMALL_LATE = ['meta_tokens', 'norm_mix_g', 'mla_cq_norm_g', 'mla_ckv_norm_g', 'mla_q_head_g', 'mla_k_head_g']
SMALL_EARLY = [n for n in REPLICATED + SMALL_SHARDED if n not in SMALL_LATE]


def _cparams():
    return pltpu.CompilerParams(vmem_limit_bytes=VMEM_LIMIT_V7X)


def _dg(a, b, ca, cb):
    return lax.dot_general(a.astype(BF16), b.astype(BF16), (((ca,), (cb,)), ((), ())),
                           preferred_element_type=F32)


@jax.custom_vjp
def mm_nn(a, b):
    return _dg(a, b, 1, 0)


@jax.custom_vjp
def mm_nt(a, b):
    return _dg(a, b, 1, 1)


@jax.custom_vjp
def mm_tn(a, b):
    return _dg(a, b, 0, 0)


mm_nn.defvjp(lambda a, b: (mm_nn(a, b), (a, b)),
             lambda r, g: (mm_nt(g, r[1]).astype(r[0].dtype), mm_tn(r[0], g).astype(r[1].dtype)))
mm_nt.defvjp(lambda a, b: (mm_nt(a, b), (a, b)),
             lambda r, g: (mm_nn(g, r[1]).astype(r[0].dtype), mm_tn(g, r[0]).astype(r[1].dtype)))
mm_tn.defvjp(lambda a, b: (mm_tn(a, b), (a, b)),
             lambda r, g: (mm_nt(r[1], g).astype(r[0].dtype), mm_nn(r[0], g).astype(r[1].dtype)))


def _dot_f32(a, b):
    return jnp.dot(a, b, precision=HIGHEST, preferred_element_type=F32)


def _shift_rows(x, s, up):
    n = x.shape[0]
    r = lax.broadcasted_iota(jnp.int32, x.shape, 0)
    if up:
        return jnp.where(r < n - s, pltpu.roll(x, n - s, 0), 0.0)
    return jnp.where(r >= s, pltpu.roll(x, s, 0), 0.0)


@functools.partial(jax.custom_vjp, nondiff_argnums=(1,))
def shift_down(x, s):
    return _shift_rows(x, s, False)


shift_down.defvjp(lambda x, s: (_shift_rows(x, s, False), None), lambda s, _, g: (_shift_rows(g, s, True),))


@functools.partial(jax.custom_vjp, nondiff_argnums=(1,))
def shift_up(x, s):
    return _shift_rows(x, s, True)


shift_up.defvjp(lambda x, s: (_shift_rows(x, s, True), None), lambda s, _, g: (_shift_rows(g, s, False),))


def _swap32_impl(x):
    ax = x.ndim - 1
    lane = lax.broadcasted_iota(jnp.int32, x.shape, ax)
    return jnp.where(lane < 32, pltpu.roll(x, 96, ax), jnp.where(lane < 64, pltpu.roll(x, 32, ax), 0.0))


@jax.custom_vjp
def swap32(x):
    return _swap32_impl(x)


swap32.defvjp(lambda x: (_swap32_impl(x), None), lambda _, g: (_swap32_impl(g),))


def _rms(x, g):
    return x * lax.rsqrt(jnp.mean(x * x, axis=-1, keepdims=True) + EPS) * g


def _silu(x):
    return x * jax.nn.sigmoid(x)


def _row_ids(pid, n, shape, axis=0):
    return pid * n + lax.broadcasted_iota(jnp.int32, shape, axis)


class Arg:
    def __init__(self, arr, block, imap, diff=True, gdtype=F32):
        self.arr, self.block, self.imap, self.diff, self.gdtype = arr, block, imap, diff, gdtype


class Out:
    def __init__(self, shape, dtype, block, imap):
        self.shape, self.dtype, self.block, self.imap = shape, dtype, block, imap


def _free_axes(imap, grid):
    ng = len(grid)
    base = tuple(imap(*([0] * ng)))
    free = []
    for ax in range(ng):
        p = [0] * ng
        p[ax] = 1
        if grid[ax] > 1 and tuple(imap(*p)) == base:
            free.append(ax)
    assert free == list(range(ng - len(free), ng)), "revisited blocks must be revisited on the innermost axes"
    return free


def _pallas(name, body, grid, in_specs, out_specs, out_shape, scratch, operands, rider=None):
    if isinstance(rider, (list, tuple)):
        riders = [r for r in rider if r is not None]
        rider = RiderGroup(riders) if riders else None
    if rider is None:
        return pl.pallas_call(body, grid=grid, in_specs=in_specs, out_specs=out_specs, out_shape=out_shape,
                              scratch_shapes=scratch, name=name, compiler_params=_cparams())(*operands)
    n_in, n_out, n_sc = len(in_specs), len(out_specs), len(scratch)
    r_in, r_out = len(rider.operands), len(rider.out_shapes)

    def body_with_rider(*refs):
        ins, refs = refs[:n_in], refs[n_in:]
        r_ins, refs = refs[:r_in], refs[r_in:]
        outs, refs = refs[:n_out], refs[n_out:]
        r_outs, refs = refs[:r_out], refs[r_out:]
        sc, r_sc = refs[:n_sc], refs[n_sc:]
        pids = [pl.program_id(a) for a in range(len(grid))]
        first = functools.reduce(jnp.logical_and, [p == 0 for p in pids])
        last = functools.reduce(jnp.logical_and, [p == g - 1 for p, g in zip(pids, grid)])

        @pl.when(first)
        def _():
            rider.start(r_ins, r_outs, r_sc)

        if hasattr(rider, 'middle'):
            step = functools.reduce(lambda acc, pg: acc * pg[1] + pg[0], zip(pids, grid), 0)

            @pl.when(step == (math.prod(grid) * 3) // 5)
            def _():
                rider.middle(r_ins, r_outs, r_sc)

        body(*ins, *outs, *sc)

        @pl.when(last)
        def _():
            rider.finish(r_ins, r_outs, r_sc)

    res = pl.pallas_call(body_with_rider, grid=grid, in_specs=list(in_specs) + [ANY] * r_in,
                         out_specs=list(out_specs) + [ANY] * r_out, out_shape=list(out_shape) + rider.out_shapes,
                         scratch_shapes=list(scratch) + rider.scratch, name=name,
                         compiler_params=_cparams())(*operands, *rider.operands)
    rider.results = list(res[n_out:])
    return res[:n_out]


def stage_fwd(name, fn, grid, args, outs, state_shape=None, rider=None):
    n_in, n_out, ng = len(args), len(outs), len(grid)

    def body(*refs):
        pids = tuple(pl.program_id(a) for a in range(ng))
        vals = [r[...] for r in refs[:n_in]]
        o_refs = refs[n_in:n_in + n_out]
        if state_shape is None:
            res = fn(pids, *vals)
        else:
            sv_ref, st_ref = refs[n_in + n_out], refs[n_in + n_out + 1]

            @pl.when(pids[-1] == 0)
            def _():
                st_ref[...] = jnp.zeros(state_shape, F32)

            s = st_ref[...]
            sv_ref[...] = s
            res = fn(pids, *vals, s)
            st_ref[...] = res[-1]
            res = res[:-1]
        for r, v in zip(o_refs, res):
            r[...] = v.astype(r.dtype)

    in_specs = [pl.BlockSpec(a.block, a.imap) for a in args]
    out_specs = [pl.BlockSpec(o.block, o.imap) for o in outs]
    out_shape = [jax.ShapeDtypeStruct(o.shape, o.dtype) for o in outs]
    scratch = []
    if state_shape is not None:
        nz = len(state_shape)
        out_specs.append(pl.BlockSpec((None, None) + tuple(state_shape), lambda i, j: (i, j) + (0,) * nz))
        out_shape.append(jax.ShapeDtypeStruct(tuple(grid) + tuple(state_shape), F32))
        scratch = [pltpu.VMEM(state_shape, F32)]
    return _pallas(name, body, grid, in_specs, out_specs, out_shape, scratch, [a.arr for a in args], rider)


def stage_bwd(name, fn, grid, args, outs, cots, state_shape=None, states=None, rider=None):
    n_in, n_out, ng = len(args), len(outs), len(grid)
    nb = grid[-1]
    rev = state_shape is not None
    didx = [k for k, a in enumerate(args) if a.diff]
    frees = [_free_axes(args[k].imap, grid) for k in didx]

    def eff(p):
        return tuple(p[:-1]) + (nb - 1 - p[-1],) if rev else tuple(p)

    def wrap(imap):
        return lambda *p: imap(*eff(p))

    def body(*refs):
        pids = tuple(pl.program_id(a) for a in range(ng))
        e = eff(pids)
        vals = [r[...] for r in refs[:n_in]]
        cts = tuple(r[...].astype(F32) for r in refs[n_in:n_in + n_out])
        pos = n_in + n_out
        if rev:
            st_in_ref = refs[pos]
            pos += 1
        g_refs = refs[pos:pos + len(didx)]
        pos += len(didx)
        dvals = [vals[k].astype(F32) for k in didx]

        def f(*dv):
            full = list(vals)
            for k, v in zip(didx, dv[:len(didx)]):
                full[k] = v
            return tuple(fn(e, *full, *dv[len(didx):]))

        if rev:
            ds_ref = refs[pos]

            @pl.when(pids[-1] == 0)
            def _():
                ds_ref[...] = jnp.zeros(state_shape, F32)

            _, vjp = jax.vjp(f, *dvals, st_in_ref[...])
            grads = vjp(cts + (ds_ref[...],))
            ds_ref[...] = grads[-1]
            grads = grads[:-1]
        else:
            _, vjp = jax.vjp(f, *dvals)
            grads = vjp(cts)
        for gref, g, free in zip(g_refs, grads, frees):
            g = g.astype(F32)
            if not free:
                gref[...] = g.astype(gref.dtype)
            else:
                first = functools.reduce(jnp.logical_and, [pids[ax] == 0 for ax in free])

                @pl.when(first)
                def _():
                    gref[...] = g

                @pl.when(jnp.logical_not(first))
                def _():
                    gref[...] += g

    in_specs = [pl.BlockSpec(a.block, wrap(a.imap)) for a in args]
    in_specs += [pl.BlockSpec(o.block, wrap(o.imap)) for o in outs]
    operands = [a.arr for a in args] + list(cots)
    scratch = []
    if rev:
        nz = len(state_shape)
        in_specs.append(pl.BlockSpec((None, None) + tuple(state_shape), lambda i, j: (i, nb - 1 - j) + (0,) * nz))
        operands.append(states)
        scratch = [pltpu.VMEM(state_shape, F32)]
    out_specs = [pl.BlockSpec(args[k].block, wrap(args[k].imap)) for k in didx]
    assert all(args[k].gdtype == F32 or not free for k, free in zip(didx, frees))
    out_shape = [jax.ShapeDtypeStruct(args[k].arr.shape, args[k].gdtype) for k in didx]
    return _pallas(name, body, grid, in_specs, out_specs, out_shape, scratch, operands, rider)


def _divisors(n, cands):
    return [c for c in cands if n % c == 0] or [n]


def _nbytes(dt):
    return jnp.dtype(dt).itemsize


def matmul(name, a, b, mode, out_dtype=F32, res=None, mask=False, res_mask=True, window=None, into=None):
    sa, sb, so = _nbytes(a.dtype), _nbytes(b.dtype), _nbytes(out_dtype)
    off, width = (window[0], window[1]) if window is not None else (0, None)
    if mode in ('nn', 'nt'):
        M, K = a.shape
        N = (width or b.shape[1]) if mode == 'nn' else b.shape[0]
        assert mode == 'nn' or width is None or width == K
        best = None
        for tm in _divisors(M, (1408, 1056, 768, 384, 128)):
            for tn in _divisors(N, (1408, 1024, 768, 512, 384, 256, 128)):
                est = 2 * (tm * K * sa + tn * K * sb + tm * tn * (so + (4 if res is not None else 0)))
                if est <= MM_VMEM_BUDGET and (best is None or tm * tn > best[0] * best[1]):
                    best = (tm, tn)
        tm, tn = best
        grid = (M // tm, N // tn)

        def body(*refs):
            a_ref, b_ref = refs[0], refs[1]
            o_ref = refs[-1]
            x = a_ref[...]
            rows = _row_ids(pl.program_id(0), tm, (tm, 1))
            if mask:
                x = jnp.where(rows >= PAD, x, jnp.zeros_like(x))
            acc = _dg(x, b_ref[...], 1, 0 if mode == 'nn' else 1)
            if res is not None:
                acc = refs[2][...] + (jnp.where(rows >= PAD, acc, 0.0) if res_mask else acc)
            o_ref[...] = acc.astype(o_ref.dtype)

        assert off % (tn if mode == 'nn' else K) == 0
        cb, kb = off // tn, off // K
        in_specs = [pl.BlockSpec((tm, K), lambda i, j: (i, 0)),
                    pl.BlockSpec((K, tn), lambda i, j: (0, j + cb)) if mode == 'nn' else
                    pl.BlockSpec((tn, K), lambda i, j: (j, kb))]
        ops = [a, b]
        if res is not None:
            in_specs.append(pl.BlockSpec((tm, tn), lambda i, j: (i, j)))
            ops.append(res)
        return pl.pallas_call(body, grid=grid, in_specs=in_specs,
                              out_specs=pl.BlockSpec((tm, tn), lambda i, j: (i, j)),
                              out_shape=jax.ShapeDtypeStruct((M, N), out_dtype), name=name,
                              compiler_params=_cparams())(*ops)
    assert mode == 'tn' and res is None
    M, K = a.shape
    N = b.shape[1]
    best = None
    for tk in _divisors(K, (1408, 1024, 768, 512, 384, 256, 128)):
        for tn in _divisors(N, (1408, 1024, 768, 512, 384, 256, 128)):
            est = 2 * (M * tk * sa + M * tn * sb + tk * tn * so)
            if est <= MM_VMEM_BUDGET and (best is None or tk * tn > best[0] * best[1]):
                best = (tk, tn)
    tk, tn = best

    def body_t(*refs):
        a_ref, b_ref, o_ref = refs[0], refs[1], refs[-1]
        y = b_ref[...]
        if mask:
            rows = lax.broadcasted_iota(jnp.int32, (M, 1), 0)
            y = jnp.where(rows >= PAD, y, jnp.zeros_like(y))
        o_ref[...] = _dg(a_ref[...], y, 0, 0).astype(o_ref.dtype)

    assert off % tn == 0
    cb = off // tn
    total = window[2] if window is not None else N
    in_specs = [pl.BlockSpec((M, tk), lambda i, j: (0, i)), pl.BlockSpec((M, tn), lambda i, j: (0, j))]
    ops, alias = [a, b], {}
    if into is not None:
        in_specs.append(ANY)
        ops.append(into)
        alias = {2: 0}
    return pl.pallas_call(body_t, grid=(K // tk, N // tn), in_specs=in_specs,
                          out_specs=pl.BlockSpec((tk, tn), lambda i, j: (i, j + cb)),
                          out_shape=jax.ShapeDtypeStruct((K, total), out_dtype), name=name,
                          input_output_aliases=alias, compiler_params=_cparams())(*ops)


def linear_bwd(name, act, w, dy, mask=False):
    return (matmul(name + "_da", dy, w, 'nt', mask=mask),
            matmul(name + "_dw", act, dy, 'tn', out_dtype=BF16, mask=mask))


def _row_tile(T):
    return _divisors(T, (384, 128))[0]


def _rows(arr, tm, gdtype=F32):
    return Arg(arr, (tm, arr.shape[1]), lambda i: (i, 0), gdtype=gdtype)


def _const(arr, diff=True):
    return Arg(arr, arr.shape, lambda *p: (0,) * arr.ndim, diff)


def _norm_fn(pids, h, g):
    return (_rms(h, g),)


def _norm_bwd_fn(pids, h, g):
    return (_rms(h, g), h)


def norm_fwd(name, h, g, dtype):
    T = h.shape[0]
    tm = _row_tile(T)
    return stage_fwd(name, _norm_fn, (T // tm,), [_rows(h, tm), _const(g)],
                     [Out((T, D), dtype, (tm, D), lambda i: (i, 0))])[0]


def norm_bwd(name, h, g, da, dh):
    T = h.shape[0]
    tm = _row_tile(T)
    o = Out((T, D), F32, (tm, D), lambda i: (i, 0))
    return stage_bwd(name, _norm_bwd_fn, (T // tm,), [_rows(h, tm), _const(g)], [o, o], [da, dh])


def _causal_conv3(u, cw, cb):
    return cw[2:3] * u + cw[1:2] * shift_down(u, 1) + cw[0:1] * shift_down(u, 2) + cb


def _ffn_act_fn(pids, ug, uv, cwg, cwv, cbg, cbv):
    return (_silu(_causal_conv3(ug, cwg, cbg)) * _causal_conv3(uv, cwv, cbv),)


def _ffn_act_args(ug, uv, cw, cb):
    T = ug.shape[0]
    col = lambda j: (0, j)
    args = [Arg(ug, (T, 128), col, gdtype=BF16), Arg(uv, (T, 128), col, gdtype=BF16),
            Arg(cw[:, :FFN_F], (3, 128), col), Arg(cw[:, FFN_F:], (3, 128), col),
            Arg(cb[:, :FFN_F], (1, 128), col), Arg(cb[:, FFN_F:], (1, 128), col)]
    outs = [Out((T, FFN_F), BF16, (T, 128), col)]
    return (FFN_F // 128,), args, outs


def _interleave_cols(w, n_parts, tile=128):
    lead = w.shape[:-1]
    n = w.shape[-1] // (n_parts * tile)
    k = len(lead)
    return w.reshape(lead + (n_parts, n, tile)).transpose(tuple(range(k)) + (k + 1, k, k + 2)).reshape(w.shape)


def _deinterleave_cols(w, n_parts, tile=128):
    lead = w.shape[:-1]
    n = w.shape[-1] // (n_parts * tile)
    k = len(lead)
    return w.reshape(lead + (n, n_parts, tile)).transpose(tuple(range(k)) + (k + 1, k, k + 2)).reshape(w.shape)


def ffn_layer(i, h, g, w_up, cw, cb, w_down):
    gate_w, val_w = (0, FFN_F, 2 * FFN_F), (FFN_F, FFN_F, 2 * FFN_F)
    b = norm_fwd(f"ffn{i}_norm", h, g, BF16)
    ug = matmul(f"ffn{i}_up_g", b, w_up, 'nn', window=gate_w)
    uv = matmul(f"ffn{i}_up_v", b, w_up, 'nn', window=val_w)
    grid, args, outs = _ffn_act_args(ug, uv, cw, cb)
    p = stage_fwd(f"ffn{i}_act", _ffn_act_fn, grid, args, outs)[0]
    h_new = matmul(f"ffn{i}_down", p, w_down, 'nn', res=h)

    def bwd(dh):
        dp, dwd = linear_bwd(f"ffn{i}_down_b", p, w_down, dh, mask=True)
        dug, duv, dcwg, dcwv, dcbg, dcbv = stage_bwd(f"ffn{i}_act_b", _ffn_act_fn, grid, args, outs, [dp])
        db = matmul(f"ffn{i}_up_b_da_g", dug, w_up, 'nt', window=gate_w)
        db = matmul(f"ffn{i}_up_b_da_v", duv, w_up, 'nt', window=val_w, res=db, res_mask=False)
        dwu = matmul(f"ffn{i}_up_b_dw_g", b, dug, 'tn', out_dtype=BF16, window=gate_w)
        dwu = matmul(f"ffn{i}_up_b_dw_v", b, duv, 'tn', out_dtype=BF16, window=val_w, into=dwu)
        dh2, dg = norm_bwd(f"ffn{i}_norm_b", h, g, db, dh)
        return dh2, dict(g=dg, w_up=dwu, cw=jnp.concatenate([dcwg, dcwv], axis=1),
                         cb=jnp.concatenate([dcbg, dcbv], axis=1), w_down=dwd)

    return h_new, bwd


def _mla_latent_fn(pids, down, gcq, gckv):
    cq = _rms(down[:, :MLA_QL], gcq)
    ckv = _rms(down[:, MLA_QL:MLA_QL + MLA_KVL], gckv)
    return cq, ckv, down[:, MLA_QL + MLA_KVL:]


def _rope64(x, cos, sin_signed):
    return x * cos + swap32(x) * sin_signed


def _mla_heads_fn(pids, qraw, kv, kpe, gqn, gqr, gkn, gkr, cos, sin_signed):
    qn, qr = qraw[:, :128], qraw[:, 128:]
    rq = lax.rsqrt((jnp.sum(qn * qn, -1, keepdims=True) + jnp.sum(qr * qr, -1, keepdims=True)) / MLA_QK + EPS)
    q = jnp.concatenate([qn * rq * gqn, _rope64(qr * rq * gqr, cos, sin_signed)], axis=1)
    kn, v = kv[:, :128], kv[:, 128:]
    rk = lax.rsqrt((jnp.sum(kn * kn, -1, keepdims=True) + jnp.sum(kpe * kpe, -1, keepdims=True)) / MLA_QK + EPS)
    k = jnp.concatenate([kn * rk * gkn, _rope64(kpe * rk * gkr, cos, sin_signed)], axis=1)
    return q, k, v


def _chunk_id(r):
    return jnp.where(r < LEAD, 0, 1 + lax.shift_right_arithmetic(r - LEAD, 6))


ATT_T = 384
ATT_SCALE = MLA_QK ** -0.5


def _att_mask(s, qb, kb):
    qrow = _row_ids(qb, ATT_T, (ATT_T, 1))
    krow = _row_ids(kb, ATT_T, (1, ATT_T), axis=1)
    ok = jnp.logical_and(_chunk_id(krow) <= _chunk_id(qrow), krow >= PAD)
    return jnp.where(ok, s, NEG_INF)


def _att_scores(q, k_ref, qb, kb, masked):
    ks = k_ref[pl.ds(pl.multiple_of(kb * ATT_T, ATT_T), ATT_T), :]
    s = _dg(q, ks, 1, 1) * ATT_SCALE
    return (_att_mask(s, qb, kb) if masked else s), ks


def _att_key_loop(j, step, init):
    carry = step(0, init, True)
    carry = lax.fori_loop(1, j, lambda kb, c: step(kb, c, False), carry)
    return lax.cond(j > 0, lambda c: step(j, c, True), lambda c: c, carry)


def _att_rows(ref, b):
    return ref[pl.ds(pl.multiple_of(b * ATT_T, ATT_T), ATT_T), :]


def attention_fwd(q, k, v, rider=None):
    H, T, _ = q.shape
    nq = T // ATT_T

    def body(q_ref, k_ref, v_ref, o_ref, lse_ref):
        j = pl.program_id(1)
        qv = q_ref[...]

        def step(kb, carry, masked):
            m, l, acc = carry
            s, _ = _att_scores(qv, k_ref, j, kb, masked)
            m_new = jnp.maximum(m, jnp.max(s, axis=-1, keepdims=True))
            p = jnp.exp(s - m_new)
            alpha = jnp.exp(m - m_new)
            return (m_new, alpha * l + jnp.sum(p, axis=-1, keepdims=True),
                    alpha * acc + _dg(p, _att_rows(v_ref, kb), 1, 0))

        init = (jnp.full((ATT_T, 1), NEG_INF, F32), jnp.zeros((ATT_T, 1), F32), jnp.zeros((ATT_T, MLA_V), F32))
        m, l, acc = _att_key_loop(j, step, init)
        o_ref[...] = acc / l
        lse_ref[...] = m + jnp.log(l)

    return _pallas("mla_attn", body, (H, nq),
                   [pl.BlockSpec((None, ATT_T, 256), lambda hh, j: (hh, j, 0)),
                    pl.BlockSpec((None, T, 256), lambda hh, j: (hh, 0, 0)),
                    pl.BlockSpec((None, T, MLA_V), lambda hh, j: (hh, 0, 0))],
                   [pl.BlockSpec((ATT_T, MLA_V), lambda hh, j: (j, hh)),
                    pl.BlockSpec((None, ATT_T, 1), lambda hh, j: (hh, j, 0))],
                   [jax.ShapeDtypeStruct((T, H * MLA_V), F32), jax.ShapeDtypeStruct((H, T, 1), F32)], [],
                   [q, k, v], rider)


def attention_bwd(q, k, v, o, lse, do, rider=None, rider_dq=None):
    H, T, _ = q.shape
    nq = T // ATT_T

    def dq_body(q_ref, k_ref, v_ref, o_ref, do_ref, lse_ref, dq_ref, delta_ref):
        j = pl.program_id(1)
        qv, dov, lsev = q_ref[...], do_ref[...], lse_ref[...]
        delta = jnp.sum(dov * o_ref[...], axis=-1, keepdims=True)
        delta_ref[...] = delta

        def step(kb, acc, masked):
            s, ks = _att_scores(qv, k_ref, j, kb, masked)
            p = jnp.exp(s - lsev)
            ds = p * (_dg(dov, _att_rows(v_ref, kb), 1, 1) - delta) * ATT_SCALE
            return acc + _dg(ds, ks, 1, 0)

        dq_ref[...] = _att_key_loop(j, step, jnp.zeros((ATT_T, 256), F32))

    q_blk = pl.BlockSpec((None, ATT_T, 256), lambda hh, j: (hh, j, 0))
    k_all = pl.BlockSpec((None, T, 256), lambda hh, j: (hh, 0, 0))
    v_all = pl.BlockSpec((None, T, MLA_V), lambda hh, j: (hh, 0, 0))
    o_blk = pl.BlockSpec((ATT_T, MLA_V), lambda hh, j: (j, hh))
    col_blk = pl.BlockSpec((None, ATT_T, 1), lambda hh, j: (hh, j, 0))
    dq, delta = _pallas("mla_attn_dq", dq_body, (H, nq), [q_blk, k_all, v_all, o_blk, o_blk, col_blk],
                        [q_blk, col_blk],
                        [jax.ShapeDtypeStruct((H, T, 256), F32), jax.ShapeDtypeStruct((H, T, 1), F32)], [],
                        [q, k, v, o, do, lse], rider_dq)

    def dkv_body(q_ref, k_ref, v_ref, do_ref, lse_ref, delta_ref, dk_ref, dv_ref):
        kb = pl.program_id(1)
        kv = k_ref[...]
        vv = v_ref[...]

        def step(qb, carry, masked):
            dk, dv = carry
            qv, dov = _att_rows(q_ref, qb), _att_rows(do_ref, qb)
            s = _dg(qv, kv, 1, 1) * ATT_SCALE
            if masked:
                s = _att_mask(s, qb, kb)
            p = jnp.exp(s - _att_rows(lse_ref, qb))
            ds = p * (_dg(dov, vv, 1, 1) - _att_rows(delta_ref, qb)) * ATT_SCALE
            return dk + _dg(ds, qv, 0, 0), dv + _dg(p, dov, 0, 0)

        carry = step(kb, (jnp.zeros((ATT_T, 256), F32), jnp.zeros((ATT_T, MLA_V), F32)), True)
        dk, dv = lax.cond(kb == 0,
                          lambda c: lax.fori_loop(kb + 1, nq, lambda qb, cc: step(qb, cc, True), c),
                          lambda c: lax.fori_loop(kb + 1, nq, lambda qb, cc: step(qb, cc, False), c), carry)
        dk_ref[...] = dk
        dv_ref[...] = dv

    q_all = pl.BlockSpec((None, T, 256), lambda hh, j: (hh, 0, 0))
    v_blk = pl.BlockSpec((None, ATT_T, MLA_V), lambda hh, j: (hh, j, 0))
    do_all = pl.BlockSpec((T, MLA_V), lambda hh, j: (0, hh))
    col_all = pl.BlockSpec((None, T, 1), lambda hh, j: (hh, 0, 0))
    dk, dv = _pallas("mla_attn_dkv", dkv_body, (H, nq), [q_all, q_blk, v_blk, do_all, col_all, col_all],
                     [q_blk, v_blk],
                     [jax.ShapeDtypeStruct((H, T, 256), F32), jax.ShapeDtypeStruct((H, T, MLA_V), F32)], [],
                     [q, k, v, do, lse, delta], rider)
    return dq, dk, dv


def mla_mixer(h, g, w, tabs, rider=None):
    T = h.shape[0]
    tm = _row_tile(T)
    nt = T // tm
    a = norm_fwd("mla_norm", h, g, BF16)
    down = matmul("mla_down", a, w['w_down'], 'nn')
    lat_args = [_rows(down, tm, BF16), _const(w['gcq']), _const(w['gckv'])]
    lat_outs = [Out((T, MLA_QL), BF16, (tm, MLA_QL), lambda i: (i, 0)),
                Out((T, MLA_KVL), BF16, (tm, MLA_KVL), lambda i: (i, 0)),
                Out((T, 128), F32, (tm, 128), lambda i: (i, 0))]
    cq, ckv, kpe = stage_fwd("mla_latent", _mla_latent_fn, (nt,), lat_args, lat_outs)
    qraw = matmul("mla_uq", cq, w['w_uq'], 'nn')
    kv = matmul("mla_ukv", ckv, w['w_ukv'], 'nn')
    hd_args = [Arg(qraw, (tm, 256), lambda i, hh: (i, hh), gdtype=BF16),
               Arg(kv, (tm, 256), lambda i, hh: (i, hh), gdtype=BF16),
               Arg(kpe, (tm, 128), lambda i, hh: (i, 0)),
               _const(w['gqn']), _const(w['gqr']), _const(w['gkn']), _const(w['gkr']),
               Arg(tabs['cos_a'], (tm, 128), lambda i, hh: (i, 0), False),
               Arg(tabs['sin_a'], (tm, 128), lambda i, hh: (i, 0), False)]
    hd_outs = [Out((MLA_H, T, 256), BF16, (None, tm, 256), lambda i, hh: (hh, i, 0)),
               Out((MLA_H, T, 256), BF16, (None, tm, 256), lambda i, hh: (hh, i, 0)),
               Out((MLA_H, T, 128), BF16, (None, tm, 128), lambda i, hh: (hh, i, 0))]
    q, k, v = stage_fwd("mla_heads", _mla_heads_fn, (nt, MLA_H), hd_args, hd_outs)
    o, lse = attention_fwd(q, k, v, rider=rider)
    h_new = matmul("mla_o", o, w['w_o'], 'nn', res=h)

    def bwd(dh, rider=None, rider_dq=None):
        do, dwo = linear_bwd("mla_o_b", o, w['w_o'], dh, mask=True)
        dq, dk, dv = attention_bwd(q, k, v, o, lse, do, rider=rider, rider_dq=rider_dq)
        dqraw, dkv, dkpe, dgqn, dgqr, dgkn, dgkr = stage_bwd("mla_heads_b", _mla_heads_fn, (nt, MLA_H), hd_args,
                                                             hd_outs, [dq, dk, dv])
        dcq, dwuq = linear_bwd("mla_uq_b", cq, w['w_uq'], dqraw)
        dckv, dwukv = linear_bwd("mla_ukv_b", ckv, w['w_ukv'], dkv)
        ddown, dgcq, dgckv = stage_bwd("mla_latent_b", _mla_latent_fn, (nt,), lat_args, lat_outs, [dcq, dckv, dkpe])
        da, dwdown = linear_bwd("mla_down_b", a, w['w_down'], ddown)
        dh2, dg = norm_bwd("mla_norm_b", h, g, da, dh)
        return dh2, dict(g=dg, w_down=dwdown, gcq=dgcq, gckv=dgckv, w_uq=dwuq, w_ukv=dwukv, gqn=dgqn, gqr=dgqr,
                         gkn=dgkn, gkr=dgkr, w_o=dwo)

    return h_new, bwd


HG_R = 384


def _hgrn_fn(pids, z, lb, go, st):
    outs = []
    for lo in range(0, z.shape[0], 128):
        o, st = _hgrn_block(z[lo:lo + 128], lb, go, st)
        outs.append(o)
    return jnp.concatenate(outs, axis=0), st


def _hgrn_block(z, lb, go, st):
    R = z.shape[0]
    zq, zf, zi, zg = z[:, :128], z[:, 128:256], z[:, 256:384], z[:, 384:]
    assert R == 128
    q = _silu(zq)
    fg = lb + (1.0 - lb) * jax.nn.sigmoid(zf)
    logf = jnp.log(fg)
    k = 1.0 - fg
    row = lax.broadcasted_iota(jnp.int32, logf.shape, 0)
    pos = row & (HG_C - 1)
    cum, rev = logf, logf
    for d in (1, 2, 4, 8):
        cum = cum + jnp.where(pos >= d, shift_down(cum, d), 0.0)
        rev = rev + jnp.where(pos < HG_C - d, shift_up(rev, d), 0.0)
    cums, tots = [cum], [cum + rev - logf]
    for s in (16, 32, 64):
        odd = (row & s) != 0
        before = shift_down(tots[-1], s)
        cums.append(cums[-1] + jnp.where(odd, before, 0.0))
        tots.append(tots[-1] + jnp.where(odd, before, shift_up(tots[-1], s)))
    t = lax.broadcasted_iota(jnp.int32, (R, R), 0)
    j = lax.broadcasted_iota(jnp.int32, (R, R), 1)
    sh = lax.shift_right_arithmetic
    a = jnp.where(jnp.logical_and(sh(t, 4) == sh(j, 4), j <= t), mm_nt(q * jnp.exp(cum), k * jnp.exp(-cum)), 0.0)
    for n, s in enumerate((16, 32, 64)):
        m = jnp.logical_and(sh(t, 5 + n) == sh(j, 5 + n), jnp.logical_and((t & s) != 0, (j & s) == 0))
        a = a + jnp.where(m, mm_nt(q * jnp.exp(cums[n]), k * jnp.exp(tots[n] - cums[n])), 0.0)
    o = mm_nn(a, zi) + mm_nt(q * jnp.exp(cums[3]), st)
    st = st * jnp.exp(tots[3][0:1, :]) + mm_tn(zi, k * jnp.exp(tots[3] - cums[3]))
    return _rms(o, go) * _silu(zg), st


def hgrn_mixer(h, g, w, rider=None):
    T = h.shape[0]
    a = norm_fwd("hgrn_norm", h, g, BF16)
    z = matmul("hgrn_in", a, w['w_in'], 'nn')
    grid = (HG_H, T // HG_R)
    args = [Arg(z, (HG_R, 512), lambda hh, j: (j, hh), gdtype=BF16), Arg(w['lb'], (1, 128), lambda hh, j: (0, hh)),
            _const(w['go'])]
    outs = [Out((T, D), BF16, (HG_R, 128), lambda hh, j: (j, hh))]
    o, states = stage_fwd("hgrn_gla", _hgrn_fn, grid, args, outs, state_shape=(HG_D, HG_D), rider=rider)
    h_new = matmul("hgrn_o", o, w['w_o'], 'nn', res=h)

    def bwd(dh, rider=None):
        do, dwo = linear_bwd("hgrn_o_b", o, w['w_o'], dh, mask=True)
        dz, dlb, dgo = stage_bwd("hgrn_gla_b", _hgrn_fn, grid, args, outs, [do], state_shape=(HG_D, HG_D),
                                 states=states, rider=rider)
        da, dwin = linear_bwd("hgrn_in_b", a, w['w_in'], dz)
        dh2, dg = norm_bwd("hgrn_norm_b", h, g, da, dh)
        return dh2, dict(g=dg, w_in=dwin, lb=dlb, go=dgo, w_o=dwo)

    return h_new, bwd


S5_R = 384
S5_W = 512
S5_SLABS = D // 128


def _cmul(ar, ai, br, bi):
    return ar * br - ai * bi, ar * bi + ai * br


def _s5_scan(br, bi, tab, cr, ci, reverse):
    R, W = br.shape
    G = R // 8
    xr, xi = br.reshape(G, 8, W), bi.reshape(G, 8, W)
    for n, d in enumerate((1, 2, 4)):
        sh = (8 - d) if reverse else d
        mr, mi = _cmul(tab[2 * n][None], tab[2 * n + 1][None], pltpu.roll(xr, sh, 1), pltpu.roll(xi, sh, 1))
        xr, xi = xr + mr, xi + mi
    pr, pi = tab[6], tab[7]
    edge = 0 if reverse else 7
    out_r, out_i = [None] * G, [None] * G
    for g in (range(G - 1, -1, -1) if reverse else range(G)):
        ar, ai = _cmul(pr, pi, cr, ci)
        gr, gi = xr[g] + ar, xi[g] + ai
        cr, ci = gr[edge:edge + 1], gi[edge:edge + 1]
        out_r[g], out_i[g] = gr, gi
    return jnp.concatenate(out_r, axis=0), jnp.concatenate(out_i, axis=0), cr, ci


def s5_scan_fwd(a, bb, cb, tab, rider=None):
    T = a.shape[0]
    nb = T // S5_R

    def body(a_ref, bb_ref, cb_ref, tab_ref, y_ref, xs_ref, c_ref):
        @pl.when(pl.program_id(1) == 0)
        def _():
            c_ref[...] = jnp.zeros(c_ref.shape, F32)

        bu = _dg(a_ref[...], bb_ref[...], 1, 0)
        t = tab_ref[...]
        xr, xi, cr, ci = _s5_scan(bu[:, :S5_W], bu[:, S5_W:], t, c_ref[0:1, :S5_W], c_ref[0:1, S5_W:], False)
        x = jnp.concatenate([xr, xi], axis=1)
        xs_ref[...] = x
        y_ref[...] = _dg(x, cb_ref[...], 1, 0)
        c_ref[0:1, :] = jnp.concatenate([cr, ci], axis=1)

    return _pallas(
        "s5_scan", body, (S5_SLABS, nb),
        [pl.BlockSpec((S5_R, 128), lambda j, i: (i, j)),
         pl.BlockSpec((None, 128, 2 * S5_W), lambda j, i: (j, 0, 0)),
         pl.BlockSpec((None, 2 * S5_W, 128), lambda j, i: (j, 0, 0)),
         pl.BlockSpec((None, 10, 8, S5_W), lambda j, i: (j, 0, 0, 0))],
        [pl.BlockSpec((S5_R, 128), lambda j, i: (i, j)),
         pl.BlockSpec((None, S5_R, 2 * S5_W), lambda j, i: (j, i, 0))],
        [jax.ShapeDtypeStruct((T, D), F32), jax.ShapeDtypeStruct((S5_SLABS, T, 2 * S5_W), F32)],
        [pltpu.VMEM((8, 2 * S5_W), F32)], [a, bb, cb, tab], rider)


def s5_scan_bwd(a, bb, cb, tab_rev, xs, dy, rider=None):
    T = a.shape[0]
    nb = T // S5_R
    rg = S5_R // 8

    def body(a_ref, dy_ref, xs_ref, xp_ref, bb_ref, cb_ref, tab_ref, da_ref, dbb_ref, dcb_ref, dab_ref, c_ref):
        i = pl.program_id(1)

        @pl.when(i == 0)
        def _():
            c_ref[...] = jnp.zeros(c_ref.shape, F32)

        dy_v = dy_ref[...]
        x = xs_ref[...]
        dxo = _dg(dy_v, cb_ref[...], 1, 1)
        gr, gi, cr, ci = _s5_scan(dxo[:, :S5_W], dxo[:, S5_W:], tab_ref[...], c_ref[0:1, :S5_W], c_ref[0:1, S5_W:], True)
        c_ref[0:1, :] = jnp.concatenate([cr, ci], axis=1)
        g = jnp.concatenate([gr, gi], axis=1)
        da_ref[...] = _dg(g, bb_ref[...], 1, 1)
        dbb = _dg(a_ref[...], g, 0, 0)
        dcb = _dg(x, dy_v, 0, 0)
        first_tile = i == nb - 1
        prev_last = jnp.where(first_tile, 0.0, xp_ref[7:8, :])
        rows = lax.broadcasted_iota(jnp.int32, x.shape, 0)
        xp = jnp.where(rows == 0, prev_last, pltpu.roll(x, 1, 0))
        xpr, xpi = xp[:, :S5_W], xp[:, S5_W:]
        dar = (gr * xpr + gi * xpi).reshape(rg, 8, S5_W).sum(axis=0)
        dai = (gi * xpr - gr * xpi).reshape(rg, 8, S5_W).sum(axis=0)
        dab = jnp.concatenate([dar, dai], axis=1)

        @pl.when(i == 0)
        def _():
            dbb_ref[...] = dbb
            dcb_ref[...] = dcb
            dab_ref[...] = dab

        @pl.when(i != 0)
        def _():
            dbb_ref[...] += dbb
            dcb_ref[...] += dcb
            dab_ref[...] += dab

    def prev_rows(j, i):
        return (j, jnp.maximum((nb - 1 - i) * rg - 1, 0), 0)

    return _pallas(
        "s5_scan_b", body, (S5_SLABS, nb),
        [pl.BlockSpec((S5_R, 128), lambda j, i: (nb - 1 - i, j)),
         pl.BlockSpec((S5_R, 128), lambda j, i: (nb - 1 - i, j)),
         pl.BlockSpec((None, S5_R, 2 * S5_W), lambda j, i: (j, nb - 1 - i, 0)),
         pl.BlockSpec((None, 8, 2 * S5_W), prev_rows),
         pl.BlockSpec((None, 128, 2 * S5_W), lambda j, i: (j, 0, 0)),
         pl.BlockSpec((None, 2 * S5_W, 128), lambda j, i: (j, 0, 0)),
         pl.BlockSpec((None, 10, 8, S5_W), lambda j, i: (j, 0, 0, 0))],
        [pl.BlockSpec((S5_R, 128), lambda j, i: (nb - 1 - i, j)),
         pl.BlockSpec((None, 128, 2 * S5_W), lambda j, i: (j, 0, 0)),
         pl.BlockSpec((None, 2 * S5_W, 128), lambda j, i: (j, 0, 0)),
         pl.BlockSpec((None, 8, 2 * S5_W), lambda j, i: (j, 0, 0))],
        [jax.ShapeDtypeStruct((T, D), F32), jax.ShapeDtypeStruct((S5_SLABS, 128, 2 * S5_W), F32),
         jax.ShapeDtypeStruct((S5_SLABS, 2 * S5_W, 128), F32), jax.ShapeDtypeStruct((S5_SLABS, 8, 2 * S5_W), F32)],
        [pltpu.VMEM((8, 2 * S5_W), F32)], [a, dy, xs, xs, bb, cb, tab_rev], rider)


def _s5_discretise(lam_re, lam_im, log_dt, b_re, b_im, c_re, c_im):
    dt = jnp.exp(log_dt)[:, None]
    mag = jnp.exp(lam_re * dt)
    abar_re = mag * jnp.cos(lam_im * dt)
    abar_im = mag * jnp.sin(lam_im * dt)
    den = lam_re * lam_re + lam_im * lam_im
    zoh_re = ((abar_re - 1.0) * lam_re + abar_im * lam_im) / den
    zoh_im = (abar_im * lam_re - (abar_re - 1.0) * lam_im) / den
    bbar_re = zoh_re[..., None] * b_re - zoh_im[..., None] * b_im
    bbar_im = zoh_re[..., None] * b_im + zoh_im[..., None] * b_re
    eye = jnp.eye(8, dtype=F32)

    def in_map(bbar):
        t = bbar.reshape(8, 8, S5_P, S5_K).transpose(0, 1, 3, 2)
        return (t[:, :, :, None, :] * eye[None, :, None, :, None]).reshape(8, 8 * S5_K, 8 * S5_P)

    def out_map(c):
        t = c.reshape(8, 8, S5_K, S5_P).transpose(0, 1, 3, 2)
        return (t[:, :, :, None, :] * eye[None, :, None, :, None]).reshape(8, 8 * S5_P, 8 * S5_K)

    bb = jnp.concatenate([in_map(bbar_re), in_map(bbar_im)], axis=2)
    cb = jnp.concatenate([out_map(c_re), -out_map(c_im)], axis=1)
    return bb, cb, abar_re.reshape(8, S5_W), abar_im.reshape(8, S5_W)


def _s5_tables(ar, ai, reverse):
    if reverse:
        ai = -ai
    pw = [(jnp.ones_like(ar), jnp.zeros_like(ar))]
    for _ in range(8):
        pw.append(_cmul(pw[-1][0], pw[-1][1], ar, ai))
    r = jnp.arange(8)[None, :, None]
    rows = []
    for d in (1, 2, 4):
        keep = (r <= 7 - d) if reverse else (r >= d)
        rows += [jnp.where(keep, pw[d][0][:, None, :], 0.0), jnp.where(keep, pw[d][1][:, None, :], 0.0)]
    order = [8 - k for k in range(8)] if reverse else [k + 1 for k in range(8)]
    rows += [jnp.stack([pw[n][0] for n in order], axis=1), jnp.stack([pw[n][1] for n in order], axis=1)]
    rows += [jnp.broadcast_to(pw[8][0][:, None, :], (8, 8, S5_W)), jnp.broadcast_to(pw[8][1][:, None, :], (8, 8, S5_W))]
    return jnp.stack(rows, axis=1)


def _s5_act_fn(pids, yc, a, dskip):
    return (jax.nn.gelu(yc + dskip * a),)


def _make_glu_res_fn(tm):
    def glu_res_fn(pids, zz, h):
        rows = _row_ids(pids[0], tm, (tm, 1))
        return (h + jnp.where(rows >= PAD, zz[:, :D] * jax.nn.sigmoid(zz[:, D:]), 0.0),)
    return glu_res_fn


def s5_mixer(h, g, w, rider=None):
    T = h.shape[0]
    tm = _row_tile(T)
    nt = T // tm
    a = norm_fwd("s5_norm", h, g, F32)
    ssm = [w[n] for n in ('lam_re', 'lam_im', 'log_dt', 'b_re', 'b_im', 'c_re', 'c_im')]
    (bb, cb, ar, ai), disc_vjp = jax.vjp(_s5_discretise, *ssm)
    yc, xs = s5_scan_fwd(a, bb, cb, _s5_tables(ar, ai, False), rider=rider)
    row = lambda arr: _rows(arr, tm)
    act_args = [row(yc), row(a), _const(w['dskip'])]
    act_outs = [Out((T, D), BF16, (tm, D), lambda i: (i, 0))]
    y = stage_fwd("s5_act", _s5_act_fn, (nt,), act_args, act_outs)[0]
    zz = matmul("s5_glu", y, w['w_glu'], 'nn')
    glu_fn = _make_glu_res_fn(tm)
    glu_args = [_rows(zz, tm, BF16), row(h)]
    glu_outs = [Out((T, D), F32, (tm, D), lambda i: (i, 0))]
    h_new = stage_fwd("s5_gate", glu_fn, (nt,), glu_args, glu_outs)[0]

    def bwd(dh, rider=None):
        dzz, dh_res = stage_bwd("s5_gate_b", glu_fn, (nt,), glu_args, glu_outs, [dh])
        dy, dwglu = linear_bwd("s5_glu_b", y, w['w_glu'], dzz)
        dyc, da1, ddskip = stage_bwd("s5_act_b", _s5_act_fn, (nt,), act_args, act_outs, [dy])
        da2, dbb, dcb, dab = s5_scan_bwd(a, bb, cb, _s5_tables(ar, ai, True), xs, dyc, rider=rider)
        dab = dab.sum(axis=1)
        dssm = disc_vjp((dbb, dcb, dab[:, :S5_W], dab[:, S5_W:]))
        dh2, dg = _s5_norm_bwd(h, g, da1, da2, dh_res, tm)
        grads = dict(zip(('lam_re', 'lam_im', 'log_dt', 'b_re', 'b_im', 'c_re', 'c_im'), dssm))
        grads.update(g=dg, dskip=ddskip, w_glu=dwglu)
        return dh2, grads

    return h_new, bwd


def _norm3_bwd_fn(pids, h, g):
    a = _rms(h, g)
    return a, a, h


def _s5_norm_bwd(h, g, da1, da2, dh, tm):
    T = h.shape[0]
    o = Out((T, D), F32, (tm, D), lambda i: (i, 0))
    return stage_bwd("s5_norm_b", _norm3_bwd_fn, (T // tm,), [_rows(h, tm), _const(g)], [o, o, o], [da1, da2, dh])


RET_R = 384


def _rope256(x, cos, sin):
    x1, x2 = x[:, :128], x[:, 128:]
    return jnp.concatenate([x1 * cos - x2 * sin, x1 * sin + x2 * cos], axis=1)


def _ret_fn(pids, z, gn, cos, sin, dmat, qdec, kdec, cdec, st):
    R = z.shape[0]
    q = _rope256(z[:, :256], cos, sin)
    k = _rope256(z[:, 256:512], cos, sin) * (RET_DK ** -0.5)
    v, gate = z[:, 512:1024], z[:, 1024:]
    outs = []
    for cc in range(R // CHUNK):
        lo = cc * CHUNK
        qc, kc, vc = q[lo:lo + CHUNK], k[lo:lo + CHUNK], v[lo:lo + CHUNK]
        outs.append(mm_nn(mm_nt(qc, kc) * dmat, vc) + mm_nn(qc * qdec, st))
        st = st * cdec + mm_tn(kc * kdec, vc)
    o = jnp.concatenate(outs, axis=0)
    mu = jnp.mean(o, axis=-1, keepdims=True)
    var = jnp.mean(jnp.square(o - mu), axis=-1, keepdims=True)
    o = (o - mu) * lax.rsqrt(var + EPS)
    return o * gn * _silu(gate), st


def ret_mixer(h, g, w, tabs, rider=None):
    T = h.shape[0]
    a = norm_fwd("ret_norm", h, g, BF16)
    z = matmul("ret_in", a, w['w_in'], 'nn')
    grid = (RET_H, T // RET_R)
    hw = RET_DK * 2 + RET_DV * 2
    args = [Arg(z, (RET_R, hw), lambda hh, j: (j, hh), gdtype=BF16), Arg(w['gn'], (1, RET_DV), lambda hh, j: (0, hh)),
            Arg(tabs['cos_d'], (RET_R, 128), lambda hh, j: (j, 0), False),
            Arg(tabs['sin_d'], (RET_R, 128), lambda hh, j: (j, 0), False),
            Arg(tabs['ret_dmat'], (None, CHUNK, CHUNK), lambda hh, j: (hh, 0, 0), False),
            Arg(tabs['ret_qdec'], (None, CHUNK, 1), lambda hh, j: (hh, 0, 0), False),
            Arg(tabs['ret_kdec'], (None, CHUNK, 1), lambda hh, j: (hh, 0, 0), False),
            Arg(tabs['ret_cdec'], (None, 1, 1), lambda hh, j: (hh, 0, 0), False)]
    outs = [Out((T, RET_H * RET_DV), BF16, (RET_R, RET_DV), lambda hh, j: (j, hh))]
    o, states = stage_fwd("ret_chunks", _ret_fn, grid, args, outs, state_shape=(RET_DK, RET_DV), rider=rider)
    h_new = matmul("ret_o", o, w['w_o'], 'nn', res=h)

    def bwd(dh, rider=None):
        do, dwo = linear_bwd("ret_o_b", o, w['w_o'], dh, mask=True)
        dz, dgn = stage_bwd("ret_chunks_b", _ret_fn, grid, args, outs, [do], state_shape=(RET_DK, RET_DV),
                            states=states, rider=rider)
        da, dwin = linear_bwd("ret_in_b", a, w['w_in'], dz)
        dh2, dg = norm_bwd("ret_norm_b", h, g, da, dh)
        return dh2, dict(g=dg, w_in=dwin, gn=dgn, w_o=dwo)

    return h_new, bwd


def loss_head(h, tgt):
    T = h.shape[0]
    tm = _row_tile(T)

    def body(h_ref, t_ref, loss_ref, dh_ref):
        i = pl.program_id(0)
        rows = _row_ids(i, tm, (tm, 1))
        err = jnp.where(rows >= LEAD, h_ref[...] - t_ref[...], 0.0)
        dh_ref[...] = err * (1.0 / D)
        part = jnp.full((8, 128), 0.5 * jnp.sum(jnp.sum(err * err, axis=1, keepdims=True) * (1.0 / D)), F32)

        @pl.when(i == 0)
        def _():
            loss_ref[...] = part

        @pl.when(i != 0)
        def _():
            loss_ref[...] += part

    loss, dh = pl.pallas_call(
        body, grid=(T // tm,),
        in_specs=[pl.BlockSpec((tm, D), lambda i: (i, 0)), pl.BlockSpec((tm, D), lambda i: (i, 0))],
        out_specs=[pl.BlockSpec((8, 128), lambda i: (0, 0)), pl.BlockSpec((tm, D), lambda i: (i, 0))],
        out_shape=[jax.ShapeDtypeStruct((8, 128), F32), jax.ShapeDtypeStruct((T, D), F32)], name="loss_head",
        compiler_params=_cparams())(h, tgt)
    return loss[0, 0], dh


def _tables(T):
    pos = jnp.maximum(jnp.arange(T, dtype=jnp.int32) - PAD, 0).astype(F32)

    def cs(dim):
        inv_freq = 1.0 / (10000.0 ** (jnp.arange(0, dim, 2, dtype=F32) / dim))
        ang = pos[:, None] * inv_freq[None, :]
        return jnp.cos(ang), jnp.sin(ang)

    ca, sa = cs(MLA_ROPE)
    zeros = jnp.zeros((T, 64), F32)
    cd, sd = cs(RET_DK)
    log_gamma = jnp.log(1.0 - jnp.exp2(-5.0 - jnp.arange(RET_H, dtype=F32)))
    p = jnp.arange(CHUNK, dtype=F32)
    diff = p[:, None] - p[None, :]
    dmat = jnp.where(diff >= 0, jnp.exp(diff[None] * log_gamma[:, None, None]), 0.0)
    return dict(cos_a=jnp.concatenate([ca, ca, zeros], axis=1), sin_a=jnp.concatenate([-sa, sa, zeros], axis=1),
                cos_d=cd, sin_d=sd, ret_dmat=dmat,
                ret_qdec=jnp.exp((p[None, :] + 1.0) * log_gamma[:, None])[..., None],
                ret_kdec=jnp.exp((CHUNK - 1.0 - p[None, :]) * log_gamma[:, None])[..., None],
                ret_cdec=jnp.exp(CHUNK * log_gamma)[:, None, None])


def _hgrn_lower_bound(logits):
    lb_cum = jnp.cumsum(jax.nn.softmax(logits, axis=0), axis=0)
    return (lb_cum - lb_cum[0:1])[1:2]


def _uq_to_heads(w):
    t = w.reshape(w.shape[0], MLA_H, MLA_QK)
    return jnp.pad(t, ((0, 0), (0, 0), (0, 256 - MLA_QK))).reshape(w.shape[0], MLA_H * 256)


def _uq_from_heads(g):
    return g.reshape(g.shape[0], MLA_H, 256)[:, :, :MLA_QK].reshape(g.shape[0], MLA_H * MLA_QK)


def _head_interleave(w, widths, heads):
    parts, lo = [], 0
    for wd in widths:
        parts.append(w[:, lo:lo + heads * wd].reshape(w.shape[0], heads, wd))
        lo += heads * wd
    return jnp.concatenate(parts, axis=2).reshape(w.shape[0], -1)


def _head_deinterleave(g, widths, heads):
    t = g.reshape(g.shape[0], heads, sum(widths))
    parts, lo = [], 0
    for wd in widths:
        parts.append(t[:, :, lo:lo + wd].reshape(g.shape[0], heads * wd))
        lo += wd
    return jnp.concatenate(parts, axis=1)


HG_WIDTHS = (128, 128, 128, 128)
RET_WIDTHS = (RET_DK, RET_DK, RET_DV, RET_DV)


def _split_head_gain(g):
    return g[:, :128], jnp.pad(g[:, 128:], ((0, 0), (0, 64)))


def _join_head_gain(dn, dr):
    return jnp.concatenate([dn, dr[:, :64]], axis=1)


def local_step(x, target, W, ex):
    S = x.shape[0]
    T = S + LEAD
    tabs = _tables(T)
    h = jnp.concatenate([jnp.zeros((PAD, D), F32), W['meta_tokens'], x], axis=0)
    tgt = jnp.concatenate([jnp.zeros((LEAD, D), F32), target], axis=0)

    gqn, gqr = _split_head_gain(W['mla_q_head_g'])
    gkn, gkr = _split_head_gain(W['mla_k_head_g'])
    lb, lb_vjp = jax.vjp(_hgrn_lower_bound, W['hgrn_lb_logits'])

    def ffn(i, hh):
        return ffn_layer(i, hh, W['norm_ffn_g'][i:i + 1], ex.weight('ffn_w_up', i), W['ffn_conv_w'][i],
                         W['ffn_conv_b'][i:i + 1], ex.weight('ffn_w_down', i))

    bm, bf = [None] * 4, [None] * 4
    ex.gather(['mla'], name="gather_mla")
    w0 = dict(w_down=jnp.pad(ex.weight('mla_w_down'), ((0, 0), (0, 64))), gcq=W['mla_cq_norm_g'],
              gckv=W['mla_ckv_norm_g'], w_uq=_uq_to_heads(ex.weight('mla_w_uq')), w_ukv=ex.weight('mla_w_ukv'),
              gqn=gqn, gqr=gqr, gkn=gkn, gkr=gkr, w_o=ex.weight('mla_w_o'))
    h, bm[0] = mla_mixer(h, W['norm_mix_g'][0:1], w0, tabs, rider=ex.gather(['ffn0', 'hgrn', 'ffn1']))
    h, bf[0] = ffn(0, h)
    w1 = dict(w_in=_head_interleave(ex.weight('hgrn_w_in'), HG_WIDTHS, HG_H), lb=lb, go=W['hgrn_o_norm_g'],
              w_o=ex.weight('hgrn_w_o'))
    h, bm[1] = hgrn_mixer(h, W['norm_mix_g'][1:2], w1, rider=ex.gather(['s5', 'ffn2']))
    h, bf[1] = ffn(1, h)
    w2 = dict(lam_re=W['s5_lam_re'][0], lam_im=W['s5_lam_im'][0], log_dt=W['s5_log_dt'][0], b_re=W['s5_b_re'][0],
              b_im=W['s5_b_im'][0], c_re=W['s5_c_re'][0], c_im=W['s5_c_im'][0], dskip=W['s5_d'],
              w_glu=ex.weight('s5_w_glu'))
    h, bm[2] = s5_mixer(h, W['norm_mix_g'][2:3], w2, rider=ex.gather(['ret']))
    h, bf[2] = ffn(2, h)
    w3 = dict(w_in=_head_interleave(ex.weight('ret_w_in'), RET_WIDTHS, RET_H), gn=W['ret_gn_g'],
              w_o=ex.weight('ret_w_o'))
    h, bm[3] = ret_mixer(h, W['norm_mix_g'][3:4], w3, tabs, rider=ex.gather(['ffn3']))
    h, bf[3] = ffn(3, h)

    loss, dh = loss_head(h, tgt)

    def ffn_grads(i, g):
        return {('ffn_w_up', i): g['w_up'], ('ffn_w_down', i): g['w_down']}

    gm, gf = [None] * 4, [None] * 4
    dh, gf[3] = bf[3](dh)
    dh, gm[3] = bm[3](dh, rider=ex.scatter(ffn_grads(3, gf[3])))
    dh, gf[2] = bf[2](dh)
    ret_grads = {('ret_w_in', 0): _head_deinterleave(gm[3]['w_in'], RET_WIDTHS, RET_H), ('ret_w_o', 0): gm[3]['w_o']}
    dh, gm[2] = bm[2](dh, rider=ex.scatter({**ret_grads, **ffn_grads(2, gf[2])}))
    dh, gf[1] = bf[1](dh)
    dh, gm[1] = bm[1](dh, rider=ex.scatter({('s5_w_glu', 0): gm[2]['w_glu'], **ffn_grads(1, gf[1])}))
    dh, gf[0] = bf[0](dh)
    hgrn_grads = {('hgrn_w_in', 0): _head_deinterleave(gm[1]['w_in'], HG_WIDTHS, HG_H), ('hgrn_w_o', 0): gm[1]['w_o']}
    G = {}
    G['norm_ffn_g'] = jnp.concatenate([gf[i]['g'] for i in range(4)], axis=0)
    G['hgrn_lb_logits'] = lb_vjp(gm[1]['lb'])[0]
    G['hgrn_o_norm_g'] = gm[1]['go']
    for n in ('lam_re', 'lam_im', 'log_dt', 'b_re', 'b_im', 'c_re', 'c_im'):
        G['s5_' + n] = gm[2][n][None]
    G['s5_d'] = gm[2]['dskip']
    G['ret_gn_g'] = gm[3]['gn']
    G['ffn_conv_w'] = jnp.stack([gf[i]['cw'] for i in range(4)])
    G['ffn_conv_b'] = jnp.concatenate([gf[i]['cb'] for i in range(4)], axis=0)
    early = ex.all_devices(_pack([G[n] for n in SMALL_EARLY], F32, 8))

    dh, gm[0] = bm[0](dh, rider=[ex.scatter({**hgrn_grads, **ffn_grads(0, gf[0])}), ex.swap()], rider_dq=early)
    a = gm[0]
    ex.scatter({('mla_w_down', 0): a['w_down'][:, :MLA_QL + MLA_KVL + MLA_ROPE], ('mla_w_uq', 0): _uq_from_heads(a['w_uq']),
                ('mla_w_ukv', 0): a['w_ukv'], ('mla_w_o', 0): a['w_o']}, name="scatter_mla")
    G['meta_tokens'] = dh[PAD:LEAD]
    G['norm_mix_g'] = jnp.concatenate([gm[i]['g'] for i in range(4)], axis=0)
    G['mla_cq_norm_g'], G['mla_ckv_norm_g'] = a['gcq'], a['gckv']
    G['mla_q_head_g'] = _join_head_gain(a['gqn'], a['gqr'])
    G['mla_k_head_g'] = _join_head_gain(a['gkn'], a['gkr'])
    ex.all_devices(_pack([G[n] for n in SMALL_LATE], F32, 8), name="grad_small_gather")
    return loss, dh[LEAD:], G


PACK_W = 1024
ANY = pl.BlockSpec(memory_space=pl.ANY)


def _pack(arrs, dtype, row_mult):
    flat = jnp.concatenate([a.reshape(-1).astype(dtype) for a in arrs])
    n = flat.shape[0]
    rows = -(-n // (PACK_W * row_mult)) * row_mult
    return jnp.pad(flat, (0, rows * PACK_W - n)).reshape(rows, PACK_W)


def _unpack(buf, shapes):
    flat = buf.reshape(-1)
    out, off = [], 0
    for s in shapes:
        n = math.prod(s)
        out.append(flat[off:off + n].reshape(s))
        off += n
    return out


def _my_pos():
    return lax.axis_index("x"), lax.axis_index("y"), lax.axis_index("c")


def _other_chips(x, y):
    return [(1 - x, y), (x, 1 - y), (1 - x, 1 - y)]


def gather_chips(name, src):
    def body(src_ref, out_ref, send_sems, recv_sems, local_sem):
        x, y, c = _my_pos()
        q = 2 * x + y
        mine = pltpu.make_async_copy(src_ref, out_ref.at[q], local_sem)
        mine.start()
        peers = _other_chips(x, y)

        def copy(k, slot, peer):
            return pltpu.make_async_remote_copy(src_ref=src_ref, dst_ref=out_ref.at[slot], send_sem=send_sems.at[k],
                                                recv_sem=recv_sems.at[k], device_id=(peer[0], peer[1], c),
                                                device_id_type=MESH_ID)
        sends = [copy(k, q, p) for k, p in enumerate(peers)]
        for cp in sends:
            cp.start()
        for k, p in enumerate(peers):
            copy(k, 2 * p[0] + p[1], p).wait_recv()
        for cp in sends:
            cp.wait_send()
        mine.wait()

    return pl.pallas_call(body, out_shape=jax.ShapeDtypeStruct((4,) + src.shape, src.dtype), in_specs=[ANY],
                          out_specs=ANY, name=name,
                          scratch_shapes=[pltpu.SemaphoreType.DMA((3,)), pltpu.SemaphoreType.DMA((3,)),
                                          pltpu.SemaphoreType.DMA(())])(src)


def scatter_chips(name, src):
    def body(src_ref, out_ref, send_sems, recv_sems, local_sem):
        x, y, c = _my_pos()
        q = 2 * x + y
        mine = pltpu.make_async_copy(src_ref.at[q], out_ref.at[q], local_sem)
        mine.start()
        peers = _other_chips(x, y)

        def copy(k, peer):
            slot = 2 * peer[0] + peer[1]
            return pltpu.make_async_remote_copy(src_ref=src_ref.at[slot], dst_ref=out_ref.at[q], send_sem=send_sems.at[k],
                                                recv_sem=recv_sems.at[k], device_id=(peer[0], peer[1], c),
                                                device_id_type=MESH_ID)

        def landing(k, peer):
            slot = 2 * peer[0] + peer[1]
            return pltpu.make_async_remote_copy(src_ref=src_ref.at[slot], dst_ref=out_ref.at[slot],
                                                send_sem=send_sems.at[k], recv_sem=recv_sems.at[k],
                                                device_id=(peer[0], peer[1], c), device_id_type=MESH_ID)
        sends = [copy(k, p) for k, p in enumerate(peers)]
        for cp in sends:
            cp.start()
        for k, p in enumerate(peers):
            landing(k, p).wait_recv()
        for cp in sends:
            cp.wait_send()
        mine.wait()

    return pl.pallas_call(body, out_shape=jax.ShapeDtypeStruct(src.shape, src.dtype), in_specs=[ANY], out_specs=ANY,
                          name=name, scratch_shapes=[pltpu.SemaphoreType.DMA((3,)), pltpu.SemaphoreType.DMA((3,)),
                                                     pltpu.SemaphoreType.DMA(())])(src)


def swap_sibling(name, src):
    def body(src_ref, out_ref, send_sem, recv_sem):
        x, y, c = _my_pos()
        cp = pltpu.make_async_remote_copy(src_ref=src_ref, dst_ref=out_ref, send_sem=send_sem, recv_sem=recv_sem,
                                          device_id=(x, y, 1 - c), device_id_type=MESH_ID)
        cp.start()
        cp.wait()

    return pl.pallas_call(body, out_shape=jax.ShapeDtypeStruct(src.shape, src.dtype), in_specs=[ANY], out_specs=ANY,
                          name=name, scratch_shapes=[pltpu.SemaphoreType.DMA(()), pltpu.SemaphoreType.DMA(())])(src)


def gather_all(name, src):
    def body(src_ref, out_ref, send_sems, recv_sems, local_sem):
        x, y, c = _my_pos()
        me = 4 * x + 2 * y + c
        mine = pltpu.make_async_copy(src_ref, out_ref.at[me], local_sem)
        mine.start()
        peers = [((1 - x) if m & 4 else x, (1 - y) if m & 2 else y, (1 - c) if m & 1 else c) for m in range(1, 8)]

        def copy(k, slot, peer):
            return pltpu.make_async_remote_copy(src_ref=src_ref, dst_ref=out_ref.at[slot], send_sem=send_sems.at[k],
                                                recv_sem=recv_sems.at[k], device_id=peer, device_id_type=MESH_ID)
        sends = [copy(k, me, p) for k, p in enumerate(peers)]
        for cp in sends:
            cp.start()
        for k, p in enumerate(peers):
            copy(k, 4 * p[0] + 2 * p[1] + p[2], p).wait_recv()
        for cp in sends:
            cp.wait_send()
        mine.wait()

    return pl.pallas_call(body, out_shape=jax.ShapeDtypeStruct((8,) + src.shape, src.dtype), in_specs=[ANY],
                          out_specs=ANY, name=name,
                          scratch_shapes=[pltpu.SemaphoreType.DMA((7,)), pltpu.SemaphoreType.DMA((7,)),
                                          pltpu.SemaphoreType.DMA(())])(src)


def _pack_tile(rows):
    return _divisors(rows, (256, 128, 64, 32, 16, 8))[0] if rows > 512 else rows


def sum_slots(name, slots):
    n, rows, w = slots.shape
    tr = _pack_tile(rows)

    def body(s_ref, o_ref):
        acc = s_ref[0].astype(F32)
        for k in range(1, n):
            acc = acc + s_ref[k].astype(F32)
        o_ref[...] = acc

    return pl.pallas_call(body, grid=(rows // tr,), in_specs=[pl.BlockSpec((n, tr, w), lambda i: (0, i, 0))],
                          out_specs=pl.BlockSpec((tr, w), lambda i: (i, 0)),
                          out_shape=jax.ShapeDtypeStruct((rows, w), F32), name=name, compiler_params=_cparams())(slots)


def adamw(name, grads, w, m, v):
    rows, wd = w.shape
    tr = _pack_tile(rows)
    ng = len(grads)

    def body(*refs):
        g = refs[0][...]
        for r in refs[1:ng]:
            g = g + r[...]
        w_ref, m_ref, v_ref = refs[ng:ng + 3]
        g_out, d_out, m_out, v_out = refs[ng + 3:]
        m_new = ADAM_B1 * m_ref[...] + (1.0 - ADAM_B1) * g
        v_new = ADAM_B2 * v_ref[...] + (1.0 - ADAM_B2) * jnp.square(g)
        m_hat = m_new / (1.0 - ADAM_B1 ** ADAM_STEP)
        v_hat = v_new / (1.0 - ADAM_B2 ** ADAM_STEP)
        g_out[...] = g
        d_out[...] = -ADAM_LR * (m_hat / (jnp.sqrt(v_hat) + ADAM_EPS) + ADAM_WD * w_ref[...])
        m_out[...] = m_new
        v_out[...] = v_new

    spec = pl.BlockSpec((tr, wd), lambda i: (i, 0))
    shape = jax.ShapeDtypeStruct((rows, wd), F32)
    return pl.pallas_call(body, grid=(rows // tr,), in_specs=[spec] * (ng + 3), out_specs=[spec] * 4,
                          out_shape=[shape] * 4, name=name, compiler_params=_cparams())(*grads, w, m, v)


def _shard_of(ref, name, p):
    ax = SHARD_AXIS[name]
    n = ref.shape[ax] // 4
    idx = [slice(None)] * 3
    idx[ax] = pl.ds(pl.multiple_of(p * n, 128 if ax == 2 else 16), n)
    return ref.at[tuple(idx)]


def _sem_scratch(nw):
    return [pltpu.SemaphoreType.DMA((3 * nw,)), pltpu.SemaphoreType.DMA((3 * nw,)), pltpu.SemaphoreType.DMA((nw,))]


def gather_weights(name, names, shards):
    nw = len(shards)

    def full_shape(n, s):
        return tuple(d * 4 if ax == SHARD_AXIS[n] else d for ax, d in enumerate(s.shape))

    def body(*refs):
        src, dst = refs[:nw], refs[nw:2 * nw]
        send_sems, recv_sems, local_sems = refs[2 * nw:]
        x, y, c = _my_pos()
        q = 2 * x + y
        peers = _other_chips(x, y)
        local = [pltpu.make_async_copy(src[w], _shard_of(dst[w], names[w], q), local_sems.at[w]) for w in range(nw)]
        for cp in local:
            cp.start()

        def copy(w, k, slot):
            p = peers[k]
            return pltpu.make_async_remote_copy(src_ref=src[w], dst_ref=_shard_of(dst[w], names[w], slot),
                                                send_sem=send_sems.at[3 * w + k], recv_sem=recv_sems.at[3 * w + k],
                                                device_id=(p[0], p[1], c), device_id_type=MESH_ID)
        sends = [copy(w, k, q) for w in range(nw) for k in range(3)]
        for cp in sends:
            cp.start()
        for w in range(nw):
            for k in range(3):
                copy(w, k, 2 * peers[k][0] + peers[k][1]).wait_recv()
        for cp in sends:
            cp.wait_send()
        for cp in local:
            cp.wait()

    return pl.pallas_call(body, out_shape=[jax.ShapeDtypeStruct(full_shape(n, s), s.dtype) for n, s in zip(names, shards)],
                          in_specs=[ANY] * nw, out_specs=[ANY] * nw, name=name, scratch_shapes=_sem_scratch(nw))(*shards)


def scatter_grads(name, names, grads):
    nw = len(grads)

    def shard_shape(n, s):
        return tuple(d // 4 if ax == SHARD_AXIS[n] else d for ax, d in enumerate(s.shape))

    def body(*refs):
        src, dst = refs[:nw], refs[nw:2 * nw]
        send_sems, recv_sems, local_sems = refs[2 * nw:]
        x, y, c = _my_pos()
        q = 2 * x + y
        peers = _other_chips(x, y)
        local = [pltpu.make_async_copy(_shard_of(src[w], names[w], q), dst[w].at[q], local_sems.at[w])
                 for w in range(nw)]
        for cp in local:
            cp.start()

        def copy(w, k, slot):
            p = peers[k]
            return pltpu.make_async_remote_copy(src_ref=_shard_of(src[w], names[w], 2 * p[0] + p[1]),
                                                dst_ref=dst[w].at[slot], send_sem=send_sems.at[3 * w + k],
                                                recv_sem=recv_sems.at[3 * w + k], device_id=(p[0], p[1], c),
                                                device_id_type=MESH_ID)
        sends = [copy(w, k, q) for w in range(nw) for k in range(3)]
        for cp in sends:
            cp.start()
        for w in range(nw):
            for k in range(3):
                copy(w, k, 2 * peers[k][0] + peers[k][1]).wait_recv()
        for cp in sends:
            cp.wait_send()
        for cp in local:
            cp.wait()

    return pl.pallas_call(body, out_shape=[jax.ShapeDtypeStruct((4,) + shard_shape(n, g), g.dtype)
                                           for n, g in zip(names, grads)],
                          in_specs=[ANY] * nw, out_specs=[ANY] * nw, name=name, scratch_shapes=_sem_scratch(nw))(*grads)


def swap_siblings(name, arrs):
    nw = len(arrs)

    def body(*refs):
        src, dst = refs[:nw], refs[nw:2 * nw]
        send_sems, recv_sems = refs[2 * nw:]
        x, y, c = _my_pos()
        cps = [pltpu.make_async_remote_copy(src_ref=src[w], dst_ref=dst[w], send_sem=send_sems.at[w],
                                            recv_sem=recv_sems.at[w], device_id=(x, y, 1 - c), device_id_type=MESH_ID)
               for w in range(nw)]
        for cp in cps:
            cp.start()
        for cp in cps:
            cp.wait()

    return pl.pallas_call(body, out_shape=[jax.ShapeDtypeStruct(a.shape, a.dtype) for a in arrs], in_specs=[ANY] * nw,
                          out_specs=[ANY] * nw, name=name,
                          scratch_shapes=[pltpu.SemaphoreType.DMA((nw,)), pltpu.SemaphoreType.DMA((nw,))])(*arrs)


def _block2d(ref, axis, p, n):
    if axis == 0:
        return ref.at[pl.ds(pl.multiple_of(p * n, 16), n), :]
    return ref.at[:, pl.ds(pl.multiple_of(p * n, 128), n)]


class ScatterRider:
    def __init__(self, items):
        self.items = items
        self.operands = [it[0] for it in items]
        self.results = None
        self.out_shapes = [jax.ShapeDtypeStruct((4, arr.shape[0] // 4, arr.shape[1]) if axis == 0 else
                                                (4, arr.shape[0], arr.shape[1] // 4), arr.dtype) for arr, axis in items]
        self.scratch = _sem_scratch(len(items))

    def _copies(self, ins, outs, sems):
        send_sems, recv_sems, local_sems = sems
        x, y, c = _my_pos()
        q = 2 * x + y
        local, sends, lands = [], [], []
        for w, (arr, axis) in enumerate(self.items):
            n = arr.shape[axis] // 4
            local.append(pltpu.make_async_copy(_block2d(ins[w], axis, q, n), outs[w].at[q], local_sems.at[w]))
            for k, (px, py) in enumerate(_other_chips(x, y)):
                p = 2 * px + py
                sems_k = dict(send_sem=send_sems.at[3 * w + k], recv_sem=recv_sems.at[3 * w + k],
                              device_id=(px, py, c), device_id_type=MESH_ID)
                theirs = _block2d(ins[w], axis, p, n)
                sends.append(pltpu.make_async_remote_copy(src_ref=theirs, dst_ref=outs[w].at[q], **sems_k))
                lands.append(pltpu.make_async_remote_copy(src_ref=theirs, dst_ref=outs[w].at[p], **sems_k))
        return local, sends, lands

    def start(self, ins, outs, sems):
        local, sends, _ = self._copies(ins, outs, sems)
        for cp in local + sends:
            cp.start()

    def finish(self, ins, outs, sems):
        local, sends, lands = self._copies(ins, outs, sems)
        for cp in lands:
            cp.wait_recv()
        for cp in sends:
            cp.wait_send()
        for cp in local:
            cp.wait()


class GatherRider:
    def __init__(self, items):
        self.items = items
        self.operands = [it[0] for it in items]
        self.results = None
        self.out_shapes = []
        for arr, _, axis in items:
            r, c = arr.shape[1:]
            assert r % 32 == 0
            self.out_shapes.append(jax.ShapeDtypeStruct((4 * r, c) if axis == 0 else (r, 4 * c), arr.dtype))
        n = len(items)
        dma = pltpu.SemaphoreType.DMA
        self.scratch = [dma((3 * n,)), dma((3 * n,)), dma((n,)), dma((3 * n,)), dma((3 * n,))]

    def _copies(self, ins, outs, sems):
        send_sems, recv_sems, local_sems, pass_send_sems, pass_recv_sems = sems
        x, y, c = _my_pos()
        q = 2 * x + y
        local, sends, lands, passes, pass_lands = [], [], [], [], []
        for w, (arr, layer, axis) in enumerate(self.items):
            r, cols = arr.shape[1:]
            half = r // 2
            src = ins[w].at[layer]

            def part(blk, hc, w=w, axis=axis, r=r, cols=cols, half=half):
                if axis == 0:
                    return outs[w].at[pl.ds(pl.multiple_of(blk * r + hc * half, 16), half), :]
                return outs[w].at[pl.ds(pl.multiple_of(hc * half, 16), half), pl.ds(pl.multiple_of(blk * cols, 128), cols)]

            local.append(pltpu.make_async_copy(src, _block2d(outs[w], axis, q, arr.shape[1 + axis]), local_sems.at[w]))
            for k, (px, py) in enumerate(_other_chips(x, y)):
                p = 2 * px + py
                ici = dict(send_sem=send_sems.at[3 * w + k], recv_sem=recv_sems.at[3 * w + k],
                           device_id=(px, py, c), device_id_type=MESH_ID)
                d2d = dict(send_sem=pass_send_sems.at[3 * w + k], recv_sem=pass_recv_sems.at[3 * w + k],
                           device_id=(x, y, 1 - c), device_id_type=MESH_ID)
                mine = src.at[pl.ds(pl.multiple_of(c * half, 16), half), :]
                sends.append(pltpu.make_async_remote_copy(src_ref=mine, dst_ref=part(q, c), **ici))
                lands.append(pltpu.make_async_remote_copy(src_ref=mine, dst_ref=part(p, c), **ici))
                passes.append(pltpu.make_async_remote_copy(src_ref=part(p, c), dst_ref=part(p, c), **d2d))
                pass_lands.append(pltpu.make_async_remote_copy(src_ref=part(p, c), dst_ref=part(p, 1 - c), **d2d))
        return local, sends, lands, passes, pass_lands

    def start(self, ins, outs, sems):
        local, sends, _, _, _ = self._copies(ins, outs, sems)
        for cp in local + sends:
            cp.start()

    def middle(self, ins, outs, sems):
        _, _, lands, passes, _ = self._copies(ins, outs, sems)
        for land, cp in zip(lands, passes):
            land.wait_recv()
            cp.start()

    def finish(self, ins, outs, sems):
        local, sends, _, passes, pass_lands = self._copies(ins, outs, sems)
        for cp in pass_lands:
            cp.wait_recv()
        for cp in sends + passes:
            cp.wait_send()
        for cp in local:
            cp.wait()


class SwapRider:
    def __init__(self, arrs):
        self.operands = list(arrs)
        self.out_shapes = [jax.ShapeDtypeStruct(a.shape, a.dtype) for a in arrs]
        self.scratch = [pltpu.SemaphoreType.DMA((len(arrs),)), pltpu.SemaphoreType.DMA((len(arrs),))]
        self.results = None

    def _copies(self, ins, outs, sems):
        x, y, c = _my_pos()
        return [pltpu.make_async_remote_copy(src_ref=ins[w], dst_ref=outs[w], send_sem=sems[0].at[w],
                                             recv_sem=sems[1].at[w], device_id=(x, y, 1 - c), device_id_type=MESH_ID)
                for w in range(len(self.operands))]

    def start(self, ins, outs, sems):
        for cp in self._copies(ins, outs, sems):
            cp.start()

    def finish(self, ins, outs, sems):
        for cp in self._copies(ins, outs, sems):
            cp.wait()


class AllDevicesRider:
    def __init__(self, src):
        self.operands = [src]
        self.out_shapes = [jax.ShapeDtypeStruct((8,) + src.shape, src.dtype)]
        self.scratch = [pltpu.SemaphoreType.DMA((7,)), pltpu.SemaphoreType.DMA((7,)), pltpu.SemaphoreType.DMA(())]
        self.results = None

    def _copies(self, ins, outs, sems):
        x, y, c = _my_pos()
        me = 4 * x + 2 * y + c
        local = pltpu.make_async_copy(ins[0], outs[0].at[me], sems[2])
        sends, lands = [], []
        for k, m in enumerate(range(1, 8)):
            peer = ((1 - x) if m & 4 else x, (1 - y) if m & 2 else y, (1 - c) if m & 1 else c)
            sems_k = dict(send_sem=sems[0].at[k], recv_sem=sems[1].at[k], device_id=peer, device_id_type=MESH_ID)
            sends.append(pltpu.make_async_remote_copy(src_ref=ins[0], dst_ref=outs[0].at[me], **sems_k))
            lands.append(pltpu.make_async_remote_copy(src_ref=ins[0], dst_ref=outs[0].at[4 * peer[0] + 2 * peer[1] + peer[2]],
                                                      **sems_k))
        return local, sends, lands

    def start(self, ins, outs, sems):
        local, sends, _ = self._copies(ins, outs, sems)
        for cp in [local] + sends:
            cp.start()

    def finish(self, ins, outs, sems):
        local, sends, lands = self._copies(ins, outs, sems)
        for cp in lands:
            cp.wait_recv()
        for cp in sends:
            cp.wait_send()
        local.wait()


class RiderGroup:
    def __init__(self, riders):
        self.riders = riders
        self.operands = [a for r in riders for a in r.operands]
        self.out_shapes = [s for r in riders for s in r.out_shapes]
        self.scratch = [s for r in riders for s in r.scratch]

    def _split(self, ins, outs, sems):
        for r in self.riders:
            ni, no, ns = len(r.operands), len(r.out_shapes), len(r.scratch)
            yield r, ins[:ni], outs[:no], sems[:ns]
            ins, outs, sems = ins[ni:], outs[no:], sems[ns:]

    def start(self, ins, outs, sems):
        for r, i, o, s in self._split(ins, outs, sems):
            r.start(i, o, s)

    def middle(self, ins, outs, sems):
        for r, i, o, s in self._split(ins, outs, sems):
            if hasattr(r, 'middle'):
                r.middle(i, o, s)

    def finish(self, ins, outs, sems):
        for r, i, o, s in self._split(ins, outs, sems):
            r.finish(i, o, s)

    @property
    def results(self):
        return None

    @results.setter
    def results(self, res):
        for r in self.riders:
            no = len(r.out_shapes)
            r.results, res = list(res[:no]), res[no:]


def run_rider(name, rider):
    n_in, n_out = len(rider.operands), len(rider.out_shapes)

    def body(*refs):
        ins, outs, sems = refs[:n_in], refs[n_in:n_in + n_out], refs[n_in + n_out:]
        rider.start(ins, outs, sems)
        if hasattr(rider, 'middle'):
            rider.middle(ins, outs, sems)
        rider.finish(ins, outs, sems)

    rider.results = list(pl.pallas_call(body, out_shape=rider.out_shapes, in_specs=[ANY] * n_in, out_specs=[ANY] * n_out,
                                        name=name, scratch_shapes=rider.scratch)(*rider.operands))


WEIGHT_GROUPS = {'mla': [('mla_w_down', 0), ('mla_w_uq', 0), ('mla_w_ukv', 0), ('mla_w_o', 0)],
                 'hgrn': [('hgrn_w_in', 0), ('hgrn_w_o', 0)], 's5': [('s5_w_glu', 0)],
                 'ret': [('ret_w_in', 0), ('ret_w_o', 0)]}
WEIGHT_GROUPS.update({f'ffn{i}': [('ffn_w_up', i), ('ffn_w_down', i)] for i in range(4)})


class Exchange:
    def __init__(self, shards=None, full=None):
        self.shards, self.full = shards, dict(full or {})
        self.got, self.recv, self.sib, self.grads, self.small = {}, {}, {}, {}, []

    def gather(self, groups, name=None):
        if self.shards is None:
            return None
        keys = [k for g in groups for k in WEIGHT_GROUPS[g]]
        rider = GatherRider([(self.shards[n], layer, SHARD_AXIS[n] - 1) for n, layer in keys])
        self.got.update({k: (rider, j) for j, k in enumerate(keys)})
        if name is not None:
            run_rider(name, rider)
        return rider

    def weight(self, n, layer=0):
        if self.shards is None:
            return self.full[n][layer]
        rider, j = self.got[(n, layer)]
        return rider.results[j]

    def scatter(self, grads, name=None):
        if self.shards is None:
            self.grads.update(grads)
            return None
        keys = list(grads)
        rider = ScatterRider([(grads[k], SHARD_AXIS[k[0]] - 1) for k in keys])
        self.recv.update({k: (rider, j) for j, k in enumerate(keys)})
        if name is not None:
            run_rider(name, rider)
        return rider

    def received(self, n, layer):
        rider, j = self.recv[(n, layer)]
        return rider.results[j]

    def swap(self, name=None):
        if self.shards is None:
            return None
        keys = [k for k, (r, _) in self.recv.items() if k not in self.sib and r.results is not None]
        rider = SwapRider([self.received(*k) for k in keys])
        self.sib.update({k: (rider, j) for j, k in enumerate(keys)})
        if name is not None:
            run_rider(name, rider)
        return rider

    def sibling(self, n, layer):
        rider, j = self.sib[(n, layer)]
        return rider.results[j]

    def all_devices(self, packed, name=None):
        if self.shards is None:
            return None
        rider = AllDevicesRider(packed)
        self.small.append(rider)
        if name is not None:
            run_rider(name, rider)
        return rider


ADAM_BLOCK_ELEMS = 256 * 1024


def adamw_shard(name, mine, sib, w, m, v):
    nl, rows, cols = w.shape
    tr = [t for t in (512, 384, 352, 256, 176, 128, 64, 32, 16) if rows % t == 0 and t * cols <= ADAM_BLOCK_ELEMS][0]

    def body(a_ref, b_ref, w_ref, m_ref, v_ref, g_out, d_out, m_out, v_out):
        def total(r):
            acc = r[0].astype(F32)
            for k in range(1, 4):
                acc = acc + r[k].astype(F32)
            return acc
        g = total(a_ref) + total(b_ref)
        m_new = ADAM_B1 * m_ref[...] + (1.0 - ADAM_B1) * g
        v_new = ADAM_B2 * v_ref[...] + (1.0 - ADAM_B2) * jnp.square(g)
        m_hat = m_new / (1.0 - ADAM_B1 ** ADAM_STEP)
        v_hat = v_new / (1.0 - ADAM_B2 ** ADAM_STEP)
        g_out[...] = g
        d_out[...] = -ADAM_LR * (m_hat / (jnp.sqrt(v_hat) + ADAM_EPS) + ADAM_WD * w_ref[...])
        m_out[...] = m_new
        v_out[...] = v_new

    slots = pl.BlockSpec((4, None, tr, cols), lambda l, i: (0, l, i, 0))
    spec = pl.BlockSpec((None, tr, cols), lambda l, i: (l, i, 0))
    shape = jax.ShapeDtypeStruct(w.shape, F32)
    return pl.pallas_call(body, grid=(nl, rows // tr), in_specs=[slots, slots, spec, spec, spec], out_specs=[spec] * 4,
                          out_shape=[shape] * 4, name=name, compiler_params=_cparams())(mine, sib, w, m, v)


def kernel(x, meta_tokens, norm_mix_g, norm_ffn_g, mla_w_down, mla_cq_norm_g, mla_ckv_norm_g, mla_w_uq, mla_w_ukv, mla_q_head_g, mla_k_head_g, mla_w_o, hgrn_w_in, hgrn_lb_logits, hgrn_o_norm_g, hgrn_w_o, s5_lam_re, s5_lam_im, s5_log_dt, s5_b_re, s5_b_im, s5_c_re, s5_c_im, s5_d, s5_w_glu, ret_w_in, ret_gn_g, ret_w_o, ffn_w_up, ffn_conv_w, ffn_conv_b, ffn_w_down, loss_target, m_meta_tokens, m_norm_mix_g, m_norm_ffn_g, m_mla_w_down, m_mla_cq_norm_g, m_mla_ckv_norm_g, m_mla_w_uq, m_mla_w_ukv, m_mla_q_head_g, m_mla_k_head_g, m_mla_w_o, m_hgrn_w_in, m_hgrn_lb_logits, m_hgrn_o_norm_g, m_hgrn_w_o, m_s5_lam_re, m_s5_lam_im, m_s5_log_dt, m_s5_b_re, m_s5_b_im, m_s5_c_re, m_s5_c_im, m_s5_d, m_s5_w_glu, m_ret_w_in, m_ret_gn_g, m_ret_w_o, m_ffn_w_up, m_ffn_conv_w, m_ffn_conv_b, m_ffn_w_down, v_meta_tokens, v_norm_mix_g, v_norm_ffn_g, v_mla_w_down, v_mla_cq_norm_g, v_mla_ckv_norm_g, v_mla_w_uq, v_mla_w_ukv, v_mla_q_head_g, v_mla_k_head_g, v_mla_w_o, v_hgrn_w_in, v_hgrn_lb_logits, v_hgrn_o_norm_g, v_hgrn_w_o, v_s5_lam_re, v_s5_lam_im, v_s5_log_dt, v_s5_b_re, v_s5_b_im, v_s5_c_re, v_s5_c_im, v_s5_d, v_s5_w_glu, v_ret_w_in, v_ret_gn_g, v_ret_w_o, v_ffn_w_up, v_ffn_conv_w, v_ffn_conv_b, v_ffn_w_down):
    vals = (x, meta_tokens, norm_mix_g, norm_ffn_g, mla_w_down, mla_cq_norm_g, mla_ckv_norm_g, mla_w_uq, mla_w_ukv, mla_q_head_g, mla_k_head_g, mla_w_o, hgrn_w_in, hgrn_lb_logits, hgrn_o_norm_g, hgrn_w_o, s5_lam_re, s5_lam_im, s5_log_dt, s5_b_re, s5_b_im, s5_c_re, s5_c_im, s5_d, s5_w_glu, ret_w_in, ret_gn_g, ret_w_o, ffn_w_up, ffn_conv_w, ffn_conv_b, ffn_w_down, loss_target, m_meta_tokens, m_norm_mix_g, m_norm_ffn_g, m_mla_w_down, m_mla_cq_norm_g, m_mla_ckv_norm_g, m_mla_w_uq, m_mla_w_ukv, m_mla_q_head_g, m_mla_k_head_g, m_mla_w_o, m_hgrn_w_in, m_hgrn_lb_logits, m_hgrn_o_norm_g, m_hgrn_w_o, m_s5_lam_re, m_s5_lam_im, m_s5_log_dt, m_s5_b_re, m_s5_b_im, m_s5_c_re, m_s5_c_im, m_s5_d, m_s5_w_glu, m_ret_w_in, m_ret_gn_g, m_ret_w_o, m_ffn_w_up, m_ffn_conv_w, m_ffn_conv_b, m_ffn_w_down, v_meta_tokens, v_norm_mix_g, v_norm_ffn_g, v_mla_w_down, v_mla_cq_norm_g, v_mla_ckv_norm_g, v_mla_w_uq, v_mla_w_ukv, v_mla_q_head_g, v_mla_k_head_g, v_mla_w_o, v_hgrn_w_in, v_hgrn_lb_logits, v_hgrn_o_norm_g, v_hgrn_w_o, v_s5_lam_re, v_s5_lam_im, v_s5_log_dt, v_s5_b_re, v_s5_b_im, v_s5_c_re, v_s5_c_im, v_s5_d, v_s5_w_glu, v_ret_w_in, v_ret_gn_g, v_ret_w_o, v_ffn_w_up, v_ffn_conv_w, v_ffn_conv_b, v_ffn_w_down)
    names = ['x'] + WEIGHTS + ['loss_target'] + ['m_' + n for n in WEIGHTS] + ['v_' + n for n in WEIGHTS]
    A = dict(zip(names, vals))
    q = 2 * lax.axis_index("x") + lax.axis_index("y")

    small_shapes = [A[n].shape for n in SMALL_SHARDED]
    got_small = gather_chips("gather_small", _pack([A[n] for n in SMALL_SHARDED], F32, 8))
    W = {n: A[n] for n in REPLICATED}
    parts_small = [_unpack(got_small[p], small_shapes) for p in range(4)]
    for k, n in enumerate(SMALL_SHARDED):
        W[n] = jnp.concatenate([parts_small[p][k] for p in range(4)], axis=SHARD_AXIS[n])

    ex = Exchange(shards={n: A[n].astype(BF16) for n in BIG})
    loss, grad_x, G = local_step(A['x'][0], A['loss_target'][0], W, ex)
    loss = lax.psum(loss, ("x", "y", "c"))

    ex.swap(name="grad_big_sibling")
    res_big = []
    for n in BIG:
        layers = range(A[n].shape[0])
        res_big.append(adamw_shard("adam_" + n, jnp.stack([ex.received(n, layer) for layer in layers], axis=1),
                                   jnp.stack([ex.sibling(n, layer) for layer in layers], axis=1),
                                   A[n], A['m_' + n], A['v_' + n]))

    small_names = REPLICATED + SMALL_SHARDED
    gs = {}
    for part, names, rider in (("early", SMALL_EARLY, ex.small[0]), ("late", SMALL_LATE, ex.small[1])):
        total = sum_slots("grad_small_sum_" + part, rider.results[0])
        gs.update(zip(names, _unpack(total, [G[n].shape for n in names])))
    for n in SMALL_SHARDED:
        ax = SHARD_AXIS[n]
        size = gs[n].shape[ax] // 4
        gs[n] = lax.dynamic_slice_in_dim(gs[n], q * size, size, axis=ax)
    pk = lambda pre: _pack([A[pre + n] for n in small_names], F32, 8)
    own_shapes = [A[n].shape for n in small_names]
    res_small = [_unpack(r, own_shapes) for r in
                 adamw("adam_small", [_pack([gs[n] for n in small_names], F32, 8)], pk(''), pk('m_'), pk('v_'))]

    out = {}
    for j, kind in enumerate(('grad_', 'delta_', 'new_m_', 'new_v_')):
        for k, n in enumerate(BIG):
            out[kind + n] = res_big[k][j]
        for k, n in enumerate(small_names):
            out[kind + n] = res_small[j][k]
    return (loss, grad_x[None]) + tuple(out[kind + n] for kind in ('grad_', 'delta_', 'new_m_', 'new_v_')
                                        for n in WEIGHTS)
```

```python
import functools
import math

import jax
import jax.numpy as jnp
from jax import lax
from jax.experimental import pallas as pl
from jax.experimental.pallas import tpu as pltpu

F32, BF16 = jnp.float32, jnp.bfloat16
HIGHEST = lax.Precision.HIGHEST
MESH_ID = pl.DeviceIdType.MESH

D = 1024
N_META = 16
PAD = 112
LEAD = PAD + N_META
EPS = 1e-6
NEG_INF = -1e30
CHUNK = 64
VMEM_LIMIT_V7X = 56 * 1024 * 1024
MM_VMEM_BUDGET = 36 * 1024 * 1024

MLA_H, MLA_NOPE, MLA_ROPE, MLA_V = 8, 128, 64, 128
MLA_QK = MLA_NOPE + MLA_ROPE
MLA_QL, MLA_KVL = 384, 256
HG_H, HG_D, HG_C = 8, 128, 16
S5_G, S5_P, S5_K = 64, 64, 16
RET_H, RET_DK, RET_DV = 4, 256, 512
FFN_F = 2816

ADAM_LR, ADAM_B1, ADAM_B2, ADAM_EPS, ADAM_WD, ADAM_STEP = 0.001, 0.9, 0.999, 1e-08, 0.01, 10

WEIGHTS = ['meta_tokens', 'norm_mix_g', 'norm_ffn_g', 'mla_w_down', 'mla_cq_norm_g', 'mla_ckv_norm_g', 'mla_w_uq',
           'mla_w_ukv', 'mla_q_head_g', 'mla_k_head_g', 'mla_w_o', 'hgrn_w_in', 'hgrn_lb_logits', 'hgrn_o_norm_g',
           'hgrn_w_o', 's5_lam_re', 's5_lam_im', 's5_log_dt', 's5_b_re', 's5_b_im', 's5_c_re', 's5_c_im', 's5_d',
           's5_w_glu', 'ret_w_in', 'ret_gn_g', 'ret_w_o', 'ffn_w_up', 'ffn_conv_w', 'ffn_conv_b', 'ffn_w_down']
SHARD_AXIS = {'meta_tokens': 1, 'mla_w_down': 1, 'mla_w_uq': 2, 'mla_w_ukv': 2, 'mla_w_o': 1, 'hgrn_w_in': 2,
              'hgrn_w_o': 1, 's5_d': 1, 's5_w_glu': 2, 'ret_w_in': 2, 'ret_gn_g': 1, 'ret_w_o': 1, 'ffn_w_up': 2,
              'ffn_conv_w': 2, 'ffn_w_down': 1}
BIG = ['mla_w_down', 'mla_w_uq', 'mla_w_ukv', 'mla_w_o', 'hgrn_w_in', 'hgrn_w_o', 's5_w_glu', 'ret_w_in', 'ret_w_o',
       'ffn_w_up', 'ffn_w_down']
SMALL_SHARDED = ['meta_tokens', 's5_d', 'ret_gn_g', 'ffn_conv_w']
REPLICATED = [n for n in WEIGHTS if n not in SHARD_AXIS]
SMALL_LATE = ['meta_tokens', 'norm_mix_g', 'mla_cq_norm_g', 'mla_ckv_norm_g', 'mla_q_head_g', 'mla_k_head_g']
SMALL_EARLY = [n for n in REPLICATED + SMALL_SHARDED if n not in SMALL_LATE]


def _cparams():
    return pltpu.CompilerParams(vmem_limit_bytes=VMEM_LIMIT_V7X)


def _dg(a, b, ca, cb):
    return lax.dot_general(a.astype(BF16), b.astype(BF16), (((ca,), (cb,)), ((), ())),
                           preferred_element_type=F32)


@jax.custom_vjp
def mm_nn(a, b):
    return _dg(a, b, 1, 0)


@jax.custom_vjp
def mm_nt(a, b):
    return _dg(a, b, 1, 1)


@jax.custom_vjp
def mm_tn(a, b):
    return _dg(a, b, 0, 0)


mm_nn.defvjp(lambda a, b: (mm_nn(a, b), (a, b)),
             lambda r, g: (mm_nt(g, r[1]).astype(r[0].dtype), mm_tn(r[0], g).astype(r[1].dtype)))
mm_nt.defvjp(lambda a, b: (mm_nt(a, b), (a, b)),
             lambda r, g: (mm_nn(g, r[1]).astype(r[0].dtype), mm_tn(g, r[0]).astype(r[1].dtype)))
mm_tn.defvjp(lambda a, b: (mm_tn(a, b), (a, b)),
             lambda r, g: (mm_nt(r[1], g).astype(r[0].dtype), mm_nn(r[0], g).astype(r[1].dtype)))


def _dot_f32(a, b):
    return jnp.dot(a, b, precision=HIGHEST, preferred_element_type=F32)


def _shift_rows(x, s, up):
    n = x.shape[0]
    r = lax.broadcasted_iota(jnp.int32, x.shape, 0)
    if up:
        return jnp.where(r < n - s, pltpu.roll(x, n - s, 0), 0.0)
    return jnp.where(r >= s, pltpu.roll(x, s, 0), 0.0)


@functools.partial(jax.custom_vjp, nondiff_argnums=(1,))
def shift_down(x, s):
    return _shift_rows(x, s, False)


shift_down.defvjp(lambda x, s: (_shift_rows(x, s, False), None), lambda s, _, g: (_shift_rows(g, s, True),))


@functools.partial(jax.custom_vjp, nondiff_argnums=(1,))
def shift_up(x, s):
    return _shift_rows(x, s, True)


shift_up.defvjp(lambda x, s: (_shift_rows(x, s, True), None), lambda s, _, g: (_shift_rows(g, s, False),))


def _swap32_impl(x):
    ax = x.ndim - 1
    lane = lax.broadcasted_iota(jnp.int32, x.shape, ax)
    return jnp.where(lane < 32, pltpu.roll(x, 96, ax), jnp.where(lane < 64, pltpu.roll(x, 32, ax), 0.0))


@jax.custom_vjp
def swap32(x):
    return _swap32_impl(x)


swap32.defvjp(lambda x: (_swap32_impl(x), None), lambda _, g: (_swap32_impl(g),))


def _rms(x, g):
    return x * lax.rsqrt(jnp.mean(x * x, axis=-1, keepdims=True) + EPS) * g


def _silu(x):
    return x * jax.nn.sigmoid(x)


def _row_ids(pid, n, shape, axis=0):
    return pid * n + lax.broadcasted_iota(jnp.int32, shape, axis)


class Arg:
    def __init__(self, arr, block, imap, diff=True, gdtype=F32):
        self.arr, self.block, self.imap, self.diff, self.gdtype = arr, block, imap, diff, gdtype


class Out:
    def __init__(self, shape, dtype, block, imap):
        self.shape, self.dtype, self.block, self.imap = shape, dtype, block, imap


def _free_axes(imap, grid):
    ng = len(grid)
    base = tuple(imap(*([0] * ng)))
    free = []
    for ax in range(ng):
        p = [0] * ng
        p[ax] = 1
        if grid[ax] > 1 and tuple(imap(*p)) == base:
            free.append(ax)
    walked = [ax for ax in range(ng) if grid[ax] > 1]
    assert free == walked[len(walked) - len(free):], "revisited blocks must be revisited on the innermost axes"
    return free


def _pallas(name, body, grid, in_specs, out_specs, out_shape, scratch, operands, rider=None):
    if isinstance(rider, (list, tuple)):
        riders = [r for r in rider if r is not None]
        rider = RiderGroup(riders) if riders else None
    if rider is None:
        return pl.pallas_call(body, grid=grid, in_specs=in_specs, out_specs=out_specs, out_shape=out_shape,
                              scratch_shapes=scratch, name=name, compiler_params=_cparams())(*operands)
    n_in, n_out, n_sc = len(in_specs), len(out_specs), len(scratch)
    r_in, r_out = len(rider.operands), len(rider.out_shapes)

    def body_with_rider(*refs):
        ins, refs = refs[:n_in], refs[n_in:]
        r_ins, refs = refs[:r_in], refs[r_in:]
        outs, refs = refs[:n_out], refs[n_out:]
        r_outs, refs = refs[:r_out], refs[r_out:]
        sc, r_sc = refs[:n_sc], refs[n_sc:]
        pids = [pl.program_id(a) for a in range(len(grid))]
        first = functools.reduce(jnp.logical_and, [p == 0 for p in pids])
        last = functools.reduce(jnp.logical_and, [p == g - 1 for p, g in zip(pids, grid)])

        @pl.when(first)
        def _():
            rider.start(r_ins, r_outs, r_sc)

        if hasattr(rider, 'middle'):
            step = functools.reduce(lambda acc, pg: acc * pg[1] + pg[0], zip(pids, grid), 0)

            @pl.when(step == (math.prod(grid) * 5) // 6)
            def _():
                rider.middle(r_ins, r_outs, r_sc)

        body(*ins, *outs, *sc)

        @pl.when(last)
        def _():
            rider.finish(r_ins, r_outs, r_sc)

    res = pl.pallas_call(body_with_rider, grid=grid, in_specs=list(in_specs) + [ANY] * r_in,
                         out_specs=list(out_specs) + [ANY] * r_out, out_shape=list(out_shape) + rider.out_shapes,
                         scratch_shapes=list(scratch) + rider.scratch, name=name,
                         compiler_params=_cparams())(*operands, *rider.operands)
    rider.results = list(res[n_out:])
    return res[:n_out]


def stage_fwd(name, fn, grid, args, outs, state_shape=None, rider=None):
    n_in, n_out, ng = len(args), len(outs), len(grid)

    def body(*refs):
        pids = tuple(pl.program_id(a) for a in range(ng))
        vals = [r[...] for r in refs[:n_in]]
        o_refs = refs[n_in:n_in + n_out]
        if state_shape is None:
            res = fn(pids, *vals)
        else:
            sv_ref, st_ref = refs[n_in + n_out], refs[n_in + n_out + 1]

            @pl.when(pids[-1] == 0)
            def _():
                st_ref[...] = jnp.zeros(state_shape, F32)

            s = st_ref[...]
            sv_ref[...] = s
            res = fn(pids, *vals, s)
            st_ref[...] = res[-1]
            res = res[:-1]
        for r, v in zip(o_refs, res):
            r[...] = v.astype(r.dtype)

    in_specs = [pl.BlockSpec(a.block, a.imap) for a in args]
    out_specs = [pl.BlockSpec(o.block, o.imap) for o in outs]
    out_shape = [jax.ShapeDtypeStruct(o.shape, o.dtype) for o in outs]
    scratch = []
    if state_shape is not None:
        nz = len(state_shape)
        out_specs.append(pl.BlockSpec((None, None) + tuple(state_shape), lambda i, j: (i, j) + (0,) * nz))
        out_shape.append(jax.ShapeDtypeStruct(tuple(grid) + tuple(state_shape), F32))
        scratch = [pltpu.VMEM(state_shape, F32)]
    return _pallas(name, body, grid, in_specs, out_specs, out_shape, scratch, [a.arr for a in args], rider)


def stage_bwd(name, fn, grid, args, outs, cots, state_shape=None, states=None, rider=None):
    n_in, n_out, ng = len(args), len(outs), len(grid)
    nb = grid[-1]
    rev = state_shape is not None
    didx = [k for k, a in enumerate(args) if a.diff]
    frees = [_free_axes(args[k].imap, grid) for k in didx]

    def eff(p):
        return tuple(p[:-1]) + (nb - 1 - p[-1],) if rev else tuple(p)

    def wrap(imap):
        return lambda *p: imap(*eff(p))

    def body(*refs):
        pids = tuple(pl.program_id(a) for a in range(ng))
        e = eff(pids)
        vals = [r[...] for r in refs[:n_in]]
        cts = tuple(r[...].astype(F32) for r in refs[n_in:n_in + n_out])
        pos = n_in + n_out
        if rev:
            st_in_ref = refs[pos]
            pos += 1
        g_refs = refs[pos:pos + len(didx)]
        pos += len(didx)
        dvals = [vals[k].astype(F32) for k in didx]

        def f(*dv):
            full = list(vals)
            for k, v in zip(didx, dv[:len(didx)]):
                full[k] = v
            return tuple(fn(e, *full, *dv[len(didx):]))

        if rev:
            ds_ref = refs[pos]

            @pl.when(pids[-1] == 0)
            def _():
                ds_ref[...] = jnp.zeros(state_shape, F32)

            _, vjp = jax.vjp(f, *dvals, st_in_ref[...])
            grads = vjp(cts + (ds_ref[...],))
            ds_ref[...] = grads[-1]
            grads = grads[:-1]
        else:
            _, vjp = jax.vjp(f, *dvals)
            grads = vjp(cts)
        for gref, g, free in zip(g_refs, grads, frees):
            g = g.astype(F32)
            if not free:
                gref[...] = g.astype(gref.dtype)
            else:
                first = functools.reduce(jnp.logical_and, [pids[ax] == 0 for ax in free])

                @pl.when(first)
                def _():
                    gref[...] = g

                @pl.when(jnp.logical_not(first))
                def _():
                    gref[...] += g

    in_specs = [pl.BlockSpec(a.block, wrap(a.imap)) for a in args]
    in_specs += [pl.BlockSpec(o.block, wrap(o.imap)) for o in outs]
    operands = [a.arr for a in args] + list(cots)
    scratch = []
    if rev:
        nz = len(state_shape)
        in_specs.append(pl.BlockSpec((None, None) + tuple(state_shape), lambda i, j: (i, nb - 1 - j) + (0,) * nz))
        operands.append(states)
        scratch = [pltpu.VMEM(state_shape, F32)]
    out_specs = [pl.BlockSpec(args[k].block, wrap(args[k].imap)) for k in didx]
    assert all(args[k].gdtype == F32 or not free for k, free in zip(didx, frees))
    out_shape = [jax.ShapeDtypeStruct(args[k].arr.shape, args[k].gdtype) for k in didx]
    return _pallas(name, body, grid, in_specs, out_specs, out_shape, scratch, operands, rider)


def _divisors(n, cands):
    return [c for c in cands if n % c == 0] or [n]


def _nbytes(dt):
    return jnp.dtype(dt).itemsize


def matmul(name, a, b, mode, out_dtype=F32, res=None, mask=False, res_mask=True, window=None, into=None):
    sa, sb, so = _nbytes(a.dtype), _nbytes(b.dtype), _nbytes(out_dtype)
    off, width = (window[0], window[1]) if window is not None else (0, None)
    if mode in ('nn', 'nt'):
        M, K = a.shape
        N = (width or b.shape[1]) if mode == 'nn' else b.shape[0]
        assert mode == 'nn' or width is None or width == K
        best = None
        for tm in _divisors(M, (1408, 1056, 768, 384, 128)):
            for tn in _divisors(N, (1408, 1024, 768, 512, 384, 256, 128)):
                est = 2 * (tm * K * sa + tn * K * sb + tm * tn * (so + (4 if res is not None else 0)))
                if est <= MM_VMEM_BUDGET and (best is None or tm * tn > best[0] * best[1]):
                    best = (tm, tn)
        tm, tn = best
        grid = (M // tm, N // tn)

        def body(*refs):
            a_ref, b_ref = refs[0], refs[1]
            o_ref = refs[-1]
            x = a_ref[...]
            rows = _row_ids(pl.program_id(0), tm, (tm, 1))
            if mask:
                x = jnp.where(rows >= PAD, x, jnp.zeros_like(x))
            acc = _dg(x, b_ref[...], 1, 0 if mode == 'nn' else 1)
            if res is not None:
                acc = refs[2][...] + (jnp.where(rows >= PAD, acc, 0.0) if res_mask else acc)
            o_ref[...] = acc.astype(o_ref.dtype)

        assert off % (tn if mode == 'nn' else K) == 0
        cb, kb = off // tn, off // K
        in_specs = [pl.BlockSpec((tm, K), lambda i, j: (i, 0)),
                    pl.BlockSpec((K, tn), lambda i, j: (0, j + cb)) if mode == 'nn' else
                    pl.BlockSpec((tn, K), lambda i, j: (j, kb))]
        ops = [a, b]
        if res is not None:
            in_specs.append(pl.BlockSpec((tm, tn), lambda i, j: (i, j)))
            ops.append(res)
        return pl.pallas_call(body, grid=grid, in_specs=in_specs,
                              out_specs=pl.BlockSpec((tm, tn), lambda i, j: (i, j)),
                              out_shape=jax.ShapeDtypeStruct((M, N), out_dtype), name=name,
                              compiler_params=_cparams())(*ops)
    assert mode == 'tn' and res is None
    M, K = a.shape
    N = b.shape[1]
    best = None
    for tk in _divisors(K, (1408, 1024, 768, 512, 384, 256, 128)):
        for tn in _divisors(N, (1408, 1024, 768, 512, 384, 256, 128)):
            est = 2 * (M * tk * sa + M * tn * sb + tk * tn * so)
            if est <= MM_VMEM_BUDGET and (best is None or tk * tn > best[0] * best[1]):
                best = (tk, tn)
    tk, tn = best

    def body_t(*refs):
        a_ref, b_ref, o_ref = refs[0], refs[1], refs[-1]
        y = b_ref[...]
        if mask:
            rows = lax.broadcasted_iota(jnp.int32, (M, 1), 0)
            y = jnp.where(rows >= PAD, y, jnp.zeros_like(y))
        o_ref[...] = _dg(a_ref[...], y, 0, 0).astype(o_ref.dtype)

    assert off % tn == 0
    cb = off // tn
    total = window[2] if window is not None else N
    in_specs = [pl.BlockSpec((M, tk), lambda i, j: (0, i)), pl.BlockSpec((M, tn), lambda i, j: (0, j))]
    ops, alias = [a, b], {}
    if into is not None:
        in_specs.append(ANY)
        ops.append(into)
        alias = {2: 0}
    return pl.pallas_call(body_t, grid=(K // tk, N // tn), in_specs=in_specs,
                          out_specs=pl.BlockSpec((tk, tn), lambda i, j: (i, j + cb)),
                          out_shape=jax.ShapeDtypeStruct((K, total), out_dtype), name=name,
                          input_output_aliases=alias, compiler_params=_cparams())(*ops)


def linear_bwd(name, act, w, dy, mask=False, da_dtype=F32):
    return (matmul(name + "_da", dy, w, 'nt', out_dtype=da_dtype, mask=mask),
            matmul(name + "_dw", act, dy, 'tn', out_dtype=BF16, mask=mask))


def _row_tile(T):
    return _divisors(T, (384, 128))[0]


def _rows(arr, tm, gdtype=F32):
    return Arg(arr, (tm, arr.shape[1]), lambda i: (i, 0), gdtype=gdtype)


def _const(arr, diff=True):
    return Arg(arr, arr.shape, lambda *p: (0,) * arr.ndim, diff)


def _norm_fn(pids, h, g):
    return (_rms(h, g),)


def _norm_bwd_fn(pids, h, g):
    return (_rms(h, g), h)


def norm_fwd(name, h, g, dtype):
    T = h.shape[0]
    tm = _row_tile(T)
    return stage_fwd(name, _norm_fn, (T // tm,), [_rows(h, tm), _const(g)],
                     [Out((T, D), dtype, (tm, D), lambda i: (i, 0))])[0]


def norm_bwd(name, h, g, da, dh):
    T = h.shape[0]
    tm = _row_tile(T)
    o = Out((T, D), F32, (tm, D), lambda i: (i, 0))
    return stage_bwd(name, _norm_bwd_fn, (T // tm,), [_rows(h, tm), _const(g)], [o, o], [da, dh])


def _causal_conv3(u, cw, cb):
    return cw[2:3] * u + cw[1:2] * shift_down(u, 1) + cw[0:1] * shift_down(u, 2) + cb


def _ffn_act_fn(pids, ug, uv, cwg, cwv, cbg, cbv):
    return (_silu(_causal_conv3(ug, cwg, cbg)) * _causal_conv3(uv, cwv, cbv),)


def _ffn_act_args(ug, uv, cw, cb):
    T = ug.shape[0]
    col = lambda j: (0, j)
    args = [Arg(ug, (T, 128), col, gdtype=BF16), Arg(uv, (T, 128), col, gdtype=BF16),
            Arg(cw[:, :FFN_F], (3, 128), col), Arg(cw[:, FFN_F:], (3, 128), col),
            Arg(cb[:, :FFN_F], (1, 128), col), Arg(cb[:, FFN_F:], (1, 128), col)]
    outs = [Out((T, FFN_F), BF16, (T, 128), col)]
    return (FFN_F // 128,), args, outs


def _interleave_cols(w, n_parts, tile=128):
    lead = w.shape[:-1]
    n = w.shape[-1] // (n_parts * tile)
    k = len(lead)
    return w.reshape(lead + (n_parts, n, tile)).transpose(tuple(range(k)) + (k + 1, k, k + 2)).reshape(w.shape)


def _deinterleave_cols(w, n_parts, tile=128):
    lead = w.shape[:-1]
    n = w.shape[-1] // (n_parts * tile)
    k = len(lead)
    return w.reshape(lead + (n, n_parts, tile)).transpose(tuple(range(k)) + (k + 1, k, k + 2)).reshape(w.shape)


def ffn_layer(i, h, g, w_up, cw, cb, w_down):
    gate_w, val_w = (0, FFN_F, 2 * FFN_F), (FFN_F, FFN_F, 2 * FFN_F)
    b = norm_fwd(f"ffn{i}_norm", h, g, BF16)
    ug = matmul(f"ffn{i}_up_g", b, w_up, 'nn', out_dtype=BF16, window=gate_w)
    uv = matmul(f"ffn{i}_up_v", b, w_up, 'nn', out_dtype=BF16, window=val_w)
    grid, args, outs = _ffn_act_args(ug, uv, cw, cb)
    p = stage_fwd(f"ffn{i}_act", _ffn_act_fn, grid, args, outs)[0]
    h_new = matmul(f"ffn{i}_down", p, w_down, 'nn', res=h)

    def bwd(dh):
        dp, dwd = linear_bwd(f"ffn{i}_down_b", p, w_down, dh, mask=True, da_dtype=BF16)
        dug, duv, dcwg, dcwv, dcbg, dcbv = stage_bwd(f"ffn{i}_act_b", _ffn_act_fn, grid, args, outs, [dp])
        db = matmul(f"ffn{i}_up_b_da_g", dug, w_up, 'nt', window=gate_w)
        db = matmul(f"ffn{i}_up_b_da_v", duv, w_up, 'nt', window=val_w, res=db, res_mask=False)
        dwu = matmul(f"ffn{i}_up_b_dw_g", b, dug, 'tn', out_dtype=BF16, window=gate_w)
        dwu = matmul(f"ffn{i}_up_b_dw_v", b, duv, 'tn', out_dtype=BF16, window=val_w, into=dwu)
        dh2, dg = norm_bwd(f"ffn{i}_norm_b", h, g, db, dh)
        return dh2, dict(g=dg, w_up=dwu, cw=jnp.concatenate([dcwg, dcwv], axis=1),
                         cb=jnp.concatenate([dcbg, dcbv], axis=1), w_down=dwd)

    return h_new, bwd


def _mla_latent_fn(pids, down, gcq, gckv):
    cq = _rms(down[:, :MLA_QL], gcq)
    ckv = _rms(down[:, MLA_QL:MLA_QL + MLA_KVL], gckv)
    return cq, ckv, down[:, MLA_QL + MLA_KVL:]


def _rope64(x, cos, sin_signed):
    return x * cos + swap32(x) * sin_signed


def _mla_heads_fn(pids, qraw, kv, kpe, gqn, gqr, gkn, gkr, cos, sin_signed):
    qn, qr = qraw[:, :128], qraw[:, 128:]
    rq = lax.rsqrt((jnp.sum(qn * qn, -1, keepdims=True) + jnp.sum(qr * qr, -1, keepdims=True)) / MLA_QK + EPS)
    q = jnp.concatenate([qn * rq * gqn, _rope64(qr * rq * gqr, cos, sin_signed)], axis=1)
    kn, v = kv[:, :128], kv[:, 128:]
    rk = lax.rsqrt((jnp.sum(kn * kn, -1, keepdims=True) + jnp.sum(kpe * kpe, -1, keepdims=True)) / MLA_QK + EPS)
    k = jnp.concatenate([kn * rk * gkn, _rope64(kpe * rk * gkr, cos, sin_signed)], axis=1)
    return q, k, v


def _chunk_id(r):
    return jnp.where(r < LEAD, 0, 1 + lax.shift_right_arithmetic(r - LEAD, 6))


ATT_T = 384
ATT_SCALE = MLA_QK ** -0.5


def _att_mask(s, qb, kb):
    qrow = _row_ids(qb, ATT_T, (ATT_T, 1))
    krow = _row_ids(kb, ATT_T, (1, ATT_T), axis=1)
    ok = jnp.logical_and(_chunk_id(krow) <= _chunk_id(qrow), krow >= PAD)
    return jnp.where(ok, s, NEG_INF)


def _att_scores(q, k_ref, qb, kb, masked):
    ks = k_ref[pl.ds(pl.multiple_of(kb * ATT_T, ATT_T), ATT_T), :]
    s = _dg(q, ks, 1, 1) * ATT_SCALE
    return (_att_mask(s, qb, kb) if masked else s), ks


def _att_key_loop(j, step, init):
    carry = step(0, init, True)
    carry = lax.fori_loop(1, j, lambda kb, c: step(kb, c, False), carry)
    return lax.cond(j > 0, lambda c: step(j, c, True), lambda c: c, carry)


def _att_rows(ref, b):
    return ref[pl.ds(pl.multiple_of(b * ATT_T, ATT_T), ATT_T), :]


def attention_fwd(q, k, v, rider=None):
    H, T, _ = q.shape
    nq = T // ATT_T

    def body(q_ref, k_ref, v_ref, o_ref, lse_ref):
        j = pl.program_id(1)
        qv = q_ref[...]

        def step(kb, carry, masked):
            m, l, acc = carry
            s, _ = _att_scores(qv, k_ref, j, kb, masked)
            m_new = jnp.maximum(m, jnp.max(s, axis=-1, keepdims=True))
            p = jnp.exp(s - m_new)
            alpha = jnp.exp(m - m_new)
            return (m_new, alpha * l + jnp.sum(p, axis=-1, keepdims=True),
                    alpha * acc + _dg(p, _att_rows(v_ref, kb), 1, 0))

        init = (jnp.full((ATT_T, 1), NEG_INF, F32), jnp.zeros((ATT_T, 1), F32), jnp.zeros((ATT_T, MLA_V), F32))
        m, l, acc = _att_key_loop(j, step, init)
        o_ref[...] = acc / l
        lse_ref[...] = m + jnp.log(l)

    return _pallas("mla_attn", body, (H, nq),
                   [pl.BlockSpec((None, ATT_T, 256), lambda hh, j: (hh, j, 0)),
                    pl.BlockSpec((None, T, 256), lambda hh, j: (hh, 0, 0)),
                    pl.BlockSpec((None, T, MLA_V), lambda hh, j: (hh, 0, 0))],
                   [pl.BlockSpec((ATT_T, MLA_V), lambda hh, j: (j, hh)),
                    pl.BlockSpec((None, ATT_T, 1), lambda hh, j: (hh, j, 0))],
                   [jax.ShapeDtypeStruct((T, H * MLA_V), F32), jax.ShapeDtypeStruct((H, T, 1), F32)], [],
                   [q, k, v], rider)


def attention_bwd(q, k, v, o, lse, do, rider=None, rider_dq=None):
    H, T, _ = q.shape
    nq = T // ATT_T

    def dq_body(q_ref, k_ref, v_ref, o_ref, do_ref, lse_ref, dq_ref, delta_ref):
        j = pl.program_id(1)
        qv, dov, lsev = q_ref[...], do_ref[...], lse_ref[...]
        delta = jnp.sum(dov * o_ref[...], axis=-1, keepdims=True)
        delta_ref[...] = delta

        def step(kb, acc, masked):
            s, ks = _att_scores(qv, k_ref, j, kb, masked)
            p = jnp.exp(s - lsev)
            ds = p * (_dg(dov, _att_rows(v_ref, kb), 1, 1) - delta) * ATT_SCALE
            return acc + _dg(ds, ks, 1, 0)

        dq_ref[...] = _att_key_loop(j, step, jnp.zeros((ATT_T, 256), F32))

    q_blk = pl.BlockSpec((None, ATT_T, 256), lambda hh, j: (hh, j, 0))
    k_all = pl.BlockSpec((None, T, 256), lambda hh, j: (hh, 0, 0))
    v_all = pl.BlockSpec((None, T, MLA_V), lambda hh, j: (hh, 0, 0))
    o_blk = pl.BlockSpec((ATT_T, MLA_V), lambda hh, j: (j, hh))
    col_blk = pl.BlockSpec((None, ATT_T, 1), lambda hh, j: (hh, j, 0))
    dq, delta = _pallas("mla_attn_dq", dq_body, (H, nq), [q_blk, k_all, v_all, o_blk, o_blk, col_blk],
                        [q_blk, col_blk],
                        [jax.ShapeDtypeStruct((H, T, 256), F32), jax.ShapeDtypeStruct((H, T, 1), F32)], [],
                        [q, k, v, o, do, lse], rider_dq)

    def dkv_body(q_ref, k_ref, v_ref, do_ref, lse_ref, delta_ref, dk_ref, dv_ref):
        kb = pl.program_id(1)
        kv = k_ref[...]
        vv = v_ref[...]

        def step(qb, carry, masked):
            dk, dv = carry
            qv, dov = _att_rows(q_ref, qb), _att_rows(do_ref, qb)
            s = _dg(qv, kv, 1, 1) * ATT_SCALE
            if masked:
                s = _att_mask(s, qb, kb)
            p = jnp.exp(s - _att_rows(lse_ref, qb))
            ds = p * (_dg(dov, vv, 1, 1) - _att_rows(delta_ref, qb)) * ATT_SCALE
            return dk + _dg(ds, qv, 0, 0), dv + _dg(p, dov, 0, 0)

        carry = step(kb, (jnp.zeros((ATT_T, 256), F32), jnp.zeros((ATT_T, MLA_V), F32)), True)
        dk, dv = lax.cond(kb == 0,
                          lambda c: lax.fori_loop(kb + 1, nq, lambda qb, cc: step(qb, cc, True), c),
                          lambda c: lax.fori_loop(kb + 1, nq, lambda qb, cc: step(qb, cc, False), c), carry)
        dk_ref[...] = dk
        dv_ref[...] = dv

    q_all = pl.BlockSpec((None, T, 256), lambda hh, j: (hh, 0, 0))
    v_blk = pl.BlockSpec((None, ATT_T, MLA_V), lambda hh, j: (hh, j, 0))
    do_all = pl.BlockSpec((T, MLA_V), lambda hh, j: (0, hh))
    col_all = pl.BlockSpec((None, T, 1), lambda hh, j: (hh, 0, 0))
    dk, dv = _pallas("mla_attn_dkv", dkv_body, (H, nq), [q_all, q_blk, v_blk, do_all, col_all, col_all],
                     [q_blk, v_blk],
                     [jax.ShapeDtypeStruct((H, T, 256), F32), jax.ShapeDtypeStruct((H, T, MLA_V), F32)], [],
                     [q, k, v, do, lse, delta], rider)
    return dq, dk, dv


def mla_mixer(h, g, w, tabs, rider=None):
    T = h.shape[0]
    tm = _row_tile(T)
    nt = T // tm
    a = norm_fwd("mla_norm", h, g, BF16)
    down = matmul("mla_down", a, w['w_down'], 'nn')
    lat_args = [_rows(down, tm, BF16), _const(w['gcq']), _const(w['gckv'])]
    lat_outs = [Out((T, MLA_QL), BF16, (tm, MLA_QL), lambda i: (i, 0)),
                Out((T, MLA_KVL), BF16, (tm, MLA_KVL), lambda i: (i, 0)),
                Out((T, 128), F32, (tm, 128), lambda i: (i, 0))]
    cq, ckv, kpe = stage_fwd("mla_latent", _mla_latent_fn, (nt,), lat_args, lat_outs)
    qraw = matmul("mla_uq", cq, w['w_uq'], 'nn')
    kv = matmul("mla_ukv", ckv, w['w_ukv'], 'nn')
    hd_args = [Arg(qraw, (tm, 256), lambda i, hh: (i, hh), gdtype=BF16),
               Arg(kv, (tm, 256), lambda i, hh: (i, hh), gdtype=BF16),
               Arg(kpe, (tm, 128), lambda i, hh: (i, 0)),
               _const(w['gqn']), _const(w['gqr']), _const(w['gkn']), _const(w['gkr']),
               Arg(tabs['cos_a'], (tm, 128), lambda i, hh: (i, 0), False),
               Arg(tabs['sin_a'], (tm, 128), lambda i, hh: (i, 0), False)]
    hd_outs = [Out((MLA_H, T, 256), BF16, (None, tm, 256), lambda i, hh: (hh, i, 0)),
               Out((MLA_H, T, 256), BF16, (None, tm, 256), lambda i, hh: (hh, i, 0)),
               Out((MLA_H, T, 128), BF16, (None, tm, 128), lambda i, hh: (hh, i, 0))]
    q, k, v = stage_fwd("mla_heads", _mla_heads_fn, (nt, MLA_H), hd_args, hd_outs)
    o, lse = attention_fwd(q, k, v, rider=rider)
    h_new = matmul("mla_o", o, w['w_o'], 'nn', res=h)

    def bwd(dh, rider=None, rider_dq=None):
        do, dwo = linear_bwd("mla_o_b", o, w['w_o'], dh, mask=True)
        dq, dk, dv = attention_bwd(q, k, v, o, lse, do, rider=rider, rider_dq=rider_dq)
        dqraw, dkv, dkpe, dgqn, dgqr, dgkn, dgkr = stage_bwd("mla_heads_b", _mla_heads_fn, (nt, MLA_H), hd_args,
                                                             hd_outs, [dq, dk, dv])
        dcq, dwuq = linear_bwd("mla_uq_b", cq, w['w_uq'], dqraw)
        dckv, dwukv = linear_bwd("mla_ukv_b", ckv, w['w_ukv'], dkv)
        ddown, dgcq, dgckv = stage_bwd("mla_latent_b", _mla_latent_fn, (nt,), lat_args, lat_outs, [dcq, dckv, dkpe])
        da, dwdown = linear_bwd("mla_down_b", a, w['w_down'], ddown)
        dh2, dg = norm_bwd("mla_norm_b", h, g, da, dh)
        return dh2, dict(g=dg, w_down=dwdown, gcq=dgcq, gckv=dgckv, w_uq=dwuq, w_ukv=dwukv, gqn=dgqn, gqr=dgqr,
                         gkn=dgkn, gkr=dgkr, w_o=dwo)

    return h_new, bwd


HG_R = 384


def _hgrn_fn(pids, z, lb, go, st):
    outs = []
    for lo in range(0, z.shape[0], 128):
        o, st = _hgrn_block(z[lo:lo + 128], lb, go, st)
        outs.append(o)
    return jnp.concatenate(outs, axis=0), st


def _hgrn_block(z, lb, go, st):
    R = z.shape[0]
    zq, zf, zi, zg = z[:, :128], z[:, 128:256], z[:, 256:384], z[:, 384:]
    assert R == 128
    q = _silu(zq)
    fg = lb + (1.0 - lb) * jax.nn.sigmoid(zf)
    logf = jnp.log(fg)
    k = 1.0 - fg
    row = lax.broadcasted_iota(jnp.int32, logf.shape, 0)
    pos = row & (HG_C - 1)
    cum, rev = logf, logf
    for d in (1, 2, 4, 8):
        cum = cum + jnp.where(pos >= d, shift_down(cum, d), 0.0)
        rev = rev + jnp.where(pos < HG_C - d, shift_up(rev, d), 0.0)
    cums, tots = [cum], [cum + rev - logf]
    for s in (16, 32, 64):
        odd = (row & s) != 0
        before = shift_down(tots[-1], s)
        cums.append(cums[-1] + jnp.where(odd, before, 0.0))
        tots.append(tots[-1] + jnp.where(odd, before, shift_up(tots[-1], s)))
    t = lax.broadcasted_iota(jnp.int32, (R, R), 0)
    j = lax.broadcasted_iota(jnp.int32, (R, R), 1)
    sh = lax.shift_right_arithmetic
    a = jnp.where(jnp.logical_and(sh(t, 4) == sh(j, 4), j <= t), mm_nt(q * jnp.exp(cum), k * jnp.exp(-cum)), 0.0)
    for n, s in enumerate((16, 32, 64)):
        m = jnp.logical_and(sh(t, 5 + n) == sh(j, 5 + n), jnp.logical_and((t & s) != 0, (j & s) == 0))
        a = a + jnp.where(m, mm_nt(q * jnp.exp(cums[n]), k * jnp.exp(tots[n] - cums[n])), 0.0)
    o = mm_nn(a, zi) + mm_nt(q * jnp.exp(cums[3]), st)
    st = st * jnp.exp(tots[3][0:1, :]) + mm_tn(zi, k * jnp.exp(tots[3] - cums[3]))
    return _rms(o, go) * _silu(zg), st


def hgrn_mixer(h, g, w, rider=None):
    T = h.shape[0]
    a = norm_fwd("hgrn_norm", h, g, BF16)
    z = matmul("hgrn_in", a, w['w_in'], 'nn')
    grid = (HG_H, T // HG_R)
    args = [Arg(z, (HG_R, 512), lambda hh, j: (j, hh), gdtype=BF16), Arg(w['lb'], (1, 128), lambda hh, j: (0, hh)),
            _const(w['go'])]
    outs = [Out((T, D), BF16, (HG_R, 128), lambda hh, j: (j, hh))]
    o, states = stage_fwd("hgrn_gla", _hgrn_fn, grid, args, outs, state_shape=(HG_D, HG_D), rider=rider)
    h_new = matmul("hgrn_o", o, w['w_o'], 'nn', res=h)

    def bwd(dh, rider=None):
        do, dwo = linear_bwd("hgrn_o_b", o, w['w_o'], dh, mask=True)
        dz, dlb, dgo = stage_bwd("hgrn_gla_b", _hgrn_fn, grid, args, outs, [do], state_shape=(HG_D, HG_D),
                                 states=states, rider=rider)
        da, dwin = linear_bwd("hgrn_in_b", a, w['w_in'], dz)
        dh2, dg = norm_bwd("hgrn_norm_b", h, g, da, dh)
        return dh2, dict(g=dg, w_in=dwin, lb=dlb, go=dgo, w_o=dwo)

    return h_new, bwd


S5_R = 384
S5_W = 512
S5_SLABS = D // 128


def _cmul(ar, ai, br, bi):
    return ar * br - ai * bi, ar * bi + ai * br


def _s5_scan(br, bi, tab, cr, ci, reverse):
    R, W = br.shape
    G = R // 8
    xr, xi = br.reshape(G, 8, W), bi.reshape(G, 8, W)
    for n, d in enumerate((1, 2, 4)):
        sh = (8 - d) if reverse else d
        mr, mi = _cmul(tab[2 * n][None], tab[2 * n + 1][None], pltpu.roll(xr, sh, 1), pltpu.roll(xi, sh, 1))
        xr, xi = xr + mr, xi + mi
    pr, pi = tab[6], tab[7]
    edge = 0 if reverse else 7
    out_r, out_i = [None] * G, [None] * G
    for g in (range(G - 1, -1, -1) if reverse else range(G)):
        ar, ai = _cmul(pr, pi, cr, ci)
        gr, gi = xr[g] + ar, xi[g] + ai
        cr, ci = gr[edge:edge + 1], gi[edge:edge + 1]
        out_r[g], out_i[g] = gr, gi
    return jnp.concatenate(out_r, axis=0), jnp.concatenate(out_i, axis=0), cr, ci


def s5_scan_fwd(a, bb, cb, tab, rider=None):
    T = a.shape[0]
    nb = T // S5_R

    def body(a_ref, bb_ref, cb_ref, tab_ref, y_ref, xs_ref, c_ref):
        @pl.when(pl.program_id(1) == 0)
        def _():
            c_ref[...] = jnp.zeros(c_ref.shape, F32)

        bu = _dg(a_ref[...], bb_ref[...], 1, 0)
        t = tab_ref[...]
        xr, xi, cr, ci = _s5_scan(bu[:, :S5_W], bu[:, S5_W:], t, c_ref[0:1, :S5_W], c_ref[0:1, S5_W:], False)
        x = jnp.concatenate([xr, xi], axis=1)
        xs_ref[...] = x
        y_ref[...] = _dg(x, cb_ref[...], 1, 0)
        c_ref[0:1, :] = jnp.concatenate([cr, ci], axis=1)

    return _pallas(
        "s5_scan", body, (S5_SLABS, nb),
        [pl.BlockSpec((S5_R, 128), lambda j, i: (i, j)),
         pl.BlockSpec((None, 128, 2 * S5_W), lambda j, i: (j, 0, 0)),
         pl.BlockSpec((None, 2 * S5_W, 128), lambda j, i: (j, 0, 0)),
         pl.BlockSpec((None, 10, 8, S5_W), lambda j, i: (j, 0, 0, 0))],
        [pl.BlockSpec((S5_R, 128), lambda j, i: (i, j)),
         pl.BlockSpec((None, S5_R, 2 * S5_W), lambda j, i: (j, i, 0))],
        [jax.ShapeDtypeStruct((T, D), F32), jax.ShapeDtypeStruct((S5_SLABS, T, 2 * S5_W), F32)],
        [pltpu.VMEM((8, 2 * S5_W), F32)], [a, bb, cb, tab], rider)


def s5_scan_bwd(a, bb, cb, tab_rev, xs, dy, rider=None):
    T = a.shape[0]
    nb = T // S5_R
    rg = S5_R // 8

    def body(a_ref, dy_ref, xs_ref, xp_ref, bb_ref, cb_ref, tab_ref, da_ref, dbb_ref, dcb_ref, dab_ref, c_ref):
        i = pl.program_id(1)

        @pl.when(i == 0)
        def _():
            c_ref[...] = jnp.zeros(c_ref.shape, F32)

        dy_v = dy_ref[...]
        x = xs_ref[...]
        dxo = _dg(dy_v, cb_ref[...], 1, 1)
        gr, gi, cr, ci = _s5_scan(dxo[:, :S5_W], dxo[:, S5_W:], tab_ref[...], c_ref[0:1, :S5_W], c_ref[0:1, S5_W:], True)
        c_ref[0:1, :] = jnp.concatenate([cr, ci], axis=1)
        g = jnp.concatenate([gr, gi], axis=1)
        da_ref[...] = _dg(g, bb_ref[...], 1, 1)
        dbb = _dg(a_ref[...], g, 0, 0)
        dcb = _dg(x, dy_v, 0, 0)
        first_tile = i == nb - 1
        prev_last = jnp.where(first_tile, 0.0, xp_ref[7:8, :])
        rows = lax.broadcasted_iota(jnp.int32, x.shape, 0)
        xp = jnp.where(rows == 0, prev_last, pltpu.roll(x, 1, 0))
        xpr, xpi = xp[:, :S5_W], xp[:, S5_W:]
        dar = (gr * xpr + gi * xpi).reshape(rg, 8, S5_W).sum(axis=0)
        dai = (gi * xpr - gr * xpi).reshape(rg, 8, S5_W).sum(axis=0)
        dab = jnp.concatenate([dar, dai], axis=1)

        @pl.when(i == 0)
        def _():
            dbb_ref[...] = dbb
            dcb_ref[...] = dcb
            dab_ref[...] = dab

        @pl.when(i != 0)
        def _():
            dbb_ref[...] += dbb
            dcb_ref[...] += dcb
            dab_ref[...] += dab

    def prev_rows(j, i):
        return (j, jnp.maximum((nb - 1 - i) * rg - 1, 0), 0)

    return _pallas(
        "s5_scan_b", body, (S5_SLABS, nb),
        [pl.BlockSpec((S5_R, 128), lambda j, i: (nb - 1 - i, j)),
         pl.BlockSpec((S5_R, 128), lambda j, i: (nb - 1 - i, j)),
         pl.BlockSpec((None, S5_R, 2 * S5_W), lambda j, i: (j, nb - 1 - i, 0)),
         pl.BlockSpec((None, 8, 2 * S5_W), prev_rows),
         pl.BlockSpec((None, 128, 2 * S5_W), lambda j, i: (j, 0, 0)),
         pl.BlockSpec((None, 2 * S5_W, 128), lambda j, i: (j, 0, 0)),
         pl.BlockSpec((None, 10, 8, S5_W), lambda j, i: (j, 0, 0, 0))],
        [pl.BlockSpec((S5_R, 128), lambda j, i: (nb - 1 - i, j)),
         pl.BlockSpec((None, 128, 2 * S5_W), lambda j, i: (j, 0, 0)),
         pl.BlockSpec((None, 2 * S5_W, 128), lambda j, i: (j, 0, 0)),
         pl.BlockSpec((None, 8, 2 * S5_W), lambda j, i: (j, 0, 0))],
        [jax.ShapeDtypeStruct((T, D), F32), jax.ShapeDtypeStruct((S5_SLABS, 128, 2 * S5_W), F32),
         jax.ShapeDtypeStruct((S5_SLABS, 2 * S5_W, 128), F32), jax.ShapeDtypeStruct((S5_SLABS, 8, 2 * S5_W), F32)],
        [pltpu.VMEM((8, 2 * S5_W), F32)], [a, dy, xs, xs, bb, cb, tab_rev], rider)


def _s5_discretise(lam_re, lam_im, log_dt, b_re, b_im, c_re, c_im):
    dt = jnp.exp(log_dt)[:, None]
    mag = jnp.exp(lam_re * dt)
    abar_re = mag * jnp.cos(lam_im * dt)
    abar_im = mag * jnp.sin(lam_im * dt)
    den = lam_re * lam_re + lam_im * lam_im
    zoh_re = ((abar_re - 1.0) * lam_re + abar_im * lam_im) / den
    zoh_im = (abar_im * lam_re - (abar_re - 1.0) * lam_im) / den
    bbar_re = zoh_re[..., None] * b_re - zoh_im[..., None] * b_im
    bbar_im = zoh_re[..., None] * b_im + zoh_im[..., None] * b_re
    eye = jnp.eye(8, dtype=F32)

    def in_map(bbar):
        t = bbar.reshape(8, 8, S5_P, S5_K).transpose(0, 1, 3, 2)
        return (t[:, :, :, None, :] * eye[None, :, None, :, None]).reshape(8, 8 * S5_K, 8 * S5_P)

    def out_map(c):
        t = c.reshape(8, 8, S5_K, S5_P).transpose(0, 1, 3, 2)
        return (t[:, :, :, None, :] * eye[None, :, None, :, None]).reshape(8, 8 * S5_P, 8 * S5_K)

    bb = jnp.concatenate([in_map(bbar_re), in_map(bbar_im)], axis=2)
    cb = jnp.concatenate([out_map(c_re), -out_map(c_im)], axis=1)
    return bb, cb, abar_re.reshape(8, S5_W), abar_im.reshape(8, S5_W)


def _s5_tables(ar, ai, reverse):
    if reverse:
        ai = -ai
    pw = [(jnp.ones_like(ar), jnp.zeros_like(ar))]
    for _ in range(8):
        pw.append(_cmul(pw[-1][0], pw[-1][1], ar, ai))
    r = jnp.arange(8)[None, :, None]
    rows = []
    for d in (1, 2, 4):
        keep = (r <= 7 - d) if reverse else (r >= d)
        rows += [jnp.where(keep, pw[d][0][:, None, :], 0.0), jnp.where(keep, pw[d][1][:, None, :], 0.0)]
    order = [8 - k for k in range(8)] if reverse else [k + 1 for k in range(8)]
    rows += [jnp.stack([pw[n][0] for n in order], axis=1), jnp.stack([pw[n][1] for n in order], axis=1)]
    rows += [jnp.broadcast_to(pw[8][0][:, None, :], (8, 8, S5_W)), jnp.broadcast_to(pw[8][1][:, None, :], (8, 8, S5_W))]
    return jnp.stack(rows, axis=1)


def _s5_act_fn(pids, yc, a, dskip):
    return (jax.nn.gelu(yc + dskip * a),)


def _make_glu_res_fn(tm):
    def glu_res_fn(pids, zz, h):
        rows = _row_ids(pids[0], tm, (tm, 1))
        return (h + jnp.where(rows >= PAD, zz[:, :D] * jax.nn.sigmoid(zz[:, D:]), 0.0),)
    return glu_res_fn


def s5_mixer(h, g, w, rider=None):
    T = h.shape[0]
    tm = _row_tile(T)
    nt = T // tm
    a = norm_fwd("s5_norm", h, g, F32)
    ssm = [w[n] for n in ('lam_re', 'lam_im', 'log_dt', 'b_re', 'b_im', 'c_re', 'c_im')]
    (bb, cb, ar, ai), disc_vjp = jax.vjp(_s5_discretise, *ssm)
    yc, xs = s5_scan_fwd(a, bb, cb, _s5_tables(ar, ai, False), rider=rider)
    row = lambda arr: _rows(arr, tm)
    act_args = [row(yc), row(a), _const(w['dskip'])]
    act_outs = [Out((T, D), BF16, (tm, D), lambda i: (i, 0))]
    y = stage_fwd("s5_act", _s5_act_fn, (nt,), act_args, act_outs)[0]
    zz = matmul("s5_glu", y, w['w_glu'], 'nn')
    glu_fn = _make_glu_res_fn(tm)
    glu_args = [_rows(zz, tm, BF16), row(h)]
    glu_outs = [Out((T, D), F32, (tm, D), lambda i: (i, 0))]
    h_new = stage_fwd("s5_gate", glu_fn, (nt,), glu_args, glu_outs)[0]

    def bwd(dh, rider=None):
        dzz, dh_res = stage_bwd("s5_gate_b", glu_fn, (nt,), glu_args, glu_outs, [dh])
        dy, dwglu = linear_bwd("s5_glu_b", y, w['w_glu'], dzz)
        dyc, da1, ddskip = stage_bwd("s5_act_b", _s5_act_fn, (nt,), act_args, act_outs, [dy])
        da2, dbb, dcb, dab = s5_scan_bwd(a, bb, cb, _s5_tables(ar, ai, True), xs, dyc, rider=rider)
        dab = dab.sum(axis=1)
        dssm = disc_vjp((dbb, dcb, dab[:, :S5_W], dab[:, S5_W:]))
        dh2, dg = _s5_norm_bwd(h, g, da1, da2, dh_res, tm)
        grads = dict(zip(('lam_re', 'lam_im', 'log_dt', 'b_re', 'b_im', 'c_re', 'c_im'), dssm))
        grads.update(g=dg, dskip=ddskip, w_glu=dwglu)
        return dh2, grads

    return h_new, bwd


def _norm3_bwd_fn(pids, h, g):
    a = _rms(h, g)
    return a, a, h


def _s5_norm_bwd(h, g, da1, da2, dh, tm):
    T = h.shape[0]
    o = Out((T, D), F32, (tm, D), lambda i: (i, 0))
    return stage_bwd("s5_norm_b", _norm3_bwd_fn, (T // tm,), [_rows(h, tm), _const(g)], [o, o, o], [da1, da2, dh])


RET_R = 384


def _rope256(x, cos, sin):
    x1, x2 = x[:, :128], x[:, 128:]
    return jnp.concatenate([x1 * cos - x2 * sin, x1 * sin + x2 * cos], axis=1)


def _ret_fn(pids, z, gn, cos, sin, dmat, qdec, kdec, cdec, st):
    R = z.shape[0]
    q = _rope256(z[:, :256], cos, sin)
    k = _rope256(z[:, 256:512], cos, sin) * (RET_DK ** -0.5)
    v, gate = z[:, 512:1024], z[:, 1024:]
    outs = []
    for cc in range(R // CHUNK):
        lo = cc * CHUNK
        qc, kc, vc = q[lo:lo + CHUNK], k[lo:lo + CHUNK], v[lo:lo + CHUNK]
        outs.append(mm_nn(mm_nt(qc, kc) * dmat, vc) + mm_nn(qc * qdec, st))
        st = st * cdec + mm_tn(kc * kdec, vc)
    o = jnp.concatenate(outs, axis=0)
    mu = jnp.mean(o, axis=-1, keepdims=True)
    var = jnp.mean(jnp.square(o - mu), axis=-1, keepdims=True)
    o = (o - mu) * lax.rsqrt(var + EPS)
    return o * gn * _silu(gate), st


def ret_mixer(h, g, w, tabs, rider=None):
    T = h.shape[0]
    a = norm_fwd("ret_norm", h, g, BF16)
    z = matmul("ret_in", a, w['w_in'], 'nn')
    grid = (RET_H, T // RET_R)
    hw = RET_DK * 2 + RET_DV * 2
    args = [Arg(z, (RET_R, hw), lambda hh, j: (j, hh), gdtype=BF16), Arg(w['gn'], (1, RET_DV), lambda hh, j: (0, hh)),
            Arg(tabs['cos_d'], (RET_R, 128), lambda hh, j: (j, 0), False),
            Arg(tabs['sin_d'], (RET_R, 128), lambda hh, j: (j, 0), False),
            Arg(tabs['ret_dmat'], (None, CHUNK, CHUNK), lambda hh, j: (hh, 0, 0), False),
            Arg(tabs['ret_qdec'], (None, CHUNK, 1), lambda hh, j: (hh, 0, 0), False),
            Arg(tabs['ret_kdec'], (None, CHUNK, 1), lambda hh, j: (hh, 0, 0), False),
            Arg(tabs['ret_cdec'], (None, 1, 1), lambda hh, j: (hh, 0, 0), False)]
    outs = [Out((T, RET_H * RET_DV), BF16, (RET_R, RET_DV), lambda hh, j: (j, hh))]
    o, states = stage_fwd("ret_chunks", _ret_fn, grid, args, outs, state_shape=(RET_DK, RET_DV), rider=rider)
    h_new = matmul("ret_o", o, w['w_o'], 'nn', res=h)

    def bwd(dh, rider=None):
        do, dwo = linear_bwd("ret_o_b", o, w['w_o'], dh, mask=True)
        dz, dgn = stage_bwd("ret_chunks_b", _ret_fn, grid, args, outs, [do], state_shape=(RET_DK, RET_DV),
                            states=states, rider=rider)
        da, dwin = linear_bwd("ret_in_b", a, w['w_in'], dz)
        dh2, dg = norm_bwd("ret_norm_b", h, g, da, dh)
        return dh2, dict(g=dg, w_in=dwin, gn=dgn, w_o=dwo)

    return h_new, bwd


def loss_head(h, tgt):
    T = h.shape[0]
    tm = _row_tile(T)

    def body(h_ref, t_ref, loss_ref, dh_ref):
        i = pl.program_id(0)
        rows = _row_ids(i, tm, (tm, 1))
        err = jnp.where(rows >= LEAD, h_ref[...] - t_ref[...], 0.0)
        dh_ref[...] = err * (1.0 / D)
        part = jnp.full((8, 128), 0.5 * jnp.sum(jnp.sum(err * err, axis=1, keepdims=True) * (1.0 / D)), F32)

        @pl.when(i == 0)
        def _():
            loss_ref[...] = part

        @pl.when(i != 0)
        def _():
            loss_ref[...] += part

    loss, dh = pl.pallas_call(
        body, grid=(T // tm,),
        in_specs=[pl.BlockSpec((tm, D), lambda i: (i, 0)), pl.BlockSpec((tm, D), lambda i: (i, 0))],
        out_specs=[pl.BlockSpec((8, 128), lambda i: (0, 0)), pl.BlockSpec((tm, D), lambda i: (i, 0))],
        out_shape=[jax.ShapeDtypeStruct((8, 128), F32), jax.ShapeDtypeStruct((T, D), F32)], name="loss_head",
        compiler_params=_cparams())(h, tgt)
    return loss[0, 0], dh


def _tables(T):
    pos = jnp.maximum(jnp.arange(T, dtype=jnp.int32) - PAD, 0).astype(F32)

    def cs(dim):
        inv_freq = 1.0 / (10000.0 ** (jnp.arange(0, dim, 2, dtype=F32) / dim))
        ang = pos[:, None] * inv_freq[None, :]
        return jnp.cos(ang), jnp.sin(ang)

    ca, sa = cs(MLA_ROPE)
    zeros = jnp.zeros((T, 64), F32)
    cd, sd = cs(RET_DK)
    log_gamma = jnp.log(1.0 - jnp.exp2(-5.0 - jnp.arange(RET_H, dtype=F32)))
    p = jnp.arange(CHUNK, dtype=F32)
    diff = p[:, None] - p[None, :]
    dmat = jnp.where(diff >= 0, jnp.exp(diff[None] * log_gamma[:, None, None]), 0.0)
    return dict(cos_a=jnp.concatenate([ca, ca, zeros], axis=1), sin_a=jnp.concatenate([-sa, sa, zeros], axis=1),
                cos_d=cd, sin_d=sd, ret_dmat=dmat,
                ret_qdec=jnp.exp((p[None, :] + 1.0) * log_gamma[:, None])[..., None],
                ret_kdec=jnp.exp((CHUNK - 1.0 - p[None, :]) * log_gamma[:, None])[..., None],
                ret_cdec=jnp.exp(CHUNK * log_gamma)[:, None, None])


def _hgrn_lower_bound(logits):
    lb_cum = jnp.cumsum(jax.nn.softmax(logits, axis=0), axis=0)
    return (lb_cum - lb_cum[0:1])[1:2]


def _uq_to_heads(w):
    t = w.reshape(w.shape[0], MLA_H, MLA_QK)
    return jnp.pad(t, ((0, 0), (0, 0), (0, 256 - MLA_QK))).reshape(w.shape[0], MLA_H * 256)


def _uq_from_heads(g):
    return g.reshape(g.shape[0], MLA_H, 256)[:, :, :MLA_QK].reshape(g.shape[0], MLA_H * MLA_QK)


def _head_interleave(w, widths, heads):
    parts, lo = [], 0
    for wd in widths:
        parts.append(w[:, lo:lo + heads * wd].reshape(w.shape[0], heads, wd))
        lo += heads * wd
    return jnp.concatenate(parts, axis=2).reshape(w.shape[0], -1)


def _head_deinterleave(g, widths, heads):
    t = g.reshape(g.shape[0], heads, sum(widths))
    parts, lo = [], 0
    for wd in widths:
        parts.append(t[:, :, lo:lo + wd].reshape(g.shape[0], heads * wd))
        lo += wd
    return jnp.concatenate(parts, axis=1)


HG_WIDTHS = (128, 128, 128, 128)
RET_WIDTHS = (RET_DK, RET_DK, RET_DV, RET_DV)


def _split_head_gain(g):
    return g[:, :128], jnp.pad(g[:, 128:], ((0, 0), (0, 64)))


def _join_head_gain(dn, dr):
    return jnp.concatenate([dn, dr[:, :64]], axis=1)


def local_step(x, target, W, ex):
    S = x.shape[0]
    T = S + LEAD
    tabs = _tables(T)
    h = jnp.concatenate([jnp.zeros((PAD, D), F32), W['meta_tokens'], x], axis=0)
    tgt = jnp.concatenate([jnp.zeros((LEAD, D), F32), target], axis=0)

    gqn, gqr = _split_head_gain(W['mla_q_head_g'])
    gkn, gkr = _split_head_gain(W['mla_k_head_g'])
    lb, lb_vjp = jax.vjp(_hgrn_lower_bound, W['hgrn_lb_logits'])

    def ffn(i, hh):
        return ffn_layer(i, hh, W['norm_ffn_g'][i:i + 1], ex.weight('ffn_w_up', i), W['ffn_conv_w'][i],
                         W['ffn_conv_b'][i:i + 1], ex.weight('ffn_w_down', i))

    bm, bf = [None] * 4, [None] * 4
    ex.gather(['mla'], name="gather_mla")
    w0 = dict(w_down=jnp.pad(ex.weight('mla_w_down'), ((0, 0), (0, 64))), gcq=W['mla_cq_norm_g'],
              gckv=W['mla_ckv_norm_g'], w_uq=_uq_to_heads(ex.weight('mla_w_uq')), w_ukv=ex.weight('mla_w_ukv'),
              gqn=gqn, gqr=gqr, gkn=gkn, gkr=gkr, w_o=ex.weight('mla_w_o'))
    h, bm[0] = mla_mixer(h, W['norm_mix_g'][0:1], w0, tabs, rider=ex.gather(['ffn0', 'hgrn', 'ffn1']))
    h, bf[0] = ffn(0, h)
    w1 = dict(w_in=_head_interleave(ex.weight('hgrn_w_in'), HG_WIDTHS, HG_H), lb=lb, go=W['hgrn_o_norm_g'],
              w_o=ex.weight('hgrn_w_o'))
    h, bm[1] = hgrn_mixer(h, W['norm_mix_g'][1:2], w1, rider=ex.gather(['s5', 'ffn2']))
    h, bf[1] = ffn(1, h)
    w2 = dict(lam_re=W['s5_lam_re'][0], lam_im=W['s5_lam_im'][0], log_dt=W['s5_log_dt'][0], b_re=W['s5_b_re'][0],
              b_im=W['s5_b_im'][0], c_re=W['s5_c_re'][0], c_im=W['s5_c_im'][0], dskip=W['s5_d'],
              w_glu=ex.weight('s5_w_glu'))
    h, bm[2] = s5_mixer(h, W['norm_mix_g'][2:3], w2, rider=ex.gather(['ret']))
    h, bf[2] = ffn(2, h)
    w3 = dict(w_in=_head_interleave(ex.weight('ret_w_in'), RET_WIDTHS, RET_H), gn=W['ret_gn_g'],
              w_o=ex.weight('ret_w_o'))
    h, bm[3] = ret_mixer(h, W['norm_mix_g'][3:4], w3, tabs, rider=ex.gather(['ffn3']))
    h, bf[3] = ffn(3, h)

    loss, dh = loss_head(h, tgt)

    def ffn_grads(i, g):
        return {('ffn_w_up', i): g['w_up'], ('ffn_w_down', i): g['w_down']}

    gm, gf = [None] * 4, [None] * 4
    dh, gf[3] = bf[3](dh)
    dh, gm[3] = bm[3](dh, rider=ex.scatter(ffn_grads(3, gf[3])))
    dh, gf[2] = bf[2](dh)
    ret_grads = {('ret_w_in', 0): _head_deinterleave(gm[3]['w_in'], RET_WIDTHS, RET_H), ('ret_w_o', 0): gm[3]['w_o']}
    dh, gm[2] = bm[2](dh, rider=ex.scatter(ffn_grads(2, gf[2])))
    dh, gf[1] = bf[1](dh)
    dh, gm[1] = bm[1](dh, rider=ex.scatter(ffn_grads(1, gf[1])))
    dh, gf[0] = bf[0](dh)
    hgrn_grads = {('hgrn_w_in', 0): _head_deinterleave(gm[1]['w_in'], HG_WIDTHS, HG_H), ('hgrn_w_o', 0): gm[1]['w_o']}
    G = {}
    G['norm_ffn_g'] = jnp.concatenate([gf[i]['g'] for i in range(4)], axis=0)
    G['hgrn_lb_logits'] = lb_vjp(gm[1]['lb'])[0]
    G['hgrn_o_norm_g'] = gm[1]['go']
    for n in ('lam_re', 'lam_im', 'log_dt', 'b_re', 'b_im', 'c_re', 'c_im'):
        G['s5_' + n] = gm[2][n][None]
    G['s5_d'] = gm[2]['dskip']
    G['ret_gn_g'] = gm[3]['gn']
    G['ffn_conv_w'] = jnp.stack([gf[i]['cw'] for i in range(4)])
    G['ffn_conv_b'] = jnp.concatenate([gf[i]['cb'] for i in range(4)], axis=0)
    early = ex.all_devices(_pack([G[n] for n in SMALL_EARLY], F32, 8))

    dh, gm[0] = bm[0](dh, rider=[ex.scatter({('s5_w_glu', 0): gm[2]['w_glu'], **hgrn_grads, **ffn_grads(0, gf[0])}),
                                 ex.swap()], rider_dq=[early, ex.scatter(ret_grads)])
    a = gm[0]
    ex.scatter({('mla_w_down', 0): a['w_down'][:, :MLA_QL + MLA_KVL + MLA_ROPE], ('mla_w_uq', 0): _uq_from_heads(a['w_uq']),
                ('mla_w_ukv', 0): a['w_ukv'], ('mla_w_o', 0): a['w_o']}, name="scatter_mla")
    G['meta_tokens'] = dh[PAD:LEAD]
    G['norm_mix_g'] = jnp.concatenate([gm[i]['g'] for i in range(4)], axis=0)
    G['mla_cq_norm_g'], G['mla_ckv_norm_g'] = a['gcq'], a['gckv']
    G['mla_q_head_g'] = _join_head_gain(a['gqn'], a['gqr'])
    G['mla_k_head_g'] = _join_head_gain(a['gkn'], a['gkr'])
    ex.all_devices(_pack([G[n] for n in SMALL_LATE], F32, 8), name="grad_small_gather")
    return loss, dh[LEAD:], G


PACK_W = 1024
ANY = pl.BlockSpec(memory_space=pl.ANY)


def _pack(arrs, dtype, row_mult):
    flat = jnp.concatenate([a.reshape(-1).astype(dtype) for a in arrs])
    n = flat.shape[0]
    rows = -(-n // (PACK_W * row_mult)) * row_mult
    return jnp.pad(flat, (0, rows * PACK_W - n)).reshape(rows, PACK_W)


def _unpack(buf, shapes):
    flat = buf.reshape(-1)
    out, off = [], 0
    for s in shapes:
        n = math.prod(s)
        out.append(flat[off:off + n].reshape(s))
        off += n
    return out


def _my_pos():
    return lax.axis_index("x"), lax.axis_index("y"), lax.axis_index("c")


def _other_chips(x, y):
    return [(1 - x, y), (x, 1 - y), (1 - x, 1 - y)]


def gather_chips(name, src):
    def body(src_ref, out_ref, send_sems, recv_sems, local_sem):
        x, y, c = _my_pos()
        q = 2 * x + y
        mine = pltpu.make_async_copy(src_ref, out_ref.at[q], local_sem)
        mine.start()
        peers = _other_chips(x, y)

        def copy(k, slot, peer):
            return pltpu.make_async_remote_copy(src_ref=src_ref, dst_ref=out_ref.at[slot], send_sem=send_sems.at[k],
                                                recv_sem=recv_sems.at[k], device_id=(peer[0], peer[1], c),
                                                device_id_type=MESH_ID)
        sends = [copy(k, q, p) for k, p in enumerate(peers)]
        for cp in sends:
            cp.start()
        for k, p in enumerate(peers):
            copy(k, 2 * p[0] + p[1], p).wait_recv()
        for cp in sends:
            cp.wait_send()
        mine.wait()

    return pl.pallas_call(body, out_shape=jax.ShapeDtypeStruct((4,) + src.shape, src.dtype), in_specs=[ANY],
                          out_specs=ANY, name=name,
                          scratch_shapes=[pltpu.SemaphoreType.DMA((3,)), pltpu.SemaphoreType.DMA((3,)),
                                          pltpu.SemaphoreType.DMA(())])(src)


def _pack_tile(rows):
    return _divisors(rows, (256, 128, 64, 32, 16, 8))[0] if rows > 512 else rows


def sum_slots(name, slots):
    n, rows, w = slots.shape
    tr = _pack_tile(rows)

    def body(s_ref, o_ref):
        acc = s_ref[0].astype(F32)
        for k in range(1, n):
            acc = acc + s_ref[k].astype(F32)
        o_ref[...] = acc

    return pl.pallas_call(body, grid=(rows // tr,), in_specs=[pl.BlockSpec((n, tr, w), lambda i: (0, i, 0))],
                          out_specs=pl.BlockSpec((tr, w), lambda i: (i, 0)),
                          out_shape=jax.ShapeDtypeStruct((rows, w), F32), name=name, compiler_params=_cparams())(slots)


def adamw(name, grads, w, m, v):
    rows, wd = w.shape
    tr = _pack_tile(rows)
    ng = len(grads)

    def body(*refs):
        g = refs[0][...]
        for r in refs[1:ng]:
            g = g + r[...]
        w_ref, m_ref, v_ref = refs[ng:ng + 3]
        g_out, d_out, m_out, v_out = refs[ng + 3:]
        m_new = ADAM_B1 * m_ref[...] + (1.0 - ADAM_B1) * g
        v_new = ADAM_B2 * v_ref[...] + (1.0 - ADAM_B2) * jnp.square(g)
        m_hat = m_new / (1.0 - ADAM_B1 ** ADAM_STEP)
        v_hat = v_new / (1.0 - ADAM_B2 ** ADAM_STEP)
        g_out[...] = g
        d_out[...] = -ADAM_LR * (m_hat / (jnp.sqrt(v_hat) + ADAM_EPS) + ADAM_WD * w_ref[...])
        m_out[...] = m_new
        v_out[...] = v_new

    spec = pl.BlockSpec((tr, wd), lambda i: (i, 0))
    shape = jax.ShapeDtypeStruct((rows, wd), F32)
    return pl.pallas_call(body, grid=(rows // tr,), in_specs=[spec] * (ng + 3), out_specs=[spec] * 4,
                          out_shape=[shape] * 4, name=name, compiler_params=_cparams())(*grads, w, m, v)


def _sem_scratch(nw):
    return [pltpu.SemaphoreType.DMA((3 * nw,)), pltpu.SemaphoreType.DMA((3 * nw,)), pltpu.SemaphoreType.DMA((nw,))]


def _block2d(ref, axis, p, n):
    if axis == 0:
        return ref.at[pl.ds(pl.multiple_of(p * n, 16), n), :]
    return ref.at[:, pl.ds(pl.multiple_of(p * n, 128), n)]


class ScatterRider:
    def __init__(self, items):
        self.items = items
        self.operands = [it[0] for it in items]
        self.results = None
        self.out_shapes = [jax.ShapeDtypeStruct((4, arr.shape[0] // 4, arr.shape[1]) if axis == 0 else
                                                (4, arr.shape[0], arr.shape[1] // 4), arr.dtype) for arr, axis in items]
        self.scratch = _sem_scratch(len(items))

    def _copies(self, ins, outs, sems):
        send_sems, recv_sems, local_sems = sems
        x, y, c = _my_pos()
        q = 2 * x + y
        local, sends, lands = [], [], []
        for w, (arr, axis) in enumerate(self.items):
            n = arr.shape[axis] // 4
            local.append(pltpu.make_async_copy(_block2d(ins[w], axis, q, n), outs[w].at[q], local_sems.at[w]))
            for k, (px, py) in enumerate(_other_chips(x, y)):
                p = 2 * px + py
                sems_k = dict(send_sem=send_sems.at[3 * w + k], recv_sem=recv_sems.at[3 * w + k],
                              device_id=(px, py, c), device_id_type=MESH_ID)
                theirs = _block2d(ins[w], axis, p, n)
                sends.append(pltpu.make_async_remote_copy(src_ref=theirs, dst_ref=outs[w].at[q], **sems_k))
                lands.append(pltpu.make_async_remote_copy(src_ref=theirs, dst_ref=outs[w].at[p], **sems_k))
        return local, sends, lands

    def start(self, ins, outs, sems):
        local, sends, _ = self._copies(ins, outs, sems)
        for cp in local + sends:
            cp.start()

    def finish(self, ins, outs, sems):
        local, sends, lands = self._copies(ins, outs, sems)
        for cp in lands:
            cp.wait_recv()
        for cp in sends:
            cp.wait_send()
        for cp in local:
            cp.wait()


class GatherRider:
    def __init__(self, items):
        self.items = items
        self.operands = [it[0] for it in items]
        self.results = None
        self.out_shapes = []
        for arr, _, axis in items:
            r, c = arr.shape[1:]
            assert r % 32 == 0
            self.out_shapes.append(jax.ShapeDtypeStruct((4 * r, c) if axis == 0 else (r, 4 * c), arr.dtype))
        n = len(items)
        dma = pltpu.SemaphoreType.DMA
        self.scratch = [dma((3 * n,)), dma((3 * n,)), dma((n,)), dma((3 * n,)), dma((3 * n,))]

    def _copies(self, ins, outs, sems):
        send_sems, recv_sems, local_sems, pass_send_sems, pass_recv_sems = sems
        x, y, c = _my_pos()
        q = 2 * x + y
        local, sends, lands, passes, pass_lands = [], [], [], [], []
        for w, (arr, layer, axis) in enumerate(self.items):
            r, cols = arr.shape[1:]
            half = r // 2
            src = ins[w].at[layer]

            def part(blk, hc, w=w, axis=axis, r=r, cols=cols, half=half):
                if axis == 0:
                    return outs[w].at[pl.ds(pl.multiple_of(blk * r + hc * half, 16), half), :]
                return outs[w].at[pl.ds(pl.multiple_of(hc * half, 16), half), pl.ds(pl.multiple_of(blk * cols, 128), cols)]

            local.append(pltpu.make_async_copy(src, _block2d(outs[w], axis, q, arr.shape[1 + axis]), local_sems.at[w]))
            for k, (px, py) in enumerate(_other_chips(x, y)):
                p = 2 * px + py
                ici = dict(send_sem=send_sems.at[3 * w + k], recv_sem=recv_sems.at[3 * w + k],
                           device_id=(px, py, c), device_id_type=MESH_ID)
                d2d = dict(send_sem=pass_send_sems.at[3 * w + k], recv_sem=pass_recv_sems.at[3 * w + k],
                           device_id=(x, y, 1 - c), device_id_type=MESH_ID)
                mine = src.at[pl.ds(pl.multiple_of(c * half, 16), half), :]
                sends.append(pltpu.make_async_remote_copy(src_ref=mine, dst_ref=part(q, c), **ici))
                lands.append(pltpu.make_async_remote_copy(src_ref=mine, dst_ref=part(p, c), **ici))
                passes.append(pltpu.make_async_remote_copy(src_ref=part(p, c), dst_ref=part(p, c), **d2d))
                pass_lands.append(pltpu.make_async_remote_copy(src_ref=part(p, c), dst_ref=part(p, 1 - c), **d2d))
        return local, sends, lands, passes, pass_lands

    def start(self, ins, outs, sems):
        local, sends, _, _, _ = self._copies(ins, outs, sems)
        for cp in local + sends:
            cp.start()

    def middle(self, ins, outs, sems):
        _, _, lands, passes, _ = self._copies(ins, outs, sems)
        for land, cp in zip(lands, passes):
            land.wait_recv()
            cp.start()

    def finish(self, ins, outs, sems):
        local, sends, _, passes, pass_lands = self._copies(ins, outs, sems)
        for cp in pass_lands:
            cp.wait_recv()
        for cp in sends + passes:
            cp.wait_send()
        for cp in local:
            cp.wait()


class SwapRider:
    def __init__(self, arrs):
        self.operands = list(arrs)
        self.out_shapes = [jax.ShapeDtypeStruct(a.shape, a.dtype) for a in arrs]
        self.scratch = [pltpu.SemaphoreType.DMA((len(arrs),)), pltpu.SemaphoreType.DMA((len(arrs),))]
        self.results = None

    def _copies(self, ins, outs, sems):
        x, y, c = _my_pos()
        return [pltpu.make_async_remote_copy(src_ref=ins[w], dst_ref=outs[w], send_sem=sems[0].at[w],
                                             recv_sem=sems[1].at[w], device_id=(x, y, 1 - c), device_id_type=MESH_ID)
                for w in range(len(self.operands))]

    def start(self, ins, outs, sems):
        for cp in self._copies(ins, outs, sems):
            cp.start()

    def finish(self, ins, outs, sems):
        for cp in self._copies(ins, outs, sems):
            cp.wait()


class AllDevicesRider:
    def __init__(self, src):
        self.operands = [src]
        self.out_shapes = [jax.ShapeDtypeStruct((8,) + src.shape, src.dtype)]
        self.scratch = [pltpu.SemaphoreType.DMA((7,)), pltpu.SemaphoreType.DMA((7,)), pltpu.SemaphoreType.DMA(())]
        self.results = None

    def _copies(self, ins, outs, sems):
        x, y, c = _my_pos()
        me = 4 * x + 2 * y + c
        local = pltpu.make_async_copy(ins[0], outs[0].at[me], sems[2])
        sends, lands = [], []
        for k, m in enumerate(range(1, 8)):
            peer = ((1 - x) if m & 4 else x, (1 - y) if m & 2 else y, (1 - c) if m & 1 else c)
            sems_k = dict(send_sem=sems[0].at[k], recv_sem=sems[1].at[k], device_id=peer, device_id_type=MESH_ID)
            sends.append(pltpu.make_async_remote_copy(src_ref=ins[0], dst_ref=outs[0].at[me], **sems_k))
            lands.append(pltpu.make_async_remote_copy(src_ref=ins[0], dst_ref=outs[0].at[4 * peer[0] + 2 * peer[1] + peer[2]],
                                                      **sems_k))
        return local, sends, lands

    def start(self, ins, outs, sems):
        local, sends, _ = self._copies(ins, outs, sems)
        for cp in [local] + sends:
            cp.start()

    def finish(self, ins, outs, sems):
        local, sends, lands = self._copies(ins, outs, sems)
        for cp in lands:
            cp.wait_recv()
        for cp in sends:
            cp.wait_send()
        local.wait()


class RiderGroup:
    def __init__(self, riders):
        self.riders = riders
        self.operands = [a for r in riders for a in r.operands]
        self.out_shapes = [s for r in riders for s in r.out_shapes]
        self.scratch = [s for r in riders for s in r.scratch]

    def _split(self, ins, outs, sems):
        for r in self.riders:
            ni, no, ns = len(r.operands), len(r.out_shapes), len(r.scratch)
            yield r, ins[:ni], outs[:no], sems[:ns]
            ins, outs, sems = ins[ni:], outs[no:], sems[ns:]

    def start(self, ins, outs, sems):
        for r, i, o, s in self._split(ins, outs, sems):
            r.start(i, o, s)

    def middle(self, ins, outs, sems):
        for r, i, o, s in self._split(ins, outs, sems):
            if hasattr(r, 'middle'):
                r.middle(i, o, s)

    def finish(self, ins, outs, sems):
        for r, i, o, s in self._split(ins, outs, sems):
            r.finish(i, o, s)

    @property
    def results(self):
        return None

    @results.setter
    def results(self, res):
        for r in self.riders:
            no = len(r.out_shapes)
            r.results, res = list(res[:no]), res[no:]


def run_rider(name, rider):
    n_in, n_out = len(rider.operands), len(rider.out_shapes)

    def body(*refs):
        ins, outs, sems = refs[:n_in], refs[n_in:n_in + n_out], refs[n_in + n_out:]
        rider.start(ins, outs, sems)
        if hasattr(rider, 'middle'):
            rider.middle(ins, outs, sems)
        rider.finish(ins, outs, sems)

    rider.results = list(pl.pallas_call(body, out_shape=rider.out_shapes, in_specs=[ANY] * n_in, out_specs=[ANY] * n_out,
                                        name=name, scratch_shapes=rider.scratch)(*rider.operands))


WEIGHT_GROUPS = {'mla': [('mla_w_down', 0), ('mla_w_uq', 0), ('mla_w_ukv', 0), ('mla_w_o', 0)],
                 'hgrn': [('hgrn_w_in', 0), ('hgrn_w_o', 0)], 's5': [('s5_w_glu', 0)],
                 'ret': [('ret_w_in', 0), ('ret_w_o', 0)]}
WEIGHT_GROUPS.update({f'ffn{i}': [('ffn_w_up', i), ('ffn_w_down', i)] for i in range(4)})


class Exchange:
    def __init__(self, shards=None, full=None):
        self.shards, self.full = shards, dict(full or {})
        self.got, self.recv, self.sib, self.grads, self.small = {}, {}, {}, {}, []

    def gather(self, groups, name=None):
        if self.shards is None:
            return None
        keys = [k for g in groups for k in WEIGHT_GROUPS[g]]
        rider = GatherRider([(self.shards[n], layer, SHARD_AXIS[n] - 1) for n, layer in keys])
        self.got.update({k: (rider, j) for j, k in enumerate(keys)})
        if name is not None:
            run_rider(name, rider)
        return rider

    def weight(self, n, layer=0):
        if self.shards is None:
            return self.full[n][layer]
        rider, j = self.got[(n, layer)]
        return rider.results[j]

    def scatter(self, grads, name=None):
        if self.shards is None:
            self.grads.update(grads)
            return None
        keys = list(grads)
        rider = ScatterRider([(grads[k], SHARD_AXIS[k[0]] - 1) for k in keys])
        self.recv.update({k: (rider, j) for j, k in enumerate(keys)})
        if name is not None:
            run_rider(name, rider)
        return rider

    def received(self, n, layer):
        rider, j = self.recv[(n, layer)]
        return rider.results[j]

    def swap(self, name=None):
        if self.shards is None:
            return None
        keys = [k for k, (r, _) in self.recv.items() if k not in self.sib and r.results is not None]
        rider = SwapRider([self.received(*k) for k in keys])
        self.sib.update({k: (rider, j) for j, k in enumerate(keys)})
        if name is not None:
            run_rider(name, rider)
        return rider

    def sibling(self, n, layer):
        rider, j = self.sib[(n, layer)]
        return rider.results[j]

    def all_devices(self, packed, name=None):
        if self.shards is None:
            return None
        rider = AllDevicesRider(packed)
        self.small.append(rider)
        if name is not None:
            run_rider(name, rider)
        return rider


ADAM_BLOCK_ELEMS = 256 * 1024


def adamw_shard(name, mine, sib, w, m, v):
    nl, rows, cols = w.shape
    tr = [t for t in (512, 384, 352, 256, 176, 128, 64, 32, 16) if rows % t == 0 and t * cols <= ADAM_BLOCK_ELEMS][0]

    def body(a_ref, b_ref, w_ref, m_ref, v_ref, g_out, d_out, m_out, v_out):
        def total(r):
            acc = r[0].astype(F32)
            for k in range(1, 4):
                acc = acc + r[k].astype(F32)
            return acc
        g = total(a_ref) + total(b_ref)
        m_new = ADAM_B1 * m_ref[...] + (1.0 - ADAM_B1) * g
        v_new = ADAM_B2 * v_ref[...] + (1.0 - ADAM_B2) * jnp.square(g)
        m_hat = m_new / (1.0 - ADAM_B1 ** ADAM_STEP)
        v_hat = v_new / (1.0 - ADAM_B2 ** ADAM_STEP)
        g_out[...] = g
        d_out[...] = -ADAM_LR * (m_hat / (jnp.sqrt(v_hat) + ADAM_EPS) + ADAM_WD * w_ref[...])
        m_out[...] = m_new
        v_out[...] = v_new

    slots = pl.BlockSpec((4, None, tr, cols), lambda l, i: (0, l, i, 0))
    spec = pl.BlockSpec((None, tr, cols), lambda l, i: (l, i, 0))
    shape = jax.ShapeDtypeStruct(w.shape, F32)
    return pl.pallas_call(body, grid=(nl, rows // tr), in_specs=[slots, slots, spec, spec, spec], out_specs=[spec] * 4,
                          out_shape=[shape] * 4, name=name, compiler_params=_cparams())(mine, sib, w, m, v)


def kernel(x, meta_tokens, norm_mix_g, norm_ffn_g, mla_w_down, mla_cq_norm_g, mla_ckv_norm_g, mla_w_uq, mla_w_ukv, mla_q_head_g, mla_k_head_g, mla_w_o, hgrn_w_in, hgrn_lb_logits, hgrn_o_norm_g, hgrn_w_o, s5_lam_re, s5_lam_im, s5_log_dt, s5_b_re, s5_b_im, s5_c_re, s5_c_im, s5_d, s5_w_glu, ret_w_in, ret_gn_g, ret_w_o, ffn_w_up, ffn_conv_w, ffn_conv_b, ffn_w_down, loss_target, m_meta_tokens, m_norm_mix_g, m_norm_ffn_g, m_mla_w_down, m_mla_cq_norm_g, m_mla_ckv_norm_g, m_mla_w_uq, m_mla_w_ukv, m_mla_q_head_g, m_mla_k_head_g, m_mla_w_o, m_hgrn_w_in, m_hgrn_lb_logits, m_hgrn_o_norm_g, m_hgrn_w_o, m_s5_lam_re, m_s5_lam_im, m_s5_log_dt, m_s5_b_re, m_s5_b_im, m_s5_c_re, m_s5_c_im, m_s5_d, m_s5_w_glu, m_ret_w_in, m_ret_gn_g, m_ret_w_o, m_ffn_w_up, m_ffn_conv_w, m_ffn_conv_b, m_ffn_w_down, v_meta_tokens, v_norm_mix_g, v_norm_ffn_g, v_mla_w_down, v_mla_cq_norm_g, v_mla_ckv_norm_g, v_mla_w_uq, v_mla_w_ukv, v_mla_q_head_g, v_mla_k_head_g, v_mla_w_o, v_hgrn_w_in, v_hgrn_lb_logits, v_hgrn_o_norm_g, v_hgrn_w_o, v_s5_lam_re, v_s5_lam_im, v_s5_log_dt, v_s5_b_re, v_s5_b_im, v_s5_c_re, v_s5_c_im, v_s5_d, v_s5_w_glu, v_ret_w_in, v_ret_gn_g, v_ret_w_o, v_ffn_w_up, v_ffn_conv_w, v_ffn_conv_b, v_ffn_w_down):
    vals = (x, meta_tokens, norm_mix_g, norm_ffn_g, mla_w_down, mla_cq_norm_g, mla_ckv_norm_g, mla_w_uq, mla_w_ukv, mla_q_head_g, mla_k_head_g, mla_w_o, hgrn_w_in, hgrn_lb_logits, hgrn_o_norm_g, hgrn_w_o, s5_lam_re, s5_lam_im, s5_log_dt, s5_b_re, s5_b_im, s5_c_re, s5_c_im, s5_d, s5_w_glu, ret_w_in, ret_gn_g, ret_w_o, ffn_w_up, ffn_conv_w, ffn_conv_b, ffn_w_down, loss_target, m_meta_tokens, m_norm_mix_g, m_norm_ffn_g, m_mla_w_down, m_mla_cq_norm_g, m_mla_ckv_norm_g, m_mla_w_uq, m_mla_w_ukv, m_mla_q_head_g, m_mla_k_head_g, m_mla_w_o, m_hgrn_w_in, m_hgrn_lb_logits, m_hgrn_o_norm_g, m_hgrn_w_o, m_s5_lam_re, m_s5_lam_im, m_s5_log_dt, m_s5_b_re, m_s5_b_im, m_s5_c_re, m_s5_c_im, m_s5_d, m_s5_w_glu, m_ret_w_in, m_ret_gn_g, m_ret_w_o, m_ffn_w_up, m_ffn_conv_w, m_ffn_conv_b, m_ffn_w_down, v_meta_tokens, v_norm_mix_g, v_norm_ffn_g, v_mla_w_down, v_mla_cq_norm_g, v_mla_ckv_norm_g, v_mla_w_uq, v_mla_w_ukv, v_mla_q_head_g, v_mla_k_head_g, v_mla_w_o, v_hgrn_w_in, v_hgrn_lb_logits, v_hgrn_o_norm_g, v_hgrn_w_o, v_s5_lam_re, v_s5_lam_im, v_s5_log_dt, v_s5_b_re, v_s5_b_im, v_s5_c_re, v_s5_c_im, v_s5_d, v_s5_w_glu, v_ret_w_in, v_ret_gn_g, v_ret_w_o, v_ffn_w_up, v_ffn_conv_w, v_ffn_conv_b, v_ffn_w_down)
    names = ['x'] + WEIGHTS + ['loss_target'] + ['m_' + n for n in WEIGHTS] + ['v_' + n for n in WEIGHTS]
    A = dict(zip(names, vals))
    q = 2 * lax.axis_index("x") + lax.axis_index("y")

    small_shapes = [A[n].shape for n in SMALL_SHARDED]
    got_small = gather_chips("gather_small", _pack([A[n] for n in SMALL_SHARDED], F32, 8))
    W = {n: A[n] for n in REPLICATED}
    parts_small = [_unpack(got_small[p], small_shapes) for p in range(4)]
    for k, n in enumerate(SMALL_SHARDED):
        W[n] = jnp.concatenate([parts_small[p][k] for p in range(4)], axis=SHARD_AXIS[n])

    ex = Exchange(shards={n: A[n].astype(BF16) for n in BIG})
    loss, grad_x, G = local_step(A['x'][0], A['loss_target'][0], W, ex)
    loss = lax.psum(loss, ("x", "y", "c"))

    ex.swap(name="grad_big_sibling")
    res_big = []
    for n in BIG:
        layers = range(A[n].shape[0])
        res_big.append(adamw_shard("adam_" + n, jnp.stack([ex.received(n, layer) for layer in layers], axis=1),
                                   jnp.stack([ex.sibling(n, layer) for layer in layers], axis=1),
                                   A[n], A['m_' + n], A['v_' + n]))

    small_names = REPLICATED + SMALL_SHARDED
    gs = {}
    for part, names, rider in (("early", SMALL_EARLY, ex.small[0]), ("late", SMALL_LATE, ex.small[1])):
        total = sum_slots("grad_small_sum_" + part, rider.results[0])
        gs.update(zip(names, _unpack(total, [G[n].shape for n in names])))
    for n in SMALL_SHARDED:
        ax = SHARD_AXIS[n]
        size = gs[n].shape[ax] // 4
        gs[n] = lax.dynamic_slice_in_dim(gs[n], q * size, size, axis=ax)
    pk = lambda pre: _pack([A[pre + n] for n in small_names], F32, 8)
    own_shapes = [A[n].shape for n in small_names]
    res_small = [_unpack(r, own_shapes) for r in
                 adamw("adam_small", [_pack([gs[n] for n in small_names], F32, 8)], pk(''), pk('m_'), pk('v_'))]

    out = {}
    for j, kind in enumerate(('grad_', 'delta_', 'new_m_', 'new_v_')):
        for k, n in enumerate(BIG):
            out[kind + n] = res_big[k][j]
        for k, n in enumerate(small_names):
            out[kind + n] = res_small[j][k]
    return (loss, grad_x[None]) + tuple(out[kind + n] for kind in ('grad_', 'delta_', 'new_m_', 'new_v_')
                                        for n in WEIGHTS)
```

```python
import functools
import math

import jax
import jax.numpy as jnp
from jax import lax
from jax.experimental import pallas as pl
from jax.experimental.pallas import tpu as pltpu

F32, BF16 = jnp.float32, jnp.bfloat16
HIGHEST = lax.Precision.HIGHEST
MESH_ID = pl.DeviceIdType.MESH

D = 1024
N_META = 16
PAD = 112
LEAD = PAD + N_META
EPS = 1e-6
NEG_INF = -1e30
CHUNK = 64
VMEM_LIMIT_V7X = 56 * 1024 * 1024
MM_VMEM_BUDGET = 36 * 1024 * 1024

MLA_H, MLA_NOPE, MLA_ROPE, MLA_V = 8, 128, 64, 128
MLA_QK = MLA_NOPE + MLA_ROPE
MLA_QL, MLA_KVL = 384, 256
HG_H, HG_D, HG_C = 8, 128, 16
S5_G, S5_P, S5_K = 64, 64, 16
RET_H, RET_DK, RET_DV = 4, 256, 512
FFN_F = 2816

ADAM_LR, ADAM_B1, ADAM_B2, ADAM_EPS, ADAM_WD, ADAM_STEP = 0.001, 0.9, 0.999, 1e-08, 0.01, 10

WEIGHTS = ['meta_tokens', 'norm_mix_g', 'norm_ffn_g', 'mla_w_down', 'mla_cq_norm_g', 'mla_ckv_norm_g', 'mla_w_uq',
           'mla_w_ukv', 'mla_q_head_g', 'mla_k_head_g', 'mla_w_o', 'hgrn_w_in', 'hgrn_lb_logits', 'hgrn_o_norm_g',
           'hgrn_w_o', 's5_lam_re', 's5_lam_im', 's5_log_dt', 's5_b_re', 's5_b_im', 's5_c_re', 's5_c_im', 's5_d',
           's5_w_glu', 'ret_w_in', 'ret_gn_g', 'ret_w_o', 'ffn_w_up', 'ffn_conv_w', 'ffn_conv_b', 'ffn_w_down']
SHARD_AXIS = {'meta_tokens': 1, 'mla_w_down': 1, 'mla_w_uq': 2, 'mla_w_ukv': 2, 'mla_w_o': 1, 'hgrn_w_in': 2,
              'hgrn_w_o': 1, 's5_d': 1, 's5_w_glu': 2, 'ret_w_in': 2, 'ret_gn_g': 1, 'ret_w_o': 1, 'ffn_w_up': 2,
              'ffn_conv_w': 2, 'ffn_w_down': 1}
BIG = ['mla_w_down', 'mla_w_uq', 'mla_w_ukv', 'mla_w_o', 'hgrn_w_in', 'hgrn_w_o', 's5_w_glu', 'ret_w_in', 'ret_w_o',
       'ffn_w_up', 'ffn_w_down']
SMALL_SHARDED = ['meta_tokens', 's5_d', 'ret_gn_g', 'ffn_conv_w']
REPLICATED = [n for n in WEIGHTS if n not in SHARD_AXIS]
SMALL_LATE = ['meta_tokens', 'norm_mix_g', 'mla_cq_norm_g', 'mla_ckv_norm_g', 'mla_q_head_g', 'mla_k_head_g']
SMALL_EARLY = [n for n in REPLICATED + SMALL_SHARDED if n not in SMALL_LATE]


def _cparams():
    return pltpu.CompilerParams(vmem_limit_bytes=VMEM_LIMIT_V7X)


def _dg(a, b, ca, cb):
    return lax.dot_general(a.astype(BF16), b.astype(BF16), (((ca,), (cb,)), ((), ())),
                           preferred_element_type=F32)


@jax.custom_vjp
def mm_nn(a, b):
    return _dg(a, b, 1, 0)


@jax.custom_vjp
def mm_nt(a, b):
    return _dg(a, b, 1, 1)


@jax.custom_vjp
def mm_tn(a, b):
    return _dg(a, b, 0, 0)


mm_nn.defvjp(lambda a, b: (mm_nn(a, b), (a, b)),
             lambda r, g: (mm_nt(g, r[1]).astype(r[0].dtype), mm_tn(r[0], g).astype(r[1].dtype)))
mm_nt.defvjp(lambda a, b: (mm_nt(a, b), (a, b)),
             lambda r, g: (mm_nn(g, r[1]).astype(r[0].dtype), mm_tn(g, r[0]).astype(r[1].dtype)))
mm_tn.defvjp(lambda a, b: (mm_tn(a, b), (a, b)),
             lambda r, g: (mm_nt(r[1], g).astype(r[0].dtype), mm_nn(r[0], g).astype(r[1].dtype)))


def _dot_f32(a, b):
    return jnp.dot(a, b, precision=HIGHEST, preferred_element_type=F32)


def _shift_rows(x, s, up):
    n = x.shape[0]
    r = lax.broadcasted_iota(jnp.int32, x.shape, 0)
    if up:
        return jnp.where(r < n - s, pltpu.roll(x, n - s, 0), 0.0)
    return jnp.where(r >= s, pltpu.roll(x, s, 0), 0.0)


@functools.partial(jax.custom_vjp, nondiff_argnums=(1,))
def shift_down(x, s):
    return _shift_rows(x, s, False)


shift_down.defvjp(lambda x, s: (_shift_rows(x, s, False), None), lambda s, _, g: (_shift_rows(g, s, True),))


@functools.partial(jax.custom_vjp, nondiff_argnums=(1,))
def shift_up(x, s):
    return _shift_rows(x, s, True)


shift_up.defvjp(lambda x, s: (_shift_rows(x, s, True), None), lambda s, _, g: (_shift_rows(g, s, False),))


def _swap32_impl(x):
    ax = x.ndim - 1
    lane = lax.broadcasted_iota(jnp.int32, x.shape, ax)
    return jnp.where(lane < 32, pltpu.roll(x, 96, ax), jnp.where(lane < 64, pltpu.roll(x, 32, ax), 0.0))


@jax.custom_vjp
def swap32(x):
    return _swap32_impl(x)


swap32.defvjp(lambda x: (_swap32_impl(x), None), lambda _, g: (_swap32_impl(g),))


def _rms(x, g):
    return x * lax.rsqrt(jnp.mean(x * x, axis=-1, keepdims=True) + EPS) * g


def _silu(x):
    return x * jax.nn.sigmoid(x)


def _row_ids(pid, n, shape, axis=0):
    return pid * n + lax.broadcasted_iota(jnp.int32, shape, axis)


class Arg:
    def __init__(self, arr, block, imap, diff=True, gdtype=F32):
        self.arr, self.block, self.imap, self.diff, self.gdtype = arr, block, imap, diff, gdtype


class Out:
    def __init__(self, shape, dtype, block, imap):
        self.shape, self.dtype, self.block, self.imap = shape, dtype, block, imap


def _free_axes(imap, grid):
    ng = len(grid)
    base = tuple(imap(*([0] * ng)))
    free = []
    for ax in range(ng):
        p = [0] * ng
        p[ax] = 1
        if grid[ax] > 1 and tuple(imap(*p)) == base:
            free.append(ax)
    walked = [ax for ax in range(ng) if grid[ax] > 1]
    assert free == walked[len(walked) - len(free):], "revisited blocks must be revisited on the innermost axes"
    return free


def _pallas(name, body, grid, in_specs, out_specs, out_shape, scratch, operands, rider=None, aliases=None):
    aliases = aliases or {}
    if isinstance(rider, (list, tuple)):
        riders = [r for r in rider if r is not None]
        rider = RiderGroup(riders) if riders else None
    if rider is None:
        return pl.pallas_call(body, grid=grid, in_specs=in_specs, out_specs=out_specs, out_shape=out_shape,
                              scratch_shapes=scratch, name=name, input_output_aliases=aliases,
                              compiler_params=_cparams())(*operands)
    n_in, n_out, n_sc = len(in_specs), len(out_specs), len(scratch)
    r_in, r_out = len(rider.operands), len(rider.out_shapes)

    def body_with_rider(*refs):
        ins, refs = refs[:n_in], refs[n_in:]
        r_ins, refs = refs[:r_in], refs[r_in:]
        outs, refs = refs[:n_out], refs[n_out:]
        r_outs, refs = refs[:r_out], refs[r_out:]
        sc, r_sc = refs[:n_sc], refs[n_sc:]
        pids = [pl.program_id(a) for a in range(len(grid))]
        first = functools.reduce(jnp.logical_and, [p == 0 for p in pids])
        last = functools.reduce(jnp.logical_and, [p == g - 1 for p, g in zip(pids, grid)])

        @pl.when(first)
        def _():
            rider.start(r_ins, r_outs, r_sc)

        if hasattr(rider, 'middle'):
            step = functools.reduce(lambda acc, pg: acc * pg[1] + pg[0], zip(pids, grid), 0)

            @pl.when(step == (math.prod(grid) * 5) // 6)
            def _():
                rider.middle(r_ins, r_outs, r_sc)

        body(*ins, *outs, *sc)

        @pl.when(last)
        def _():
            rider.finish(r_ins, r_outs, r_sc)

    res = pl.pallas_call(body_with_rider, grid=grid, in_specs=list(in_specs) + [ANY] * r_in,
                         out_specs=list(out_specs) + [ANY] * r_out, out_shape=list(out_shape) + rider.out_shapes,
                         scratch_shapes=list(scratch) + rider.scratch, name=name, input_output_aliases=aliases,
                         compiler_params=_cparams())(*operands, *rider.operands)
    rider.results = list(res[n_out:])
    return res[:n_out]


def stage_fwd(name, fn, grid, args, outs, state_shape=None, rider=None):
    n_in, n_out, ng = len(args), len(outs), len(grid)

    def body(*refs):
        pids = tuple(pl.program_id(a) for a in range(ng))
        vals = [r[...] for r in refs[:n_in]]
        o_refs = refs[n_in:n_in + n_out]
        if state_shape is None:
            res = fn(pids, *vals)
        else:
            sv_ref, st_ref = refs[n_in + n_out], refs[n_in + n_out + 1]

            @pl.when(pids[-1] == 0)
            def _():
                st_ref[...] = jnp.zeros(state_shape, F32)

            s = st_ref[...]
            sv_ref[...] = s
            res = fn(pids, *vals, s)
            st_ref[...] = res[-1]
            res = res[:-1]
        for r, v in zip(o_refs, res):
            r[...] = v.astype(r.dtype)

    in_specs = [pl.BlockSpec(a.block, a.imap) for a in args]
    out_specs = [pl.BlockSpec(o.block, o.imap) for o in outs]
    out_shape = [jax.ShapeDtypeStruct(o.shape, o.dtype) for o in outs]
    scratch = []
    if state_shape is not None:
        nz = len(state_shape)
        out_specs.append(pl.BlockSpec((None, None) + tuple(state_shape), lambda i, j: (i, j) + (0,) * nz))
        out_shape.append(jax.ShapeDtypeStruct(tuple(grid) + tuple(state_shape), F32))
        scratch = [pltpu.VMEM(state_shape, F32)]
    return _pallas(name, body, grid, in_specs, out_specs, out_shape, scratch, [a.arr for a in args], rider)


def stage_bwd(name, fn, grid, args, outs, cots, state_shape=None, states=None, rider=None):
    n_in, n_out, ng = len(args), len(outs), len(grid)
    nb = grid[-1]
    rev = state_shape is not None
    didx = [k for k, a in enumerate(args) if a.diff]
    frees = [_free_axes(args[k].imap, grid) for k in didx]

    def eff(p):
        return tuple(p[:-1]) + (nb - 1 - p[-1],) if rev else tuple(p)

    def wrap(imap):
        return lambda *p: imap(*eff(p))

    def body(*refs):
        pids = tuple(pl.program_id(a) for a in range(ng))
        e = eff(pids)
        vals = [r[...] for r in refs[:n_in]]
        cts = tuple(r[...].astype(F32) for r in refs[n_in:n_in + n_out])
        pos = n_in + n_out
        if rev:
            st_in_ref = refs[pos]
            pos += 1
        g_refs = refs[pos:pos + len(didx)]
        pos += len(didx)
        dvals = [vals[k].astype(F32) for k in didx]

        def f(*dv):
            full = list(vals)
            for k, v in zip(didx, dv[:len(didx)]):
                full[k] = v
            return tuple(fn(e, *full, *dv[len(didx):]))

        if rev:
            ds_ref = refs[pos]

            @pl.when(pids[-1] == 0)
            def _():
                ds_ref[...] = jnp.zeros(state_shape, F32)

            _, vjp = jax.vjp(f, *dvals, st_in_ref[...])
            grads = vjp(cts + (ds_ref[...],))
            ds_ref[...] = grads[-1]
            grads = grads[:-1]
        else:
            _, vjp = jax.vjp(f, *dvals)
            grads = vjp(cts)
        for gref, g, free in zip(g_refs, grads, frees):
            g = g.astype(F32)
            if not free:
                gref[...] = g.astype(gref.dtype)
            else:
                first = functools.reduce(jnp.logical_and, [pids[ax] == 0 for ax in free])

                @pl.when(first)
                def _():
                    gref[...] = g

                @pl.when(jnp.logical_not(first))
                def _():
                    gref[...] += g

    in_specs = [pl.BlockSpec(a.block, wrap(a.imap)) for a in args]
    in_specs += [pl.BlockSpec(o.block, wrap(o.imap)) for o in outs]
    operands = [a.arr for a in args] + list(cots)
    scratch = []
    if rev:
        nz = len(state_shape)
        in_specs.append(pl.BlockSpec((None, None) + tuple(state_shape), lambda i, j: (i, nb - 1 - j) + (0,) * nz))
        operands.append(states)
        scratch = [pltpu.VMEM(state_shape, F32)]
    out_specs = [pl.BlockSpec(args[k].block, wrap(args[k].imap)) for k in didx]
    assert all(args[k].gdtype == F32 or not free for k, free in zip(didx, frees))
    out_shape = [jax.ShapeDtypeStruct(args[k].arr.shape, args[k].gdtype) for k in didx]
    return _pallas(name, body, grid, in_specs, out_specs, out_shape, scratch, operands, rider)


def _divisors(n, cands):
    return [c for c in cands if n % c == 0] or [n]


def _nbytes(dt):
    return jnp.dtype(dt).itemsize


def matmul(name, a, b, mode, out_dtype=F32, res=None, mask=False, res_mask=True, window=None, into=None):
    sa, sb, so = _nbytes(a.dtype), _nbytes(b.dtype), _nbytes(out_dtype)
    off, width = (window[0], window[1]) if window is not None else (0, None)
    if mode in ('nn', 'nt'):
        M, K = a.shape
        N = (width or b.shape[1]) if mode == 'nn' else b.shape[0]
        assert mode == 'nn' or width is None or width == K
        best = None
        for tm in _divisors(M, (1408, 1056, 768, 384, 128)):
            for tn in _divisors(N, (1408, 1024, 768, 512, 384, 256, 128)):
                est = 2 * (tm * K * sa + tn * K * sb + tm * tn * (so + (4 if res is not None else 0)))
                if est <= MM_VMEM_BUDGET and (best is None or tm * tn > best[0] * best[1]):
                    best = (tm, tn)
        tm, tn = best
        grid = (M // tm, N // tn)

        def body(*refs):
            a_ref, b_ref = refs[0], refs[1]
            o_ref = refs[-1]
            x = a_ref[...]
            rows = _row_ids(pl.program_id(0), tm, (tm, 1))
            if mask:
                x = jnp.where(rows >= PAD, x, jnp.zeros_like(x))
            acc = _dg(x, b_ref[...], 1, 0 if mode == 'nn' else 1)
            if res is not None:
                acc = refs[2][...] + (jnp.where(rows >= PAD, acc, 0.0) if res_mask else acc)
            o_ref[...] = acc.astype(o_ref.dtype)

        assert off % (tn if mode == 'nn' else K) == 0
        cb, kb = off // tn, off // K
        in_specs = [pl.BlockSpec((tm, K), lambda i, j: (i, 0)),
                    pl.BlockSpec((K, tn), lambda i, j: (0, j + cb)) if mode == 'nn' else
                    pl.BlockSpec((tn, K), lambda i, j: (j, kb))]
        ops = [a, b]
        if res is not None:
            in_specs.append(pl.BlockSpec((tm, tn), lambda i, j: (i, j)))
            ops.append(res)
        return pl.pallas_call(body, grid=grid, in_specs=in_specs,
                              out_specs=pl.BlockSpec((tm, tn), lambda i, j: (i, j)),
                              out_shape=jax.ShapeDtypeStruct((M, N), out_dtype), name=name,
                              compiler_params=_cparams())(*ops)
    assert mode == 'tn' and res is None
    M, K = a.shape
    N = b.shape[1]
    best = None
    for tk in _divisors(K, (1408, 1024, 768, 512, 384, 256, 128)):
        for tn in _divisors(N, (1408, 1024, 768, 512, 384, 256, 128)):
            est = 2 * (M * tk * sa + M * tn * sb + tk * tn * so)
            if est <= MM_VMEM_BUDGET and (best is None or tk * tn > best[0] * best[1]):
                best = (tk, tn)
    tk, tn = best

    def body_t(*refs):
        a_ref, b_ref, o_ref = refs[0], refs[1], refs[-1]
        y = b_ref[...]
        if mask:
            rows = lax.broadcasted_iota(jnp.int32, (M, 1), 0)
            y = jnp.where(rows >= PAD, y, jnp.zeros_like(y))
        o_ref[...] = _dg(a_ref[...], y, 0, 0).astype(o_ref.dtype)

    assert off % tn == 0
    cb = off // tn
    total = window[2] if window is not None else N
    in_specs = [pl.BlockSpec((M, tk), lambda i, j: (0, i)), pl.BlockSpec((M, tn), lambda i, j: (0, j))]
    ops, alias = [a, b], {}
    if into is not None:
        in_specs.append(ANY)
        ops.append(into)
        alias = {2: 0}
    return pl.pallas_call(body_t, grid=(K // tk, N // tn), in_specs=in_specs,
                          out_specs=pl.BlockSpec((tk, tn), lambda i, j: (i, j + cb)),
                          out_shape=jax.ShapeDtypeStruct((K, total), out_dtype), name=name,
                          input_output_aliases=alias, compiler_params=_cparams())(*ops)


def linear_bwd(name, act, w, dy, mask=False, da_dtype=F32):
    return (matmul(name + "_da", dy, w, 'nt', out_dtype=da_dtype, mask=mask),
            matmul(name + "_dw", act, dy, 'tn', out_dtype=BF16, mask=mask))


def _row_tile(T):
    return _divisors(T, (384, 128))[0]


def _rows(arr, tm, gdtype=F32):
    return Arg(arr, (tm, arr.shape[1]), lambda i: (i, 0), gdtype=gdtype)


def _const(arr, diff=True):
    return Arg(arr, arr.shape, lambda *p: (0,) * arr.ndim, diff)


def _norm_fn(pids, h, g):
    return (_rms(h, g),)


def _norm_bwd_fn(pids, h, g):
    return (_rms(h, g), h)


def norm_fwd(name, h, g, dtype):
    T = h.shape[0]
    tm = _row_tile(T)
    return stage_fwd(name, _norm_fn, (T // tm,), [_rows(h, tm), _const(g)],
                     [Out((T, D), dtype, (tm, D), lambda i: (i, 0))])[0]


def norm_bwd(name, h, g, da, dh):
    T = h.shape[0]
    tm = _row_tile(T)
    o = Out((T, D), F32, (tm, D), lambda i: (i, 0))
    return stage_bwd(name, _norm_bwd_fn, (T // tm,), [_rows(h, tm), _const(g)], [o, o], [da, dh])


def _causal_conv3(u, cw, cb):
    return cw[2:3] * u + cw[1:2] * shift_down(u, 1) + cw[0:1] * shift_down(u, 2) + cb


def _ffn_act_fn(pids, ug, uv, cwg, cwv, cbg, cbv):
    return (_silu(_causal_conv3(ug, cwg, cbg)) * _causal_conv3(uv, cwv, cbv),)


def _ffn_act_args(ug, uv, cw, cb):
    T = ug.shape[0]
    col = lambda j: (0, j)
    args = [Arg(ug, (T, 128), col, gdtype=BF16), Arg(uv, (T, 128), col, gdtype=BF16),
            Arg(cw[:, :FFN_F], (3, 128), col), Arg(cw[:, FFN_F:], (3, 128), col),
            Arg(cb[:, :FFN_F], (1, 128), col), Arg(cb[:, FFN_F:], (1, 128), col)]
    outs = [Out((T, FFN_F), BF16, (T, 128), col)]
    return (FFN_F // 128,), args, outs


def _interleave_cols(w, n_parts, tile=128):
    lead = w.shape[:-1]
    n = w.shape[-1] // (n_parts * tile)
    k = len(lead)
    return w.reshape(lead + (n_parts, n, tile)).transpose(tuple(range(k)) + (k + 1, k, k + 2)).reshape(w.shape)


def _deinterleave_cols(w, n_parts, tile=128):
    lead = w.shape[:-1]
    n = w.shape[-1] // (n_parts * tile)
    k = len(lead)
    return w.reshape(lead + (n, n_parts, tile)).transpose(tuple(range(k)) + (k + 1, k, k + 2)).reshape(w.shape)


def ffn_layer(i, h, g, w_up, cw, cb, w_down):
    gate_w, val_w = (0, FFN_F, 2 * FFN_F), (FFN_F, FFN_F, 2 * FFN_F)
    b = norm_fwd(f"ffn{i}_norm", h, g, BF16)
    ug = matmul(f"ffn{i}_up_g", b, w_up, 'nn', out_dtype=BF16, window=gate_w)
    uv = matmul(f"ffn{i}_up_v", b, w_up, 'nn', out_dtype=BF16, window=val_w)
    grid, args, outs = _ffn_act_args(ug, uv, cw, cb)
    p = stage_fwd(f"ffn{i}_act", _ffn_act_fn, grid, args, outs)[0]
    h_new = matmul(f"ffn{i}_down", p, w_down, 'nn', res=h)

    def bwd(dh):
        dp, dwd = linear_bwd(f"ffn{i}_down_b", p, w_down, dh, mask=True, da_dtype=BF16)
        dug, duv, dcwg, dcwv, dcbg, dcbv = stage_bwd(f"ffn{i}_act_b", _ffn_act_fn, grid, args, outs, [dp])
        db = matmul(f"ffn{i}_up_b_da_g", dug, w_up, 'nt', window=gate_w)
        db = matmul(f"ffn{i}_up_b_da_v", duv, w_up, 'nt', window=val_w, res=db, res_mask=False)
        dwu = matmul(f"ffn{i}_up_b_dw_g", b, dug, 'tn', out_dtype=BF16, window=gate_w)
        dwu = matmul(f"ffn{i}_up_b_dw_v", b, duv, 'tn', out_dtype=BF16, window=val_w, into=dwu)
        dh2, dg = norm_bwd(f"ffn{i}_norm_b", h, g, db, dh)
        return dh2, dict(g=dg, w_up=dwu, cw=jnp.concatenate([dcwg, dcwv], axis=1),
                         cb=jnp.concatenate([dcbg, dcbv], axis=1), w_down=dwd)

    return h_new, bwd


def _mla_latent_fn(pids, down, gcq, gckv):
    cq = _rms(down[:, :MLA_QL], gcq)
    ckv = _rms(down[:, MLA_QL:MLA_QL + MLA_KVL], gckv)
    return cq, ckv, down[:, MLA_QL + MLA_KVL:]


def _rope64(x, cos, sin_signed):
    return x * cos + swap32(x) * sin_signed


def _mla_heads_fn(pids, qraw, kv, kpe, gqn, gqr, gkn, gkr, cos, sin_signed):
    qn, qr = qraw[:, :128], qraw[:, 128:]
    rq = lax.rsqrt((jnp.sum(qn * qn, -1, keepdims=True) + jnp.sum(qr * qr, -1, keepdims=True)) / MLA_QK + EPS)
    q = jnp.concatenate([qn * rq * gqn, _rope64(qr * rq * gqr, cos, sin_signed)], axis=1)
    kn, v = kv[:, :128], kv[:, 128:]
    rk = lax.rsqrt((jnp.sum(kn * kn, -1, keepdims=True) + jnp.sum(kpe * kpe, -1, keepdims=True)) / MLA_QK + EPS)
    k = jnp.concatenate([kn * rk * gkn, _rope64(kpe * rk * gkr, cos, sin_signed)], axis=1)
    return q, k, v


def _chunk_id(r):
    return jnp.where(r < LEAD, 0, 1 + lax.shift_right_arithmetic(r - LEAD, 6))


ATT_T = 384
ATT_SCALE = MLA_QK ** -0.5


def _att_mask(s, qb, kb):
    qrow = _row_ids(qb, ATT_T, (ATT_T, 1))
    krow = _row_ids(kb, ATT_T, (1, ATT_T), axis=1)
    ok = jnp.logical_and(_chunk_id(krow) <= _chunk_id(qrow), krow >= PAD)
    return jnp.where(ok, s, NEG_INF)


def _att_scores(q, k_ref, qb, kb, masked):
    ks = k_ref[pl.ds(pl.multiple_of(kb * ATT_T, ATT_T), ATT_T), :]
    s = _dg(q, ks, 1, 1) * ATT_SCALE
    return (_att_mask(s, qb, kb) if masked else s), ks


def _att_key_loop(j, step, init):
    carry = step(0, init, True)
    carry = lax.fori_loop(1, j, lambda kb, c: step(kb, c, False), carry)
    return lax.cond(j > 0, lambda c: step(j, c, True), lambda c: c, carry)


def _att_rows(ref, b):
    return ref[pl.ds(pl.multiple_of(b * ATT_T, ATT_T), ATT_T), :]


def attention_fwd(q, k, v, rider=None):
    H, T, _ = q.shape
    nq = T // ATT_T

    def body(q_ref, k_ref, v_ref, o_ref, lse_ref):
        j = pl.program_id(1)
        qv = q_ref[...]

        def step(kb, carry, masked):
            m, l, acc = carry
            s, _ = _att_scores(qv, k_ref, j, kb, masked)
            m_new = jnp.maximum(m, jnp.max(s, axis=-1, keepdims=True))
            p = jnp.exp(s - m_new)
            alpha = jnp.exp(m - m_new)
            return (m_new, alpha * l + jnp.sum(p, axis=-1, keepdims=True),
                    alpha * acc + _dg(p, _att_rows(v_ref, kb), 1, 0))

        init = (jnp.full((ATT_T, 1), NEG_INF, F32), jnp.zeros((ATT_T, 1), F32), jnp.zeros((ATT_T, MLA_V), F32))
        m, l, acc = _att_key_loop(j, step, init)
        o_ref[...] = acc / l
        lse_ref[...] = m + jnp.log(l)

    return _pallas("mla_attn", body, (H, nq),
                   [pl.BlockSpec((None, ATT_T, 256), lambda hh, j: (hh, j, 0)),
                    pl.BlockSpec((None, T, 256), lambda hh, j: (hh, 0, 0)),
                    pl.BlockSpec((None, T, MLA_V), lambda hh, j: (hh, 0, 0))],
                   [pl.BlockSpec((ATT_T, MLA_V), lambda hh, j: (j, hh)),
                    pl.BlockSpec((None, ATT_T, 1), lambda hh, j: (hh, j, 0))],
                   [jax.ShapeDtypeStruct((T, H * MLA_V), F32), jax.ShapeDtypeStruct((H, T, 1), F32)], [],
                   [q, k, v], rider)


def attention_bwd(q, k, v, o, lse, do, rider=None, rider_dq=None):
    H, T, _ = q.shape
    nq = T // ATT_T

    def dq_body(q_ref, k_ref, v_ref, o_ref, do_ref, lse_ref, dq_ref, delta_ref):
        j = pl.program_id(1)
        qv, dov, lsev = q_ref[...], do_ref[...], lse_ref[...]
        delta = jnp.sum(dov * o_ref[...], axis=-1, keepdims=True)
        delta_ref[...] = delta

        def step(kb, acc, masked):
            s, ks = _att_scores(qv, k_ref, j, kb, masked)
            p = jnp.exp(s - lsev)
            ds = p * (_dg(dov, _att_rows(v_ref, kb), 1, 1) - delta) * ATT_SCALE
            return acc + _dg(ds, ks, 1, 0)

        dq_ref[...] = _att_key_loop(j, step, jnp.zeros((ATT_T, 256), F32))

    q_blk = pl.BlockSpec((None, ATT_T, 256), lambda hh, j: (hh, j, 0))
    k_all = pl.BlockSpec((None, T, 256), lambda hh, j: (hh, 0, 0))
    v_all = pl.BlockSpec((None, T, MLA_V), lambda hh, j: (hh, 0, 0))
    o_blk = pl.BlockSpec((ATT_T, MLA_V), lambda hh, j: (j, hh))
    col_blk = pl.BlockSpec((None, ATT_T, 1), lambda hh, j: (hh, j, 0))
    dq, delta = _pallas("mla_attn_dq", dq_body, (H, nq), [q_blk, k_all, v_all, o_blk, o_blk, col_blk],
                        [q_blk, col_blk],
                        [jax.ShapeDtypeStruct((H, T, 256), F32), jax.ShapeDtypeStruct((H, T, 1), F32)], [],
                        [q, k, v, o, do, lse], rider_dq)

    def dkv_body(q_ref, k_ref, v_ref, do_ref, lse_ref, delta_ref, dk_ref, dv_ref):
        kb = pl.program_id(1)
        kv = k_ref[...]
        vv = v_ref[...]

        def step(qb, carry, masked):
            dk, dv = carry
            qv, dov = _att_rows(q_ref, qb), _att_rows(do_ref, qb)
            s = _dg(qv, kv, 1, 1) * ATT_SCALE
            if masked:
                s = _att_mask(s, qb, kb)
            p = jnp.exp(s - _att_rows(lse_ref, qb))
            ds = p * (_dg(dov, vv, 1, 1) - _att_rows(delta_ref, qb)) * ATT_SCALE
            return dk + _dg(ds, qv, 0, 0), dv + _dg(p, dov, 0, 0)

        carry = step(kb, (jnp.zeros((ATT_T, 256), F32), jnp.zeros((ATT_T, MLA_V), F32)), True)
        dk, dv = lax.cond(kb == 0,
                          lambda c: lax.fori_loop(kb + 1, nq, lambda qb, cc: step(qb, cc, True), c),
                          lambda c: lax.fori_loop(kb + 1, nq, lambda qb, cc: step(qb, cc, False), c), carry)
        dk_ref[...] = dk
        dv_ref[...] = dv

    q_all = pl.BlockSpec((None, T, 256), lambda hh, j: (hh, 0, 0))
    v_blk = pl.BlockSpec((None, ATT_T, MLA_V), lambda hh, j: (hh, j, 0))
    do_all = pl.BlockSpec((T, MLA_V), lambda hh, j: (0, hh))
    col_all = pl.BlockSpec((None, T, 1), lambda hh, j: (hh, 0, 0))
    dk, dv = _pallas("mla_attn_dkv", dkv_body, (H, nq), [q_all, q_blk, v_blk, do_all, col_all, col_all],
                     [q_blk, v_blk],
                     [jax.ShapeDtypeStruct((H, T, 256), F32), jax.ShapeDtypeStruct((H, T, MLA_V), F32)], [],
                     [q, k, v, do, lse, delta], rider)
    return dq, dk, dv


def mla_mixer(h, g, w, tabs, rider=None):
    T = h.shape[0]
    tm = _row_tile(T)
    nt = T // tm
    a = norm_fwd("mla_norm", h, g, BF16)
    down = matmul("mla_down", a, w['w_down'], 'nn')
    lat_args = [_rows(down, tm, BF16), _const(w['gcq']), _const(w['gckv'])]
    lat_outs = [Out((T, MLA_QL), BF16, (tm, MLA_QL), lambda i: (i, 0)),
                Out((T, MLA_KVL), BF16, (tm, MLA_KVL), lambda i: (i, 0)),
                Out((T, 128), F32, (tm, 128), lambda i: (i, 0))]
    cq, ckv, kpe = stage_fwd("mla_latent", _mla_latent_fn, (nt,), lat_args, lat_outs)
    qraw = matmul("mla_uq", cq, w['w_uq'], 'nn')
    kv = matmul("mla_ukv", ckv, w['w_ukv'], 'nn')
    hd_args = [Arg(qraw, (tm, 256), lambda i, hh: (i, hh), gdtype=BF16),
               Arg(kv, (tm, 256), lambda i, hh: (i, hh), gdtype=BF16),
               Arg(kpe, (tm, 128), lambda i, hh: (i, 0)),
               _const(w['gqn']), _const(w['gqr']), _const(w['gkn']), _const(w['gkr']),
               Arg(tabs['cos_a'], (tm, 128), lambda i, hh: (i, 0), False),
               Arg(tabs['sin_a'], (tm, 128), lambda i, hh: (i, 0), False)]
    hd_outs = [Out((MLA_H, T, 256), BF16, (None, tm, 256), lambda i, hh: (hh, i, 0)),
               Out((MLA_H, T, 256), BF16, (None, tm, 256), lambda i, hh: (hh, i, 0)),
               Out((MLA_H, T, 128), BF16, (None, tm, 128), lambda i, hh: (hh, i, 0))]
    q, k, v = stage_fwd("mla_heads", _mla_heads_fn, (nt, MLA_H), hd_args, hd_outs)
    o, lse = attention_fwd(q, k, v, rider=rider)
    h_new = matmul("mla_o", o, w['w_o'], 'nn', res=h)

    def bwd(dh, rider=None, rider_dq=None):
        do, dwo = linear_bwd("mla_o_b", o, w['w_o'], dh, mask=True)
        dq, dk, dv = attention_bwd(q, k, v, o, lse, do, rider=rider, rider_dq=rider_dq)
        dqraw, dkv, dkpe, dgqn, dgqr, dgkn, dgkr = stage_bwd("mla_heads_b", _mla_heads_fn, (nt, MLA_H), hd_args,
                                                             hd_outs, [dq, dk, dv])
        dcq, dwuq = linear_bwd("mla_uq_b", cq, w['w_uq'], dqraw)
        dckv, dwukv = linear_bwd("mla_ukv_b", ckv, w['w_ukv'], dkv)
        ddown, dgcq, dgckv = stage_bwd("mla_latent_b", _mla_latent_fn, (nt,), lat_args, lat_outs, [dcq, dckv, dkpe])
        da, dwdown = linear_bwd("mla_down_b", a, w['w_down'], ddown)
        dh2, dg = norm_bwd("mla_norm_b", h, g, da, dh)
        return dh2, dict(g=dg, w_down=dwdown, gcq=dgcq, gckv=dgckv, w_uq=dwuq, w_ukv=dwukv, gqn=dgqn, gqr=dgqr,
                         gkn=dgkn, gkr=dgkr, w_o=dwo)

    return h_new, bwd


HG_R = 384


def _hgrn_fn(pids, z, lb, go, st):
    outs = []
    for lo in range(0, z.shape[0], 128):
        o, st = _hgrn_block(z[lo:lo + 128], lb, go, st)
        outs.append(o)
    return jnp.concatenate(outs, axis=0), st


def _hgrn_block(z, lb, go, st):
    R = z.shape[0]
    zq, zf, zi, zg = z[:, :128], z[:, 128:256], z[:, 256:384], z[:, 384:]
    assert R == 128
    q = _silu(zq)
    fg = lb + (1.0 - lb) * jax.nn.sigmoid(zf)
    logf = jnp.log(fg)
    k = 1.0 - fg
    row = lax.broadcasted_iota(jnp.int32, logf.shape, 0)
    pos = row & (HG_C - 1)
    cum, rev = logf, logf
    for d in (1, 2, 4, 8):
        cum = cum + jnp.where(pos >= d, shift_down(cum, d), 0.0)
        rev = rev + jnp.where(pos < HG_C - d, shift_up(rev, d), 0.0)
    cums, tots = [cum], [cum + rev - logf]
    for s in (16, 32, 64):
        odd = (row & s) != 0
        before = shift_down(tots[-1], s)
        cums.append(cums[-1] + jnp.where(odd, before, 0.0))
        tots.append(tots[-1] + jnp.where(odd, before, shift_up(tots[-1], s)))
    t = lax.broadcasted_iota(jnp.int32, (R, R), 0)
    j = lax.broadcasted_iota(jnp.int32, (R, R), 1)
    sh = lax.shift_right_arithmetic
    a = jnp.where(jnp.logical_and(sh(t, 4) == sh(j, 4), j <= t), mm_nt(q * jnp.exp(cum), k * jnp.exp(-cum)), 0.0)
    for n, s in enumerate((16, 32, 64)):
        m = jnp.logical_and(sh(t, 5 + n) == sh(j, 5 + n), jnp.logical_and((t & s) != 0, (j & s) == 0))
        a = a + jnp.where(m, mm_nt(q * jnp.exp(cums[n]), k * jnp.exp(tots[n] - cums[n])), 0.0)
    o = mm_nn(a, zi) + mm_nt(q * jnp.exp(cums[3]), st)
    st = st * jnp.exp(tots[3][0:1, :]) + mm_tn(zi, k * jnp.exp(tots[3] - cums[3]))
    return _rms(o, go) * _silu(zg), st


def hgrn_mixer(h, g, w, rider=None):
    T = h.shape[0]
    a = norm_fwd("hgrn_norm", h, g, BF16)
    z = matmul("hgrn_in", a, w['w_in'], 'nn')
    grid = (HG_H, T // HG_R)
    args = [Arg(z, (HG_R, 512), lambda hh, j: (j, hh), gdtype=BF16), Arg(w['lb'], (1, 128), lambda hh, j: (0, hh)),
            _const(w['go'])]
    outs = [Out((T, D), BF16, (HG_R, 128), lambda hh, j: (j, hh))]
    o, states = stage_fwd("hgrn_gla", _hgrn_fn, grid, args, outs, state_shape=(HG_D, HG_D), rider=rider)
    h_new = matmul("hgrn_o", o, w['w_o'], 'nn', res=h)

    def bwd(dh, rider=None):
        do, dwo = linear_bwd("hgrn_o_b", o, w['w_o'], dh, mask=True)
        dz, dlb, dgo = stage_bwd("hgrn_gla_b", _hgrn_fn, grid, args, outs, [do], state_shape=(HG_D, HG_D),
                                 states=states, rider=rider)
        da, dwin = linear_bwd("hgrn_in_b", a, w['w_in'], dz)
        dh2, dg = norm_bwd("hgrn_norm_b", h, g, da, dh)
        return dh2, dict(g=dg, w_in=dwin, lb=dlb, go=dgo, w_o=dwo)

    return h_new, bwd


S5_R = 384
S5_W = 512
S5_SLABS = D // 128


def _cmul(ar, ai, br, bi):
    return ar * br - ai * bi, ar * bi + ai * br


def _s5_scan(br, bi, tab, cr, ci, reverse):
    R, W = br.shape
    G = R // 8
    xr, xi = br.reshape(G, 8, W), bi.reshape(G, 8, W)
    for n, d in enumerate((1, 2, 4)):
        sh = (8 - d) if reverse else d
        mr, mi = _cmul(tab[2 * n][None], tab[2 * n + 1][None], pltpu.roll(xr, sh, 1), pltpu.roll(xi, sh, 1))
        xr, xi = xr + mr, xi + mi
    pr, pi = tab[6], tab[7]
    edge = 0 if reverse else 7
    out_r, out_i = [None] * G, [None] * G
    for g in (range(G - 1, -1, -1) if reverse else range(G)):
        ar, ai = _cmul(pr, pi, cr, ci)
        gr, gi = xr[g] + ar, xi[g] + ai
        cr, ci = gr[edge:edge + 1], gi[edge:edge + 1]
        out_r[g], out_i[g] = gr, gi
    return jnp.concatenate(out_r, axis=0), jnp.concatenate(out_i, axis=0), cr, ci


def s5_scan_fwd(a, bb, cb, tab, rider=None):
    T = a.shape[0]
    nb = T // S5_R

    def body(a_ref, bb_ref, cb_ref, tab_ref, y_ref, xs_ref, c_ref):
        @pl.when(pl.program_id(1) == 0)
        def _():
            c_ref[...] = jnp.zeros(c_ref.shape, F32)

        bu = _dg(a_ref[...], bb_ref[...], 1, 0)
        t = tab_ref[...]
        xr, xi, cr, ci = _s5_scan(bu[:, :S5_W], bu[:, S5_W:], t, c_ref[0:1, :S5_W], c_ref[0:1, S5_W:], False)
        x = jnp.concatenate([xr, xi], axis=1)
        xs_ref[...] = x
        y_ref[...] = _dg(x, cb_ref[...], 1, 0)
        c_ref[0:1, :] = jnp.concatenate([cr, ci], axis=1)

    return _pallas(
        "s5_scan", body, (S5_SLABS, nb),
        [pl.BlockSpec((S5_R, 128), lambda j, i: (i, j)),
         pl.BlockSpec((None, 128, 2 * S5_W), lambda j, i: (j, 0, 0)),
         pl.BlockSpec((None, 2 * S5_W, 128), lambda j, i: (j, 0, 0)),
         pl.BlockSpec((None, 10, 8, S5_W), lambda j, i: (j, 0, 0, 0))],
        [pl.BlockSpec((S5_R, 128), lambda j, i: (i, j)),
         pl.BlockSpec((None, S5_R, 2 * S5_W), lambda j, i: (j, i, 0))],
        [jax.ShapeDtypeStruct((T, D), F32), jax.ShapeDtypeStruct((S5_SLABS, T, 2 * S5_W), F32)],
        [pltpu.VMEM((8, 2 * S5_W), F32)], [a, bb, cb, tab], rider)


def s5_scan_bwd(a, bb, cb, tab_rev, xs, dy, rider=None):
    T = a.shape[0]
    nb = T // S5_R
    rg = S5_R // 8

    def body(a_ref, dy_ref, xs_ref, xp_ref, bb_ref, cb_ref, tab_ref, da_ref, dbb_ref, dcb_ref, dab_ref, c_ref):
        i = pl.program_id(1)

        @pl.when(i == 0)
        def _():
            c_ref[...] = jnp.zeros(c_ref.shape, F32)

        dy_v = dy_ref[...]
        x = xs_ref[...]
        dxo = _dg(dy_v, cb_ref[...], 1, 1)
        gr, gi, cr, ci = _s5_scan(dxo[:, :S5_W], dxo[:, S5_W:], tab_ref[...], c_ref[0:1, :S5_W], c_ref[0:1, S5_W:], True)
        c_ref[0:1, :] = jnp.concatenate([cr, ci], axis=1)
        g = jnp.concatenate([gr, gi], axis=1)
        da_ref[...] = _dg(g, bb_ref[...], 1, 1)
        dbb = _dg(a_ref[...], g, 0, 0)
        dcb = _dg(x, dy_v, 0, 0)
        first_tile = i == nb - 1
        prev_last = jnp.where(first_tile, 0.0, xp_ref[7:8, :])
        rows = lax.broadcasted_iota(jnp.int32, x.shape, 0)
        xp = jnp.where(rows == 0, prev_last, pltpu.roll(x, 1, 0))
        xpr, xpi = xp[:, :S5_W], xp[:, S5_W:]
        dar = (gr * xpr + gi * xpi).reshape(rg, 8, S5_W).sum(axis=0)
        dai = (gi * xpr - gr * xpi).reshape(rg, 8, S5_W).sum(axis=0)
        dab = jnp.concatenate([dar, dai], axis=1)

        @pl.when(i == 0)
        def _():
            dbb_ref[...] = dbb
            dcb_ref[...] = dcb
            dab_ref[...] = dab

        @pl.when(i != 0)
        def _():
            dbb_ref[...] += dbb
            dcb_ref[...] += dcb
            dab_ref[...] += dab

    def prev_rows(j, i):
        return (j, jnp.maximum((nb - 1 - i) * rg - 1, 0), 0)

    return _pallas(
        "s5_scan_b", body, (S5_SLABS, nb),
        [pl.BlockSpec((S5_R, 128), lambda j, i: (nb - 1 - i, j)),
         pl.BlockSpec((S5_R, 128), lambda j, i: (nb - 1 - i, j)),
         pl.BlockSpec((None, S5_R, 2 * S5_W), lambda j, i: (j, nb - 1 - i, 0)),
         pl.BlockSpec((None, 8, 2 * S5_W), prev_rows),
         pl.BlockSpec((None, 128, 2 * S5_W), lambda j, i: (j, 0, 0)),
         pl.BlockSpec((None, 2 * S5_W, 128), lambda j, i: (j, 0, 0)),
         pl.BlockSpec((None, 10, 8, S5_W), lambda j, i: (j, 0, 0, 0))],
        [pl.BlockSpec((S5_R, 128), lambda j, i: (nb - 1 - i, j)),
         pl.BlockSpec((None, 128, 2 * S5_W), lambda j, i: (j, 0, 0)),
         pl.BlockSpec((None, 2 * S5_W, 128), lambda j, i: (j, 0, 0)),
         pl.BlockSpec((None, 8, 2 * S5_W), lambda j, i: (j, 0, 0))],
        [jax.ShapeDtypeStruct((T, D), F32), jax.ShapeDtypeStruct((S5_SLABS, 128, 2 * S5_W), F32),
         jax.ShapeDtypeStruct((S5_SLABS, 2 * S5_W, 128), F32), jax.ShapeDtypeStruct((S5_SLABS, 8, 2 * S5_W), F32)],
        [pltpu.VMEM((8, 2 * S5_W), F32)], [a, dy, xs, xs, bb, cb, tab_rev], rider)


def _s5_discretise(lam_re, lam_im, log_dt, b_re, b_im, c_re, c_im):
    dt = jnp.exp(log_dt)[:, None]
    mag = jnp.exp(lam_re * dt)
    abar_re = mag * jnp.cos(lam_im * dt)
    abar_im = mag * jnp.sin(lam_im * dt)
    den = lam_re * lam_re + lam_im * lam_im
    zoh_re = ((abar_re - 1.0) * lam_re + abar_im * lam_im) / den
    zoh_im = (abar_im * lam_re - (abar_re - 1.0) * lam_im) / den
    bbar_re = zoh_re[..., None] * b_re - zoh_im[..., None] * b_im
    bbar_im = zoh_re[..., None] * b_im + zoh_im[..., None] * b_re
    eye = jnp.eye(8, dtype=F32)

    def in_map(bbar):
        t = bbar.reshape(8, 8, S5_P, S5_K).transpose(0, 1, 3, 2)
        return (t[:, :, :, None, :] * eye[None, :, None, :, None]).reshape(8, 8 * S5_K, 8 * S5_P)

    def out_map(c):
        t = c.reshape(8, 8, S5_K, S5_P).transpose(0, 1, 3, 2)
        return (t[:, :, :, None, :] * eye[None, :, None, :, None]).reshape(8, 8 * S5_P, 8 * S5_K)

    bb = jnp.concatenate([in_map(bbar_re), in_map(bbar_im)], axis=2)
    cb = jnp.concatenate([out_map(c_re), -out_map(c_im)], axis=1)
    return bb, cb, abar_re.reshape(8, S5_W), abar_im.reshape(8, S5_W)


def _s5_tables(ar, ai, reverse):
    if reverse:
        ai = -ai
    pw = [(jnp.ones_like(ar), jnp.zeros_like(ar))]
    for _ in range(8):
        pw.append(_cmul(pw[-1][0], pw[-1][1], ar, ai))
    r = jnp.arange(8)[None, :, None]
    rows = []
    for d in (1, 2, 4):
        keep = (r <= 7 - d) if reverse else (r >= d)
        rows += [jnp.where(keep, pw[d][0][:, None, :], 0.0), jnp.where(keep, pw[d][1][:, None, :], 0.0)]
    order = [8 - k for k in range(8)] if reverse else [k + 1 for k in range(8)]
    rows += [jnp.stack([pw[n][0] for n in order], axis=1), jnp.stack([pw[n][1] for n in order], axis=1)]
    rows += [jnp.broadcast_to(pw[8][0][:, None, :], (8, 8, S5_W)), jnp.broadcast_to(pw[8][1][:, None, :], (8, 8, S5_W))]
    return jnp.stack(rows, axis=1)


def _s5_act_fn(pids, yc, a, dskip):
    return (jax.nn.gelu(yc + dskip * a),)


def _make_glu_res_fn(tm):
    def glu_res_fn(pids, zz, h):
        rows = _row_ids(pids[0], tm, (tm, 1))
        return (h + jnp.where(rows >= PAD, zz[:, :D] * jax.nn.sigmoid(zz[:, D:]), 0.0),)
    return glu_res_fn


def s5_mixer(h, g, w, rider=None):
    T = h.shape[0]
    tm = _row_tile(T)
    nt = T // tm
    a = norm_fwd("s5_norm", h, g, F32)
    ssm = [w[n] for n in ('lam_re', 'lam_im', 'log_dt', 'b_re', 'b_im', 'c_re', 'c_im')]
    (bb, cb, ar, ai), disc_vjp = jax.vjp(_s5_discretise, *ssm)
    yc, xs = s5_scan_fwd(a, bb, cb, _s5_tables(ar, ai, False), rider=rider)
    row = lambda arr: _rows(arr, tm)
    act_args = [row(yc), row(a), _const(w['dskip'])]
    act_outs = [Out((T, D), BF16, (tm, D), lambda i: (i, 0))]
    y = stage_fwd("s5_act", _s5_act_fn, (nt,), act_args, act_outs)[0]
    zz = matmul("s5_glu", y, w['w_glu'], 'nn')
    glu_fn = _make_glu_res_fn(tm)
    glu_args = [_rows(zz, tm, BF16), row(h)]
    glu_outs = [Out((T, D), F32, (tm, D), lambda i: (i, 0))]
    h_new = stage_fwd("s5_gate", glu_fn, (nt,), glu_args, glu_outs)[0]

    def bwd(dh, rider=None):
        dzz, dh_res = stage_bwd("s5_gate_b", glu_fn, (nt,), glu_args, glu_outs, [dh])
        dy, dwglu = linear_bwd("s5_glu_b", y, w['w_glu'], dzz)
        dyc, da1, ddskip = stage_bwd("s5_act_b", _s5_act_fn, (nt,), act_args, act_outs, [dy])
        da2, dbb, dcb, dab = s5_scan_bwd(a, bb, cb, _s5_tables(ar, ai, True), xs, dyc, rider=rider)
        dab = dab.sum(axis=1)
        dssm = disc_vjp((dbb, dcb, dab[:, :S5_W], dab[:, S5_W:]))
        dh2, dg = _s5_norm_bwd(h, g, da1, da2, dh_res, tm)
        grads = dict(zip(('lam_re', 'lam_im', 'log_dt', 'b_re', 'b_im', 'c_re', 'c_im'), dssm))
        grads.update(g=dg, dskip=ddskip, w_glu=dwglu)
        return dh2, grads

    return h_new, bwd


def _norm3_bwd_fn(pids, h, g):
    a = _rms(h, g)
    return a, a, h


def _s5_norm_bwd(h, g, da1, da2, dh, tm):
    T = h.shape[0]
    o = Out((T, D), F32, (tm, D), lambda i: (i, 0))
    return stage_bwd("s5_norm_b", _norm3_bwd_fn, (T // tm,), [_rows(h, tm), _const(g)], [o, o, o], [da1, da2, dh])


RET_R = 384


def _rope256(x, cos, sin):
    x1, x2 = x[:, :128], x[:, 128:]
    return jnp.concatenate([x1 * cos - x2 * sin, x1 * sin + x2 * cos], axis=1)


def _ret_fn(pids, z, gn, cos, sin, dmat, qdec, kdec, cdec, st):
    R = z.shape[0]
    q = _rope256(z[:, :256], cos, sin)
    k = _rope256(z[:, 256:512], cos, sin) * (RET_DK ** -0.5)
    v, gate = z[:, 512:1024], z[:, 1024:]
    outs = []
    for cc in range(R // CHUNK):
        lo = cc * CHUNK
        qc, kc, vc = q[lo:lo + CHUNK], k[lo:lo + CHUNK], v[lo:lo + CHUNK]
        outs.append(mm_nn(mm_nt(qc, kc) * dmat, vc) + mm_nn(qc * qdec, st))
        st = st * cdec + mm_tn(kc * kdec, vc)
    o = jnp.concatenate(outs, axis=0)
    mu = jnp.mean(o, axis=-1, keepdims=True)
    var = jnp.mean(jnp.square(o - mu), axis=-1, keepdims=True)
    o = (o - mu) * lax.rsqrt(var + EPS)
    return o * gn * _silu(gate), st


def ret_mixer(h, g, w, tabs, rider=None):
    T = h.shape[0]
    a = norm_fwd("ret_norm", h, g, BF16)
    z = matmul("ret_in", a, w['w_in'], 'nn')
    grid = (RET_H, T // RET_R)
    hw = RET_DK * 2 + RET_DV * 2
    args = [Arg(z, (RET_R, hw), lambda hh, j: (j, hh), gdtype=BF16), Arg(w['gn'], (1, RET_DV), lambda hh, j: (0, hh)),
            Arg(tabs['cos_d'], (RET_R, 128), lambda hh, j: (j, 0), False),
            Arg(tabs['sin_d'], (RET_R, 128), lambda hh, j: (j, 0), False),
            Arg(tabs['ret_dmat'], (None, CHUNK, CHUNK), lambda hh, j: (hh, 0, 0), False),
            Arg(tabs['ret_qdec'], (None, CHUNK, 1), lambda hh, j: (hh, 0, 0), False),
            Arg(tabs['ret_kdec'], (None, CHUNK, 1), lambda hh, j: (hh, 0, 0), False),
            Arg(tabs['ret_cdec'], (None, 1, 1), lambda hh, j: (hh, 0, 0), False)]
    outs = [Out((T, RET_H * RET_DV), BF16, (RET_R, RET_DV), lambda hh, j: (j, hh))]
    o, states = stage_fwd("ret_chunks", _ret_fn, grid, args, outs, state_shape=(RET_DK, RET_DV), rider=rider)
    h_new = matmul("ret_o", o, w['w_o'], 'nn', res=h)

    def bwd(dh, rider=None):
        do, dwo = linear_bwd("ret_o_b", o, w['w_o'], dh, mask=True)
        dz, dgn = stage_bwd("ret_chunks_b", _ret_fn, grid, args, outs, [do], state_shape=(RET_DK, RET_DV),
                            states=states, rider=rider)
        da, dwin = linear_bwd("ret_in_b", a, w['w_in'], dz)
        dh2, dg = norm_bwd("ret_norm_b", h, g, da, dh)
        return dh2, dict(g=dg, w_in=dwin, gn=dgn, w_o=dwo)

    return h_new, bwd


def loss_head(h, tgt):
    T = h.shape[0]
    tm = _row_tile(T)

    def body(h_ref, t_ref, loss_ref, dh_ref):
        i = pl.program_id(0)
        rows = _row_ids(i, tm, (tm, 1))
        err = jnp.where(rows >= LEAD, h_ref[...] - t_ref[...], 0.0)
        dh_ref[...] = err * (1.0 / D)
        part = jnp.full((8, 128), 0.5 * jnp.sum(jnp.sum(err * err, axis=1, keepdims=True) * (1.0 / D)), F32)

        @pl.when(i == 0)
        def _():
            loss_ref[...] = part

        @pl.when(i != 0)
        def _():
            loss_ref[...] += part

    loss, dh = pl.pallas_call(
        body, grid=(T // tm,),
        in_specs=[pl.BlockSpec((tm, D), lambda i: (i, 0)), pl.BlockSpec((tm, D), lambda i: (i, 0))],
        out_specs=[pl.BlockSpec((8, 128), lambda i: (0, 0)), pl.BlockSpec((tm, D), lambda i: (i, 0))],
        out_shape=[jax.ShapeDtypeStruct((8, 128), F32), jax.ShapeDtypeStruct((T, D), F32)], name="loss_head",
        compiler_params=_cparams())(h, tgt)
    return loss[0, 0], dh


def _tables(T):
    pos = jnp.maximum(jnp.arange(T, dtype=jnp.int32) - PAD, 0).astype(F32)

    def cs(dim):
        inv_freq = 1.0 / (10000.0 ** (jnp.arange(0, dim, 2, dtype=F32) / dim))
        ang = pos[:, None] * inv_freq[None, :]
        return jnp.cos(ang), jnp.sin(ang)

    ca, sa = cs(MLA_ROPE)
    zeros = jnp.zeros((T, 64), F32)
    cd, sd = cs(RET_DK)
    log_gamma = jnp.log(1.0 - jnp.exp2(-5.0 - jnp.arange(RET_H, dtype=F32)))
    p = jnp.arange(CHUNK, dtype=F32)
    diff = p[:, None] - p[None, :]
    dmat = jnp.where(diff >= 0, jnp.exp(diff[None] * log_gamma[:, None, None]), 0.0)
    return dict(cos_a=jnp.concatenate([ca, ca, zeros], axis=1), sin_a=jnp.concatenate([-sa, sa, zeros], axis=1),
                cos_d=cd, sin_d=sd, ret_dmat=dmat,
                ret_qdec=jnp.exp((p[None, :] + 1.0) * log_gamma[:, None])[..., None],
                ret_kdec=jnp.exp((CHUNK - 1.0 - p[None, :]) * log_gamma[:, None])[..., None],
                ret_cdec=jnp.exp(CHUNK * log_gamma)[:, None, None])


def _hgrn_lower_bound(logits):
    lb_cum = jnp.cumsum(jax.nn.softmax(logits, axis=0), axis=0)
    return (lb_cum - lb_cum[0:1])[1:2]


def _uq_to_heads(w):
    t = w.reshape(w.shape[0], MLA_H, MLA_QK)
    return jnp.pad(t, ((0, 0), (0, 0), (0, 256 - MLA_QK))).reshape(w.shape[0], MLA_H * 256)


def _uq_from_heads(g):
    return g.reshape(g.shape[0], MLA_H, 256)[:, :, :MLA_QK].reshape(g.shape[0], MLA_H * MLA_QK)


def _head_interleave(w, widths, heads):
    parts, lo = [], 0
    for wd in widths:
        parts.append(w[:, lo:lo + heads * wd].reshape(w.shape[0], heads, wd))
        lo += heads * wd
    return jnp.concatenate(parts, axis=2).reshape(w.shape[0], -1)


def _head_deinterleave(g, widths, heads):
    t = g.reshape(g.shape[0], heads, sum(widths))
    parts, lo = [], 0
    for wd in widths:
        parts.append(t[:, :, lo:lo + wd].reshape(g.shape[0], heads * wd))
        lo += wd
    return jnp.concatenate(parts, axis=1)


HG_WIDTHS = (128, 128, 128, 128)
RET_WIDTHS = (RET_DK, RET_DK, RET_DV, RET_DV)


def _split_head_gain(g):
    return g[:, :128], jnp.pad(g[:, 128:], ((0, 0), (0, 64)))


def _join_head_gain(dn, dr):
    return jnp.concatenate([dn, dr[:, :64]], axis=1)


def local_step(x, target, W, ex):
    S = x.shape[0]
    T = S + LEAD
    tabs = _tables(T)
    h = jnp.concatenate([jnp.zeros((PAD, D), F32), W['meta_tokens'], x], axis=0)
    tgt = jnp.concatenate([jnp.zeros((LEAD, D), F32), target], axis=0)

    gqn, gqr = _split_head_gain(W['mla_q_head_g'])
    gkn, gkr = _split_head_gain(W['mla_k_head_g'])
    lb, lb_vjp = jax.vjp(_hgrn_lower_bound, W['hgrn_lb_logits'])

    def ffn(i, hh):
        return ffn_layer(i, hh, W['norm_ffn_g'][i:i + 1], ex.weight('ffn_w_up', i), W['ffn_conv_w'][i],
                         W['ffn_conv_b'][i:i + 1], ex.weight('ffn_w_down', i))

    bm, bf = [None] * 4, [None] * 4
    ex.gather(['mla'], name="gather_mla")
    w0 = dict(w_down=jnp.pad(ex.weight('mla_w_down'), ((0, 0), (0, 64))), gcq=W['mla_cq_norm_g'],
              gckv=W['mla_ckv_norm_g'], w_uq=_uq_to_heads(ex.weight('mla_w_uq')), w_ukv=ex.weight('mla_w_ukv'),
              gqn=gqn, gqr=gqr, gkn=gkn, gkr=gkr, w_o=ex.weight('mla_w_o'))
    h, bm[0] = mla_mixer(h, W['norm_mix_g'][0:1], w0, tabs, rider=ex.gather(['ffn0', 'hgrn', 'ffn1']))
    h, bf[0] = ffn(0, h)
    w1 = dict(w_in=_head_interleave(ex.weight('hgrn_w_in'), HG_WIDTHS, HG_H), lb=lb, go=W['hgrn_o_norm_g'],
              w_o=ex.weight('hgrn_w_o'))
    h, bm[1] = hgrn_mixer(h, W['norm_mix_g'][1:2], w1, rider=ex.gather(['s5', 'ffn2']))
    h, bf[1] = ffn(1, h)
    w2 = dict(lam_re=W['s5_lam_re'][0], lam_im=W['s5_lam_im'][0], log_dt=W['s5_log_dt'][0], b_re=W['s5_b_re'][0],
              b_im=W['s5_b_im'][0], c_re=W['s5_c_re'][0], c_im=W['s5_c_im'][0], dskip=W['s5_d'],
              w_glu=ex.weight('s5_w_glu'))
    h, bm[2] = s5_mixer(h, W['norm_mix_g'][2:3], w2, rider=ex.gather(['ret']))
    h, bf[2] = ffn(2, h)
    w3 = dict(w_in=_head_interleave(ex.weight('ret_w_in'), RET_WIDTHS, RET_H), gn=W['ret_gn_g'],
              w_o=ex.weight('ret_w_o'))
    h, bm[3] = ret_mixer(h, W['norm_mix_g'][3:4], w3, tabs, rider=ex.gather(['ffn3']))
    h, bf[3] = ffn(3, h)

    loss, dh = loss_head(h, tgt)

    def ffn_grads(i, g):
        return {('ffn_w_up', i): g['w_up'], ('ffn_w_down', i): g['w_down']}

    gm, gf = [None] * 4, [None] * 4
    dh, gf[3] = bf[3](dh)
    dh, gm[3] = bm[3](dh, rider=ex.scatter(ffn_grads(3, gf[3])))
    dh, gf[2] = bf[2](dh)
    ret_grads = {('ret_w_in', 0): _head_deinterleave(gm[3]['w_in'], RET_WIDTHS, RET_H), ('ret_w_o', 0): gm[3]['w_o']}
    dh, gm[2] = bm[2](dh, rider=ex.scatter(ffn_grads(2, gf[2])))
    dh, gf[1] = bf[1](dh)
    dh, gm[1] = bm[1](dh, rider=ex.scatter(ffn_grads(1, gf[1])))
    dh, gf[0] = bf[0](dh)
    hgrn_grads = {('hgrn_w_in', 0): _head_deinterleave(gm[1]['w_in'], HG_WIDTHS, HG_H), ('hgrn_w_o', 0): gm[1]['w_o']}
    G = {}
    G['norm_ffn_g'] = jnp.concatenate([gf[i]['g'] for i in range(4)], axis=0)
    G['hgrn_lb_logits'] = lb_vjp(gm[1]['lb'])[0]
    G['hgrn_o_norm_g'] = gm[1]['go']
    for n in ('lam_re', 'lam_im', 'log_dt', 'b_re', 'b_im', 'c_re', 'c_im'):
        G['s5_' + n] = gm[2][n][None]
    G['s5_d'] = gm[2]['dskip']
    G['ret_gn_g'] = gm[3]['gn']
    G['ffn_conv_w'] = jnp.stack([gf[i]['cw'] for i in range(4)])
    G['ffn_conv_b'] = jnp.concatenate([gf[i]['cb'] for i in range(4)], axis=0)
    early = ex.all_devices(_pack([G[n] for n in SMALL_EARLY], F32, 8))

    dh, gm[0] = bm[0](dh, rider=[ex.scatter({('s5_w_glu', 0): gm[2]['w_glu'], **hgrn_grads, **ffn_grads(0, gf[0])}),
                                 ex.swap()], rider_dq=[early, ex.scatter(ret_grads)])
    a = gm[0]
    mla_grads = {('mla_w_down', 0): a['w_down'][:, :MLA_QL + MLA_KVL + MLA_ROPE], ('mla_w_uq', 0): _uq_from_heads(a['w_uq']),
                 ('mla_w_ukv', 0): a['w_ukv'], ('mla_w_o', 0): a['w_o']}
    G['meta_tokens'] = dh[PAD:LEAD]
    G['norm_mix_g'] = jnp.concatenate([gm[i]['g'] for i in range(4)], axis=0)
    G['mla_cq_norm_g'], G['mla_ckv_norm_g'] = a['gcq'], a['gckv']
    G['mla_q_head_g'] = _join_head_gain(a['gqn'], a['gqr'])
    G['mla_k_head_g'] = _join_head_gain(a['gkn'], a['gkr'])
    ex.tail = [ex.scatter(mla_grads), ex.all_devices(_pack([G[n] for n in SMALL_LATE], F32, 8))]
    return loss, dh[LEAD:], G


PACK_W = 1024
ANY = pl.BlockSpec(memory_space=pl.ANY)


def _pack(arrs, dtype, row_mult):
    flat = jnp.concatenate([a.reshape(-1).astype(dtype) for a in arrs])
    n = flat.shape[0]
    rows = -(-n // (PACK_W * row_mult)) * row_mult
    return jnp.pad(flat, (0, rows * PACK_W - n)).reshape(rows, PACK_W)


def _unpack(buf, shapes):
    flat = buf.reshape(-1)
    out, off = [], 0
    for s in shapes:
        n = math.prod(s)
        out.append(flat[off:off + n].reshape(s))
        off += n
    return out


def _my_pos():
    return lax.axis_index("x"), lax.axis_index("y"), lax.axis_index("c")


def _other_chips(x, y):
    return [(1 - x, y), (x, 1 - y), (1 - x, 1 - y)]


def gather_chips(name, src):
    def body(src_ref, out_ref, send_sems, recv_sems, local_sem):
        x, y, c = _my_pos()
        q = 2 * x + y
        mine = pltpu.make_async_copy(src_ref, out_ref.at[q], local_sem)
        mine.start()
        peers = _other_chips(x, y)

        def copy(k, slot, peer):
            return pltpu.make_async_remote_copy(src_ref=src_ref, dst_ref=out_ref.at[slot], send_sem=send_sems.at[k],
                                                recv_sem=recv_sems.at[k], device_id=(peer[0], peer[1], c),
                                                device_id_type=MESH_ID)
        sends = [copy(k, q, p) for k, p in enumerate(peers)]
        for cp in sends:
            cp.start()
        for k, p in enumerate(peers):
            copy(k, 2 * p[0] + p[1], p).wait_recv()
        for cp in sends:
            cp.wait_send()
        mine.wait()

    return pl.pallas_call(body, out_shape=jax.ShapeDtypeStruct((4,) + src.shape, src.dtype), in_specs=[ANY],
                          out_specs=ANY, name=name,
                          scratch_shapes=[pltpu.SemaphoreType.DMA((3,)), pltpu.SemaphoreType.DMA((3,)),
                                          pltpu.SemaphoreType.DMA(())])(src)


def _pack_tile(rows):
    return _divisors(rows, (256, 128, 64, 32, 16, 8))[0] if rows > 512 else rows


def sum_slots(name, slots):
    n, rows, w = slots.shape
    tr = _pack_tile(rows)

    def body(s_ref, o_ref):
        acc = s_ref[0].astype(F32)
        for k in range(1, n):
            acc = acc + s_ref[k].astype(F32)
        o_ref[...] = acc

    return pl.pallas_call(body, grid=(rows // tr,), in_specs=[pl.BlockSpec((n, tr, w), lambda i: (0, i, 0))],
                          out_specs=pl.BlockSpec((tr, w), lambda i: (i, 0)),
                          out_shape=jax.ShapeDtypeStruct((rows, w), F32), name=name, compiler_params=_cparams())(slots)


def adamw(name, grads, w, m, v):
    rows, wd = w.shape
    tr = _pack_tile(rows)
    ng = len(grads)

    def body(*refs):
        g = refs[0][...]
        for r in refs[1:ng]:
            g = g + r[...]
        w_ref, m_ref, v_ref = refs[ng:ng + 3]
        g_out, d_out, m_out, v_out = refs[ng + 3:]
        m_new = ADAM_B1 * m_ref[...] + (1.0 - ADAM_B1) * g
        v_new = ADAM_B2 * v_ref[...] + (1.0 - ADAM_B2) * jnp.square(g)
        m_hat = m_new / (1.0 - ADAM_B1 ** ADAM_STEP)
        v_hat = v_new / (1.0 - ADAM_B2 ** ADAM_STEP)
        g_out[...] = g
        d_out[...] = -ADAM_LR * (m_hat / (jnp.sqrt(v_hat) + ADAM_EPS) + ADAM_WD * w_ref[...])
        m_out[...] = m_new
        v_out[...] = v_new

    spec = pl.BlockSpec((tr, wd), lambda i: (i, 0))
    shape = jax.ShapeDtypeStruct((rows, wd), F32)
    return pl.pallas_call(body, grid=(rows // tr,), in_specs=[spec] * (ng + 3), out_specs=[spec] * 4,
                          out_shape=[shape] * 4, name=name, compiler_params=_cparams())(*grads, w, m, v)


def _sem_scratch(nw):
    return [pltpu.SemaphoreType.DMA((3 * nw,)), pltpu.SemaphoreType.DMA((3 * nw,)), pltpu.SemaphoreType.DMA((nw,))]


def _block2d(ref, axis, p, n):
    if axis == 0:
        return ref.at[pl.ds(pl.multiple_of(p * n, 16), n), :]
    return ref.at[:, pl.ds(pl.multiple_of(p * n, 128), n)]


class ScatterRider:
    def __init__(self, items):
        self.items = items
        self.operands = [it[0] for it in items]
        self.results = None
        self.out_shapes = [jax.ShapeDtypeStruct((4, arr.shape[0] // 4, arr.shape[1]) if axis == 0 else
                                                (4, arr.shape[0], arr.shape[1] // 4), arr.dtype) for arr, axis in items]
        self.scratch = _sem_scratch(len(items))

    def _copies(self, ins, outs, sems):
        send_sems, recv_sems, local_sems = sems
        x, y, c = _my_pos()
        q = 2 * x + y
        local, sends, lands = [], [], []
        for w, (arr, axis) in enumerate(self.items):
            n = arr.shape[axis] // 4
            local.append(pltpu.make_async_copy(_block2d(ins[w], axis, q, n), outs[w].at[q], local_sems.at[w]))
            for k, (px, py) in enumerate(_other_chips(x, y)):
                p = 2 * px + py
                sems_k = dict(send_sem=send_sems.at[3 * w + k], recv_sem=recv_sems.at[3 * w + k],
                              device_id=(px, py, c), device_id_type=MESH_ID)
                theirs = _block2d(ins[w], axis, p, n)
                sends.append(pltpu.make_async_remote_copy(src_ref=theirs, dst_ref=outs[w].at[q], **sems_k))
                lands.append(pltpu.make_async_remote_copy(src_ref=theirs, dst_ref=outs[w].at[p], **sems_k))
        return local, sends, lands

    def start(self, ins, outs, sems):
        local, sends, _ = self._copies(ins, outs, sems)
        for cp in local + sends:
            cp.start()

    def finish(self, ins, outs, sems):
        local, sends, lands = self._copies(ins, outs, sems)
        for cp in lands:
            cp.wait_recv()
        for cp in sends:
            cp.wait_send()
        for cp in local:
            cp.wait()


class GatherRider:
    def __init__(self, items):
        self.items = items
        self.operands = [it[0] for it in items]
        self.results = None
        self.out_shapes = []
        for arr, _, axis in items:
            r, c = arr.shape[1:]
            assert r % 32 == 0
            self.out_shapes.append(jax.ShapeDtypeStruct((4 * r, c) if axis == 0 else (r, 4 * c), arr.dtype))
        n = len(items)
        dma = pltpu.SemaphoreType.DMA
        self.scratch = [dma((3 * n,)), dma((3 * n,)), dma((n,)), dma((3 * n,)), dma((3 * n,))]

    def _copies(self, ins, outs, sems):
        send_sems, recv_sems, local_sems, pass_send_sems, pass_recv_sems = sems
        x, y, c = _my_pos()
        q = 2 * x + y
        local, sends, lands, passes, pass_lands = [], [], [], [], []
        for w, (arr, layer, axis) in enumerate(self.items):
            r, cols = arr.shape[1:]
            half = r // 2
            src = ins[w].at[layer]

            def part(blk, hc, w=w, axis=axis, r=r, cols=cols, half=half):
                if axis == 0:
                    return outs[w].at[pl.ds(pl.multiple_of(blk * r + hc * half, 16), half), :]
                return outs[w].at[pl.ds(pl.multiple_of(hc * half, 16), half), pl.ds(pl.multiple_of(blk * cols, 128), cols)]

            local.append(pltpu.make_async_copy(src, _block2d(outs[w], axis, q, arr.shape[1 + axis]), local_sems.at[w]))
            for k, (px, py) in enumerate(_other_chips(x, y)):
                p = 2 * px + py
                ici = dict(send_sem=send_sems.at[3 * w + k], recv_sem=recv_sems.at[3 * w + k],
                           device_id=(px, py, c), device_id_type=MESH_ID)
                d2d = dict(send_sem=pass_send_sems.at[3 * w + k], recv_sem=pass_recv_sems.at[3 * w + k],
                           device_id=(x, y, 1 - c), device_id_type=MESH_ID)
                mine = src.at[pl.ds(pl.multiple_of(c * half, 16), half), :]
                sends.append(pltpu.make_async_remote_copy(src_ref=mine, dst_ref=part(q, c), **ici))
                lands.append(pltpu.make_async_remote_copy(src_ref=mine, dst_ref=part(p, c), **ici))
                passes.append(pltpu.make_async_remote_copy(src_ref=part(p, c), dst_ref=part(p, c), **d2d))
                pass_lands.append(pltpu.make_async_remote_copy(src_ref=part(p, c), dst_ref=part(p, 1 - c), **d2d))
        return local, sends, lands, passes, pass_lands

    def start(self, ins, outs, sems):
        local, sends, _, _, _ = self._copies(ins, outs, sems)
        for cp in local + sends:
            cp.start()

    def middle(self, ins, outs, sems):
        _, _, lands, passes, _ = self._copies(ins, outs, sems)
        for land, cp in zip(lands, passes):
            land.wait_recv()
            cp.start()

    def finish(self, ins, outs, sems):
        local, sends, _, passes, pass_lands = self._copies(ins, outs, sems)
        for cp in pass_lands:
            cp.wait_recv()
        for cp in sends + passes:
            cp.wait_send()
        for cp in local:
            cp.wait()


class SwapRider:
    def __init__(self, arrs):
        self.operands = list(arrs)
        self.out_shapes = [jax.ShapeDtypeStruct(a.shape, a.dtype) for a in arrs]
        self.scratch = [pltpu.SemaphoreType.DMA((len(arrs),)), pltpu.SemaphoreType.DMA((len(arrs),))]
        self.results = None

    def _copies(self, ins, outs, sems):
        x, y, c = _my_pos()
        return [pltpu.make_async_remote_copy(src_ref=ins[w], dst_ref=outs[w], send_sem=sems[0].at[w],
                                             recv_sem=sems[1].at[w], device_id=(x, y, 1 - c), device_id_type=MESH_ID)
                for w in range(len(self.operands))]

    def start(self, ins, outs, sems):
        for cp in self._copies(ins, outs, sems):
            cp.start()

    def finish(self, ins, outs, sems):
        for cp in self._copies(ins, outs, sems):
            cp.wait()


class AllDevicesRider:
    def __init__(self, src):
        self.operands = [src]
        self.out_shapes = [jax.ShapeDtypeStruct((8,) + src.shape, src.dtype)]
        self.scratch = [pltpu.SemaphoreType.DMA((7,)), pltpu.SemaphoreType.DMA((7,)), pltpu.SemaphoreType.DMA(())]
        self.results = None

    def _copies(self, ins, outs, sems):
        x, y, c = _my_pos()
        me = 4 * x + 2 * y + c
        local = pltpu.make_async_copy(ins[0], outs[0].at[me], sems[2])
        sends, lands = [], []
        for k, m in enumerate(range(1, 8)):
            peer = ((1 - x) if m & 4 else x, (1 - y) if m & 2 else y, (1 - c) if m & 1 else c)
            sems_k = dict(send_sem=sems[0].at[k], recv_sem=sems[1].at[k], device_id=peer, device_id_type=MESH_ID)
            sends.append(pltpu.make_async_remote_copy(src_ref=ins[0], dst_ref=outs[0].at[me], **sems_k))
            lands.append(pltpu.make_async_remote_copy(src_ref=ins[0], dst_ref=outs[0].at[4 * peer[0] + 2 * peer[1] + peer[2]],
                                                      **sems_k))
        return local, sends, lands

    def start(self, ins, outs, sems):
        local, sends, _ = self._copies(ins, outs, sems)
        for cp in [local] + sends:
            cp.start()

    def finish(self, ins, outs, sems):
        local, sends, lands = self._copies(ins, outs, sems)
        for cp in lands:
            cp.wait_recv()
        for cp in sends:
            cp.wait_send()
        local.wait()


class RiderGroup:
    def __init__(self, riders):
        self.riders = riders
        self.operands = [a for r in riders for a in r.operands]
        self.out_shapes = [s for r in riders for s in r.out_shapes]
        self.scratch = [s for r in riders for s in r.scratch]

    def _split(self, ins, outs, sems):
        for r in self.riders:
            ni, no, ns = len(r.operands), len(r.out_shapes), len(r.scratch)
            yield r, ins[:ni], outs[:no], sems[:ns]
            ins, outs, sems = ins[ni:], outs[no:], sems[ns:]

    def start(self, ins, outs, sems):
        for r, i, o, s in self._split(ins, outs, sems):
            r.start(i, o, s)

    def middle(self, ins, outs, sems):
        for r, i, o, s in self._split(ins, outs, sems):
            if hasattr(r, 'middle'):
                r.middle(i, o, s)

    def finish(self, ins, outs, sems):
        for r, i, o, s in self._split(ins, outs, sems):
            r.finish(i, o, s)

    @property
    def results(self):
        return None

    @results.setter
    def results(self, res):
        for r in self.riders:
            no = len(r.out_shapes)
            r.results, res = list(res[:no]), res[no:]


def run_rider(name, rider):
    n_in, n_out = len(rider.operands), len(rider.out_shapes)

    def body(*refs):
        ins, outs, sems = refs[:n_in], refs[n_in:n_in + n_out], refs[n_in + n_out:]
        rider.start(ins, outs, sems)
        if hasattr(rider, 'middle'):
            rider.middle(ins, outs, sems)
        rider.finish(ins, outs, sems)

    rider.results = list(pl.pallas_call(body, out_shape=rider.out_shapes, in_specs=[ANY] * n_in, out_specs=[ANY] * n_out,
                                        name=name, scratch_shapes=rider.scratch)(*rider.operands))


WEIGHT_GROUPS = {'mla': [('mla_w_down', 0), ('mla_w_uq', 0), ('mla_w_ukv', 0), ('mla_w_o', 0)],
                 'hgrn': [('hgrn_w_in', 0), ('hgrn_w_o', 0)], 's5': [('s5_w_glu', 0)],
                 'ret': [('ret_w_in', 0), ('ret_w_o', 0)]}
WEIGHT_GROUPS.update({f'ffn{i}': [('ffn_w_up', i), ('ffn_w_down', i)] for i in range(4)})


class Exchange:
    def __init__(self, shards=None, full=None):
        self.shards, self.full = shards, dict(full or {})
        self.got, self.recv, self.sib, self.grads, self.small, self.tail = {}, {}, {}, {}, [], []

    def gather(self, groups, name=None):
        if self.shards is None:
            return None
        keys = [k for g in groups for k in WEIGHT_GROUPS[g]]
        rider = GatherRider([(self.shards[n], layer, SHARD_AXIS[n] - 1) for n, layer in keys])
        self.got.update({k: (rider, j) for j, k in enumerate(keys)})
        if name is not None:
            run_rider(name, rider)
        return rider

    def weight(self, n, layer=0):
        if self.shards is None:
            return self.full[n][layer]
        rider, j = self.got[(n, layer)]
        return rider.results[j]

    def scatter(self, grads, name=None):
        if self.shards is None:
            self.grads.update(grads)
            return None
        keys = list(grads)
        rider = ScatterRider([(grads[k], SHARD_AXIS[k[0]] - 1) for k in keys])
        self.recv.update({k: (rider, j) for j, k in enumerate(keys)})
        if name is not None:
            run_rider(name, rider)
        return rider

    def received(self, n, layer):
        rider, j = self.recv[(n, layer)]
        return rider.results[j]

    def swap(self, name=None):
        if self.shards is None:
            return None
        keys = [k for k, (r, _) in self.recv.items() if k not in self.sib and r.results is not None]
        rider = SwapRider([self.received(*k) for k in keys])
        self.sib.update({k: (rider, j) for j, k in enumerate(keys)})
        if name is not None:
            run_rider(name, rider)
        return rider

    def sibling(self, n, layer):
        rider, j = self.sib[(n, layer)]
        return rider.results[j]

    def all_devices(self, packed, name=None):
        if self.shards is None:
            return None
        rider = AllDevicesRider(packed)
        self.small.append(rider)
        if name is not None:
            run_rider(name, rider)
        return rider


ADAM_BLOCK_ELEMS = 256 * 1024


def adamw_shard(name, mine, sib, w, m, v, first_layer=0, into=None, rider=None):
    _, rows, cols = w.shape
    nl = mine.shape[1]
    tr = [t for t in (512, 384, 352, 256, 176, 128, 64, 32, 16) if rows % t == 0 and t * cols <= ADAM_BLOCK_ELEMS][0]

    def body(a_ref, b_ref, w_ref, m_ref, v_ref, *rest):
        g_out, d_out, m_out, v_out = rest[-4:]

        def total(r):
            acc = r[0].astype(F32)
            for k in range(1, 4):
                acc = acc + r[k].astype(F32)
            return acc
        g = total(a_ref) + total(b_ref)
        m_new = ADAM_B1 * m_ref[...] + (1.0 - ADAM_B1) * g
        v_new = ADAM_B2 * v_ref[...] + (1.0 - ADAM_B2) * jnp.square(g)
        m_hat = m_new / (1.0 - ADAM_B1 ** ADAM_STEP)
        v_hat = v_new / (1.0 - ADAM_B2 ** ADAM_STEP)
        g_out[...] = g
        d_out[...] = -ADAM_LR * (m_hat / (jnp.sqrt(v_hat) + ADAM_EPS) + ADAM_WD * w_ref[...])
        m_out[...] = m_new
        v_out[...] = v_new

    slots = pl.BlockSpec((4, None, tr, cols), lambda l, i: (0, l, i, 0))
    spec = pl.BlockSpec((None, tr, cols), lambda l, i: (l + first_layer, i, 0))
    shape = jax.ShapeDtypeStruct(w.shape, F32)
    in_specs, operands, aliases = [slots, slots, spec, spec, spec], [mine, sib, w, m, v], {}
    if into is not None:
        in_specs += [ANY] * 4
        operands += list(into)
        aliases = {5 + k: k for k in range(4)}
    return _pallas(name, body, (nl, rows // tr), in_specs, [spec] * 4, [shape] * 4, [], operands, rider, aliases)


def kernel(x, meta_tokens, norm_mix_g, norm_ffn_g, mla_w_down, mla_cq_norm_g, mla_ckv_norm_g, mla_w_uq, mla_w_ukv, mla_q_head_g, mla_k_head_g, mla_w_o, hgrn_w_in, hgrn_lb_logits, hgrn_o_norm_g, hgrn_w_o, s5_lam_re, s5_lam_im, s5_log_dt, s5_b_re, s5_b_im, s5_c_re, s5_c_im, s5_d, s5_w_glu, ret_w_in, ret_gn_g, ret_w_o, ffn_w_up, ffn_conv_w, ffn_conv_b, ffn_w_down, loss_target, m_meta_tokens, m_norm_mix_g, m_norm_ffn_g, m_mla_w_down, m_mla_cq_norm_g, m_mla_ckv_norm_g, m_mla_w_uq, m_mla_w_ukv, m_mla_q_head_g, m_mla_k_head_g, m_mla_w_o, m_hgrn_w_in, m_hgrn_lb_logits, m_hgrn_o_norm_g, m_hgrn_w_o, m_s5_lam_re, m_s5_lam_im, m_s5_log_dt, m_s5_b_re, m_s5_b_im, m_s5_c_re, m_s5_c_im, m_s5_d, m_s5_w_glu, m_ret_w_in, m_ret_gn_g, m_ret_w_o, m_ffn_w_up, m_ffn_conv_w, m_ffn_conv_b, m_ffn_w_down, v_meta_tokens, v_norm_mix_g, v_norm_ffn_g, v_mla_w_down, v_mla_cq_norm_g, v_mla_ckv_norm_g, v_mla_w_uq, v_mla_w_ukv, v_mla_q_head_g, v_mla_k_head_g, v_mla_w_o, v_hgrn_w_in, v_hgrn_lb_logits, v_hgrn_o_norm_g, v_hgrn_w_o, v_s5_lam_re, v_s5_lam_im, v_s5_log_dt, v_s5_b_re, v_s5_b_im, v_s5_c_re, v_s5_c_im, v_s5_d, v_s5_w_glu, v_ret_w_in, v_ret_gn_g, v_ret_w_o, v_ffn_w_up, v_ffn_conv_w, v_ffn_conv_b, v_ffn_w_down):
    vals = (x, meta_tokens, norm_mix_g, norm_ffn_g, mla_w_down, mla_cq_norm_g, mla_ckv_norm_g, mla_w_uq, mla_w_ukv, mla_q_head_g, mla_k_head_g, mla_w_o, hgrn_w_in, hgrn_lb_logits, hgrn_o_norm_g, hgrn_w_o, s5_lam_re, s5_lam_im, s5_log_dt, s5_b_re, s5_b_im, s5_c_re, s5_c_im, s5_d, s5_w_glu, ret_w_in, ret_gn_g, ret_w_o, ffn_w_up, ffn_conv_w, ffn_conv_b, ffn_w_down, loss_target, m_meta_tokens, m_norm_mix_g, m_norm_ffn_g, m_mla_w_down, m_mla_cq_norm_g, m_mla_ckv_norm_g, m_mla_w_uq, m_mla_w_ukv, m_mla_q_head_g, m_mla_k_head_g, m_mla_w_o, m_hgrn_w_in, m_hgrn_lb_logits, m_hgrn_o_norm_g, m_hgrn_w_o, m_s5_lam_re, m_s5_lam_im, m_s5_log_dt, m_s5_b_re, m_s5_b_im, m_s5_c_re, m_s5_c_im, m_s5_d, m_s5_w_glu, m_ret_w_in, m_ret_gn_g, m_ret_w_o, m_ffn_w_up, m_ffn_conv_w, m_ffn_conv_b, m_ffn_w_down, v_meta_tokens, v_norm_mix_g, v_norm_ffn_g, v_mla_w_down, v_mla_cq_norm_g, v_mla_ckv_norm_g, v_mla_w_uq, v_mla_w_ukv, v_mla_q_head_g, v_mla_k_head_g, v_mla_w_o, v_hgrn_w_in, v_hgrn_lb_logits, v_hgrn_o_norm_g, v_hgrn_w_o, v_s5_lam_re, v_s5_lam_im, v_s5_log_dt, v_s5_b_re, v_s5_b_im, v_s5_c_re, v_s5_c_im, v_s5_d, v_s5_w_glu, v_ret_w_in, v_ret_gn_g, v_ret_w_o, v_ffn_w_up, v_ffn_conv_w, v_ffn_conv_b, v_ffn_w_down)
    names = ['x'] + WEIGHTS + ['loss_target'] + ['m_' + n for n in WEIGHTS] + ['v_' + n for n in WEIGHTS]
    A = dict(zip(names, vals))
    q = 2 * lax.axis_index("x") + lax.axis_index("y")

    small_shapes = [A[n].shape for n in SMALL_SHARDED]
    got_small = gather_chips("gather_small", _pack([A[n] for n in SMALL_SHARDED], F32, 8))
    W = {n: A[n] for n in REPLICATED}
    parts_small = [_unpack(got_small[p], small_shapes) for p in range(4)]
    for k, n in enumerate(SMALL_SHARDED):
        W[n] = jnp.concatenate([parts_small[p][k] for p in range(4)], axis=SHARD_AXIS[n])

    ex = Exchange(shards={n: A[n].astype(BF16) for n in BIG})
    loss, grad_x, G = local_step(A['x'][0], A['loss_target'][0], W, ex)
    loss = lax.psum(loss, ("x", "y", "c"))

    def adam(n, layers, **kw):
        return adamw_shard(f"adam_{n}_{layers[0]}", jnp.stack([ex.received(n, layer) for layer in layers], axis=1),
                           jnp.stack([ex.sibling(n, layer) for layer in layers], axis=1),
                           A[n], A['m_' + n], A['v_' + n], first_layer=layers[0], **kw)

    late = {'ffn_w_up': adam('ffn_w_up', (1, 2, 3), rider=ex.tail + [ex.swap()])}
    late['ffn_w_down'] = adam('ffn_w_down', (1, 2, 3), rider=[ex.swap()])
    res_big = [adam(n, (0,), into=late[n]) if n in late else adam(n, (0,)) for n in BIG]

    small_names = REPLICATED + SMALL_SHARDED
    gs = {}
    for part, names, rider in (("early", SMALL_EARLY, ex.small[0]), ("late", SMALL_LATE, ex.small[1])):
        total = sum_slots("grad_small_sum_" + part, rider.results[0])
        gs.update(zip(names, _unpack(total, [G[n].shape for n in names])))
    for n in SMALL_SHARDED:
        ax = SHARD_AXIS[n]
        size = gs[n].shape[ax] // 4
        gs[n] = lax.dynamic_slice_in_dim(gs[n], q * size, size, axis=ax)
    pk = lambda pre: _pack([A[pre + n] for n in small_names], F32, 8)
    own_shapes = [A[n].shape for n in small_names]
    res_small = [_unpack(r, own_shapes) for r in
                 adamw("adam_small", [_pack([gs[n] for n in small_names], F32, 8)], pk(''), pk('m_'), pk('v_'))]

    out = {}
    for j, kind in enumerate(('grad_', 'delta_', 'new_m_', 'new_v_')):
        for k, n in enumerate(BIG):
            out[kind + n] = res_big[k][j]
        for k, n in enumerate(small_names):
            out[kind + n] = res_small[j][k]
    return (loss, grad_x[None]) + tuple(out[kind + n] for kind in ('grad_', 'delta_', 'new_m_', 'new_v_')
                                        for n in WEIGHTS)
```

```python
import functools
import math

import jax
import jax.numpy as jnp
from jax import lax
from jax.experimental import pallas as pl
from jax.experimental.pallas import tpu as pltpu

F32, BF16 = jnp.float32, jnp.bfloat16
HIGHEST = lax.Precision.HIGHEST
MESH_ID = pl.DeviceIdType.MESH

D = 1024
N_META = 16
PAD = 112
LEAD = PAD + N_META
EPS = 1e-6
NEG_INF = -1e30
CHUNK = 64
VMEM_LIMIT_V7X = 56 * 1024 * 1024
MM_VMEM_BUDGET = 36 * 1024 * 1024

MLA_H, MLA_NOPE, MLA_ROPE, MLA_V = 8, 128, 64, 128
MLA_QK = MLA_NOPE + MLA_ROPE
MLA_QL, MLA_KVL = 384, 256
HG_H, HG_D, HG_C = 8, 128, 16
S5_G, S5_P, S5_K = 64, 64, 16
RET_H, RET_DK, RET_DV = 4, 256, 512
FFN_F = 2816

ADAM_LR, ADAM_B1, ADAM_B2, ADAM_EPS, ADAM_WD, ADAM_STEP = 0.001, 0.9, 0.999, 1e-08, 0.01, 10

WEIGHTS = ['meta_tokens', 'norm_mix_g', 'norm_ffn_g', 'mla_w_down', 'mla_cq_norm_g', 'mla_ckv_norm_g', 'mla_w_uq',
           'mla_w_ukv', 'mla_q_head_g', 'mla_k_head_g', 'mla_w_o', 'hgrn_w_in', 'hgrn_lb_logits', 'hgrn_o_norm_g',
           'hgrn_w_o', 's5_lam_re', 's5_lam_im', 's5_log_dt', 's5_b_re', 's5_b_im', 's5_c_re', 's5_c_im', 's5_d',
           's5_w_glu', 'ret_w_in', 'ret_gn_g', 'ret_w_o', 'ffn_w_up', 'ffn_conv_w', 'ffn_conv_b', 'ffn_w_down']
SHARD_AXIS = {'meta_tokens': 1, 'mla_w_down': 1, 'mla_w_uq': 2, 'mla_w_ukv': 2, 'mla_w_o': 1, 'hgrn_w_in': 2,
              'hgrn_w_o': 1, 's5_d': 1, 's5_w_glu': 2, 'ret_w_in': 2, 'ret_gn_g': 1, 'ret_w_o': 1, 'ffn_w_up': 2,
              'ffn_conv_w': 2, 'ffn_w_down': 1}
BIG = ['mla_w_down', 'mla_w_uq', 'mla_w_ukv', 'mla_w_o', 'hgrn_w_in', 'hgrn_w_o', 's5_w_glu', 'ret_w_in', 'ret_w_o',
       'ffn_w_up', 'ffn_w_down']
SMALL_SHARDED = ['meta_tokens', 's5_d', 'ret_gn_g', 'ffn_conv_w']
REPLICATED = [n for n in WEIGHTS if n not in SHARD_AXIS]
SMALL_LATE = ['meta_tokens', 'norm_mix_g', 'mla_cq_norm_g', 'mla_ckv_norm_g', 'mla_q_head_g', 'mla_k_head_g']
SMALL_EARLY = [n for n in REPLICATED + SMALL_SHARDED if n not in SMALL_LATE]


def _cparams():
    return pltpu.CompilerParams(vmem_limit_bytes=VMEM_LIMIT_V7X)


def _dg(a, b, ca, cb):
    return lax.dot_general(a.astype(BF16), b.astype(BF16), (((ca,), (cb,)), ((), ())),
                           preferred_element_type=F32)


@jax.custom_vjp
def mm_nn(a, b):
    return _dg(a, b, 1, 0)


@jax.custom_vjp
def mm_nt(a, b):
    return _dg(a, b, 1, 1)


@jax.custom_vjp
def mm_tn(a, b):
    return _dg(a, b, 0, 0)


mm_nn.defvjp(lambda a, b: (mm_nn(a, b), (a, b)),
             lambda r, g: (mm_nt(g, r[1]).astype(r[0].dtype), mm_tn(r[0], g).astype(r[1].dtype)))
mm_nt.defvjp(lambda a, b: (mm_nt(a, b), (a, b)),
             lambda r, g: (mm_nn(g, r[1]).astype(r[0].dtype), mm_tn(g, r[0]).astype(r[1].dtype)))
mm_tn.defvjp(lambda a, b: (mm_tn(a, b), (a, b)),
             lambda r, g: (mm_nt(r[1], g).astype(r[0].dtype), mm_nn(r[0], g).astype(r[1].dtype)))


def _dot_f32(a, b):
    return jnp.dot(a, b, precision=HIGHEST, preferred_element_type=F32)


def _shift_rows(x, s, up):
    n = x.shape[0]
    r = lax.broadcasted_iota(jnp.int32, x.shape, 0)
    if up:
        return jnp.where(r < n - s, pltpu.roll(x, n - s, 0), 0.0)
    return jnp.where(r >= s, pltpu.roll(x, s, 0), 0.0)


@functools.partial(jax.custom_vjp, nondiff_argnums=(1,))
def shift_down(x, s):
    return _shift_rows(x, s, False)


shift_down.defvjp(lambda x, s: (_shift_rows(x, s, False), None), lambda s, _, g: (_shift_rows(g, s, True),))


@functools.partial(jax.custom_vjp, nondiff_argnums=(1,))
def shift_up(x, s):
    return _shift_rows(x, s, True)


shift_up.defvjp(lambda x, s: (_shift_rows(x, s, True), None), lambda s, _, g: (_shift_rows(g, s, False),))


def _swap32_impl(x):
    ax = x.ndim - 1
    lane = lax.broadcasted_iota(jnp.int32, x.shape, ax)
    return jnp.where(lane < 32, pltpu.roll(x, 96, ax), jnp.where(lane < 64, pltpu.roll(x, 32, ax), 0.0))


@jax.custom_vjp
def swap32(x):
    return _swap32_impl(x)


swap32.defvjp(lambda x: (_swap32_impl(x), None), lambda _, g: (_swap32_impl(g),))


def _rms(x, g):
    return x * lax.rsqrt(jnp.mean(x * x, axis=-1, keepdims=True) + EPS) * g


def _silu(x):
    return x * jax.nn.sigmoid(x)


def _row_ids(pid, n, shape, axis=0):
    return pid * n + lax.broadcasted_iota(jnp.int32, shape, axis)


class Arg:
    def __init__(self, arr, block, imap, diff=True, gdtype=F32):
        self.arr, self.block, self.imap, self.diff, self.gdtype = arr, block, imap, diff, gdtype


class Out:
    def __init__(self, shape, dtype, block, imap):
        self.shape, self.dtype, self.block, self.imap = shape, dtype, block, imap


def _free_axes(imap, grid):
    ng = len(grid)
    base = tuple(imap(*([0] * ng)))
    free = []
    for ax in range(ng):
        p = [0] * ng
        p[ax] = 1
        if grid[ax] > 1 and tuple(imap(*p)) == base:
            free.append(ax)
    walked = [ax for ax in range(ng) if grid[ax] > 1]
    assert free == walked[len(walked) - len(free):], "revisited blocks must be revisited on the innermost axes"
    return free


def _pallas(name, body, grid, in_specs, out_specs, out_shape, scratch, operands, rider=None, aliases=None):
    aliases = aliases or {}
    if isinstance(rider, (list, tuple)):
        riders = [r for r in rider if r is not None]
        rider = RiderGroup(riders) if riders else None
    if rider is None:
        return pl.pallas_call(body, grid=grid, in_specs=in_specs, out_specs=out_specs, out_shape=out_shape,
                              scratch_shapes=scratch, name=name, input_output_aliases=aliases,
                              compiler_params=_cparams())(*operands)
    n_in, n_out, n_sc = len(in_specs), len(out_specs), len(scratch)
    r_in, r_out = len(rider.operands), len(rider.out_shapes)

    def body_with_rider(*refs):
        ins, refs = refs[:n_in], refs[n_in:]
        r_ins, refs = refs[:r_in], refs[r_in:]
        outs, refs = refs[:n_out], refs[n_out:]
        r_outs, refs = refs[:r_out], refs[r_out:]
        sc, r_sc = refs[:n_sc], refs[n_sc:]
        pids = [pl.program_id(a) for a in range(len(grid))]
        first = functools.reduce(jnp.logical_and, [p == 0 for p in pids])
        last = functools.reduce(jnp.logical_and, [p == g - 1 for p, g in zip(pids, grid)])

        @pl.when(first)
        def _():
            rider.start(r_ins, r_outs, r_sc)

        if hasattr(rider, 'middle'):
            step = functools.reduce(lambda acc, pg: acc * pg[1] + pg[0], zip(pids, grid), 0)

            @pl.when(step == (math.prod(grid) * 5) // 6)
            def _():
                rider.middle(r_ins, r_outs, r_sc)

        body(*ins, *outs, *sc)

        @pl.when(last)
        def _():
            rider.finish(r_ins, r_outs, r_sc)

    res = pl.pallas_call(body_with_rider, grid=grid, in_specs=list(in_specs) + [ANY] * r_in,
                         out_specs=list(out_specs) + [ANY] * r_out, out_shape=list(out_shape) + rider.out_shapes,
                         scratch_shapes=list(scratch) + rider.scratch, name=name, input_output_aliases=aliases,
                         compiler_params=_cparams())(*operands, *rider.operands)
    rider.results = list(res[n_out:])
    return res[:n_out]


def stage_fwd(name, fn, grid, args, outs, state_shape=None, rider=None):
    n_in, n_out, ng = len(args), len(outs), len(grid)

    def body(*refs):
        pids = tuple(pl.program_id(a) for a in range(ng))
        vals = [r[...] for r in refs[:n_in]]
        o_refs = refs[n_in:n_in + n_out]
        if state_shape is None:
            res = fn(pids, *vals)
        else:
            sv_ref, st_ref = refs[n_in + n_out], refs[n_in + n_out + 1]

            @pl.when(pids[-1] == 0)
            def _():
                st_ref[...] = jnp.zeros(state_shape, F32)

            s = st_ref[...]
            sv_ref[...] = s
            res = fn(pids, *vals, s)
            st_ref[...] = res[-1]
            res = res[:-1]
        for r, v in zip(o_refs, res):
            r[...] = v.astype(r.dtype)

    in_specs = [pl.BlockSpec(a.block, a.imap) for a in args]
    out_specs = [pl.BlockSpec(o.block, o.imap) for o in outs]
    out_shape = [jax.ShapeDtypeStruct(o.shape, o.dtype) for o in outs]
    scratch = []
    if state_shape is not None:
        nz = len(state_shape)
        out_specs.append(pl.BlockSpec((None, None) + tuple(state_shape), lambda i, j: (i, j) + (0,) * nz))
        out_shape.append(jax.ShapeDtypeStruct(tuple(grid) + tuple(state_shape), F32))
        scratch = [pltpu.VMEM(state_shape, F32)]
    return _pallas(name, body, grid, in_specs, out_specs, out_shape, scratch, [a.arr for a in args], rider)


def stage_bwd(name, fn, grid, args, outs, cots, state_shape=None, states=None, rider=None):
    n_in, n_out, ng = len(args), len(outs), len(grid)
    nb = grid[-1]
    rev = state_shape is not None
    didx = [k for k, a in enumerate(args) if a.diff]
    frees = [_free_axes(args[k].imap, grid) for k in didx]

    def eff(p):
        return tuple(p[:-1]) + (nb - 1 - p[-1],) if rev else tuple(p)

    def wrap(imap):
        return lambda *p: imap(*eff(p))

    def body(*refs):
        pids = tuple(pl.program_id(a) for a in range(ng))
        e = eff(pids)
        vals = [r[...] for r in refs[:n_in]]
        cts = tuple(r[...].astype(F32) for r in refs[n_in:n_in + n_out])
        pos = n_in + n_out
        if rev:
            st_in_ref = refs[pos]
            pos += 1
        g_refs = refs[pos:pos + len(didx)]
        pos += len(didx)
        dvals = [vals[k].astype(F32) for k in didx]

        def f(*dv):
            full = list(vals)
            for k, v in zip(didx, dv[:len(didx)]):
                full[k] = v
            return tuple(fn(e, *full, *dv[len(didx):]))

        if rev:
            ds_ref = refs[pos]

            @pl.when(pids[-1] == 0)
            def _():
                ds_ref[...] = jnp.zeros(state_shape, F32)

            _, vjp = jax.vjp(f, *dvals, st_in_ref[...])
            grads = vjp(cts + (ds_ref[...],))
            ds_ref[...] = grads[-1]
            grads = grads[:-1]
        else:
            _, vjp = jax.vjp(f, *dvals)
            grads = vjp(cts)
        for gref, g, free in zip(g_refs, grads, frees):
            g = g.astype(F32)
            if not free:
                gref[...] = g.astype(gref.dtype)
            else:
                first = functools.reduce(jnp.logical_and, [pids[ax] == 0 for ax in free])

                @pl.when(first)
                def _():
                    gref[...] = g

                @pl.when(jnp.logical_not(first))
                def _():
                    gref[...] += g

    in_specs = [pl.BlockSpec(a.block, wrap(a.imap)) for a in args]
    in_specs += [pl.BlockSpec(o.block, wrap(o.imap)) for o in outs]
    operands = [a.arr for a in args] + list(cots)
    scratch = []
    if rev:
        nz = len(state_shape)
        in_specs.append(pl.BlockSpec((None, None) + tuple(state_shape), lambda i, j: (i, nb - 1 - j) + (0,) * nz))
        operands.append(states)
        scratch = [pltpu.VMEM(state_shape, F32)]
    out_specs = [pl.BlockSpec(args[k].block, wrap(args[k].imap)) for k in didx]
    assert all(args[k].gdtype == F32 or not free for k, free in zip(didx, frees))
    out_shape = [jax.ShapeDtypeStruct(args[k].arr.shape, args[k].gdtype) for k in didx]
    return _pallas(name, body, grid, in_specs, out_specs, out_shape, scratch, operands, rider)


def _divisors(n, cands):
    return [c for c in cands if n % c == 0] or [n]


def _nbytes(dt):
    return jnp.dtype(dt).itemsize


def matmul(name, a, b, mode, out_dtype=F32, res=None, mask=False, res_mask=True, window=None, into=None):
    sa, sb, so = _nbytes(a.dtype), _nbytes(b.dtype), _nbytes(out_dtype)
    off, width = (window[0], window[1]) if window is not None else (0, None)
    if mode in ('nn', 'nt'):
        M, K = a.shape
        N = (width or b.shape[1]) if mode == 'nn' else b.shape[0]
        assert mode == 'nn' or width is None or width == K
        best = None
        for tm in _divisors(M, (1408, 1056, 768, 384, 128)):
            for tn in _divisors(N, (1408, 1024, 768, 512, 384, 256, 128)):
                est = 2 * (tm * K * sa + tn * K * sb + tm * tn * (so + (4 if res is not None else 0)))
                if est <= MM_VMEM_BUDGET and (best is None or tm * tn > best[0] * best[1]):
                    best = (tm, tn)
        tm, tn = best
        grid = (M // tm, N // tn)

        def body(*refs):
            a_ref, b_ref = refs[0], refs[1]
            o_ref = refs[-1]
            x = a_ref[...]
            rows = _row_ids(pl.program_id(0), tm, (tm, 1))
            if mask:
                x = jnp.where(rows >= PAD, x, jnp.zeros_like(x))
            acc = _dg(x, b_ref[...], 1, 0 if mode == 'nn' else 1)
            if res is not None:
                acc = refs[2][...] + (jnp.where(rows >= PAD, acc, 0.0) if res_mask else acc)
            o_ref[...] = acc.astype(o_ref.dtype)

        assert off % (tn if mode == 'nn' else K) == 0
        cb, kb = off // tn, off // K
        in_specs = [pl.BlockSpec((tm, K), lambda i, j: (i, 0)),
                    pl.BlockSpec((K, tn), lambda i, j: (0, j + cb)) if mode == 'nn' else
                    pl.BlockSpec((tn, K), lambda i, j: (j, kb))]
        ops = [a, b]
        if res is not None:
            in_specs.append(pl.BlockSpec((tm, tn), lambda i, j: (i, j)))
            ops.append(res)
        return pl.pallas_call(body, grid=grid, in_specs=in_specs,
                              out_specs=pl.BlockSpec((tm, tn), lambda i, j: (i, j)),
                              out_shape=jax.ShapeDtypeStruct((M, N), out_dtype), name=name,
                              compiler_params=_cparams())(*ops)
    assert mode == 'tn' and res is None
    M, K = a.shape
    N = b.shape[1]
    best = None
    for tk in _divisors(K, (1408, 1024, 768, 512, 384, 256, 128)):
        for tn in _divisors(N, (1408, 1024, 768, 512, 384, 256, 128)):
            est = 2 * (M * tk * sa + M * tn * sb + tk * tn * so)
            if est <= MM_VMEM_BUDGET and (best is None or tk * tn > best[0] * best[1]):
                best = (tk, tn)
    tk, tn = best

    def body_t(*refs):
        a_ref, b_ref, o_ref = refs[0], refs[1], refs[-1]
        y = b_ref[...]
        if mask:
            rows = lax.broadcasted_iota(jnp.int32, (M, 1), 0)
            y = jnp.where(rows >= PAD, y, jnp.zeros_like(y))
        o_ref[...] = _dg(a_ref[...], y, 0, 0).astype(o_ref.dtype)

    assert off % tn == 0
    cb = off // tn
    total = window[2] if window is not None else N
    in_specs = [pl.BlockSpec((M, tk), lambda i, j: (0, i)), pl.BlockSpec((M, tn), lambda i, j: (0, j))]
    ops, alias = [a, b], {}
    if into is not None:
        in_specs.append(ANY)
        ops.append(into)
        alias = {2: 0}
    return pl.pallas_call(body_t, grid=(K // tk, N // tn), in_specs=in_specs,
                          out_specs=pl.BlockSpec((tk, tn), lambda i, j: (i, j + cb)),
                          out_shape=jax.ShapeDtypeStruct((K, total), out_dtype), name=name,
                          input_output_aliases=alias, compiler_params=_cparams())(*ops)


def linear_bwd(name, act, w, dy, mask=False, da_dtype=F32):
    return (matmul(name + "_da", dy, w, 'nt', out_dtype=da_dtype, mask=mask),
            matmul(name + "_dw", act, dy, 'tn', out_dtype=BF16, mask=mask))


def _row_tile(T):
    return _divisors(T, (384, 128))[0]


def _rows(arr, tm, gdtype=F32):
    return Arg(arr, (tm, arr.shape[1]), lambda i: (i, 0), gdtype=gdtype)


def _const(arr, diff=True):
    return Arg(arr, arr.shape, lambda *p: (0,) * arr.ndim, diff)


def _norm_fn(pids, h, g):
    return (_rms(h, g),)


def _norm_bwd_fn(pids, h, g):
    return (_rms(h, g), h)


def norm_fwd(name, h, g, dtype):
    T = h.shape[0]
    tm = _row_tile(T)
    return stage_fwd(name, _norm_fn, (T // tm,), [_rows(h, tm), _const(g)],
                     [Out((T, D), dtype, (tm, D), lambda i: (i, 0))])[0]


def norm_bwd(name, h, g, da, dh):
    T = h.shape[0]
    tm = _row_tile(T)
    o = Out((T, D), F32, (tm, D), lambda i: (i, 0))
    return stage_bwd(name, _norm_bwd_fn, (T // tm,), [_rows(h, tm), _const(g)], [o, o], [da, dh])


def _causal_conv3(u, cw, cb):
    return cw[2:3] * u + cw[1:2] * shift_down(u, 1) + cw[0:1] * shift_down(u, 2) + cb


def _ffn_act_fn(pids, ug, uv, cwg, cwv, cbg, cbv):
    return (_silu(_causal_conv3(ug, cwg, cbg)) * _causal_conv3(uv, cwv, cbv),)


def _ffn_act_args(ug, uv, cw, cb):
    T = ug.shape[0]
    col = lambda j: (0, j)
    args = [Arg(ug, (T, 128), col, gdtype=BF16), Arg(uv, (T, 128), col, gdtype=BF16),
            Arg(cw[:, :FFN_F], (3, 128), col), Arg(cw[:, FFN_F:], (3, 128), col),
            Arg(cb[:, :FFN_F], (1, 128), col), Arg(cb[:, FFN_F:], (1, 128), col)]
    outs = [Out((T, FFN_F), BF16, (T, 128), col)]
    return (FFN_F // 128,), args, outs


def _interleave_cols(w, n_parts, tile=128):
    lead = w.shape[:-1]
    n = w.shape[-1] // (n_parts * tile)
    k = len(lead)
    return w.reshape(lead + (n_parts, n, tile)).transpose(tuple(range(k)) + (k + 1, k, k + 2)).reshape(w.shape)


def _deinterleave_cols(w, n_parts, tile=128):
    lead = w.shape[:-1]
    n = w.shape[-1] // (n_parts * tile)
    k = len(lead)
    return w.reshape(lead + (n, n_parts, tile)).transpose(tuple(range(k)) + (k + 1, k, k + 2)).reshape(w.shape)


def ffn_layer(i, h, g, w_up, cw, cb, w_down):
    gate_w, val_w = (0, FFN_F, 2 * FFN_F), (FFN_F, FFN_F, 2 * FFN_F)
    b = norm_fwd(f"ffn{i}_norm", h, g, BF16)
    ug = matmul(f"ffn{i}_up_g", b, w_up, 'nn', out_dtype=BF16, window=gate_w)
    uv = matmul(f"ffn{i}_up_v", b, w_up, 'nn', out_dtype=BF16, window=val_w)
    grid, args, outs = _ffn_act_args(ug, uv, cw, cb)
    p = stage_fwd(f"ffn{i}_act", _ffn_act_fn, grid, args, outs)[0]
    h_new = matmul(f"ffn{i}_down", p, w_down, 'nn', res=h)

    def bwd(dh):
        dp, dwd = linear_bwd(f"ffn{i}_down_b", p, w_down, dh, mask=True, da_dtype=BF16)
        dug, duv, dcwg, dcwv, dcbg, dcbv = stage_bwd(f"ffn{i}_act_b", _ffn_act_fn, grid, args, outs, [dp])
        db = matmul(f"ffn{i}_up_b_da_g", dug, w_up, 'nt', window=gate_w)
        db = matmul(f"ffn{i}_up_b_da_v", duv, w_up, 'nt', window=val_w, res=db, res_mask=False)
        dwu = matmul(f"ffn{i}_up_b_dw_g", b, dug, 'tn', out_dtype=BF16, window=gate_w)
        dwu = matmul(f"ffn{i}_up_b_dw_v", b, duv, 'tn', out_dtype=BF16, window=val_w, into=dwu)
        dh2, dg = norm_bwd(f"ffn{i}_norm_b", h, g, db, dh)
        return dh2, dict(g=dg, w_up=dwu, cw=jnp.concatenate([dcwg, dcwv], axis=1),
                         cb=jnp.concatenate([dcbg, dcbv], axis=1), w_down=dwd)

    return h_new, bwd


def _mla_latent_fn(pids, down, gcq, gckv):
    cq = _rms(down[:, :MLA_QL], gcq)
    ckv = _rms(down[:, MLA_QL:MLA_QL + MLA_KVL], gckv)
    return cq, ckv, down[:, MLA_QL + MLA_KVL:]


def _rope64(x, cos, sin_signed):
    return x * cos + swap32(x) * sin_signed


def _mla_heads_fn(pids, qraw, kv, kpe, gqn, gqr, gkn, gkr, cos, sin_signed):
    qn, qr = qraw[:, :128], qraw[:, 128:]
    rq = lax.rsqrt((jnp.sum(qn * qn, -1, keepdims=True) + jnp.sum(qr * qr, -1, keepdims=True)) / MLA_QK + EPS)
    q = jnp.concatenate([qn * rq * gqn, _rope64(qr * rq * gqr, cos, sin_signed)], axis=1)
    kn, v = kv[:, :128], kv[:, 128:]
    rk = lax.rsqrt((jnp.sum(kn * kn, -1, keepdims=True) + jnp.sum(kpe * kpe, -1, keepdims=True)) / MLA_QK + EPS)
    k = jnp.concatenate([kn * rk * gkn, _rope64(kpe * rk * gkr, cos, sin_signed)], axis=1)
    return q, k, v


def _chunk_id(r):
    return jnp.where(r < LEAD, 0, 1 + lax.shift_right_arithmetic(r - LEAD, 6))


ATT_T = 384
ATT_SCALE = MLA_QK ** -0.5


def _att_mask(s, qb, kb):
    qrow = _row_ids(qb, ATT_T, (ATT_T, 1))
    krow = _row_ids(kb, ATT_T, (1, ATT_T), axis=1)
    ok = jnp.logical_and(_chunk_id(krow) <= _chunk_id(qrow), krow >= PAD)
    return jnp.where(ok, s, NEG_INF)


def _att_scores(q, k_ref, qb, kb, masked):
    ks = k_ref[pl.ds(pl.multiple_of(kb * ATT_T, ATT_T), ATT_T), :]
    s = _dg(q, ks, 1, 1) * ATT_SCALE
    return (_att_mask(s, qb, kb) if masked else s), ks


def _att_key_loop(j, step, init):
    carry = step(0, init, True)
    n_mid = jnp.maximum(j - 1, 0)
    carry = lax.fori_loop(0, n_mid // 2, lambda i, c: step(2 * i + 2, step(2 * i + 1, c, False), False), carry)
    carry = lax.cond(n_mid % 2 == 1, lambda c: step(j - 1, c, False), lambda c: c, carry)
    return lax.cond(j > 0, lambda c: step(j, c, True), lambda c: c, carry)


def _att_rows(ref, b):
    return ref[pl.ds(pl.multiple_of(b * ATT_T, ATT_T), ATT_T), :]


def attention_fwd(q, k, v, rider=None):
    H, T, _ = q.shape
    nq = T // ATT_T

    def body(q_ref, k_ref, v_ref, o_ref, lse_ref):
        j = pl.program_id(1)
        qv = q_ref[...]

        def step(kb, carry, masked):
            m, l, acc = carry
            s, _ = _att_scores(qv, k_ref, j, kb, masked)
            m_new = jnp.maximum(m, jnp.max(s, axis=-1, keepdims=True))
            p = jnp.exp(s - m_new)
            alpha = jnp.exp(m - m_new)
            return (m_new, alpha * l + jnp.sum(p, axis=-1, keepdims=True),
                    alpha * acc + _dg(p, _att_rows(v_ref, kb), 1, 0))

        init = (jnp.full((ATT_T, 1), NEG_INF, F32), jnp.zeros((ATT_T, 1), F32), jnp.zeros((ATT_T, MLA_V), F32))
        m, l, acc = _att_key_loop(j, step, init)
        o_ref[...] = acc / l
        lse_ref[...] = m + jnp.log(l)

    return _pallas("mla_attn", body, (H, nq),
                   [pl.BlockSpec((None, ATT_T, 256), lambda hh, j: (hh, j, 0)),
                    pl.BlockSpec((None, T, 256), lambda hh, j: (hh, 0, 0)),
                    pl.BlockSpec((None, T, MLA_V), lambda hh, j: (hh, 0, 0))],
                   [pl.BlockSpec((ATT_T, MLA_V), lambda hh, j: (j, hh)),
                    pl.BlockSpec((None, ATT_T, 1), lambda hh, j: (hh, j, 0))],
                   [jax.ShapeDtypeStruct((T, H * MLA_V), F32), jax.ShapeDtypeStruct((H, T, 1), F32)], [],
                   [q, k, v], rider)


def attention_bwd(q, k, v, o, lse, do, rider=None, rider_dq=None):
    H, T, _ = q.shape
    nq = T // ATT_T

    def dq_body(q_ref, k_ref, v_ref, o_ref, do_ref, lse_ref, dq_ref, delta_ref):
        j = pl.program_id(1)
        qv, dov, lsev = q_ref[...], do_ref[...], lse_ref[...]
        delta = jnp.sum(dov * o_ref[...], axis=-1, keepdims=True)
        delta_ref[...] = delta

        def step(kb, acc, masked):
            s, ks = _att_scores(qv, k_ref, j, kb, masked)
            p = jnp.exp(s - lsev)
            ds = p * (_dg(dov, _att_rows(v_ref, kb), 1, 1) - delta) * ATT_SCALE
            return acc + _dg(ds, ks, 1, 0)

        dq_ref[...] = _att_key_loop(j, step, jnp.zeros((ATT_T, 256), F32))

    q_blk = pl.BlockSpec((None, ATT_T, 256), lambda hh, j: (hh, j, 0))
    k_all = pl.BlockSpec((None, T, 256), lambda hh, j: (hh, 0, 0))
    v_all = pl.BlockSpec((None, T, MLA_V), lambda hh, j: (hh, 0, 0))
    o_blk = pl.BlockSpec((ATT_T, MLA_V), lambda hh, j: (j, hh))
    col_blk = pl.BlockSpec((None, ATT_T, 1), lambda hh, j: (hh, j, 0))
    dq, delta = _pallas("mla_attn_dq", dq_body, (H, nq), [q_blk, k_all, v_all, o_blk, o_blk, col_blk],
                        [q_blk, col_blk],
                        [jax.ShapeDtypeStruct((H, T, 256), F32), jax.ShapeDtypeStruct((H, T, 1), F32)], [],
                        [q, k, v, o, do, lse], rider_dq)

    def dkv_body(q_ref, k_ref, v_ref, do_ref, lse_ref, delta_ref, dk_ref, dv_ref):
        kb = pl.program_id(1)
        kv = k_ref[...]
        vv = v_ref[...]

        def step(qb, carry, masked):
            dk, dv = carry
            qv, dov = _att_rows(q_ref, qb), _att_rows(do_ref, qb)
            s = _dg(qv, kv, 1, 1) * ATT_SCALE
            if masked:
                s = _att_mask(s, qb, kb)
            p = jnp.exp(s - _att_rows(lse_ref, qb))
            ds = p * (_dg(dov, vv, 1, 1) - _att_rows(delta_ref, qb)) * ATT_SCALE
            return dk + _dg(ds, qv, 0, 0), dv + _dg(p, dov, 0, 0)

        carry = step(kb, (jnp.zeros((ATT_T, 256), F32), jnp.zeros((ATT_T, MLA_V), F32)), True)

        def later_blocks(c, masked):
            n = nq - 1 - kb
            c = lax.fori_loop(0, n // 2,
                              lambda i, cc: step(kb + 2 + 2 * i, step(kb + 1 + 2 * i, cc, masked), masked), c)
            return lax.cond(n % 2 == 1, lambda cc: step(nq - 1, cc, masked), lambda cc: cc, c)

        dk, dv = lax.cond(kb == 0, lambda c: later_blocks(c, True), lambda c: later_blocks(c, False), carry)
        dk_ref[...] = dk
        dv_ref[...] = dv

    q_all = pl.BlockSpec((None, T, 256), lambda hh, j: (hh, 0, 0))
    v_blk = pl.BlockSpec((None, ATT_T, MLA_V), lambda hh, j: (hh, j, 0))
    do_all = pl.BlockSpec((T, MLA_V), lambda hh, j: (0, hh))
    col_all = pl.BlockSpec((None, T, 1), lambda hh, j: (hh, 0, 0))
    dk, dv = _pallas("mla_attn_dkv", dkv_body, (H, nq), [q_all, q_blk, v_blk, do_all, col_all, col_all],
                     [q_blk, v_blk],
                     [jax.ShapeDtypeStruct((H, T, 256), F32), jax.ShapeDtypeStruct((H, T, MLA_V), F32)], [],
                     [q, k, v, do, lse, delta], rider)
    return dq, dk, dv


def mla_mixer(h, g, w, tabs, rider=None):
    T = h.shape[0]
    tm = _row_tile(T)
    nt = T // tm
    a = norm_fwd("mla_norm", h, g, BF16)
    down = matmul("mla_down", a, w['w_down'], 'nn')
    lat_args = [_rows(down, tm, BF16), _const(w['gcq']), _const(w['gckv'])]
    lat_outs = [Out((T, MLA_QL), BF16, (tm, MLA_QL), lambda i: (i, 0)),
                Out((T, MLA_KVL), BF16, (tm, MLA_KVL), lambda i: (i, 0)),
                Out((T, 128), F32, (tm, 128), lambda i: (i, 0))]
    cq, ckv, kpe = stage_fwd("mla_latent", _mla_latent_fn, (nt,), lat_args, lat_outs)
    qraw = matmul("mla_uq", cq, w['w_uq'], 'nn')
    kv = matmul("mla_ukv", ckv, w['w_ukv'], 'nn')
    hd_args = [Arg(qraw, (tm, 256), lambda i, hh: (i, hh), gdtype=BF16),
               Arg(kv, (tm, 256), lambda i, hh: (i, hh), gdtype=BF16),
               Arg(kpe, (tm, 128), lambda i, hh: (i, 0)),
               _const(w['gqn']), _const(w['gqr']), _const(w['gkn']), _const(w['gkr']),
               Arg(tabs['cos_a'], (tm, 128), lambda i, hh: (i, 0), False),
               Arg(tabs['sin_a'], (tm, 128), lambda i, hh: (i, 0), False)]
    hd_outs = [Out((MLA_H, T, 256), BF16, (None, tm, 256), lambda i, hh: (hh, i, 0)),
               Out((MLA_H, T, 256), BF16, (None, tm, 256), lambda i, hh: (hh, i, 0)),
               Out((MLA_H, T, 128), BF16, (None, tm, 128), lambda i, hh: (hh, i, 0))]
    q, k, v = stage_fwd("mla_heads", _mla_heads_fn, (nt, MLA_H), hd_args, hd_outs)
    o, lse = attention_fwd(q, k, v, rider=rider)
    h_new = matmul("mla_o", o, w['w_o'], 'nn', res=h)

    def bwd(dh, rider=None, rider_dq=None):
        do, dwo = linear_bwd("mla_o_b", o, w['w_o'], dh, mask=True)
        dq, dk, dv = attention_bwd(q, k, v, o, lse, do, rider=rider, rider_dq=rider_dq)
        dqraw, dkv, dkpe, dgqn, dgqr, dgkn, dgkr = stage_bwd("mla_heads_b", _mla_heads_fn, (nt, MLA_H), hd_args,
                                                             hd_outs, [dq, dk, dv])
        dcq, dwuq = linear_bwd("mla_uq_b", cq, w['w_uq'], dqraw)
        dckv, dwukv = linear_bwd("mla_ukv_b", ckv, w['w_ukv'], dkv)
        ddown, dgcq, dgckv = stage_bwd("mla_latent_b", _mla_latent_fn, (nt,), lat_args, lat_outs, [dcq, dckv, dkpe])
        da, dwdown = linear_bwd("mla_down_b", a, w['w_down'], ddown)
        dh2, dg = norm_bwd("mla_norm_b", h, g, da, dh)
        return dh2, dict(g=dg, w_down=dwdown, gcq=dgcq, gckv=dgckv, w_uq=dwuq, w_ukv=dwukv, gqn=dgqn, gqr=dgqr,
                         gkn=dgkn, gkr=dgkr, w_o=dwo)

    return h_new, bwd


HG_R = 384


def _hgrn_fn(pids, z, lb, go, st):
    outs = []
    for lo in range(0, z.shape[0], 128):
        o, st = _hgrn_block(z[lo:lo + 128], lb, go, st)
        outs.append(o)
    return jnp.concatenate(outs, axis=0), st


def _hgrn_block(z, lb, go, st):
    R = z.shape[0]
    zq, zf, zi, zg = z[:, :128], z[:, 128:256], z[:, 256:384], z[:, 384:]
    assert R == 128
    q = _silu(zq)
    fg = lb + (1.0 - lb) * jax.nn.sigmoid(zf)
    logf = jnp.log(fg)
    k = 1.0 - fg
    row = lax.broadcasted_iota(jnp.int32, logf.shape, 0)
    pos = row & (HG_C - 1)
    cum, rev = logf, logf
    for d in (1, 2, 4, 8):
        cum = cum + jnp.where(pos >= d, shift_down(cum, d), 0.0)
        rev = rev + jnp.where(pos < HG_C - d, shift_up(rev, d), 0.0)
    cums, tots = [cum], [cum + rev - logf]
    for s in (16, 32, 64):
        odd = (row & s) != 0
        before = shift_down(tots[-1], s)
        cums.append(cums[-1] + jnp.where(odd, before, 0.0))
        tots.append(tots[-1] + jnp.where(odd, before, shift_up(tots[-1], s)))
    t = lax.broadcasted_iota(jnp.int32, (R, R), 0)
    j = lax.broadcasted_iota(jnp.int32, (R, R), 1)
    sh = lax.shift_right_arithmetic
    a = jnp.where(jnp.logical_and(sh(t, 4) == sh(j, 4), j <= t), mm_nt(q * jnp.exp(cum), k * jnp.exp(-cum)), 0.0)
    for n, s in enumerate((16, 32, 64)):
        m = jnp.logical_and(sh(t, 5 + n) == sh(j, 5 + n), jnp.logical_and((t & s) != 0, (j & s) == 0))
        a = a + jnp.where(m, mm_nt(q * jnp.exp(cums[n]), k * jnp.exp(tots[n] - cums[n])), 0.0)
    o = mm_nn(a, zi) + mm_nt(q * jnp.exp(cums[3]), st)
    st = st * jnp.exp(tots[3][0:1, :]) + mm_tn(zi, k * jnp.exp(tots[3] - cums[3]))
    return _rms(o, go) * _silu(zg), st


def hgrn_mixer(h, g, w, rider=None):
    T = h.shape[0]
    a = norm_fwd("hgrn_norm", h, g, BF16)
    z = matmul("hgrn_in", a, w['w_in'], 'nn')
    grid = (HG_H, T // HG_R)
    args = [Arg(z, (HG_R, 512), lambda hh, j: (j, hh), gdtype=BF16), Arg(w['lb'], (1, 128), lambda hh, j: (0, hh)),
            _const(w['go'])]
    outs = [Out((T, D), BF16, (HG_R, 128), lambda hh, j: (j, hh))]
    o, states = stage_fwd("hgrn_gla", _hgrn_fn, grid, args, outs, state_shape=(HG_D, HG_D), rider=rider)
    h_new = matmul("hgrn_o", o, w['w_o'], 'nn', res=h)

    def bwd(dh, rider=None):
        do, dwo = linear_bwd("hgrn_o_b", o, w['w_o'], dh, mask=True)
        dz, dlb, dgo = stage_bwd("hgrn_gla_b", _hgrn_fn, grid, args, outs, [do], state_shape=(HG_D, HG_D),
                                 states=states, rider=rider)
        da, dwin = linear_bwd("hgrn_in_b", a, w['w_in'], dz)
        dh2, dg = norm_bwd("hgrn_norm_b", h, g, da, dh)
        return dh2, dict(g=dg, w_in=dwin, lb=dlb, go=dgo, w_o=dwo)

    return h_new, bwd


S5_R = 384
S5_W = 512
S5_SLABS = D // 128


def _cmul(ar, ai, br, bi):
    return ar * br - ai * bi, ar * bi + ai * br


def _s5_scan(br, bi, tab, cr, ci, reverse):
    R, W = br.shape
    G = R // 8
    xr, xi = br.reshape(G, 8, W), bi.reshape(G, 8, W)
    for n, d in enumerate((1, 2, 4)):
        sh = (8 - d) if reverse else d
        mr, mi = _cmul(tab[2 * n][None], tab[2 * n + 1][None], pltpu.roll(xr, sh, 1), pltpu.roll(xi, sh, 1))
        xr, xi = xr + mr, xi + mi
    pr, pi = tab[6], tab[7]
    edge = 0 if reverse else 7
    out_r, out_i = [None] * G, [None] * G
    for g in (range(G - 1, -1, -1) if reverse else range(G)):
        ar, ai = _cmul(pr, pi, cr, ci)
        gr, gi = xr[g] + ar, xi[g] + ai
        cr, ci = gr[edge:edge + 1], gi[edge:edge + 1]
        out_r[g], out_i[g] = gr, gi
    return jnp.concatenate(out_r, axis=0), jnp.concatenate(out_i, axis=0), cr, ci


def s5_scan_fwd(a, bb, cb, tab, rider=None):
    T = a.shape[0]
    nb = T // S5_R

    def body(a_ref, bb_ref, cb_ref, tab_ref, y_ref, xs_ref, c_ref):
        @pl.when(pl.program_id(1) == 0)
        def _():
            c_ref[...] = jnp.zeros(c_ref.shape, F32)

        bu = _dg(a_ref[...], bb_ref[...], 1, 0)
        t = tab_ref[...]
        xr, xi, cr, ci = _s5_scan(bu[:, :S5_W], bu[:, S5_W:], t, c_ref[0:1, :S5_W], c_ref[0:1, S5_W:], False)
        x = jnp.concatenate([xr, xi], axis=1)
        xs_ref[...] = x
        y_ref[...] = _dg(x, cb_ref[...], 1, 0)
        c_ref[0:1, :] = jnp.concatenate([cr, ci], axis=1)

    return _pallas(
        "s5_scan", body, (S5_SLABS, nb),
        [pl.BlockSpec((S5_R, 128), lambda j, i: (i, j)),
         pl.BlockSpec((None, 128, 2 * S5_W), lambda j, i: (j, 0, 0)),
         pl.BlockSpec((None, 2 * S5_W, 128), lambda j, i: (j, 0, 0)),
         pl.BlockSpec((None, 10, 8, S5_W), lambda j, i: (j, 0, 0, 0))],
        [pl.BlockSpec((S5_R, 128), lambda j, i: (i, j)),
         pl.BlockSpec((None, S5_R, 2 * S5_W), lambda j, i: (j, i, 0))],
        [jax.ShapeDtypeStruct((T, D), F32), jax.ShapeDtypeStruct((S5_SLABS, T, 2 * S5_W), F32)],
        [pltpu.VMEM((8, 2 * S5_W), F32)], [a, bb, cb, tab], rider)


def s5_scan_bwd(a, bb, cb, tab_rev, xs, dy, rider=None):
    T = a.shape[0]
    nb = T // S5_R
    rg = S5_R // 8

    def body(a_ref, dy_ref, xs_ref, xp_ref, bb_ref, cb_ref, tab_ref, da_ref, dbb_ref, dcb_ref, dab_ref, c_ref):
        i = pl.program_id(1)

        @pl.when(i == 0)
        def _():
            c_ref[...] = jnp.zeros(c_ref.shape, F32)

        dy_v = dy_ref[...]
        x = xs_ref[...]
        dxo = _dg(dy_v, cb_ref[...], 1, 1)
        gr, gi, cr, ci = _s5_scan(dxo[:, :S5_W], dxo[:, S5_W:], tab_ref[...], c_ref[0:1, :S5_W], c_ref[0:1, S5_W:], True)
        c_ref[0:1, :] = jnp.concatenate([cr, ci], axis=1)
        g = jnp.concatenate([gr, gi], axis=1)
        da_ref[...] = _dg(g, bb_ref[...], 1, 1)
        dbb = _dg(a_ref[...], g, 0, 0)
        dcb = _dg(x, dy_v, 0, 0)
        first_tile = i == nb - 1
        prev_last = jnp.where(first_tile, 0.0, xp_ref[7:8, :])
        rows = lax.broadcasted_iota(jnp.int32, x.shape, 0)
        xp = jnp.where(rows == 0, prev_last, pltpu.roll(x, 1, 0))
        xpr, xpi = xp[:, :S5_W], xp[:, S5_W:]
        dar = (gr * xpr + gi * xpi).reshape(rg, 8, S5_W).sum(axis=0)
        dai = (gi * xpr - gr * xpi).reshape(rg, 8, S5_W).sum(axis=0)
        dab = jnp.concatenate([dar, dai], axis=1)

        @pl.when(i == 0)
        def _():
            dbb_ref[...] = dbb
            dcb_ref[...] = dcb
            dab_ref[...] = dab

        @pl.when(i != 0)
        def _():
            dbb_ref[...] += dbb
            dcb_ref[...] += dcb
            dab_ref[...] += dab

    def prev_rows(j, i):
        return (j, jnp.maximum((nb - 1 - i) * rg - 1, 0), 0)

    return _pallas(
        "s5_scan_b", body, (S5_SLABS, nb),
        [pl.BlockSpec((S5_R, 128), lambda j, i: (nb - 1 - i, j)),
         pl.BlockSpec((S5_R, 128), lambda j, i: (nb - 1 - i, j)),
         pl.BlockSpec((None, S5_R, 2 * S5_W), lambda j, i: (j, nb - 1 - i, 0)),
         pl.BlockSpec((None, 8, 2 * S5_W), prev_rows),
         pl.BlockSpec((None, 128, 2 * S5_W), lambda j, i: (j, 0, 0)),
         pl.BlockSpec((None, 2 * S5_W, 128), lambda j, i: (j, 0, 0)),
         pl.BlockSpec((None, 10, 8, S5_W), lambda j, i: (j, 0, 0, 0))],
        [pl.BlockSpec((S5_R, 128), lambda j, i: (nb - 1 - i, j)),
         pl.BlockSpec((None, 128, 2 * S5_W), lambda j, i: (j, 0, 0)),
         pl.BlockSpec((None, 2 * S5_W, 128), lambda j, i: (j, 0, 0)),
         pl.BlockSpec((None, 8, 2 * S5_W), lambda j, i: (j, 0, 0))],
        [jax.ShapeDtypeStruct((T, D), F32), jax.ShapeDtypeStruct((S5_SLABS, 128, 2 * S5_W), F32),
         jax.ShapeDtypeStruct((S5_SLABS, 2 * S5_W, 128), F32), jax.ShapeDtypeStruct((S5_SLABS, 8, 2 * S5_W), F32)],
        [pltpu.VMEM((8, 2 * S5_W), F32)], [a, dy, xs, xs, bb, cb, tab_rev], rider)


def _s5_discretise(lam_re, lam_im, log_dt, b_re, b_im, c_re, c_im):
    dt = jnp.exp(log_dt)[:, None]
    mag = jnp.exp(lam_re * dt)
    abar_re = mag * jnp.cos(lam_im * dt)
    abar_im = mag * jnp.sin(lam_im * dt)
    den = lam_re * lam_re + lam_im * lam_im
    zoh_re = ((abar_re - 1.0) * lam_re + abar_im * lam_im) / den
    zoh_im = (abar_im * lam_re - (abar_re - 1.0) * lam_im) / den
    bbar_re = zoh_re[..., None] * b_re - zoh_im[..., None] * b_im
    bbar_im = zoh_re[..., None] * b_im + zoh_im[..., None] * b_re
    eye = jnp.eye(8, dtype=F32)

    def in_map(bbar):
        t = bbar.reshape(8, 8, S5_P, S5_K).transpose(0, 1, 3, 2)
        return (t[:, :, :, None, :] * eye[None, :, None, :, None]).reshape(8, 8 * S5_K, 8 * S5_P)

    def out_map(c):
        t = c.reshape(8, 8, S5_K, S5_P).transpose(0, 1, 3, 2)
        return (t[:, :, :, None, :] * eye[None, :, None, :, None]).reshape(8, 8 * S5_P, 8 * S5_K)

    bb = jnp.concatenate([in_map(bbar_re), in_map(bbar_im)], axis=2)
    cb = jnp.concatenate([out_map(c_re), -out_map(c_im)], axis=1)
    return bb, cb, abar_re.reshape(8, S5_W), abar_im.reshape(8, S5_W)


def _s5_tables(ar, ai, reverse):
    if reverse:
        ai = -ai
    pw = [(jnp.ones_like(ar), jnp.zeros_like(ar))]
    for _ in range(8):
        pw.append(_cmul(pw[-1][0], pw[-1][1], ar, ai))
    r = jnp.arange(8)[None, :, None]
    rows = []
    for d in (1, 2, 4):
        keep = (r <= 7 - d) if reverse else (r >= d)
        rows += [jnp.where(keep, pw[d][0][:, None, :], 0.0), jnp.where(keep, pw[d][1][:, None, :], 0.0)]
    order = [8 - k for k in range(8)] if reverse else [k + 1 for k in range(8)]
    rows += [jnp.stack([pw[n][0] for n in order], axis=1), jnp.stack([pw[n][1] for n in order], axis=1)]
    rows += [jnp.broadcast_to(pw[8][0][:, None, :], (8, 8, S5_W)), jnp.broadcast_to(pw[8][1][:, None, :], (8, 8, S5_W))]
    return jnp.stack(rows, axis=1)


def _s5_act_fn(pids, yc, a, dskip):
    return (jax.nn.gelu(yc + dskip * a),)


def _make_glu_res_fn(tm):
    def glu_res_fn(pids, zz, h):
        rows = _row_ids(pids[0], tm, (tm, 1))
        return (h + jnp.where(rows >= PAD, zz[:, :D] * jax.nn.sigmoid(zz[:, D:]), 0.0),)
    return glu_res_fn


def s5_mixer(h, g, w, rider=None):
    T = h.shape[0]
    tm = _row_tile(T)
    nt = T // tm
    a = norm_fwd("s5_norm", h, g, F32)
    ssm = [w[n] for n in ('lam_re', 'lam_im', 'log_dt', 'b_re', 'b_im', 'c_re', 'c_im')]
    (bb, cb, ar, ai), disc_vjp = jax.vjp(_s5_discretise, *ssm)
    yc, xs = s5_scan_fwd(a, bb, cb, _s5_tables(ar, ai, False), rider=rider)
    row = lambda arr: _rows(arr, tm)
    act_args = [row(yc), row(a), _const(w['dskip'])]
    act_outs = [Out((T, D), BF16, (tm, D), lambda i: (i, 0))]
    y = stage_fwd("s5_act", _s5_act_fn, (nt,), act_args, act_outs)[0]
    zz = matmul("s5_glu", y, w['w_glu'], 'nn')
    glu_fn = _make_glu_res_fn(tm)
    glu_args = [_rows(zz, tm, BF16), row(h)]
    glu_outs = [Out((T, D), F32, (tm, D), lambda i: (i, 0))]
    h_new = stage_fwd("s5_gate", glu_fn, (nt,), glu_args, glu_outs)[0]

    def bwd(dh, rider=None):
        dzz, dh_res = stage_bwd("s5_gate_b", glu_fn, (nt,), glu_args, glu_outs, [dh])
        dy, dwglu = linear_bwd("s5_glu_b", y, w['w_glu'], dzz)
        dyc, da1, ddskip = stage_bwd("s5_act_b", _s5_act_fn, (nt,), act_args, act_outs, [dy])
        da2, dbb, dcb, dab = s5_scan_bwd(a, bb, cb, _s5_tables(ar, ai, True), xs, dyc, rider=rider)
        dab = dab.sum(axis=1)
        dssm = disc_vjp((dbb, dcb, dab[:, :S5_W], dab[:, S5_W:]))
        dh2, dg = _s5_norm_bwd(h, g, da1, da2, dh_res, tm)
        grads = dict(zip(('lam_re', 'lam_im', 'log_dt', 'b_re', 'b_im', 'c_re', 'c_im'), dssm))
        grads.update(g=dg, dskip=ddskip, w_glu=dwglu)
        return dh2, grads

    return h_new, bwd


def _norm3_bwd_fn(pids, h, g):
    a = _rms(h, g)
    return a, a, h


def _s5_norm_bwd(h, g, da1, da2, dh, tm):
    T = h.shape[0]
    o = Out((T, D), F32, (tm, D), lambda i: (i, 0))
    return stage_bwd("s5_norm_b", _norm3_bwd_fn, (T // tm,), [_rows(h, tm), _const(g)], [o, o, o], [da1, da2, dh])


RET_R = 384


def _rope256(x, cos, sin):
    x1, x2 = x[:, :128], x[:, 128:]
    return jnp.concatenate([x1 * cos - x2 * sin, x1 * sin + x2 * cos], axis=1)


def _ret_fn(pids, z, gn, cos, sin, dmat, qdec, kdec, cdec, st):
    R = z.shape[0]
    q = _rope256(z[:, :256], cos, sin)
    k = _rope256(z[:, 256:512], cos, sin) * (RET_DK ** -0.5)
    v, gate = z[:, 512:1024], z[:, 1024:]
    outs = []
    for cc in range(R // CHUNK):
        lo = cc * CHUNK
        qc, kc, vc = q[lo:lo + CHUNK], k[lo:lo + CHUNK], v[lo:lo + CHUNK]
        outs.append(mm_nn(mm_nt(qc, kc) * dmat, vc) + mm_nn(qc * qdec, st))
        st = st * cdec + mm_tn(kc * kdec, vc)
    o = jnp.concatenate(outs, axis=0)
    mu = jnp.mean(o, axis=-1, keepdims=True)
    var = jnp.mean(jnp.square(o - mu), axis=-1, keepdims=True)
    o = (o - mu) * lax.rsqrt(var + EPS)
    return o * gn * _silu(gate), st


def ret_mixer(h, g, w, tabs, rider=None):
    T = h.shape[0]
    a = norm_fwd("ret_norm", h, g, BF16)
    z = matmul("ret_in", a, w['w_in'], 'nn')
    grid = (RET_H, T // RET_R)
    hw = RET_DK * 2 + RET_DV * 2
    args = [Arg(z, (RET_R, hw), lambda hh, j: (j, hh), gdtype=BF16), Arg(w['gn'], (1, RET_DV), lambda hh, j: (0, hh)),
            Arg(tabs['cos_d'], (RET_R, 128), lambda hh, j: (j, 0), False),
            Arg(tabs['sin_d'], (RET_R, 128), lambda hh, j: (j, 0), False),
            Arg(tabs['ret_dmat'], (None, CHUNK, CHUNK), lambda hh, j: (hh, 0, 0), False),
            Arg(tabs['ret_qdec'], (None, CHUNK, 1), lambda hh, j: (hh, 0, 0), False),
            Arg(tabs['ret_kdec'], (None, CHUNK, 1), lambda hh, j: (hh, 0, 0), False),
            Arg(tabs['ret_cdec'], (None, 1, 1), lambda hh, j: (hh, 0, 0), False)]
    outs = [Out((T, RET_H * RET_DV), BF16, (RET_R, RET_DV), lambda hh, j: (j, hh))]
    o, states = stage_fwd("ret_chunks", _ret_fn, grid, args, outs, state_shape=(RET_DK, RET_DV), rider=rider)
    h_new = matmul("ret_o", o, w['w_o'], 'nn', res=h)

    def bwd(dh, rider=None):
        do, dwo = linear_bwd("ret_o_b", o, w['w_o'], dh, mask=True)
        dz, dgn = stage_bwd("ret_chunks_b", _ret_fn, grid, args, outs, [do], state_shape=(RET_DK, RET_DV),
                            states=states, rider=rider)
        da, dwin = linear_bwd("ret_in_b", a, w['w_in'], dz)
        dh2, dg = norm_bwd("ret_norm_b", h, g, da, dh)
        return dh2, dict(g=dg, w_in=dwin, gn=dgn, w_o=dwo)

    return h_new, bwd


def loss_head(h, tgt):
    T = h.shape[0]
    tm = _row_tile(T)

    def body(h_ref, t_ref, loss_ref, dh_ref):
        i = pl.program_id(0)
        rows = _row_ids(i, tm, (tm, 1))
        err = jnp.where(rows >= LEAD, h_ref[...] - t_ref[...], 0.0)
        dh_ref[...] = err * (1.0 / D)
        part = jnp.full((8, 128), 0.5 * jnp.sum(jnp.sum(err * err, axis=1, keepdims=True) * (1.0 / D)), F32)

        @pl.when(i == 0)
        def _():
            loss_ref[...] = part

        @pl.when(i != 0)
        def _():
            loss_ref[...] += part

    loss, dh = pl.pallas_call(
        body, grid=(T // tm,),
        in_specs=[pl.BlockSpec((tm, D), lambda i: (i, 0)), pl.BlockSpec((tm, D), lambda i: (i, 0))],
        out_specs=[pl.BlockSpec((8, 128), lambda i: (0, 0)), pl.BlockSpec((tm, D), lambda i: (i, 0))],
        out_shape=[jax.ShapeDtypeStruct((8, 128), F32), jax.ShapeDtypeStruct((T, D), F32)], name="loss_head",
        compiler_params=_cparams())(h, tgt)
    return loss[0, 0], dh


def _tables(T):
    pos = jnp.maximum(jnp.arange(T, dtype=jnp.int32) - PAD, 0).astype(F32)

    def cs(dim):
        inv_freq = 1.0 / (10000.0 ** (jnp.arange(0, dim, 2, dtype=F32) / dim))
        ang = pos[:, None] * inv_freq[None, :]
        return jnp.cos(ang), jnp.sin(ang)

    ca, sa = cs(MLA_ROPE)
    zeros = jnp.zeros((T, 64), F32)
    cd, sd = cs(RET_DK)
    log_gamma = jnp.log(1.0 - jnp.exp2(-5.0 - jnp.arange(RET_H, dtype=F32)))
    p = jnp.arange(CHUNK, dtype=F32)
    diff = p[:, None] - p[None, :]
    dmat = jnp.where(diff >= 0, jnp.exp(diff[None] * log_gamma[:, None, None]), 0.0)
    return dict(cos_a=jnp.concatenate([ca, ca, zeros], axis=1), sin_a=jnp.concatenate([-sa, sa, zeros], axis=1),
                cos_d=cd, sin_d=sd, ret_dmat=dmat,
                ret_qdec=jnp.exp((p[None, :] + 1.0) * log_gamma[:, None])[..., None],
                ret_kdec=jnp.exp((CHUNK - 1.0 - p[None, :]) * log_gamma[:, None])[..., None],
                ret_cdec=jnp.exp(CHUNK * log_gamma)[:, None, None])


def _hgrn_lower_bound(logits):
    lb_cum = jnp.cumsum(jax.nn.softmax(logits, axis=0), axis=0)
    return (lb_cum - lb_cum[0:1])[1:2]


def _uq_to_heads(w):
    t = w.reshape(w.shape[0], MLA_H, MLA_QK)
    return jnp.pad(t, ((0, 0), (0, 0), (0, 256 - MLA_QK))).reshape(w.shape[0], MLA_H * 256)


def _uq_from_heads(g):
    return g.reshape(g.shape[0], MLA_H, 256)[:, :, :MLA_QK].reshape(g.shape[0], MLA_H * MLA_QK)


def _head_interleave(w, widths, heads):
    parts, lo = [], 0
    for wd in widths:
        parts.append(w[:, lo:lo + heads * wd].reshape(w.shape[0], heads, wd))
        lo += heads * wd
    return jnp.concatenate(parts, axis=2).reshape(w.shape[0], -1)


def _head_deinterleave(g, widths, heads):
    t = g.reshape(g.shape[0], heads, sum(widths))
    parts, lo = [], 0
    for wd in widths:
        parts.append(t[:, :, lo:lo + wd].reshape(g.shape[0], heads * wd))
        lo += wd
    return jnp.concatenate(parts, axis=1)


HG_WIDTHS = (128, 128, 128, 128)
RET_WIDTHS = (RET_DK, RET_DK, RET_DV, RET_DV)


def _split_head_gain(g):
    return g[:, :128], jnp.pad(g[:, 128:], ((0, 0), (0, 64)))


def _join_head_gain(dn, dr):
    return jnp.concatenate([dn, dr[:, :64]], axis=1)


def local_step(x, target, W, ex):
    S = x.shape[0]
    T = S + LEAD
    tabs = _tables(T)
    h = jnp.concatenate([jnp.zeros((PAD, D), F32), W['meta_tokens'], x], axis=0)
    tgt = jnp.concatenate([jnp.zeros((LEAD, D), F32), target], axis=0)

    gqn, gqr = _split_head_gain(W['mla_q_head_g'])
    gkn, gkr = _split_head_gain(W['mla_k_head_g'])
    lb, lb_vjp = jax.vjp(_hgrn_lower_bound, W['hgrn_lb_logits'])

    def ffn(i, hh):
        return ffn_layer(i, hh, W['norm_ffn_g'][i:i + 1], ex.weight('ffn_w_up', i), W['ffn_conv_w'][i],
                         W['ffn_conv_b'][i:i + 1], ex.weight('ffn_w_down', i))

    bm, bf = [None] * 4, [None] * 4
    ex.gather(['mla'], name="gather_mla")
    w0 = dict(w_down=jnp.pad(ex.weight('mla_w_down'), ((0, 0), (0, 64))), gcq=W['mla_cq_norm_g'],
              gckv=W['mla_ckv_norm_g'], w_uq=_uq_to_heads(ex.weight('mla_w_uq')), w_ukv=ex.weight('mla_w_ukv'),
              gqn=gqn, gqr=gqr, gkn=gkn, gkr=gkr, w_o=ex.weight('mla_w_o'))
    h, bm[0] = mla_mixer(h, W['norm_mix_g'][0:1], w0, tabs, rider=ex.gather(['ffn0', 'hgrn', 'ffn1']))
    h, bf[0] = ffn(0, h)
    w1 = dict(w_in=_head_interleave(ex.weight('hgrn_w_in'), HG_WIDTHS, HG_H), lb=lb, go=W['hgrn_o_norm_g'],
              w_o=ex.weight('hgrn_w_o'))
    h, bm[1] = hgrn_mixer(h, W['norm_mix_g'][1:2], w1, rider=ex.gather(['s5', 'ffn2']))
    h, bf[1] = ffn(1, h)
    w2 = dict(lam_re=W['s5_lam_re'][0], lam_im=W['s5_lam_im'][0], log_dt=W['s5_log_dt'][0], b_re=W['s5_b_re'][0],
              b_im=W['s5_b_im'][0], c_re=W['s5_c_re'][0], c_im=W['s5_c_im'][0], dskip=W['s5_d'],
              w_glu=ex.weight('s5_w_glu'))
    h, bm[2] = s5_mixer(h, W['norm_mix_g'][2:3], w2, rider=ex.gather(['ret']))
    h, bf[2] = ffn(2, h)
    w3 = dict(w_in=_head_interleave(ex.weight('ret_w_in'), RET_WIDTHS, RET_H), gn=W['ret_gn_g'],
              w_o=ex.weight('ret_w_o'))
    h, bm[3] = ret_mixer(h, W['norm_mix_g'][3:4], w3, tabs, rider=ex.gather(['ffn3']))
    h, bf[3] = ffn(3, h)

    loss, dh = loss_head(h, tgt)

    def ffn_grads(i, g):
        return {('ffn_w_up', i): g['w_up'], ('ffn_w_down', i): g['w_down']}

    gm, gf = [None] * 4, [None] * 4
    dh, gf[3] = bf[3](dh)
    dh, gm[3] = bm[3](dh, rider=ex.scatter(ffn_grads(3, gf[3])))
    dh, gf[2] = bf[2](dh)
    ret_grads = {('ret_w_in', 0): _head_deinterleave(gm[3]['w_in'], RET_WIDTHS, RET_H), ('ret_w_o', 0): gm[3]['w_o']}
    dh, gm[2] = bm[2](dh, rider=ex.scatter(ffn_grads(2, gf[2])))
    dh, gf[1] = bf[1](dh)
    dh, gm[1] = bm[1](dh, rider=ex.scatter(ffn_grads(1, gf[1])))
    dh, gf[0] = bf[0](dh)
    hgrn_grads = {('hgrn_w_in', 0): _head_deinterleave(gm[1]['w_in'], HG_WIDTHS, HG_H), ('hgrn_w_o', 0): gm[1]['w_o']}
    G = {}
    G['norm_ffn_g'] = jnp.concatenate([gf[i]['g'] for i in range(4)], axis=0)
    G['hgrn_lb_logits'] = lb_vjp(gm[1]['lb'])[0]
    G['hgrn_o_norm_g'] = gm[1]['go']
    for n in ('lam_re', 'lam_im', 'log_dt', 'b_re', 'b_im', 'c_re', 'c_im'):
        G['s5_' + n] = gm[2][n][None]
    G['s5_d'] = gm[2]['dskip']
    G['ret_gn_g'] = gm[3]['gn']
    G['ffn_conv_w'] = jnp.stack([gf[i]['cw'] for i in range(4)])
    G['ffn_conv_b'] = jnp.concatenate([gf[i]['cb'] for i in range(4)], axis=0)
    early = ex.all_devices(_pack([G[n] for n in SMALL_EARLY], F32, 8))

    dh, gm[0] = bm[0](dh, rider=[ex.scatter({('s5_w_glu', 0): gm[2]['w_glu'], **hgrn_grads, **ffn_grads(0, gf[0])}),
                                 ex.swap()], rider_dq=[early, ex.scatter(ret_grads)])
    a = gm[0]
    mla_grads = {('mla_w_down', 0): a['w_down'][:, :MLA_QL + MLA_KVL + MLA_ROPE], ('mla_w_uq', 0): _uq_from_heads(a['w_uq']),
                 ('mla_w_ukv', 0): a['w_ukv'], ('mla_w_o', 0): a['w_o']}
    G['meta_tokens'] = dh[PAD:LEAD]
    G['norm_mix_g'] = jnp.concatenate([gm[i]['g'] for i in range(4)], axis=0)
    G['mla_cq_norm_g'], G['mla_ckv_norm_g'] = a['gcq'], a['gckv']
    G['mla_q_head_g'] = _join_head_gain(a['gqn'], a['gqr'])
    G['mla_k_head_g'] = _join_head_gain(a['gkn'], a['gkr'])
    ex.tail = [ex.scatter(mla_grads), ex.all_devices(_pack([G[n] for n in SMALL_LATE], F32, 8))]
    return loss, dh[LEAD:], G


PACK_W = 1024
ANY = pl.BlockSpec(memory_space=pl.ANY)


def _pack(arrs, dtype, row_mult):
    flat = jnp.concatenate([a.reshape(-1).astype(dtype) for a in arrs])
    n = flat.shape[0]
    rows = -(-n // (PACK_W * row_mult)) * row_mult
    return jnp.pad(flat, (0, rows * PACK_W - n)).reshape(rows, PACK_W)


def _unpack(buf, shapes):
    flat = buf.reshape(-1)
    out, off = [], 0
    for s in shapes:
        n = math.prod(s)
        out.append(flat[off:off + n].reshape(s))
        off += n
    return out


def _my_pos():
    return lax.axis_index("x"), lax.axis_index("y"), lax.axis_index("c")


def _other_chips(x, y):
    return [(1 - x, y), (x, 1 - y), (1 - x, 1 - y)]


def gather_chips(name, src):
    def body(src_ref, out_ref, send_sems, recv_sems, local_sem):
        x, y, c = _my_pos()
        q = 2 * x + y
        mine = pltpu.make_async_copy(src_ref, out_ref.at[q], local_sem)
        mine.start()
        peers = _other_chips(x, y)

        def copy(k, slot, peer):
            return pltpu.make_async_remote_copy(src_ref=src_ref, dst_ref=out_ref.at[slot], send_sem=send_sems.at[k],
                                                recv_sem=recv_sems.at[k], device_id=(peer[0], peer[1], c),
                                                device_id_type=MESH_ID)
        sends = [copy(k, q, p) for k, p in enumerate(peers)]
        for cp in sends:
            cp.start()
        for k, p in enumerate(peers):
            copy(k, 2 * p[0] + p[1], p).wait_recv()
        for cp in sends:
            cp.wait_send()
        mine.wait()

    return pl.pallas_call(body, out_shape=jax.ShapeDtypeStruct((4,) + src.shape, src.dtype), in_specs=[ANY],
                          out_specs=ANY, name=name,
                          scratch_shapes=[pltpu.SemaphoreType.DMA((3,)), pltpu.SemaphoreType.DMA((3,)),
                                          pltpu.SemaphoreType.DMA(())])(src)


def _pack_tile(rows):
    return _divisors(rows, (256, 128, 64, 32, 16, 8))[0] if rows > 512 else rows


def sum_slots(name, slots):
    n, rows, w = slots.shape
    tr = _pack_tile(rows)

    def body(s_ref, o_ref):
        acc = s_ref[0].astype(F32)
        for k in range(1, n):
            acc = acc + s_ref[k].astype(F32)
        o_ref[...] = acc

    return pl.pallas_call(body, grid=(rows // tr,), in_specs=[pl.BlockSpec((n, tr, w), lambda i: (0, i, 0))],
                          out_specs=pl.BlockSpec((tr, w), lambda i: (i, 0)),
                          out_shape=jax.ShapeDtypeStruct((rows, w), F32), name=name, compiler_params=_cparams())(slots)


def adamw(name, grads, w, m, v):
    rows, wd = w.shape
    tr = _pack_tile(rows)
    ng = len(grads)

    def body(*refs):
        g = refs[0][...]
        for r in refs[1:ng]:
            g = g + r[...]
        w_ref, m_ref, v_ref = refs[ng:ng + 3]
        g_out, d_out, m_out, v_out = refs[ng + 3:]
        m_new = ADAM_B1 * m_ref[...] + (1.0 - ADAM_B1) * g
        v_new = ADAM_B2 * v_ref[...] + (1.0 - ADAM_B2) * jnp.square(g)
        m_hat = m_new / (1.0 - ADAM_B1 ** ADAM_STEP)
        v_hat = v_new / (1.0 - ADAM_B2 ** ADAM_STEP)
        g_out[...] = g
        d_out[...] = -ADAM_LR * (m_hat / (jnp.sqrt(v_hat) + ADAM_EPS) + ADAM_WD * w_ref[...])
        m_out[...] = m_new
        v_out[...] = v_new

    spec = pl.BlockSpec((tr, wd), lambda i: (i, 0))
    shape = jax.ShapeDtypeStruct((rows, wd), F32)
    return pl.pallas_call(body, grid=(rows // tr,), in_specs=[spec] * (ng + 3), out_specs=[spec] * 4,
                          out_shape=[shape] * 4, name=name, compiler_params=_cparams())(*grads, w, m, v)


def _sem_scratch(nw):
    return [pltpu.SemaphoreType.DMA((3 * nw,)), pltpu.SemaphoreType.DMA((3 * nw,)), pltpu.SemaphoreType.DMA((nw,))]


def _block2d(ref, axis, p, n):
    if axis == 0:
        return ref.at[pl.ds(pl.multiple_of(p * n, 16), n), :]
    return ref.at[:, pl.ds(pl.multiple_of(p * n, 128), n)]


class ScatterRider:
    def __init__(self, items):
        self.items = items
        self.operands = [it[0] for it in items]
        self.results = None
        self.out_shapes = [jax.ShapeDtypeStruct((4, arr.shape[0] // 4, arr.shape[1]) if axis == 0 else
                                                (4, arr.shape[0], arr.shape[1] // 4), arr.dtype) for arr, axis in items]
        self.scratch = _sem_scratch(len(items))

    def _copies(self, ins, outs, sems):
        send_sems, recv_sems, local_sems = sems
        x, y, c = _my_pos()
        q = 2 * x + y
        local, sends, lands = [], [], []
        for w, (arr, axis) in enumerate(self.items):
            n = arr.shape[axis] // 4
            local.append(pltpu.make_async_copy(_block2d(ins[w], axis, q, n), outs[w].at[q], local_sems.at[w]))
            for k, (px, py) in enumerate(_other_chips(x, y)):
                p = 2 * px + py
                sems_k = dict(send_sem=send_sems.at[3 * w + k], recv_sem=recv_sems.at[3 * w + k],
                              device_id=(px, py, c), device_id_type=MESH_ID)
                theirs = _block2d(ins[w], axis, p, n)
                sends.append(pltpu.make_async_remote_copy(src_ref=theirs, dst_ref=outs[w].at[q], **sems_k))
                lands.append(pltpu.make_async_remote_copy(src_ref=theirs, dst_ref=outs[w].at[p], **sems_k))
        return local, sends, lands

    def start(self, ins, outs, sems):
        local, sends, _ = self._copies(ins, outs, sems)
        for cp in local + sends:
            cp.start()

    def finish(self, ins, outs, sems):
        local, sends, lands = self._copies(ins, outs, sems)
        for cp in lands:
            cp.wait_recv()
        for cp in sends:
            cp.wait_send()
        for cp in local:
            cp.wait()


class GatherRider:
    def __init__(self, items):
        self.items = items
        self.operands = [it[0] for it in items]
        self.results = None
        self.out_shapes = []
        for arr, _, axis in items:
            r, c = arr.shape[1:]
            assert r % 32 == 0
            self.out_shapes.append(jax.ShapeDtypeStruct((4 * r, c) if axis == 0 else (r, 4 * c), arr.dtype))
        n = len(items)
        dma = pltpu.SemaphoreType.DMA
        self.scratch = [dma((3 * n,)), dma((3 * n,)), dma((n,)), dma((3 * n,)), dma((3 * n,))]

    def _copies(self, ins, outs, sems):
        send_sems, recv_sems, local_sems, pass_send_sems, pass_recv_sems = sems
        x, y, c = _my_pos()
        q = 2 * x + y
        local, sends, lands, passes, pass_lands = [], [], [], [], []
        for w, (arr, layer, axis) in enumerate(self.items):
            r, cols = arr.shape[1:]
            half = r // 2
            src = ins[w].at[layer]

            def part(blk, hc, w=w, axis=axis, r=r, cols=cols, half=half):
                if axis == 0:
                    return outs[w].at[pl.ds(pl.multiple_of(blk * r + hc * half, 16), half), :]
                return outs[w].at[pl.ds(pl.multiple_of(hc * half, 16), half), pl.ds(pl.multiple_of(blk * cols, 128), cols)]

            local.append(pltpu.make_async_copy(src, _block2d(outs[w], axis, q, arr.shape[1 + axis]), local_sems.at[w]))
            for k, (px, py) in enumerate(_other_chips(x, y)):
                p = 2 * px + py
                ici = dict(send_sem=send_sems.at[3 * w + k], recv_sem=recv_sems.at[3 * w + k],
                           device_id=(px, py, c), device_id_type=MESH_ID)
                d2d = dict(send_sem=pass_send_sems.at[3 * w + k], recv_sem=pass_recv_sems.at[3 * w + k],
                           device_id=(x, y, 1 - c), device_id_type=MESH_ID)
                mine = src.at[pl.ds(pl.multiple_of(c * half, 16), half), :]
                sends.append(pltpu.make_async_remote_copy(src_ref=mine, dst_ref=part(q, c), **ici))
                lands.append(pltpu.make_async_remote_copy(src_ref=mine, dst_ref=part(p, c), **ici))
                passes.append(pltpu.make_async_remote_copy(src_ref=part(p, c), dst_ref=part(p, c), **d2d))
                pass_lands.append(pltpu.make_async_remote_copy(src_ref=part(p, c), dst_ref=part(p, 1 - c), **d2d))
        return local, sends, lands, passes, pass_lands

    def start(self, ins, outs, sems):
        local, sends, _, _, _ = self._copies(ins, outs, sems)
        for cp in local + sends:
            cp.start()

    def middle(self, ins, outs, sems):
        _, _, lands, passes, _ = self._copies(ins, outs, sems)
        for land, cp in zip(lands, passes):
            land.wait_recv()
            cp.start()

    def finish(self, ins, outs, sems):
        local, sends, _, passes, pass_lands = self._copies(ins, outs, sems)
        for cp in pass_lands:
            cp.wait_recv()
        for cp in sends + passes:
            cp.wait_send()
        for cp in local:
            cp.wait()


class SwapRider:
    def __init__(self, arrs):
        self.operands = list(arrs)
        self.out_shapes = [jax.ShapeDtypeStruct(a.shape, a.dtype) for a in arrs]
        self.scratch = [pltpu.SemaphoreType.DMA((len(arrs),)), pltpu.SemaphoreType.DMA((len(arrs),))]
        self.results = None

    def _copies(self, ins, outs, sems):
        x, y, c = _my_pos()
        return [pltpu.make_async_remote_copy(src_ref=ins[w], dst_ref=outs[w], send_sem=sems[0].at[w],
                                             recv_sem=sems[1].at[w], device_id=(x, y, 1 - c), device_id_type=MESH_ID)
                for w in range(len(self.operands))]

    def start(self, ins, outs, sems):
        for cp in self._copies(ins, outs, sems):
            cp.start()

    def finish(self, ins, outs, sems):
        for cp in self._copies(ins, outs, sems):
            cp.wait()


class AllDevicesRider:
    def __init__(self, src):
        self.operands = [src]
        self.out_shapes = [jax.ShapeDtypeStruct((8,) + src.shape, src.dtype)]
        self.scratch = [pltpu.SemaphoreType.DMA((7,)), pltpu.SemaphoreType.DMA((7,)), pltpu.SemaphoreType.DMA(())]
        self.results = None

    def _copies(self, ins, outs, sems):
        x, y, c = _my_pos()
        me = 4 * x + 2 * y + c
        local = pltpu.make_async_copy(ins[0], outs[0].at[me], sems[2])
        sends, lands = [], []
        for k, m in enumerate(range(1, 8)):
            peer = ((1 - x) if m & 4 else x, (1 - y) if m & 2 else y, (1 - c) if m & 1 else c)
            sems_k = dict(send_sem=sems[0].at[k], recv_sem=sems[1].at[k], device_id=peer, device_id_type=MESH_ID)
            sends.append(pltpu.make_async_remote_copy(src_ref=ins[0], dst_ref=outs[0].at[me], **sems_k))
            lands.append(pltpu.make_async_remote_copy(src_ref=ins[0], dst_ref=outs[0].at[4 * peer[0] + 2 * peer[1] + peer[2]],
                                                      **sems_k))
        return local, sends, lands

    def start(self, ins, outs, sems):
        local, sends, _ = self._copies(ins, outs, sems)
        for cp in [local] + sends:
            cp.start()

    def finish(self, ins, outs, sems):
        local, sends, lands = self._copies(ins, outs, sems)
        for cp in lands:
            cp.wait_recv()
        for cp in sends:
            cp.wait_send()
        local.wait()


class RiderGroup:
    def __init__(self, riders):
        self.riders = riders
        self.operands = [a for r in riders for a in r.operands]
        self.out_shapes = [s for r in riders for s in r.out_shapes]
        self.scratch = [s for r in riders for s in r.scratch]

    def _split(self, ins, outs, sems):
        for r in self.riders:
            ni, no, ns = len(r.operands), len(r.out_shapes), len(r.scratch)
            yield r, ins[:ni], outs[:no], sems[:ns]
            ins, outs, sems = ins[ni:], outs[no:], sems[ns:]

    def start(self, ins, outs, sems):
        for r, i, o, s in self._split(ins, outs, sems):
            r.start(i, o, s)

    def middle(self, ins, outs, sems):
        for r, i, o, s in self._split(ins, outs, sems):
            if hasattr(r, 'middle'):
                r.middle(i, o, s)

    def finish(self, ins, outs, sems):
        for r, i, o, s in self._split(ins, outs, sems):
            r.finish(i, o, s)

    @property
    def results(self):
        return None

    @results.setter
    def results(self, res):
        for r in self.riders:
            no = len(r.out_shapes)
            r.results, res = list(res[:no]), res[no:]


def run_rider(name, rider):
    n_in, n_out = len(rider.operands), len(rider.out_shapes)

    def body(*refs):
        ins, outs, sems = refs[:n_in], refs[n_in:n_in + n_out], refs[n_in + n_out:]
        rider.start(ins, outs, sems)
        if hasattr(rider, 'middle'):
            rider.middle(ins, outs, sems)
        rider.finish(ins, outs, sems)

    rider.results = list(pl.pallas_call(body, out_shape=rider.out_shapes, in_specs=[ANY] * n_in, out_specs=[ANY] * n_out,
                                        name=name, scratch_shapes=rider.scratch)(*rider.operands))


WEIGHT_GROUPS = {'mla': [('mla_w_down', 0), ('mla_w_uq', 0), ('mla_w_ukv', 0), ('mla_w_o', 0)],
                 'hgrn': [('hgrn_w_in', 0), ('hgrn_w_o', 0)], 's5': [('s5_w_glu', 0)],
                 'ret': [('ret_w_in', 0), ('ret_w_o', 0)]}
WEIGHT_GROUPS.update({f'ffn{i}': [('ffn_w_up', i), ('ffn_w_down', i)] for i in range(4)})


class Exchange:
    def __init__(self, shards=None, full=None):
        self.shards, self.full = shards, dict(full or {})
        self.got, self.recv, self.sib, self.grads, self.small, self.tail = {}, {}, {}, {}, [], []

    def gather(self, groups, name=None):
        if self.shards is None:
            return None
        keys = [k for g in groups for k in WEIGHT_GROUPS[g]]
        rider = GatherRider([(self.shards[n], layer, SHARD_AXIS[n] - 1) for n, layer in keys])
        self.got.update({k: (rider, j) for j, k in enumerate(keys)})
        if name is not None:
            run_rider(name, rider)
        return rider

    def weight(self, n, layer=0):
        if self.shards is None:
            return self.full[n][layer]
        rider, j = self.got[(n, layer)]
        return rider.results[j]

    def scatter(self, grads, name=None):
        if self.shards is None:
            self.grads.update(grads)
            return None
        keys = list(grads)
        rider = ScatterRider([(grads[k], SHARD_AXIS[k[0]] - 1) for k in keys])
        self.recv.update({k: (rider, j) for j, k in enumerate(keys)})
        if name is not None:
            run_rider(name, rider)
        return rider

    def received(self, n, layer):
        rider, j = self.recv[(n, layer)]
        return rider.results[j]

    def swap(self, name=None):
        if self.shards is None:
            return None
        keys = [k for k, (r, _) in self.recv.items() if k not in self.sib and r.results is not None]
        rider = SwapRider([self.received(*k) for k in keys])
        self.sib.update({k: (rider, j) for j, k in enumerate(keys)})
        if name is not None:
            run_rider(name, rider)
        return rider

    def sibling(self, n, layer):
        rider, j = self.sib[(n, layer)]
        return rider.results[j]

    def all_devices(self, packed, name=None):
        if self.shards is None:
            return None
        rider = AllDevicesRider(packed)
        self.small.append(rider)
        if name is not None:
            run_rider(name, rider)
        return rider


ADAM_BLOCK_ELEMS = 256 * 1024


def adamw_shard(name, mine, sib, w, m, v, first_layer=0, into=None, rider=None):
    _, rows, cols = w.shape
    nl = mine.shape[1]
    tr = [t for t in (512, 384, 352, 256, 176, 128, 64, 32, 16) if rows % t == 0 and t * cols <= ADAM_BLOCK_ELEMS][0]

    def body(a_ref, b_ref, w_ref, m_ref, v_ref, *rest):
        g_out, d_out, m_out, v_out = rest[-4:]

        def total(r):
            acc = r[0].astype(F32)
            for k in range(1, 4):
                acc = acc + r[k].astype(F32)
            return acc
        g = total(a_ref) + total(b_ref)
        m_new = ADAM_B1 * m_ref[...] + (1.0 - ADAM_B1) * g
        v_new = ADAM_B2 * v_ref[...] + (1.0 - ADAM_B2) * jnp.square(g)
        m_hat = m_new / (1.0 - ADAM_B1 ** ADAM_STEP)
        v_hat = v_new / (1.0 - ADAM_B2 ** ADAM_STEP)
        g_out[...] = g
        d_out[...] = -ADAM_LR * (m_hat / (jnp.sqrt(v_hat) + ADAM_EPS) + ADAM_WD * w_ref[...])
        m_out[...] = m_new
        v_out[...] = v_new

    slots = pl.BlockSpec((4, None, tr, cols), lambda l, i: (0, l, i, 0))
    spec = pl.BlockSpec((None, tr, cols), lambda l, i: (l + first_layer, i, 0))
    shape = jax.ShapeDtypeStruct(w.shape, F32)
    in_specs, operands, aliases = [slots, slots, spec, spec, spec], [mine, sib, w, m, v], {}
    if into is not None:
        in_specs += [ANY] * 4
        operands += list(into)
        aliases = {5 + k: k for k in range(4)}
    return _pallas(name, body, (nl, rows // tr), in_specs, [spec] * 4, [shape] * 4, [], operands, rider, aliases)


def kernel(x, meta_tokens, norm_mix_g, norm_ffn_g, mla_w_down, mla_cq_norm_g, mla_ckv_norm_g, mla_w_uq, mla_w_ukv, mla_q_head_g, mla_k_head_g, mla_w_o, hgrn_w_in, hgrn_lb_logits, hgrn_o_norm_g, hgrn_w_o, s5_lam_re, s5_lam_im, s5_log_dt, s5_b_re, s5_b_im, s5_c_re, s5_c_im, s5_d, s5_w_glu, ret_w_in, ret_gn_g, ret_w_o, ffn_w_up, ffn_conv_w, ffn_conv_b, ffn_w_down, loss_target, m_meta_tokens, m_norm_mix_g, m_norm_ffn_g, m_mla_w_down, m_mla_cq_norm_g, m_mla_ckv_norm_g, m_mla_w_uq, m_mla_w_ukv, m_mla_q_head_g, m_mla_k_head_g, m_mla_w_o, m_hgrn_w_in, m_hgrn_lb_logits, m_hgrn_o_norm_g, m_hgrn_w_o, m_s5_lam_re, m_s5_lam_im, m_s5_log_dt, m_s5_b_re, m_s5_b_im, m_s5_c_re, m_s5_c_im, m_s5_d, m_s5_w_glu, m_ret_w_in, m_ret_gn_g, m_ret_w_o, m_ffn_w_up, m_ffn_conv_w, m_ffn_conv_b, m_ffn_w_down, v_meta_tokens, v_norm_mix_g, v_norm_ffn_g, v_mla_w_down, v_mla_cq_norm_g, v_mla_ckv_norm_g, v_mla_w_uq, v_mla_w_ukv, v_mla_q_head_g, v_mla_k_head_g, v_mla_w_o, v_hgrn_w_in, v_hgrn_lb_logits, v_hgrn_o_norm_g, v_hgrn_w_o, v_s5_lam_re, v_s5_lam_im, v_s5_log_dt, v_s5_b_re, v_s5_b_im, v_s5_c_re, v_s5_c_im, v_s5_d, v_s5_w_glu, v_ret_w_in, v_ret_gn_g, v_ret_w_o, v_ffn_w_up, v_ffn_conv_w, v_ffn_conv_b, v_ffn_w_down):
    vals = (x, meta_tokens, norm_mix_g, norm_ffn_g, mla_w_down, mla_cq_norm_g, mla_ckv_norm_g, mla_w_uq, mla_w_ukv, mla_q_head_g, mla_k_head_g, mla_w_o, hgrn_w_in, hgrn_lb_logits, hgrn_o_norm_g, hgrn_w_o, s5_lam_re, s5_lam_im, s5_log_dt, s5_b_re, s5_b_im, s5_c_re, s5_c_im, s5_d, s5_w_glu, ret_w_in, ret_gn_g, ret_w_o, ffn_w_up, ffn_conv_w, ffn_conv_b, ffn_w_down, loss_target, m_meta_tokens, m_norm_mix_g, m_norm_ffn_g, m_mla_w_down, m_mla_cq_norm_g, m_mla_ckv_norm_g, m_mla_w_uq, m_mla_w_ukv, m_mla_q_head_g, m_mla_k_head_g, m_mla_w_o, m_hgrn_w_in, m_hgrn_lb_logits, m_hgrn_o_norm_g, m_hgrn_w_o, m_s5_lam_re, m_s5_lam_im, m_s5_log_dt, m_s5_b_re, m_s5_b_im, m_s5_c_re, m_s5_c_im, m_s5_d, m_s5_w_glu, m_ret_w_in, m_ret_gn_g, m_ret_w_o, m_ffn_w_up, m_ffn_conv_w, m_ffn_conv_b, m_ffn_w_down, v_meta_tokens, v_norm_mix_g, v_norm_ffn_g, v_mla_w_down, v_mla_cq_norm_g, v_mla_ckv_norm_g, v_mla_w_uq, v_mla_w_ukv, v_mla_q_head_g, v_mla_k_head_g, v_mla_w_o, v_hgrn_w_in, v_hgrn_lb_logits, v_hgrn_o_norm_g, v_hgrn_w_o, v_s5_lam_re, v_s5_lam_im, v_s5_log_dt, v_s5_b_re, v_s5_b_im, v_s5_c_re, v_s5_c_im, v_s5_d, v_s5_w_glu, v_ret_w_in, v_ret_gn_g, v_ret_w_o, v_ffn_w_up, v_ffn_conv_w, v_ffn_conv_b, v_ffn_w_down)
    names = ['x'] + WEIGHTS + ['loss_target'] + ['m_' + n for n in WEIGHTS] + ['v_' + n for n in WEIGHTS]
    A = dict(zip(names, vals))
    q = 2 * lax.axis_index("x") + lax.axis_index("y")

    small_shapes = [A[n].shape for n in SMALL_SHARDED]
    got_small = gather_chips("gather_small", _pack([A[n] for n in SMALL_SHARDED], F32, 8))
    W = {n: A[n] for n in REPLICATED}
    parts_small = [_unpack(got_small[p], small_shapes) for p in range(4)]
    for k, n in enumerate(SMALL_SHARDED):
        W[n] = jnp.concatenate([parts_small[p][k] for p in range(4)], axis=SHARD_AXIS[n])

    ex = Exchange(shards={n: A[n].astype(BF16) for n in BIG})
    loss, grad_x, G = local_step(A['x'][0], A['loss_target'][0], W, ex)
    loss = lax.psum(loss, ("x", "y", "c"))

    def adam(n, layers, **kw):
        return adamw_shard(f"adam_{n}_{layers[0]}", jnp.stack([ex.received(n, layer) for layer in layers], axis=1),
                           jnp.stack([ex.sibling(n, layer) for layer in layers], axis=1),
                           A[n], A['m_' + n], A['v_' + n], first_layer=layers[0], **kw)

    late = {'ffn_w_up': adam('ffn_w_up', (1, 2, 3), rider=ex.tail + [ex.swap()])}
    late['ffn_w_down'] = adam('ffn_w_down', (1, 2, 3), rider=[ex.swap()])
    res_big = [adam(n, (0,), into=late[n]) if n in late else adam(n, (0,)) for n in BIG]

    small_names = REPLICATED + SMALL_SHARDED
    gs = {}
    for part, names, rider in (("early", SMALL_EARLY, ex.small[0]), ("late", SMALL_LATE, ex.small[1])):
        total = sum_slots("grad_small_sum_" + part, rider.results[0])
        gs.update(zip(names, _unpack(total, [G[n].shape for n in names])))
    for n in SMALL_SHARDED:
        ax = SHARD_AXIS[n]
        size = gs[n].shape[ax] // 4
        gs[n] = lax.dynamic_slice_in_dim(gs[n], q * size, size, axis=ax)
    pk = lambda pre: _pack([A[pre + n] for n in small_names], F32, 8)
    own_shapes = [A[n].shape for n in small_names]
    res_small = [_unpack(r, own_shapes) for r in
                 adamw("adam_small", [_pack([gs[n] for n in small_names], F32, 8)], pk(''), pk('m_'), pk('v_'))]

    out = {}
    for j, kind in enumerate(('grad_', 'delta_', 'new_m_', 'new_v_')):
        for k, n in enumerate(BIG):
            out[kind + n] = res_big[k][j]
        for k, n in enumerate(small_names):
            out[kind + n] = res_small[j][k]
    return (loss, grad_x[None]) + tuple(out[kind + n] for kind in ('grad_', 'delta_', 'new_m_', 'new_v_')
                                        for n in WEIGHTS)
```

```python
import functools
import math

import jax
import jax.numpy as jnp
from jax import lax
from jax.experimental import pallas as pl
from jax.experimental.pallas import tpu as pltpu

F32, BF16 = jnp.float32, jnp.bfloat16
HIGHEST = lax.Precision.HIGHEST
MESH_ID = pl.DeviceIdType.MESH

D = 1024
N_META = 16
PAD = 112
LEAD = PAD + N_META
EPS = 1e-6
NEG_INF = -1e30
CHUNK = 64
VMEM_LIMIT_V7X = 56 * 1024 * 1024
MM_VMEM_BUDGET = 36 * 1024 * 1024

MLA_H, MLA_NOPE, MLA_ROPE, MLA_V = 8, 128, 64, 128
MLA_QK = MLA_NOPE + MLA_ROPE
MLA_QL, MLA_KVL = 384, 256
HG_H, HG_D, HG_C = 8, 128, 16
S5_G, S5_P, S5_K = 64, 64, 16
RET_H, RET_DK, RET_DV = 4, 256, 512
FFN_F = 2816

ADAM_LR, ADAM_B1, ADAM_B2, ADAM_EPS, ADAM_WD, ADAM_STEP = 0.001, 0.9, 0.999, 1e-08, 0.01, 10

WEIGHTS = ['meta_tokens', 'norm_mix_g', 'norm_ffn_g', 'mla_w_down', 'mla_cq_norm_g', 'mla_ckv_norm_g', 'mla_w_uq',
           'mla_w_ukv', 'mla_q_head_g', 'mla_k_head_g', 'mla_w_o', 'hgrn_w_in', 'hgrn_lb_logits', 'hgrn_o_norm_g',
           'hgrn_w_o', 's5_lam_re', 's5_lam_im', 's5_log_dt', 's5_b_re', 's5_b_im', 's5_c_re', 's5_c_im', 's5_d',
           's5_w_glu', 'ret_w_in', 'ret_gn_g', 'ret_w_o', 'ffn_w_up', 'ffn_conv_w', 'ffn_conv_b', 'ffn_w_down']
SHARD_AXIS = {'meta_tokens': 1, 'mla_w_down': 1, 'mla_w_uq': 2, 'mla_w_ukv': 2, 'mla_w_o': 1, 'hgrn_w_in': 2,
              'hgrn_w_o': 1, 's5_d': 1, 's5_w_glu': 2, 'ret_w_in': 2, 'ret_gn_g': 1, 'ret_w_o': 1, 'ffn_w_up': 2,
              'ffn_conv_w': 2, 'ffn_w_down': 1}
BIG = ['mla_w_down', 'mla_w_uq', 'mla_w_ukv', 'mla_w_o', 'hgrn_w_in', 'hgrn_w_o', 's5_w_glu', 'ret_w_in', 'ret_w_o',
       'ffn_w_up', 'ffn_w_down']
SMALL_SHARDED = ['meta_tokens', 's5_d', 'ret_gn_g', 'ffn_conv_w']
REPLICATED = [n for n in WEIGHTS if n not in SHARD_AXIS]
SMALL_LATE = ['meta_tokens', 'norm_mix_g', 'mla_cq_norm_g', 'mla_ckv_norm_g', 'mla_q_head_g', 'mla_k_head_g']
SMALL_EARLY = [n for n in REPLICATED + SMALL_SHARDED if n not in SMALL_LATE]


def _cparams():
    return pltpu.CompilerParams(vmem_limit_bytes=VMEM_LIMIT_V7X)


def _dg(a, b, ca, cb):
    return lax.dot_general(a.astype(BF16), b.astype(BF16), (((ca,), (cb,)), ((), ())),
                           preferred_element_type=F32)


@jax.custom_vjp
def mm_nn(a, b):
    return _dg(a, b, 1, 0)


@jax.custom_vjp
def mm_nt(a, b):
    return _dg(a, b, 1, 1)


@jax.custom_vjp
def mm_tn(a, b):
    return _dg(a, b, 0, 0)


mm_nn.defvjp(lambda a, b: (mm_nn(a, b), (a, b)),
             lambda r, g: (mm_nt(g, r[1]).astype(r[0].dtype), mm_tn(r[0], g).astype(r[1].dtype)))
mm_nt.defvjp(lambda a, b: (mm_nt(a, b), (a, b)),
             lambda r, g: (mm_nn(g, r[1]).astype(r[0].dtype), mm_tn(g, r[0]).astype(r[1].dtype)))
mm_tn.defvjp(lambda a, b: (mm_tn(a, b), (a, b)),
             lambda r, g: (mm_nt(r[1], g).astype(r[0].dtype), mm_nn(r[0], g).astype(r[1].dtype)))


def _dot_f32(a, b):
    return jnp.dot(a, b, precision=HIGHEST, preferred_element_type=F32)


def _shift_rows(x, s, up):
    n = x.shape[0]
    r = lax.broadcasted_iota(jnp.int32, x.shape, 0)
    if up:
        return jnp.where(r < n - s, pltpu.roll(x, n - s, 0), 0.0)
    return jnp.where(r >= s, pltpu.roll(x, s, 0), 0.0)


@functools.partial(jax.custom_vjp, nondiff_argnums=(1,))
def shift_down(x, s):
    return _shift_rows(x, s, False)


shift_down.defvjp(lambda x, s: (_shift_rows(x, s, False), None), lambda s, _, g: (_shift_rows(g, s, True),))


@functools.partial(jax.custom_vjp, nondiff_argnums=(1,))
def shift_up(x, s):
    return _shift_rows(x, s, True)


shift_up.defvjp(lambda x, s: (_shift_rows(x, s, True), None), lambda s, _, g: (_shift_rows(g, s, False),))


def _swap32_impl(x):
    ax = x.ndim - 1
    lane = lax.broadcasted_iota(jnp.int32, x.shape, ax)
    return jnp.where(lane < 32, pltpu.roll(x, 96, ax), jnp.where(lane < 64, pltpu.roll(x, 32, ax), 0.0))


@jax.custom_vjp
def swap32(x):
    return _swap32_impl(x)


swap32.defvjp(lambda x: (_swap32_impl(x), None), lambda _, g: (_swap32_impl(g),))


def _rms(x, g):
    return x * lax.rsqrt(jnp.mean(x * x, axis=-1, keepdims=True) + EPS) * g


def _silu(x):
    return x * jax.nn.sigmoid(x)


def _row_ids(pid, n, shape, axis=0):
    return pid * n + lax.broadcasted_iota(jnp.int32, shape, axis)


class Arg:
    def __init__(self, arr, block, imap, diff=True, gdtype=F32):
        self.arr, self.block, self.imap, self.diff, self.gdtype = arr, block, imap, diff, gdtype


class Out:
    def __init__(self, shape, dtype, block, imap):
        self.shape, self.dtype, self.block, self.imap = shape, dtype, block, imap


def _free_axes(imap, grid):
    ng = len(grid)
    base = tuple(imap(*([0] * ng)))
    free = []
    for ax in range(ng):
        p = [0] * ng
        p[ax] = 1
        if grid[ax] > 1 and tuple(imap(*p)) == base:
            free.append(ax)
    walked = [ax for ax in range(ng) if grid[ax] > 1]
    assert free == walked[len(walked) - len(free):], "revisited blocks must be revisited on the innermost axes"
    return free


def _pallas(name, body, grid, in_specs, out_specs, out_shape, scratch, operands, rider=None, aliases=None):
    aliases = aliases or {}
    if isinstance(rider, (list, tuple)):
        riders = [r for r in rider if r is not None]
        rider = RiderGroup(riders) if riders else None
    if rider is None:
        return pl.pallas_call(body, grid=grid, in_specs=in_specs, out_specs=out_specs, out_shape=out_shape,
                              scratch_shapes=scratch, name=name, input_output_aliases=aliases,
                              compiler_params=_cparams())(*operands)
    n_in, n_out, n_sc = len(in_specs), len(out_specs), len(scratch)
    r_in, r_out = len(rider.operands), len(rider.out_shapes)

    def body_with_rider(*refs):
        ins, refs = refs[:n_in], refs[n_in:]
        r_ins, refs = refs[:r_in], refs[r_in:]
        outs, refs = refs[:n_out], refs[n_out:]
        r_outs, refs = refs[:r_out], refs[r_out:]
        sc, r_sc = refs[:n_sc], refs[n_sc:]
        pids = [pl.program_id(a) for a in range(len(grid))]
        first = functools.reduce(jnp.logical_and, [p == 0 for p in pids])
        last = functools.reduce(jnp.logical_and, [p == g - 1 for p, g in zip(pids, grid)])

        @pl.when(first)
        def _():
            rider.start(r_ins, r_outs, r_sc)

        if hasattr(rider, 'middle'):
            step = functools.reduce(lambda acc, pg: acc * pg[1] + pg[0], zip(pids, grid), 0)

            @pl.when(step == (math.prod(grid) * 5) // 6)
            def _():
                rider.middle(r_ins, r_outs, r_sc)

        body(*ins, *outs, *sc)

        @pl.when(last)
        def _():
            rider.finish(r_ins, r_outs, r_sc)

    res = pl.pallas_call(body_with_rider, grid=grid, in_specs=list(in_specs) + [ANY] * r_in,
                         out_specs=list(out_specs) + [ANY] * r_out, out_shape=list(out_shape) + rider.out_shapes,
                         scratch_shapes=list(scratch) + rider.scratch, name=name, input_output_aliases=aliases,
                         compiler_params=_cparams())(*operands, *rider.operands)
    rider.results = list(res[n_out:])
    return res[:n_out]


def stage_fwd(name, fn, grid, args, outs, state_shape=None, rider=None):
    n_in, n_out, ng = len(args), len(outs), len(grid)

    def body(*refs):
        pids = tuple(pl.program_id(a) for a in range(ng))
        vals = [r[...] for r in refs[:n_in]]
        o_refs = refs[n_in:n_in + n_out]
        if state_shape is None:
            res = fn(pids, *vals)
        else:
            sv_ref, st_ref = refs[n_in + n_out], refs[n_in + n_out + 1]

            @pl.when(pids[-1] == 0)
            def _():
                st_ref[...] = jnp.zeros(state_shape, F32)

            s = st_ref[...]
            sv_ref[...] = s
            res = fn(pids, *vals, s)
            st_ref[...] = res[-1]
            res = res[:-1]
        for r, v in zip(o_refs, res):
            r[...] = v.astype(r.dtype)

    in_specs = [pl.BlockSpec(a.block, a.imap) for a in args]
    out_specs = [pl.BlockSpec(o.block, o.imap) for o in outs]
    out_shape = [jax.ShapeDtypeStruct(o.shape, o.dtype) for o in outs]
    scratch = []
    if state_shape is not None:
        nz = len(state_shape)
        out_specs.append(pl.BlockSpec((None, None) + tuple(state_shape), lambda i, j: (i, j) + (0,) * nz))
        out_shape.append(jax.ShapeDtypeStruct(tuple(grid) + tuple(state_shape), F32))
        scratch = [pltpu.VMEM(state_shape, F32)]
    return _pallas(name, body, grid, in_specs, out_specs, out_shape, scratch, [a.arr for a in args], rider)


def stage_bwd(name, fn, grid, args, outs, cots, state_shape=None, states=None, rider=None):
    n_in, n_out, ng = len(args), len(outs), len(grid)
    nb = grid[-1]
    rev = state_shape is not None
    didx = [k for k, a in enumerate(args) if a.diff]
    frees = [_free_axes(args[k].imap, grid) for k in didx]

    def eff(p):
        return tuple(p[:-1]) + (nb - 1 - p[-1],) if rev else tuple(p)

    def wrap(imap):
        return lambda *p: imap(*eff(p))

    def body(*refs):
        pids = tuple(pl.program_id(a) for a in range(ng))
        e = eff(pids)
        vals = [r[...] for r in refs[:n_in]]
        cts = tuple(r[...].astype(F32) for r in refs[n_in:n_in + n_out])
        pos = n_in + n_out
        if rev:
            st_in_ref = refs[pos]
            pos += 1
        g_refs = refs[pos:pos + len(didx)]
        pos += len(didx)
        dvals = [vals[k].astype(F32) for k in didx]

        def f(*dv):
            full = list(vals)
            for k, v in zip(didx, dv[:len(didx)]):
                full[k] = v
            return tuple(fn(e, *full, *dv[len(didx):]))

        if rev:
            ds_ref = refs[pos]

            @pl.when(pids[-1] == 0)
            def _():
                ds_ref[...] = jnp.zeros(state_shape, F32)

            _, vjp = jax.vjp(f, *dvals, st_in_ref[...])
            grads = vjp(cts + (ds_ref[...],))
            ds_ref[...] = grads[-1]
            grads = grads[:-1]
        else:
            _, vjp = jax.vjp(f, *dvals)
            grads = vjp(cts)
        for gref, g, free in zip(g_refs, grads, frees):
            g = g.astype(F32)
            if not free:
                gref[...] = g.astype(gref.dtype)
            else:
                first = functools.reduce(jnp.logical_and, [pids[ax] == 0 for ax in free])

                @pl.when(first)
                def _():
                    gref[...] = g

                @pl.when(jnp.logical_not(first))
                def _():
                    gref[...] += g

    in_specs = [pl.BlockSpec(a.block, wrap(a.imap)) for a in args]
    in_specs += [pl.BlockSpec(o.block, wrap(o.imap)) for o in outs]
    operands = [a.arr for a in args] + list(cots)
    scratch = []
    if rev:
        nz = len(state_shape)
        in_specs.append(pl.BlockSpec((None, None) + tuple(state_shape), lambda i, j: (i, nb - 1 - j) + (0,) * nz))
        operands.append(states)
        scratch = [pltpu.VMEM(state_shape, F32)]
    out_specs = [pl.BlockSpec(args[k].block, wrap(args[k].imap)) for k in didx]
    assert all(args[k].gdtype == F32 or not free for k, free in zip(didx, frees))
    out_shape = [jax.ShapeDtypeStruct(args[k].arr.shape, args[k].gdtype) for k in didx]
    return _pallas(name, body, grid, in_specs, out_specs, out_shape, scratch, operands, rider)


def _divisors(n, cands):
    return [c for c in cands if n % c == 0] or [n]


def _nbytes(dt):
    return jnp.dtype(dt).itemsize


def matmul(name, a, b, mode, out_dtype=F32, res=None, mask=False, res_mask=True, window=None, into=None):
    sa, sb, so = _nbytes(a.dtype), _nbytes(b.dtype), _nbytes(out_dtype)
    off, width = (window[0], window[1]) if window is not None else (0, None)
    if mode in ('nn', 'nt'):
        M, K = a.shape
        N = (width or b.shape[1]) if mode == 'nn' else b.shape[0]
        assert mode == 'nn' or width is None or width == K
        best = None
        for tm in _divisors(M, (1408, 1056, 768, 384, 128)):
            for tn in _divisors(N, (1408, 1024, 768, 512, 384, 256, 128)):
                est = 2 * (tm * K * sa + tn * K * sb + tm * tn * (so + (4 if res is not None else 0)))
                if est <= MM_VMEM_BUDGET and (best is None or tm * tn > best[0] * best[1]):
                    best = (tm, tn)
        tm, tn = best
        grid = (M // tm, N // tn)

        def body(*refs):
            a_ref, b_ref = refs[0], refs[1]
            o_ref = refs[-1]
            x = a_ref[...]
            rows = _row_ids(pl.program_id(0), tm, (tm, 1))
            if mask:
                x = jnp.where(rows >= PAD, x, jnp.zeros_like(x))
            acc = _dg(x, b_ref[...], 1, 0 if mode == 'nn' else 1)
            if res is not None:
                acc = refs[2][...] + (jnp.where(rows >= PAD, acc, 0.0) if res_mask else acc)
            o_ref[...] = acc.astype(o_ref.dtype)

        assert off % (tn if mode == 'nn' else K) == 0
        cb, kb = off // tn, off // K
        in_specs = [pl.BlockSpec((tm, K), lambda i, j: (i, 0)),
                    pl.BlockSpec((K, tn), lambda i, j: (0, j + cb)) if mode == 'nn' else
                    pl.BlockSpec((tn, K), lambda i, j: (j, kb))]
        ops = [a, b]
        if res is not None:
            in_specs.append(pl.BlockSpec((tm, tn), lambda i, j: (i, j)))
            ops.append(res)
        return pl.pallas_call(body, grid=grid, in_specs=in_specs,
                              out_specs=pl.BlockSpec((tm, tn), lambda i, j: (i, j)),
                              out_shape=jax.ShapeDtypeStruct((M, N), out_dtype), name=name,
                              compiler_params=_cparams())(*ops)
    assert mode == 'tn' and res is None
    M, K = a.shape
    N = b.shape[1]
    best = None
    for tk in _divisors(K, (1408, 1024, 768, 512, 384, 256, 128)):
        for tn in _divisors(N, (1408, 1024, 768, 512, 384, 256, 128)):
            est = 2 * (M * tk * sa + M * tn * sb + tk * tn * so)
            if est <= MM_VMEM_BUDGET and (best is None or tk * tn > best[0] * best[1]):
                best = (tk, tn)
    tk, tn = best

    def body_t(*refs):
        a_ref, b_ref, o_ref = refs[0], refs[1], refs[-1]
        y = b_ref[...]
        if mask:
            rows = lax.broadcasted_iota(jnp.int32, (M, 1), 0)
            y = jnp.where(rows >= PAD, y, jnp.zeros_like(y))
        o_ref[...] = _dg(a_ref[...], y, 0, 0).astype(o_ref.dtype)

    assert off % tn == 0
    cb = off // tn
    total = window[2] if window is not None else N
    in_specs = [pl.BlockSpec((M, tk), lambda i, j: (0, i)), pl.BlockSpec((M, tn), lambda i, j: (0, j))]
    ops, alias = [a, b], {}
    if into is not None:
        in_specs.append(ANY)
        ops.append(into)
        alias = {2: 0}
    return pl.pallas_call(body_t, grid=(K // tk, N // tn), in_specs=in_specs,
                          out_specs=pl.BlockSpec((tk, tn), lambda i, j: (i, j + cb)),
                          out_shape=jax.ShapeDtypeStruct((K, total), out_dtype), name=name,
                          input_output_aliases=alias, compiler_params=_cparams())(*ops)


def linear_bwd(name, act, w, dy, mask=False, da_dtype=F32):
    return (matmul(name + "_da", dy, w, 'nt', out_dtype=da_dtype, mask=mask),
            matmul(name + "_dw", act, dy, 'tn', out_dtype=BF16, mask=mask))


def _row_tile(T):
    return _divisors(T, (384, 128))[0]


def _rows(arr, tm, gdtype=F32):
    return Arg(arr, (tm, arr.shape[1]), lambda i: (i, 0), gdtype=gdtype)


def _const(arr, diff=True):
    return Arg(arr, arr.shape, lambda *p: (0,) * arr.ndim, diff)


def _norm_fn(pids, h, g):
    return (_rms(h, g),)


def _norm_bwd_fn(pids, h, g):
    return (_rms(h, g), h)


def norm_fwd(name, h, g, dtype):
    T = h.shape[0]
    tm = _row_tile(T)
    return stage_fwd(name, _norm_fn, (T // tm,), [_rows(h, tm), _const(g)],
                     [Out((T, D), dtype, (tm, D), lambda i: (i, 0))])[0]


def norm_bwd(name, h, g, da, dh):
    T = h.shape[0]
    tm = _row_tile(T)
    o = Out((T, D), F32, (tm, D), lambda i: (i, 0))
    return stage_bwd(name, _norm_bwd_fn, (T // tm,), [_rows(h, tm), _const(g)], [o, o], [da, dh])


def _causal_conv3(u, cw, cb):
    return cw[2:3] * u + cw[1:2] * shift_down(u, 1) + cw[0:1] * shift_down(u, 2) + cb


def _ffn_act_fn(pids, ug, uv, cwg, cwv, cbg, cbv):
    return (_silu(_causal_conv3(ug, cwg, cbg)) * _causal_conv3(uv, cwv, cbv),)


def _ffn_act_args(ug, uv, cw, cb):
    T = ug.shape[0]
    col = lambda j: (0, j)
    args = [Arg(ug, (T, 128), col, gdtype=BF16), Arg(uv, (T, 128), col, gdtype=BF16),
            Arg(cw[:, :FFN_F], (3, 128), col), Arg(cw[:, FFN_F:], (3, 128), col),
            Arg(cb[:, :FFN_F], (1, 128), col), Arg(cb[:, FFN_F:], (1, 128), col)]
    outs = [Out((T, FFN_F), BF16, (T, 128), col)]
    return (FFN_F // 128,), args, outs


def _interleave_cols(w, n_parts, tile=128):
    lead = w.shape[:-1]
    n = w.shape[-1] // (n_parts * tile)
    k = len(lead)
    return w.reshape(lead + (n_parts, n, tile)).transpose(tuple(range(k)) + (k + 1, k, k + 2)).reshape(w.shape)


def _deinterleave_cols(w, n_parts, tile=128):
    lead = w.shape[:-1]
    n = w.shape[-1] // (n_parts * tile)
    k = len(lead)
    return w.reshape(lead + (n, n_parts, tile)).transpose(tuple(range(k)) + (k + 1, k, k + 2)).reshape(w.shape)


def ffn_layer(i, h, g, w_up, cw, cb, w_down):
    gate_w, val_w = (0, FFN_F, 2 * FFN_F), (FFN_F, FFN_F, 2 * FFN_F)
    b = norm_fwd(f"ffn{i}_norm", h, g, BF16)
    ug = matmul(f"ffn{i}_up_g", b, w_up, 'nn', out_dtype=BF16, window=gate_w)
    uv = matmul(f"ffn{i}_up_v", b, w_up, 'nn', out_dtype=BF16, window=val_w)
    grid, args, outs = _ffn_act_args(ug, uv, cw, cb)
    p = stage_fwd(f"ffn{i}_act", _ffn_act_fn, grid, args, outs)[0]
    h_new = matmul(f"ffn{i}_down", p, w_down, 'nn', res=h)

    def bwd(dh):
        dp, dwd = linear_bwd(f"ffn{i}_down_b", p, w_down, dh, mask=True, da_dtype=BF16)
        dug, duv, dcwg, dcwv, dcbg, dcbv = stage_bwd(f"ffn{i}_act_b", _ffn_act_fn, grid, args, outs, [dp])
        db = matmul(f"ffn{i}_up_b_da_g", dug, w_up, 'nt', window=gate_w)
        db = matmul(f"ffn{i}_up_b_da_v", duv, w_up, 'nt', window=val_w, res=db, res_mask=False)
        dwu = matmul(f"ffn{i}_up_b_dw_g", b, dug, 'tn', out_dtype=BF16, window=gate_w)
        dwu = matmul(f"ffn{i}_up_b_dw_v", b, duv, 'tn', out_dtype=BF16, window=val_w, into=dwu)
        dh2, dg = norm_bwd(f"ffn{i}_norm_b", h, g, db, dh)
        return dh2, dict(g=dg, w_up=dwu, cw=jnp.concatenate([dcwg, dcwv], axis=1),
                         cb=jnp.concatenate([dcbg, dcbv], axis=1), w_down=dwd)

    return h_new, bwd


def _mla_latent_fn(pids, down, gcq, gckv):
    cq = _rms(down[:, :MLA_QL], gcq)
    ckv = _rms(down[:, MLA_QL:MLA_QL + MLA_KVL], gckv)
    return cq, ckv, down[:, MLA_QL + MLA_KVL:]


def _rope64(x, cos, sin_signed):
    return x * cos + swap32(x) * sin_signed


def _mla_heads_fn(pids, qraw, kv, kpe, gqn, gqr, gkn, gkr, cos, sin_signed):
    qn, qr = qraw[:, :128], qraw[:, 128:]
    rq = lax.rsqrt((jnp.sum(qn * qn, -1, keepdims=True) + jnp.sum(qr * qr, -1, keepdims=True)) / MLA_QK + EPS)
    q = jnp.concatenate([qn * rq * gqn, _rope64(qr * rq * gqr, cos, sin_signed)], axis=1)
    kn, v = kv[:, :128], kv[:, 128:]
    rk = lax.rsqrt((jnp.sum(kn * kn, -1, keepdims=True) + jnp.sum(kpe * kpe, -1, keepdims=True)) / MLA_QK + EPS)
    k = jnp.concatenate([kn * rk * gkn, _rope64(kpe * rk * gkr, cos, sin_signed)], axis=1)
    return q, k, v


def _chunk_id(r):
    return jnp.where(r < LEAD, 0, 1 + lax.shift_right_arithmetic(r - LEAD, 6))


ATT_T = 384
ATT_SCALE = MLA_QK ** -0.5


def _att_mask(s, qb, kb):
    qrow = _row_ids(qb, ATT_T, (ATT_T, 1))
    krow = _row_ids(kb, ATT_T, (1, ATT_T), axis=1)
    ok = jnp.logical_and(_chunk_id(krow) <= _chunk_id(qrow), krow >= PAD)
    return jnp.where(ok, s, NEG_INF)


def _att_scores(q, k_ref, qb, kb, masked):
    ks = k_ref[pl.ds(pl.multiple_of(kb * ATT_T, ATT_T), ATT_T), :]
    s = _dg(q, ks, 1, 1) * ATT_SCALE
    return (_att_mask(s, qb, kb) if masked else s), ks


def _att_key_loop(j, step, init):
    carry = step(0, init, True)
    n_mid = jnp.maximum(j - 1, 0)
    carry = lax.fori_loop(0, n_mid // 2, lambda i, c: step(2 * i + 2, step(2 * i + 1, c, False), False), carry)
    carry = lax.cond(n_mid % 2 == 1, lambda c: step(j - 1, c, False), lambda c: c, carry)
    return lax.cond(j > 0, lambda c: step(j, c, True), lambda c: c, carry)


def _att_rows(ref, b):
    return ref[pl.ds(pl.multiple_of(b * ATT_T, ATT_T), ATT_T), :]


def attention_fwd(q, k, v, rider=None):
    H, T, _ = q.shape
    nq = T // ATT_T

    def body(q_ref, k_ref, v_ref, o_ref, lse_ref):
        j = pl.program_id(1)
        qv = q_ref[...]

        def step(kb, carry, masked):
            m, l, acc = carry
            s, _ = _att_scores(qv, k_ref, j, kb, masked)
            m_new = jnp.maximum(m, jnp.max(s, axis=-1, keepdims=True))
            p = jnp.exp(s - m_new)
            alpha = jnp.exp(m - m_new)
            return (m_new, alpha * l + jnp.sum(p, axis=-1, keepdims=True),
                    alpha * acc + _dg(p, _att_rows(v_ref, kb), 1, 0))

        init = (jnp.full((ATT_T, 1), NEG_INF, F32), jnp.zeros((ATT_T, 1), F32), jnp.zeros((ATT_T, MLA_V), F32))
        m, l, acc = _att_key_loop(j, step, init)
        o_ref[...] = acc / l
        lse_ref[...] = m + jnp.log(l)

    return _pallas("mla_attn", body, (H, nq),
                   [pl.BlockSpec((None, ATT_T, 256), lambda hh, j: (hh, j, 0)),
                    pl.BlockSpec((None, T, 256), lambda hh, j: (hh, 0, 0)),
                    pl.BlockSpec((None, T, MLA_V), lambda hh, j: (hh, 0, 0))],
                   [pl.BlockSpec((ATT_T, MLA_V), lambda hh, j: (j, hh)),
                    pl.BlockSpec((None, ATT_T, 1), lambda hh, j: (hh, j, 0))],
                   [jax.ShapeDtypeStruct((T, H * MLA_V), F32), jax.ShapeDtypeStruct((H, T, 1), F32)], [],
                   [q, k, v], rider)


def attention_bwd(q, k, v, o, lse, do, rider=None, rider_dq=None):
    H, T, _ = q.shape
    nq = T // ATT_T

    def dq_body(q_ref, k_ref, v_ref, o_ref, do_ref, lse_ref, dq_ref, delta_ref):
        j = pl.program_id(1)
        qv, dov, lsev = q_ref[...], do_ref[...], lse_ref[...]
        delta = jnp.sum(dov * o_ref[...], axis=-1, keepdims=True)
        delta_ref[...] = delta

        def step(kb, acc, masked):
            s, ks = _att_scores(qv, k_ref, j, kb, masked)
            p = jnp.exp(s - lsev)
            ds = p * (_dg(dov, _att_rows(v_ref, kb), 1, 1) - delta) * ATT_SCALE
            return acc + _dg(ds, ks, 1, 0)

        dq_ref[...] = _att_key_loop(j, step, jnp.zeros((ATT_T, 256), F32))

    q_blk = pl.BlockSpec((None, ATT_T, 256), lambda hh, j: (hh, j, 0))
    k_all = pl.BlockSpec((None, T, 256), lambda hh, j: (hh, 0, 0))
    v_all = pl.BlockSpec((None, T, MLA_V), lambda hh, j: (hh, 0, 0))
    o_blk = pl.BlockSpec((ATT_T, MLA_V), lambda hh, j: (j, hh))
    col_blk = pl.BlockSpec((None, ATT_T, 1), lambda hh, j: (hh, j, 0))
    dq, delta = _pallas("mla_attn_dq", dq_body, (H, nq), [q_blk, k_all, v_all, o_blk, o_blk, col_blk],
                        [q_blk, col_blk],
                        [jax.ShapeDtypeStruct((H, T, 256), F32), jax.ShapeDtypeStruct((H, T, 1), F32)], [],
                        [q, k, v, o, do, lse], rider_dq)

    def dkv_body(q_ref, k_ref, v_ref, do_ref, lse_ref, delta_ref, dk_ref, dv_ref):
        kb = pl.program_id(1)
        kv = k_ref[...]
        vv = v_ref[...]

        def step(qb, carry, masked):
            dk, dv = carry
            qv, dov = _att_rows(q_ref, qb), _att_rows(do_ref, qb)
            s = _dg(qv, kv, 1, 1) * ATT_SCALE
            if masked:
                s = _att_mask(s, qb, kb)
            p = jnp.exp(s - _att_rows(lse_ref, qb))
            ds = p * (_dg(dov, vv, 1, 1) - _att_rows(delta_ref, qb)) * ATT_SCALE
            return dk + _dg(ds, qv, 0, 0), dv + _dg(p, dov, 0, 0)

        carry = step(kb, (jnp.zeros((ATT_T, 256), F32), jnp.zeros((ATT_T, MLA_V), F32)), True)

        def later_blocks(c, masked):
            n = nq - 1 - kb
            c = lax.fori_loop(0, n // 2,
                              lambda i, cc: step(kb + 2 + 2 * i, step(kb + 1 + 2 * i, cc, masked), masked), c)
            return lax.cond(n % 2 == 1, lambda cc: step(nq - 1, cc, masked), lambda cc: cc, c)

        dk, dv = lax.cond(kb == 0, lambda c: later_blocks(c, True), lambda c: later_blocks(c, False), carry)
        dk_ref[...] = dk
        dv_ref[...] = dv

    q_all = pl.BlockSpec((None, T, 256), lambda hh, j: (hh, 0, 0))
    v_blk = pl.BlockSpec((None, ATT_T, MLA_V), lambda hh, j: (hh, j, 0))
    do_all = pl.BlockSpec((T, MLA_V), lambda hh, j: (0, hh))
    col_all = pl.BlockSpec((None, T, 1), lambda hh, j: (hh, 0, 0))
    dk, dv = _pallas("mla_attn_dkv", dkv_body, (H, nq), [q_all, q_blk, v_blk, do_all, col_all, col_all],
                     [q_blk, v_blk],
                     [jax.ShapeDtypeStruct((H, T, 256), F32), jax.ShapeDtypeStruct((H, T, MLA_V), F32)], [],
                     [q, k, v, do, lse, delta], rider)
    return dq, dk, dv


def mla_mixer(h, g, w, tabs, rider=None):
    T = h.shape[0]
    tm = _row_tile(T)
    nt = T // tm
    a = norm_fwd("mla_norm", h, g, BF16)
    down = matmul("mla_down", a, w['w_down'], 'nn')
    lat_args = [_rows(down, tm, BF16), _const(w['gcq']), _const(w['gckv'])]
    lat_outs = [Out((T, MLA_QL), BF16, (tm, MLA_QL), lambda i: (i, 0)),
                Out((T, MLA_KVL), BF16, (tm, MLA_KVL), lambda i: (i, 0)),
                Out((T, 128), F32, (tm, 128), lambda i: (i, 0))]
    cq, ckv, kpe = stage_fwd("mla_latent", _mla_latent_fn, (nt,), lat_args, lat_outs)
    qraw = matmul("mla_uq", cq, w['w_uq'], 'nn')
    kv = matmul("mla_ukv", ckv, w['w_ukv'], 'nn')
    hd_args = [Arg(qraw, (tm, 256), lambda i, hh: (i, hh), gdtype=BF16),
               Arg(kv, (tm, 256), lambda i, hh: (i, hh), gdtype=BF16),
               Arg(kpe, (tm, 128), lambda i, hh: (i, 0)),
               _const(w['gqn']), _const(w['gqr']), _const(w['gkn']), _const(w['gkr']),
               Arg(tabs['cos_a'], (tm, 128), lambda i, hh: (i, 0), False),
               Arg(tabs['sin_a'], (tm, 128), lambda i, hh: (i, 0), False)]
    hd_outs = [Out((MLA_H, T, 256), BF16, (None, tm, 256), lambda i, hh: (hh, i, 0)),
               Out((MLA_H, T, 256), BF16, (None, tm, 256), lambda i, hh: (hh, i, 0)),
               Out((MLA_H, T, 128), BF16, (None, tm, 128), lambda i, hh: (hh, i, 0))]
    q, k, v = stage_fwd("mla_heads", _mla_heads_fn, (nt, MLA_H), hd_args, hd_outs)
    o, lse = attention_fwd(q, k, v, rider=rider)
    h_new = matmul("mla_o", o, w['w_o'], 'nn', res=h)

    def bwd(dh, rider_dkv=None, rider_dq=None, rider_heads=None):
        do, dwo = linear_bwd("mla_o_b", o, w['w_o'], dh, mask=True)
        dq, dk, dv = attention_bwd(q, k, v, o, lse, do, rider=rider_dkv and rider_dkv(dwo),
                                   rider_dq=rider_dq and rider_dq())
        dqraw, dkv, dkpe, dgqn, dgqr, dgkn, dgkr = stage_bwd("mla_heads_b", _mla_heads_fn, (nt, MLA_H), hd_args,
                                                             hd_outs, [dq, dk, dv],
                                                             rider=rider_heads and rider_heads())
        dcq, dwuq = linear_bwd("mla_uq_b", cq, w['w_uq'], dqraw)
        dckv, dwukv = linear_bwd("mla_ukv_b", ckv, w['w_ukv'], dkv)
        ddown, dgcq, dgckv = stage_bwd("mla_latent_b", _mla_latent_fn, (nt,), lat_args, lat_outs, [dcq, dckv, dkpe])
        da, dwdown = linear_bwd("mla_down_b", a, w['w_down'], ddown)
        dh2, dg = norm_bwd("mla_norm_b", h, g, da, dh)
        return dh2, dict(g=dg, w_down=dwdown, gcq=dgcq, gckv=dgckv, w_uq=dwuq, w_ukv=dwukv, gqn=dgqn, gqr=dgqr,
                         gkn=dgkn, gkr=dgkr, w_o=dwo)

    return h_new, bwd


HG_R = 384


def _hgrn_fn(pids, z, lb, go, st):
    outs = []
    for lo in range(0, z.shape[0], 128):
        o, st = _hgrn_block(z[lo:lo + 128], lb, go, st)
        outs.append(o)
    return jnp.concatenate(outs, axis=0), st


def _hgrn_block(z, lb, go, st):
    R = z.shape[0]
    zq, zf, zi, zg = z[:, :128], z[:, 128:256], z[:, 256:384], z[:, 384:]
    assert R == 128
    q = _silu(zq)
    fg = lb + (1.0 - lb) * jax.nn.sigmoid(zf)
    logf = jnp.log(fg)
    k = 1.0 - fg
    row = lax.broadcasted_iota(jnp.int32, logf.shape, 0)
    pos = row & (HG_C - 1)
    cum, rev = logf, logf
    for d in (1, 2, 4, 8):
        cum = cum + jnp.where(pos >= d, shift_down(cum, d), 0.0)
        rev = rev + jnp.where(pos < HG_C - d, shift_up(rev, d), 0.0)
    cums, tots = [cum], [cum + rev - logf]
    for s in (16, 32, 64):
        odd = (row & s) != 0
        before = shift_down(tots[-1], s)
        cums.append(cums[-1] + jnp.where(odd, before, 0.0))
        tots.append(tots[-1] + jnp.where(odd, before, shift_up(tots[-1], s)))
    t = lax.broadcasted_iota(jnp.int32, (R, R), 0)
    j = lax.broadcasted_iota(jnp.int32, (R, R), 1)
    sh = lax.shift_right_arithmetic
    a = jnp.where(jnp.logical_and(sh(t, 4) == sh(j, 4), j <= t), mm_nt(q * jnp.exp(cum), k * jnp.exp(-cum)), 0.0)
    for n, s in enumerate((16, 32, 64)):
        m = jnp.logical_and(sh(t, 5 + n) == sh(j, 5 + n), jnp.logical_and((t & s) != 0, (j & s) == 0))
        a = a + jnp.where(m, mm_nt(q * jnp.exp(cums[n]), k * jnp.exp(tots[n] - cums[n])), 0.0)
    o = mm_nn(a, zi) + mm_nt(q * jnp.exp(cums[3]), st)
    st = st * jnp.exp(tots[3][0:1, :]) + mm_tn(zi, k * jnp.exp(tots[3] - cums[3]))
    return _rms(o, go) * _silu(zg), st


def hgrn_mixer(h, g, w, rider=None):
    T = h.shape[0]
    a = norm_fwd("hgrn_norm", h, g, BF16)
    z = matmul("hgrn_in", a, w['w_in'], 'nn')
    grid = (HG_H, T // HG_R)
    args = [Arg(z, (HG_R, 512), lambda hh, j: (j, hh), gdtype=BF16), Arg(w['lb'], (1, 128), lambda hh, j: (0, hh)),
            _const(w['go'])]
    outs = [Out((T, D), BF16, (HG_R, 128), lambda hh, j: (j, hh))]
    o, states = stage_fwd("hgrn_gla", _hgrn_fn, grid, args, outs, state_shape=(HG_D, HG_D), rider=rider)
    h_new = matmul("hgrn_o", o, w['w_o'], 'nn', res=h)

    def bwd(dh, rider=None):
        do, dwo = linear_bwd("hgrn_o_b", o, w['w_o'], dh, mask=True)
        dz, dlb, dgo = stage_bwd("hgrn_gla_b", _hgrn_fn, grid, args, outs, [do], state_shape=(HG_D, HG_D),
                                 states=states, rider=rider)
        da, dwin = linear_bwd("hgrn_in_b", a, w['w_in'], dz)
        dh2, dg = norm_bwd("hgrn_norm_b", h, g, da, dh)
        return dh2, dict(g=dg, w_in=dwin, lb=dlb, go=dgo, w_o=dwo)

    return h_new, bwd


S5_R = 384
S5_W = 512
S5_SLABS = D // 128


def _cmul(ar, ai, br, bi):
    return ar * br - ai * bi, ar * bi + ai * br


def _s5_scan(br, bi, tab, cr, ci, reverse):
    R, W = br.shape
    G = R // 8
    xr, xi = br.reshape(G, 8, W), bi.reshape(G, 8, W)
    for n, d in enumerate((1, 2, 4)):
        sh = (8 - d) if reverse else d
        mr, mi = _cmul(tab[2 * n][None], tab[2 * n + 1][None], pltpu.roll(xr, sh, 1), pltpu.roll(xi, sh, 1))
        xr, xi = xr + mr, xi + mi
    pr, pi = tab[6], tab[7]
    edge = 0 if reverse else 7
    out_r, out_i = [None] * G, [None] * G
    for g in (range(G - 1, -1, -1) if reverse else range(G)):
        ar, ai = _cmul(pr, pi, cr, ci)
        gr, gi = xr[g] + ar, xi[g] + ai
        cr, ci = gr[edge:edge + 1], gi[edge:edge + 1]
        out_r[g], out_i[g] = gr, gi
    return jnp.concatenate(out_r, axis=0), jnp.concatenate(out_i, axis=0), cr, ci


def s5_scan_fwd(a, bb, cb, tab, rider=None):
    T = a.shape[0]
    nb = T // S5_R

    def body(a_ref, bb_ref, cb_ref, tab_ref, y_ref, xs_ref, c_ref):
        @pl.when(pl.program_id(1) == 0)
        def _():
            c_ref[...] = jnp.zeros(c_ref.shape, F32)

        bu = _dg(a_ref[...], bb_ref[...], 1, 0)
        t = tab_ref[...]
        xr, xi, cr, ci = _s5_scan(bu[:, :S5_W], bu[:, S5_W:], t, c_ref[0:1, :S5_W], c_ref[0:1, S5_W:], False)
        x = jnp.concatenate([xr, xi], axis=1)
        xs_ref[...] = x
        y_ref[...] = _dg(x, cb_ref[...], 1, 0)
        c_ref[0:1, :] = jnp.concatenate([cr, ci], axis=1)

    return _pallas(
        "s5_scan", body, (S5_SLABS, nb),
        [pl.BlockSpec((S5_R, 128), lambda j, i: (i, j)),
         pl.BlockSpec((None, 128, 2 * S5_W), lambda j, i: (j, 0, 0)),
         pl.BlockSpec((None, 2 * S5_W, 128), lambda j, i: (j, 0, 0)),
         pl.BlockSpec((None, 10, 8, S5_W), lambda j, i: (j, 0, 0, 0))],
        [pl.BlockSpec((S5_R, 128), lambda j, i: (i, j)),
         pl.BlockSpec((None, S5_R, 2 * S5_W), lambda j, i: (j, i, 0))],
        [jax.ShapeDtypeStruct((T, D), F32), jax.ShapeDtypeStruct((S5_SLABS, T, 2 * S5_W), F32)],
        [pltpu.VMEM((8, 2 * S5_W), F32)], [a, bb, cb, tab], rider)


def s5_scan_bwd(a, bb, cb, tab_rev, xs, dy, rider=None):
    T = a.shape[0]
    nb = T // S5_R
    rg = S5_R // 8

    def body(a_ref, dy_ref, xs_ref, xp_ref, bb_ref, cb_ref, tab_ref, da_ref, dbb_ref, dcb_ref, dab_ref, c_ref):
        i = pl.program_id(1)

        @pl.when(i == 0)
        def _():
            c_ref[...] = jnp.zeros(c_ref.shape, F32)

        dy_v = dy_ref[...]
        x = xs_ref[...]
        dxo = _dg(dy_v, cb_ref[...], 1, 1)
        gr, gi, cr, ci = _s5_scan(dxo[:, :S5_W], dxo[:, S5_W:], tab_ref[...], c_ref[0:1, :S5_W], c_ref[0:1, S5_W:], True)
        c_ref[0:1, :] = jnp.concatenate([cr, ci], axis=1)
        g = jnp.concatenate([gr, gi], axis=1)
        da_ref[...] = _dg(g, bb_ref[...], 1, 1)
        dbb = _dg(a_ref[...], g, 0, 0)
        dcb = _dg(x, dy_v, 0, 0)
        first_tile = i == nb - 1
        prev_last = jnp.where(first_tile, 0.0, xp_ref[7:8, :])
        rows = lax.broadcasted_iota(jnp.int32, x.shape, 0)
        xp = jnp.where(rows == 0, prev_last, pltpu.roll(x, 1, 0))
        xpr, xpi = xp[:, :S5_W], xp[:, S5_W:]
        dar = (gr * xpr + gi * xpi).reshape(rg, 8, S5_W).sum(axis=0)
        dai = (gi * xpr - gr * xpi).reshape(rg, 8, S5_W).sum(axis=0)
        dab = jnp.concatenate([dar, dai], axis=1)

        @pl.when(i == 0)
        def _():
            dbb_ref[...] = dbb
            dcb_ref[...] = dcb
            dab_ref[...] = dab

        @pl.when(i != 0)
        def _():
            dbb_ref[...] += dbb
            dcb_ref[...] += dcb
            dab_ref[...] += dab

    def prev_rows(j, i):
        return (j, jnp.maximum((nb - 1 - i) * rg - 1, 0), 0)

    return _pallas(
        "s5_scan_b", body, (S5_SLABS, nb),
        [pl.BlockSpec((S5_R, 128), lambda j, i: (nb - 1 - i, j)),
         pl.BlockSpec((S5_R, 128), lambda j, i: (nb - 1 - i, j)),
         pl.BlockSpec((None, S5_R, 2 * S5_W), lambda j, i: (j, nb - 1 - i, 0)),
         pl.BlockSpec((None, 8, 2 * S5_W), prev_rows),
         pl.BlockSpec((None, 128, 2 * S5_W), lambda j, i: (j, 0, 0)),
         pl.BlockSpec((None, 2 * S5_W, 128), lambda j, i: (j, 0, 0)),
         pl.BlockSpec((None, 10, 8, S5_W), lambda j, i: (j, 0, 0, 0))],
        [pl.BlockSpec((S5_R, 128), lambda j, i: (nb - 1 - i, j)),
         pl.BlockSpec((None, 128, 2 * S5_W), lambda j, i: (j, 0, 0)),
         pl.BlockSpec((None, 2 * S5_W, 128), lambda j, i: (j, 0, 0)),
         pl.BlockSpec((None, 8, 2 * S5_W), lambda j, i: (j, 0, 0))],
        [jax.ShapeDtypeStruct((T, D), F32), jax.ShapeDtypeStruct((S5_SLABS, 128, 2 * S5_W), F32),
         jax.ShapeDtypeStruct((S5_SLABS, 2 * S5_W, 128), F32), jax.ShapeDtypeStruct((S5_SLABS, 8, 2 * S5_W), F32)],
        [pltpu.VMEM((8, 2 * S5_W), F32)], [a, dy, xs, xs, bb, cb, tab_rev], rider)


def _s5_discretise(lam_re, lam_im, log_dt, b_re, b_im, c_re, c_im):
    dt = jnp.exp(log_dt)[:, None]
    mag = jnp.exp(lam_re * dt)
    abar_re = mag * jnp.cos(lam_im * dt)
    abar_im = mag * jnp.sin(lam_im * dt)
    den = lam_re * lam_re + lam_im * lam_im
    zoh_re = ((abar_re - 1.0) * lam_re + abar_im * lam_im) / den
    zoh_im = (abar_im * lam_re - (abar_re - 1.0) * lam_im) / den
    bbar_re = zoh_re[..., None] * b_re - zoh_im[..., None] * b_im
    bbar_im = zoh_re[..., None] * b_im + zoh_im[..., None] * b_re
    eye = jnp.eye(8, dtype=F32)

    def in_map(bbar):
        t = bbar.reshape(8, 8, S5_P, S5_K).transpose(0, 1, 3, 2)
        return (t[:, :, :, None, :] * eye[None, :, None, :, None]).reshape(8, 8 * S5_K, 8 * S5_P)

    def out_map(c):
        t = c.reshape(8, 8, S5_K, S5_P).transpose(0, 1, 3, 2)
        return (t[:, :, :, None, :] * eye[None, :, None, :, None]).reshape(8, 8 * S5_P, 8 * S5_K)

    bb = jnp.concatenate([in_map(bbar_re), in_map(bbar_im)], axis=2)
    cb = jnp.concatenate([out_map(c_re), -out_map(c_im)], axis=1)
    return bb, cb, abar_re.reshape(8, S5_W), abar_im.reshape(8, S5_W)


def _s5_tables(ar, ai, reverse):
    if reverse:
        ai = -ai
    pw = [(jnp.ones_like(ar), jnp.zeros_like(ar))]
    for _ in range(8):
        pw.append(_cmul(pw[-1][0], pw[-1][1], ar, ai))
    r = jnp.arange(8)[None, :, None]
    rows = []
    for d in (1, 2, 4):
        keep = (r <= 7 - d) if reverse else (r >= d)
        rows += [jnp.where(keep, pw[d][0][:, None, :], 0.0), jnp.where(keep, pw[d][1][:, None, :], 0.0)]
    order = [8 - k for k in range(8)] if reverse else [k + 1 for k in range(8)]
    rows += [jnp.stack([pw[n][0] for n in order], axis=1), jnp.stack([pw[n][1] for n in order], axis=1)]
    rows += [jnp.broadcast_to(pw[8][0][:, None, :], (8, 8, S5_W)), jnp.broadcast_to(pw[8][1][:, None, :], (8, 8, S5_W))]
    return jnp.stack(rows, axis=1)


def _s5_act_fn(pids, yc, a, dskip):
    return (jax.nn.gelu(yc + dskip * a),)


def _make_glu_res_fn(tm):
    def glu_res_fn(pids, zz, h):
        rows = _row_ids(pids[0], tm, (tm, 1))
        return (h + jnp.where(rows >= PAD, zz[:, :D] * jax.nn.sigmoid(zz[:, D:]), 0.0),)
    return glu_res_fn


def s5_mixer(h, g, w, rider=None):
    T = h.shape[0]
    tm = _row_tile(T)
    nt = T // tm
    a = norm_fwd("s5_norm", h, g, F32)
    ssm = [w[n] for n in ('lam_re', 'lam_im', 'log_dt', 'b_re', 'b_im', 'c_re', 'c_im')]
    (bb, cb, ar, ai), disc_vjp = jax.vjp(_s5_discretise, *ssm)
    yc, xs = s5_scan_fwd(a, bb, cb, _s5_tables(ar, ai, False), rider=rider)
    row = lambda arr: _rows(arr, tm)
    act_args = [row(yc), row(a), _const(w['dskip'])]
    act_outs = [Out((T, D), BF16, (tm, D), lambda i: (i, 0))]
    y = stage_fwd("s5_act", _s5_act_fn, (nt,), act_args, act_outs)[0]
    zz = matmul("s5_glu", y, w['w_glu'], 'nn')
    glu_fn = _make_glu_res_fn(tm)
    glu_args = [_rows(zz, tm, BF16), row(h)]
    glu_outs = [Out((T, D), F32, (tm, D), lambda i: (i, 0))]
    h_new = stage_fwd("s5_gate", glu_fn, (nt,), glu_args, glu_outs)[0]

    def bwd(dh, rider=None):
        dzz, dh_res = stage_bwd("s5_gate_b", glu_fn, (nt,), glu_args, glu_outs, [dh])
        dy, dwglu = linear_bwd("s5_glu_b", y, w['w_glu'], dzz)
        dyc, da1, ddskip = stage_bwd("s5_act_b", _s5_act_fn, (nt,), act_args, act_outs, [dy])
        da2, dbb, dcb, dab = s5_scan_bwd(a, bb, cb, _s5_tables(ar, ai, True), xs, dyc, rider=rider)
        dab = dab.sum(axis=1)
        dssm = disc_vjp((dbb, dcb, dab[:, :S5_W], dab[:, S5_W:]))
        dh2, dg = _s5_norm_bwd(h, g, da1, da2, dh_res, tm)
        grads = dict(zip(('lam_re', 'lam_im', 'log_dt', 'b_re', 'b_im', 'c_re', 'c_im'), dssm))
        grads.update(g=dg, dskip=ddskip, w_glu=dwglu)
        return dh2, grads

    return h_new, bwd


def _norm3_bwd_fn(pids, h, g):
    a = _rms(h, g)
    return a, a, h


def _s5_norm_bwd(h, g, da1, da2, dh, tm):
    T = h.shape[0]
    o = Out((T, D), F32, (tm, D), lambda i: (i, 0))
    return stage_bwd("s5_norm_b", _norm3_bwd_fn, (T // tm,), [_rows(h, tm), _const(g)], [o, o, o], [da1, da2, dh])


RET_R = 384


def _rope256(x, cos, sin):
    x1, x2 = x[:, :128], x[:, 128:]
    return jnp.concatenate([x1 * cos - x2 * sin, x1 * sin + x2 * cos], axis=1)


def _ret_fn(pids, z, gn, cos, sin, dmat, qdec, kdec, cdec, st):
    R = z.shape[0]
    q = _rope256(z[:, :256], cos, sin)
    k = _rope256(z[:, 256:512], cos, sin) * (RET_DK ** -0.5)
    v, gate = z[:, 512:1024], z[:, 1024:]
    outs = []
    for cc in range(R // CHUNK):
        lo = cc * CHUNK
        qc, kc, vc = q[lo:lo + CHUNK], k[lo:lo + CHUNK], v[lo:lo + CHUNK]
        outs.append(mm_nn(mm_nt(qc, kc) * dmat, vc) + mm_nn(qc * qdec, st))
        st = st * cdec + mm_tn(kc * kdec, vc)
    o = jnp.concatenate(outs, axis=0)
    mu = jnp.mean(o, axis=-1, keepdims=True)
    var = jnp.mean(jnp.square(o - mu), axis=-1, keepdims=True)
    o = (o - mu) * lax.rsqrt(var + EPS)
    return o * gn * _silu(gate), st


def ret_mixer(h, g, w, tabs, rider=None):
    T = h.shape[0]
    a = norm_fwd("ret_norm", h, g, BF16)
    z = matmul("ret_in", a, w['w_in'], 'nn')
    grid = (RET_H, T // RET_R)
    hw = RET_DK * 2 + RET_DV * 2
    args = [Arg(z, (RET_R, hw), lambda hh, j: (j, hh), gdtype=BF16), Arg(w['gn'], (1, RET_DV), lambda hh, j: (0, hh)),
            Arg(tabs['cos_d'], (RET_R, 128), lambda hh, j: (j, 0), False),
            Arg(tabs['sin_d'], (RET_R, 128), lambda hh, j: (j, 0), False),
            Arg(tabs['ret_dmat'], (None, CHUNK, CHUNK), lambda hh, j: (hh, 0, 0), False),
            Arg(tabs['ret_qdec'], (None, CHUNK, 1), lambda hh, j: (hh, 0, 0), False),
            Arg(tabs['ret_kdec'], (None, CHUNK, 1), lambda hh, j: (hh, 0, 0), False),
            Arg(tabs['ret_cdec'], (None, 1, 1), lambda hh, j: (hh, 0, 0), False)]
    outs = [Out((T, RET_H * RET_DV), BF16, (RET_R, RET_DV), lambda hh, j: (j, hh))]
    o, states = stage_fwd("ret_chunks", _ret_fn, grid, args, outs, state_shape=(RET_DK, RET_DV), rider=rider)
    h_new = matmul("ret_o", o, w['w_o'], 'nn', res=h)

    def bwd(dh, rider=None):
        do, dwo = linear_bwd("ret_o_b", o, w['w_o'], dh, mask=True)
        dz, dgn = stage_bwd("ret_chunks_b", _ret_fn, grid, args, outs, [do], state_shape=(RET_DK, RET_DV),
                            states=states, rider=rider)
        da, dwin = linear_bwd("ret_in_b", a, w['w_in'], dz)
        dh2, dg = norm_bwd("ret_norm_b", h, g, da, dh)
        return dh2, dict(g=dg, w_in=dwin, gn=dgn, w_o=dwo)

    return h_new, bwd


def loss_head(h, tgt):
    T = h.shape[0]
    tm = _row_tile(T)

    def body(h_ref, t_ref, loss_ref, dh_ref):
        i = pl.program_id(0)
        rows = _row_ids(i, tm, (tm, 1))
        err = jnp.where(rows >= LEAD, h_ref[...] - t_ref[...], 0.0)
        dh_ref[...] = err * (1.0 / D)
        part = jnp.full((8, 128), 0.5 * jnp.sum(jnp.sum(err * err, axis=1, keepdims=True) * (1.0 / D)), F32)

        @pl.when(i == 0)
        def _():
            loss_ref[...] = part

        @pl.when(i != 0)
        def _():
            loss_ref[...] += part

    loss, dh = pl.pallas_call(
        body, grid=(T // tm,),
        in_specs=[pl.BlockSpec((tm, D), lambda i: (i, 0)), pl.BlockSpec((tm, D), lambda i: (i, 0))],
        out_specs=[pl.BlockSpec((8, 128), lambda i: (0, 0)), pl.BlockSpec((tm, D), lambda i: (i, 0))],
        out_shape=[jax.ShapeDtypeStruct((8, 128), F32), jax.ShapeDtypeStruct((T, D), F32)], name="loss_head",
        compiler_params=_cparams())(h, tgt)
    return loss[0, 0], dh


def _tables(T):
    pos = jnp.maximum(jnp.arange(T, dtype=jnp.int32) - PAD, 0).astype(F32)

    def cs(dim):
        inv_freq = 1.0 / (10000.0 ** (jnp.arange(0, dim, 2, dtype=F32) / dim))
        ang = pos[:, None] * inv_freq[None, :]
        return jnp.cos(ang), jnp.sin(ang)

    ca, sa = cs(MLA_ROPE)
    zeros = jnp.zeros((T, 64), F32)
    cd, sd = cs(RET_DK)
    log_gamma = jnp.log(1.0 - jnp.exp2(-5.0 - jnp.arange(RET_H, dtype=F32)))
    p = jnp.arange(CHUNK, dtype=F32)
    diff = p[:, None] - p[None, :]
    dmat = jnp.where(diff >= 0, jnp.exp(diff[None] * log_gamma[:, None, None]), 0.0)
    return dict(cos_a=jnp.concatenate([ca, ca, zeros], axis=1), sin_a=jnp.concatenate([-sa, sa, zeros], axis=1),
                cos_d=cd, sin_d=sd, ret_dmat=dmat,
                ret_qdec=jnp.exp((p[None, :] + 1.0) * log_gamma[:, None])[..., None],
                ret_kdec=jnp.exp((CHUNK - 1.0 - p[None, :]) * log_gamma[:, None])[..., None],
                ret_cdec=jnp.exp(CHUNK * log_gamma)[:, None, None])


def _hgrn_lower_bound(logits):
    lb_cum = jnp.cumsum(jax.nn.softmax(logits, axis=0), axis=0)
    return (lb_cum - lb_cum[0:1])[1:2]


def _uq_to_heads(w):
    t = w.reshape(w.shape[0], MLA_H, MLA_QK)
    return jnp.pad(t, ((0, 0), (0, 0), (0, 256 - MLA_QK))).reshape(w.shape[0], MLA_H * 256)


def _uq_from_heads(g):
    return g.reshape(g.shape[0], MLA_H, 256)[:, :, :MLA_QK].reshape(g.shape[0], MLA_H * MLA_QK)


def _head_interleave(w, widths, heads):
    parts, lo = [], 0
    for wd in widths:
        parts.append(w[:, lo:lo + heads * wd].reshape(w.shape[0], heads, wd))
        lo += heads * wd
    return jnp.concatenate(parts, axis=2).reshape(w.shape[0], -1)


def _head_deinterleave(g, widths, heads):
    t = g.reshape(g.shape[0], heads, sum(widths))
    parts, lo = [], 0
    for wd in widths:
        parts.append(t[:, :, lo:lo + wd].reshape(g.shape[0], heads * wd))
        lo += wd
    return jnp.concatenate(parts, axis=1)


HG_WIDTHS = (128, 128, 128, 128)
RET_WIDTHS = (RET_DK, RET_DK, RET_DV, RET_DV)


def _split_head_gain(g):
    return g[:, :128], jnp.pad(g[:, 128:], ((0, 0), (0, 64)))


def _join_head_gain(dn, dr):
    return jnp.concatenate([dn, dr[:, :64]], axis=1)


def local_step(x, target, W, ex):
    S = x.shape[0]
    T = S + LEAD
    tabs = _tables(T)
    h = jnp.concatenate([jnp.zeros((PAD, D), F32), W['meta_tokens'], x], axis=0)
    tgt = jnp.concatenate([jnp.zeros((LEAD, D), F32), target], axis=0)

    gqn, gqr = _split_head_gain(W['mla_q_head_g'])
    gkn, gkr = _split_head_gain(W['mla_k_head_g'])
    lb, lb_vjp = jax.vjp(_hgrn_lower_bound, W['hgrn_lb_logits'])

    def ffn(i, hh):
        return ffn_layer(i, hh, W['norm_ffn_g'][i:i + 1], ex.weight('ffn_w_up', i), W['ffn_conv_w'][i],
                         W['ffn_conv_b'][i:i + 1], ex.weight('ffn_w_down', i))

    bm, bf = [None] * 4, [None] * 4
    ex.gather(['mla'], name="gather_mla")
    w0 = dict(w_down=jnp.pad(ex.weight('mla_w_down'), ((0, 0), (0, 64))), gcq=W['mla_cq_norm_g'],
              gckv=W['mla_ckv_norm_g'], w_uq=_uq_to_heads(ex.weight('mla_w_uq')), w_ukv=ex.weight('mla_w_ukv'),
              gqn=gqn, gqr=gqr, gkn=gkn, gkr=gkr, w_o=ex.weight('mla_w_o'))
    h, bm[0] = mla_mixer(h, W['norm_mix_g'][0:1], w0, tabs, rider=ex.gather(['ffn0', 'hgrn', 'ffn1']))
    h, bf[0] = ffn(0, h)
    w1 = dict(w_in=_head_interleave(ex.weight('hgrn_w_in'), HG_WIDTHS, HG_H), lb=lb, go=W['hgrn_o_norm_g'],
              w_o=ex.weight('hgrn_w_o'))
    h, bm[1] = hgrn_mixer(h, W['norm_mix_g'][1:2], w1, rider=ex.gather(['s5', 'ffn2']))
    h, bf[1] = ffn(1, h)
    w2 = dict(lam_re=W['s5_lam_re'][0], lam_im=W['s5_lam_im'][0], log_dt=W['s5_log_dt'][0], b_re=W['s5_b_re'][0],
              b_im=W['s5_b_im'][0], c_re=W['s5_c_re'][0], c_im=W['s5_c_im'][0], dskip=W['s5_d'],
              w_glu=ex.weight('s5_w_glu'))
    h, bm[2] = s5_mixer(h, W['norm_mix_g'][2:3], w2, rider=ex.gather(['ret']))
    h, bf[2] = ffn(2, h)
    w3 = dict(w_in=_head_interleave(ex.weight('ret_w_in'), RET_WIDTHS, RET_H), gn=W['ret_gn_g'],
              w_o=ex.weight('ret_w_o'))
    h, bm[3] = ret_mixer(h, W['norm_mix_g'][3:4], w3, tabs, rider=ex.gather(['ffn3']))
    h, bf[3] = ffn(3, h)

    loss, dh = loss_head(h, tgt)

    def ffn_grads(i, g):
        return {('ffn_w_up', i): g['w_up'], ('ffn_w_down', i): g['w_down']}

    gm, gf = [None] * 4, [None] * 4
    dh, gf[3] = bf[3](dh)
    dh, gm[3] = bm[3](dh, rider=ex.scatter(ffn_grads(3, gf[3])))
    dh, gf[2] = bf[2](dh)
    ret_grads = {('ret_w_in', 0): _head_deinterleave(gm[3]['w_in'], RET_WIDTHS, RET_H), ('ret_w_o', 0): gm[3]['w_o']}
    dh, gm[2] = bm[2](dh, rider=ex.scatter(ffn_grads(2, gf[2])))
    dh, gf[1] = bf[1](dh)
    dh, gm[1] = bm[1](dh, rider=ex.scatter(ffn_grads(1, gf[1])))
    dh, gf[0] = bf[0](dh)
    hgrn_grads = {('hgrn_w_in', 0): _head_deinterleave(gm[1]['w_in'], HG_WIDTHS, HG_H), ('hgrn_w_o', 0): gm[1]['w_o']}
    G = {}
    G['norm_ffn_g'] = jnp.concatenate([gf[i]['g'] for i in range(4)], axis=0)
    G['hgrn_lb_logits'] = lb_vjp(gm[1]['lb'])[0]
    G['hgrn_o_norm_g'] = gm[1]['go']
    for n in ('lam_re', 'lam_im', 'log_dt', 'b_re', 'b_im', 'c_re', 'c_im'):
        G['s5_' + n] = gm[2][n][None]
    G['s5_d'] = gm[2]['dskip']
    G['ret_gn_g'] = gm[3]['gn']
    G['ffn_conv_w'] = jnp.stack([gf[i]['cw'] for i in range(4)])
    G['ffn_conv_b'] = jnp.concatenate([gf[i]['cb'] for i in range(4)], axis=0)
    early = ex.all_devices(_pack([G[n] for n in SMALL_EARLY], F32, 8))

    dh, gm[0] = bm[0](
        dh,
        rider_dkv=lambda dwo: [ex.scatter({('s5_w_glu', 0): gm[2]['w_glu'], **hgrn_grads, **ffn_grads(0, gf[0]),
                                           ('mla_w_o', 0): dwo}), ex.swap()],
        rider_dq=lambda: [early, ex.scatter(ret_grads)], rider_heads=ex.swap)
    a = gm[0]
    mla_grads = {('mla_w_down', 0): a['w_down'][:, :MLA_QL + MLA_KVL + MLA_ROPE], ('mla_w_uq', 0): _uq_from_heads(a['w_uq']),
                 ('mla_w_ukv', 0): a['w_ukv']}
    G['meta_tokens'] = dh[PAD:LEAD]
    G['norm_mix_g'] = jnp.concatenate([gm[i]['g'] for i in range(4)], axis=0)
    G['mla_cq_norm_g'], G['mla_ckv_norm_g'] = a['gcq'], a['gckv']
    G['mla_q_head_g'] = _join_head_gain(a['gqn'], a['gqr'])
    G['mla_k_head_g'] = _join_head_gain(a['gkn'], a['gkr'])
    ex.tail = [ex.scatter(mla_grads), ex.all_devices(_pack([G[n] for n in SMALL_LATE], F32, 8))]
    return loss, dh[LEAD:], G


PACK_W = 1024
ANY = pl.BlockSpec(memory_space=pl.ANY)


def _pack(arrs, dtype, row_mult):
    flat = jnp.concatenate([a.reshape(-1).astype(dtype) for a in arrs])
    n = flat.shape[0]
    rows = -(-n // (PACK_W * row_mult)) * row_mult
    return jnp.pad(flat, (0, rows * PACK_W - n)).reshape(rows, PACK_W)


def _unpack(buf, shapes):
    flat = buf.reshape(-1)
    out, off = [], 0
    for s in shapes:
        n = math.prod(s)
        out.append(flat[off:off + n].reshape(s))
        off += n
    return out


def _my_pos():
    return lax.axis_index("x"), lax.axis_index("y"), lax.axis_index("c")


def _other_chips(x, y):
    return [(1 - x, y), (x, 1 - y), (1 - x, 1 - y)]


def gather_chips(name, src):
    def body(src_ref, out_ref, send_sems, recv_sems, local_sem):
        x, y, c = _my_pos()
        q = 2 * x + y
        mine = pltpu.make_async_copy(src_ref, out_ref.at[q], local_sem)
        mine.start()
        peers = _other_chips(x, y)

        def copy(k, slot, peer):
            return pltpu.make_async_remote_copy(src_ref=src_ref, dst_ref=out_ref.at[slot], send_sem=send_sems.at[k],
                                                recv_sem=recv_sems.at[k], device_id=(peer[0], peer[1], c),
                                                device_id_type=MESH_ID)
        sends = [copy(k, q, p) for k, p in enumerate(peers)]
        for cp in sends:
            cp.start()
        for k, p in enumerate(peers):
            copy(k, 2 * p[0] + p[1], p).wait_recv()
        for cp in sends:
            cp.wait_send()
        mine.wait()

    return pl.pallas_call(body, out_shape=jax.ShapeDtypeStruct((4,) + src.shape, src.dtype), in_specs=[ANY],
                          out_specs=ANY, name=name,
                          scratch_shapes=[pltpu.SemaphoreType.DMA((3,)), pltpu.SemaphoreType.DMA((3,)),
                                          pltpu.SemaphoreType.DMA(())])(src)


def _pack_tile(rows):
    return _divisors(rows, (256, 128, 64, 32, 16, 8))[0] if rows > 512 else rows


def sum_slots(name, slots):
    n, rows, w = slots.shape
    tr = _pack_tile(rows)

    def body(s_ref, o_ref):
        acc = s_ref[0].astype(F32)
        for k in range(1, n):
            acc = acc + s_ref[k].astype(F32)
        o_ref[...] = acc

    return pl.pallas_call(body, grid=(rows // tr,), in_specs=[pl.BlockSpec((n, tr, w), lambda i: (0, i, 0))],
                          out_specs=pl.BlockSpec((tr, w), lambda i: (i, 0)),
                          out_shape=jax.ShapeDtypeStruct((rows, w), F32), name=name, compiler_params=_cparams())(slots)


def adamw(name, grads, w, m, v):
    rows, wd = w.shape
    tr = _pack_tile(rows)
    ng = len(grads)

    def body(*refs):
        g = refs[0][...]
        for r in refs[1:ng]:
            g = g + r[...]
        w_ref, m_ref, v_ref = refs[ng:ng + 3]
        g_out, d_out, m_out, v_out = refs[ng + 3:]
        m_new = ADAM_B1 * m_ref[...] + (1.0 - ADAM_B1) * g
        v_new = ADAM_B2 * v_ref[...] + (1.0 - ADAM_B2) * jnp.square(g)
        m_hat = m_new / (1.0 - ADAM_B1 ** ADAM_STEP)
        v_hat = v_new / (1.0 - ADAM_B2 ** ADAM_STEP)
        g_out[...] = g
        d_out[...] = -ADAM_LR * (m_hat / (jnp.sqrt(v_hat) + ADAM_EPS) + ADAM_WD * w_ref[...])
        m_out[...] = m_new
        v_out[...] = v_new

    spec = pl.BlockSpec((tr, wd), lambda i: (i, 0))
    shape = jax.ShapeDtypeStruct((rows, wd), F32)
    return pl.pallas_call(body, grid=(rows // tr,), in_specs=[spec] * (ng + 3), out_specs=[spec] * 4,
                          out_shape=[shape] * 4, name=name, compiler_params=_cparams())(*grads, w, m, v)


def _sem_scratch(nw):
    return [pltpu.SemaphoreType.DMA((3 * nw,)), pltpu.SemaphoreType.DMA((3 * nw,)), pltpu.SemaphoreType.DMA((nw,))]


def _block2d(ref, axis, p, n):
    if axis == 0:
        return ref.at[pl.ds(pl.multiple_of(p * n, 16), n), :]
    return ref.at[:, pl.ds(pl.multiple_of(p * n, 128), n)]


class ScatterRider:
    def __init__(self, items):
        self.items = items
        self.operands = [it[0] for it in items]
        self.results = None
        self.out_shapes = [jax.ShapeDtypeStruct((4, arr.shape[0] // 4, arr.shape[1]) if axis == 0 else
                                                (4, arr.shape[0], arr.shape[1] // 4), arr.dtype) for arr, axis in items]
        self.scratch = _sem_scratch(len(items))

    def _copies(self, ins, outs, sems):
        send_sems, recv_sems, local_sems = sems
        x, y, c = _my_pos()
        q = 2 * x + y
        local, sends, lands = [], [], []
        for w, (arr, axis) in enumerate(self.items):
            n = arr.shape[axis] // 4
            local.append(pltpu.make_async_copy(_block2d(ins[w], axis, q, n), outs[w].at[q], local_sems.at[w]))
            for k, (px, py) in enumerate(_other_chips(x, y)):
                p = 2 * px + py
                sems_k = dict(send_sem=send_sems.at[3 * w + k], recv_sem=recv_sems.at[3 * w + k],
                              device_id=(px, py, c), device_id_type=MESH_ID)
                theirs = _block2d(ins[w], axis, p, n)
                sends.append(pltpu.make_async_remote_copy(src_ref=theirs, dst_ref=outs[w].at[q], **sems_k))
                lands.append(pltpu.make_async_remote_copy(src_ref=theirs, dst_ref=outs[w].at[p], **sems_k))
        return local, sends, lands

    def start(self, ins, outs, sems):
        local, sends, _ = self._copies(ins, outs, sems)
        for cp in local + sends:
            cp.start()

    def finish(self, ins, outs, sems):
        local, sends, lands = self._copies(ins, outs, sems)
        for cp in lands:
            cp.wait_recv()
        for cp in sends:
            cp.wait_send()
        for cp in local:
            cp.wait()


class GatherRider:
    def __init__(self, items):
        self.items = items
        self.operands = [it[0] for it in items]
        self.results = None
        self.out_shapes = []
        for arr, _, axis in items:
            r, c = arr.shape[1:]
            assert r % 32 == 0
            self.out_shapes.append(jax.ShapeDtypeStruct((4 * r, c) if axis == 0 else (r, 4 * c), arr.dtype))
        n = len(items)
        dma = pltpu.SemaphoreType.DMA
        self.scratch = [dma((3 * n,)), dma((3 * n,)), dma((n,)), dma((3 * n,)), dma((3 * n,))]

    def _copies(self, ins, outs, sems):
        send_sems, recv_sems, local_sems, pass_send_sems, pass_recv_sems = sems
        x, y, c = _my_pos()
        q = 2 * x + y
        local, sends, lands, passes, pass_lands = [], [], [], [], []
        for w, (arr, layer, axis) in enumerate(self.items):
            r, cols = arr.shape[1:]
            half = r // 2
            src = ins[w].at[layer]

            def part(blk, hc, w=w, axis=axis, r=r, cols=cols, half=half):
                if axis == 0:
                    return outs[w].at[pl.ds(pl.multiple_of(blk * r + hc * half, 16), half), :]
                return outs[w].at[pl.ds(pl.multiple_of(hc * half, 16), half), pl.ds(pl.multiple_of(blk * cols, 128), cols)]

            local.append(pltpu.make_async_copy(src, _block2d(outs[w], axis, q, arr.shape[1 + axis]), local_sems.at[w]))
            for k, (px, py) in enumerate(_other_chips(x, y)):
                p = 2 * px + py
                ici = dict(send_sem=send_sems.at[3 * w + k], recv_sem=recv_sems.at[3 * w + k],
                           device_id=(px, py, c), device_id_type=MESH_ID)
                d2d = dict(send_sem=pass_send_sems.at[3 * w + k], recv_sem=pass_recv_sems.at[3 * w + k],
                           device_id=(x, y, 1 - c), device_id_type=MESH_ID)
                mine = src.at[pl.ds(pl.multiple_of(c * half, 16), half), :]
                sends.append(pltpu.make_async_remote_copy(src_ref=mine, dst_ref=part(q, c), **ici))
                lands.append(pltpu.make_async_remote_copy(src_ref=mine, dst_ref=part(p, c), **ici))
                passes.append(pltpu.make_async_remote_copy(src_ref=part(p, c), dst_ref=part(p, c), **d2d))
                pass_lands.append(pltpu.make_async_remote_copy(src_ref=part(p, c), dst_ref=part(p, 1 - c), **d2d))
        return local, sends, lands, passes, pass_lands

    def start(self, ins, outs, sems):
        local, sends, _, _, _ = self._copies(ins, outs, sems)
        for cp in local + sends:
            cp.start()

    def middle(self, ins, outs, sems):
        _, _, lands, passes, _ = self._copies(ins, outs, sems)
        for land, cp in zip(lands, passes):
            land.wait_recv()
            cp.start()

    def finish(self, ins, outs, sems):
        local, sends, _, passes, pass_lands = self._copies(ins, outs, sems)
        for cp in pass_lands:
            cp.wait_recv()
        for cp in sends + passes:
            cp.wait_send()
        for cp in local:
            cp.wait()


class SwapRider:
    def __init__(self, arrs):
        self.operands = list(arrs)
        self.out_shapes = [jax.ShapeDtypeStruct(a.shape, a.dtype) for a in arrs]
        self.scratch = [pltpu.SemaphoreType.DMA((len(arrs),)), pltpu.SemaphoreType.DMA((len(arrs),))]
        self.results = None

    def _copies(self, ins, outs, sems):
        x, y, c = _my_pos()
        return [pltpu.make_async_remote_copy(src_ref=ins[w], dst_ref=outs[w], send_sem=sems[0].at[w],
                                             recv_sem=sems[1].at[w], device_id=(x, y, 1 - c), device_id_type=MESH_ID)
                for w in range(len(self.operands))]

    def start(self, ins, outs, sems):
        for cp in self._copies(ins, outs, sems):
            cp.start()

    def finish(self, ins, outs, sems):
        for cp in self._copies(ins, outs, sems):
            cp.wait()


class AllDevicesRider:
    def __init__(self, src):
        self.operands = [src]
        self.out_shapes = [jax.ShapeDtypeStruct((8,) + src.shape, src.dtype)]
        self.scratch = [pltpu.SemaphoreType.DMA((7,)), pltpu.SemaphoreType.DMA((7,)), pltpu.SemaphoreType.DMA(())]
        self.results = None

    def _copies(self, ins, outs, sems):
        x, y, c = _my_pos()
        me = 4 * x + 2 * y + c
        local = pltpu.make_async_copy(ins[0], outs[0].at[me], sems[2])
        sends, lands = [], []
        for k, m in enumerate(range(1, 8)):
            peer = ((1 - x) if m & 4 else x, (1 - y) if m & 2 else y, (1 - c) if m & 1 else c)
            sems_k = dict(send_sem=sems[0].at[k], recv_sem=sems[1].at[k], device_id=peer, device_id_type=MESH_ID)
            sends.append(pltpu.make_async_remote_copy(src_ref=ins[0], dst_ref=outs[0].at[me], **sems_k))
            lands.append(pltpu.make_async_remote_copy(src_ref=ins[0], dst_ref=outs[0].at[4 * peer[0] + 2 * peer[1] + peer[2]],
                                                      **sems_k))
        return local, sends, lands

    def start(self, ins, outs, sems):
        local, sends, _ = self._copies(ins, outs, sems)
        for cp in [local] + sends:
            cp.start()

    def finish(self, ins, outs, sems):
        local, sends, lands = self._copies(ins, outs, sems)
        for cp in lands:
            cp.wait_recv()
        for cp in sends:
            cp.wait_send()
        local.wait()


class RiderGroup:
    def __init__(self, riders):
        self.riders = riders
        self.operands = [a for r in riders for a in r.operands]
        self.out_shapes = [s for r in riders for s in r.out_shapes]
        self.scratch = [s for r in riders for s in r.scratch]

    def _split(self, ins, outs, sems):
        for r in self.riders:
            ni, no, ns = len(r.operands), len(r.out_shapes), len(r.scratch)
            yield r, ins[:ni], outs[:no], sems[:ns]
            ins, outs, sems = ins[ni:], outs[no:], sems[ns:]

    def start(self, ins, outs, sems):
        for r, i, o, s in self._split(ins, outs, sems):
            r.start(i, o, s)

    def middle(self, ins, outs, sems):
        for r, i, o, s in self._split(ins, outs, sems):
            if hasattr(r, 'middle'):
                r.middle(i, o, s)

    def finish(self, ins, outs, sems):
        for r, i, o, s in self._split(ins, outs, sems):
            r.finish(i, o, s)

    @property
    def results(self):
        return None

    @results.setter
    def results(self, res):
        for r in self.riders:
            no = len(r.out_shapes)
            r.results, res = list(res[:no]), res[no:]


def run_rider(name, rider):
    n_in, n_out = len(rider.operands), len(rider.out_shapes)

    def body(*refs):
        ins, outs, sems = refs[:n_in], refs[n_in:n_in + n_out], refs[n_in + n_out:]
        rider.start(ins, outs, sems)
        if hasattr(rider, 'middle'):
            rider.middle(ins, outs, sems)
        rider.finish(ins, outs, sems)

    rider.results = list(pl.pallas_call(body, out_shape=rider.out_shapes, in_specs=[ANY] * n_in, out_specs=[ANY] * n_out,
                                        name=name, scratch_shapes=rider.scratch)(*rider.operands))


WEIGHT_GROUPS = {'mla': [('mla_w_down', 0), ('mla_w_uq', 0), ('mla_w_ukv', 0), ('mla_w_o', 0)],
                 'hgrn': [('hgrn_w_in', 0), ('hgrn_w_o', 0)], 's5': [('s5_w_glu', 0)],
                 'ret': [('ret_w_in', 0), ('ret_w_o', 0)]}
WEIGHT_GROUPS.update({f'ffn{i}': [('ffn_w_up', i), ('ffn_w_down', i)] for i in range(4)})


class Exchange:
    def __init__(self, shards=None, full=None):
        self.shards, self.full = shards, dict(full or {})
        self.got, self.recv, self.sib, self.grads, self.small, self.tail = {}, {}, {}, {}, [], []

    def gather(self, groups, name=None):
        if self.shards is None:
            return None
        keys = [k for g in groups for k in WEIGHT_GROUPS[g]]
        rider = GatherRider([(self.shards[n], layer, SHARD_AXIS[n] - 1) for n, layer in keys])
        self.got.update({k: (rider, j) for j, k in enumerate(keys)})
        if name is not None:
            run_rider(name, rider)
        return rider

    def weight(self, n, layer=0):
        if self.shards is None:
            return self.full[n][layer]
        rider, j = self.got[(n, layer)]
        return rider.results[j]

    def scatter(self, grads, name=None):
        if self.shards is None:
            self.grads.update(grads)
            return None
        keys = list(grads)
        rider = ScatterRider([(grads[k], SHARD_AXIS[k[0]] - 1) for k in keys])
        self.recv.update({k: (rider, j) for j, k in enumerate(keys)})
        if name is not None:
            run_rider(name, rider)
        return rider

    def received(self, n, layer):
        rider, j = self.recv[(n, layer)]
        return rider.results[j]

    def swap(self, name=None):
        if self.shards is None:
            return None
        keys = [k for k, (r, _) in self.recv.items() if k not in self.sib and r.results is not None]
        rider = SwapRider([self.received(*k) for k in keys])
        self.sib.update({k: (rider, j) for j, k in enumerate(keys)})
        if name is not None:
            run_rider(name, rider)
        return rider

    def sibling(self, n, layer):
        rider, j = self.sib[(n, layer)]
        return rider.results[j]

    def all_devices(self, packed, name=None):
        if self.shards is None:
            return None
        rider = AllDevicesRider(packed)
        self.small.append(rider)
        if name is not None:
            run_rider(name, rider)
        return rider


ADAM_BLOCK_ELEMS = 256 * 1024


def adamw_shard(name, mine, sib, w, m, v, first_layer=0, into=None, rider=None):
    _, rows, cols = w.shape
    nl = mine.shape[1]
    tr = [t for t in (512, 384, 352, 256, 176, 128, 64, 32, 16) if rows % t == 0 and t * cols <= ADAM_BLOCK_ELEMS][0]

    def body(a_ref, b_ref, w_ref, m_ref, v_ref, *rest):
        g_out, d_out, m_out, v_out = rest[-4:]

        def total(r):
            acc = r[0].astype(F32)
            for k in range(1, 4):
                acc = acc + r[k].astype(F32)
            return acc
        g = total(a_ref) + total(b_ref)
        m_new = ADAM_B1 * m_ref[...] + (1.0 - ADAM_B1) * g
        v_new = ADAM_B2 * v_ref[...] + (1.0 - ADAM_B2) * jnp.square(g)
        m_hat = m_new / (1.0 - ADAM_B1 ** ADAM_STEP)
        v_hat = v_new / (1.0 - ADAM_B2 ** ADAM_STEP)
        g_out[...] = g
        d_out[...] = -ADAM_LR * (m_hat / (jnp.sqrt(v_hat) + ADAM_EPS) + ADAM_WD * w_ref[...])
        m_out[...] = m_new
        v_out[...] = v_new

    slots = pl.BlockSpec((4, None, tr, cols), lambda l, i: (0, l, i, 0))
    spec = pl.BlockSpec((None, tr, cols), lambda l, i: (l + first_layer, i, 0))
    shape = jax.ShapeDtypeStruct(w.shape, F32)
    in_specs, operands, aliases = [slots, slots, spec, spec, spec], [mine, sib, w, m, v], {}
    if into is not None:
        in_specs += [ANY] * 4
        operands += list(into)
        aliases = {5 + k: k for k in range(4)}
    return _pallas(name, body, (nl, rows // tr), in_specs, [spec] * 4, [shape] * 4, [], operands, rider, aliases)


def kernel(x, meta_tokens, norm_mix_g, norm_ffn_g, mla_w_down, mla_cq_norm_g, mla_ckv_norm_g, mla_w_uq, mla_w_ukv, mla_q_head_g, mla_k_head_g, mla_w_o, hgrn_w_in, hgrn_lb_logits, hgrn_o_norm_g, hgrn_w_o, s5_lam_re, s5_lam_im, s5_log_dt, s5_b_re, s5_b_im, s5_c_re, s5_c_im, s5_d, s5_w_glu, ret_w_in, ret_gn_g, ret_w_o, ffn_w_up, ffn_conv_w, ffn_conv_b, ffn_w_down, loss_target, m_meta_tokens, m_norm_mix_g, m_norm_ffn_g, m_mla_w_down, m_mla_cq_norm_g, m_mla_ckv_norm_g, m_mla_w_uq, m_mla_w_ukv, m_mla_q_head_g, m_mla_k_head_g, m_mla_w_o, m_hgrn_w_in, m_hgrn_lb_logits, m_hgrn_o_norm_g, m_hgrn_w_o, m_s5_lam_re, m_s5_lam_im, m_s5_log_dt, m_s5_b_re, m_s5_b_im, m_s5_c_re, m_s5_c_im, m_s5_d, m_s5_w_glu, m_ret_w_in, m_ret_gn_g, m_ret_w_o, m_ffn_w_up, m_ffn_conv_w, m_ffn_conv_b, m_ffn_w_down, v_meta_tokens, v_norm_mix_g, v_norm_ffn_g, v_mla_w_down, v_mla_cq_norm_g, v_mla_ckv_norm_g, v_mla_w_uq, v_mla_w_ukv, v_mla_q_head_g, v_mla_k_head_g, v_mla_w_o, v_hgrn_w_in, v_hgrn_lb_logits, v_hgrn_o_norm_g, v_hgrn_w_o, v_s5_lam_re, v_s5_lam_im, v_s5_log_dt, v_s5_b_re, v_s5_b_im, v_s5_c_re, v_s5_c_im, v_s5_d, v_s5_w_glu, v_ret_w_in, v_ret_gn_g, v_ret_w_o, v_ffn_w_up, v_ffn_conv_w, v_ffn_conv_b, v_ffn_w_down):
    vals = (x, meta_tokens, norm_mix_g, norm_ffn_g, mla_w_down, mla_cq_norm_g, mla_ckv_norm_g, mla_w_uq, mla_w_ukv, mla_q_head_g, mla_k_head_g, mla_w_o, hgrn_w_in, hgrn_lb_logits, hgrn_o_norm_g, hgrn_w_o, s5_lam_re, s5_lam_im, s5_log_dt, s5_b_re, s5_b_im, s5_c_re, s5_c_im, s5_d, s5_w_glu, ret_w_in, ret_gn_g, ret_w_o, ffn_w_up, ffn_conv_w, ffn_conv_b, ffn_w_down, loss_target, m_meta_tokens, m_norm_mix_g, m_norm_ffn_g, m_mla_w_down, m_mla_cq_norm_g, m_mla_ckv_norm_g, m_mla_w_uq, m_mla_w_ukv, m_mla_q_head_g, m_mla_k_head_g, m_mla_w_o, m_hgrn_w_in, m_hgrn_lb_logits, m_hgrn_o_norm_g, m_hgrn_w_o, m_s5_lam_re, m_s5_lam_im, m_s5_log_dt, m_s5_b_re, m_s5_b_im, m_s5_c_re, m_s5_c_im, m_s5_d, m_s5_w_glu, m_ret_w_in, m_ret_gn_g, m_ret_w_o, m_ffn_w_up, m_ffn_conv_w, m_ffn_conv_b, m_ffn_w_down, v_meta_tokens, v_norm_mix_g, v_norm_ffn_g, v_mla_w_down, v_mla_cq_norm_g, v_mla_ckv_norm_g, v_mla_w_uq, v_mla_w_ukv, v_mla_q_head_g, v_mla_k_head_g, v_mla_w_o, v_hgrn_w_in, v_hgrn_lb_logits, v_hgrn_o_norm_g, v_hgrn_w_o, v_s5_lam_re, v_s5_lam_im, v_s5_log_dt, v_s5_b_re, v_s5_b_im, v_s5_c_re, v_s5_c_im, v_s5_d, v_s5_w_glu, v_ret_w_in, v_ret_gn_g, v_ret_w_o, v_ffn_w_up, v_ffn_conv_w, v_ffn_conv_b, v_ffn_w_down)
    names = ['x'] + WEIGHTS + ['loss_target'] + ['m_' + n for n in WEIGHTS] + ['v_' + n for n in WEIGHTS]
    A = dict(zip(names, vals))
    q = 2 * lax.axis_index("x") + lax.axis_index("y")

    small_shapes = [A[n].shape for n in SMALL_SHARDED]
    got_small = gather_chips("gather_small", _pack([A[n] for n in SMALL_SHARDED], F32, 8))
    W = {n: A[n] for n in REPLICATED}
    parts_small = [_unpack(got_small[p], small_shapes) for p in range(4)]
    for k, n in enumerate(SMALL_SHARDED):
        W[n] = jnp.concatenate([parts_small[p][k] for p in range(4)], axis=SHARD_AXIS[n])

    ex = Exchange(shards={n: A[n].astype(BF16) for n in BIG})
    loss, grad_x, G = local_step(A['x'][0], A['loss_target'][0], W, ex)
    loss = lax.psum(loss, ("x", "y", "c"))

    run_rider("scatter_mla", RiderGroup(ex.tail))
    ex.swap(name="grad_big_sibling")
    res_big = []
    for n in BIG:
        res = None
        for layer in range(A[n].shape[0]):
            res = adamw_shard(f"adam_{n}_{layer}", ex.received(n, layer)[:, None], ex.sibling(n, layer)[:, None],
                              A[n], A['m_' + n], A['v_' + n], first_layer=layer, into=res)
        res_big.append(res)

    small_names = REPLICATED + SMALL_SHARDED
    gs = {}
    for part, names, rider in (("early", SMALL_EARLY, ex.small[0]), ("late", SMALL_LATE, ex.small[1])):
        total = sum_slots("grad_small_sum_" + part, rider.results[0])
        gs.update(zip(names, _unpack(total, [G[n].shape for n in names])))
    for n in SMALL_SHARDED:
        ax = SHARD_AXIS[n]
        size = gs[n].shape[ax] // 4
        gs[n] = lax.dynamic_slice_in_dim(gs[n], q * size, size, axis=ax)
    pk = lambda pre: _pack([A[pre + n] for n in small_names], F32, 8)
    own_shapes = [A[n].shape for n in small_names]
    res_small = [_unpack(r, own_shapes) for r in
                 adamw("adam_small", [_pack([gs[n] for n in small_names], F32, 8)], pk(''), pk('m_'), pk('v_'))]

    out = {}
    for j, kind in enumerate(('grad_', 'delta_', 'new_m_', 'new_v_')):
        for k, n in enumerate(BIG):
            out[kind + n] = res_big[k][j]
        for k, n in enumerate(small_names):
            out[kind + n] = res_small[j][k]
    return (loss, grad_x[None]) + tuple(out[kind + n] for kind in ('grad_', 'delta_', 'new_m_', 'new_v_')
                                        for n in WEIGHTS)
```

```python
import functools
import math

import jax
import jax.numpy as jnp
from jax import lax
from jax.experimental import pallas as pl
from jax.experimental.pallas import tpu as pltpu

F32, BF16 = jnp.float32, jnp.bfloat16
MESH_ID = pl.DeviceIdType.MESH

D = 1024
N_META = 16
PAD = 112
LEAD = PAD + N_META
EPS = 1e-6
NEG_INF = -1e30
CHUNK = 64
VMEM_LIMIT_V7X = 56 * 1024 * 1024
MM_VMEM_BUDGET = 36 * 1024 * 1024

MLA_H, MLA_NOPE, MLA_ROPE, MLA_V = 8, 128, 64, 128
MLA_QK = MLA_NOPE + MLA_ROPE
MLA_QL, MLA_KVL = 384, 256
HG_H, HG_D, HG_C = 8, 128, 16
S5_G, S5_P, S5_K = 64, 64, 16
RET_H, RET_DK, RET_DV = 4, 256, 512
FFN_F = 2816

ADAM_LR, ADAM_B1, ADAM_B2, ADAM_EPS, ADAM_WD, ADAM_STEP = 0.001, 0.9, 0.999, 1e-08, 0.01, 10

WEIGHTS = ['meta_tokens', 'norm_mix_g', 'norm_ffn_g', 'mla_w_down', 'mla_cq_norm_g', 'mla_ckv_norm_g', 'mla_w_uq',
           'mla_w_ukv', 'mla_q_head_g', 'mla_k_head_g', 'mla_w_o', 'hgrn_w_in', 'hgrn_lb_logits', 'hgrn_o_norm_g',
           'hgrn_w_o', 's5_lam_re', 's5_lam_im', 's5_log_dt', 's5_b_re', 's5_b_im', 's5_c_re', 's5_c_im', 's5_d',
           's5_w_glu', 'ret_w_in', 'ret_gn_g', 'ret_w_o', 'ffn_w_up', 'ffn_conv_w', 'ffn_conv_b', 'ffn_w_down']
SHARD_AXIS = {'meta_tokens': 1, 'mla_w_down': 1, 'mla_w_uq': 2, 'mla_w_ukv': 2, 'mla_w_o': 1, 'hgrn_w_in': 2,
              'hgrn_w_o': 1, 's5_d': 1, 's5_w_glu': 2, 'ret_w_in': 2, 'ret_gn_g': 1, 'ret_w_o': 1, 'ffn_w_up': 2,
              'ffn_conv_w': 2, 'ffn_w_down': 1}
BIG = ['mla_w_down', 'mla_w_uq', 'mla_w_ukv', 'mla_w_o', 'hgrn_w_in', 'hgrn_w_o', 's5_w_glu', 'ret_w_in', 'ret_w_o',
       'ffn_w_up', 'ffn_w_down']
SMALL_SHARDED = ['meta_tokens', 's5_d', 'ret_gn_g', 'ffn_conv_w']
REPLICATED = [n for n in WEIGHTS if n not in SHARD_AXIS]
SMALL_LATE = ['meta_tokens', 'norm_mix_g', 'mla_cq_norm_g', 'mla_ckv_norm_g', 'mla_q_head_g', 'mla_k_head_g']
SMALL_EARLY = [n for n in REPLICATED + SMALL_SHARDED if n not in SMALL_LATE]


def _cparams():
    return pltpu.CompilerParams(vmem_limit_bytes=VMEM_LIMIT_V7X)


def _dg(a, b, ca, cb):
    return lax.dot_general(a.astype(BF16), b.astype(BF16), (((ca,), (cb,)), ((), ())),
                           preferred_element_type=F32)


@jax.custom_vjp
def mm_nn(a, b):
    return _dg(a, b, 1, 0)


@jax.custom_vjp
def mm_nt(a, b):
    return _dg(a, b, 1, 1)


@jax.custom_vjp
def mm_tn(a, b):
    return _dg(a, b, 0, 0)


mm_nn.defvjp(lambda a, b: (mm_nn(a, b), (a, b)),
             lambda r, g: (mm_nt(g, r[1]).astype(r[0].dtype), mm_tn(r[0], g).astype(r[1].dtype)))
mm_nt.defvjp(lambda a, b: (mm_nt(a, b), (a, b)),
             lambda r, g: (mm_nn(g, r[1]).astype(r[0].dtype), mm_tn(g, r[0]).astype(r[1].dtype)))
mm_tn.defvjp(lambda a, b: (mm_tn(a, b), (a, b)),
             lambda r, g: (mm_nt(r[1], g).astype(r[0].dtype), mm_nn(r[0], g).astype(r[1].dtype)))


def _shift_rows(x, s, up):
    n = x.shape[0]
    r = lax.broadcasted_iota(jnp.int32, x.shape, 0)
    if up:
        return jnp.where(r < n - s, pltpu.roll(x, n - s, 0), 0.0)
    return jnp.where(r >= s, pltpu.roll(x, s, 0), 0.0)


@functools.partial(jax.custom_vjp, nondiff_argnums=(1,))
def shift_down(x, s):
    return _shift_rows(x, s, False)


shift_down.defvjp(lambda x, s: (_shift_rows(x, s, False), None), lambda s, _, g: (_shift_rows(g, s, True),))


@functools.partial(jax.custom_vjp, nondiff_argnums=(1,))
def shift_up(x, s):
    return _shift_rows(x, s, True)


shift_up.defvjp(lambda x, s: (_shift_rows(x, s, True), None), lambda s, _, g: (_shift_rows(g, s, False),))


def _swap32_impl(x):
    ax = x.ndim - 1
    lane = lax.broadcasted_iota(jnp.int32, x.shape, ax)
    return jnp.where(lane < 32, pltpu.roll(x, 96, ax), jnp.where(lane < 64, pltpu.roll(x, 32, ax), 0.0))


@jax.custom_vjp
def swap32(x):
    return _swap32_impl(x)


swap32.defvjp(lambda x: (_swap32_impl(x), None), lambda _, g: (_swap32_impl(g),))


def _rms(x, g):
    return x * lax.rsqrt(jnp.mean(x * x, axis=-1, keepdims=True) + EPS) * g


def _silu(x):
    return x * jax.nn.sigmoid(x)


def _row_ids(pid, n, shape, axis=0):
    return pid * n + lax.broadcasted_iota(jnp.int32, shape, axis)


class Arg:
    def __init__(self, arr, block, imap, diff=True, gdtype=F32):
        self.arr, self.block, self.imap, self.diff, self.gdtype = arr, block, imap, diff, gdtype


class Out:
    def __init__(self, shape, dtype, block, imap):
        self.shape, self.dtype, self.block, self.imap = shape, dtype, block, imap


def _free_axes(imap, grid):
    ng = len(grid)
    base = tuple(imap(*([0] * ng)))
    free = []
    for ax in range(ng):
        p = [0] * ng
        p[ax] = 1
        if grid[ax] > 1 and tuple(imap(*p)) == base:
            free.append(ax)
    walked = [ax for ax in range(ng) if grid[ax] > 1]
    assert free == walked[len(walked) - len(free):], "revisited blocks must be revisited on the innermost axes"
    return free


def _pallas(name, body, grid, in_specs, out_specs, out_shape, scratch, operands, rider=None, aliases=None):
    aliases = aliases or {}
    if isinstance(rider, (list, tuple)):
        riders = [r for r in rider if r is not None]
        rider = RiderGroup(riders) if riders else None
    if rider is None:
        return pl.pallas_call(body, grid=grid, in_specs=in_specs, out_specs=out_specs, out_shape=out_shape,
                              scratch_shapes=scratch, name=name, input_output_aliases=aliases,
                              compiler_params=_cparams())(*operands)
    n_in, n_out, n_sc = len(in_specs), len(out_specs), len(scratch)
    r_in, r_out = len(rider.operands), len(rider.out_shapes)

    def body_with_rider(*refs):
        ins, refs = refs[:n_in], refs[n_in:]
        r_ins, refs = refs[:r_in], refs[r_in:]
        outs, refs = refs[:n_out], refs[n_out:]
        r_outs, refs = refs[:r_out], refs[r_out:]
        sc, r_sc = refs[:n_sc], refs[n_sc:]
        pids = [pl.program_id(a) for a in range(len(grid))]
        first = functools.reduce(jnp.logical_and, [p == 0 for p in pids])
        last = functools.reduce(jnp.logical_and, [p == g - 1 for p, g in zip(pids, grid)])

        @pl.when(first)
        def _():
            rider.start(r_ins, r_outs, r_sc)

        if hasattr(rider, 'middle'):
            step = functools.reduce(lambda acc, pg: acc * pg[1] + pg[0], zip(pids, grid), 0)

            @pl.when(step == (math.prod(grid) * 5) // 6)
            def _():
                rider.middle(r_ins, r_outs, r_sc)

        body(*ins, *outs, *sc)

        @pl.when(last)
        def _():
            rider.finish(r_ins, r_outs, r_sc)

    res = pl.pallas_call(body_with_rider, grid=grid, in_specs=list(in_specs) + [ANY] * r_in,
                         out_specs=list(out_specs) + [ANY] * r_out, out_shape=list(out_shape) + rider.out_shapes,
                         scratch_shapes=list(scratch) + rider.scratch, name=name, input_output_aliases=aliases,
                         compiler_params=_cparams())(*operands, *rider.operands)
    rider.results = list(res[n_out:])
    return res[:n_out]


def stage_fwd(name, fn, grid, args, outs, state_shape=None, rider=None):
    n_in, n_out, ng = len(args), len(outs), len(grid)

    def body(*refs):
        pids = tuple(pl.program_id(a) for a in range(ng))
        vals = [r[...] for r in refs[:n_in]]
        o_refs = refs[n_in:n_in + n_out]
        if state_shape is None:
            res = fn(pids, *vals)
        else:
            sv_ref, st_ref = refs[n_in + n_out], refs[n_in + n_out + 1]

            @pl.when(pids[-1] == 0)
            def _():
                st_ref[...] = jnp.zeros(state_shape, F32)

            s = st_ref[...]
            sv_ref[...] = s
            res = fn(pids, *vals, s)
            st_ref[...] = res[-1]
            res = res[:-1]
        for r, v in zip(o_refs, res):
            r[...] = v.astype(r.dtype)

    in_specs = [pl.BlockSpec(a.block, a.imap) for a in args]
    out_specs = [pl.BlockSpec(o.block, o.imap) for o in outs]
    out_shape = [jax.ShapeDtypeStruct(o.shape, o.dtype) for o in outs]
    scratch = []
    if state_shape is not None:
        nz = len(state_shape)
        out_specs.append(pl.BlockSpec((None, None) + tuple(state_shape), lambda i, j: (i, j) + (0,) * nz))
        out_shape.append(jax.ShapeDtypeStruct(tuple(grid) + tuple(state_shape), F32))
        scratch = [pltpu.VMEM(state_shape, F32)]
    return _pallas(name, body, grid, in_specs, out_specs, out_shape, scratch, [a.arr for a in args], rider)


def stage_bwd(name, fn, grid, args, outs, cots, state_shape=None, states=None, rider=None):
    n_in, n_out, ng = len(args), len(outs), len(grid)
    nb = grid[-1]
    rev = state_shape is not None
    didx = [k for k, a in enumerate(args) if a.diff]
    frees = [_free_axes(args[k].imap, grid) for k in didx]

    def eff(p):
        return tuple(p[:-1]) + (nb - 1 - p[-1],) if rev else tuple(p)

    def wrap(imap):
        return lambda *p: imap(*eff(p))

    def body(*refs):
        pids = tuple(pl.program_id(a) for a in range(ng))
        e = eff(pids)
        vals = [r[...] for r in refs[:n_in]]
        cts = tuple(r[...].astype(F32) for r in refs[n_in:n_in + n_out])
        pos = n_in + n_out
        if rev:
            st_in_ref = refs[pos]
            pos += 1
        g_refs = refs[pos:pos + len(didx)]
        pos += len(didx)
        dvals = [vals[k].astype(F32) for k in didx]

        def f(*dv):
            full = list(vals)
            for k, v in zip(didx, dv[:len(didx)]):
                full[k] = v
            return tuple(fn(e, *full, *dv[len(didx):]))

        if rev:
            ds_ref = refs[pos]

            @pl.when(pids[-1] == 0)
            def _():
                ds_ref[...] = jnp.zeros(state_shape, F32)

            _, vjp = jax.vjp(f, *dvals, st_in_ref[...])
            grads = vjp(cts + (ds_ref[...],))
            ds_ref[...] = grads[-1]
            grads = grads[:-1]
        else:
            _, vjp = jax.vjp(f, *dvals)
            grads = vjp(cts)
        for gref, g, free in zip(g_refs, grads, frees):
            g = g.astype(F32)
            if not free:
                gref[...] = g.astype(gref.dtype)
            else:
                first = functools.reduce(jnp.logical_and, [pids[ax] == 0 for ax in free])

                @pl.when(first)
                def _():
                    gref[...] = g

                @pl.when(jnp.logical_not(first))
                def _():
                    gref[...] += g

    in_specs = [pl.BlockSpec(a.block, wrap(a.imap)) for a in args]
    in_specs += [pl.BlockSpec(o.block, wrap(o.imap)) for o in outs]
    operands = [a.arr for a in args] + list(cots)
    scratch = []
    if rev:
        nz = len(state_shape)
        in_specs.append(pl.BlockSpec((None, None) + tuple(state_shape), lambda i, j: (i, nb - 1 - j) + (0,) * nz))
        operands.append(states)
        scratch = [pltpu.VMEM(state_shape, F32)]
    out_specs = [pl.BlockSpec(args[k].block, wrap(args[k].imap)) for k in didx]
    assert all(args[k].gdtype == F32 or not free for k, free in zip(didx, frees))
    out_shape = [jax.ShapeDtypeStruct(args[k].arr.shape, args[k].gdtype) for k in didx]
    return _pallas(name, body, grid, in_specs, out_specs, out_shape, scratch, operands, rider)


def _divisors(n, cands):
    return [c for c in cands if n % c == 0] or [n]


def _nbytes(dt):
    return jnp.dtype(dt).itemsize


def matmul(name, a, b, mode, out_dtype=F32, res=None, mask=False, res_mask=True, window=None, into=None):
    sa, sb, so = _nbytes(a.dtype), _nbytes(b.dtype), _nbytes(out_dtype)
    off, width = (window[0], window[1]) if window is not None else (0, None)
    if mode in ('nn', 'nt'):
        M, K = a.shape
        N = (width or b.shape[1]) if mode == 'nn' else b.shape[0]
        assert mode == 'nn' or width is None or width == K
        best = None
        for tm in _divisors(M, (1408, 1056, 768, 384, 128)):
            for tn in _divisors(N, (1408, 1024, 768, 512, 384, 256, 128)):
                est = 2 * (tm * K * sa + tn * K * sb + tm * tn * (so + (4 if res is not None else 0)))
                if est <= MM_VMEM_BUDGET and (best is None or tm * tn > best[0] * best[1]):
                    best = (tm, tn)
        tm, tn = best
        grid = (M // tm, N // tn)

        def body(*refs):
            a_ref, b_ref = refs[0], refs[1]
            o_ref = refs[-1]
            x = a_ref[...]
            rows = _row_ids(pl.program_id(0), tm, (tm, 1))
            if mask:
                x = jnp.where(rows >= PAD, x, jnp.zeros_like(x))
            acc = _dg(x, b_ref[...], 1, 0 if mode == 'nn' else 1)
            if res is not None:
                acc = refs[2][...] + (jnp.where(rows >= PAD, acc, 0.0) if res_mask else acc)
            o_ref[...] = acc.astype(o_ref.dtype)

        assert off % (tn if mode == 'nn' else K) == 0
        cb, kb = off // tn, off // K
        in_specs = [pl.BlockSpec((tm, K), lambda i, j: (i, 0)),
                    pl.BlockSpec((K, tn), lambda i, j: (0, j + cb)) if mode == 'nn' else
                    pl.BlockSpec((tn, K), lambda i, j: (j, kb))]
        ops = [a, b]
        if res is not None:
            in_specs.append(pl.BlockSpec((tm, tn), lambda i, j: (i, j)))
            ops.append(res)
        return pl.pallas_call(body, grid=grid, in_specs=in_specs,
                              out_specs=pl.BlockSpec((tm, tn), lambda i, j: (i, j)),
                              out_shape=jax.ShapeDtypeStruct((M, N), out_dtype), name=name,
                              compiler_params=_cparams())(*ops)
    assert mode == 'tn' and res is None
    M, K = a.shape
    N = b.shape[1]
    best = None
    for tk in _divisors(K, (1408, 1024, 768, 512, 384, 256, 128)):
        for tn in _divisors(N, (1408, 1024, 768, 512, 384, 256, 128)):
            est = 2 * (M * tk * sa + M * tn * sb + tk * tn * so)
            if est <= MM_VMEM_BUDGET and (best is None or tk * tn > best[0] * best[1]):
                best = (tk, tn)
    tk, tn = best

    def body_t(*refs):
        a_ref, b_ref, o_ref = refs[0], refs[1], refs[-1]
        y = b_ref[...]
        if mask:
            rows = lax.broadcasted_iota(jnp.int32, (M, 1), 0)
            y = jnp.where(rows >= PAD, y, jnp.zeros_like(y))
        o_ref[...] = _dg(a_ref[...], y, 0, 0).astype(o_ref.dtype)

    assert off % tn == 0
    cb = off // tn
    total = window[2] if window is not None else N
    in_specs = [pl.BlockSpec((M, tk), lambda i, j: (0, i)), pl.BlockSpec((M, tn), lambda i, j: (0, j))]
    ops, alias = [a, b], {}
    if into is not None:
        in_specs.append(ANY)
        ops.append(into)
        alias = {2: 0}
    return pl.pallas_call(body_t, grid=(K // tk, N // tn), in_specs=in_specs,
                          out_specs=pl.BlockSpec((tk, tn), lambda i, j: (i, j + cb)),
                          out_shape=jax.ShapeDtypeStruct((K, total), out_dtype), name=name,
                          input_output_aliases=alias, compiler_params=_cparams())(*ops)


def matmul_nt_halves(name, a_lo, a_hi, b):
    M, K = a_lo.shape
    N = b.shape[0]
    tm = [t for t in (384, 128) if M % t == 0][0]

    def body(lo_ref, hi_ref, b_lo_ref, b_hi_ref, o_ref):
        o_ref[...] = _dg(lo_ref[...], b_lo_ref[...], 1, 1) + _dg(hi_ref[...], b_hi_ref[...], 1, 1)

    rows = pl.BlockSpec((tm, K), lambda i: (i, 0))
    return pl.pallas_call(body, grid=(M // tm,),
                          in_specs=[rows, rows, pl.BlockSpec((N, K), lambda i: (0, 0)), pl.BlockSpec((N, K), lambda i: (0, 1))],
                          out_specs=pl.BlockSpec((tm, N), lambda i: (i, 0)),
                          out_shape=jax.ShapeDtypeStruct((M, N), F32), name=name, compiler_params=_cparams())(a_lo, a_hi, b, b)


def linear_bwd(name, act, w, dy, mask=False, da_dtype=F32):
    return (matmul(name + "_da", dy, w, 'nt', out_dtype=da_dtype, mask=mask),
            matmul(name + "_dw", act, dy, 'tn', out_dtype=BF16, mask=mask))


def _row_tile(T):
    return _divisors(T, (384, 128))[0]


def _rows(arr, tm, gdtype=F32):
    return Arg(arr, (tm, arr.shape[1]), lambda i: (i, 0), gdtype=gdtype)


def _const(arr, diff=True):
    return Arg(arr, arr.shape, lambda *p: (0,) * arr.ndim, diff)


def _norm_fn(pids, h, g):
    return (_rms(h, g),)


def _norm_bwd_fn(pids, h, g):
    return (_rms(h, g), h)


def norm_fwd(name, h, g, dtype):
    T = h.shape[0]
    tm = _row_tile(T)
    return stage_fwd(name, _norm_fn, (T // tm,), [_rows(h, tm), _const(g)],
                     [Out((T, D), dtype, (tm, D), lambda i: (i, 0))])[0]


def norm_bwd(name, h, g, da, dh):
    T = h.shape[0]
    tm = _row_tile(T)
    o = Out((T, D), F32, (tm, D), lambda i: (i, 0))
    return stage_bwd(name, _norm_bwd_fn, (T // tm,), [_rows(h, tm), _const(g)], [o, o], [da, dh])


def _causal_conv3(u, cw, cb):
    return cw[2:3] * u + cw[1:2] * shift_down(u, 1) + cw[0:1] * shift_down(u, 2) + cb


def _ffn_act_fn(pids, ug, uv, cwg, cwv, cbg, cbv):
    return (_silu(_causal_conv3(ug, cwg, cbg)) * _causal_conv3(uv, cwv, cbv),)


def _ffn_act_args(ug, uv, cw, cb):
    T = ug.shape[0]
    col = lambda j: (0, j)
    args = [Arg(ug, (T, 128), col, gdtype=BF16), Arg(uv, (T, 128), col, gdtype=BF16),
            Arg(cw[:, :FFN_F], (3, 128), col), Arg(cw[:, FFN_F:], (3, 128), col),
            Arg(cb[:, :FFN_F], (1, 128), col), Arg(cb[:, FFN_F:], (1, 128), col)]
    outs = [Out((T, FFN_F), BF16, (T, 128), col)]
    return (FFN_F // 128,), args, outs


def ffn_layer(i, h, g, w_up, cw, cb, w_down):
    gate_w, val_w = (0, FFN_F, 2 * FFN_F), (FFN_F, FFN_F, 2 * FFN_F)
    b = norm_fwd(f"ffn{i}_norm", h, g, BF16)
    ug = matmul(f"ffn{i}_up_g", b, w_up, 'nn', out_dtype=BF16, window=gate_w)
    uv = matmul(f"ffn{i}_up_v", b, w_up, 'nn', out_dtype=BF16, window=val_w)
    grid, args, outs = _ffn_act_args(ug, uv, cw, cb)
    p = stage_fwd(f"ffn{i}_act", _ffn_act_fn, grid, args, outs)[0]
    h_new = matmul(f"ffn{i}_down", p, w_down, 'nn', res=h)

    def bwd(dh):
        dp, dwd = linear_bwd(f"ffn{i}_down_b", p, w_down, dh, mask=True, da_dtype=BF16)
        dug, duv, dcwg, dcwv, dcbg, dcbv = stage_bwd(f"ffn{i}_act_b", _ffn_act_fn, grid, args, outs, [dp])
        db = matmul_nt_halves(f"ffn{i}_up_b_da", dug, duv, w_up)
        dwu = matmul(f"ffn{i}_up_b_dw_g", b, dug, 'tn', out_dtype=BF16, window=gate_w)
        dwu = matmul(f"ffn{i}_up_b_dw_v", b, duv, 'tn', out_dtype=BF16, window=val_w, into=dwu)
        dh2, dg = norm_bwd(f"ffn{i}_norm_b", h, g, db, dh)
        return dh2, dict(g=dg, w_up=dwu, cw=jnp.concatenate([dcwg, dcwv], axis=1),
                         cb=jnp.concatenate([dcbg, dcbv], axis=1), w_down=dwd)

    return h_new, bwd


def _mla_latent_fn(pids, down, gcq, gckv):
    cq = _rms(down[:, :MLA_QL], gcq)
    ckv = _rms(down[:, MLA_QL:MLA_QL + MLA_KVL], gckv)
    return cq, ckv, down[:, MLA_QL + MLA_KVL:]


def _rope64(x, cos, sin_signed):
    return x * cos + swap32(x) * sin_signed


def _mla_heads_fn(pids, qraw, kv, kpe, gqn, gqr, gkn, gkr, cos, sin_signed):
    qn, qr = qraw[:, :128], qraw[:, 128:]
    rq = lax.rsqrt((jnp.sum(qn * qn, -1, keepdims=True) + jnp.sum(qr * qr, -1, keepdims=True)) / MLA_QK + EPS)
    q = jnp.concatenate([qn * rq * gqn, _rope64(qr * rq * gqr, cos, sin_signed)], axis=1)
    kn, v = kv[:, :128], kv[:, 128:]
    rk = lax.rsqrt((jnp.sum(kn * kn, -1, keepdims=True) + jnp.sum(kpe * kpe, -1, keepdims=True)) / MLA_QK + EPS)
    k = jnp.concatenate([kn * rk * gkn, _rope64(kpe * rk * gkr, cos, sin_signed)], axis=1)
    return q, k, v


def _chunk_id(r):
    return jnp.where(r < LEAD, 0, 1 + lax.shift_right_arithmetic(r - LEAD, 6))


ATT_T = 384
ATT_SCALE = MLA_QK ** -0.5


def _att_mask(s, qb, kb):
    qrow = _row_ids(qb, ATT_T, (ATT_T, 1))
    krow = _row_ids(kb, ATT_T, (1, ATT_T), axis=1)
    ok = jnp.logical_and(_chunk_id(krow) <= _chunk_id(qrow), krow >= PAD)
    return jnp.where(ok, s, NEG_INF)


def _att_scores(q, k_ref, qb, kb, masked):
    ks = k_ref[pl.ds(pl.multiple_of(kb * ATT_T, ATT_T), ATT_T), :]
    s = _dg(q, ks, 1, 1) * ATT_SCALE
    return (_att_mask(s, qb, kb) if masked else s), ks


def _att_key_loop(j, step, init):
    carry = step(0, init, True)
    n_mid = jnp.maximum(j - 1, 0)
    carry = lax.fori_loop(0, n_mid // 2, lambda i, c: step(2 * i + 2, step(2 * i + 1, c, False), False), carry)
    carry = lax.cond(n_mid % 2 == 1, lambda c: step(j - 1, c, False), lambda c: c, carry)
    return lax.cond(j > 0, lambda c: step(j, c, True), lambda c: c, carry)


def _att_rows(ref, b):
    return ref[pl.ds(pl.multiple_of(b * ATT_T, ATT_T), ATT_T), :]


def attention_fwd(q, k, v, rider=None):
    H, T, _ = q.shape
    nq = T // ATT_T

    def body(q_ref, k_ref, v_ref, o_ref, lse_ref):
        j = pl.program_id(1)
        qv = q_ref[...]

        def step(kb, carry, masked):
            m, l, acc = carry
            s, _ = _att_scores(qv, k_ref, j, kb, masked)
            m_new = jnp.maximum(m, jnp.max(s, axis=-1, keepdims=True))
            p = jnp.exp(s - m_new)
            alpha = jnp.exp(m - m_new)
            return (m_new, alpha * l + jnp.sum(p, axis=-1, keepdims=True),
                    alpha * acc + _dg(p, _att_rows(v_ref, kb), 1, 0))

        init = (jnp.full((ATT_T, 1), NEG_INF, F32), jnp.zeros((ATT_T, 1), F32), jnp.zeros((ATT_T, MLA_V), F32))
        m, l, acc = _att_key_loop(j, step, init)
        o_ref[...] = acc / l
        lse_ref[...] = m + jnp.log(l)

    return _pallas("mla_attn", body, (H, nq),
                   [pl.BlockSpec((None, ATT_T, 256), lambda hh, j: (hh, j, 0)),
                    pl.BlockSpec((None, T, 256), lambda hh, j: (hh, 0, 0)),
                    pl.BlockSpec((None, T, MLA_V), lambda hh, j: (hh, 0, 0))],
                   [pl.BlockSpec((ATT_T, MLA_V), lambda hh, j: (j, hh)),
                    pl.BlockSpec((None, ATT_T, 1), lambda hh, j: (hh, j, 0))],
                   [jax.ShapeDtypeStruct((T, H * MLA_V), F32), jax.ShapeDtypeStruct((H, T, 1), F32)], [],
                   [q, k, v], rider)


def attention_bwd(q, k, v, o, lse, do, rider=None, rider_dq=None):
    H, T, _ = q.shape
    nq = T // ATT_T

    def dq_body(q_ref, k_ref, v_ref, o_ref, do_ref, lse_ref, dq_ref, delta_ref):
        j = pl.program_id(1)
        qv, dov, lsev = q_ref[...], do_ref[...], lse_ref[...]
        delta = jnp.sum(dov * o_ref[...], axis=-1, keepdims=True)
        delta_ref[...] = delta

        def step(kb, acc, masked):
            s, ks = _att_scores(qv, k_ref, j, kb, masked)
            p = jnp.exp(s - lsev)
            ds = p * (_dg(dov, _att_rows(v_ref, kb), 1, 1) - delta) * ATT_SCALE
            return acc + _dg(ds, ks, 1, 0)

        dq_ref[...] = _att_key_loop(j, step, jnp.zeros((ATT_T, 256), F32))

    q_blk = pl.BlockSpec((None, ATT_T, 256), lambda hh, j: (hh, j, 0))
    k_all = pl.BlockSpec((None, T, 256), lambda hh, j: (hh, 0, 0))
    v_all = pl.BlockSpec((None, T, MLA_V), lambda hh, j: (hh, 0, 0))
    o_blk = pl.BlockSpec((ATT_T, MLA_V), lambda hh, j: (j, hh))
    col_blk = pl.BlockSpec((None, ATT_T, 1), lambda hh, j: (hh, j, 0))
    dq, delta = _pallas("mla_attn_dq", dq_body, (H, nq), [q_blk, k_all, v_all, o_blk, o_blk, col_blk],
                        [q_blk, col_blk],
                        [jax.ShapeDtypeStruct((H, T, 256), F32), jax.ShapeDtypeStruct((H, T, 1), F32)], [],
                        [q, k, v, o, do, lse], rider_dq)

    def dkv_body(q_ref, k_ref, v_ref, do_ref, lse_ref, delta_ref, dk_ref, dv_ref):
        kb = pl.program_id(1)
        kv = k_ref[...]
        vv = v_ref[...]

        def step(qb, carry, masked):
            dk, dv = carry
            qv, dov = _att_rows(q_ref, qb), _att_rows(do_ref, qb)
            s = _dg(qv, kv, 1, 1) * ATT_SCALE
            if masked:
                s = _att_mask(s, qb, kb)
            p = jnp.exp(s - _att_rows(lse_ref, qb))
            ds = p * (_dg(dov, vv, 1, 1) - _att_rows(delta_ref, qb)) * ATT_SCALE
            return dk + _dg(ds, qv, 0, 0), dv + _dg(p, dov, 0, 0)

        carry = step(kb, (jnp.zeros((ATT_T, 256), F32), jnp.zeros((ATT_T, MLA_V), F32)), True)

        def later_blocks(c, masked):
            n = nq - 1 - kb
            c = lax.fori_loop(0, n // 2,
                              lambda i, cc: step(kb + 2 + 2 * i, step(kb + 1 + 2 * i, cc, masked), masked), c)
            return lax.cond(n % 2 == 1, lambda cc: step(nq - 1, cc, masked), lambda cc: cc, c)

        dk, dv = lax.cond(kb == 0, lambda c: later_blocks(c, True), lambda c: later_blocks(c, False), carry)
        dk_ref[...] = dk
        dv_ref[...] = dv

    q_all = pl.BlockSpec((None, T, 256), lambda hh, j: (hh, 0, 0))
    v_blk = pl.BlockSpec((None, ATT_T, MLA_V), lambda hh, j: (hh, j, 0))
    do_all = pl.BlockSpec((T, MLA_V), lambda hh, j: (0, hh))
    col_all = pl.BlockSpec((None, T, 1), lambda hh, j: (hh, 0, 0))
    dk, dv = _pallas("mla_attn_dkv", dkv_body, (H, nq), [q_all, q_blk, v_blk, do_all, col_all, col_all],
                     [q_blk, v_blk],
                     [jax.ShapeDtypeStruct((H, T, 256), F32), jax.ShapeDtypeStruct((H, T, MLA_V), F32)], [],
                     [q, k, v, do, lse, delta], rider)
    return dq, dk, dv


def mla_mixer(h, g, w, tabs, rider=None):
    T = h.shape[0]
    tm = _row_tile(T)
    nt = T // tm
    a = norm_fwd("mla_norm", h, g, BF16)
    down = matmul("mla_down", a, w['w_down'], 'nn')
    lat_args = [_rows(down, tm, BF16), _const(w['gcq']), _const(w['gckv'])]
    lat_outs = [Out((T, MLA_QL), BF16, (tm, MLA_QL), lambda i: (i, 0)),
                Out((T, MLA_KVL), BF16, (tm, MLA_KVL), lambda i: (i, 0)),
                Out((T, 128), F32, (tm, 128), lambda i: (i, 0))]
    cq, ckv, kpe = stage_fwd("mla_latent", _mla_latent_fn, (nt,), lat_args, lat_outs)
    qraw = matmul("mla_uq", cq, w['w_uq'], 'nn')
    kv = matmul("mla_ukv", ckv, w['w_ukv'], 'nn')
    hd_args = [Arg(qraw, (tm, 256), lambda i, hh: (i, hh), gdtype=BF16),
               Arg(kv, (tm, 256), lambda i, hh: (i, hh), gdtype=BF16),
               Arg(kpe, (tm, 128), lambda i, hh: (i, 0)),
               _const(w['gqn']), _const(w['gqr']), _const(w['gkn']), _const(w['gkr']),
               Arg(tabs['cos_a'], (tm, 128), lambda i, hh: (i, 0), False),
               Arg(tabs['sin_a'], (tm, 128), lambda i, hh: (i, 0), False)]
    hd_outs = [Out((MLA_H, T, 256), BF16, (None, tm, 256), lambda i, hh: (hh, i, 0)),
               Out((MLA_H, T, 256), BF16, (None, tm, 256), lambda i, hh: (hh, i, 0)),
               Out((MLA_H, T, 128), BF16, (None, tm, 128), lambda i, hh: (hh, i, 0))]
    q, k, v = stage_fwd("mla_heads", _mla_heads_fn, (nt, MLA_H), hd_args, hd_outs)
    o, lse = attention_fwd(q, k, v, rider=rider)
    h_new = matmul("mla_o", o, w['w_o'], 'nn', res=h)

    def bwd(dh, rider_dkv=None, rider_dq=None, rider_heads=None):
        do, dwo = linear_bwd("mla_o_b", o, w['w_o'], dh, mask=True)
        dq, dk, dv = attention_bwd(q, k, v, o, lse, do, rider=rider_dkv and rider_dkv(dwo),
                                   rider_dq=rider_dq and rider_dq())
        dqraw, dkv, dkpe, dgqn, dgqr, dgkn, dgkr = stage_bwd("mla_heads_b", _mla_heads_fn, (nt, MLA_H), hd_args,
                                                             hd_outs, [dq, dk, dv],
                                                             rider=rider_heads and rider_heads())
        dcq, dwuq = linear_bwd("mla_uq_b", cq, w['w_uq'], dqraw)
        dckv, dwukv = linear_bwd("mla_ukv_b", ckv, w['w_ukv'], dkv)
        ddown, dgcq, dgckv = stage_bwd("mla_latent_b", _mla_latent_fn, (nt,), lat_args, lat_outs, [dcq, dckv, dkpe])
        da, dwdown = linear_bwd("mla_down_b", a, w['w_down'], ddown)
        dh2, dg = norm_bwd("mla_norm_b", h, g, da, dh)
        return dh2, dict(g=dg, w_down=dwdown, gcq=dgcq, gckv=dgckv, w_uq=dwuq, w_ukv=dwukv, gqn=dgqn, gqr=dgqr,
                         gkn=dgkn, gkr=dgkr, w_o=dwo)

    return h_new, bwd


HG_R = 384


def _hgrn_fn(pids, z, lb, go, st):
    outs = []
    for lo in range(0, z.shape[0], 128):
        o, st = _hgrn_block(z[lo:lo + 128], lb, go, st)
        outs.append(o)
    return jnp.concatenate(outs, axis=0), st


def _hgrn_block(z, lb, go, st):
    R = z.shape[0]
    zq, zf, zi, zg = z[:, :128], z[:, 128:256], z[:, 256:384], z[:, 384:]
    assert R == 128
    q = _silu(zq)
    fg = lb + (1.0 - lb) * jax.nn.sigmoid(zf)
    logf = jnp.log(fg)
    k = 1.0 - fg
    row = lax.broadcasted_iota(jnp.int32, logf.shape, 0)
    pos = row & (HG_C - 1)
    cum, rev = logf, logf
    for d in (1, 2, 4, 8):
        cum = cum + jnp.where(pos >= d, shift_down(cum, d), 0.0)
        rev = rev + jnp.where(pos < HG_C - d, shift_up(rev, d), 0.0)
    cums, tots = [cum], [cum + rev - logf]
    for s in (16, 32, 64):
        odd = (row & s) != 0
        before = shift_down(tots[-1], s)
        cums.append(cums[-1] + jnp.where(odd, before, 0.0))
        tots.append(tots[-1] + jnp.where(odd, before, shift_up(tots[-1], s)))
    t = lax.broadcasted_iota(jnp.int32, (R, R), 0)
    j = lax.broadcasted_iota(jnp.int32, (R, R), 1)
    sh = lax.shift_right_arithmetic
    a = jnp.where(jnp.logical_and(sh(t, 4) == sh(j, 4), j <= t), mm_nt(q * jnp.exp(cum), k * jnp.exp(-cum)), 0.0)
    for n, s in enumerate((16, 32, 64)):
        m = jnp.logical_and(sh(t, 5 + n) == sh(j, 5 + n), jnp.logical_and((t & s) != 0, (j & s) == 0))
        a = a + jnp.where(m, mm_nt(q * jnp.exp(cums[n]), k * jnp.exp(tots[n] - cums[n])), 0.0)
    o = mm_nn(a, zi) + mm_nt(q * jnp.exp(cums[3]), st)
    st = st * jnp.exp(tots[3][0:1, :]) + mm_tn(zi, k * jnp.exp(tots[3] - cums[3]))
    return _rms(o, go) * _silu(zg), st


def hgrn_mixer(h, g, w, rider=None):
    T = h.shape[0]
    a = norm_fwd("hgrn_norm", h, g, BF16)
    z = matmul("hgrn_in", a, w['w_in'], 'nn')
    grid = (HG_H, T // HG_R)
    args = [Arg(z, (HG_R, 512), lambda hh, j: (j, hh), gdtype=BF16), Arg(w['lb'], (1, 128), lambda hh, j: (0, hh)),
            _const(w['go'])]
    outs = [Out((T, D), BF16, (HG_R, 128), lambda hh, j: (j, hh))]
    o, states = stage_fwd("hgrn_gla", _hgrn_fn, grid, args, outs, state_shape=(HG_D, HG_D), rider=rider)
    h_new = matmul("hgrn_o", o, w['w_o'], 'nn', res=h)

    def bwd(dh, rider=None):
        do, dwo = linear_bwd("hgrn_o_b", o, w['w_o'], dh, mask=True)
        dz, dlb, dgo = stage_bwd("hgrn_gla_b", _hgrn_fn, grid, args, outs, [do], state_shape=(HG_D, HG_D),
                                 states=states, rider=rider)
        da, dwin = linear_bwd("hgrn_in_b", a, w['w_in'], dz)
        dh2, dg = norm_bwd("hgrn_norm_b", h, g, da, dh)
        return dh2, dict(g=dg, w_in=dwin, lb=dlb, go=dgo, w_o=dwo)

    return h_new, bwd


S5_R = 384
S5_W = 512
S5_SLABS = D // 128


def _cmul(ar, ai, br, bi):
    return ar * br - ai * bi, ar * bi + ai * br


def _s5_scan(br, bi, tab, cr, ci, reverse):
    R, W = br.shape
    G = R // 8
    xr, xi = br.reshape(G, 8, W), bi.reshape(G, 8, W)
    for n, d in enumerate((1, 2, 4)):
        sh = (8 - d) if reverse else d
        mr, mi = _cmul(tab[2 * n][None], tab[2 * n + 1][None], pltpu.roll(xr, sh, 1), pltpu.roll(xi, sh, 1))
        xr, xi = xr + mr, xi + mi
    pr, pi = tab[6], tab[7]
    edge = 0 if reverse else 7
    out_r, out_i = [None] * G, [None] * G
    for g in (range(G - 1, -1, -1) if reverse else range(G)):
        ar, ai = _cmul(pr, pi, cr, ci)
        gr, gi = xr[g] + ar, xi[g] + ai
        cr, ci = gr[edge:edge + 1], gi[edge:edge + 1]
        out_r[g], out_i[g] = gr, gi
    return jnp.concatenate(out_r, axis=0), jnp.concatenate(out_i, axis=0), cr, ci


def s5_scan_fwd(a, bb, cb, tab, rider=None):
    T = a.shape[0]
    nb = T // S5_R

    def body(a_ref, bb_ref, cb_ref, tab_ref, y_ref, xs_ref, c_ref):
        @pl.when(pl.program_id(1) == 0)
        def _():
            c_ref[...] = jnp.zeros(c_ref.shape, F32)

        bu = _dg(a_ref[...], bb_ref[...], 1, 0)
        t = tab_ref[...]
        xr, xi, cr, ci = _s5_scan(bu[:, :S5_W], bu[:, S5_W:], t, c_ref[0:1, :S5_W], c_ref[0:1, S5_W:], False)
        x = jnp.concatenate([xr, xi], axis=1)
        xs_ref[...] = x
        y_ref[...] = _dg(x, cb_ref[...], 1, 0)
        c_ref[0:1, :] = jnp.concatenate([cr, ci], axis=1)

    return _pallas(
        "s5_scan", body, (S5_SLABS, nb),
        [pl.BlockSpec((S5_R, 128), lambda j, i: (i, j)),
         pl.BlockSpec((None, 128, 2 * S5_W), lambda j, i: (j, 0, 0)),
         pl.BlockSpec((None, 2 * S5_W, 128), lambda j, i: (j, 0, 0)),
         pl.BlockSpec((None, 10, 8, S5_W), lambda j, i: (j, 0, 0, 0))],
        [pl.BlockSpec((S5_R, 128), lambda j, i: (i, j)),
         pl.BlockSpec((None, S5_R, 2 * S5_W), lambda j, i: (j, i, 0))],
        [jax.ShapeDtypeStruct((T, D), F32), jax.ShapeDtypeStruct((S5_SLABS, T, 2 * S5_W), F32)],
        [pltpu.VMEM((8, 2 * S5_W), F32)], [a, bb, cb, tab], rider)


def s5_scan_bwd(a, bb, cb, tab_rev, xs, dy, rider=None):
    T = a.shape[0]
    nb = T // S5_R
    rg = S5_R // 8

    def body(a_ref, dy_ref, xs_ref, xp_ref, bb_ref, cb_ref, tab_ref, da_ref, dbb_ref, dcb_ref, dab_ref, c_ref):
        i = pl.program_id(1)

        @pl.when(i == 0)
        def _():
            c_ref[...] = jnp.zeros(c_ref.shape, F32)

        dy_v = dy_ref[...]
        x = xs_ref[...]
        dxo = _dg(dy_v, cb_ref[...], 1, 1)
        gr, gi, cr, ci = _s5_scan(dxo[:, :S5_W], dxo[:, S5_W:], tab_ref[...], c_ref[0:1, :S5_W], c_ref[0:1, S5_W:], True)
        c_ref[0:1, :] = jnp.concatenate([cr, ci], axis=1)
        g = jnp.concatenate([gr, gi], axis=1)
        da_ref[...] = _dg(g, bb_ref[...], 1, 1)
        dbb = _dg(a_ref[...], g, 0, 0)
        dcb = _dg(x, dy_v, 0, 0)
        first_tile = i == nb - 1
        prev_last = jnp.where(first_tile, 0.0, xp_ref[7:8, :])
        rows = lax.broadcasted_iota(jnp.int32, x.shape, 0)
        xp = jnp.where(rows == 0, prev_last, pltpu.roll(x, 1, 0))
        xpr, xpi = xp[:, :S5_W], xp[:, S5_W:]
        dar = (gr * xpr + gi * xpi).reshape(rg, 8, S5_W).sum(axis=0)
        dai = (gi * xpr - gr * xpi).reshape(rg, 8, S5_W).sum(axis=0)
        dab = jnp.concatenate([dar, dai], axis=1)

        @pl.when(i == 0)
        def _():
            dbb_ref[...] = dbb
            dcb_ref[...] = dcb
            dab_ref[...] = dab

        @pl.when(i != 0)
        def _():
            dbb_ref[...] += dbb
            dcb_ref[...] += dcb
            dab_ref[...] += dab

    def prev_rows(j, i):
        return (j, jnp.maximum((nb - 1 - i) * rg - 1, 0), 0)

    return _pallas(
        "s5_scan_b", body, (S5_SLABS, nb),
        [pl.BlockSpec((S5_R, 128), lambda j, i: (nb - 1 - i, j)),
         pl.BlockSpec((S5_R, 128), lambda j, i: (nb - 1 - i, j)),
         pl.BlockSpec((None, S5_R, 2 * S5_W), lambda j, i: (j, nb - 1 - i, 0)),
         pl.BlockSpec((None, 8, 2 * S5_W), prev_rows),
         pl.BlockSpec((None, 128, 2 * S5_W), lambda j, i: (j, 0, 0)),
         pl.BlockSpec((None, 2 * S5_W, 128), lambda j, i: (j, 0, 0)),
         pl.BlockSpec((None, 10, 8, S5_W), lambda j, i: (j, 0, 0, 0))],
        [pl.BlockSpec((S5_R, 128), lambda j, i: (nb - 1 - i, j)),
         pl.BlockSpec((None, 128, 2 * S5_W), lambda j, i: (j, 0, 0)),
         pl.BlockSpec((None, 2 * S5_W, 128), lambda j, i: (j, 0, 0)),
         pl.BlockSpec((None, 8, 2 * S5_W), lambda j, i: (j, 0, 0))],
        [jax.ShapeDtypeStruct((T, D), F32), jax.ShapeDtypeStruct((S5_SLABS, 128, 2 * S5_W), F32),
         jax.ShapeDtypeStruct((S5_SLABS, 2 * S5_W, 128), F32), jax.ShapeDtypeStruct((S5_SLABS, 8, 2 * S5_W), F32)],
        [pltpu.VMEM((8, 2 * S5_W), F32)], [a, dy, xs, xs, bb, cb, tab_rev], rider)


def _s5_discretise(lam_re, lam_im, log_dt, b_re, b_im, c_re, c_im):
    dt = jnp.exp(log_dt)[:, None]
    mag = jnp.exp(lam_re * dt)
    abar_re = mag * jnp.cos(lam_im * dt)
    abar_im = mag * jnp.sin(lam_im * dt)
    den = lam_re * lam_re + lam_im * lam_im
    zoh_re = ((abar_re - 1.0) * lam_re + abar_im * lam_im) / den
    zoh_im = (abar_im * lam_re - (abar_re - 1.0) * lam_im) / den
    bbar_re = zoh_re[..., None] * b_re - zoh_im[..., None] * b_im
    bbar_im = zoh_re[..., None] * b_im + zoh_im[..., None] * b_re
    eye = jnp.eye(8, dtype=F32)

    def in_map(bbar):
        t = bbar.reshape(8, 8, S5_P, S5_K).transpose(0, 1, 3, 2)
        return (t[:, :, :, None, :] * eye[None, :, None, :, None]).reshape(8, 8 * S5_K, 8 * S5_P)

    def out_map(c):
        t = c.reshape(8, 8, S5_K, S5_P).transpose(0, 1, 3, 2)
        return (t[:, :, :, None, :] * eye[None, :, None, :, None]).reshape(8, 8 * S5_P, 8 * S5_K)

    bb = jnp.concatenate([in_map(bbar_re), in_map(bbar_im)], axis=2)
    cb = jnp.concatenate([out_map(c_re), -out_map(c_im)], axis=1)
    return bb, cb, abar_re.reshape(8, S5_W), abar_im.reshape(8, S5_W)


def _s5_tables(ar, ai, reverse):
    if reverse:
        ai = -ai
    pw = [(jnp.ones_like(ar), jnp.zeros_like(ar))]
    for _ in range(8):
        pw.append(_cmul(pw[-1][0], pw[-1][1], ar, ai))
    r = jnp.arange(8)[None, :, None]
    rows = []
    for d in (1, 2, 4):
        keep = (r <= 7 - d) if reverse else (r >= d)
        rows += [jnp.where(keep, pw[d][0][:, None, :], 0.0), jnp.where(keep, pw[d][1][:, None, :], 0.0)]
    order = [8 - k for k in range(8)] if reverse else [k + 1 for k in range(8)]
    rows += [jnp.stack([pw[n][0] for n in order], axis=1), jnp.stack([pw[n][1] for n in order], axis=1)]
    rows += [jnp.broadcast_to(pw[8][0][:, None, :], (8, 8, S5_W)), jnp.broadcast_to(pw[8][1][:, None, :], (8, 8, S5_W))]
    return jnp.stack(rows, axis=1)


def _s5_act_fn(pids, yc, a, dskip):
    return (jax.nn.gelu(yc + dskip * a),)


def _make_glu_res_fn(tm):
    def glu_res_fn(pids, zz, h):
        rows = _row_ids(pids[0], tm, (tm, 1))
        return (h + jnp.where(rows >= PAD, zz[:, :D] * jax.nn.sigmoid(zz[:, D:]), 0.0),)
    return glu_res_fn


def s5_mixer(h, g, w, rider=None):
    T = h.shape[0]
    tm = _row_tile(T)
    nt = T // tm
    a = norm_fwd("s5_norm", h, g, F32)
    ssm = [w[n] for n in ('lam_re', 'lam_im', 'log_dt', 'b_re', 'b_im', 'c_re', 'c_im')]
    (bb, cb, ar, ai), disc_vjp = jax.vjp(_s5_discretise, *ssm)
    yc, xs = s5_scan_fwd(a, bb, cb, _s5_tables(ar, ai, False), rider=rider)
    row = lambda arr: _rows(arr, tm)
    act_args = [row(yc), row(a), _const(w['dskip'])]
    act_outs = [Out((T, D), BF16, (tm, D), lambda i: (i, 0))]
    y = stage_fwd("s5_act", _s5_act_fn, (nt,), act_args, act_outs)[0]
    zz = matmul("s5_glu", y, w['w_glu'], 'nn')
    glu_fn = _make_glu_res_fn(tm)
    glu_args = [_rows(zz, tm, BF16), row(h)]
    glu_outs = [Out((T, D), F32, (tm, D), lambda i: (i, 0))]
    h_new = stage_fwd("s5_gate", glu_fn, (nt,), glu_args, glu_outs)[0]

    def bwd(dh, rider=None):
        dzz, dh_res = stage_bwd("s5_gate_b", glu_fn, (nt,), glu_args, glu_outs, [dh])
        dy, dwglu = linear_bwd("s5_glu_b", y, w['w_glu'], dzz)
        dyc, da1, ddskip = stage_bwd("s5_act_b", _s5_act_fn, (nt,), act_args, act_outs, [dy])
        da2, dbb, dcb, dab = s5_scan_bwd(a, bb, cb, _s5_tables(ar, ai, True), xs, dyc, rider=rider)
        dab = dab.sum(axis=1)
        dssm = disc_vjp((dbb, dcb, dab[:, :S5_W], dab[:, S5_W:]))
        dh2, dg = _s5_norm_bwd(h, g, da1, da2, dh_res, tm)
        grads = dict(zip(('lam_re', 'lam_im', 'log_dt', 'b_re', 'b_im', 'c_re', 'c_im'), dssm))
        grads.update(g=dg, dskip=ddskip, w_glu=dwglu)
        return dh2, grads

    return h_new, bwd


def _norm3_bwd_fn(pids, h, g):
    a = _rms(h, g)
    return a, a, h


def _s5_norm_bwd(h, g, da1, da2, dh, tm):
    T = h.shape[0]
    o = Out((T, D), F32, (tm, D), lambda i: (i, 0))
    return stage_bwd("s5_norm_b", _norm3_bwd_fn, (T // tm,), [_rows(h, tm), _const(g)], [o, o, o], [da1, da2, dh])


RET_R = 384


def _rope256(x, cos, sin):
    x1, x2 = x[:, :128], x[:, 128:]
    return jnp.concatenate([x1 * cos - x2 * sin, x1 * sin + x2 * cos], axis=1)


def _ret_fn(pids, z, gn, cos, sin, dmat, qdec, kdec, cdec, st):
    R = z.shape[0]
    q = _rope256(z[:, :256], cos, sin)
    k = _rope256(z[:, 256:512], cos, sin) * (RET_DK ** -0.5)
    v, gate = z[:, 512:1024], z[:, 1024:]
    outs = []
    for cc in range(R // CHUNK):
        lo = cc * CHUNK
        qc, kc, vc = q[lo:lo + CHUNK], k[lo:lo + CHUNK], v[lo:lo + CHUNK]
        outs.append(mm_nn(mm_nt(qc, kc) * dmat, vc) + mm_nn(qc * qdec, st))
        st = st * cdec + mm_tn(kc * kdec, vc)
    o = jnp.concatenate(outs, axis=0)
    mu = jnp.mean(o, axis=-1, keepdims=True)
    var = jnp.mean(jnp.square(o - mu), axis=-1, keepdims=True)
    o = (o - mu) * lax.rsqrt(var + EPS)
    return o * gn * _silu(gate), st


def ret_mixer(h, g, w, tabs, rider=None):
    T = h.shape[0]
    a = norm_fwd("ret_norm", h, g, BF16)
    z = matmul("ret_in", a, w['w_in'], 'nn')
    grid = (RET_H, T // RET_R)
    hw = RET_DK * 2 + RET_DV * 2
    args = [Arg(z, (RET_R, hw), lambda hh, j: (j, hh), gdtype=BF16), Arg(w['gn'], (1, RET_DV), lambda hh, j: (0, hh)),
            Arg(tabs['cos_d'], (RET_R, 128), lambda hh, j: (j, 0), False),
            Arg(tabs['sin_d'], (RET_R, 128), lambda hh, j: (j, 0), False),
            Arg(tabs['ret_dmat'], (None, CHUNK, CHUNK), lambda hh, j: (hh, 0, 0), False),
            Arg(tabs['ret_qdec'], (None, CHUNK, 1), lambda hh, j: (hh, 0, 0), False),
            Arg(tabs['ret_kdec'], (None, CHUNK, 1), lambda hh, j: (hh, 0, 0), False),
            Arg(tabs['ret_cdec'], (None, 1, 1), lambda hh, j: (hh, 0, 0), False)]
    outs = [Out((T, RET_H * RET_DV), BF16, (RET_R, RET_DV), lambda hh, j: (j, hh))]
    o, states = stage_fwd("ret_chunks", _ret_fn, grid, args, outs, state_shape=(RET_DK, RET_DV), rider=rider)
    h_new = matmul("ret_o", o, w['w_o'], 'nn', res=h)

    def bwd(dh, rider=None):
        do, dwo = linear_bwd("ret_o_b", o, w['w_o'], dh, mask=True)
        dz, dgn = stage_bwd("ret_chunks_b", _ret_fn, grid, args, outs, [do], state_shape=(RET_DK, RET_DV),
                            states=states, rider=rider)
        da, dwin = linear_bwd("ret_in_b", a, w['w_in'], dz)
        dh2, dg = norm_bwd("ret_norm_b", h, g, da, dh)
        return dh2, dict(g=dg, w_in=dwin, gn=dgn, w_o=dwo)

    return h_new, bwd


def loss_head(h, tgt):
    T = h.shape[0]
    tm = _row_tile(T)

    def body(h_ref, t_ref, loss_ref, dh_ref):
        i = pl.program_id(0)
        rows = _row_ids(i, tm, (tm, 1))
        err = jnp.where(rows >= LEAD, h_ref[...] - t_ref[...], 0.0)
        dh_ref[...] = err * (1.0 / D)
        part = jnp.full((8, 128), 0.5 * jnp.sum(jnp.sum(err * err, axis=1, keepdims=True) * (1.0 / D)), F32)

        @pl.when(i == 0)
        def _():
            loss_ref[...] = part

        @pl.when(i != 0)
        def _():
            loss_ref[...] += part

    loss, dh = pl.pallas_call(
        body, grid=(T // tm,),
        in_specs=[pl.BlockSpec((tm, D), lambda i: (i, 0)), pl.BlockSpec((tm, D), lambda i: (i, 0))],
        out_specs=[pl.BlockSpec((8, 128), lambda i: (0, 0)), pl.BlockSpec((tm, D), lambda i: (i, 0))],
        out_shape=[jax.ShapeDtypeStruct((8, 128), F32), jax.ShapeDtypeStruct((T, D), F32)], name="loss_head",
        compiler_params=_cparams())(h, tgt)
    return loss[0, 0], dh


def _tables(T):
    pos = jnp.maximum(jnp.arange(T, dtype=jnp.int32) - PAD, 0).astype(F32)

    def cs(dim):
        inv_freq = 1.0 / (10000.0 ** (jnp.arange(0, dim, 2, dtype=F32) / dim))
        ang = pos[:, None] * inv_freq[None, :]
        return jnp.cos(ang), jnp.sin(ang)

    ca, sa = cs(MLA_ROPE)
    zeros = jnp.zeros((T, 64), F32)
    cd, sd = cs(RET_DK)
    log_gamma = jnp.log(1.0 - jnp.exp2(-5.0 - jnp.arange(RET_H, dtype=F32)))
    p = jnp.arange(CHUNK, dtype=F32)
    diff = p[:, None] - p[None, :]
    dmat = jnp.where(diff >= 0, jnp.exp(diff[None] * log_gamma[:, None, None]), 0.0)
    return dict(cos_a=jnp.concatenate([ca, ca, zeros], axis=1), sin_a=jnp.concatenate([-sa, sa, zeros], axis=1),
                cos_d=cd, sin_d=sd, ret_dmat=dmat,
                ret_qdec=jnp.exp((p[None, :] + 1.0) * log_gamma[:, None])[..., None],
                ret_kdec=jnp.exp((CHUNK - 1.0 - p[None, :]) * log_gamma[:, None])[..., None],
                ret_cdec=jnp.exp(CHUNK * log_gamma)[:, None, None])


def _hgrn_lower_bound(logits):
    lb_cum = jnp.cumsum(jax.nn.softmax(logits, axis=0), axis=0)
    return (lb_cum - lb_cum[0:1])[1:2]


def _uq_to_heads(w):
    t = w.reshape(w.shape[0], MLA_H, MLA_QK)
    return jnp.pad(t, ((0, 0), (0, 0), (0, 256 - MLA_QK))).reshape(w.shape[0], MLA_H * 256)


def _uq_from_heads(g):
    return g.reshape(g.shape[0], MLA_H, 256)[:, :, :MLA_QK].reshape(g.shape[0], MLA_H * MLA_QK)


def _head_interleave(w, widths, heads):
    parts, lo = [], 0
    for wd in widths:
        parts.append(w[:, lo:lo + heads * wd].reshape(w.shape[0], heads, wd))
        lo += heads * wd
    return jnp.concatenate(parts, axis=2).reshape(w.shape[0], -1)


def _head_deinterleave(g, widths, heads):
    t = g.reshape(g.shape[0], heads, sum(widths))
    parts, lo = [], 0
    for wd in widths:
        parts.append(t[:, :, lo:lo + wd].reshape(g.shape[0], heads * wd))
        lo += wd
    return jnp.concatenate(parts, axis=1)


HG_WIDTHS = (128, 128, 128, 128)
RET_WIDTHS = (RET_DK, RET_DK, RET_DV, RET_DV)


def _split_head_gain(g):
    return g[:, :128], jnp.pad(g[:, 128:], ((0, 0), (0, 64)))


def _join_head_gain(dn, dr):
    return jnp.concatenate([dn, dr[:, :64]], axis=1)


def local_step(x, target, W, ex):
    S = x.shape[0]
    T = S + LEAD
    tabs = _tables(T)
    h = jnp.concatenate([jnp.zeros((PAD, D), F32), W['meta_tokens'], x], axis=0)
    tgt = jnp.concatenate([jnp.zeros((LEAD, D), F32), target], axis=0)

    gqn, gqr = _split_head_gain(W['mla_q_head_g'])
    gkn, gkr = _split_head_gain(W['mla_k_head_g'])
    lb, lb_vjp = jax.vjp(_hgrn_lower_bound, W['hgrn_lb_logits'])

    def ffn(i, hh):
        return ffn_layer(i, hh, W['norm_ffn_g'][i:i + 1], ex.weight('ffn_w_up', i), W['ffn_conv_w'][i],
                         W['ffn_conv_b'][i:i + 1], ex.weight('ffn_w_down', i))

    bm, bf = [None] * 4, [None] * 4
    ex.gather(['mla'], name="gather_mla")
    w0 = dict(w_down=jnp.pad(ex.weight('mla_w_down'), ((0, 0), (0, 64))), gcq=W['mla_cq_norm_g'],
              gckv=W['mla_ckv_norm_g'], w_uq=_uq_to_heads(ex.weight('mla_w_uq')), w_ukv=ex.weight('mla_w_ukv'),
              gqn=gqn, gqr=gqr, gkn=gkn, gkr=gkr, w_o=ex.weight('mla_w_o'))
    h, bm[0] = mla_mixer(h, W['norm_mix_g'][0:1], w0, tabs, rider=ex.gather(['ffn0', 'hgrn', 'ffn1']))
    h, bf[0] = ffn(0, h)
    w1 = dict(w_in=_head_interleave(ex.weight('hgrn_w_in'), HG_WIDTHS, HG_H), lb=lb, go=W['hgrn_o_norm_g'],
              w_o=ex.weight('hgrn_w_o'))
    h, bm[1] = hgrn_mixer(h, W['norm_mix_g'][1:2], w1, rider=ex.gather(['s5', 'ffn2']))
    h, bf[1] = ffn(1, h)
    w2 = dict(lam_re=W['s5_lam_re'][0], lam_im=W['s5_lam_im'][0], log_dt=W['s5_log_dt'][0], b_re=W['s5_b_re'][0],
              b_im=W['s5_b_im'][0], c_re=W['s5_c_re'][0], c_im=W['s5_c_im'][0], dskip=W['s5_d'],
              w_glu=ex.weight('s5_w_glu'))
    h, bm[2] = s5_mixer(h, W['norm_mix_g'][2:3], w2, rider=ex.gather(['ret']))
    h, bf[2] = ffn(2, h)
    w3 = dict(w_in=_head_interleave(ex.weight('ret_w_in'), RET_WIDTHS, RET_H), gn=W['ret_gn_g'],
              w_o=ex.weight('ret_w_o'))
    h, bm[3] = ret_mixer(h, W['norm_mix_g'][3:4], w3, tabs, rider=ex.gather(['ffn3']))
    h, bf[3] = ffn(3, h)

    loss, dh = loss_head(h, tgt)

    def ffn_grads(i, g):
        return {('ffn_w_up', i): g['w_up'], ('ffn_w_down', i): g['w_down']}

    gm, gf = [None] * 4, [None] * 4
    dh, gf[3] = bf[3](dh)
    dh, gm[3] = bm[3](dh, rider=ex.scatter(ffn_grads(3, gf[3])))
    dh, gf[2] = bf[2](dh)
    ret_grads = {('ret_w_in', 0): _head_deinterleave(gm[3]['w_in'], RET_WIDTHS, RET_H), ('ret_w_o', 0): gm[3]['w_o']}
    dh, gm[2] = bm[2](dh, rider=ex.scatter(ffn_grads(2, gf[2])))
    dh, gf[1] = bf[1](dh)
    dh, gm[1] = bm[1](dh, rider=ex.scatter(ffn_grads(1, gf[1])))
    dh, gf[0] = bf[0](dh)
    hgrn_grads = {('hgrn_w_in', 0): _head_deinterleave(gm[1]['w_in'], HG_WIDTHS, HG_H), ('hgrn_w_o', 0): gm[1]['w_o']}
    G = {}
    G['norm_ffn_g'] = jnp.concatenate([gf[i]['g'] for i in range(4)], axis=0)
    G['hgrn_lb_logits'] = lb_vjp(gm[1]['lb'])[0]
    G['hgrn_o_norm_g'] = gm[1]['go']
    for n in ('lam_re', 'lam_im', 'log_dt', 'b_re', 'b_im', 'c_re', 'c_im'):
        G['s5_' + n] = gm[2][n][None]
    G['s5_d'] = gm[2]['dskip']
    G['ret_gn_g'] = gm[3]['gn']
    G['ffn_conv_w'] = jnp.stack([gf[i]['cw'] for i in range(4)])
    G['ffn_conv_b'] = jnp.concatenate([gf[i]['cb'] for i in range(4)], axis=0)
    early = ex.all_devices(_pack([G[n] for n in SMALL_EARLY], F32, 8))

    dh, gm[0] = bm[0](
        dh,
        rider_dkv=lambda dwo: [ex.scatter({('s5_w_glu', 0): gm[2]['w_glu'], **hgrn_grads, **ffn_grads(0, gf[0]),
                                           ('mla_w_o', 0): dwo}), ex.swap()],
        rider_dq=lambda: [early, ex.scatter(ret_grads)], rider_heads=ex.swap)
    a = gm[0]
    mla_grads = {('mla_w_down', 0): a['w_down'][:, :MLA_QL + MLA_KVL + MLA_ROPE], ('mla_w_uq', 0): _uq_from_heads(a['w_uq']),
                 ('mla_w_ukv', 0): a['w_ukv']}
    G['meta_tokens'] = dh[PAD:LEAD]
    G['norm_mix_g'] = jnp.concatenate([gm[i]['g'] for i in range(4)], axis=0)
    G['mla_cq_norm_g'], G['mla_ckv_norm_g'] = a['gcq'], a['gckv']
    G['mla_q_head_g'] = _join_head_gain(a['gqn'], a['gqr'])
    G['mla_k_head_g'] = _join_head_gain(a['gkn'], a['gkr'])
    ex.tail = [ex.scatter(mla_grads), ex.all_devices(_pack([G[n] for n in SMALL_LATE], F32, 8))]
    return loss, dh[LEAD:], G


PACK_W = 1024
ANY = pl.BlockSpec(memory_space=pl.ANY)


def _pack(arrs, dtype, row_mult):
    flat = jnp.concatenate([a.reshape(-1).astype(dtype) for a in arrs])
    n = flat.shape[0]
    rows = -(-n // (PACK_W * row_mult)) * row_mult
    return jnp.pad(flat, (0, rows * PACK_W - n)).reshape(rows, PACK_W)


def _unpack(buf, shapes):
    flat = buf.reshape(-1)
    out, off = [], 0
    for s in shapes:
        n = math.prod(s)
        out.append(flat[off:off + n].reshape(s))
        off += n
    return out


def _my_pos():
    return lax.axis_index("x"), lax.axis_index("y"), lax.axis_index("c")


def _other_chips(x, y):
    return [(1 - x, y), (x, 1 - y), (1 - x, 1 - y)]


def gather_chips(name, src):
    def body(src_ref, out_ref, send_sems, recv_sems, local_sem):
        x, y, c = _my_pos()
        q = 2 * x + y
        mine = pltpu.make_async_copy(src_ref, out_ref.at[q], local_sem)
        mine.start()
        peers = _other_chips(x, y)

        def copy(k, slot, peer):
            return pltpu.make_async_remote_copy(src_ref=src_ref, dst_ref=out_ref.at[slot], send_sem=send_sems.at[k],
                                                recv_sem=recv_sems.at[k], device_id=(peer[0], peer[1], c),
                                                device_id_type=MESH_ID)
        sends = [copy(k, q, p) for k, p in enumerate(peers)]
        for cp in sends:
            cp.start()
        for k, p in enumerate(peers):
            copy(k, 2 * p[0] + p[1], p).wait_recv()
        for cp in sends:
            cp.wait_send()
        mine.wait()

    return pl.pallas_call(body, out_shape=jax.ShapeDtypeStruct((4,) + src.shape, src.dtype), in_specs=[ANY],
                          out_specs=ANY, name=name,
                          scratch_shapes=[pltpu.SemaphoreType.DMA((3,)), pltpu.SemaphoreType.DMA((3,)),
                                          pltpu.SemaphoreType.DMA(())])(src)


def _pack_tile(rows):
    return _divisors(rows, (256, 128, 64, 32, 16, 8))[0] if rows > 512 else rows


def sum_slots(name, slots):
    n, rows, w = slots.shape
    tr = _pack_tile(rows)

    def body(s_ref, o_ref):
        acc = s_ref[0].astype(F32)
        for k in range(1, n):
            acc = acc + s_ref[k].astype(F32)
        o_ref[...] = acc

    return pl.pallas_call(body, grid=(rows // tr,), in_specs=[pl.BlockSpec((n, tr, w), lambda i: (0, i, 0))],
                          out_specs=pl.BlockSpec((tr, w), lambda i: (i, 0)),
                          out_shape=jax.ShapeDtypeStruct((rows, w), F32), name=name, compiler_params=_cparams())(slots)


def adamw(name, grads, w, m, v):
    rows, wd = w.shape
    tr = _pack_tile(rows)
    ng = len(grads)

    def body(*refs):
        g = refs[0][...]
        for r in refs[1:ng]:
            g = g + r[...]
        w_ref, m_ref, v_ref = refs[ng:ng + 3]
        g_out, d_out, m_out, v_out = refs[ng + 3:]
        m_new = ADAM_B1 * m_ref[...] + (1.0 - ADAM_B1) * g
        v_new = ADAM_B2 * v_ref[...] + (1.0 - ADAM_B2) * jnp.square(g)
        m_hat = m_new / (1.0 - ADAM_B1 ** ADAM_STEP)
        v_hat = v_new / (1.0 - ADAM_B2 ** ADAM_STEP)
        g_out[...] = g
        d_out[...] = -ADAM_LR * (m_hat / (jnp.sqrt(v_hat) + ADAM_EPS) + ADAM_WD * w_ref[...])
        m_out[...] = m_new
        v_out[...] = v_new

    spec = pl.BlockSpec((tr, wd), lambda i: (i, 0))
    shape = jax.ShapeDtypeStruct((rows, wd), F32)
    return pl.pallas_call(body, grid=(rows // tr,), in_specs=[spec] * (ng + 3), out_specs=[spec] * 4,
                          out_shape=[shape] * 4, name=name, compiler_params=_cparams())(*grads, w, m, v)


def _sem_scratch(nw):
    return [pltpu.SemaphoreType.DMA((3 * nw,)), pltpu.SemaphoreType.DMA((3 * nw,)), pltpu.SemaphoreType.DMA((nw,))]


def _block2d(ref, axis, p, n):
    if axis == 0:
        return ref.at[pl.ds(pl.multiple_of(p * n, 16), n), :]
    return ref.at[:, pl.ds(pl.multiple_of(p * n, 128), n)]


class ScatterRider:
    def __init__(self, items):
        self.items = items
        self.operands = [it[0] for it in items]
        self.results = None
        self.out_shapes = [jax.ShapeDtypeStruct((4, arr.shape[0] // 4, arr.shape[1]) if axis == 0 else
                                                (4, arr.shape[0], arr.shape[1] // 4), arr.dtype) for arr, axis in items]
        self.scratch = _sem_scratch(len(items))

    def _copies(self, ins, outs, sems):
        send_sems, recv_sems, local_sems = sems
        x, y, c = _my_pos()
        q = 2 * x + y
        local, sends, lands = [], [], []
        for w, (arr, axis) in enumerate(self.items):
            n = arr.shape[axis] // 4
            local.append(pltpu.make_async_copy(_block2d(ins[w], axis, q, n), outs[w].at[q], local_sems.at[w]))
            for k, (px, py) in enumerate(_other_chips(x, y)):
                p = 2 * px + py
                sems_k = dict(send_sem=send_sems.at[3 * w + k], recv_sem=recv_sems.at[3 * w + k],
                              device_id=(px, py, c), device_id_type=MESH_ID)
                theirs = _block2d(ins[w], axis, p, n)
                sends.append(pltpu.make_async_remote_copy(src_ref=theirs, dst_ref=outs[w].at[q], **sems_k))
                lands.append(pltpu.make_async_remote_copy(src_ref=theirs, dst_ref=outs[w].at[p], **sems_k))
        return local, sends, lands

    def start(self, ins, outs, sems):
        local, sends, _ = self._copies(ins, outs, sems)
        for cp in local + sends:
            cp.start()

    def finish(self, ins, outs, sems):
        local, sends, lands = self._copies(ins, outs, sems)
        for cp in lands:
            cp.wait_recv()
        for cp in sends:
            cp.wait_send()
        for cp in local:
            cp.wait()


class GatherRider:
    def __init__(self, items):
        self.items = items
        self.operands = [it[0] for it in items]
        self.results = None
        self.out_shapes = []
        for arr, _, axis in items:
            r, c = arr.shape[1:]
            assert r % 32 == 0
            self.out_shapes.append(jax.ShapeDtypeStruct((4 * r, c) if axis == 0 else (r, 4 * c), arr.dtype))
        n = len(items)
        dma = pltpu.SemaphoreType.DMA
        self.scratch = [dma((3 * n,)), dma((3 * n,)), dma((n,)), dma((3 * n,)), dma((3 * n,))]

    def _copies(self, ins, outs, sems):
        send_sems, recv_sems, local_sems, pass_send_sems, pass_recv_sems = sems
        x, y, c = _my_pos()
        q = 2 * x + y
        local, sends, lands, passes, pass_lands = [], [], [], [], []
        for w, (arr, layer, axis) in enumerate(self.items):
            r, cols = arr.shape[1:]
            half = r // 2
            src = ins[w].at[layer]

            def part(blk, hc, w=w, axis=axis, r=r, cols=cols, half=half):
                if axis == 0:
                    return outs[w].at[pl.ds(pl.multiple_of(blk * r + hc * half, 16), half), :]
                return outs[w].at[pl.ds(pl.multiple_of(hc * half, 16), half), pl.ds(pl.multiple_of(blk * cols, 128), cols)]

            local.append(pltpu.make_async_copy(src, _block2d(outs[w], axis, q, arr.shape[1 + axis]), local_sems.at[w]))
            for k, (px, py) in enumerate(_other_chips(x, y)):
                p = 2 * px + py
                ici = dict(send_sem=send_sems.at[3 * w + k], recv_sem=recv_sems.at[3 * w + k],
                           device_id=(px, py, c), device_id_type=MESH_ID)
                d2d = dict(send_sem=pass_send_sems.at[3 * w + k], recv_sem=pass_recv_sems.at[3 * w + k],
                           device_id=(x, y, 1 - c), device_id_type=MESH_ID)
                mine = src.at[pl.ds(pl.multiple_of(c * half, 16), half), :]
                sends.append(pltpu.make_async_remote_copy(src_ref=mine, dst_ref=part(q, c), **ici))
                lands.append(pltpu.make_async_remote_copy(src_ref=mine, dst_ref=part(p, c), **ici))
                passes.append(pltpu.make_async_remote_copy(src_ref=part(p, c), dst_ref=part(p, c), **d2d))
                pass_lands.append(pltpu.make_async_remote_copy(src_ref=part(p, c), dst_ref=part(p, 1 - c), **d2d))
        return local, sends, lands, passes, pass_lands

    def start(self, ins, outs, sems):
        local, sends, _, _, _ = self._copies(ins, outs, sems)
        for cp in local + sends:
            cp.start()

    def middle(self, ins, outs, sems):
        _, _, lands, passes, _ = self._copies(ins, outs, sems)
        for land, cp in zip(lands, passes):
            land.wait_recv()
            cp.start()

    def finish(self, ins, outs, sems):
        local, sends, _, passes, pass_lands = self._copies(ins, outs, sems)
        for cp in pass_lands:
            cp.wait_recv()
        for cp in sends + passes:
            cp.wait_send()
        for cp in local:
            cp.wait()


class SwapRider:
    def __init__(self, arrs):
        self.operands = list(arrs)
        self.out_shapes = [jax.ShapeDtypeStruct(a.shape, a.dtype) for a in arrs]
        self.scratch = [pltpu.SemaphoreType.DMA((len(arrs),)), pltpu.SemaphoreType.DMA((len(arrs),))]
        self.results = None

    def _copies(self, ins, outs, sems):
        x, y, c = _my_pos()
        return [pltpu.make_async_remote_copy(src_ref=ins[w], dst_ref=outs[w], send_sem=sems[0].at[w],
                                             recv_sem=sems[1].at[w], device_id=(x, y, 1 - c), device_id_type=MESH_ID)
                for w in range(len(self.operands))]

    def start(self, ins, outs, sems):
        for cp in self._copies(ins, outs, sems):
            cp.start()

    def finish(self, ins, outs, sems):
        for cp in self._copies(ins, outs, sems):
            cp.wait()


class AllDevicesRider:
    def __init__(self, src):
        self.operands = [src]
        self.out_shapes = [jax.ShapeDtypeStruct((8,) + src.shape, src.dtype)]
        self.scratch = [pltpu.SemaphoreType.DMA((7,)), pltpu.SemaphoreType.DMA((7,)), pltpu.SemaphoreType.DMA(())]
        self.results = None

    def _copies(self, ins, outs, sems):
        x, y, c = _my_pos()
        me = 4 * x + 2 * y + c
        local = pltpu.make_async_copy(ins[0], outs[0].at[me], sems[2])
        sends, lands = [], []
        for k, m in enumerate(range(1, 8)):
            peer = ((1 - x) if m & 4 else x, (1 - y) if m & 2 else y, (1 - c) if m & 1 else c)
            sems_k = dict(send_sem=sems[0].at[k], recv_sem=sems[1].at[k], device_id=peer, device_id_type=MESH_ID)
            sends.append(pltpu.make_async_remote_copy(src_ref=ins[0], dst_ref=outs[0].at[me], **sems_k))
            lands.append(pltpu.make_async_remote_copy(src_ref=ins[0], dst_ref=outs[0].at[4 * peer[0] + 2 * peer[1] + peer[2]],
                                                      **sems_k))
        return local, sends, lands

    def start(self, ins, outs, sems):
        local, sends, _ = self._copies(ins, outs, sems)
        for cp in [local] + sends:
            cp.start()

    def finish(self, ins, outs, sems):
        local, sends, lands = self._copies(ins, outs, sems)
        for cp in lands:
            cp.wait_recv()
        for cp in sends:
            cp.wait_send()
        local.wait()


class RiderGroup:
    def __init__(self, riders):
        self.riders = riders
        self.operands = [a for r in riders for a in r.operands]
        self.out_shapes = [s for r in riders for s in r.out_shapes]
        self.scratch = [s for r in riders for s in r.scratch]

    def _split(self, ins, outs, sems):
        for r in self.riders:
            ni, no, ns = len(r.operands), len(r.out_shapes), len(r.scratch)
            yield r, ins[:ni], outs[:no], sems[:ns]
            ins, outs, sems = ins[ni:], outs[no:], sems[ns:]

    def start(self, ins, outs, sems):
        for r, i, o, s in self._split(ins, outs, sems):
            r.start(i, o, s)

    def middle(self, ins, outs, sems):
        for r, i, o, s in self._split(ins, outs, sems):
            if hasattr(r, 'middle'):
                r.middle(i, o, s)

    def finish(self, ins, outs, sems):
        for r, i, o, s in self._split(ins, outs, sems):
            r.finish(i, o, s)

    @property
    def results(self):
        return None

    @results.setter
    def results(self, res):
        for r in self.riders:
            no = len(r.out_shapes)
            r.results, res = list(res[:no]), res[no:]


def run_rider(name, rider):
    n_in, n_out = len(rider.operands), len(rider.out_shapes)

    def body(*refs):
        ins, outs, sems = refs[:n_in], refs[n_in:n_in + n_out], refs[n_in + n_out:]
        rider.start(ins, outs, sems)
        if hasattr(rider, 'middle'):
            rider.middle(ins, outs, sems)
        rider.finish(ins, outs, sems)

    rider.results = list(pl.pallas_call(body, out_shape=rider.out_shapes, in_specs=[ANY] * n_in, out_specs=[ANY] * n_out,
                                        name=name, scratch_shapes=rider.scratch)(*rider.operands))


WEIGHT_GROUPS = {'mla': [('mla_w_down', 0), ('mla_w_uq', 0), ('mla_w_ukv', 0), ('mla_w_o', 0)],
                 'hgrn': [('hgrn_w_in', 0), ('hgrn_w_o', 0)], 's5': [('s5_w_glu', 0)],
                 'ret': [('ret_w_in', 0), ('ret_w_o', 0)]}
WEIGHT_GROUPS.update({f'ffn{i}': [('ffn_w_up', i), ('ffn_w_down', i)] for i in range(4)})


class Exchange:
    def __init__(self, shards=None, full=None):
        self.shards, self.full = shards, dict(full or {})
        self.got, self.recv, self.sib, self.grads, self.small, self.tail = {}, {}, {}, {}, [], []

    def gather(self, groups, name=None):
        if self.shards is None:
            return None
        keys = [k for g in groups for k in WEIGHT_GROUPS[g]]
        rider = GatherRider([(self.shards[n], layer, SHARD_AXIS[n] - 1) for n, layer in keys])
        self.got.update({k: (rider, j) for j, k in enumerate(keys)})
        if name is not None:
            run_rider(name, rider)
        return rider

    def weight(self, n, layer=0):
        if self.shards is None:
            return self.full[n][layer]
        rider, j = self.got[(n, layer)]
        return rider.results[j]

    def scatter(self, grads, name=None):
        if self.shards is None:
            self.grads.update(grads)
            return None
        keys = list(grads)
        rider = ScatterRider([(grads[k], SHARD_AXIS[k[0]] - 1) for k in keys])
        self.recv.update({k: (rider, j) for j, k in enumerate(keys)})
        if name is not None:
            run_rider(name, rider)
        return rider

    def received(self, n, layer):
        rider, j = self.recv[(n, layer)]
        return rider.results[j]

    def swap(self, name=None):
        if self.shards is None:
            return None
        keys = [k for k, (r, _) in self.recv.items() if k not in self.sib and r.results is not None]
        rider = SwapRider([self.received(*k) for k in keys])
        self.sib.update({k: (rider, j) for j, k in enumerate(keys)})
        if name is not None:
            run_rider(name, rider)
        return rider

    def sibling(self, n, layer):
        rider, j = self.sib[(n, layer)]
        return rider.results[j]

    def all_devices(self, packed, name=None):
        if self.shards is None:
            return None
        rider = AllDevicesRider(packed)
        self.small.append(rider)
        if name is not None:
            run_rider(name, rider)
        return rider


ADAM_BLOCK_ELEMS = 256 * 1024


def adamw_shard(name, mine, sib, w, m, v, first_layer=0, into=None, rider=None):
    _, rows, cols = w.shape
    nl = mine.shape[1]
    tr = [t for t in (512, 384, 352, 256, 176, 128, 64, 32, 16) if rows % t == 0 and t * cols <= ADAM_BLOCK_ELEMS][0]

    def body(a_ref, b_ref, w_ref, m_ref, v_ref, *rest):
        g_out, d_out, m_out, v_out = rest[-4:]

        def total(r):
            acc = r[0].astype(F32)
            for k in range(1, 4):
                acc = acc + r[k].astype(F32)
            return acc
        g = total(a_ref) + total(b_ref)
        m_new = ADAM_B1 * m_ref[...] + (1.0 - ADAM_B1) * g
        v_new = ADAM_B2 * v_ref[...] + (1.0 - ADAM_B2) * jnp.square(g)
        m_hat = m_new / (1.0 - ADAM_B1 ** ADAM_STEP)
        v_hat = v_new / (1.0 - ADAM_B2 ** ADAM_STEP)
        g_out[...] = g
        d_out[...] = -ADAM_LR * (m_hat / (jnp.sqrt(v_hat) + ADAM_EPS) + ADAM_WD * w_ref[...])
        m_out[...] = m_new
        v_out[...] = v_new

    slots = pl.BlockSpec((4, None, tr, cols), lambda l, i: (0, l, i, 0))
    spec = pl.BlockSpec((None, tr, cols), lambda l, i: (l + first_layer, i, 0))
    shape = jax.ShapeDtypeStruct(w.shape, F32)
    in_specs, operands, aliases = [slots, slots, spec, spec, spec], [mine, sib, w, m, v], {}
    if into is not None:
        in_specs += [ANY] * 4
        operands += list(into)
        aliases = {5 + k: k for k in range(4)}
    return _pallas(name, body, (nl, rows // tr), in_specs, [spec] * 4, [shape] * 4, [], operands, rider, aliases)


def kernel(x, meta_tokens, norm_mix_g, norm_ffn_g, mla_w_down, mla_cq_norm_g, mla_ckv_norm_g, mla_w_uq, mla_w_ukv, mla_q_head_g, mla_k_head_g, mla_w_o, hgrn_w_in, hgrn_lb_logits, hgrn_o_norm_g, hgrn_w_o, s5_lam_re, s5_lam_im, s5_log_dt, s5_b_re, s5_b_im, s5_c_re, s5_c_im, s5_d, s5_w_glu, ret_w_in, ret_gn_g, ret_w_o, ffn_w_up, ffn_conv_w, ffn_conv_b, ffn_w_down, loss_target, m_meta_tokens, m_norm_mix_g, m_norm_ffn_g, m_mla_w_down, m_mla_cq_norm_g, m_mla_ckv_norm_g, m_mla_w_uq, m_mla_w_ukv, m_mla_q_head_g, m_mla_k_head_g, m_mla_w_o, m_hgrn_w_in, m_hgrn_lb_logits, m_hgrn_o_norm_g, m_hgrn_w_o, m_s5_lam_re, m_s5_lam_im, m_s5_log_dt, m_s5_b_re, m_s5_b_im, m_s5_c_re, m_s5_c_im, m_s5_d, m_s5_w_glu, m_ret_w_in, m_ret_gn_g, m_ret_w_o, m_ffn_w_up, m_ffn_conv_w, m_ffn_conv_b, m_ffn_w_down, v_meta_tokens, v_norm_mix_g, v_norm_ffn_g, v_mla_w_down, v_mla_cq_norm_g, v_mla_ckv_norm_g, v_mla_w_uq, v_mla_w_ukv, v_mla_q_head_g, v_mla_k_head_g, v_mla_w_o, v_hgrn_w_in, v_hgrn_lb_logits, v_hgrn_o_norm_g, v_hgrn_w_o, v_s5_lam_re, v_s5_lam_im, v_s5_log_dt, v_s5_b_re, v_s5_b_im, v_s5_c_re, v_s5_c_im, v_s5_d, v_s5_w_glu, v_ret_w_in, v_ret_gn_g, v_ret_w_o, v_ffn_w_up, v_ffn_conv_w, v_ffn_conv_b, v_ffn_w_down):
    vals = (x, meta_tokens, norm_mix_g, norm_ffn_g, mla_w_down, mla_cq_norm_g, mla_ckv_norm_g, mla_w_uq, mla_w_ukv, mla_q_head_g, mla_k_head_g, mla_w_o, hgrn_w_in, hgrn_lb_logits, hgrn_o_norm_g, hgrn_w_o, s5_lam_re, s5_lam_im, s5_log_dt, s5_b_re, s5_b_im, s5_c_re, s5_c_im, s5_d, s5_w_glu, ret_w_in, ret_gn_g, ret_w_o, ffn_w_up, ffn_conv_w, ffn_conv_b, ffn_w_down, loss_target, m_meta_tokens, m_norm_mix_g, m_norm_ffn_g, m_mla_w_down, m_mla_cq_norm_g, m_mla_ckv_norm_g, m_mla_w_uq, m_mla_w_ukv, m_mla_q_head_g, m_mla_k_head_g, m_mla_w_o, m_hgrn_w_in, m_hgrn_lb_logits, m_hgrn_o_norm_g, m_hgrn_w_o, m_s5_lam_re, m_s5_lam_im, m_s5_log_dt, m_s5_b_re, m_s5_b_im, m_s5_c_re, m_s5_c_im, m_s5_d, m_s5_w_glu, m_ret_w_in, m_ret_gn_g, m_ret_w_o, m_ffn_w_up, m_ffn_conv_w, m_ffn_conv_b, m_ffn_w_down, v_meta_tokens, v_norm_mix_g, v_norm_ffn_g, v_mla_w_down, v_mla_cq_norm_g, v_mla_ckv_norm_g, v_mla_w_uq, v_mla_w_ukv, v_mla_q_head_g, v_mla_k_head_g, v_mla_w_o, v_hgrn_w_in, v_hgrn_lb_logits, v_hgrn_o_norm_g, v_hgrn_w_o, v_s5_lam_re, v_s5_lam_im, v_s5_log_dt, v_s5_b_re, v_s5_b_im, v_s5_c_re, v_s5_c_im, v_s5_d, v_s5_w_glu, v_ret_w_in, v_ret_gn_g, v_ret_w_o, v_ffn_w_up, v_ffn_conv_w, v_ffn_conv_b, v_ffn_w_down)
    names = ['x'] + WEIGHTS + ['loss_target'] + ['m_' + n for n in WEIGHTS] + ['v_' + n for n in WEIGHTS]
    A = dict(zip(names, vals))
    q = 2 * lax.axis_index("x") + lax.axis_index("y")

    small_shapes = [A[n].shape for n in SMALL_SHARDED]
    got_small = gather_chips("gather_small", _pack([A[n] for n in SMALL_SHARDED], F32, 8))
    W = {n: A[n] for n in REPLICATED}
    parts_small = [_unpack(got_small[p], small_shapes) for p in range(4)]
    for k, n in enumerate(SMALL_SHARDED):
        W[n] = jnp.concatenate([parts_small[p][k] for p in range(4)], axis=SHARD_AXIS[n])

    ex = Exchange(shards={n: A[n].astype(BF16) for n in BIG})
    loss, grad_x, G = local_step(A['x'][0], A['loss_target'][0], W, ex)
    loss = lax.psum(loss, ("x", "y", "c"))

    run_rider("scatter_mla", RiderGroup(ex.tail))
    ex.swap(name="grad_big_sibling")
    res_big = []
    for n in BIG:
        res = None
        for layer in range(A[n].shape[0]):
            res = adamw_shard(f"adam_{n}_{layer}", ex.received(n, layer)[:, None], ex.sibling(n, layer)[:, None],
                              A[n], A['m_' + n], A['v_' + n], first_layer=layer, into=res)
        res_big.append(res)

    small_names = REPLICATED + SMALL_SHARDED
    gs = {}
    for part, names, rider in (("early", SMALL_EARLY, ex.small[0]), ("late", SMALL_LATE, ex.small[1])):
        total = sum_slots("grad_small_sum_" + part, rider.results[0])
        gs.update(zip(names, _unpack(total, [G[n].shape for n in names])))
    for n in SMALL_SHARDED:
        ax = SHARD_AXIS[n]
        size = gs[n].shape[ax] // 4
        gs[n] = lax.dynamic_slice_in_dim(gs[n], q * size, size, axis=ax)
    pk = lambda pre: _pack([A[pre + n] for n in small_names], F32, 8)
    own_shapes = [A[n].shape for n in small_names]
    res_small = [_unpack(r, own_shapes) for r in
                 adamw("adam_small", [_pack([gs[n] for n in small_names], F32, 8)], pk(''), pk('m_'), pk('v_'))]

    out = {}
    for j, kind in enumerate(('grad_', 'delta_', 'new_m_', 'new_v_')):
        for k, n in enumerate(BIG):
            out[kind + n] = res_big[k][j]
        for k, n in enumerate(small_names):
            out[kind + n] = res_small[j][k]
    return (loss, grad_x[None]) + tuple(out[kind + n] for kind in ('grad_', 'delta_', 'new_m_', 'new_v_')
                                        for n in WEIGHTS)
```

```python
import functools
import math

import jax
import jax.numpy as jnp
from jax import lax
from jax.experimental import pallas as pl
from jax.experimental.pallas import tpu as pltpu

F32, BF16 = jnp.float32, jnp.bfloat16
MESH_ID = pl.DeviceIdType.MESH

D = 1024
N_META = 16
PAD = 112
LEAD = PAD + N_META
EPS = 1e-6
NEG_INF = -1e30
CHUNK = 64
VMEM_LIMIT_V7X = 56 * 1024 * 1024
MM_VMEM_BUDGET = 36 * 1024 * 1024

MLA_H, MLA_NOPE, MLA_ROPE, MLA_V = 8, 128, 64, 128
MLA_QK = MLA_NOPE + MLA_ROPE
MLA_QL, MLA_KVL = 384, 256
HG_H, HG_D, HG_C = 8, 128, 16
S5_G, S5_P, S5_K = 64, 64, 16
RET_H, RET_DK, RET_DV = 4, 256, 512
FFN_F = 2816

ADAM_LR, ADAM_B1, ADAM_B2, ADAM_EPS, ADAM_WD, ADAM_STEP = 0.001, 0.9, 0.999, 1e-08, 0.01, 10

WEIGHTS = ['meta_tokens', 'norm_mix_g', 'norm_ffn_g', 'mla_w_down', 'mla_cq_norm_g', 'mla_ckv_norm_g', 'mla_w_uq',
           'mla_w_ukv', 'mla_q_head_g', 'mla_k_head_g', 'mla_w_o', 'hgrn_w_in', 'hgrn_lb_logits', 'hgrn_o_norm_g',
           'hgrn_w_o', 's5_lam_re', 's5_lam_im', 's5_log_dt', 's5_b_re', 's5_b_im', 's5_c_re', 's5_c_im', 's5_d',
           's5_w_glu', 'ret_w_in', 'ret_gn_g', 'ret_w_o', 'ffn_w_up', 'ffn_conv_w', 'ffn_conv_b', 'ffn_w_down']
SHARD_AXIS = {'meta_tokens': 1, 'mla_w_down': 1, 'mla_w_uq': 2, 'mla_w_ukv': 2, 'mla_w_o': 1, 'hgrn_w_in': 2,
              'hgrn_w_o': 1, 's5_d': 1, 's5_w_glu': 2, 'ret_w_in': 2, 'ret_gn_g': 1, 'ret_w_o': 1, 'ffn_w_up': 2,
              'ffn_conv_w': 2, 'ffn_w_down': 1}
BIG = ['mla_w_down', 'mla_w_uq', 'mla_w_ukv', 'mla_w_o', 'hgrn_w_in', 'hgrn_w_o', 's5_w_glu', 'ret_w_in', 'ret_w_o',
       'ffn_w_up', 'ffn_w_down']
SMALL_SHARDED = ['meta_tokens', 's5_d', 'ret_gn_g', 'ffn_conv_w']
REPLICATED = [n for n in WEIGHTS if n not in SHARD_AXIS]
SMALL_LATE = ['meta_tokens', 'norm_mix_g', 'mla_cq_norm_g', 'mla_ckv_norm_g', 'mla_q_head_g', 'mla_k_head_g']
SMALL_EARLY = [n for n in REPLICATED + SMALL_SHARDED if n not in SMALL_LATE]


def _cparams():
    return pltpu.CompilerParams(vmem_limit_bytes=VMEM_LIMIT_V7X)


def _dg(a, b, ca, cb):
    return lax.dot_general(a.astype(BF16), b.astype(BF16), (((ca,), (cb,)), ((), ())),
                           preferred_element_type=F32)


@jax.custom_vjp
def mm_nn(a, b):
    return _dg(a, b, 1, 0)


@jax.custom_vjp
def mm_nt(a, b):
    return _dg(a, b, 1, 1)


@jax.custom_vjp
def mm_tn(a, b):
    return _dg(a, b, 0, 0)


mm_nn.defvjp(lambda a, b: (mm_nn(a, b), (a, b)),
             lambda r, g: (mm_nt(g, r[1]).astype(r[0].dtype), mm_tn(r[0], g).astype(r[1].dtype)))
mm_nt.defvjp(lambda a, b: (mm_nt(a, b), (a, b)),
             lambda r, g: (mm_nn(g, r[1]).astype(r[0].dtype), mm_tn(g, r[0]).astype(r[1].dtype)))
mm_tn.defvjp(lambda a, b: (mm_tn(a, b), (a, b)),
             lambda r, g: (mm_nt(r[1], g).astype(r[0].dtype), mm_nn(r[0], g).astype(r[1].dtype)))


def _shift_rows(x, s, up):
    n = x.shape[0]
    r = lax.broadcasted_iota(jnp.int32, x.shape, 0)
    if up:
        return jnp.where(r < n - s, pltpu.roll(x, n - s, 0), 0.0)
    return jnp.where(r >= s, pltpu.roll(x, s, 0), 0.0)


@functools.partial(jax.custom_vjp, nondiff_argnums=(1,))
def shift_down(x, s):
    return _shift_rows(x, s, False)


shift_down.defvjp(lambda x, s: (_shift_rows(x, s, False), None), lambda s, _, g: (_shift_rows(g, s, True),))


@functools.partial(jax.custom_vjp, nondiff_argnums=(1,))
def shift_up(x, s):
    return _shift_rows(x, s, True)


shift_up.defvjp(lambda x, s: (_shift_rows(x, s, True), None), lambda s, _, g: (_shift_rows(g, s, False),))


def _swap32_impl(x):
    ax = x.ndim - 1
    lane = lax.broadcasted_iota(jnp.int32, x.shape, ax)
    return jnp.where(lane < 32, pltpu.roll(x, 96, ax), jnp.where(lane < 64, pltpu.roll(x, 32, ax), 0.0))


@jax.custom_vjp
def swap32(x):
    return _swap32_impl(x)


swap32.defvjp(lambda x: (_swap32_impl(x), None), lambda _, g: (_swap32_impl(g),))


def _rms(x, g):
    return x * lax.rsqrt(jnp.mean(x * x, axis=-1, keepdims=True) + EPS) * g


def _silu(x):
    return x * jax.nn.sigmoid(x)


def _row_ids(pid, n, shape, axis=0):
    return pid * n + lax.broadcasted_iota(jnp.int32, shape, axis)


class Arg:
    def __init__(self, arr, block, imap, diff=True, gdtype=F32):
        self.arr, self.block, self.imap, self.diff, self.gdtype = arr, block, imap, diff, gdtype


class Out:
    def __init__(self, shape, dtype, block, imap):
        self.shape, self.dtype, self.block, self.imap = shape, dtype, block, imap


def _free_axes(imap, grid):
    ng = len(grid)
    base = tuple(imap(*([0] * ng)))
    free = []
    for ax in range(ng):
        p = [0] * ng
        p[ax] = 1
        if grid[ax] > 1 and tuple(imap(*p)) == base:
            free.append(ax)
    walked = [ax for ax in range(ng) if grid[ax] > 1]
    assert free == walked[len(walked) - len(free):], "revisited blocks must be revisited on the innermost axes"
    return free


def _pallas(name, body, grid, in_specs, out_specs, out_shape, scratch, operands, rider=None, aliases=None):
    aliases = aliases or {}
    if isinstance(rider, (list, tuple)):
        riders = [r for r in rider if r is not None]
        rider = RiderGroup(riders) if riders else None
    if rider is None:
        return pl.pallas_call(body, grid=grid, in_specs=in_specs, out_specs=out_specs, out_shape=out_shape,
                              scratch_shapes=scratch, name=name, input_output_aliases=aliases,
                              compiler_params=_cparams())(*operands)
    n_in, n_out, n_sc = len(in_specs), len(out_specs), len(scratch)
    r_in, r_out = len(rider.operands), len(rider.out_shapes)

    def body_with_rider(*refs):
        ins, refs = refs[:n_in], refs[n_in:]
        r_ins, refs = refs[:r_in], refs[r_in:]
        outs, refs = refs[:n_out], refs[n_out:]
        r_outs, refs = refs[:r_out], refs[r_out:]
        sc, r_sc = refs[:n_sc], refs[n_sc:]
        pids = [pl.program_id(a) for a in range(len(grid))]
        first = functools.reduce(jnp.logical_and, [p == 0 for p in pids])
        last = functools.reduce(jnp.logical_and, [p == g - 1 for p, g in zip(pids, grid)])

        @pl.when(first)
        def _():
            rider.start(r_ins, r_outs, r_sc)

        if hasattr(rider, 'middle'):
            step = functools.reduce(lambda acc, pg: acc * pg[1] + pg[0], zip(pids, grid), 0)

            @pl.when(step == (math.prod(grid) * 5) // 6)
            def _():
                rider.middle(r_ins, r_outs, r_sc)

        body(*ins, *outs, *sc)

        @pl.when(last)
        def _():
            rider.finish(r_ins, r_outs, r_sc)

    res = pl.pallas_call(body_with_rider, grid=grid, in_specs=list(in_specs) + [ANY] * r_in,
                         out_specs=list(out_specs) + [ANY] * r_out, out_shape=list(out_shape) + rider.out_shapes,
                         scratch_shapes=list(scratch) + rider.scratch, name=name, input_output_aliases=aliases,
                         compiler_params=_cparams())(*operands, *rider.operands)
    rider.results = list(res[n_out:])
    return res[:n_out]


def stage_fwd(name, fn, grid, args, outs, state_shape=None, rider=None):
    n_in, n_out, ng = len(args), len(outs), len(grid)

    def body(*refs):
        pids = tuple(pl.program_id(a) for a in range(ng))
        vals = [r[...] for r in refs[:n_in]]
        o_refs = refs[n_in:n_in + n_out]
        if state_shape is None:
            res = fn(pids, *vals)
        else:
            sv_ref, st_ref = refs[n_in + n_out], refs[n_in + n_out + 1]

            @pl.when(pids[-1] == 0)
            def _():
                st_ref[...] = jnp.zeros(state_shape, F32)

            s = st_ref[...]
            sv_ref[...] = s
            res = fn(pids, *vals, s)
            st_ref[...] = res[-1]
            res = res[:-1]
        for r, v in zip(o_refs, res):
            r[...] = v.astype(r.dtype)

    in_specs = [pl.BlockSpec(a.block, a.imap) for a in args]
    out_specs = [pl.BlockSpec(o.block, o.imap) for o in outs]
    out_shape = [jax.ShapeDtypeStruct(o.shape, o.dtype) for o in outs]
    scratch = []
    if state_shape is not None:
        nz = len(state_shape)
        out_specs.append(pl.BlockSpec((None, None) + tuple(state_shape), lambda i, j: (i, j) + (0,) * nz))
        out_shape.append(jax.ShapeDtypeStruct(tuple(grid) + tuple(state_shape), F32))
        scratch = [pltpu.VMEM(state_shape, F32)]
    return _pallas(name, body, grid, in_specs, out_specs, out_shape, scratch, [a.arr for a in args], rider)


def stage_bwd(name, fn, grid, args, outs, cots, state_shape=None, states=None, rider=None):
    n_in, n_out, ng = len(args), len(outs), len(grid)
    nb = grid[-1]
    rev = state_shape is not None
    didx = [k for k, a in enumerate(args) if a.diff]
    frees = [_free_axes(args[k].imap, grid) for k in didx]

    def eff(p):
        return tuple(p[:-1]) + (nb - 1 - p[-1],) if rev else tuple(p)

    def wrap(imap):
        return lambda *p: imap(*eff(p))

    def body(*refs):
        pids = tuple(pl.program_id(a) for a in range(ng))
        e = eff(pids)
        vals = [r[...] for r in refs[:n_in]]
        cts = tuple(r[...].astype(F32) for r in refs[n_in:n_in + n_out])
        pos = n_in + n_out
        if rev:
            st_in_ref = refs[pos]
            pos += 1
        g_refs = refs[pos:pos + len(didx)]
        pos += len(didx)
        dvals = [vals[k].astype(F32) for k in didx]

        def f(*dv):
            full = list(vals)
            for k, v in zip(didx, dv[:len(didx)]):
                full[k] = v
            return tuple(fn(e, *full, *dv[len(didx):]))

        if rev:
            ds_ref = refs[pos]

            @pl.when(pids[-1] == 0)
            def _():
                ds_ref[...] = jnp.zeros(state_shape, F32)

            _, vjp = jax.vjp(f, *dvals, st_in_ref[...])
            grads = vjp(cts + (ds_ref[...],))
            ds_ref[...] = grads[-1]
            grads = grads[:-1]
        else:
            _, vjp = jax.vjp(f, *dvals)
            grads = vjp(cts)
        for gref, g, free in zip(g_refs, grads, frees):
            g = g.astype(F32)
            if not free:
                gref[...] = g.astype(gref.dtype)
            else:
                first = functools.reduce(jnp.logical_and, [pids[ax] == 0 for ax in free])

                @pl.when(first)
                def _():
                    gref[...] = g

                @pl.when(jnp.logical_not(first))
                def _():
                    gref[...] += g

    in_specs = [pl.BlockSpec(a.block, wrap(a.imap)) for a in args]
    in_specs += [pl.BlockSpec(o.block, wrap(o.imap)) for o in outs]
    operands = [a.arr for a in args] + list(cots)
    scratch = []
    if rev:
        nz = len(state_shape)
        in_specs.append(pl.BlockSpec((None, None) + tuple(state_shape), lambda i, j: (i, nb - 1 - j) + (0,) * nz))
        operands.append(states)
        scratch = [pltpu.VMEM(state_shape, F32)]
    out_specs = [pl.BlockSpec(args[k].block, wrap(args[k].imap)) for k in didx]
    assert all(args[k].gdtype == F32 or not free for k, free in zip(didx, frees))
    out_shape = [jax.ShapeDtypeStruct(args[k].arr.shape, args[k].gdtype) for k in didx]
    return _pallas(name, body, grid, in_specs, out_specs, out_shape, scratch, operands, rider)


def _divisors(n, cands):
    return [c for c in cands if n % c == 0] or [n]


def _nbytes(dt):
    return jnp.dtype(dt).itemsize


def matmul(name, a, b, mode, out_dtype=F32, res=None, mask=False, res_mask=True, window=None, into=None):
    sa, sb, so = _nbytes(a.dtype), _nbytes(b.dtype), _nbytes(out_dtype)
    off, width = (window[0], window[1]) if window is not None else (0, None)
    if mode in ('nn', 'nt'):
        M, K = a.shape
        N = (width or b.shape[1]) if mode == 'nn' else b.shape[0]
        assert mode == 'nn' or width is None or width == K
        best = None
        for tm in _divisors(M, (1408, 1056, 768, 384, 128)):
            for tn in _divisors(N, (1408, 1024, 768, 512, 384, 256, 128)):
                est = 2 * (tm * K * sa + tn * K * sb + tm * tn * (so + (4 if res is not None else 0)))
                if est <= MM_VMEM_BUDGET and (best is None or tm * tn > best[0] * best[1]):
                    best = (tm, tn)
        tm, tn = best
        grid = (M // tm, N // tn)

        def body(*refs):
            a_ref, b_ref = refs[0], refs[1]
            o_ref = refs[-1]
            x = a_ref[...]
            rows = _row_ids(pl.program_id(0), tm, (tm, 1))
            if mask:
                x = jnp.where(rows >= PAD, x, jnp.zeros_like(x))
            acc = _dg(x, b_ref[...], 1, 0 if mode == 'nn' else 1)
            if res is not None:
                acc = refs[2][...] + (jnp.where(rows >= PAD, acc, 0.0) if res_mask else acc)
            o_ref[...] = acc.astype(o_ref.dtype)

        assert off % (tn if mode == 'nn' else K) == 0
        cb, kb = off // tn, off // K
        in_specs = [pl.BlockSpec((tm, K), lambda i, j: (i, 0)),
                    pl.BlockSpec((K, tn), lambda i, j: (0, j + cb)) if mode == 'nn' else
                    pl.BlockSpec((tn, K), lambda i, j: (j, kb))]
        ops = [a, b]
        if res is not None:
            in_specs.append(pl.BlockSpec((tm, tn), lambda i, j: (i, j)))
            ops.append(res)
        return pl.pallas_call(body, grid=grid, in_specs=in_specs,
                              out_specs=pl.BlockSpec((tm, tn), lambda i, j: (i, j)),
                              out_shape=jax.ShapeDtypeStruct((M, N), out_dtype), name=name,
                              compiler_params=_cparams())(*ops)
    assert mode == 'tn' and res is None
    M, K = a.shape
    N = b.shape[1]
    best = None
    for tk in _divisors(K, (1408, 1024, 768, 512, 384, 256, 128)):
        for tn in _divisors(N, (1408, 1024, 768, 512, 384, 256, 128)):
            est = 2 * (M * tk * sa + M * tn * sb + tk * tn * so)
            if est <= MM_VMEM_BUDGET and (best is None or tk * tn > best[0] * best[1]):
                best = (tk, tn)
    tk, tn = best

    def body_t(*refs):
        a_ref, b_ref, o_ref = refs[0], refs[1], refs[-1]
        y = b_ref[...]
        if mask:
            rows = lax.broadcasted_iota(jnp.int32, (M, 1), 0)
            y = jnp.where(rows >= PAD, y, jnp.zeros_like(y))
        o_ref[...] = _dg(a_ref[...], y, 0, 0).astype(o_ref.dtype)

    assert off % tn == 0
    cb = off // tn
    total = window[2] if window is not None else N
    in_specs = [pl.BlockSpec((M, tk), lambda i, j: (0, i)), pl.BlockSpec((M, tn), lambda i, j: (0, j))]
    ops, alias = [a, b], {}
    if into is not None:
        in_specs.append(ANY)
        ops.append(into)
        alias = {2: 0}
    return pl.pallas_call(body_t, grid=(K // tk, N // tn), in_specs=in_specs,
                          out_specs=pl.BlockSpec((tk, tn), lambda i, j: (i, j + cb)),
                          out_shape=jax.ShapeDtypeStruct((K, total), out_dtype), name=name,
                          input_output_aliases=alias, compiler_params=_cparams())(*ops)


def matmul_nt_halves(name, a_lo, a_hi, b):
    M, K = a_lo.shape
    N = b.shape[0]
    tm = [t for t in (384, 128) if M % t == 0][0]

    def body(lo_ref, hi_ref, b_lo_ref, b_hi_ref, o_ref):
        o_ref[...] = _dg(lo_ref[...], b_lo_ref[...], 1, 1) + _dg(hi_ref[...], b_hi_ref[...], 1, 1)

    rows = pl.BlockSpec((tm, K), lambda i: (i, 0))
    return pl.pallas_call(body, grid=(M // tm,),
                          in_specs=[rows, rows, pl.BlockSpec((N, K), lambda i: (0, 0)), pl.BlockSpec((N, K), lambda i: (0, 1))],
                          out_specs=pl.BlockSpec((tm, N), lambda i: (i, 0)),
                          out_shape=jax.ShapeDtypeStruct((M, N), F32), name=name, compiler_params=_cparams())(a_lo, a_hi, b, b)


def linear_bwd(name, act, w, dy, mask=False, da_dtype=F32):
    return (matmul(name + "_da", dy, w, 'nt', out_dtype=da_dtype, mask=mask),
            matmul(name + "_dw", act, dy, 'tn', out_dtype=BF16, mask=mask))


def _row_tile(T):
    return _divisors(T, (384, 128))[0]


def _rows(arr, tm, gdtype=F32):
    return Arg(arr, (tm, arr.shape[1]), lambda i: (i, 0), gdtype=gdtype)


def _const(arr, diff=True):
    return Arg(arr, arr.shape, lambda *p: (0,) * arr.ndim, diff)


def _norm_fn(pids, h, g):
    return (_rms(h, g),)


def _norm_bwd_fn(pids, h, g):
    return (_rms(h, g), h)


def norm_fwd(name, h, g, dtype):
    T = h.shape[0]
    tm = _row_tile(T)
    return stage_fwd(name, _norm_fn, (T // tm,), [_rows(h, tm), _const(g)],
                     [Out((T, D), dtype, (tm, D), lambda i: (i, 0))])[0]


def norm_bwd(name, h, g, da, dh):
    T = h.shape[0]
    tm = _row_tile(T)
    o = Out((T, D), F32, (tm, D), lambda i: (i, 0))
    return stage_bwd(name, _norm_bwd_fn, (T // tm,), [_rows(h, tm), _const(g)], [o, o], [da, dh])


def _causal_conv3(u, cw, cb):
    return cw[2:3] * u + cw[1:2] * shift_down(u, 1) + cw[0:1] * shift_down(u, 2) + cb


def _ffn_act_fn(pids, ug, uv, cwg, cwv, cbg, cbv):
    return (_silu(_causal_conv3(ug, cwg, cbg)) * _causal_conv3(uv, cwv, cbv),)


def _ffn_act_args(ug, uv, cw, cb):
    T = ug.shape[0]
    col = lambda j: (0, j)
    args = [Arg(ug, (T, 128), col, gdtype=BF16), Arg(uv, (T, 128), col, gdtype=BF16),
            Arg(cw[:, :FFN_F], (3, 128), col), Arg(cw[:, FFN_F:], (3, 128), col),
            Arg(cb[:, :FFN_F], (1, 128), col), Arg(cb[:, FFN_F:], (1, 128), col)]
    outs = [Out((T, FFN_F), BF16, (T, 128), col)]
    return (FFN_F // 128,), args, outs


def ffn_layer(i, h, g, w_up, cw, cb, w_down):
    gate_w, val_w = (0, FFN_F, 2 * FFN_F), (FFN_F, FFN_F, 2 * FFN_F)
    b = norm_fwd(f"ffn{i}_norm", h, g, BF16)
    ug = matmul(f"ffn{i}_up_g", b, w_up, 'nn', out_dtype=BF16, window=gate_w)
    uv = matmul(f"ffn{i}_up_v", b, w_up, 'nn', out_dtype=BF16, window=val_w)
    grid, args, outs = _ffn_act_args(ug, uv, cw, cb)
    p = stage_fwd(f"ffn{i}_act", _ffn_act_fn, grid, args, outs)[0]
    h_new = matmul(f"ffn{i}_down", p, w_down, 'nn', res=h)

    def bwd(dh):
        dp, dwd = linear_bwd(f"ffn{i}_down_b", p, w_down, dh, mask=True, da_dtype=BF16)
        dug, duv, dcwg, dcwv, dcbg, dcbv = stage_bwd(f"ffn{i}_act_b", _ffn_act_fn, grid, args, outs, [dp])
        db = matmul_nt_halves(f"ffn{i}_up_b_da", dug, duv, w_up)
        dwu = matmul(f"ffn{i}_up_b_dw_g", b, dug, 'tn', out_dtype=BF16, window=gate_w)
        dwu = matmul(f"ffn{i}_up_b_dw_v", b, duv, 'tn', out_dtype=BF16, window=val_w, into=dwu)
        dh2, dg = norm_bwd(f"ffn{i}_norm_b", h, g, db, dh)
        return dh2, dict(g=dg, w_up=dwu, cw=jnp.concatenate([dcwg, dcwv], axis=1),
                         cb=jnp.concatenate([dcbg, dcbv], axis=1), w_down=dwd)

    return h_new, bwd


def _mla_latent_fn(pids, down, gcq, gckv):
    cq = _rms(down[:, :MLA_QL], gcq)
    ckv = _rms(down[:, MLA_QL:MLA_QL + MLA_KVL], gckv)
    return cq, ckv, down[:, MLA_QL + MLA_KVL:]


def _rope64(x, cos, sin_signed):
    return x * cos + swap32(x) * sin_signed


def _mla_heads_fn(pids, qraw, kv, kpe, gqn, gqr, gkn, gkr, cos, sin_signed):
    qn, qr = qraw[:, :128], qraw[:, 128:]
    rq = lax.rsqrt((jnp.sum(qn * qn, -1, keepdims=True) + jnp.sum(qr * qr, -1, keepdims=True)) / MLA_QK + EPS)
    q = jnp.concatenate([qn * rq * gqn, _rope64(qr * rq * gqr, cos, sin_signed)], axis=1)
    kn, v = kv[:, :128], kv[:, 128:]
    rk = lax.rsqrt((jnp.sum(kn * kn, -1, keepdims=True) + jnp.sum(kpe * kpe, -1, keepdims=True)) / MLA_QK + EPS)
    k = jnp.concatenate([kn * rk * gkn, _rope64(kpe * rk * gkr, cos, sin_signed)], axis=1)
    return q, k, v


def _chunk_id(r):
    return jnp.where(r < LEAD, 0, 1 + lax.shift_right_arithmetic(r - LEAD, 6))


ATT_T = 384
ATT_SCALE = MLA_QK ** -0.5


def _att_mask(s, qb, kb):
    qrow = _row_ids(qb, ATT_T, (ATT_T, 1))
    krow = _row_ids(kb, ATT_T, (1, ATT_T), axis=1)
    ok = jnp.logical_and(_chunk_id(krow) <= _chunk_id(qrow), krow >= PAD)
    return jnp.where(ok, s, NEG_INF)


def _att_scores(q, k_ref, qb, kb, masked):
    ks = k_ref[pl.ds(pl.multiple_of(kb * ATT_T, ATT_T), ATT_T), :]
    s = _dg(q, ks, 1, 1) * ATT_SCALE
    return (_att_mask(s, qb, kb) if masked else s), ks


def _att_key_loop(j, step, init):
    carry = step(0, init, True)
    n_mid = jnp.maximum(j - 1, 0)
    carry = lax.fori_loop(0, n_mid // 2, lambda i, c: step(2 * i + 2, step(2 * i + 1, c, False), False), carry)
    carry = lax.cond(n_mid % 2 == 1, lambda c: step(j - 1, c, False), lambda c: c, carry)
    return lax.cond(j > 0, lambda c: step(j, c, True), lambda c: c, carry)


def _att_rows(ref, b):
    return ref[pl.ds(pl.multiple_of(b * ATT_T, ATT_T), ATT_T), :]


def attention_fwd(q, k, v, rider=None):
    H, T, _ = q.shape
    nq = T // ATT_T

    def body(q_ref, k_ref, v_ref, o_ref, lse_ref):
        j = pl.program_id(1)
        qv = q_ref[...]

        def step(kb, carry, masked):
            m, l, acc = carry
            s, _ = _att_scores(qv, k_ref, j, kb, masked)
            m_new = jnp.maximum(m, jnp.max(s, axis=-1, keepdims=True))
            p = jnp.exp(s - m_new)
            alpha = jnp.exp(m - m_new)
            return (m_new, alpha * l + jnp.sum(p, axis=-1, keepdims=True),
                    alpha * acc + _dg(p, _att_rows(v_ref, kb), 1, 0))

        init = (jnp.full((ATT_T, 1), NEG_INF, F32), jnp.zeros((ATT_T, 1), F32), jnp.zeros((ATT_T, MLA_V), F32))
        m, l, acc = _att_key_loop(j, step, init)
        o_ref[...] = acc / l
        lse_ref[...] = m + jnp.log(l)

    return _pallas("mla_attn", body, (H, nq),
                   [pl.BlockSpec((None, ATT_T, 256), lambda hh, j: (hh, j, 0)),
                    pl.BlockSpec((None, T, 256), lambda hh, j: (hh, 0, 0)),
                    pl.BlockSpec((None, T, MLA_V), lambda hh, j: (hh, 0, 0))],
                   [pl.BlockSpec((ATT_T, MLA_V), lambda hh, j: (j, hh)),
                    pl.BlockSpec((None, ATT_T, 1), lambda hh, j: (hh, j, 0))],
                   [jax.ShapeDtypeStruct((T, H * MLA_V), F32), jax.ShapeDtypeStruct((H, T, 1), F32)], [],
                   [q, k, v], rider)


def attention_bwd(q, k, v, o, lse, do, rider=None, rider_dq=None):
    H, T, _ = q.shape
    nq = T // ATT_T

    def dq_body(q_ref, k_ref, v_ref, o_ref, do_ref, lse_ref, dq_ref, delta_ref):
        j = pl.program_id(1)
        qv, dov, lsev = q_ref[...], do_ref[...], lse_ref[...]
        delta = jnp.sum(dov * o_ref[...], axis=-1, keepdims=True)
        delta_ref[...] = delta

        def step(kb, acc, masked):
            s, ks = _att_scores(qv, k_ref, j, kb, masked)
            p = jnp.exp(s - lsev)
            ds = p * (_dg(dov, _att_rows(v_ref, kb), 1, 1) - delta) * ATT_SCALE
            return acc + _dg(ds, ks, 1, 0)

        dq_ref[...] = _att_key_loop(j, step, jnp.zeros((ATT_T, 256), F32))

    q_blk = pl.BlockSpec((None, ATT_T, 256), lambda hh, j: (hh, j, 0))
    k_all = pl.BlockSpec((None, T, 256), lambda hh, j: (hh, 0, 0))
    v_all = pl.BlockSpec((None, T, MLA_V), lambda hh, j: (hh, 0, 0))
    o_blk = pl.BlockSpec((ATT_T, MLA_V), lambda hh, j: (j, hh))
    col_blk = pl.BlockSpec((None, ATT_T, 1), lambda hh, j: (hh, j, 0))
    dq, delta = _pallas("mla_attn_dq", dq_body, (H, nq), [q_blk, k_all, v_all, o_blk, o_blk, col_blk],
                        [q_blk, col_blk],
                        [jax.ShapeDtypeStruct((H, T, 256), F32), jax.ShapeDtypeStruct((H, T, 1), F32)], [],
                        [q, k, v, o, do, lse], rider_dq)

    def dkv_body(q_ref, k_ref, v_ref, do_ref, lse_ref, delta_ref, dk_ref, dv_ref):
        kb = pl.program_id(1)
        kv = k_ref[...]
        vv = v_ref[...]

        def step(qb, carry, masked):
            dk, dv = carry
            qv, dov = _att_rows(q_ref, qb), _att_rows(do_ref, qb)
            s = _dg(qv, kv, 1, 1) * ATT_SCALE
            if masked:
                s = _att_mask(s, qb, kb)
            p = jnp.exp(s - _att_rows(lse_ref, qb))
            ds = p * (_dg(dov, vv, 1, 1) - _att_rows(delta_ref, qb)) * ATT_SCALE
            return dk + _dg(ds, qv, 0, 0), dv + _dg(p, dov, 0, 0)

        carry = step(kb, (jnp.zeros((ATT_T, 256), F32), jnp.zeros((ATT_T, MLA_V), F32)), True)

        def later_blocks(c, masked):
            n = nq - 1 - kb
            c = lax.fori_loop(0, n // 2,
                              lambda i, cc: step(kb + 2 + 2 * i, step(kb + 1 + 2 * i, cc, masked), masked), c)
            return lax.cond(n % 2 == 1, lambda cc: step(nq - 1, cc, masked), lambda cc: cc, c)

        dk, dv = lax.cond(kb == 0, lambda c: later_blocks(c, True), lambda c: later_blocks(c, False), carry)
        dk_ref[...] = dk
        dv_ref[...] = dv

    q_all = pl.BlockSpec((None, T, 256), lambda hh, j: (hh, 0, 0))
    v_blk = pl.BlockSpec((None, ATT_T, MLA_V), lambda hh, j: (hh, j, 0))
    do_all = pl.BlockSpec((T, MLA_V), lambda hh, j: (0, hh))
    col_all = pl.BlockSpec((None, T, 1), lambda hh, j: (hh, 0, 0))
    dk, dv = _pallas("mla_attn_dkv", dkv_body, (H, nq), [q_all, q_blk, v_blk, do_all, col_all, col_all],
                     [q_blk, v_blk],
                     [jax.ShapeDtypeStruct((H, T, 256), F32), jax.ShapeDtypeStruct((H, T, MLA_V), F32)], [],
                     [q, k, v, do, lse, delta], rider)
    return dq, dk, dv


def mla_mixer(h, g, w, tabs, rider=None):
    T = h.shape[0]
    tm = _row_tile(T)
    nt = T // tm
    a = norm_fwd("mla_norm", h, g, BF16)
    down = matmul("mla_down", a, w['w_down'], 'nn')
    lat_args = [_rows(down, tm, BF16), _const(w['gcq']), _const(w['gckv'])]
    lat_outs = [Out((T, MLA_QL), BF16, (tm, MLA_QL), lambda i: (i, 0)),
                Out((T, MLA_KVL), BF16, (tm, MLA_KVL), lambda i: (i, 0)),
                Out((T, 128), F32, (tm, 128), lambda i: (i, 0))]
    cq, ckv, kpe = stage_fwd("mla_latent", _mla_latent_fn, (nt,), lat_args, lat_outs)
    qraw = matmul("mla_uq", cq, w['w_uq'], 'nn')
    kv = matmul("mla_ukv", ckv, w['w_ukv'], 'nn')
    th = _divisors(T, (1408, 384, 128))[0]
    hd_grid = (T // th, MLA_H)
    hd_args = [Arg(qraw, (th, 256), lambda i, hh: (i, hh), gdtype=BF16),
               Arg(kv, (th, 256), lambda i, hh: (i, hh), gdtype=BF16),
               Arg(kpe, (th, 128), lambda i, hh: (i, 0)),
               _const(w['gqn']), _const(w['gqr']), _const(w['gkn']), _const(w['gkr']),
               Arg(tabs['cos_a'], (th, 128), lambda i, hh: (i, 0), False),
               Arg(tabs['sin_a'], (th, 128), lambda i, hh: (i, 0), False)]
    hd_outs = [Out((MLA_H, T, 256), BF16, (None, th, 256), lambda i, hh: (hh, i, 0)),
               Out((MLA_H, T, 256), BF16, (None, th, 256), lambda i, hh: (hh, i, 0)),
               Out((MLA_H, T, 128), BF16, (None, th, 128), lambda i, hh: (hh, i, 0))]
    q, k, v = stage_fwd("mla_heads", _mla_heads_fn, hd_grid, hd_args, hd_outs)
    o, lse = attention_fwd(q, k, v, rider=rider)
    h_new = matmul("mla_o", o, w['w_o'], 'nn', res=h)

    def bwd(dh, rider_dkv=None, rider_dq=None, rider_heads=None):
        do, dwo = linear_bwd("mla_o_b", o, w['w_o'], dh, mask=True)
        dq, dk, dv = attention_bwd(q, k, v, o, lse, do, rider=rider_dkv and rider_dkv(dwo),
                                   rider_dq=rider_dq and rider_dq())
        dqraw, dkv, dkpe, dgqn, dgqr, dgkn, dgkr = stage_bwd("mla_heads_b", _mla_heads_fn, hd_grid, hd_args,
                                                             hd_outs, [dq, dk, dv],
                                                             rider=rider_heads and rider_heads())
        dcq, dwuq = linear_bwd("mla_uq_b", cq, w['w_uq'], dqraw)
        dckv, dwukv = linear_bwd("mla_ukv_b", ckv, w['w_ukv'], dkv)
        ddown, dgcq, dgckv = stage_bwd("mla_latent_b", _mla_latent_fn, (nt,), lat_args, lat_outs, [dcq, dckv, dkpe])
        da, dwdown = linear_bwd("mla_down_b", a, w['w_down'], ddown)
        dh2, dg = norm_bwd("mla_norm_b", h, g, da, dh)
        return dh2, dict(g=dg, w_down=dwdown, gcq=dgcq, gckv=dgckv, w_uq=dwuq, w_ukv=dwukv, gqn=dgqn, gqr=dgqr,
                         gkn=dgkn, gkr=dgkr, w_o=dwo)

    return h_new, bwd


HG_R = 384


def _hgrn_fn(pids, z, lb, go, st):
    outs = []
    for lo in range(0, z.shape[0], 128):
        o, st = _hgrn_block(z[lo:lo + 128], lb, go, st)
        outs.append(o)
    return jnp.concatenate(outs, axis=0), st


def _hgrn_block(z, lb, go, st):
    R = z.shape[0]
    zq, zf, zi, zg = z[:, :128], z[:, 128:256], z[:, 256:384], z[:, 384:]
    assert R == 128
    q = _silu(zq)
    fg = lb + (1.0 - lb) * jax.nn.sigmoid(zf)
    logf = jnp.log(fg)
    k = 1.0 - fg
    row = lax.broadcasted_iota(jnp.int32, logf.shape, 0)
    pos = row & (HG_C - 1)
    cum, rev = logf, logf
    for d in (1, 2, 4, 8):
        cum = cum + jnp.where(pos >= d, shift_down(cum, d), 0.0)
        rev = rev + jnp.where(pos < HG_C - d, shift_up(rev, d), 0.0)
    cums, tots = [cum], [cum + rev - logf]
    for s in (16, 32, 64):
        odd = (row & s) != 0
        before = shift_down(tots[-1], s)
        cums.append(cums[-1] + jnp.where(odd, before, 0.0))
        tots.append(tots[-1] + jnp.where(odd, before, shift_up(tots[-1], s)))
    t = lax.broadcasted_iota(jnp.int32, (R, R), 0)
    j = lax.broadcasted_iota(jnp.int32, (R, R), 1)
    sh = lax.shift_right_arithmetic
    a = jnp.where(jnp.logical_and(sh(t, 4) == sh(j, 4), j <= t), mm_nt(q * jnp.exp(cum), k * jnp.exp(-cum)), 0.0)
    for n, s in enumerate((16, 32, 64)):
        m = jnp.logical_and(sh(t, 5 + n) == sh(j, 5 + n), jnp.logical_and((t & s) != 0, (j & s) == 0))
        a = a + jnp.where(m, mm_nt(q * jnp.exp(cums[n]), k * jnp.exp(tots[n] - cums[n])), 0.0)
    o = mm_nn(a, zi) + mm_nt(q * jnp.exp(cums[3]), st)
    st = st * jnp.exp(tots[3][0:1, :]) + mm_tn(zi, k * jnp.exp(tots[3] - cums[3]))
    return _rms(o, go) * _silu(zg), st


def hgrn_mixer(h, g, w, rider=None):
    T = h.shape[0]
    a = norm_fwd("hgrn_norm", h, g, BF16)
    z = matmul("hgrn_in", a, w['w_in'], 'nn')
    grid = (HG_H, T // HG_R)
    args = [Arg(z, (HG_R, 512), lambda hh, j: (j, hh), gdtype=BF16), Arg(w['lb'], (1, 128), lambda hh, j: (0, hh)),
            _const(w['go'])]
    outs = [Out((T, D), BF16, (HG_R, 128), lambda hh, j: (j, hh))]
    o, states = stage_fwd("hgrn_gla", _hgrn_fn, grid, args, outs, state_shape=(HG_D, HG_D), rider=rider)
    h_new = matmul("hgrn_o", o, w['w_o'], 'nn', res=h)

    def bwd(dh, rider=None):
        do, dwo = linear_bwd("hgrn_o_b", o, w['w_o'], dh, mask=True)
        dz, dlb, dgo = stage_bwd("hgrn_gla_b", _hgrn_fn, grid, args, outs, [do], state_shape=(HG_D, HG_D),
                                 states=states, rider=rider)
        da, dwin = linear_bwd("hgrn_in_b", a, w['w_in'], dz)
        dh2, dg = norm_bwd("hgrn_norm_b", h, g, da, dh)
        return dh2, dict(g=dg, w_in=dwin, lb=dlb, go=dgo, w_o=dwo)

    return h_new, bwd


S5_R = 384
S5_W = 512
S5_SLABS = D // 128


def _cmul(ar, ai, br, bi):
    return ar * br - ai * bi, ar * bi + ai * br


def _s5_scan(br, bi, tab, cr, ci, reverse):
    R, W = br.shape
    G = R // 8
    xr, xi = br.reshape(G, 8, W), bi.reshape(G, 8, W)
    for n, d in enumerate((1, 2, 4)):
        sh = (8 - d) if reverse else d
        mr, mi = _cmul(tab[2 * n][None], tab[2 * n + 1][None], pltpu.roll(xr, sh, 1), pltpu.roll(xi, sh, 1))
        xr, xi = xr + mr, xi + mi
    pr, pi = tab[6], tab[7]
    edge = 0 if reverse else 7
    out_r, out_i = [None] * G, [None] * G
    for g in (range(G - 1, -1, -1) if reverse else range(G)):
        ar, ai = _cmul(pr, pi, cr, ci)
        gr, gi = xr[g] + ar, xi[g] + ai
        cr, ci = gr[edge:edge + 1], gi[edge:edge + 1]
        out_r[g], out_i[g] = gr, gi
    return jnp.concatenate(out_r, axis=0), jnp.concatenate(out_i, axis=0), cr, ci


def s5_scan_fwd(a, bb, cb, tab, rider=None):
    T = a.shape[0]
    nb = T // S5_R

    def body(a_ref, bb_ref, cb_ref, tab_ref, y_ref, xs_ref, c_ref):
        @pl.when(pl.program_id(1) == 0)
        def _():
            c_ref[...] = jnp.zeros(c_ref.shape, F32)

        bu = _dg(a_ref[...], bb_ref[...], 1, 0)
        t = tab_ref[...]
        xr, xi, cr, ci = _s5_scan(bu[:, :S5_W], bu[:, S5_W:], t, c_ref[0:1, :S5_W], c_ref[0:1, S5_W:], False)
        x = jnp.concatenate([xr, xi], axis=1)
        xs_ref[...] = x
        y_ref[...] = _dg(x, cb_ref[...], 1, 0)
        c_ref[0:1, :] = jnp.concatenate([cr, ci], axis=1)

    return _pallas(
        "s5_scan", body, (S5_SLABS, nb),
        [pl.BlockSpec((S5_R, 128), lambda j, i: (i, j)),
         pl.BlockSpec((None, 128, 2 * S5_W), lambda j, i: (j, 0, 0)),
         pl.BlockSpec((None, 2 * S5_W, 128), lambda j, i: (j, 0, 0)),
         pl.BlockSpec((None, 10, 8, S5_W), lambda j, i: (j, 0, 0, 0))],
        [pl.BlockSpec((S5_R, 128), lambda j, i: (i, j)),
         pl.BlockSpec((None, S5_R, 2 * S5_W), lambda j, i: (j, i, 0))],
        [jax.ShapeDtypeStruct((T, D), F32), jax.ShapeDtypeStruct((S5_SLABS, T, 2 * S5_W), F32)],
        [pltpu.VMEM((8, 2 * S5_W), F32)], [a, bb, cb, tab], rider)


def s5_scan_bwd(a, bb, cb, tab_rev, xs, dy, rider=None):
    T = a.shape[0]
    nb = T // S5_R
    rg = S5_R // 8

    def body(a_ref, dy_ref, xs_ref, xp_ref, bb_ref, cb_ref, tab_ref, da_ref, dbb_ref, dcb_ref, dab_ref, c_ref):
        i = pl.program_id(1)

        @pl.when(i == 0)
        def _():
            c_ref[...] = jnp.zeros(c_ref.shape, F32)

        dy_v = dy_ref[...]
        x = xs_ref[...]
        dxo = _dg(dy_v, cb_ref[...], 1, 1)
        gr, gi, cr, ci = _s5_scan(dxo[:, :S5_W], dxo[:, S5_W:], tab_ref[...], c_ref[0:1, :S5_W], c_ref[0:1, S5_W:], True)
        c_ref[0:1, :] = jnp.concatenate([cr, ci], axis=1)
        g = jnp.concatenate([gr, gi], axis=1)
        da_ref[...] = _dg(g, bb_ref[...], 1, 1)
        dbb = _dg(a_ref[...], g, 0, 0)
        dcb = _dg(x, dy_v, 0, 0)
        first_tile = i == nb - 1
        prev_last = jnp.where(first_tile, 0.0, xp_ref[7:8, :])
        rows = lax.broadcasted_iota(jnp.int32, x.shape, 0)
        xp = jnp.where(rows == 0, prev_last, pltpu.roll(x, 1, 0))
        xpr, xpi = xp[:, :S5_W], xp[:, S5_W:]
        dar = (gr * xpr + gi * xpi).reshape(rg, 8, S5_W).sum(axis=0)
        dai = (gi * xpr - gr * xpi).reshape(rg, 8, S5_W).sum(axis=0)
        dab = jnp.concatenate([dar, dai], axis=1)

        @pl.when(i == 0)
        def _():
            dbb_ref[...] = dbb
            dcb_ref[...] = dcb
            dab_ref[...] = dab

        @pl.when(i != 0)
        def _():
            dbb_ref[...] += dbb
            dcb_ref[...] += dcb
            dab_ref[...] += dab

    def prev_rows(j, i):
        return (j, jnp.maximum((nb - 1 - i) * rg - 1, 0), 0)

    return _pallas(
        "s5_scan_b", body, (S5_SLABS, nb),
        [pl.BlockSpec((S5_R, 128), lambda j, i: (nb - 1 - i, j)),
         pl.BlockSpec((S5_R, 128), lambda j, i: (nb - 1 - i, j)),
         pl.BlockSpec((None, S5_R, 2 * S5_W), lambda j, i: (j, nb - 1 - i, 0)),
         pl.BlockSpec((None, 8, 2 * S5_W), prev_rows),
         pl.BlockSpec((None, 128, 2 * S5_W), lambda j, i: (j, 0, 0)),
         pl.BlockSpec((None, 2 * S5_W, 128), lambda j, i: (j, 0, 0)),
         pl.BlockSpec((None, 10, 8, S5_W), lambda j, i: (j, 0, 0, 0))],
        [pl.BlockSpec((S5_R, 128), lambda j, i: (nb - 1 - i, j)),
         pl.BlockSpec((None, 128, 2 * S5_W), lambda j, i: (j, 0, 0)),
         pl.BlockSpec((None, 2 * S5_W, 128), lambda j, i: (j, 0, 0)),
         pl.BlockSpec((None, 8, 2 * S5_W), lambda j, i: (j, 0, 0))],
        [jax.ShapeDtypeStruct((T, D), F32), jax.ShapeDtypeStruct((S5_SLABS, 128, 2 * S5_W), F32),
         jax.ShapeDtypeStruct((S5_SLABS, 2 * S5_W, 128), F32), jax.ShapeDtypeStruct((S5_SLABS, 8, 2 * S5_W), F32)],
        [pltpu.VMEM((8, 2 * S5_W), F32)], [a, dy, xs, xs, bb, cb, tab_rev], rider)


def _s5_discretise(lam_re, lam_im, log_dt, b_re, b_im, c_re, c_im):
    dt = jnp.exp(log_dt)[:, None]
    mag = jnp.exp(lam_re * dt)
    abar_re = mag * jnp.cos(lam_im * dt)
    abar_im = mag * jnp.sin(lam_im * dt)
    den = lam_re * lam_re + lam_im * lam_im
    zoh_re = ((abar_re - 1.0) * lam_re + abar_im * lam_im) / den
    zoh_im = (abar_im * lam_re - (abar_re - 1.0) * lam_im) / den
    bbar_re = zoh_re[..., None] * b_re - zoh_im[..., None] * b_im
    bbar_im = zoh_re[..., None] * b_im + zoh_im[..., None] * b_re
    eye = jnp.eye(8, dtype=F32)

    def in_map(bbar):
        t = bbar.reshape(8, 8, S5_P, S5_K).transpose(0, 1, 3, 2)
        return (t[:, :, :, None, :] * eye[None, :, None, :, None]).reshape(8, 8 * S5_K, 8 * S5_P)

    def out_map(c):
        t = c.reshape(8, 8, S5_K, S5_P).transpose(0, 1, 3, 2)
        return (t[:, :, :, None, :] * eye[None, :, None, :, None]).reshape(8, 8 * S5_P, 8 * S5_K)

    bb = jnp.concatenate([in_map(bbar_re), in_map(bbar_im)], axis=2)
    cb = jnp.concatenate([out_map(c_re), -out_map(c_im)], axis=1)
    return bb, cb, abar_re.reshape(8, S5_W), abar_im.reshape(8, S5_W)


def _s5_tables(ar, ai, reverse):
    if reverse:
        ai = -ai
    pw = [(jnp.ones_like(ar), jnp.zeros_like(ar))]
    for _ in range(8):
        pw.append(_cmul(pw[-1][0], pw[-1][1], ar, ai))
    r = jnp.arange(8)[None, :, None]
    rows = []
    for d in (1, 2, 4):
        keep = (r <= 7 - d) if reverse else (r >= d)
        rows += [jnp.where(keep, pw[d][0][:, None, :], 0.0), jnp.where(keep, pw[d][1][:, None, :], 0.0)]
    order = [8 - k for k in range(8)] if reverse else [k + 1 for k in range(8)]
    rows += [jnp.stack([pw[n][0] for n in order], axis=1), jnp.stack([pw[n][1] for n in order], axis=1)]
    rows += [jnp.broadcast_to(pw[8][0][:, None, :], (8, 8, S5_W)), jnp.broadcast_to(pw[8][1][:, None, :], (8, 8, S5_W))]
    return jnp.stack(rows, axis=1)


def _s5_act_fn(pids, yc, a, dskip):
    return (jax.nn.gelu(yc + dskip * a),)


def _make_glu_res_fn(tm):
    def glu_res_fn(pids, zz, h):
        rows = _row_ids(pids[0], tm, (tm, 1))
        return (h + jnp.where(rows >= PAD, zz[:, :D] * jax.nn.sigmoid(zz[:, D:]), 0.0),)
    return glu_res_fn


def s5_mixer(h, g, w, rider=None):
    T = h.shape[0]
    tm = _row_tile(T)
    nt = T // tm
    a = norm_fwd("s5_norm", h, g, F32)
    ssm = [w[n] for n in ('lam_re', 'lam_im', 'log_dt', 'b_re', 'b_im', 'c_re', 'c_im')]
    (bb, cb, ar, ai), disc_vjp = jax.vjp(_s5_discretise, *ssm)
    yc, xs = s5_scan_fwd(a, bb, cb, _s5_tables(ar, ai, False), rider=rider)
    row = lambda arr: _rows(arr, tm)
    act_args = [row(yc), row(a), _const(w['dskip'])]
    act_outs = [Out((T, D), BF16, (tm, D), lambda i: (i, 0))]
    y = stage_fwd("s5_act", _s5_act_fn, (nt,), act_args, act_outs)[0]
    zz = matmul("s5_glu", y, w['w_glu'], 'nn')
    glu_fn = _make_glu_res_fn(tm)
    glu_args = [_rows(zz, tm, BF16), row(h)]
    glu_outs = [Out((T, D), F32, (tm, D), lambda i: (i, 0))]
    h_new = stage_fwd("s5_gate", glu_fn, (nt,), glu_args, glu_outs)[0]

    def bwd(dh, rider=None):
        dzz, dh_res = stage_bwd("s5_gate_b", glu_fn, (nt,), glu_args, glu_outs, [dh])
        dy, dwglu = linear_bwd("s5_glu_b", y, w['w_glu'], dzz)
        dyc, da1, ddskip = stage_bwd("s5_act_b", _s5_act_fn, (nt,), act_args, act_outs, [dy])
        da2, dbb, dcb, dab = s5_scan_bwd(a, bb, cb, _s5_tables(ar, ai, True), xs, dyc, rider=rider)
        dab = dab.sum(axis=1)
        dssm = disc_vjp((dbb, dcb, dab[:, :S5_W], dab[:, S5_W:]))
        dh2, dg = _s5_norm_bwd(h, g, da1, da2, dh_res, tm)
        grads = dict(zip(('lam_re', 'lam_im', 'log_dt', 'b_re', 'b_im', 'c_re', 'c_im'), dssm))
        grads.update(g=dg, dskip=ddskip, w_glu=dwglu)
        return dh2, grads

    return h_new, bwd


def _norm3_bwd_fn(pids, h, g):
    a = _rms(h, g)
    return a, a, h


def _s5_norm_bwd(h, g, da1, da2, dh, tm):
    T = h.shape[0]
    o = Out((T, D), F32, (tm, D), lambda i: (i, 0))
    return stage_bwd("s5_norm_b", _norm3_bwd_fn, (T // tm,), [_rows(h, tm), _const(g)], [o, o, o], [da1, da2, dh])


RET_R = 384


def _rope256(x, cos, sin):
    x1, x2 = x[:, :128], x[:, 128:]
    return jnp.concatenate([x1 * cos - x2 * sin, x1 * sin + x2 * cos], axis=1)


def _ret_fn(pids, z, gn, cos, sin, dmat, qdec, kdec, cdec, st):
    R = z.shape[0]
    q = _rope256(z[:, :256], cos, sin)
    k = _rope256(z[:, 256:512], cos, sin) * (RET_DK ** -0.5)
    v, gate = z[:, 512:1024], z[:, 1024:]
    outs = []
    for cc in range(R // CHUNK):
        lo = cc * CHUNK
        qc, kc, vc = q[lo:lo + CHUNK], k[lo:lo + CHUNK], v[lo:lo + CHUNK]
        outs.append(mm_nn(mm_nt(qc, kc) * dmat, vc) + mm_nn(qc * qdec, st))
        st = st * cdec + mm_tn(kc * kdec, vc)
    o = jnp.concatenate(outs, axis=0)
    mu = jnp.mean(o, axis=-1, keepdims=True)
    var = jnp.mean(jnp.square(o - mu), axis=-1, keepdims=True)
    o = (o - mu) * lax.rsqrt(var + EPS)
    return o * gn * _silu(gate), st


def ret_mixer(h, g, w, tabs, rider=None):
    T = h.shape[0]
    a = norm_fwd("ret_norm", h, g, BF16)
    z = matmul("ret_in", a, w['w_in'], 'nn')
    grid = (RET_H, T // RET_R)
    hw = RET_DK * 2 + RET_DV * 2
    args = [Arg(z, (RET_R, hw), lambda hh, j: (j, hh), gdtype=BF16), Arg(w['gn'], (1, RET_DV), lambda hh, j: (0, hh)),
            Arg(tabs['cos_d'], (RET_R, 128), lambda hh, j: (j, 0), False),
            Arg(tabs['sin_d'], (RET_R, 128), lambda hh, j: (j, 0), False),
            Arg(tabs['ret_dmat'], (None, CHUNK, CHUNK), lambda hh, j: (hh, 0, 0), False),
            Arg(tabs['ret_qdec'], (None, CHUNK, 1), lambda hh, j: (hh, 0, 0), False),
            Arg(tabs['ret_kdec'], (None, CHUNK, 1), lambda hh, j: (hh, 0, 0), False),
            Arg(tabs['ret_cdec'], (None, 1, 1), lambda hh, j: (hh, 0, 0), False)]
    outs = [Out((T, RET_H * RET_DV), BF16, (RET_R, RET_DV), lambda hh, j: (j, hh))]
    o, states = stage_fwd("ret_chunks", _ret_fn, grid, args, outs, state_shape=(RET_DK, RET_DV), rider=rider)
    h_new = matmul("ret_o", o, w['w_o'], 'nn', res=h)

    def bwd(dh, rider=None):
        do, dwo = linear_bwd("ret_o_b", o, w['w_o'], dh, mask=True)
        dz, dgn = stage_bwd("ret_chunks_b", _ret_fn, grid, args, outs, [do], state_shape=(RET_DK, RET_DV),
                            states=states, rider=rider)
        da, dwin = linear_bwd("ret_in_b", a, w['w_in'], dz)
        dh2, dg = norm_bwd("ret_norm_b", h, g, da, dh)
        return dh2, dict(g=dg, w_in=dwin, gn=dgn, w_o=dwo)

    return h_new, bwd


def loss_head(h, tgt):
    T = h.shape[0]
    tm = _row_tile(T)

    def body(h_ref, t_ref, loss_ref, dh_ref):
        i = pl.program_id(0)
        rows = _row_ids(i, tm, (tm, 1))
        err = jnp.where(rows >= LEAD, h_ref[...] - t_ref[...], 0.0)
        dh_ref[...] = err * (1.0 / D)
        part = jnp.full((8, 128), 0.5 * jnp.sum(jnp.sum(err * err, axis=1, keepdims=True) * (1.0 / D)), F32)

        @pl.when(i == 0)
        def _():
            loss_ref[...] = part

        @pl.when(i != 0)
        def _():
            loss_ref[...] += part

    loss, dh = pl.pallas_call(
        body, grid=(T // tm,),
        in_specs=[pl.BlockSpec((tm, D), lambda i: (i, 0)), pl.BlockSpec((tm, D), lambda i: (i, 0))],
        out_specs=[pl.BlockSpec((8, 128), lambda i: (0, 0)), pl.BlockSpec((tm, D), lambda i: (i, 0))],
        out_shape=[jax.ShapeDtypeStruct((8, 128), F32), jax.ShapeDtypeStruct((T, D), F32)], name="loss_head",
        compiler_params=_cparams())(h, tgt)
    return loss[0, 0], dh


def _tables(T):
    pos = jnp.maximum(jnp.arange(T, dtype=jnp.int32) - PAD, 0).astype(F32)

    def cs(dim):
        inv_freq = 1.0 / (10000.0 ** (jnp.arange(0, dim, 2, dtype=F32) / dim))
        ang = pos[:, None] * inv_freq[None, :]
        return jnp.cos(ang), jnp.sin(ang)

    ca, sa = cs(MLA_ROPE)
    zeros = jnp.zeros((T, 64), F32)
    cd, sd = cs(RET_DK)
    log_gamma = jnp.log(1.0 - jnp.exp2(-5.0 - jnp.arange(RET_H, dtype=F32)))
    p = jnp.arange(CHUNK, dtype=F32)
    diff = p[:, None] - p[None, :]
    dmat = jnp.where(diff >= 0, jnp.exp(diff[None] * log_gamma[:, None, None]), 0.0)
    return dict(cos_a=jnp.concatenate([ca, ca, zeros], axis=1), sin_a=jnp.concatenate([-sa, sa, zeros], axis=1),
                cos_d=cd, sin_d=sd, ret_dmat=dmat,
                ret_qdec=jnp.exp((p[None, :] + 1.0) * log_gamma[:, None])[..., None],
                ret_kdec=jnp.exp((CHUNK - 1.0 - p[None, :]) * log_gamma[:, None])[..., None],
                ret_cdec=jnp.exp(CHUNK * log_gamma)[:, None, None])


def _hgrn_lower_bound(logits):
    lb_cum = jnp.cumsum(jax.nn.softmax(logits, axis=0), axis=0)
    return (lb_cum - lb_cum[0:1])[1:2]


def _uq_to_heads(w):
    t = w.reshape(w.shape[0], MLA_H, MLA_QK)
    return jnp.pad(t, ((0, 0), (0, 0), (0, 256 - MLA_QK))).reshape(w.shape[0], MLA_H * 256)


def _uq_from_heads(g):
    return g.reshape(g.shape[0], MLA_H, 256)[:, :, :MLA_QK].reshape(g.shape[0], MLA_H * MLA_QK)


def _head_interleave(w, widths, heads):
    parts, lo = [], 0
    for wd in widths:
        parts.append(w[:, lo:lo + heads * wd].reshape(w.shape[0], heads, wd))
        lo += heads * wd
    return jnp.concatenate(parts, axis=2).reshape(w.shape[0], -1)


def _head_deinterleave(g, widths, heads):
    t = g.reshape(g.shape[0], heads, sum(widths))
    parts, lo = [], 0
    for wd in widths:
        parts.append(t[:, :, lo:lo + wd].reshape(g.shape[0], heads * wd))
        lo += wd
    return jnp.concatenate(parts, axis=1)


HG_WIDTHS = (128, 128, 128, 128)
RET_WIDTHS = (RET_DK, RET_DK, RET_DV, RET_DV)


def _split_head_gain(g):
    return g[:, :128], jnp.pad(g[:, 128:], ((0, 0), (0, 64)))


def _join_head_gain(dn, dr):
    return jnp.concatenate([dn, dr[:, :64]], axis=1)


def local_step(x, target, W, ex):
    S = x.shape[0]
    T = S + LEAD
    tabs = _tables(T)
    h = jnp.concatenate([jnp.zeros((PAD, D), F32), W['meta_tokens'], x], axis=0)
    tgt = jnp.concatenate([jnp.zeros((LEAD, D), F32), target], axis=0)

    gqn, gqr = _split_head_gain(W['mla_q_head_g'])
    gkn, gkr = _split_head_gain(W['mla_k_head_g'])
    lb, lb_vjp = jax.vjp(_hgrn_lower_bound, W['hgrn_lb_logits'])

    def ffn(i, hh):
        return ffn_layer(i, hh, W['norm_ffn_g'][i:i + 1], ex.weight('ffn_w_up', i), W['ffn_conv_w'][i],
                         W['ffn_conv_b'][i:i + 1], ex.weight('ffn_w_down', i))

    bm, bf = [None] * 4, [None] * 4
    ex.gather(['mla'], name="gather_mla")
    w0 = dict(w_down=jnp.pad(ex.weight('mla_w_down'), ((0, 0), (0, 64))), gcq=W['mla_cq_norm_g'],
              gckv=W['mla_ckv_norm_g'], w_uq=_uq_to_heads(ex.weight('mla_w_uq')), w_ukv=ex.weight('mla_w_ukv'),
              gqn=gqn, gqr=gqr, gkn=gkn, gkr=gkr, w_o=ex.weight('mla_w_o'))
    h, bm[0] = mla_mixer(h, W['norm_mix_g'][0:1], w0, tabs, rider=ex.gather(['ffn0', 'hgrn', 'ffn1']))
    h, bf[0] = ffn(0, h)
    w1 = dict(w_in=_head_interleave(ex.weight('hgrn_w_in'), HG_WIDTHS, HG_H), lb=lb, go=W['hgrn_o_norm_g'],
              w_o=ex.weight('hgrn_w_o'))
    h, bm[1] = hgrn_mixer(h, W['norm_mix_g'][1:2], w1, rider=ex.gather(['s5', 'ffn2']))
    h, bf[1] = ffn(1, h)
    w2 = dict(lam_re=W['s5_lam_re'][0], lam_im=W['s5_lam_im'][0], log_dt=W['s5_log_dt'][0], b_re=W['s5_b_re'][0],
              b_im=W['s5_b_im'][0], c_re=W['s5_c_re'][0], c_im=W['s5_c_im'][0], dskip=W['s5_d'],
              w_glu=ex.weight('s5_w_glu'))
    h, bm[2] = s5_mixer(h, W['norm_mix_g'][2:3], w2, rider=ex.gather(['ret']))
    h, bf[2] = ffn(2, h)
    w3 = dict(w_in=_head_interleave(ex.weight('ret_w_in'), RET_WIDTHS, RET_H), gn=W['ret_gn_g'],
              w_o=ex.weight('ret_w_o'))
    h, bm[3] = ret_mixer(h, W['norm_mix_g'][3:4], w3, tabs, rider=ex.gather(['ffn3']))
    h, bf[3] = ffn(3, h)

    loss, dh = loss_head(h, tgt)

    def ffn_grads(i, g):
        return {('ffn_w_up', i): g['w_up'], ('ffn_w_down', i): g['w_down']}

    gm, gf = [None] * 4, [None] * 4
    dh, gf[3] = bf[3](dh)
    dh, gm[3] = bm[3](dh, rider=ex.scatter(ffn_grads(3, gf[3])))
    dh, gf[2] = bf[2](dh)
    ret_grads = {('ret_w_in', 0): _head_deinterleave(gm[3]['w_in'], RET_WIDTHS, RET_H), ('ret_w_o', 0): gm[3]['w_o']}
    dh, gm[2] = bm[2](dh, rider=ex.scatter(ffn_grads(2, gf[2])))
    dh, gf[1] = bf[1](dh)
    dh, gm[1] = bm[1](dh, rider=ex.scatter(ffn_grads(1, gf[1])))
    dh, gf[0] = bf[0](dh)
    hgrn_grads = {('hgrn_w_in', 0): _head_deinterleave(gm[1]['w_in'], HG_WIDTHS, HG_H), ('hgrn_w_o', 0): gm[1]['w_o']}
    G = {}
    G['norm_ffn_g'] = jnp.concatenate([gf[i]['g'] for i in range(4)], axis=0)
    G['hgrn_lb_logits'] = lb_vjp(gm[1]['lb'])[0]
    G['hgrn_o_norm_g'] = gm[1]['go']
    for n in ('lam_re', 'lam_im', 'log_dt', 'b_re', 'b_im', 'c_re', 'c_im'):
        G['s5_' + n] = gm[2][n][None]
    G['s5_d'] = gm[2]['dskip']
    G['ret_gn_g'] = gm[3]['gn']
    G['ffn_conv_w'] = jnp.stack([gf[i]['cw'] for i in range(4)])
    G['ffn_conv_b'] = jnp.concatenate([gf[i]['cb'] for i in range(4)], axis=0)
    early = ex.all_devices(_pack([G[n] for n in SMALL_EARLY], F32, 8))

    dh, gm[0] = bm[0](
        dh,
        rider_dkv=lambda dwo: [ex.scatter({('s5_w_glu', 0): gm[2]['w_glu'], **hgrn_grads, **ffn_grads(0, gf[0]),
                                           ('mla_w_o', 0): dwo}), ex.swap()],
        rider_dq=lambda: [early, ex.scatter(ret_grads)], rider_heads=ex.swap)
    a = gm[0]
    mla_grads = {('mla_w_down', 0): a['w_down'][:, :MLA_QL + MLA_KVL + MLA_ROPE], ('mla_w_uq', 0): _uq_from_heads(a['w_uq']),
                 ('mla_w_ukv', 0): a['w_ukv']}
    G['meta_tokens'] = dh[PAD:LEAD]
    G['norm_mix_g'] = jnp.concatenate([gm[i]['g'] for i in range(4)], axis=0)
    G['mla_cq_norm_g'], G['mla_ckv_norm_g'] = a['gcq'], a['gckv']
    G['mla_q_head_g'] = _join_head_gain(a['gqn'], a['gqr'])
    G['mla_k_head_g'] = _join_head_gain(a['gkn'], a['gkr'])
    ex.tail = [ex.scatter(mla_grads), ex.all_devices(_pack([G[n] for n in SMALL_LATE], F32, 8))]
    return loss, dh[LEAD:], G


PACK_W = 1024
ANY = pl.BlockSpec(memory_space=pl.ANY)


def _pack(arrs, dtype, row_mult):
    flat = jnp.concatenate([a.reshape(-1).astype(dtype) for a in arrs])
    n = flat.shape[0]
    rows = -(-n // (PACK_W * row_mult)) * row_mult
    return jnp.pad(flat, (0, rows * PACK_W - n)).reshape(rows, PACK_W)


def _unpack(buf, shapes):
    flat = buf.reshape(-1)
    out, off = [], 0
    for s in shapes:
        n = math.prod(s)
        out.append(flat[off:off + n].reshape(s))
        off += n
    return out


def _my_pos():
    return lax.axis_index("x"), lax.axis_index("y"), lax.axis_index("c")


def _other_chips(x, y):
    return [(1 - x, y), (x, 1 - y), (1 - x, 1 - y)]


def gather_chips(name, src):
    def body(src_ref, out_ref, send_sems, recv_sems, local_sem):
        x, y, c = _my_pos()
        q = 2 * x + y
        mine = pltpu.make_async_copy(src_ref, out_ref.at[q], local_sem)
        mine.start()
        peers = _other_chips(x, y)

        def copy(k, slot, peer):
            return pltpu.make_async_remote_copy(src_ref=src_ref, dst_ref=out_ref.at[slot], send_sem=send_sems.at[k],
                                                recv_sem=recv_sems.at[k], device_id=(peer[0], peer[1], c),
                                                device_id_type=MESH_ID)
        sends = [copy(k, q, p) for k, p in enumerate(peers)]
        for cp in sends:
            cp.start()
        for k, p in enumerate(peers):
            copy(k, 2 * p[0] + p[1], p).wait_recv()
        for cp in sends:
            cp.wait_send()
        mine.wait()

    return pl.pallas_call(body, out_shape=jax.ShapeDtypeStruct((4,) + src.shape, src.dtype), in_specs=[ANY],
                          out_specs=ANY, name=name,
                          scratch_shapes=[pltpu.SemaphoreType.DMA((3,)), pltpu.SemaphoreType.DMA((3,)),
                                          pltpu.SemaphoreType.DMA(())])(src)


def _pack_tile(rows):
    return _divisors(rows, (256, 128, 64, 32, 16, 8))[0] if rows > 512 else rows


def sum_slots(name, slots):
    n, rows, w = slots.shape
    tr = _pack_tile(rows)

    def body(s_ref, o_ref):
        acc = s_ref[0].astype(F32)
        for k in range(1, n):
            acc = acc + s_ref[k].astype(F32)
        o_ref[...] = acc

    return pl.pallas_call(body, grid=(rows // tr,), in_specs=[pl.BlockSpec((n, tr, w), lambda i: (0, i, 0))],
                          out_specs=pl.BlockSpec((tr, w), lambda i: (i, 0)),
                          out_shape=jax.ShapeDtypeStruct((rows, w), F32), name=name, compiler_params=_cparams())(slots)


def adamw(name, grads, w, m, v):
    rows, wd = w.shape
    tr = _pack_tile(rows)
    ng = len(grads)

    def body(*refs):
        g = refs[0][...]
        for r in refs[1:ng]:
            g = g + r[...]
        w_ref, m_ref, v_ref = refs[ng:ng + 3]
        g_out, d_out, m_out, v_out = refs[ng + 3:]
        m_new = ADAM_B1 * m_ref[...] + (1.0 - ADAM_B1) * g
        v_new = ADAM_B2 * v_ref[...] + (1.0 - ADAM_B2) * jnp.square(g)
        m_hat = m_new / (1.0 - ADAM_B1 ** ADAM_STEP)
        v_hat = v_new / (1.0 - ADAM_B2 ** ADAM_STEP)
        g_out[...] = g
        d_out[...] = -ADAM_LR * (m_hat / (jnp.sqrt(v_hat) + ADAM_EPS) + ADAM_WD * w_ref[...])
        m_out[...] = m_new
        v_out[...] = v_new

    spec = pl.BlockSpec((tr, wd), lambda i: (i, 0))
    shape = jax.ShapeDtypeStruct((rows, wd), F32)
    return pl.pallas_call(body, grid=(rows // tr,), in_specs=[spec] * (ng + 3), out_specs=[spec] * 4,
                          out_shape=[shape] * 4, name=name, compiler_params=_cparams())(*grads, w, m, v)


def _sem_scratch(nw):
    return [pltpu.SemaphoreType.DMA((3 * nw,)), pltpu.SemaphoreType.DMA((3 * nw,)), pltpu.SemaphoreType.DMA((nw,))]


def _block2d(ref, axis, p, n):
    if axis == 0:
        return ref.at[pl.ds(pl.multiple_of(p * n, 16), n), :]
    return ref.at[:, pl.ds(pl.multiple_of(p * n, 128), n)]


class ScatterRider:
    def __init__(self, items):
        self.items = items
        self.operands = [it[0] for it in items]
        self.results = None
        self.out_shapes = [jax.ShapeDtypeStruct((4, arr.shape[0] // 4, arr.shape[1]) if axis == 0 else
                                                (4, arr.shape[0], arr.shape[1] // 4), arr.dtype) for arr, axis in items]
        self.scratch = _sem_scratch(len(items))

    def _copies(self, ins, outs, sems):
        send_sems, recv_sems, local_sems = sems
        x, y, c = _my_pos()
        q = 2 * x + y
        local, sends, lands = [], [], []
        for w, (arr, axis) in enumerate(self.items):
            n = arr.shape[axis] // 4
            local.append(pltpu.make_async_copy(_block2d(ins[w], axis, q, n), outs[w].at[q], local_sems.at[w]))
            for k, (px, py) in enumerate(_other_chips(x, y)):
                p = 2 * px + py
                sems_k = dict(send_sem=send_sems.at[3 * w + k], recv_sem=recv_sems.at[3 * w + k],
                              device_id=(px, py, c), device_id_type=MESH_ID)
                theirs = _block2d(ins[w], axis, p, n)
                sends.append(pltpu.make_async_remote_copy(src_ref=theirs, dst_ref=outs[w].at[q], **sems_k))
                lands.append(pltpu.make_async_remote_copy(src_ref=theirs, dst_ref=outs[w].at[p], **sems_k))
        return local, sends, lands

    def start(self, ins, outs, sems):
        local, sends, _ = self._copies(ins, outs, sems)
        for cp in local + sends:
            cp.start()

    def finish(self, ins, outs, sems):
        local, sends, lands = self._copies(ins, outs, sems)
        for cp in lands:
            cp.wait_recv()
        for cp in sends:
            cp.wait_send()
        for cp in local:
            cp.wait()


class GatherRider:
    def __init__(self, items):
        self.items = items
        self.operands = [it[0] for it in items]
        self.results = None
        self.out_shapes = []
        for arr, _, axis in items:
            r, c = arr.shape[1:]
            assert r % 32 == 0
            self.out_shapes.append(jax.ShapeDtypeStruct((4 * r, c) if axis == 0 else (r, 4 * c), arr.dtype))
        n = len(items)
        dma = pltpu.SemaphoreType.DMA
        self.scratch = [dma((3 * n,)), dma((3 * n,)), dma((n,)), dma((3 * n,)), dma((3 * n,))]

    def _copies(self, ins, outs, sems):
        send_sems, recv_sems, local_sems, pass_send_sems, pass_recv_sems = sems
        x, y, c = _my_pos()
        q = 2 * x + y
        local, sends, lands, passes, pass_lands = [], [], [], [], []
        for w, (arr, layer, axis) in enumerate(self.items):
            r, cols = arr.shape[1:]
            half = r // 2
            src = ins[w].at[layer]

            def part(blk, hc, w=w, axis=axis, r=r, cols=cols, half=half):
                if axis == 0:
                    return outs[w].at[pl.ds(pl.multiple_of(blk * r + hc * half, 16), half), :]
                return outs[w].at[pl.ds(pl.multiple_of(hc * half, 16), half), pl.ds(pl.multiple_of(blk * cols, 128), cols)]

            local.append(pltpu.make_async_copy(src, _block2d(outs[w], axis, q, arr.shape[1 + axis]), local_sems.at[w]))
            for k, (px, py) in enumerate(_other_chips(x, y)):
                p = 2 * px + py
                ici = dict(send_sem=send_sems.at[3 * w + k], recv_sem=recv_sems.at[3 * w + k],
                           device_id=(px, py, c), device_id_type=MESH_ID)
                d2d = dict(send_sem=pass_send_sems.at[3 * w + k], recv_sem=pass_recv_sems.at[3 * w + k],
                           device_id=(x, y, 1 - c), device_id_type=MESH_ID)
                mine = src.at[pl.ds(pl.multiple_of(c * half, 16), half), :]
                sends.append(pltpu.make_async_remote_copy(src_ref=mine, dst_ref=part(q, c), **ici))
                lands.append(pltpu.make_async_remote_copy(src_ref=mine, dst_ref=part(p, c), **ici))
                passes.append(pltpu.make_async_remote_copy(src_ref=part(p, c), dst_ref=part(p, c), **d2d))
                pass_lands.append(pltpu.make_async_remote_copy(src_ref=part(p, c), dst_ref=part(p, 1 - c), **d2d))
        return local, sends, lands, passes, pass_lands

    def start(self, ins, outs, sems):
        local, sends, _, _, _ = self._copies(ins, outs, sems)
        for cp in local + sends:
            cp.start()

    def middle(self, ins, outs, sems):
        _, _, lands, passes, _ = self._copies(ins, outs, sems)
        for land, cp in zip(lands, passes):
            land.wait_recv()
            cp.start()

    def finish(self, ins, outs, sems):
        local, sends, _, passes, pass_lands = self._copies(ins, outs, sems)
        for cp in pass_lands:
            cp.wait_recv()
        for cp in sends + passes:
            cp.wait_send()
        for cp in local:
            cp.wait()


class SwapRider:
    def __init__(self, arrs):
        self.operands = list(arrs)
        self.out_shapes = [jax.ShapeDtypeStruct(a.shape, a.dtype) for a in arrs]
        self.scratch = [pltpu.SemaphoreType.DMA((len(arrs),)), pltpu.SemaphoreType.DMA((len(arrs),))]
        self.results = None

    def _copies(self, ins, outs, sems):
        x, y, c = _my_pos()
        return [pltpu.make_async_remote_copy(src_ref=ins[w], dst_ref=outs[w], send_sem=sems[0].at[w],
                                             recv_sem=sems[1].at[w], device_id=(x, y, 1 - c), device_id_type=MESH_ID)
                for w in range(len(self.operands))]

    def start(self, ins, outs, sems):
        for cp in self._copies(ins, outs, sems):
            cp.start()

    def finish(self, ins, outs, sems):
        for cp in self._copies(ins, outs, sems):
            cp.wait()


class AllDevicesRider:
    def __init__(self, src):
        self.operands = [src]
        self.out_shapes = [jax.ShapeDtypeStruct((8,) + src.shape, src.dtype)]
        self.scratch = [pltpu.SemaphoreType.DMA((7,)), pltpu.SemaphoreType.DMA((7,)), pltpu.SemaphoreType.DMA(())]
        self.results = None

    def _copies(self, ins, outs, sems):
        x, y, c = _my_pos()
        me = 4 * x + 2 * y + c
        local = pltpu.make_async_copy(ins[0], outs[0].at[me], sems[2])
        sends, lands = [], []
        for k, m in enumerate(range(1, 8)):
            peer = ((1 - x) if m & 4 else x, (1 - y) if m & 2 else y, (1 - c) if m & 1 else c)
            sems_k = dict(send_sem=sems[0].at[k], recv_sem=sems[1].at[k], device_id=peer, device_id_type=MESH_ID)
            sends.append(pltpu.make_async_remote_copy(src_ref=ins[0], dst_ref=outs[0].at[me], **sems_k))
            lands.append(pltpu.make_async_remote_copy(src_ref=ins[0], dst_ref=outs[0].at[4 * peer[0] + 2 * peer[1] + peer[2]],
                                                      **sems_k))
        return local, sends, lands

    def start(self, ins, outs, sems):
        local, sends, _ = self._copies(ins, outs, sems)
        for cp in [local] + sends:
            cp.start()

    def finish(self, ins, outs, sems):
        local, sends, lands = self._copies(ins, outs, sems)
        for cp in lands:
            cp.wait_recv()
        for cp in sends:
            cp.wait_send()
        local.wait()


class RiderGroup:
    def __init__(self, riders):
        self.riders = riders
        self.operands = [a for r in riders for a in r.operands]
        self.out_shapes = [s for r in riders for s in r.out_shapes]
        self.scratch = [s for r in riders for s in r.scratch]

    def _split(self, ins, outs, sems):
        for r in self.riders:
            ni, no, ns = len(r.operands), len(r.out_shapes), len(r.scratch)
            yield r, ins[:ni], outs[:no], sems[:ns]
            ins, outs, sems = ins[ni:], outs[no:], sems[ns:]

    def start(self, ins, outs, sems):
        for r, i, o, s in self._split(ins, outs, sems):
            r.start(i, o, s)

    def middle(self, ins, outs, sems):
        for r, i, o, s in self._split(ins, outs, sems):
            if hasattr(r, 'middle'):
                r.middle(i, o, s)

    def finish(self, ins, outs, sems):
        for r, i, o, s in self._split(ins, outs, sems):
            r.finish(i, o, s)

    @property
    def results(self):
        return None

    @results.setter
    def results(self, res):
        for r in self.riders:
            no = len(r.out_shapes)
            r.results, res = list(res[:no]), res[no:]


def run_rider(name, rider):
    n_in, n_out = len(rider.operands), len(rider.out_shapes)

    def body(*refs):
        ins, outs, sems = refs[:n_in], refs[n_in:n_in + n_out], refs[n_in + n_out:]
        rider.start(ins, outs, sems)
        if hasattr(rider, 'middle'):
            rider.middle(ins, outs, sems)
        rider.finish(ins, outs, sems)

    rider.results = list(pl.pallas_call(body, out_shape=rider.out_shapes, in_specs=[ANY] * n_in, out_specs=[ANY] * n_out,
                                        name=name, scratch_shapes=rider.scratch)(*rider.operands))


WEIGHT_GROUPS = {'mla': [('mla_w_down', 0), ('mla_w_uq', 0), ('mla_w_ukv', 0), ('mla_w_o', 0)],
                 'hgrn': [('hgrn_w_in', 0), ('hgrn_w_o', 0)], 's5': [('s5_w_glu', 0)],
                 'ret': [('ret_w_in', 0), ('ret_w_o', 0)]}
WEIGHT_GROUPS.update({f'ffn{i}': [('ffn_w_up', i), ('ffn_w_down', i)] for i in range(4)})


class Exchange:
    def __init__(self, shards=None, full=None):
        self.shards, self.full = shards, dict(full or {})
        self.got, self.recv, self.sib, self.grads, self.small, self.tail = {}, {}, {}, {}, [], []

    def gather(self, groups, name=None):
        if self.shards is None:
            return None
        keys = [k for g in groups for k in WEIGHT_GROUPS[g]]
        rider = GatherRider([(self.shards[n], layer, SHARD_AXIS[n] - 1) for n, layer in keys])
        self.got.update({k: (rider, j) for j, k in enumerate(keys)})
        if name is not None:
            run_rider(name, rider)
        return rider

    def weight(self, n, layer=0):
        if self.shards is None:
            return self.full[n][layer]
        rider, j = self.got[(n, layer)]
        return rider.results[j]

    def scatter(self, grads, name=None):
        if self.shards is None:
            self.grads.update(grads)
            return None
        keys = list(grads)
        rider = ScatterRider([(grads[k], SHARD_AXIS[k[0]] - 1) for k in keys])
        self.recv.update({k: (rider, j) for j, k in enumerate(keys)})
        if name is not None:
            run_rider(name, rider)
        return rider

    def received(self, n, layer):
        rider, j = self.recv[(n, layer)]
        return rider.results[j]

    def swap(self, name=None):
        if self.shards is None:
            return None
        keys = [k for k, (r, _) in self.recv.items() if k not in self.sib and r.results is not None]
        rider = SwapRider([self.received(*k) for k in keys])
        self.sib.update({k: (rider, j) for j, k in enumerate(keys)})
        if name is not None:
            run_rider(name, rider)
        return rider

    def sibling(self, n, layer):
        rider, j = self.sib[(n, layer)]
        return rider.results[j]

    def all_devices(self, packed, name=None):
        if self.shards is None:
            return None
        rider = AllDevicesRider(packed)
        self.small.append(rider)
        if name is not None:
            run_rider(name, rider)
        return rider


ADAM_BLOCK_ELEMS = 256 * 1024


def adamw_shard(name, mine, sib, w, m, v, first_layer=0, into=None, rider=None):
    _, rows, cols = w.shape
    nl = mine.shape[1]
    tr = [t for t in (512, 384, 352, 256, 176, 128, 64, 32, 16) if rows % t == 0 and t * cols <= ADAM_BLOCK_ELEMS][0]

    def body(a_ref, b_ref, w_ref, m_ref, v_ref, *rest):
        g_out, d_out, m_out, v_out = rest[-4:]

        def total(r):
            acc = r[0].astype(F32)
            for k in range(1, 4):
                acc = acc + r[k].astype(F32)
            return acc
        g = total(a_ref) + total(b_ref)
        m_new = ADAM_B1 * m_ref[...] + (1.0 - ADAM_B1) * g
        v_new = ADAM_B2 * v_ref[...] + (1.0 - ADAM_B2) * jnp.square(g)
        m_hat = m_new / (1.0 - ADAM_B1 ** ADAM_STEP)
        v_hat = v_new / (1.0 - ADAM_B2 ** ADAM_STEP)
        g_out[...] = g
        d_out[...] = -ADAM_LR * (m_hat / (jnp.sqrt(v_hat) + ADAM_EPS) + ADAM_WD * w_ref[...])
        m_out[...] = m_new
        v_out[...] = v_new

    slots = pl.BlockSpec((4, None, tr, cols), lambda l, i: (0, l, i, 0))
    spec = pl.BlockSpec((None, tr, cols), lambda l, i: (l + first_layer, i, 0))
    shape = jax.ShapeDtypeStruct(w.shape, F32)
    in_specs, operands, aliases = [slots, slots, spec, spec, spec], [mine, sib, w, m, v], {}
    if into is not None:
        in_specs += [ANY] * 4
        operands += list(into)
        aliases = {5 + k: k for k in range(4)}
    return _pallas(name, body, (nl, rows // tr), in_specs, [spec] * 4, [shape] * 4, [], operands, rider, aliases)


def kernel(x, meta_tokens, norm_mix_g, norm_ffn_g, mla_w_down, mla_cq_norm_g, mla_ckv_norm_g, mla_w_uq, mla_w_ukv, mla_q_head_g, mla_k_head_g, mla_w_o, hgrn_w_in, hgrn_lb_logits, hgrn_o_norm_g, hgrn_w_o, s5_lam_re, s5_lam_im, s5_log_dt, s5_b_re, s5_b_im, s5_c_re, s5_c_im, s5_d, s5_w_glu, ret_w_in, ret_gn_g, ret_w_o, ffn_w_up, ffn_conv_w, ffn_conv_b, ffn_w_down, loss_target, m_meta_tokens, m_norm_mix_g, m_norm_ffn_g, m_mla_w_down, m_mla_cq_norm_g, m_mla_ckv_norm_g, m_mla_w_uq, m_mla_w_ukv, m_mla_q_head_g, m_mla_k_head_g, m_mla_w_o, m_hgrn_w_in, m_hgrn_lb_logits, m_hgrn_o_norm_g, m_hgrn_w_o, m_s5_lam_re, m_s5_lam_im, m_s5_log_dt, m_s5_b_re, m_s5_b_im, m_s5_c_re, m_s5_c_im, m_s5_d, m_s5_w_glu, m_ret_w_in, m_ret_gn_g, m_ret_w_o, m_ffn_w_up, m_ffn_conv_w, m_ffn_conv_b, m_ffn_w_down, v_meta_tokens, v_norm_mix_g, v_norm_ffn_g, v_mla_w_down, v_mla_cq_norm_g, v_mla_ckv_norm_g, v_mla_w_uq, v_mla_w_ukv, v_mla_q_head_g, v_mla_k_head_g, v_mla_w_o, v_hgrn_w_in, v_hgrn_lb_logits, v_hgrn_o_norm_g, v_hgrn_w_o, v_s5_lam_re, v_s5_lam_im, v_s5_log_dt, v_s5_b_re, v_s5_b_im, v_s5_c_re, v_s5_c_im, v_s5_d, v_s5_w_glu, v_ret_w_in, v_ret_gn_g, v_ret_w_o, v_ffn_w_up, v_ffn_conv_w, v_ffn_conv_b, v_ffn_w_down):
    vals = (x, meta_tokens, norm_mix_g, norm_ffn_g, mla_w_down, mla_cq_norm_g, mla_ckv_norm_g, mla_w_uq, mla_w_ukv, mla_q_head_g, mla_k_head_g, mla_w_o, hgrn_w_in, hgrn_lb_logits, hgrn_o_norm_g, hgrn_w_o, s5_lam_re, s5_lam_im, s5_log_dt, s5_b_re, s5_b_im, s5_c_re, s5_c_im, s5_d, s5_w_glu, ret_w_in, ret_gn_g, ret_w_o, ffn_w_up, ffn_conv_w, ffn_conv_b, ffn_w_down, loss_target, m_meta_tokens, m_norm_mix_g, m_norm_ffn_g, m_mla_w_down, m_mla_cq_norm_g, m_mla_ckv_norm_g, m_mla_w_uq, m_mla_w_ukv, m_mla_q_head_g, m_mla_k_head_g, m_mla_w_o, m_hgrn_w_in, m_hgrn_lb_logits, m_hgrn_o_norm_g, m_hgrn_w_o, m_s5_lam_re, m_s5_lam_im, m_s5_log_dt, m_s5_b_re, m_s5_b_im, m_s5_c_re, m_s5_c_im, m_s5_d, m_s5_w_glu, m_ret_w_in, m_ret_gn_g, m_ret_w_o, m_ffn_w_up, m_ffn_conv_w, m_ffn_conv_b, m_ffn_w_down, v_meta_tokens, v_norm_mix_g, v_norm_ffn_g, v_mla_w_down, v_mla_cq_norm_g, v_mla_ckv_norm_g, v_mla_w_uq, v_mla_w_ukv, v_mla_q_head_g, v_mla_k_head_g, v_mla_w_o, v_hgrn_w_in, v_hgrn_lb_logits, v_hgrn_o_norm_g, v_hgrn_w_o, v_s5_lam_re, v_s5_lam_im, v_s5_log_dt, v_s5_b_re, v_s5_b_im, v_s5_c_re, v_s5_c_im, v_s5_d, v_s5_w_glu, v_ret_w_in, v_ret_gn_g, v_ret_w_o, v_ffn_w_up, v_ffn_conv_w, v_ffn_conv_b, v_ffn_w_down)
    names = ['x'] + WEIGHTS + ['loss_target'] + ['m_' + n for n in WEIGHTS] + ['v_' + n for n in WEIGHTS]
    A = dict(zip(names, vals))
    q = 2 * lax.axis_index("x") + lax.axis_index("y")

    small_shapes = [A[n].shape for n in SMALL_SHARDED]
    got_small = gather_chips("gather_small", _pack([A[n] for n in SMALL_SHARDED], F32, 8))
    W = {n: A[n] for n in REPLICATED}
    parts_small = [_unpack(got_small[p], small_shapes) for p in range(4)]
    for k, n in enumerate(SMALL_SHARDED):
        W[n] = jnp.concatenate([parts_small[p][k] for p in range(4)], axis=SHARD_AXIS[n])

    ex = Exchange(shards={n: A[n].astype(BF16) for n in BIG})
    loss, grad_x, G = local_step(A['x'][0], A['loss_target'][0], W, ex)
    loss = lax.psum(loss, ("x", "y", "c"))

    run_rider("scatter_mla", RiderGroup(ex.tail))
    ex.swap(name="grad_big_sibling")
    res_big = []
    for n in BIG:
        res = None
        for layer in range(A[n].shape[0]):
            res = adamw_shard(f"adam_{n}_{layer}", ex.received(n, layer)[:, None], ex.sibling(n, layer)[:, None],
                              A[n], A['m_' + n], A['v_' + n], first_layer=layer, into=res)
        res_big.append(res)

    small_names = REPLICATED + SMALL_SHARDED
    gs = {}
    for part, names, rider in (("early", SMALL_EARLY, ex.small[0]), ("late", SMALL_LATE, ex.small[1])):
        total = sum_slots("grad_small_sum_" + part, rider.results[0])
        gs.update(zip(names, _unpack(total, [G[n].shape for n in names])))
    for n in SMALL_SHARDED:
        ax = SHARD_AXIS[n]
        size = gs[n].shape[ax] // 4
        gs[n] = lax.dynamic_slice_in_dim(gs[n], q * size, size, axis=ax)
    pk = lambda pre: _pack([A[pre + n] for n in small_names], F32, 8)
    own_shapes = [A[n].shape for n in small_names]
    res_small = [_unpack(r, own_shapes) for r in
                 adamw("adam_small", [_pack([gs[n] for n in small_names], F32, 8)], pk(''), pk('m_'), pk('v_'))]

    out = {}
    for j, kind in enumerate(('grad_', 'delta_', 'new_m_', 'new_v_')):
        for k, n in enumerate(BIG):
            out[kind + n] = res_big[k][j]
        for k, n in enumerate(small_names):
            out[kind + n] = res_small[j][k]
    return (loss, grad_x[None]) + tuple(out[kind + n] for kind in ('grad_', 'delta_', 'new_m_', 'new_v_')
                                        for n in WEIGHTS)
```
